```python
import jax, jax.numpy as jnp
from jax import lax
import numpy as np

D_MODEL = 1024
BATCH = 8
SEQ = 8192
DEPTH = 1

D_CONV = D_MODEL
CONV_A_WIDTH = 3
EXPAND = 2
D_INNER = EXPAND * D_MODEL
HEAD_DIM = 64
N_HEADS = D_INNER // HEAD_DIM
N_GROUPS = 4
D_STATE = 128
SSD_CONV_WIDTH = 4
CHUNK = 128
D_XBC = D_INNER + 2 * N_GROUPS * D_STATE
D_FF = 2816
FFN_CONV_WIDTH = 3
N_IN = 2 * D_MODEL + 3 * D_CONV + D_INNER + D_XBC + N_HEADS
EPS = 1e-5
DT_MIN = 1e-3
DT_MAX = 1e-1

kernel_name = "hybrid_shortconv_ssd_gated_merge_convffn"


def rmsnorm(x, w):
    xf = x.astype(jnp.float32)
    y = xf * lax.rsqrt(jnp.mean(xf * xf, axis=-1, keepdims=True) + EPS)
    return (y * w.astype(jnp.float32)).astype(x.dtype)


def causal_dwconv(x, w):
    k, c = w.shape
    return lax.conv_general_dilated(
        x, w[:, None, :].astype(x.dtype), window_strides=(1,), padding=[(k - 1, 0)],
        dimension_numbers=("NWC", "WIO", "NWC"), feature_group_count=c)


def ssd_chunked_scan(xh, dt, a, bmat, cmat):
    b, s, h, p = xh.shape
    g, n = bmat.shape[-2:]
    j = h // g
    c = s // CHUNK
    x_ = (xh.astype(jnp.float32) * dt[..., None]).reshape(b, c, CHUNK, g, j, p)
    log_a = jnp.moveaxis((dt * a).reshape(b, c, CHUNK, g, j), 2, -1)
    bc = bmat.astype(jnp.float32).reshape(b, c, CHUNK, g, n)
    cc = cmat.astype(jnp.float32).reshape(b, c, CHUNK, g, n)
    a_cum = jnp.cumsum(log_a, axis=-1)

    causal = jnp.tril(jnp.ones((CHUNK, CHUNK), dtype=bool))
    seg = a_cum[..., :, None] - a_cum[..., None, :]
    decay_in = jnp.exp(jnp.where(causal, seg, -jnp.inf))
    cb = jnp.einsum("bclgn,bcsgn->bcgls", cc, bc)
    y_diag = jnp.einsum("bcgjls,bcsgjp->bclgjp", cb[:, :, :, None] * decay_in, x_)

    decay_to_end = jnp.exp(a_cum[..., -1:] - a_cum)
    states = jnp.einsum("bclgn,bcgjl,bclgjp->bcgjpn", bc, decay_to_end, x_)
    chunk_decay = jnp.exp(a_cum[..., -1])

    def step(carry, inp):
        st, dec = inp
        return carry * dec[..., None, None] + st, carry

    init = jnp.zeros((b, g, j, p, n), jnp.float32)
    _, prev = lax.scan(step, init, (jnp.moveaxis(states, 1, 0), jnp.moveaxis(chunk_decay, 1, 0)))
    prev = jnp.moveaxis(prev, 0, 1)
    y_off = jnp.einsum("bclgn,bcgjpn,bcgjl->bclgjp", cc, prev, jnp.exp(a_cum))
    return (y_diag + y_off).reshape(b, s, h, p)


def gated_group_rmsnorm(y, z, w):
    bsz, s, d = y.shape
    yf = (y.astype(jnp.float32) * jax.nn.silu(z.astype(jnp.float32))).reshape(bsz, s, N_GROUPS, d // N_GROUPS)
    yf = yf * lax.rsqrt(jnp.mean(yf * yf, axis=-1, keepdims=True) + EPS)
    return (yf.reshape(bsz, s, d) * w.astype(jnp.float32)).astype(y.dtype)


def hybrid_mixer(u, w_in, conv_a_w, w_a_out, ssd_conv_w, ssd_conv_b, dt_bias, a_log,
                 d_skip, ssd_norm_w, w_s_out, w_o):
    bsz, s, _ = u.shape
    proj = u @ w_in
    sizes = [D_MODEL, D_MODEL, D_CONV, D_CONV, D_CONV, D_INNER, D_XBC, N_HEADS]
    offsets = np.cumsum(sizes)[:-1].tolist()
    gate_a, gate_s, b_a, c_a, v_a, z, xbc, dt_raw = jnp.split(proj, offsets, axis=-1)

    y_a = (b_a * causal_dwconv(c_a * v_a, conv_a_w)) @ w_a_out

    xbc = jax.nn.silu(causal_dwconv(xbc, ssd_conv_w) + ssd_conv_b)
    xs, bs, cs = jnp.split(xbc, [D_INNER, D_INNER + N_GROUPS * D_STATE], axis=-1)
    xh = xs.reshape(bsz, s, N_HEADS, HEAD_DIM)
    dt = jax.nn.softplus(dt_raw.astype(jnp.float32) + dt_bias.astype(jnp.float32))
    a = -jnp.exp(a_log.astype(jnp.float32))
    y = ssd_chunked_scan(xh, dt, a,
                         bs.reshape(bsz, s, N_GROUPS, D_STATE),
                         cs.reshape(bsz, s, N_GROUPS, D_STATE))
    y = y + d_skip.astype(jnp.float32)[:, None] * xh.astype(jnp.float32)
    y = y.reshape(bsz, s, D_INNER).astype(u.dtype)
    y_s = gated_group_rmsnorm(y, z, ssd_norm_w) @ w_s_out

    merged = jax.nn.sigmoid(gate_a) * y_a + jax.nn.sigmoid(gate_s) * y_s
    return merged @ w_o


def conv_gated_mlp(v, w_up, ffn_conv_w, ffn_conv_b, w_down):
    hv = v @ w_up
    h1, h3 = jnp.split(hv, 2, axis=-1)
    h1 = causal_dwconv(h1, ffn_conv_w) + ffn_conv_b
    return (jax.nn.silu(h1) * h3) @ w_down


def _fwd_setup_inputs(seed: int = 0) -> dict:
    key = jax.random.key(seed)
    ks = jax.random.split(key, 24)
    f32 = jnp.float32

    def nrm(k, shape, scale):
        return jax.random.normal(k, shape, f32) * scale

    dt0 = jnp.exp(jax.random.uniform(ks[9], (DEPTH, N_HEADS), f32)
                  * (np.log(DT_MAX) - np.log(DT_MIN)) + np.log(DT_MIN))
    dt_bias = dt0 + jnp.log(-jnp.expm1(-dt0))
    return {
        "x": nrm(ks[0], (BATCH, SEQ, D_MODEL), 1.0),
        "norm_mix_w": 1.0 + nrm(ks[1], (DEPTH, D_MODEL), 0.02),
        "w_in": nrm(ks[2], (DEPTH, D_MODEL, N_IN), D_MODEL ** -0.5),
        "conv_a_w": nrm(ks[3], (DEPTH, CONV_A_WIDTH, D_CONV), CONV_A_WIDTH ** -0.5),
        "w_a_out": nrm(ks[4], (DEPTH, D_CONV, D_MODEL), D_CONV ** -0.5),
        "ssd_conv_w": nrm(ks[5], (DEPTH, SSD_CONV_WIDTH, D_XBC), SSD_CONV_WIDTH ** -0.5),
        "ssd_conv_b": nrm(ks[6], (DEPTH, D_XBC), 0.02),
        "dt_bias": dt_bias,
        "a_log": jnp.log(jax.random.uniform(ks[7], (DEPTH, N_HEADS), f32, 1.0, 16.0)),
        "d_skip": 1.0 + nrm(ks[8], (DEPTH, N_HEADS), 0.02),
        "ssd_norm_w": 1.0 + nrm(ks[10], (DEPTH, D_INNER), 0.02),
        "w_s_out": nrm(ks[11], (DEPTH, D_INNER, D_MODEL), D_INNER ** -0.5),
        "w_o": nrm(ks[12], (DEPTH, D_MODEL, D_MODEL), D_MODEL ** -0.5),
        "norm_ffn_w": 1.0 + nrm(ks[13], (DEPTH, D_MODEL), 0.02),
        "w_up": nrm(ks[14], (DEPTH, D_MODEL, 2 * D_FF), D_MODEL ** -0.5),
        "ffn_conv_w": nrm(ks[15], (DEPTH, FFN_CONV_WIDTH, D_FF), FFN_CONV_WIDTH ** -0.5),
        "ffn_conv_b": nrm(ks[16], (DEPTH, D_FF), 0.02),
        "w_down": nrm(ks[17], (DEPTH, D_FF, D_MODEL), D_FF ** -0.5),
        "final_norm_w": 1.0 + nrm(ks[18], (D_MODEL,), 0.02),
    }


def _fwd_reference(x, norm_mix_w, w_in, conv_a_w, w_a_out, ssd_conv_w, ssd_conv_b, dt_bias,
              a_log, d_skip, ssd_norm_w, w_s_out, w_o, norm_ffn_w, w_up, ffn_conv_w,
              ffn_conv_b, w_down, final_norm_w):
    h = x
    for l in range(DEPTH):
        u = rmsnorm(h, norm_mix_w[l])
        h = h + hybrid_mixer(u, w_in[l], conv_a_w[l], w_a_out[l], ssd_conv_w[l], ssd_conv_b[l],
                             dt_bias[l], a_log[l], d_skip[l], ssd_norm_w[l], w_s_out[l], w_o[l])
        v = rmsnorm(h, norm_ffn_w[l])
        h = h + conv_gated_mlp(v, w_up[l], ffn_conv_w[l], ffn_conv_b[l], w_down[l])
    return rmsnorm(h, final_norm_w)


import jax as _jax
import jax.numpy as _jnp

TWIN_FORMAT = 'train_step'
FWD_PARAMS = ['x', 'norm_mix_w', 'w_in', 'conv_a_w', 'w_a_out', 'ssd_conv_w', 'ssd_conv_b', 'dt_bias', 'a_log', 'd_skip', 'ssd_norm_w', 'w_s_out', 'w_o', 'norm_ffn_w', 'w_up', 'ffn_conv_w', 'ffn_conv_b', 'w_down', 'final_norm_w']
TWIN_WEIGHTS = ['norm_mix_w', 'w_in', 'conv_a_w', 'w_a_out', 'ssd_conv_w', 'ssd_conv_b', 'dt_bias', 'a_log', 'd_skip', 'ssd_norm_w', 'w_s_out', 'w_o', 'norm_ffn_w', 'w_up', 'ffn_conv_w', 'ffn_conv_b', 'w_down', 'final_norm_w']
TWIN_DIFF_INPUT = 'x'
TWIN_INPUTS = ['x', 'norm_mix_w', 'w_in', 'conv_a_w', 'w_a_out', 'ssd_conv_w', 'ssd_conv_b', 'dt_bias', 'a_log', 'd_skip', 'ssd_norm_w', 'w_s_out', 'w_o', 'norm_ffn_w', 'w_up', 'ffn_conv_w', 'ffn_conv_b', 'w_down', 'final_norm_w', 'loss_target', 'm_norm_mix_w', 'm_w_in', 'm_conv_a_w', 'm_w_a_out', 'm_ssd_conv_w', 'm_ssd_conv_b', 'm_dt_bias', 'm_a_log', 'm_d_skip', 'm_ssd_norm_w', 'm_w_s_out', 'm_w_o', 'm_norm_ffn_w', 'm_w_up', 'm_ffn_conv_w', 'm_ffn_conv_b', 'm_w_down', 'm_final_norm_w', 'v_norm_mix_w', 'v_w_in', 'v_conv_a_w', 'v_w_a_out', 'v_ssd_conv_w', 'v_ssd_conv_b', 'v_dt_bias', 'v_a_log', 'v_d_skip', 'v_ssd_norm_w', 'v_w_s_out', 'v_w_o', 'v_norm_ffn_w', 'v_w_up', 'v_ffn_conv_w', 'v_ffn_conv_b', 'v_w_down', 'v_final_norm_w']
TWIN_OUTPUTS = ['loss', 'grad_x', 'grad_norm_mix_w', 'grad_w_in', 'grad_conv_a_w', 'grad_w_a_out', 'grad_ssd_conv_w', 'grad_ssd_conv_b', 'grad_dt_bias', 'grad_a_log', 'grad_d_skip', 'grad_ssd_norm_w', 'grad_w_s_out', 'grad_w_o', 'grad_norm_ffn_w', 'grad_w_up', 'grad_ffn_conv_w', 'grad_ffn_conv_b', 'grad_w_down', 'grad_final_norm_w', 'delta_norm_mix_w', 'delta_w_in', 'delta_conv_a_w', 'delta_w_a_out', 'delta_ssd_conv_w', 'delta_ssd_conv_b', 'delta_dt_bias', 'delta_a_log', 'delta_d_skip', 'delta_ssd_norm_w', 'delta_w_s_out', 'delta_w_o', 'delta_norm_ffn_w', 'delta_w_up', 'delta_ffn_conv_w', 'delta_ffn_conv_b', 'delta_w_down', 'delta_final_norm_w', 'new_m_norm_mix_w', 'new_m_w_in', 'new_m_conv_a_w', 'new_m_w_a_out', 'new_m_ssd_conv_w', 'new_m_ssd_conv_b', 'new_m_dt_bias', 'new_m_a_log', 'new_m_d_skip', 'new_m_ssd_norm_w', 'new_m_w_s_out', 'new_m_w_o', 'new_m_norm_ffn_w', 'new_m_w_up', 'new_m_ffn_conv_w', 'new_m_ffn_conv_b', 'new_m_w_down', 'new_m_final_norm_w', 'new_v_norm_mix_w', 'new_v_w_in', 'new_v_conv_a_w', 'new_v_w_a_out', 'new_v_ssd_conv_w', 'new_v_ssd_conv_b', 'new_v_dt_bias', 'new_v_a_log', 'new_v_d_skip', 'new_v_ssd_norm_w', 'new_v_w_s_out', 'new_v_w_o', 'new_v_norm_ffn_w', 'new_v_w_up', 'new_v_ffn_conv_w', 'new_v_ffn_conv_b', 'new_v_w_down', 'new_v_final_norm_w']
TWIN_LEAF_KINDS = {'loss': 'loss', 'grad_x': 'grad_x', 'grad_norm_mix_w': 'grad_w', 'grad_w_in': 'grad_w', 'grad_conv_a_w': 'grad_w', 'grad_w_a_out': 'grad_w', 'grad_ssd_conv_w': 'grad_w', 'grad_ssd_conv_b': 'grad_w', 'grad_dt_bias': 'grad_w', 'grad_a_log': 'grad_w', 'grad_d_skip': 'grad_w', 'grad_ssd_norm_w': 'grad_w', 'grad_w_s_out': 'grad_w', 'grad_w_o': 'grad_w', 'grad_norm_ffn_w': 'grad_w', 'grad_w_up': 'grad_w', 'grad_ffn_conv_w': 'grad_w', 'grad_ffn_conv_b': 'grad_w', 'grad_w_down': 'grad_w', 'grad_final_norm_w': 'grad_w', 'delta_norm_mix_w': 'delta_w', 'delta_w_in': 'delta_w', 'delta_conv_a_w': 'delta_w', 'delta_w_a_out': 'delta_w', 'delta_ssd_conv_w': 'delta_w', 'delta_ssd_conv_b': 'delta_w', 'delta_dt_bias': 'delta_w', 'delta_a_log': 'delta_w', 'delta_d_skip': 'delta_w', 'delta_ssd_norm_w': 'delta_w', 'delta_w_s_out': 'delta_w', 'delta_w_o': 'delta_w', 'delta_norm_ffn_w': 'delta_w', 'delta_w_up': 'delta_w', 'delta_ffn_conv_w': 'delta_w', 'delta_ffn_conv_b': 'delta_w', 'delta_w_down': 'delta_w', 'delta_final_norm_w': 'delta_w', 'new_m_norm_mix_w': 'new_m', 'new_m_w_in': 'new_m', 'new_m_conv_a_w': 'new_m', 'new_m_w_a_out': 'new_m', 'new_m_ssd_conv_w': 'new_m', 'new_m_ssd_conv_b': 'new_m', 'new_m_dt_bias': 'new_m', 'new_m_a_log': 'new_m', 'new_m_d_skip': 'new_m', 'new_m_ssd_norm_w': 'new_m', 'new_m_w_s_out': 'new_m', 'new_m_w_o': 'new_m', 'new_m_norm_ffn_w': 'new_m', 'new_m_w_up': 'new_m', 'new_m_ffn_conv_w': 'new_m', 'new_m_ffn_conv_b': 'new_m', 'new_m_w_down': 'new_m', 'new_m_final_norm_w': 'new_m', 'new_v_norm_mix_w': 'new_v', 'new_v_w_in': 'new_v', 'new_v_conv_a_w': 'new_v', 'new_v_w_a_out': 'new_v', 'new_v_ssd_conv_w': 'new_v', 'new_v_ssd_conv_b': 'new_v', 'new_v_dt_bias': 'new_v', 'new_v_a_log': 'new_v', 'new_v_d_skip': 'new_v', 'new_v_ssd_norm_w': 'new_v', 'new_v_w_s_out': 'new_v', 'new_v_w_o': 'new_v', 'new_v_norm_ffn_w': 'new_v', 'new_v_w_up': 'new_v', 'new_v_ffn_conv_w': 'new_v', 'new_v_ffn_conv_b': 'new_v', 'new_v_w_down': 'new_v', 'new_v_final_norm_w': 'new_v'}


def _forward(args):
    return _fwd_reference(*[args[k] for k in FWD_PARAMS])


def _output_shape():
    def fwd():
        inp = _fwd_setup_inputs(0)
        return _fwd_reference(*[inp[k] for k in FWD_PARAMS])
    out = _jax.eval_shape(fwd)
    return out.shape, out.dtype

N_MICROBATCH = 1
ADAM_LR = 0.001
ADAM_B1 = 0.9
ADAM_B2 = 0.999
ADAM_EPS = 1e-08
ADAM_WD = 0.01
ADAM_STEP = 10
PER_EXAMPLE_BATCH_AXIS = {'x': 0, 'loss_target': 0}
SHARED_INPUTS = []
_WEIGHT_DTYPES = {'norm_mix_w': _jnp.float32, 'w_in': _jnp.float32, 'conv_a_w': _jnp.float32, 'w_a_out': _jnp.float32, 'ssd_conv_w': _jnp.float32, 'ssd_conv_b': _jnp.float32, 'dt_bias': _jnp.float32, 'a_log': _jnp.float32, 'd_skip': _jnp.float32, 'ssd_norm_w': _jnp.float32, 'w_s_out': _jnp.float32, 'w_o': _jnp.float32, 'norm_ffn_w': _jnp.float32, 'w_up': _jnp.float32, 'ffn_conv_w': _jnp.float32, 'ffn_conv_b': _jnp.float32, 'w_down': _jnp.float32, 'final_norm_w': _jnp.float32}
MOMENT_SCALE = {'norm_mix_w': 2.864175e-01, 'w_in': 8.745362e-02, 'conv_a_w': 1.195617e-01, 'w_a_out': 1.169660e-01, 'ssd_conv_w': 7.161084e-02, 'ssd_conv_b': 9.990487e-02, 'dt_bias': 1.296290e-01, 'a_log': 2.974728e-01, 'd_skip': 6.513709e-01, 'ssd_norm_w': 8.905948e-02, 'w_s_out': 1.199271e-01, 'w_o': 1.685594e-01, 'norm_ffn_w': 1.574583e-01, 'w_up': 6.821549e-02, 'ffn_conv_w': 6.903354e-02, 'ffn_conv_b': 6.558998e-02, 'w_down': 1.116693e-01, 'final_norm_w': 6.400726e+01}


def _to_microbatches(a, axis):
    t = _jnp.moveaxis(a, axis, 0)
    t = t.reshape((N_MICROBATCH, t.shape[0] // N_MICROBATCH) + t.shape[1:])
    return _jnp.moveaxis(t, 1, axis + 1)


def setup_inputs(seed: int = 0) -> dict:
    inp = _fwd_setup_inputs(seed)
    key = _jax.random.fold_in(_jax.random.key(seed), 7919)
    shape, _ = _output_shape()
    out = dict(inp)
    out["loss_target"] = _jax.random.normal(_jax.random.fold_in(key, 0), shape, _jnp.float32)
    for i, name in enumerate(TWIN_WEIGHTS):
        w = inp[name].astype(_jnp.float32)
        if MOMENT_SCALE is None:
            s = _jnp.sqrt(_jnp.mean(_jnp.square(w)) + 1e-30)
        else:
            s = MOMENT_SCALE[name]
        km, kv = _jax.random.split(_jax.random.fold_in(key, i + 1))
        out[name] = w
        out["m_" + name] = s * _jax.random.normal(km, w.shape, _jnp.float32)
        out["v_" + name] = (s * s) * _jax.random.uniform(kv, w.shape, _jnp.float32, 0.5, 1.5)
    if N_MICROBATCH > 1:
        for name, axis in PER_EXAMPLE_BATCH_AXIS.items():
            out[name] = _to_microbatches(out[name], axis)
    return {'x': out['x'], 'norm_mix_w': out['norm_mix_w'], 'w_in': out['w_in'], 'conv_a_w': out['conv_a_w'], 'w_a_out': out['w_a_out'], 'ssd_conv_w': out['ssd_conv_w'], 'ssd_conv_b': out['ssd_conv_b'], 'dt_bias': out['dt_bias'], 'a_log': out['a_log'], 'd_skip': out['d_skip'], 'ssd_norm_w': out['ssd_norm_w'], 'w_s_out': out['w_s_out'], 'w_o': out['w_o'], 'norm_ffn_w': out['norm_ffn_w'], 'w_up': out['w_up'], 'ffn_conv_w': out['ffn_conv_w'], 'ffn_conv_b': out['ffn_conv_b'], 'w_down': out['w_down'], 'final_norm_w': out['final_norm_w'], 'loss_target': out['loss_target'], 'm_norm_mix_w': out['m_norm_mix_w'], 'm_w_in': out['m_w_in'], 'm_conv_a_w': out['m_conv_a_w'], 'm_w_a_out': out['m_w_a_out'], 'm_ssd_conv_w': out['m_ssd_conv_w'], 'm_ssd_conv_b': out['m_ssd_conv_b'], 'm_dt_bias': out['m_dt_bias'], 'm_a_log': out['m_a_log'], 'm_d_skip': out['m_d_skip'], 'm_ssd_norm_w': out['m_ssd_norm_w'], 'm_w_s_out': out['m_w_s_out'], 'm_w_o': out['m_w_o'], 'm_norm_ffn_w': out['m_norm_ffn_w'], 'm_w_up': out['m_w_up'], 'm_ffn_conv_w': out['m_ffn_conv_w'], 'm_ffn_conv_b': out['m_ffn_conv_b'], 'm_w_down': out['m_w_down'], 'm_final_norm_w': out['m_final_norm_w'], 'v_norm_mix_w': out['v_norm_mix_w'], 'v_w_in': out['v_w_in'], 'v_conv_a_w': out['v_conv_a_w'], 'v_w_a_out': out['v_w_a_out'], 'v_ssd_conv_w': out['v_ssd_conv_w'], 'v_ssd_conv_b': out['v_ssd_conv_b'], 'v_dt_bias': out['v_dt_bias'], 'v_a_log': out['v_a_log'], 'v_d_skip': out['v_d_skip'], 'v_ssd_norm_w': out['v_ssd_norm_w'], 'v_w_s_out': out['v_w_s_out'], 'v_w_o': out['v_w_o'], 'v_norm_ffn_w': out['v_norm_ffn_w'], 'v_w_up': out['v_w_up'], 'v_ffn_conv_w': out['v_ffn_conv_w'], 'v_ffn_conv_b': out['v_ffn_conv_b'], 'v_w_down': out['v_w_down'], 'v_final_norm_w': out['v_final_norm_w']}


def _loss(weights, diff, rest, loss_target):
    with _jax.named_scope("forward"):
        args = {**rest, TWIN_DIFF_INPUT: diff, **{k: w.astype(_WEIGHT_DTYPES[k]) for k, w in weights.items()}}
        y = _forward(args)
    with _jax.named_scope("loss_head"):
        err = _jnp.square(y.astype(_jnp.float32) - loss_target)
        return 0.5 * _jnp.sum(_jnp.mean(err, axis=-1)) if err.ndim else 0.5 * err


def _adamw(w, g, m, v):
    m = ADAM_B1 * m + (1.0 - ADAM_B1) * g
    v = ADAM_B2 * v + (1.0 - ADAM_B2) * _jnp.square(g)
    m_hat = m / (1.0 - ADAM_B1 ** ADAM_STEP)
    v_hat = v / (1.0 - ADAM_B2 ** ADAM_STEP)
    delta = -ADAM_LR * (m_hat / (_jnp.sqrt(v_hat) + ADAM_EPS) + ADAM_WD * w)
    return delta, m, v


def reference(x, norm_mix_w, w_in, conv_a_w, w_a_out, ssd_conv_w, ssd_conv_b, dt_bias, a_log, d_skip, ssd_norm_w, w_s_out, w_o, norm_ffn_w, w_up, ffn_conv_w, ffn_conv_b, w_down, final_norm_w, loss_target, m_norm_mix_w, m_w_in, m_conv_a_w, m_w_a_out, m_ssd_conv_w, m_ssd_conv_b, m_dt_bias, m_a_log, m_d_skip, m_ssd_norm_w, m_w_s_out, m_w_o, m_norm_ffn_w, m_w_up, m_ffn_conv_w, m_ffn_conv_b, m_w_down, m_final_norm_w, v_norm_mix_w, v_w_in, v_conv_a_w, v_w_a_out, v_ssd_conv_w, v_ssd_conv_b, v_dt_bias, v_a_log, v_d_skip, v_ssd_norm_w, v_w_s_out, v_w_o, v_norm_ffn_w, v_w_up, v_ffn_conv_w, v_ffn_conv_b, v_w_down, v_final_norm_w):
    given = dict(x=x, norm_mix_w=norm_mix_w, w_in=w_in, conv_a_w=conv_a_w, w_a_out=w_a_out, ssd_conv_w=ssd_conv_w, ssd_conv_b=ssd_conv_b, dt_bias=dt_bias, a_log=a_log, d_skip=d_skip, ssd_norm_w=ssd_norm_w, w_s_out=w_s_out, w_o=w_o, norm_ffn_w=norm_ffn_w, w_up=w_up, ffn_conv_w=ffn_conv_w, ffn_conv_b=ffn_conv_b, w_down=w_down, final_norm_w=final_norm_w, loss_target=loss_target, m_norm_mix_w=m_norm_mix_w, m_w_in=m_w_in, m_conv_a_w=m_conv_a_w, m_w_a_out=m_w_a_out, m_ssd_conv_w=m_ssd_conv_w, m_ssd_conv_b=m_ssd_conv_b, m_dt_bias=m_dt_bias, m_a_log=m_a_log, m_d_skip=m_d_skip, m_ssd_norm_w=m_ssd_norm_w, m_w_s_out=m_w_s_out, m_w_o=m_w_o, m_norm_ffn_w=m_norm_ffn_w, m_w_up=m_w_up, m_ffn_conv_w=m_ffn_conv_w, m_ffn_conv_b=m_ffn_conv_b, m_w_down=m_w_down, m_final_norm_w=m_final_norm_w, v_norm_mix_w=v_norm_mix_w, v_w_in=v_w_in, v_conv_a_w=v_conv_a_w, v_w_a_out=v_w_a_out, v_ssd_conv_w=v_ssd_conv_w, v_ssd_conv_b=v_ssd_conv_b, v_dt_bias=v_dt_bias, v_a_log=v_a_log, v_d_skip=v_d_skip, v_ssd_norm_w=v_ssd_norm_w, v_w_s_out=v_w_s_out, v_w_o=v_w_o, v_norm_ffn_w=v_norm_ffn_w, v_w_up=v_w_up, v_ffn_conv_w=v_ffn_conv_w, v_ffn_conv_b=v_ffn_conv_b, v_w_down=v_w_down, v_final_norm_w=v_final_norm_w)
    weights = {n: given[n] for n in TWIN_WEIGHTS}
    shared = {n: given[n] for n in SHARED_INPUTS}
    per_example = {n: given[n] for n in ['x']}
    grad_fn = _jax.value_and_grad(_loss, argnums=(0, 1))

    def one_microbatch(ex, loss_target):
        ex = dict(ex)
        diff = ex.pop(TWIN_DIFF_INPUT)
        return grad_fn(weights, diff, {**shared, **ex}, loss_target)

    if N_MICROBATCH == 1:
        loss, (grad_w, grad_x) = one_microbatch(per_example, given["loss_target"])
    else:
        def body(carry, xs):
            loss_sum, grad_sum = carry
            l_k, (gw_k, gx_k) = one_microbatch(xs[0], xs[1])
            with _jax.named_scope("update"):
                return (loss_sum + l_k, _jax.tree.map(_jnp.add, grad_sum, gw_k)), gx_k

        init = (_jnp.zeros((), _jnp.float32), _jax.tree.map(_jnp.zeros_like, weights))
        (loss, grad_w), grad_x = _jax.lax.scan(body, init, (per_example, given["loss_target"]))
    with _jax.named_scope("update"):
        delta_w, new_m, new_v = {}, {}, {}
        for n in TWIN_WEIGHTS:
            delta_w[n], new_m[n], new_v[n] = _adamw(weights[n], grad_w[n], given["m_" + n], given["v_" + n])
    return (loss, grad_x, *[grad_w[n] for n in TWIN_WEIGHTS], *[delta_w[n] for n in TWIN_WEIGHTS],
            *[new_m[n] for n in TWIN_WEIGHTS], *[new_v[n] for n in TWIN_WEIGHTS])
```

```python
import functools
import math

import jax
import jax.numpy as jnp
from jax import lax
from jax.experimental import pallas as pl
from jax.experimental.pallas import tpu as pltpu

F32 = jnp.float32
BF16 = jnp.bfloat16
EPS = 1e-5
HEAD_DIM = 64
N_GROUPS = 4
D_STATE = 128
CHUNK = 128
DT_LANES = 128
HALO = 16
N_DEV = 8
V7X_VMEM_LIMIT = 56 * 1024 * 1024
ADAM_LR, ADAM_B1, ADAM_B2, ADAM_EPS, ADAM_WD, ADAM_STEP = 0.001, 0.9, 0.999, 1e-08, 0.01, 10
HIGHEST = lax.Precision.HIGHEST
MESH = pl.DeviceIdType.MESH


def _pc(body, **kw):
    return pl.pallas_call(body, **kw)


def _params():
    return pltpu.CompilerParams(vmem_limit_bytes=V7X_VMEM_LIMIT)


def _pick(n, cands):
    for c in cands:
        if n % c == 0:
            return c
    return n


def _dot(a, b, ca, cb, prec=None):
    return lax.dot_general(a, b, (((ca,), (cb,)), ((), ())), preferred_element_type=F32, precision=prec)


def _sigmoid(x):
    return 1.0 / (1.0 + jnp.exp(-x))


def _sds(shape, dtype):
    return jax.ShapeDtypeStruct(shape, dtype)


def _mm(pairs, mode, out_dtype, name, tm=512, tn=512):
    m = pairs[0][0].shape[0]
    n = pairs[0][1].shape[1] if mode == "nn" else pairs[0][1].shape[0]
    tm = min(tm, m)
    tn = _pick(n, (tn, 256, 128))
    npair = len(pairs)
    cb = 0 if mode == "nn" else 1

    def body(*refs):
        o_ref = refs[2 * npair]
        acc = None
        for p in range(npair):
            part = _dot(refs[2 * p][...], refs[2 * p + 1][...], 1, cb)
            acc = part if acc is None else acc + part
        o_ref[...] = acc.astype(o_ref.dtype)

    in_specs, args = [], []
    for a, b in pairs:
        k = a.shape[1]
        in_specs.append(pl.BlockSpec((tm, k), lambda i, j: (i, 0)))
        if mode == "nn":
            in_specs.append(pl.BlockSpec((k, tn), lambda i, j: (0, j)))
        else:
            in_specs.append(pl.BlockSpec((tn, k), lambda i, j: (j, 0)))
        args += [a, b]
    return _pc(body, name=name, grid=(m // tm, n // tn), in_specs=in_specs,
               out_specs=pl.BlockSpec((tm, tn), lambda i, j: (i, j)),
               out_shape=_sds((m, n), out_dtype), compiler_params=_params())(*args)


def _mm_tn(a, b, name, tm=1024):
    m, ka = a.shape
    nb = b.shape[1]
    tm = min(tm, m)
    tk = _pick(ka, (1024, 1408, 512, 256, 128))
    tn = _pick(nb, (1024, 512, 256, 128))

    def body(a_ref, b_ref, o_ref):
        @pl.when(pl.program_id(2) == 0)
        def _():
            o_ref[...] = jnp.zeros_like(o_ref)
        o_ref[...] += _dot(a_ref[...], b_ref[...], 0, 0)

    return _pc(body, name=name, grid=(ka // tk, nb // tn, m // tm),
               in_specs=[pl.BlockSpec((tm, tk), lambda i, j, t: (t, i)),
                         pl.BlockSpec((tm, tn), lambda i, j, t: (t, j))],
               out_specs=pl.BlockSpec((tk, tn), lambda i, j, t: (i, j)),
               out_shape=_sds((ka, nb), F32), compiler_params=_params())(a, b)


def _rms_fwd(x, w, name):
    t, d = x.shape
    tm = min(256, t)

    def body(x_ref, w_ref, o_ref):
        xv = x_ref[...]
        r = lax.rsqrt(jnp.mean(xv * xv, axis=-1, keepdims=True) + EPS)
        o_ref[...] = (xv * r * w_ref[...]).astype(o_ref.dtype)

    return _pc(body, name=name, grid=(t // tm,),
               in_specs=[pl.BlockSpec((tm, d), lambda i: (i, 0)), pl.BlockSpec((1, d), lambda i: (0, 0))],
               out_specs=pl.BlockSpec((tm, d), lambda i: (i, 0)),
               out_shape=_sds((t, d), BF16), compiler_params=_params())(x, w)


def _resnorm_fwd(x, mo, w, name):
    t, d = x.shape
    tm = min(256, t)

    def body(x_ref, mo_ref, w_ref, h_ref, v_ref):
        h = x_ref[...] + mo_ref[...]
        r = lax.rsqrt(jnp.mean(h * h, axis=-1, keepdims=True) + EPS)
        h_ref[...] = h
        v_ref[...] = (h * r * w_ref[...]).astype(v_ref.dtype)

    row = pl.BlockSpec((tm, d), lambda i: (i, 0))
    return _pc(body, name=name, grid=(t // tm,),
               in_specs=[row, row, pl.BlockSpec((1, d), lambda i: (0, 0))],
               out_specs=[row, row], out_shape=[_sds((t, d), F32), _sds((t, d), BF16)],
               compiler_params=_params())(x, mo, w)


def _rms_bwd(h, dy, w, dres, name):
    t, d = h.shape
    tm = min(256, t)

    def body(h_ref, dy_ref, w_ref, dres_ref, dx_ref, dxb_ref, dw_ref):
        @pl.when(pl.program_id(0) == 0)
        def _():
            dw_ref[...] = jnp.zeros_like(dw_ref)
        hv = h_ref[...]
        dyv = dy_ref[...]
        r = lax.rsqrt(jnp.mean(hv * hv, axis=-1, keepdims=True) + EPS)
        n = hv * r
        dn = dyv * w_ref[...]
        dw_ref[0:1, :] += jnp.sum(dyv * n, axis=0, keepdims=True)
        dx = dres_ref[...] + r * (dn - n * jnp.mean(dn * n, axis=-1, keepdims=True))
        dx_ref[...] = dx
        dxb_ref[...] = dx.astype(BF16)

    row = pl.BlockSpec((tm, d), lambda i: (i, 0))
    return _pc(body, name=name, grid=(t // tm,),
               in_specs=[row, row, pl.BlockSpec((1, d), lambda i: (0, 0)), row],
               out_specs=[row, row, pl.BlockSpec((8, d), lambda i: (0, 0))],
               out_shape=[_sds((t, d), F32), _sds((t, d), BF16), _sds((8, d), F32)],
               compiler_params=_params())(h, dy, w, dres)


def _final(h1, dd, tgt, w, name):
    t, d = h1.shape
    tm = min(256, t)
    nt = t // tm

    def body(h1_ref, dd_ref, tgt_ref, w_ref, loss_ref, dh_ref, dhb_ref, dw_ref, acc):
        i = pl.program_id(0)

        @pl.when(i == 0)
        def _():
            dw_ref[...] = jnp.zeros_like(dw_ref)
            acc[...] = jnp.zeros_like(acc)
        h = h1_ref[...] + dd_ref[...]
        r = lax.rsqrt(jnp.mean(h * h, axis=-1, keepdims=True) + EPS)
        n = h * r
        wv = w_ref[...]
        e = n * wv - tgt_ref[...]
        acc[...] += jnp.sum(e * e, axis=0, keepdims=True)
        dout = e * (1.0 / d)
        dn = dout * wv
        dw_ref[0:1, :] += jnp.sum(dout * n, axis=0, keepdims=True)
        dh = r * (dn - n * jnp.mean(dn * n, axis=-1, keepdims=True))
        dh_ref[...] = dh
        dhb_ref[...] = dh.astype(BF16)

        @pl.when(i == nt - 1)
        def _():
            loss_ref[...] = jnp.sum(acc[...], axis=-1, keepdims=True) * (0.5 / d)

    row = pl.BlockSpec((tm, d), lambda i: (i, 0))
    return _pc(body, name=name, grid=(nt,),
               in_specs=[row, row, row, pl.BlockSpec((1, d), lambda i: (0, 0))],
               out_specs=[pl.BlockSpec((1, 1), lambda i: (0, 0)), row, row, pl.BlockSpec((8, d), lambda i: (0, 0))],
               out_shape=[_sds((1, 1), F32), _sds((t, d), F32), _sds((t, d), BF16), _sds((8, d), F32)],
               scratch_shapes=[pltpu.VMEM((1, d), F32)], compiler_params=_params())(h1, dd, tgt, w)


def _fill(buf, prev, cur, nxt, tm):
    if prev is not None:
        buf[0:HALO, :] = prev
    buf[HALO:HALO + tm, :] = cur
    if nxt is not None:
        buf[HALO + tm:HALO + tm + HALO, :] = nxt


def _conv_rows(xbuf, w, k, rows):
    out = None
    for j in range(k):
        term = w[j:j + 1, :] * xbuf[pl.ds(HALO - (k - 1) + j, rows), :]
        out = term if out is None else out + term
    return out


def _conv_transpose(dbuf, w, k, tm):
    out = None
    for j in range(k):
        term = w[j:j + 1, :] * dbuf[pl.ds(k - 1 - j, tm), :]
        out = term if out is None else out + term
    return out


def _acc_conv_grads(acc_ref, dcur, xbuf, k, tm, with_bias):
    for j in range(k):
        acc_ref[j:j + 1, :] += jnp.sum(dcur * xbuf[pl.ds(HALO - (k - 1) + j, tm), :], axis=0, keepdims=True)
    if with_bias:
        acc_ref[k:k + 1, :] += jnp.sum(dcur, axis=0, keepdims=True)


def _tile_specs(t, tm, tc, col0):
    th = tm // HALO
    last = t // HALO - 1
    cur = pl.BlockSpec((tm, tc), lambda j, i: (i, col0 + j))
    prev = pl.BlockSpec((HALO, tc), lambda j, i: (jnp.maximum(i * th - 1, 0), col0 + j))
    nxt = pl.BlockSpec((HALO, tc), lambda j, i: (jnp.minimum((i + 1) * th, last), col0 + j))
    return cur, prev, nxt


def _conv_a_fwd(pa, w, d, name):
    t = pa.shape[0]
    tm, tc = min(256, t), _pick(d, (512, 256, 128))
    nd = d // tc

    def body(b_ref, c_ref, v_ref, cp_ref, vp_ref, w_ref, o_ref, buf):
        i = pl.program_id(1)
        keep = (i > 0).astype(F32)
        prev = cp_ref[...].astype(F32) * vp_ref[...].astype(F32) * keep
        _fill(buf, prev, c_ref[...].astype(F32) * v_ref[...].astype(F32), None, tm)
        q = _conv_rows(buf, w_ref[...], 3, tm)
        o_ref[...] = (b_ref[...].astype(F32) * q).astype(o_ref.dtype)

    b_cur, _, _ = _tile_specs(t, tm, tc, 0)
    c_cur, c_prev, _ = _tile_specs(t, tm, tc, nd)
    v_cur, v_prev, _ = _tile_specs(t, tm, tc, 2 * nd)
    return _pc(body, name=name, grid=(nd, t // tm),
               in_specs=[b_cur, c_cur, v_cur, c_prev, v_prev, pl.BlockSpec((3, tc), lambda j, i: (0, j))],
               out_specs=pl.BlockSpec((tm, tc), lambda j, i: (i, j)),
               out_shape=_sds((t, d), BF16),
               scratch_shapes=[pltpu.VMEM((tm + 2 * HALO, tc), F32)],
               compiler_params=_params())(pa, pa, pa, pa, pa, w)


def _conv_a_bwd(pa, dya, w, d, name):
    t = pa.shape[0]
    tm, tc = min(256, t), _pick(d, (512, 256, 128))
    nd, nt = d // tc, t // tm

    def body(b_ref, c_ref, v_ref, cp_ref, vp_ref, bn_ref, g_ref, gn_ref, w_ref,
             db_ref, dc_ref, dv_ref, acc_ref, pbuf, dbuf):
        i = pl.program_id(1)

        @pl.when(i == 0)
        def _():
            acc_ref[...] = jnp.zeros_like(acc_ref)
        wv = w_ref[...]
        cv, vv = c_ref[...].astype(F32), v_ref[...].astype(F32)
        prev = cp_ref[...].astype(F32) * vp_ref[...].astype(F32) * (i > 0).astype(F32)
        _fill(pbuf, prev, cv * vv, None, tm)
        g = g_ref[...].astype(F32)
        dq = g * b_ref[...].astype(F32)
        dqn = gn_ref[...].astype(F32) * bn_ref[...].astype(F32) * (i < nt - 1).astype(F32)
        dbuf[0:tm, :] = dq
        dbuf[tm:tm + HALO, :] = dqn
        q = _conv_rows(pbuf, wv, 3, tm)
        db_ref[...] = (g * q).astype(BF16)
        dp = _conv_transpose(dbuf, wv, 3, tm)
        dc_ref[...] = (dp * vv).astype(BF16)
        dv_ref[...] = (dp * cv).astype(BF16)
        _acc_conv_grads(acc_ref, dq, pbuf, 3, tm, False)

    b_cur, _, b_next = _tile_specs(t, tm, tc, 0)
    c_cur, c_prev, _ = _tile_specs(t, tm, tc, nd)
    v_cur, v_prev, _ = _tile_specs(t, tm, tc, 2 * nd)
    g_cur, _, g_next = _tile_specs(t, tm, tc, 0)
    out = pl.BlockSpec((tm, tc), lambda j, i: (i, j))
    return _pc(body, name=name, grid=(nd, nt),
               in_specs=[b_cur, c_cur, v_cur, c_prev, v_prev, b_next, g_cur, g_next,
                         pl.BlockSpec((3, tc), lambda j, i: (0, j))],
               out_specs=[out, out, out, pl.BlockSpec((8, tc), lambda j, i: (0, j))],
               out_shape=[_sds((t, d), BF16)] * 3 + [_sds((8, d), F32)],
               scratch_shapes=[pltpu.VMEM((tm + 2 * HALO, tc), F32), pltpu.VMEM((tm + HALO, tc), F32)],
               compiler_params=_params())(pa, pa, pa, pa, pa, pa, dya, dya, w)


def _conv_s_fwd(xbc, w, b, name):
    t, dx = xbc.shape
    tm, tc = min(256, t), _pick(dx, (512, 256, 128))

    def body(x_ref, xp_ref, w_ref, b_ref, o_ref, buf):
        i = pl.program_id(1)
        _fill(buf, xp_ref[...].astype(F32) * (i > 0).astype(F32), x_ref[...].astype(F32), None, tm)
        pre = _conv_rows(buf, w_ref[...], 4, tm) + b_ref[...]
        o_ref[...] = (pre * _sigmoid(pre)).astype(o_ref.dtype)

    cur, prev, _ = _tile_specs(t, tm, tc, 0)
    return _pc(body, name=name, grid=(dx // tc, t // tm),
               in_specs=[cur, prev, pl.BlockSpec((4, tc), lambda j, i: (0, j)),
                         pl.BlockSpec((1, tc), lambda j, i: (0, j))],
               out_specs=pl.BlockSpec((tm, tc), lambda j, i: (i, j)),
               out_shape=_sds((t, dx), BF16),
               scratch_shapes=[pltpu.VMEM((tm + 2 * HALO, tc), F32)],
               compiler_params=_params())(xbc, xbc, w, b)


def _dsilu(pre):
    s = _sigmoid(pre)
    return s * (1.0 + pre * (1.0 - s))


def _conv_s_bwd(xbc, dxc, w, b, name):
    t, dx = xbc.shape
    tm, tc = min(256, t), _pick(dx, (512, 256, 128))
    nt = t // tm

    def body(x_ref, xp_ref, xn_ref, g_ref, gn_ref, w_ref, b_ref, dx_ref, acc_ref, xbuf, dbuf):
        i = pl.program_id(1)

        @pl.when(i == 0)
        def _():
            acc_ref[...] = jnp.zeros_like(acc_ref)
        wv = w_ref[...]
        _fill(xbuf, xp_ref[...].astype(F32) * (i > 0).astype(F32), x_ref[...].astype(F32),
              xn_ref[...].astype(F32), tm)
        pre = _conv_rows(xbuf, wv, 4, tm + HALO) + b_ref[...]
        ds = _dsilu(pre)
        dcur = g_ref[...].astype(F32) * ds[0:tm, :]
        dnxt = gn_ref[...].astype(F32) * ds[tm:tm + HALO, :] * (i < nt - 1).astype(F32)
        dbuf[0:tm, :] = dcur
        dbuf[tm:tm + HALO, :] = dnxt
        dx_ref[...] = _conv_transpose(dbuf, wv, 4, tm).astype(BF16)
        _acc_conv_grads(acc_ref, dcur, xbuf, 4, tm, True)

    cur, prev, nxt = _tile_specs(t, tm, tc, 0)
    return _pc(body, name=name, grid=(dx // tc, nt),
               in_specs=[cur, prev, nxt, cur, nxt, pl.BlockSpec((4, tc), lambda j, i: (0, j)),
                         pl.BlockSpec((1, tc), lambda j, i: (0, j))],
               out_specs=[pl.BlockSpec((tm, tc), lambda j, i: (i, j)), pl.BlockSpec((8, tc), lambda j, i: (0, j))],
               out_shape=[_sds((t, dx), BF16), _sds((8, dx), F32)],
               scratch_shapes=[pltpu.VMEM((tm + 2 * HALO, tc), F32), pltpu.VMEM((tm + HALO, tc), F32)],
               compiler_params=_params())(xbc, xbc, xbc, dxc, dxc, w, b)


def _ffn_fwd(hv, w, b, f, name):
    t = hv.shape[0]
    tm, tc = min(256, t), _pick(f, (512, 256, 128))
    nf = f // tc

    def body(h1_ref, h1p_ref, h3_ref, w_ref, b_ref, o_ref, buf):
        i = pl.program_id(1)
        _fill(buf, h1p_ref[...].astype(F32) * (i > 0).astype(F32), h1_ref[...].astype(F32), None, tm)
        c1 = _conv_rows(buf, w_ref[...], 3, tm) + b_ref[...]
        o_ref[...] = (c1 * _sigmoid(c1) * h3_ref[...].astype(F32)).astype(o_ref.dtype)

    h1_cur, h1_prev, _ = _tile_specs(t, tm, tc, 0)
    h3_cur, _, _ = _tile_specs(t, tm, tc, nf)
    return _pc(body, name=name, grid=(nf, t // tm),
               in_specs=[h1_cur, h1_prev, h3_cur, pl.BlockSpec((3, tc), lambda j, i: (0, j)),
                         pl.BlockSpec((1, tc), lambda j, i: (0, j))],
               out_specs=pl.BlockSpec((tm, tc), lambda j, i: (i, j)),
               out_shape=_sds((t, f), BF16),
               scratch_shapes=[pltpu.VMEM((tm + 2 * HALO, tc), F32)],
               compiler_params=_params())(hv, hv, hv, w, b)


def _ffn_bwd(hv, dact, w, b, f, name):
    t = hv.shape[0]
    tm, tc = min(256, t), _pick(f, (512, 256, 128))
    nf, nt = f // tc, t // tm

    def body(h1_ref, h1p_ref, h1n_ref, h3_ref, h3n_ref, g_ref, gn_ref, w_ref, b_ref,
             dh1_ref, dh3_ref, acc_ref, xbuf, dbuf):
        i = pl.program_id(1)

        @pl.when(i == 0)
        def _():
            acc_ref[...] = jnp.zeros_like(acc_ref)
        wv = w_ref[...]
        _fill(xbuf, h1p_ref[...].astype(F32) * (i > 0).astype(F32), h1_ref[...].astype(F32),
              h1n_ref[...].astype(F32), tm)
        c1 = _conv_rows(xbuf, wv, 3, tm + HALO) + b_ref[...]
        s = _sigmoid(c1)
        ds = s * (1.0 + c1 * (1.0 - s))
        g = g_ref[...].astype(F32)
        dh3_ref[...] = (g * c1[0:tm, :] * s[0:tm, :]).astype(BF16)
        dcur = g * h3_ref[...].astype(F32) * ds[0:tm, :]
        dnxt = gn_ref[...].astype(F32) * h3n_ref[...].astype(F32) * ds[tm:tm + HALO, :] * (i < nt - 1).astype(F32)
        dbuf[0:tm, :] = dcur
        dbuf[tm:tm + HALO, :] = dnxt
        dh1_ref[...] = _conv_transpose(dbuf, wv, 3, tm).astype(BF16)
        _acc_conv_grads(acc_ref, dcur, xbuf, 3, tm, True)

    h1_cur, h1_prev, h1_next = _tile_specs(t, tm, tc, 0)
    h3_cur, _, h3_next = _tile_specs(t, tm, tc, nf)
    g_cur, _, g_next = _tile_specs(t, tm, tc, 0)
    out = pl.BlockSpec((tm, tc), lambda j, i: (i, j))
    return _pc(body, name=name, grid=(nf, nt),
               in_specs=[h1_cur, h1_prev, h1_next, h3_cur, h3_next, g_cur, g_next,
                         pl.BlockSpec((3, tc), lambda j, i: (0, j)), pl.BlockSpec((1, tc), lambda j, i: (0, j))],
               out_specs=[out, out, pl.BlockSpec((8, tc), lambda j, i: (0, j))],
               out_shape=[_sds((t, f), BF16), _sds((t, f), BF16), _sds((8, f), F32)],
               scratch_shapes=[pltpu.VMEM((tm + 2 * HALO, tc), F32), pltpu.VMEM((tm + HALO, tc), F32)],
               compiler_params=_params())(hv, hv, hv, hv, hv, dact, dact, w, b)


def _gnorm_fwd(y, z, w, name):
    t, di = y.shape
    gw = di // N_GROUPS
    tm = min(256, t)

    def body(y_ref, z_ref, w_ref, o_ref):
        zv = z_ref[...].astype(F32)
        yz = y_ref[...].astype(F32) * zv * _sigmoid(zv)
        r = lax.rsqrt(jnp.mean(yz * yz, axis=-1, keepdims=True) + EPS)
        o_ref[...] = (yz * r * w_ref[...]).astype(o_ref.dtype)

    blk = pl.BlockSpec((tm, gw), lambda j, i: (i, j))
    return _pc(body, name=name, grid=(N_GROUPS, t // tm),
               in_specs=[blk, blk, pl.BlockSpec((1, gw), lambda j, i: (0, j))],
               out_specs=blk, out_shape=_sds((t, di), BF16), compiler_params=_params())(y, z, w)


def _gnorm_bwd(y, z, dyn, w, name):
    t, di = y.shape
    gw = di // N_GROUPS
    tm = min(256, t)

    def body(y_ref, z_ref, g_ref, w_ref, dy_ref, dz_ref, dw_ref):
        @pl.when(pl.program_id(1) == 0)
        def _():
            dw_ref[...] = jnp.zeros_like(dw_ref)
        yv, zv, g = y_ref[...].astype(F32), z_ref[...].astype(F32), g_ref[...].astype(F32)
        s = _sigmoid(zv)
        sz = zv * s
        yz = yv * sz
        r = lax.rsqrt(jnp.mean(yz * yz, axis=-1, keepdims=True) + EPS)
        n = yz * r
        dn = g * w_ref[...]
        dw_ref[0:1, :] += jnp.sum(g * n, axis=0, keepdims=True)
        dyz = r * (dn - n * jnp.mean(dn * n, axis=-1, keepdims=True))
        dy_ref[...] = (dyz * sz).astype(BF16)
        dz_ref[...] = (dyz * yv * s * (1.0 + zv * (1.0 - s))).astype(BF16)

    blk = pl.BlockSpec((tm, gw), lambda j, i: (i, j))
    return _pc(body, name=name, grid=(N_GROUPS, t // tm),
               in_specs=[blk, blk, blk, pl.BlockSpec((1, gw), lambda j, i: (0, j))],
               out_specs=[blk, blk, pl.BlockSpec((8, gw), lambda j, i: (0, j))],
               out_shape=[_sds((t, di), BF16), _sds((t, di), BF16), _sds((8, di), F32)],
               compiler_params=_params())(y, z, dyn, w)


def _merge_fwd(gates, ya, ys, d, name):
    t = ya.shape[0]
    tm, tc = min(256, t), _pick(d, (512, 256, 128))
    nd = d // tc

    def body(ga_ref, gs_ref, ya_ref, ys_ref, o_ref):
        o_ref[...] = (_sigmoid(ga_ref[...].astype(F32)) * ya_ref[...].astype(F32)
                      + _sigmoid(gs_ref[...].astype(F32)) * ys_ref[...].astype(F32)).astype(o_ref.dtype)

    blk = pl.BlockSpec((tm, tc), lambda j, i: (i, j))
    return _pc(body, name=name, grid=(nd, t // tm),
               in_specs=[blk, pl.BlockSpec((tm, tc), lambda j, i: (i, nd + j)), blk, blk],
               out_specs=blk, out_shape=_sds((t, d), BF16), compiler_params=_params())(gates, gates, ya, ys)


def _merge_bwd(dm, gates, ya, ys, d, name):
    t = ya.shape[0]
    tm, tc = min(256, t), _pick(d, (512, 256, 128))
    nd = d // tc

    def body(dm_ref, ga_ref, gs_ref, ya_ref, ys_ref, dya_ref, dys_ref, dga_ref, dgs_ref):
        g = dm_ref[...].astype(F32)
        sa, ss = _sigmoid(ga_ref[...].astype(F32)), _sigmoid(gs_ref[...].astype(F32))
        dya_ref[...] = (g * sa).astype(BF16)
        dys_ref[...] = (g * ss).astype(BF16)
        dga_ref[...] = (g * ya_ref[...].astype(F32) * sa * (1.0 - sa)).astype(BF16)
        dgs_ref[...] = (g * ys_ref[...].astype(F32) * ss * (1.0 - ss)).astype(BF16)

    blk = pl.BlockSpec((tm, tc), lambda j, i: (i, j))
    return _pc(body, name=name, grid=(nd, t // tm),
               in_specs=[blk, blk, pl.BlockSpec((tm, tc), lambda j, i: (i, nd + j)), blk, blk],
               out_specs=[blk] * 4, out_shape=[_sds((t, d), BF16)] * 4,
               compiler_params=_params())(dm, gates, gates, ya, ys)


def _ssd_chunk_terms(dtr, dtb, alog):
    xx = dtr + dtb
    dt = jnp.maximum(xx, 0.0) + jnp.log(1.0 + jnp.exp(-jnp.abs(xx)))
    a = -jnp.exp(alog)
    li = lax.broadcasted_iota(jnp.int32, (CHUNK, CHUNK), 0)
    si = lax.broadcasted_iota(jnp.int32, (CHUNK, CHUNK), 1)
    causal = li >= si
    acum = _dot(causal.astype(F32), dt * a, 1, 0, HIGHEST)
    return xx, dt, a, acum, acum.T, causal


def _ssd_fwd(xc, dtr, dtb, alog, dsk, di, name):
    t = xc.shape[0]
    dx = xc.shape[1]
    nc = t // CHUNK
    hpg = di // HEAD_DIM // N_GROUPS
    boff, coff = di, di + N_GROUPS * D_STATE

    def body(xc_ref, dtr_ref, dtb_ref, alog_ref, dsk_ref, y_ref, st_ref, state):
        @pl.when(pl.program_id(0) == 0)
        def _():
            state[...] = jnp.zeros_like(state)
        _, dt, _, acum, acum_t, causal = _ssd_chunk_terms(dtr_ref[...], dtb_ref[...], alog_ref[...])
        last = acum[CHUNK - 1:CHUNK, :]
        eacum, eend, cdec = jnp.exp(acum), jnp.exp(last - acum), jnp.exp(last)
        dskv = dsk_ref[...]
        st_ref[0] = state[...]
        for g in range(N_GROUPS):
            bg = xc_ref[:, boff + g * D_STATE:boff + (g + 1) * D_STATE]
            cg = xc_ref[:, coff + g * D_STATE:coff + (g + 1) * D_STATE]
            gm = _dot(cg, bg, 1, 1)
            for j in range(hpg):
                h = g * hpg + j
                sl = slice(h * HEAD_DIM, (h + 1) * HEAD_DIM)
                x = xc_ref[:, sl].astype(F32)
                xd = x * dt[:, h:h + 1]
                seg = acum[:, h:h + 1] - acum_t[h:h + 1, :]
                lm = jnp.exp(jnp.where(causal, seg, -1e30))
                sin = state[:, sl]
                yd = _dot((gm * lm).astype(BF16), xd.astype(BF16), 1, 0)
                yo = _dot(cg, sin.astype(BF16), 1, 0) * eacum[:, h:h + 1]
                y_ref[:, sl] = (yd + yo + dskv[:, h:h + 1] * x).astype(y_ref.dtype)
                xe = (xd * eend[:, h:h + 1]).astype(BF16)
                state[:, sl] = cdec[:, h:h + 1] * sin + _dot(bg, xe, 0, 0)

    small = pl.BlockSpec((1, DT_LANES), lambda c: (0, 0))
    return _pc(body, name=name, grid=(nc,),
               in_specs=[pl.BlockSpec((CHUNK, dx), lambda c: (c, 0)),
                         pl.BlockSpec((CHUNK, DT_LANES), lambda c: (c, 0)), small, small, small],
               out_specs=[pl.BlockSpec((CHUNK, di), lambda c: (c, 0)),
                          pl.BlockSpec((1, D_STATE, di), lambda c: (c, 0, 0))],
               out_shape=[_sds((t, di), BF16), _sds((nc, D_STATE, di), F32)],
               scratch_shapes=[pltpu.VMEM((D_STATE, di), F32)],
               compiler_params=_params())(xc, dtr, dtb, alog, dsk)


def _ssd_bwd(xc, dtr, dy, states, dtb, alog, dsk, di, name):
    t = xc.shape[0]
    dx = xc.shape[1]
    nc = t // CHUNK
    hpg = di // HEAD_DIM // N_GROUPS
    boff, coff = di, di + N_GROUPS * D_STATE

    def body(xc_ref, dtr_ref, dy_ref, st_ref, dtb_ref, alog_ref, dsk_ref, dxc_ref, ddtr_ref, sm_ref, dstate):
        @pl.when(pl.program_id(0) == 0)
        def _():
            dstate[...] = jnp.zeros_like(dstate)
            sm_ref[...] = jnp.zeros_like(sm_ref)
        xx, dt, a, acum, acum_t, causal = _ssd_chunk_terms(dtr_ref[...], dtb_ref[...], alog_ref[...])
        last = acum[CHUNK - 1:CHUNK, :]
        eacum, eend, cdec = jnp.exp(acum), jnp.exp(last - acum), jnp.exp(last)
        dskv = dsk_ref[...]
        lane = lax.broadcasted_iota(jnp.int32, (CHUNK, DT_LANES), 1)
        subl = lax.broadcasted_iota(jnp.int32, (DT_LANES, CHUNK), 0)
        lane1 = lax.broadcasted_iota(jnp.int32, (1, DT_LANES), 1)
        is_last = lax.broadcasted_iota(jnp.int32, (CHUNK, 1), 0) == CHUNK - 1
        da_col = jnp.zeros((CHUNK, DT_LANES), F32)
        da_row = jnp.zeros((DT_LANES, CHUNK), F32)
        ddt_col = jnp.zeros((CHUNK, DT_LANES), F32)
        ddsk = jnp.zeros((1, DT_LANES), F32)
        for g in range(N_GROUPS):
            bg = xc_ref[:, boff + g * D_STATE:boff + (g + 1) * D_STATE]
            cg = xc_ref[:, coff + g * D_STATE:coff + (g + 1) * D_STATE]
            gm = _dot(cg, bg, 1, 1)
            dg = jnp.zeros((CHUNK, CHUNK), F32)
            dbg = jnp.zeros((CHUNK, D_STATE), F32)
            dcg = jnp.zeros((CHUNK, D_STATE), F32)
            for j in range(hpg):
                h = g * hpg + j
                sl = slice(h * HEAD_DIM, (h + 1) * HEAD_DIM)
                x = xc_ref[:, sl].astype(F32)
                dth = dt[:, h:h + 1]
                xd = x * dth
                xdb = xd.astype(BF16)
                seg = acum[:, h:h + 1] - acum_t[h:h + 1, :]
                lm = jnp.exp(jnp.where(causal, seg, -1e30))
                mf = gm * lm
                dyb = dy_ref[:, sl]
                dyf = dyb.astype(F32)
                sin = st_ref[0, :, sl]
                sinb = sin.astype(BF16)
                ds = dstate[:, sl]
                dsb = ds.astype(BF16)
                ea, ee, cd = eacum[:, h:h + 1], eend[:, h:h + 1], cdec[:, h:h + 1]
                bds = _dot(bg, dsb, 1, 0)
                dxd = _dot(mf.astype(BF16), dyb, 0, 0) + bds * ee
                dm = _dot(dyb, xdb, 1, 1)
                dg = dg + dm * lm
                q = dm * mf
                dyeb = (dyf * ea).astype(BF16)
                dcg = dcg + _dot(dyeb, sinb, 1, 1)
                dsin = cd * ds + _dot(cg, dyeb, 0, 0)
                yo = _dot(cg, sinb, 1, 0) * ea
                dbg = dbg + _dot((xd * ee).astype(BF16), dsb, 1, 1)
                we = jnp.sum(bds * xd, axis=1, keepdims=True) * ee
                at_last = jnp.sum(we, axis=0, keepdims=True) + cd * jnp.sum(
                    jnp.sum(ds * sin, axis=1, keepdims=True), axis=0, keepdims=True)
                col = (jnp.sum(q, axis=1, keepdims=True) + jnp.sum(dyf * yo, axis=1, keepdims=True) - we
                       + jnp.where(is_last, at_last, 0.0))
                da_col = da_col + jnp.where(lane == h, col, 0.0)
                da_row = da_row - jnp.where(subl == h, jnp.sum(q, axis=0, keepdims=True), 0.0)
                ddt_col = ddt_col + jnp.where(lane == h, jnp.sum(dxd * x, axis=1, keepdims=True), 0.0)
                ddsk = ddsk + jnp.where(lane1 == h, jnp.sum(jnp.sum(dyf * x, axis=1, keepdims=True),
                                                             axis=0, keepdims=True), 0.0)
                dxc_ref[:, sl] = (dxd * dth + dskv[:, h:h + 1] * dyf).astype(dxc_ref.dtype)
                dstate[:, sl] = dsin
            dgb = dg.astype(BF16)
            dcg = dcg + _dot(dgb, bg, 1, 0)
            dbg = dbg + _dot(dgb, cg, 0, 0)
            dxc_ref[:, boff + g * D_STATE:boff + (g + 1) * D_STATE] = dbg.astype(dxc_ref.dtype)
            dxc_ref[:, coff + g * D_STATE:coff + (g + 1) * D_STATE] = dcg.astype(dxc_ref.dtype)
        da = da_col + da_row.T
        li = lax.broadcasted_iota(jnp.int32, (CHUNK, CHUNK), 0)
        si = lax.broadcasted_iota(jnp.int32, (CHUNK, CHUNK), 1)
        dla = _dot((si >= li).astype(F32), da, 1, 0, HIGHEST)
        ddtr = (ddt_col + dla * a) * _sigmoid(xx)
        ddtr_ref[...] = ddtr
        sm_ref[0:1, :] += jnp.sum(ddtr, axis=0, keepdims=True)
        sm_ref[1:2, :] += jnp.sum(dla * dt, axis=0, keepdims=True) * a
        sm_ref[2:3, :] += ddsk

    small = pl.BlockSpec((1, DT_LANES), lambda c: (0, 0))
    rev = lambda c: (nc - 1 - c, 0)
    return _pc(body, name=name, grid=(nc,),
               in_specs=[pl.BlockSpec((CHUNK, dx), rev), pl.BlockSpec((CHUNK, DT_LANES), rev),
                         pl.BlockSpec((CHUNK, di), rev),
                         pl.BlockSpec((1, D_STATE, di), lambda c: (nc - 1 - c, 0, 0)), small, small, small],
               out_specs=[pl.BlockSpec((CHUNK, dx), rev), pl.BlockSpec((CHUNK, DT_LANES), rev),
                          pl.BlockSpec((8, DT_LANES), lambda c: (0, 0))],
               out_shape=[_sds((t, dx), BF16), _sds((t, DT_LANES), F32), _sds((8, DT_LANES), F32)],
               scratch_shapes=[pltpu.VMEM((D_STATE, di), F32)],
               compiler_params=_params())(xc, dtr, dy, states, dtb, alog, dsk)


def _adamw(parts, w, m, v, name):
    npart, rows, width = parts.shape
    tr = _pick(rows, (64, 32, 16, 8))
    c1 = 1.0 - ADAM_B1 ** ADAM_STEP
    c2 = 1.0 - ADAM_B2 ** ADAM_STEP

    def body(p_ref, w_ref, m_ref, v_ref, g_ref, d_ref, nm_ref, nv_ref):
        g = p_ref[0].astype(F32)
        for p in range(1, npart):
            g = g + p_ref[p].astype(F32)
        nm = ADAM_B1 * m_ref[...] + (1.0 - ADAM_B1) * g
        nv = ADAM_B2 * v_ref[...] + (1.0 - ADAM_B2) * (g * g)
        g_ref[...] = g
        nm_ref[...] = nm
        nv_ref[...] = nv
        d_ref[...] = -ADAM_LR * ((nm / c1) / (jnp.sqrt(nv / c2) + ADAM_EPS) + ADAM_WD * w_ref[...])

    blk = pl.BlockSpec((tr, width), lambda i: (i, 0))
    return _pc(body, name=name, grid=(rows // tr,),
               in_specs=[pl.BlockSpec((npart, tr, width), lambda i: (0, i, 0)), blk, blk, blk],
               out_specs=[blk] * 4, out_shape=[_sds((rows, width), F32)] * 4,
               compiler_params=_params())(parts, w, m, v)


def _sum_parts(parts, name):
    npart, rows, width = parts.shape

    def body(p_ref, o_ref):
        g = p_ref[0]
        for p in range(1, npart):
            g = g + p_ref[p]
        o_ref[...] = g

    return _pc(body, name=name, out_shape=_sds((rows, width), F32), compiler_params=_params())(parts)


def _peers():
    x, y, c = lax.axis_index("x"), lax.axis_index("y"), lax.axis_index("c")
    out = []
    for k in range(1, N_DEV):
        px = 1 - x if k & 4 else x
        py = 1 - y if k & 2 else y
        pc = 1 - c if k & 1 else c
        out.append(((px, py, pc), 4 * px + 2 * py + pc))
    return 4 * x + 2 * y + c, out


def _exchange(big, small, scatter, name):
    r, w = big.shape[-2:]
    rs = small.shape[0]

    def body(b_ref, s_ref, ob_ref, os_ref, send_sems, recv_sems, local_sems):
        me, peers = _peers()
        mine = b_ref.at[me] if scatter else b_ref
        local = [pltpu.make_async_copy(mine, ob_ref.at[me], local_sems.at[0]),
                 pltpu.make_async_copy(s_ref, os_ref.at[me], local_sems.at[1])]
        for cp in local:
            cp.start()
        sends, recvs = [], []
        for k, (peer, pidx) in enumerate(peers):
            src = b_ref.at[pidx] if scatter else b_ref
            sends.append(pltpu.make_async_remote_copy(src_ref=src, dst_ref=ob_ref.at[me],
                                                      send_sem=send_sems.at[2 * k], recv_sem=recv_sems.at[2 * k],
                                                      device_id=peer, device_id_type=MESH))
            sends.append(pltpu.make_async_remote_copy(src_ref=s_ref, dst_ref=os_ref.at[me],
                                                      send_sem=send_sems.at[2 * k + 1], recv_sem=recv_sems.at[2 * k + 1],
                                                      device_id=peer, device_id_type=MESH))
            recvs.append(pltpu.make_async_remote_copy(src_ref=src, dst_ref=ob_ref.at[pidx],
                                                      send_sem=send_sems.at[2 * k], recv_sem=recv_sems.at[2 * k],
                                                      device_id=peer, device_id_type=MESH))
            recvs.append(pltpu.make_async_remote_copy(src_ref=s_ref, dst_ref=os_ref.at[pidx],
                                                      send_sem=send_sems.at[2 * k + 1], recv_sem=recv_sems.at[2 * k + 1],
                                                      device_id=peer, device_id_type=MESH))
        for cp in sends:
            cp.start()
        for cp in recvs:
            cp.wait_recv()
        for cp in sends:
            cp.wait_send()
        for cp in local:
            cp.wait()

    hbm = pl.BlockSpec(memory_space=pltpu.HBM)
    return _pc(body, name=name, in_specs=[hbm, hbm], out_specs=[hbm, hbm],
               out_shape=[_sds((N_DEV, r, w), big.dtype), _sds((N_DEV, rs, w), small.dtype)],
               scratch_shapes=[pltpu.SemaphoreType.DMA((2 * (N_DEV - 1),)), pltpu.SemaphoreType.DMA((2 * (N_DEV - 1),)),
                               pltpu.SemaphoreType.DMA((2,))],
               compiler_params=pltpu.CompilerParams(has_side_effects=True))(big, small)


def _pack(arrs, width, row_mult, lead=None):
    if lead is None:
        flat = jnp.concatenate([a.reshape(-1) for a in arrs])
        n = flat.shape[0]
        rows = -(-n // (width * row_mult)) * row_mult
        return jnp.pad(flat, (0, rows * width - n)).reshape(rows, width)
    flat = jnp.concatenate([a.reshape(lead, -1) for a in arrs], axis=1)
    n = flat.shape[1]
    rows = -(-n // (width * row_mult)) * row_mult
    return jnp.pad(flat, ((0, 0), (0, rows * width - n))).reshape(lead, rows, width)


def _unpack(packed, shapes, lead=None):
    out, off = [], 0
    flat = packed.reshape(-1) if lead is None else packed.reshape(lead, -1)
    for s in shapes:
        n = math.prod(s)
        if lead is None:
            out.append(flat[off:off + n].reshape(s))
        else:
            out.append(flat[:, off:off + n].reshape((lead,) + tuple(s)))
        off += n
    return out


def _cols_to_blocks(full):
    rows, cols = full.shape
    return full.reshape(rows, N_DEV, cols // N_DEV).transpose(1, 0, 2)


def _blocks_to_cols(blocks):
    nb, rows, n = blocks.shape
    return blocks.transpose(1, 0, 2).reshape(rows, nb * n)


def _pad_lanes(a, lanes):
    return jnp.pad(a, ((0, 0), (0, lanes - a.shape[1])))


BIG = ("w_in", "w_a_out", "w_s_out", "w_o", "w_up", "w_down")
CONVS = ("conv_a_w", "ssd_conv_w", "ffn_conv_w")
REPL = ("norm_mix_w", "ssd_conv_b", "dt_bias", "a_log", "d_skip", "ssd_norm_w", "norm_ffn_w", "ffn_conv_b",
        "final_norm_w")
ORDER = ("norm_mix_w", "w_in", "conv_a_w", "w_a_out", "ssd_conv_w", "ssd_conv_b", "dt_bias", "a_log", "d_skip",
         "ssd_norm_w", "w_s_out", "w_o", "norm_ffn_w", "w_up", "ffn_conv_w", "ffn_conv_b", "w_down", "final_norm_w")


def kernel(x, norm_mix_w, w_in, conv_a_w, w_a_out, ssd_conv_w, ssd_conv_b, dt_bias, a_log, d_skip, ssd_norm_w, w_s_out, w_o, norm_ffn_w, w_up, ffn_conv_w, ffn_conv_b, w_down, final_norm_w, loss_target, m_norm_mix_w, m_w_in, m_conv_a_w, m_w_a_out, m_ssd_conv_w, m_ssd_conv_b, m_dt_bias, m_a_log, m_d_skip, m_ssd_norm_w, m_w_s_out, m_w_o, m_norm_ffn_w, m_w_up, m_ffn_conv_w, m_ffn_conv_b, m_w_down, m_final_norm_w, v_norm_mix_w, v_w_in, v_conv_a_w, v_w_a_out, v_ssd_conv_w, v_ssd_conv_b, v_dt_bias, v_a_log, v_d_skip, v_ssd_norm_w, v_w_s_out, v_w_o, v_norm_ffn_w, v_w_up, v_ffn_conv_w, v_ffn_conv_b, v_w_down, v_final_norm_w):
    wts = dict(norm_mix_w=norm_mix_w, w_in=w_in, conv_a_w=conv_a_w, w_a_out=w_a_out, ssd_conv_w=ssd_conv_w,
               ssd_conv_b=ssd_conv_b, dt_bias=dt_bias, a_log=a_log, d_skip=d_skip, ssd_norm_w=ssd_norm_w,
               w_s_out=w_s_out, w_o=w_o, norm_ffn_w=norm_ffn_w, w_up=w_up, ffn_conv_w=ffn_conv_w,
               ffn_conv_b=ffn_conv_b, w_down=w_down, final_norm_w=final_norm_w)
    mom1 = dict(norm_mix_w=m_norm_mix_w, w_in=m_w_in, conv_a_w=m_conv_a_w, w_a_out=m_w_a_out,
                ssd_conv_w=m_ssd_conv_w, ssd_conv_b=m_ssd_conv_b, dt_bias=m_dt_bias, a_log=m_a_log, d_skip=m_d_skip,
                ssd_norm_w=m_ssd_norm_w, w_s_out=m_w_s_out, w_o=m_w_o, norm_ffn_w=m_norm_ffn_w, w_up=m_w_up,
                ffn_conv_w=m_ffn_conv_w, ffn_conv_b=m_ffn_conv_b, w_down=m_w_down, final_norm_w=m_final_norm_w)
    mom2 = dict(norm_mix_w=v_norm_mix_w, w_in=v_w_in, conv_a_w=v_conv_a_w, w_a_out=v_w_a_out,
                ssd_conv_w=v_ssd_conv_w, ssd_conv_b=v_ssd_conv_b, dt_bias=v_dt_bias, a_log=v_a_log, d_skip=v_d_skip,
                ssd_norm_w=v_ssd_norm_w, w_s_out=v_w_s_out, w_o=v_w_o, norm_ffn_w=v_norm_ffn_w, w_up=v_w_up,
                ffn_conv_w=v_ffn_conv_w, ffn_conv_b=v_ffn_conv_b, w_down=v_w_down, final_norm_w=v_final_norm_w)

    t, d = x.shape[1], x.shape[2]
    di = 2 * d
    nh = di // HEAD_DIM
    dxw = di + 2 * N_GROUPS * D_STATE
    f = w_down.shape[1] * N_DEV
    n_in = w_in.shape[2] * N_DEV
    me = 4 * lax.axis_index("x") + 2 * lax.axis_index("y") + lax.axis_index("c")

    big_shapes = [wts[k].shape[1:] for k in BIG]
    conv_shapes = [wts[k].shape[1:] for k in CONVS]
    big_local = _pack([wts[k].astype(BF16) for k in BIG], d, 64)
    conv_local = _pack([wts[k] for k in CONVS], d, 8)
    big_all, conv_all = _exchange(big_local, conv_local, False, "gather_weights")
    g_in, g_aout, g_sout, g_o, g_up, g_down = _unpack(big_all, big_shapes, N_DEV)
    win = _blocks_to_cols(g_in)
    waout, wsout, wo = g_aout.reshape(d, d), g_sout.reshape(di, d), g_o.reshape(d, d)
    wup, wdown = _blocks_to_cols(g_up), g_down.reshape(f, d)
    c_a, c_s, c_f = _unpack(conv_all, conv_shapes, N_DEV)
    caw, scw, fcw = _blocks_to_cols(c_a), _blocks_to_cols(c_s), _blocks_to_cols(c_f)

    o_z, o_x, o_dt = 5 * d, 7 * d, 7 * d + dxw
    w_g, w_a, w_z, w_x = win[:, :2 * d], win[:, 2 * d:o_z], win[:, o_z:o_x], win[:, o_x:o_dt]
    w_dt = _pad_lanes(win[:, o_dt:], DT_LANES)
    dtb, alog, dskp = (_pad_lanes(p[...].reshape(1, nh), DT_LANES) for p in (dt_bias, a_log, d_skip))

    x2, tgt = x[0], loss_target[0]
    u = _rms_fwd(x2, norm_mix_w, "norm_mix")
    gates = _mm([(u, w_g)], "nn", BF16, "proj_gates")
    pa = _mm([(u, w_a)], "nn", BF16, "proj_a")
    z = _mm([(u, w_z)], "nn", BF16, "proj_z")
    xbc = _mm([(u, w_x)], "nn", BF16, "proj_xbc")
    dtr = _mm([(u, w_dt)], "nn", F32, "proj_dt")
    ya_in = _conv_a_fwd(pa, caw, d, "conv_a")
    y_a = _mm([(ya_in, waout)], "nn", BF16, "a_out")
    xc = _conv_s_fwd(xbc, scw, ssd_conv_b, "conv_s")
    y, states = _ssd_fwd(xc, dtr, dtb, alog, dskp, di, "ssd")
    yn = _gnorm_fwd(y, z, ssd_norm_w, "gnorm")
    y_s = _mm([(yn, wsout)], "nn", BF16, "s_out")
    merged = _merge_fwd(gates, y_a, y_s, d, "merge")
    mo = _mm([(merged, wo)], "nn", F32, "o_proj")
    h1, v = _resnorm_fwd(x2, mo, norm_ffn_w, "norm_ffn")
    hv = _mm([(v, wup)], "nn", BF16, "up_proj")
    act = _ffn_fwd(hv, fcw, ffn_conv_b, f, "ffn_act")
    dd = _mm([(act, wdown)], "nn", F32, "down_proj")
    loss11, dh2, dh2b, g_fnw = _final(h1, dd, tgt, final_norm_w.reshape(1, d), "final")

    dact = _mm([(dh2b, wdown)], "nt", BF16, "d_act")
    gw_down = _mm_tn(act, dh2b, "gw_down")
    dh1f, dh3, g_ffn = _ffn_bwd(hv, dact, fcw, ffn_conv_b, f, "ffn_act_bwd")
    dv = _mm([(dh1f, wup[:, :f]), (dh3, wup[:, f:])], "nt", F32, "d_v")
    gw_up = jnp.concatenate([_mm_tn(v, dh1f, "gw_up1"), _mm_tn(v, dh3, "gw_up3")], axis=1)
    dh1, dh1b, g_nfw = _rms_bwd(h1, dv, norm_ffn_w, dh2, "norm_ffn_bwd")
    dmerged = _mm([(dh1b, wo)], "nt", BF16, "d_merged")
    gw_o = _mm_tn(merged, dh1b, "gw_o")
    dya, dys, dga, dgs = _merge_bwd(dmerged, gates, y_a, y_s, d, "merge_bwd")
    dyain = _mm([(dya, waout)], "nt", BF16, "d_ya_in")
    gw_aout = _mm_tn(ya_in, dya, "gw_a_out")
    db, dc, dvv, g_caw = _conv_a_bwd(pa, dyain, caw, d, "conv_a_bwd")
    dyn = _mm([(dys, wsout)], "nt", BF16, "d_yn")
    gw_sout = _mm_tn(yn, dys, "gw_s_out")
    dy, dz, g_snw = _gnorm_bwd(y, z, dyn, ssd_norm_w, "gnorm_bwd")
    dxc, ddtr, g_ssd = _ssd_bwd(xc, dtr, dy, states, dtb, alog, dskp, di, "ssd_bwd")
    dxbc, g_scw = _conv_s_bwd(xbc, dxc, scw, ssd_conv_b, "conv_s_bwd")
    ddtrb = ddtr.astype(BF16)
    segs = [(dga, win[:, :d]), (dgs, win[:, d:2 * d]), (db, win[:, 2 * d:3 * d]), (dc, win[:, 3 * d:4 * d]),
            (dvv, win[:, 4 * d:o_z]), (dz, w_z), (dxbc, w_x), (ddtrb, w_dt)]
    du = _mm(segs, "nt", F32, "d_u", tm=512, tn=256)
    gw_in = jnp.concatenate([_mm_tn(u, s[0], "gw_in%d" % i) for i, s in enumerate(segs)], axis=1)[:, :n_in]
    dx, _, g_nmw = _rms_bwd(x2, du, norm_mix_w, dh1, "norm_mix_bwd")

    big_grads = [_cols_to_blocks(gw_in), gw_aout.reshape(N_DEV, -1), gw_sout.reshape(N_DEV, -1),
                 gw_o.reshape(N_DEV, -1), _cols_to_blocks(gw_up), gw_down.reshape(N_DEV, -1)]
    big_parts = _pack([g.astype(BF16) for g in big_grads], d, 64, lead=N_DEV)
    small_grads = dict(norm_mix_w=g_nmw[0], ssd_conv_b=g_scw[4], dt_bias=g_ssd[0, :nh], a_log=g_ssd[1, :nh],
                       d_skip=g_ssd[2, :nh], ssd_norm_w=g_snw[0], norm_ffn_w=g_nfw[0], ffn_conv_b=g_ffn[3],
                       final_norm_w=g_fnw[0], conv_a_w=g_caw[:3], ssd_conv_w=g_scw[:4], ffn_conv_w=g_ffn[:3])
    small_names = REPL + CONVS
    small_parts = _pack([small_grads[k] for k in small_names], d, 8)
    big_recv, small_all = _exchange(big_parts, small_parts, True, "exchange_grads")
    small_sum = _sum_parts(small_all, "sum_small_grads")
    small_g = dict(zip(small_names, _unpack(small_sum, [small_grads[k].shape for k in small_names])))

    w_big, m_big, v_big = (_pack([src[k] for k in BIG], d, 64) for src in (wts, mom1, mom2))
    outs_big = _adamw(big_recv, w_big, m_big, v_big, "adamw_big")
    res = {}
    for kind, packed in zip(("g", "d", "m", "v"), outs_big):
        for k, a in zip(BIG, _unpack(packed, [wts[k].shape for k in BIG])):
            res[kind, k] = a
    local_g = {}
    for k in REPL:
        local_g[k] = small_g[k].reshape(wts[k].shape)
    for k in CONVS:
        n = wts[k].shape[2]
        local_g[k] = lax.dynamic_slice_in_dim(small_g[k], me * n, n, axis=1)[None]
    w_sm, m_sm, v_sm = (_pack([src[k] for k in small_names], d, 8) for src in (wts, mom1, mom2))
    g_sm = _pack([local_g[k] for k in small_names], d, 8)
    outs_sm = _adamw(g_sm[None], w_sm, m_sm, v_sm, "adamw_small")
    for kind, packed in zip(("g", "d", "m", "v"), outs_sm):
        for k, a in zip(small_names, _unpack(packed, [wts[k].shape for k in small_names])):
            res[kind, k] = a

    loss = lax.psum(loss11[0, 0], ("x", "y", "c"))
    return (loss, dx[None], *[res["g", k] for k in ORDER], *[res["d", k] for k in ORDER],
            *[res["m", k] for k in ORDER], *[res["v", k] for k in ORDER])
```

```python
import functools
import math

import jax
import jax.numpy as jnp
from jax import lax
from jax.experimental import pallas as pl
from jax.experimental.pallas import tpu as pltpu

F32 = jnp.float32
BF16 = jnp.bfloat16
EPS = 1e-5
HEAD_DIM = 64
N_GROUPS = 4
D_STATE = 128
CHUNK = 128
DT_LANES = 128
HALO = 16
N_DEV = 8
V7X_VMEM_LIMIT = 56 * 1024 * 1024
ADAM_LR, ADAM_B1, ADAM_B2, ADAM_EPS, ADAM_WD, ADAM_STEP = 0.001, 0.9, 0.999, 1e-08, 0.01, 10
HIGHEST = lax.Precision.HIGHEST
MESH = pl.DeviceIdType.MESH


def _pc(body, **kw):
    return pl.pallas_call(body, **kw)


def _params():
    return pltpu.CompilerParams(vmem_limit_bytes=V7X_VMEM_LIMIT)


def _pick(n, cands):
    for c in cands:
        if n % c == 0:
            return c
    return n


def _dot(a, b, ca, cb, prec=None):
    return lax.dot_general(a, b, (((ca,), (cb,)), ((), ())), preferred_element_type=F32, precision=prec)


def _sigmoid(x):
    return 1.0 / (1.0 + jnp.exp(-x))


def _sds(shape, dtype):
    return jax.ShapeDtypeStruct(shape, dtype)


def _mm(pairs, mode, out_dtype, name, tm=512, tn=512):
    m = pairs[0][0].shape[0]
    n = pairs[0][1].shape[1] if mode == "nn" else pairs[0][1].shape[0]
    tm = min(tm, m)
    tn = _pick(n, (tn, 256, 128))
    npair = len(pairs)
    cb = 0 if mode == "nn" else 1

    def body(*refs):
        o_ref = refs[2 * npair]
        acc = None
        for p in range(npair):
            part = _dot(refs[2 * p][...], refs[2 * p + 1][...], 1, cb)
            acc = part if acc is None else acc + part
        o_ref[...] = acc.astype(o_ref.dtype)

    in_specs, args = [], []
    for a, b in pairs:
        k = a.shape[1]
        in_specs.append(pl.BlockSpec((tm, k), lambda i, j: (i, 0)))
        if mode == "nn":
            in_specs.append(pl.BlockSpec((k, tn), lambda i, j: (0, j)))
        else:
            in_specs.append(pl.BlockSpec((tn, k), lambda i, j: (j, 0)))
        args += [a, b]
    return _pc(body, name=name, grid=(m // tm, n // tn), in_specs=in_specs,
               out_specs=pl.BlockSpec((tm, tn), lambda i, j: (i, j)),
               out_shape=_sds((m, n), out_dtype), compiler_params=_params())(*args)


def _mm_tn(a, b, name, tm=1024):
    m, ka = a.shape
    nb = b.shape[1]
    tm = min(tm, m)
    tk = _pick(ka, (1024, 1408, 512, 256, 128))
    tn = _pick(nb, (1024, 512, 256, 128))

    def body(a_ref, b_ref, o_ref):
        @pl.when(pl.program_id(2) == 0)
        def _():
            o_ref[...] = jnp.zeros_like(o_ref)
        o_ref[...] += _dot(a_ref[...], b_ref[...], 0, 0)

    return _pc(body, name=name, grid=(ka // tk, nb // tn, m // tm),
               in_specs=[pl.BlockSpec((tm, tk), lambda i, j, t: (t, i)),
                         pl.BlockSpec((tm, tn), lambda i, j, t: (t, j))],
               out_specs=pl.BlockSpec((tk, tn), lambda i, j, t: (i, j)),
               out_shape=_sds((ka, nb), F32), compiler_params=_params())(a, b)


def _rms_fwd(x, w, name):
    t, d = x.shape
    tm = min(256, t)

    def body(x_ref, w_ref, o_ref):
        xv = x_ref[...]
        r = lax.rsqrt(jnp.mean(xv * xv, axis=-1, keepdims=True) + EPS)
        o_ref[...] = (xv * r * w_ref[...]).astype(o_ref.dtype)

    return _pc(body, name=name, grid=(t // tm,),
               in_specs=[pl.BlockSpec((tm, d), lambda i: (i, 0)), pl.BlockSpec((1, d), lambda i: (0, 0))],
               out_specs=pl.BlockSpec((tm, d), lambda i: (i, 0)),
               out_shape=_sds((t, d), BF16), compiler_params=_params())(x, w)


def _resnorm_fwd(x, mo, w, name):
    t, d = x.shape
    tm = min(256, t)

    def body(x_ref, mo_ref, w_ref, h_ref, v_ref):
        h = x_ref[...] + mo_ref[...]
        r = lax.rsqrt(jnp.mean(h * h, axis=-1, keepdims=True) + EPS)
        h_ref[...] = h
        v_ref[...] = (h * r * w_ref[...]).astype(v_ref.dtype)

    row = pl.BlockSpec((tm, d), lambda i: (i, 0))
    return _pc(body, name=name, grid=(t // tm,),
               in_specs=[row, row, pl.BlockSpec((1, d), lambda i: (0, 0))],
               out_specs=[row, row], out_shape=[_sds((t, d), F32), _sds((t, d), BF16)],
               compiler_params=_params())(x, mo, w)


def _rms_bwd(h, dy, w, dres, name):
    t, d = h.shape
    tm = min(256, t)

    def body(h_ref, dy_ref, w_ref, dres_ref, dx_ref, dxb_ref, dw_ref):
        @pl.when(pl.program_id(0) == 0)
        def _():
            dw_ref[...] = jnp.zeros_like(dw_ref)
        hv = h_ref[...]
        dyv = dy_ref[...]
        r = lax.rsqrt(jnp.mean(hv * hv, axis=-1, keepdims=True) + EPS)
        n = hv * r
        dn = dyv * w_ref[...]
        dw_ref[0:1, :] += jnp.sum(dyv * n, axis=0, keepdims=True)
        dx = dres_ref[...] + r * (dn - n * jnp.mean(dn * n, axis=-1, keepdims=True))
        dx_ref[...] = dx
        dxb_ref[...] = dx.astype(BF16)

    row = pl.BlockSpec((tm, d), lambda i: (i, 0))
    return _pc(body, name=name, grid=(t // tm,),
               in_specs=[row, row, pl.BlockSpec((1, d), lambda i: (0, 0)), row],
               out_specs=[row, row, pl.BlockSpec((8, d), lambda i: (0, 0))],
               out_shape=[_sds((t, d), F32), _sds((t, d), BF16), _sds((8, d), F32)],
               compiler_params=_params())(h, dy, w, dres)


def _final(h1, dd, tgt, w, name):
    t, d = h1.shape
    tm = min(256, t)
    nt = t // tm

    def body(h1_ref, dd_ref, tgt_ref, w_ref, loss_ref, dh_ref, dhb_ref, dw_ref, acc):
        i = pl.program_id(0)

        @pl.when(i == 0)
        def _():
            dw_ref[...] = jnp.zeros_like(dw_ref)
            acc[...] = jnp.zeros_like(acc)
        h = h1_ref[...] + dd_ref[...]
        r = lax.rsqrt(jnp.mean(h * h, axis=-1, keepdims=True) + EPS)
        n = h * r
        wv = w_ref[...]
        e = n * wv - tgt_ref[...]
        acc[...] += jnp.sum(e * e, axis=0, keepdims=True)
        dout = e * (1.0 / d)
        dn = dout * wv
        dw_ref[0:1, :] += jnp.sum(dout * n, axis=0, keepdims=True)
        dh = r * (dn - n * jnp.mean(dn * n, axis=-1, keepdims=True))
        dh_ref[...] = dh
        dhb_ref[...] = dh.astype(BF16)

        @pl.when(i == nt - 1)
        def _():
            loss_ref[...] = jnp.sum(acc[...], axis=-1, keepdims=True) * (0.5 / d)

    row = pl.BlockSpec((tm, d), lambda i: (i, 0))
    return _pc(body, name=name, grid=(nt,),
               in_specs=[row, row, row, pl.BlockSpec((1, d), lambda i: (0, 0))],
               out_specs=[pl.BlockSpec((1, 1), lambda i: (0, 0)), row, row, pl.BlockSpec((8, d), lambda i: (0, 0))],
               out_shape=[_sds((1, 1), F32), _sds((t, d), F32), _sds((t, d), BF16), _sds((8, d), F32)],
               scratch_shapes=[pltpu.VMEM((1, d), F32)], compiler_params=_params())(h1, dd, tgt, w)


def _fill(buf, prev, cur, nxt, tm):
    if prev is not None:
        buf[0:HALO, :] = prev
    buf[HALO:HALO + tm, :] = cur
    if nxt is not None:
        buf[HALO + tm:HALO + tm + HALO, :] = nxt


def _conv_rows(xbuf, w, k, rows):
    out = None
    for j in range(k):
        term = w[j:j + 1, :] * xbuf[pl.ds(HALO - (k - 1) + j, rows), :]
        out = term if out is None else out + term
    return out


def _conv_transpose(dbuf, w, k, tm):
    out = None
    for j in range(k):
        term = w[j:j + 1, :] * dbuf[pl.ds(k - 1 - j, tm), :]
        out = term if out is None else out + term
    return out


def _acc_conv_grads(acc_ref, dcur, xbuf, k, tm, with_bias):
    for j in range(k):
        acc_ref[j:j + 1, :] += jnp.sum(dcur * xbuf[pl.ds(HALO - (k - 1) + j, tm), :], axis=0, keepdims=True)
    if with_bias:
        acc_ref[k:k + 1, :] += jnp.sum(dcur, axis=0, keepdims=True)


def _tile_specs(t, tm, tc, col0):
    th = tm // HALO
    last = t // HALO - 1
    cur = pl.BlockSpec((tm, tc), lambda j, i: (i, col0 + j))
    prev = pl.BlockSpec((HALO, tc), lambda j, i: (jnp.maximum(i * th - 1, 0), col0 + j))
    nxt = pl.BlockSpec((HALO, tc), lambda j, i: (jnp.minimum((i + 1) * th, last), col0 + j))
    return cur, prev, nxt


def _conv_a_fwd(pa, w, d, name):
    t = pa.shape[0]
    tm, tc = min(256, t), _pick(d, (512, 256, 128))
    nd = d // tc

    def body(b_ref, c_ref, v_ref, cp_ref, vp_ref, w_ref, o_ref, buf):
        i = pl.program_id(1)
        keep = (i > 0).astype(F32)
        prev = cp_ref[...].astype(F32) * vp_ref[...].astype(F32) * keep
        _fill(buf, prev, c_ref[...].astype(F32) * v_ref[...].astype(F32), None, tm)
        q = _conv_rows(buf, w_ref[...], 3, tm)
        o_ref[...] = (b_ref[...].astype(F32) * q).astype(o_ref.dtype)

    b_cur, _, _ = _tile_specs(t, tm, tc, 0)
    c_cur, c_prev, _ = _tile_specs(t, tm, tc, nd)
    v_cur, v_prev, _ = _tile_specs(t, tm, tc, 2 * nd)
    return _pc(body, name=name, grid=(nd, t // tm),
               in_specs=[b_cur, c_cur, v_cur, c_prev, v_prev, pl.BlockSpec((3, tc), lambda j, i: (0, j))],
               out_specs=pl.BlockSpec((tm, tc), lambda j, i: (i, j)),
               out_shape=_sds((t, d), BF16),
               scratch_shapes=[pltpu.VMEM((tm + 2 * HALO, tc), F32)],
               compiler_params=_params())(pa, pa, pa, pa, pa, w)


def _conv_a_bwd(pa, dya, w, d, name):
    t = pa.shape[0]
    tm, tc = min(256, t), _pick(d, (512, 256, 128))
    nd, nt = d // tc, t // tm

    def body(b_ref, c_ref, v_ref, cp_ref, vp_ref, bn_ref, g_ref, gn_ref, w_ref,
             db_ref, dc_ref, dv_ref, acc_ref, pbuf, dbuf):
        i = pl.program_id(1)

        @pl.when(i == 0)
        def _():
            acc_ref[...] = jnp.zeros_like(acc_ref)
        wv = w_ref[...]
        cv, vv = c_ref[...].astype(F32), v_ref[...].astype(F32)
        prev = cp_ref[...].astype(F32) * vp_ref[...].astype(F32) * (i > 0).astype(F32)
        _fill(pbuf, prev, cv * vv, None, tm)
        g = g_ref[...].astype(F32)
        dq = g * b_ref[...].astype(F32)
        dqn = gn_ref[...].astype(F32) * bn_ref[...].astype(F32) * (i < nt - 1).astype(F32)
        dbuf[0:tm, :] = dq
        dbuf[tm:tm + HALO, :] = dqn
        q = _conv_rows(pbuf, wv, 3, tm)
        db_ref[...] = (g * q).astype(BF16)
        dp = _conv_transpose(dbuf, wv, 3, tm)
        dc_ref[...] = (dp * vv).astype(BF16)
        dv_ref[...] = (dp * cv).astype(BF16)
        _acc_conv_grads(acc_ref, dq, pbuf, 3, tm, False)

    b_cur, _, b_next = _tile_specs(t, tm, tc, 0)
    c_cur, c_prev, _ = _tile_specs(t, tm, tc, nd)
    v_cur, v_prev, _ = _tile_specs(t, tm, tc, 2 * nd)
    g_cur, _, g_next = _tile_specs(t, tm, tc, 0)
    out = pl.BlockSpec((tm, tc), lambda j, i: (i, j))
    return _pc(body, name=name, grid=(nd, nt),
               in_specs=[b_cur, c_cur, v_cur, c_prev, v_prev, b_next, g_cur, g_next,
                         pl.BlockSpec((3, tc), lambda j, i: (0, j))],
               out_specs=[out, out, out, pl.BlockSpec((8, tc), lambda j, i: (0, j))],
               out_shape=[_sds((t, d), BF16)] * 3 + [_sds((8, d), F32)],
               scratch_shapes=[pltpu.VMEM((tm + 2 * HALO, tc), F32), pltpu.VMEM((tm + HALO, tc), F32)],
               compiler_params=_params())(pa, pa, pa, pa, pa, pa, dya, dya, w)


def _conv_s_fwd(xbc, w, b, name):
    t, dx = xbc.shape
    tm, tc = min(256, t), _pick(dx, (512, 256, 128))

    def body(x_ref, xp_ref, w_ref, b_ref, o_ref, buf):
        i = pl.program_id(1)
        _fill(buf, xp_ref[...].astype(F32) * (i > 0).astype(F32), x_ref[...].astype(F32), None, tm)
        pre = _conv_rows(buf, w_ref[...], 4, tm) + b_ref[...]
        o_ref[...] = (pre * _sigmoid(pre)).astype(o_ref.dtype)

    cur, prev, _ = _tile_specs(t, tm, tc, 0)
    return _pc(body, name=name, grid=(dx // tc, t // tm),
               in_specs=[cur, prev, pl.BlockSpec((4, tc), lambda j, i: (0, j)),
                         pl.BlockSpec((1, tc), lambda j, i: (0, j))],
               out_specs=pl.BlockSpec((tm, tc), lambda j, i: (i, j)),
               out_shape=_sds((t, dx), BF16),
               scratch_shapes=[pltpu.VMEM((tm + 2 * HALO, tc), F32)],
               compiler_params=_params())(xbc, xbc, w, b)


def _dsilu(pre):
    s = _sigmoid(pre)
    return s * (1.0 + pre * (1.0 - s))


def _conv_s_bwd(xbc, dxc, w, b, name):
    t, dx = xbc.shape
    tm, tc = min(256, t), _pick(dx, (512, 256, 128))
    nt = t // tm

    def body(x_ref, xp_ref, xn_ref, g_ref, gn_ref, w_ref, b_ref, dx_ref, acc_ref, xbuf, dbuf):
        i = pl.program_id(1)

        @pl.when(i == 0)
        def _():
            acc_ref[...] = jnp.zeros_like(acc_ref)
        wv = w_ref[...]
        _fill(xbuf, xp_ref[...].astype(F32) * (i > 0).astype(F32), x_ref[...].astype(F32),
              xn_ref[...].astype(F32), tm)
        pre = _conv_rows(xbuf, wv, 4, tm + HALO) + b_ref[...]
        ds = _dsilu(pre)
        dcur = g_ref[...].astype(F32) * ds[0:tm, :]
        dnxt = gn_ref[...].astype(F32) * ds[tm:tm + HALO, :] * (i < nt - 1).astype(F32)
        dbuf[0:tm, :] = dcur
        dbuf[tm:tm + HALO, :] = dnxt
        dx_ref[...] = _conv_transpose(dbuf, wv, 4, tm).astype(BF16)
        _acc_conv_grads(acc_ref, dcur, xbuf, 4, tm, True)

    cur, prev, nxt = _tile_specs(t, tm, tc, 0)
    return _pc(body, name=name, grid=(dx // tc, nt),
               in_specs=[cur, prev, nxt, cur, nxt, pl.BlockSpec((4, tc), lambda j, i: (0, j)),
                         pl.BlockSpec((1, tc), lambda j, i: (0, j))],
               out_specs=[pl.BlockSpec((tm, tc), lambda j, i: (i, j)), pl.BlockSpec((8, tc), lambda j, i: (0, j))],
               out_shape=[_sds((t, dx), BF16), _sds((8, dx), F32)],
               scratch_shapes=[pltpu.VMEM((tm + 2 * HALO, tc), F32), pltpu.VMEM((tm + HALO, tc), F32)],
               compiler_params=_params())(xbc, xbc, xbc, dxc, dxc, w, b)


def _ffn_fwd(hv, w, b, f, name):
    t = hv.shape[0]
    tm, tc = min(256, t), _pick(f, (512, 256, 128))
    nf = f // tc

    def body(h1_ref, h1p_ref, h3_ref, w_ref, b_ref, o_ref, buf):
        i = pl.program_id(1)
        _fill(buf, h1p_ref[...].astype(F32) * (i > 0).astype(F32), h1_ref[...].astype(F32), None, tm)
        c1 = _conv_rows(buf, w_ref[...], 3, tm) + b_ref[...]
        o_ref[...] = (c1 * _sigmoid(c1) * h3_ref[...].astype(F32)).astype(o_ref.dtype)

    h1_cur, h1_prev, _ = _tile_specs(t, tm, tc, 0)
    h3_cur, _, _ = _tile_specs(t, tm, tc, nf)
    return _pc(body, name=name, grid=(nf, t // tm),
               in_specs=[h1_cur, h1_prev, h3_cur, pl.BlockSpec((3, tc), lambda j, i: (0, j)),
                         pl.BlockSpec((1, tc), lambda j, i: (0, j))],
               out_specs=pl.BlockSpec((tm, tc), lambda j, i: (i, j)),
               out_shape=_sds((t, f), BF16),
               scratch_shapes=[pltpu.VMEM((tm + 2 * HALO, tc), F32)],
               compiler_params=_params())(hv, hv, hv, w, b)


def _ffn_bwd(hv, dact, w, b, f, name):
    t = hv.shape[0]
    tm, tc = min(256, t), _pick(f, (512, 256, 128))
    nf, nt = f // tc, t // tm

    def body(h1_ref, h1p_ref, h1n_ref, h3_ref, h3n_ref, g_ref, gn_ref, w_ref, b_ref,
             dh1_ref, dh3_ref, acc_ref, xbuf, dbuf):
        i = pl.program_id(1)

        @pl.when(i == 0)
        def _():
            acc_ref[...] = jnp.zeros_like(acc_ref)
        wv = w_ref[...]
        _fill(xbuf, h1p_ref[...].astype(F32) * (i > 0).astype(F32), h1_ref[...].astype(F32),
              h1n_ref[...].astype(F32), tm)
        c1 = _conv_rows(xbuf, wv, 3, tm + HALO) + b_ref[...]
        s = _sigmoid(c1)
        ds = s * (1.0 + c1 * (1.0 - s))
        g = g_ref[...].astype(F32)
        dh3_ref[...] = (g * c1[0:tm, :] * s[0:tm, :]).astype(BF16)
        dcur = g * h3_ref[...].astype(F32) * ds[0:tm, :]
        dnxt = gn_ref[...].astype(F32) * h3n_ref[...].astype(F32) * ds[tm:tm + HALO, :] * (i < nt - 1).astype(F32)
        dbuf[0:tm, :] = dcur
        dbuf[tm:tm + HALO, :] = dnxt
        dh1_ref[...] = _conv_transpose(dbuf, wv, 3, tm).astype(BF16)
        _acc_conv_grads(acc_ref, dcur, xbuf, 3, tm, True)

    h1_cur, h1_prev, h1_next = _tile_specs(t, tm, tc, 0)
    h3_cur, _, h3_next = _tile_specs(t, tm, tc, nf)
    g_cur, _, g_next = _tile_specs(t, tm, tc, 0)
    out = pl.BlockSpec((tm, tc), lambda j, i: (i, j))
    return _pc(body, name=name, grid=(nf, nt),
               in_specs=[h1_cur, h1_prev, h1_next, h3_cur, h3_next, g_cur, g_next,
                         pl.BlockSpec((3, tc), lambda j, i: (0, j)), pl.BlockSpec((1, tc), lambda j, i: (0, j))],
               out_specs=[out, out, pl.BlockSpec((8, tc), lambda j, i: (0, j))],
               out_shape=[_sds((t, f), BF16), _sds((t, f), BF16), _sds((8, f), F32)],
               scratch_shapes=[pltpu.VMEM((tm + 2 * HALO, tc), F32), pltpu.VMEM((tm + HALO, tc), F32)],
               compiler_params=_params())(hv, hv, hv, hv, hv, dact, dact, w, b)


def _gnorm_fwd(y, z, w, name):
    t, di = y.shape
    gw = di // N_GROUPS
    tm = min(256, t)

    def body(y_ref, z_ref, w_ref, o_ref):
        zv = z_ref[...].astype(F32)
        yz = y_ref[...].astype(F32) * zv * _sigmoid(zv)
        r = lax.rsqrt(jnp.mean(yz * yz, axis=-1, keepdims=True) + EPS)
        o_ref[...] = (yz * r * w_ref[...]).astype(o_ref.dtype)

    blk = pl.BlockSpec((tm, gw), lambda j, i: (i, j))
    return _pc(body, name=name, grid=(N_GROUPS, t // tm),
               in_specs=[blk, blk, pl.BlockSpec((1, gw), lambda j, i: (0, j))],
               out_specs=blk, out_shape=_sds((t, di), BF16), compiler_params=_params())(y, z, w)


def _gnorm_bwd(y, z, dyn, w, name):
    t, di = y.shape
    gw = di // N_GROUPS
    tm = min(256, t)

    def body(y_ref, z_ref, g_ref, w_ref, dy_ref, dz_ref, dw_ref):
        @pl.when(pl.program_id(1) == 0)
        def _():
            dw_ref[...] = jnp.zeros_like(dw_ref)
        yv, zv, g = y_ref[...].astype(F32), z_ref[...].astype(F32), g_ref[...].astype(F32)
        s = _sigmoid(zv)
        sz = zv * s
        yz = yv * sz
        r = lax.rsqrt(jnp.mean(yz * yz, axis=-1, keepdims=True) + EPS)
        n = yz * r
        dn = g * w_ref[...]
        dw_ref[0:1, :] += jnp.sum(g * n, axis=0, keepdims=True)
        dyz = r * (dn - n * jnp.mean(dn * n, axis=-1, keepdims=True))
        dy_ref[...] = (dyz * sz).astype(BF16)
        dz_ref[...] = (dyz * yv * s * (1.0 + zv * (1.0 - s))).astype(BF16)

    blk = pl.BlockSpec((tm, gw), lambda j, i: (i, j))
    return _pc(body, name=name, grid=(N_GROUPS, t // tm),
               in_specs=[blk, blk, blk, pl.BlockSpec((1, gw), lambda j, i: (0, j))],
               out_specs=[blk, blk, pl.BlockSpec((8, gw), lambda j, i: (0, j))],
               out_shape=[_sds((t, di), BF16), _sds((t, di), BF16), _sds((8, di), F32)],
               compiler_params=_params())(y, z, dyn, w)


def _merge_fwd(gates, ya, ys, d, name):
    t = ya.shape[0]
    tm, tc = min(256, t), _pick(d, (512, 256, 128))
    nd = d // tc

    def body(ga_ref, gs_ref, ya_ref, ys_ref, o_ref):
        o_ref[...] = (_sigmoid(ga_ref[...].astype(F32)) * ya_ref[...].astype(F32)
                      + _sigmoid(gs_ref[...].astype(F32)) * ys_ref[...].astype(F32)).astype(o_ref.dtype)

    blk = pl.BlockSpec((tm, tc), lambda j, i: (i, j))
    return _pc(body, name=name, grid=(nd, t // tm),
               in_specs=[blk, pl.BlockSpec((tm, tc), lambda j, i: (i, nd + j)), blk, blk],
               out_specs=blk, out_shape=_sds((t, d), BF16), compiler_params=_params())(gates, gates, ya, ys)


def _merge_bwd(dm, gates, ya, ys, d, name):
    t = ya.shape[0]
    tm, tc = min(256, t), _pick(d, (512, 256, 128))
    nd = d // tc

    def body(dm_ref, ga_ref, gs_ref, ya_ref, ys_ref, dya_ref, dys_ref, dga_ref, dgs_ref):
        g = dm_ref[...].astype(F32)
        sa, ss = _sigmoid(ga_ref[...].astype(F32)), _sigmoid(gs_ref[...].astype(F32))
        dya_ref[...] = (g * sa).astype(BF16)
        dys_ref[...] = (g * ss).astype(BF16)
        dga_ref[...] = (g * ya_ref[...].astype(F32) * sa * (1.0 - sa)).astype(BF16)
        dgs_ref[...] = (g * ys_ref[...].astype(F32) * ss * (1.0 - ss)).astype(BF16)

    blk = pl.BlockSpec((tm, tc), lambda j, i: (i, j))
    return _pc(body, name=name, grid=(nd, t // tm),
               in_specs=[blk, blk, pl.BlockSpec((tm, tc), lambda j, i: (i, nd + j)), blk, blk],
               out_specs=[blk] * 4, out_shape=[_sds((t, d), BF16)] * 4,
               compiler_params=_params())(dm, gates, gates, ya, ys)


def _ssd_chunk_terms(dtr, dtb, alog):
    xx = dtr + dtb
    dt = jnp.maximum(xx, 0.0) + jnp.log(1.0 + jnp.exp(-jnp.abs(xx)))
    a = -jnp.exp(alog)
    li = lax.broadcasted_iota(jnp.int32, (CHUNK, CHUNK), 0)
    si = lax.broadcasted_iota(jnp.int32, (CHUNK, CHUNK), 1)
    causal = li >= si
    acum = _dot(causal.astype(F32), dt * a, 1, 0, HIGHEST)
    return xx, dt, a, acum, acum.T, causal


def _split3(x):
    hi = x.astype(BF16)
    r = x - hi.astype(F32)
    mid = r.astype(BF16)
    lo = (r - mid.astype(F32)).astype(BF16)
    return hi, mid, lo


def _expand(v, e):
    hi, mid, lo = _split3(v)
    return _dot(hi, e, 1, 0) + _dot(mid, e, 1, 0) + _dot(lo, e, 1, 0)


def _segsum(s, e):
    hi, mid, lo = _split3(s)
    return _dot(hi, e, 1, 1) + _dot(mid, e, 1, 1) + _dot(lo, e, 1, 1)


def _head_maps(di):
    nh = di // HEAD_DIM
    h = jnp.arange(DT_LANES)[:, None]
    e64 = (jnp.arange(di)[None, :] // HEAD_DIM == h).astype(BF16)
    e128 = (jnp.arange(nh * CHUNK)[None, :] // CHUNK == h).astype(BF16)
    return e64, e128


def _pair_blockdiag(p, left):
    zero = jnp.zeros_like(p)
    return jnp.concatenate([jnp.where(left, p, zero), jnp.where(left, zero, p)], axis=0)


def _ssd_fwd(xc, dtr, dtb, alog, dskx, di, name):
    t = xc.shape[0]
    dx = xc.shape[1]
    nc = t // CHUNK
    nh = di // HEAD_DIM
    hpg = nh // N_GROUPS
    gw = hpg * HEAD_DIM
    boff, coff = di, di + N_GROUPS * D_STATE
    e64, e128 = _head_maps(di)

    def body(xc_ref, dtr_ref, dtb_ref, alog_ref, dsk_ref, e64_ref, e128_ref, y_ref, st_ref, state):
        @pl.when(pl.program_id(0) == 0)
        def _():
            state[...] = jnp.zeros_like(state)
        _, dt, _, acum, acum_t, causal = _ssd_chunk_terms(dtr_ref[...], dtb_ref[...], alog_ref[...])
        last = acum[CHUNK - 1:CHUNK, :]
        e64v = e64_ref[...]
        dtx = _expand(dt, e64v)
        eax = _expand(jnp.exp(acum), e64v)
        dex = _expand(dt * jnp.exp(last - acum), e64v)
        acx = _expand(acum, e128_ref[...])
        st_ref[0] = state[...]
        left = lax.broadcasted_iota(jnp.int32, (CHUNK, 2 * HEAD_DIM), 1) < HEAD_DIM
        for g in range(N_GROUPS):
            gs = slice(g * gw, (g + 1) * gw)
            bg = xc_ref[:, boff + g * D_STATE:boff + (g + 1) * D_STATE]
            cg = xc_ref[:, coff + g * D_STATE:coff + (g + 1) * D_STATE]
            gm = _dot(cg, bg, 1, 1)
            xg = xc_ref[:, gs].astype(F32)
            xdb = (xg * dtx[:, gs]).astype(BF16)
            sin = state[:, gs]
            yo = _dot(cg, sin.astype(BF16), 1, 0) * eax[:, gs]
            for jp in range(hpg // 2):
                h0 = g * hpg + 2 * jp
                ps = slice(jp * 2 * HEAD_DIM, (jp + 1) * 2 * HEAD_DIM)
                ms = []
                for hh in (h0, h0 + 1):
                    seg = acx[:, hh * CHUNK:(hh + 1) * CHUNK] - acum_t[hh:hh + 1, :]
                    ms.append((gm * jnp.exp(jnp.where(causal, seg, -1e30))).astype(BF16))
                yd = _dot(jnp.concatenate(ms, axis=1), _pair_blockdiag(xdb[:, ps], left), 1, 0)
                col = slice(g * gw + jp * 2 * HEAD_DIM, g * gw + (jp + 1) * 2 * HEAD_DIM)
                y_ref[:, col] = (yd + yo[:, ps] + dsk_ref[:, col] * xg[:, ps]).astype(y_ref.dtype)
            xe = (xg * dex[:, gs]).astype(BF16)
            state[:, gs] = eax[CHUNK - 1:CHUNK, gs] * sin + _dot(bg, xe, 0, 0)

    small = pl.BlockSpec((1, DT_LANES), lambda c: (0, 0))
    whole = lambda a: pl.BlockSpec(a.shape, lambda c: (0, 0))
    return _pc(body, name=name, grid=(nc,),
               in_specs=[pl.BlockSpec((CHUNK, dx), lambda c: (c, 0)),
                         pl.BlockSpec((CHUNK, DT_LANES), lambda c: (c, 0)), small, small,
                         whole(dskx), whole(e64), whole(e128)],
               out_specs=[pl.BlockSpec((CHUNK, di), lambda c: (c, 0)),
                          pl.BlockSpec((1, D_STATE, di), lambda c: (c, 0, 0))],
               out_shape=[_sds((t, di), BF16), _sds((nc, D_STATE, di), F32)],
               scratch_shapes=[pltpu.VMEM((D_STATE, di), F32)],
               compiler_params=_params())(xc, dtr, dtb, alog, dskx, e64, e128)


def _ssd_bwd(xc, dtr, dy, states, dtb, alog, dskx, di, name):
    t = xc.shape[0]
    dx = xc.shape[1]
    nc = t // CHUNK
    nh = di // HEAD_DIM
    hpg = nh // N_GROUPS
    gw = hpg * HEAD_DIM
    boff, coff = di, di + N_GROUPS * D_STATE
    e64, e128 = _head_maps(di)

    def body(xc_ref, dtr_ref, dy_ref, st_ref, dtb_ref, alog_ref, dsk_ref, e64_ref, e128_ref,
             dxc_ref, ddtr_ref, sm_ref, dstate, darow):
        @pl.when(pl.program_id(0) == 0)
        def _():
            dstate[...] = jnp.zeros_like(dstate)
            sm_ref[...] = jnp.zeros_like(sm_ref)
        darow[...] = jnp.zeros_like(darow)
        xx, dt, a, acum, acum_t, causal = _ssd_chunk_terms(dtr_ref[...], dtb_ref[...], alog_ref[...])
        last = acum[CHUNK - 1:CHUNK, :]
        e64v = e64_ref[...]
        dtx = _expand(dt, e64v)
        eax = _expand(jnp.exp(acum), e64v)
        eex = _expand(jnp.exp(last - acum), e64v)
        acx = _expand(acum, e128_ref[...])
        left = lax.broadcasted_iota(jnp.int32, (CHUNK, 2 * HEAD_DIM), 1) < HEAD_DIM
        lane = lax.broadcasted_iota(jnp.int32, (CHUNK, DT_LANES), 1)
        sub8 = lax.broadcasted_iota(jnp.int32, (8, gw), 0)
        da_col = jnp.zeros((CHUNK, DT_LANES), F32)
        ddt_col = jnp.zeros((CHUNK, DT_LANES), F32)
        rows = jnp.zeros((8, DT_LANES), F32)
        for g in range(N_GROUPS):
            gs = slice(g * gw, (g + 1) * gw)
            bg = xc_ref[:, boff + g * D_STATE:boff + (g + 1) * D_STATE]
            cg = xc_ref[:, coff + g * D_STATE:coff + (g + 1) * D_STATE]
            gm = _dot(cg, bg, 1, 1)
            e64g = e64v[:, gs]
            xg = xc_ref[:, gs].astype(F32)
            dtg, eag, eeg = dtx[:, gs], eax[:, gs], eex[:, gs]
            xd = xg * dtg
            xdb = xd.astype(BF16)
            dyb = dy_ref[:, gs]
            dyf = dyb.astype(F32)
            sin = st_ref[0, :, gs]
            sinb = sin.astype(BF16)
            ds = dstate[:, gs]
            dsb = ds.astype(BF16)
            bds = _dot(bg, dsb, 1, 0)
            dyeb = (dyf * eag).astype(BF16)
            dcg = _dot(dyeb, sinb, 1, 1)
            dstate[:, gs] = eag[CHUNK - 1:CHUNK, :] * ds + _dot(cg, dyeb, 0, 0)
            yo = _dot(cg, sinb, 1, 0) * eag
            xe = xd * eeg
            dbg = _dot(xe.astype(BF16), dsb, 1, 1)
            wterm = bds * xe
            da_col = da_col + _segsum(dyf * yo - wterm, e64g)
            dg = jnp.zeros((CHUNK, CHUNK), F32)
            dxd_parts = []
            for jp in range(hpg // 2):
                h0 = g * hpg + 2 * jp
                ps = slice(jp * 2 * HEAD_DIM, (jp + 1) * 2 * HEAD_DIM)
                lms, mfs = [], []
                for hh in (h0, h0 + 1):
                    seg = acx[:, hh * CHUNK:(hh + 1) * CHUNK] - acum_t[hh:hh + 1, :]
                    lm = jnp.exp(jnp.where(causal, seg, -1e30))
                    lms.append(lm)
                    mfs.append(gm * lm)
                mstack = jnp.concatenate([m.astype(BF16) for m in mfs], axis=0)
                dyp = dyb[:, ps]
                dxd_parts.append(_dot(mstack, _pair_blockdiag(dyp, left), 0, 0))
                dm2 = _dot(dyp, _pair_blockdiag(xdb[:, ps], left), 1, 1)
                for k, hh in enumerate((h0, h0 + 1)):
                    dm = dm2[:, k * CHUNK:(k + 1) * CHUNK]
                    dg = dg + dm * lms[k]
                    q = dm * mfs[k]
                    da_col = da_col + jnp.where(lane == hh, jnp.sum(q, axis=1, keepdims=True), 0.0)
                    darow[hh:hh + 1, :] = -jnp.sum(q, axis=0, keepdims=True)
            dxd = jnp.concatenate(dxd_parts, axis=1) + bds * eeg
            ddt_col = ddt_col + _segsum(dxd * xg, e64g)
            rsum = (jnp.where(sub8 == 0, jnp.sum(wterm, axis=0, keepdims=True), 0.0)
                    + jnp.where(sub8 == 1, jnp.sum(ds * sin, axis=0, keepdims=True), 0.0)
                    + jnp.where(sub8 == 2, jnp.sum(dyf * xg, axis=0, keepdims=True), 0.0))
            rows = rows + _segsum(rsum, e64g)
            dxc_ref[:, gs] = (dxd * dtg + dsk_ref[:, gs] * dyf).astype(dxc_ref.dtype)
            dgb = dg.astype(BF16)
            dxc_ref[:, boff + g * D_STATE:boff + (g + 1) * D_STATE] = (
                dbg + _dot(dgb, cg, 0, 0)).astype(dxc_ref.dtype)
            dxc_ref[:, coff + g * D_STATE:coff + (g + 1) * D_STATE] = (
                dcg + _dot(dgb, bg, 1, 0)).astype(dxc_ref.dtype)
        at_last = rows[0:1, :] + jnp.exp(last) * rows[1:2, :]
        is_last = lax.broadcasted_iota(jnp.int32, (CHUNK, DT_LANES), 0) == CHUNK - 1
        da = da_col + jnp.where(is_last, at_last, 0.0) + darow[...].T
        li = lax.broadcasted_iota(jnp.int32, (CHUNK, CHUNK), 0)
        si = lax.broadcasted_iota(jnp.int32, (CHUNK, CHUNK), 1)
        dla = _dot((si >= li).astype(F32), da, 1, 0, HIGHEST)
        ddtr = (ddt_col + dla * a) * _sigmoid(xx)
        ddtr_ref[...] = ddtr
        sm_ref[0:1, :] += jnp.sum(ddtr, axis=0, keepdims=True)
        sm_ref[1:2, :] += jnp.sum(dla * dt, axis=0, keepdims=True) * a
        sm_ref[2:3, :] += rows[2:3, :]

    small = pl.BlockSpec((1, DT_LANES), lambda c: (0, 0))
    whole = lambda a: pl.BlockSpec(a.shape, lambda c: (0, 0))
    rev = lambda c: (nc - 1 - c, 0)
    return _pc(body, name=name, grid=(nc,),
               in_specs=[pl.BlockSpec((CHUNK, dx), rev), pl.BlockSpec((CHUNK, DT_LANES), rev),
                         pl.BlockSpec((CHUNK, di), rev),
                         pl.BlockSpec((1, D_STATE, di), lambda c: (nc - 1 - c, 0, 0)), small, small,
                         whole(dskx), whole(e64), whole(e128)],
               out_specs=[pl.BlockSpec((CHUNK, dx), rev), pl.BlockSpec((CHUNK, DT_LANES), rev),
                          pl.BlockSpec((8, DT_LANES), lambda c: (0, 0))],
               out_shape=[_sds((t, dx), BF16), _sds((t, DT_LANES), F32), _sds((8, DT_LANES), F32)],
               scratch_shapes=[pltpu.VMEM((D_STATE, di), F32), pltpu.VMEM((DT_LANES, CHUNK), F32)],
               compiler_params=_params())(xc, dtr, dy, states, dtb, alog, dskx, e64, e128)


def _adamw(parts, w, m, v, name):
    npart, rows, width = parts.shape
    tr = _pick(rows, (64, 32, 16, 8))
    c1 = 1.0 - ADAM_B1 ** ADAM_STEP
    c2 = 1.0 - ADAM_B2 ** ADAM_STEP

    def body(p_ref, w_ref, m_ref, v_ref, g_ref, d_ref, nm_ref, nv_ref):
        g = p_ref[0].astype(F32)
        for p in range(1, npart):
            g = g + p_ref[p].astype(F32)
        nm = ADAM_B1 * m_ref[...] + (1.0 - ADAM_B1) * g
        nv = ADAM_B2 * v_ref[...] + (1.0 - ADAM_B2) * (g * g)
        g_ref[...] = g
        nm_ref[...] = nm
        nv_ref[...] = nv
        d_ref[...] = -ADAM_LR * ((nm / c1) / (jnp.sqrt(nv / c2) + ADAM_EPS) + ADAM_WD * w_ref[...])

    blk = pl.BlockSpec((tr, width), lambda i: (i, 0))
    return _pc(body, name=name, grid=(rows // tr,),
               in_specs=[pl.BlockSpec((npart, tr, width), lambda i: (0, i, 0)), blk, blk, blk],
               out_specs=[blk] * 4, out_shape=[_sds((rows, width), F32)] * 4,
               compiler_params=_params())(parts, w, m, v)


def _sum_parts(parts, name, tile=None):
    npart, rows, width = parts.shape
    tile = rows if tile is None else tile

    def body(p_ref, o_ref):
        g = p_ref[0].astype(F32)
        for p in range(1, npart):
            g = g + p_ref[p].astype(F32)
        o_ref[...] = g

    return _pc(body, name=name, grid=(rows // tile,),
               in_specs=[pl.BlockSpec((npart, tile, width), lambda i: (0, i, 0))],
               out_specs=pl.BlockSpec((tile, width), lambda i: (i, 0)),
               out_shape=_sds((rows, width), F32), compiler_params=_params())(parts)


def _peers():
    x, y, c = lax.axis_index("x"), lax.axis_index("y"), lax.axis_index("c")
    out = []
    for k in range(1, N_DEV):
        px = 1 - x if k & 4 else x
        py = 1 - y if k & 2 else y
        pc = 1 - c if k & 1 else c
        out.append(((px, py, pc), 4 * px + 2 * py + pc))
    return 4 * x + 2 * y + c, out


def _exchange(big, small, scatter, name):
    r, w = big.shape[-2:]
    rs = small.shape[0]

    def body(b_ref, s_ref, ob_ref, os_ref, send_sems, recv_sems, local_sems):
        me, peers = _peers()
        mine = b_ref.at[me] if scatter else b_ref
        local = [pltpu.make_async_copy(mine, ob_ref.at[me], local_sems.at[0]),
                 pltpu.make_async_copy(s_ref, os_ref.at[me], local_sems.at[1])]
        for cp in local:
            cp.start()
        sends, recvs = [], []
        for k, (peer, pidx) in enumerate(peers):
            src = b_ref.at[pidx] if scatter else b_ref
            sends.append(pltpu.make_async_remote_copy(src_ref=src, dst_ref=ob_ref.at[me],
                                                      send_sem=send_sems.at[2 * k], recv_sem=recv_sems.at[2 * k],
                                                      device_id=peer, device_id_type=MESH))
            sends.append(pltpu.make_async_remote_copy(src_ref=s_ref, dst_ref=os_ref.at[me],
                                                      send_sem=send_sems.at[2 * k + 1], recv_sem=recv_sems.at[2 * k + 1],
                                                      device_id=peer, device_id_type=MESH))
            recvs.append(pltpu.make_async_remote_copy(src_ref=src, dst_ref=ob_ref.at[pidx],
                                                      send_sem=send_sems.at[2 * k], recv_sem=recv_sems.at[2 * k],
                                                      device_id=peer, device_id_type=MESH))
            recvs.append(pltpu.make_async_remote_copy(src_ref=s_ref, dst_ref=os_ref.at[pidx],
                                                      send_sem=send_sems.at[2 * k + 1], recv_sem=recv_sems.at[2 * k + 1],
                                                      device_id=peer, device_id_type=MESH))
        for cp in sends:
            cp.start()
        for cp in recvs:
            cp.wait_recv()
        for cp in sends:
            cp.wait_send()
        for cp in local:
            cp.wait()

    hbm = pl.BlockSpec(memory_space=pltpu.HBM)
    return _pc(body, name=name, in_specs=[hbm, hbm], out_specs=[hbm, hbm],
               out_shape=[_sds((N_DEV, r, w), big.dtype), _sds((N_DEV, rs, w), small.dtype)],
               scratch_shapes=[pltpu.SemaphoreType.DMA((2 * (N_DEV - 1),)), pltpu.SemaphoreType.DMA((2 * (N_DEV - 1),)),
                               pltpu.SemaphoreType.DMA((2,))],
               compiler_params=pltpu.CompilerParams(has_side_effects=True))(big, small)


def _pack(arrs, width, row_mult):
    flat = jnp.concatenate([a.reshape(-1) for a in arrs])
    n = flat.shape[0]
    rows = -(-n // (width * row_mult)) * row_mult
    return jnp.pad(flat, (0, rows * width - n)).reshape(rows, width)


def _unpack(packed, shapes, lead=None):
    out, off = [], 0
    flat = packed.reshape(-1) if lead is None else packed.reshape(lead, -1)
    for s in shapes:
        n = math.prod(s)
        if lead is None:
            out.append(flat[off:off + n].reshape(s))
        else:
            out.append(flat[:, off:off + n].reshape((lead,) + tuple(s)))
        off += n
    return out


def _blocks_to_cols(blocks):
    nb, rows, n = blocks.shape
    return blocks.transpose(1, 0, 2).reshape(rows, nb * n)


def _pad_rows(a, rows):
    return jnp.pad(a, ((0, rows - a.shape[0]), (0, 0)))


def _pad_lanes(a, lanes):
    return jnp.pad(a, ((0, 0), (0, lanes - a.shape[1])))


BIG = ("w_a_out", "w_s_out", "w_o", "w_up", "w_down", "w_in")
TRANSPOSED = ("w_up", "w_in")
ROW_ALIGN = 32
CONVS = ("conv_a_w", "ssd_conv_w", "ffn_conv_w")
REPL = ("norm_mix_w", "ssd_conv_b", "dt_bias", "a_log", "d_skip", "ssd_norm_w", "norm_ffn_w", "ffn_conv_b",
        "final_norm_w")
ORDER = ("norm_mix_w", "w_in", "conv_a_w", "w_a_out", "ssd_conv_w", "ssd_conv_b", "dt_bias", "a_log", "d_skip",
         "ssd_norm_w", "w_s_out", "w_o", "norm_ffn_w", "w_up", "ffn_conv_w", "ffn_conv_b", "w_down", "final_norm_w")


def _as_rows(name, block):
    return block[0].T if name in TRANSPOSED else block[0]


def kernel(x, norm_mix_w, w_in, conv_a_w, w_a_out, ssd_conv_w, ssd_conv_b, dt_bias, a_log, d_skip, ssd_norm_w, w_s_out, w_o, norm_ffn_w, w_up, ffn_conv_w, ffn_conv_b, w_down, final_norm_w, loss_target, m_norm_mix_w, m_w_in, m_conv_a_w, m_w_a_out, m_ssd_conv_w, m_ssd_conv_b, m_dt_bias, m_a_log, m_d_skip, m_ssd_norm_w, m_w_s_out, m_w_o, m_norm_ffn_w, m_w_up, m_ffn_conv_w, m_ffn_conv_b, m_w_down, m_final_norm_w, v_norm_mix_w, v_w_in, v_conv_a_w, v_w_a_out, v_ssd_conv_w, v_ssd_conv_b, v_dt_bias, v_a_log, v_d_skip, v_ssd_norm_w, v_w_s_out, v_w_o, v_norm_ffn_w, v_w_up, v_ffn_conv_w, v_ffn_conv_b, v_w_down, v_final_norm_w):
    wts = dict(norm_mix_w=norm_mix_w, w_in=w_in, conv_a_w=conv_a_w, w_a_out=w_a_out, ssd_conv_w=ssd_conv_w,
               ssd_conv_b=ssd_conv_b, dt_bias=dt_bias, a_log=a_log, d_skip=d_skip, ssd_norm_w=ssd_norm_w,
               w_s_out=w_s_out, w_o=w_o, norm_ffn_w=norm_ffn_w, w_up=w_up, ffn_conv_w=ffn_conv_w,
               ffn_conv_b=ffn_conv_b, w_down=w_down, final_norm_w=final_norm_w)
    mom1 = dict(norm_mix_w=m_norm_mix_w, w_in=m_w_in, conv_a_w=m_conv_a_w, w_a_out=m_w_a_out,
                ssd_conv_w=m_ssd_conv_w, ssd_conv_b=m_ssd_conv_b, dt_bias=m_dt_bias, a_log=m_a_log, d_skip=m_d_skip,
                ssd_norm_w=m_ssd_norm_w, w_s_out=m_w_s_out, w_o=m_w_o, norm_ffn_w=m_norm_ffn_w, w_up=m_w_up,
                ffn_conv_w=m_ffn_conv_w, ffn_conv_b=m_ffn_conv_b, w_down=m_w_down, final_norm_w=m_final_norm_w)
    mom2 = dict(norm_mix_w=v_norm_mix_w, w_in=v_w_in, conv_a_w=v_conv_a_w, w_a_out=v_w_a_out,
                ssd_conv_w=v_ssd_conv_w, ssd_conv_b=v_ssd_conv_b, dt_bias=v_dt_bias, a_log=v_a_log, d_skip=v_d_skip,
                ssd_norm_w=v_ssd_norm_w, w_s_out=v_w_s_out, w_o=v_w_o, norm_ffn_w=v_norm_ffn_w, w_up=v_w_up,
                ffn_conv_w=v_ffn_conv_w, ffn_conv_b=v_ffn_conv_b, w_down=v_w_down, final_norm_w=v_final_norm_w)

    t, d = x.shape[1], x.shape[2]
    di = 2 * d
    nh = di // HEAD_DIM
    dxw = di + 2 * N_GROUPS * D_STATE
    f = w_down.shape[1] * N_DEV
    n_in = w_in.shape[2] * N_DEV
    me = 4 * lax.axis_index("x") + 2 * lax.axis_index("y") + lax.axis_index("c")

    local_rows = [_as_rows(k, wts[k]) for k in BIG]
    nrows = [a.shape[0] for a in local_rows]
    offs = [sum(nrows[:i]) for i in range(len(BIG))]
    assert all(o % ROW_ALIGN == 0 for o in offs)
    rtot = -(-sum(nrows) // 64) * 64
    big_local = _pad_rows(jnp.concatenate(local_rows, axis=0), rtot).astype(BF16)
    conv_shapes = [wts[k].shape[1:] for k in CONVS]
    conv_local = _pack([wts[k] for k in CONVS], d, 8)
    big_all, conv_all = _exchange(big_local, conv_local, False, "gather_weights")
    full = {k: big_all[:, o:o + n].reshape(N_DEV * n, d) for k, o, n in zip(BIG, offs, nrows)}
    waout, wsout, wo, wup_t, wdown, win_t = (full[k] for k in BIG)
    c_a, c_s, c_f = _unpack(conv_all, conv_shapes, N_DEV)
    caw, scw, fcw = _blocks_to_cols(c_a), _blocks_to_cols(c_s), _blocks_to_cols(c_f)

    o_z, o_x, o_dt = 5 * d, 7 * d, 7 * d + dxw
    seg_bounds = [0, d, 2 * d, 3 * d, 4 * d, o_z, o_x, o_dt]
    w_dt = _pad_rows(win_t[o_dt:], DT_LANES)
    dtb, alog = (_pad_lanes(p[...].reshape(1, nh), DT_LANES) for p in (dt_bias, a_log))
    dskx = jnp.repeat(d_skip.reshape(1, nh), HEAD_DIM, axis=1)

    x2, tgt = x[0], loss_target[0]
    u = _rms_fwd(x2, norm_mix_w, "norm_mix")
    gates = _mm([(u, win_t[:2 * d])], "nt", BF16, "proj_gates")
    pa = _mm([(u, win_t[2 * d:o_z])], "nt", BF16, "proj_a")
    z = _mm([(u, win_t[o_z:o_x])], "nt", BF16, "proj_z")
    xbc = _mm([(u, win_t[o_x:o_dt])], "nt", BF16, "proj_xbc")
    dtr = _mm([(u, w_dt)], "nt", F32, "proj_dt")
    ya_in = _conv_a_fwd(pa, caw, d, "conv_a")
    y_a = _mm([(ya_in, waout)], "nn", BF16, "a_out")
    xc = _conv_s_fwd(xbc, scw, ssd_conv_b, "conv_s")
    y, states = _ssd_fwd(xc, dtr, dtb, alog, dskx, di, "ssd")
    yn = _gnorm_fwd(y, z, ssd_norm_w, "gnorm")
    y_s = _mm([(yn, wsout)], "nn", BF16, "s_out")
    merged = _merge_fwd(gates, y_a, y_s, d, "merge")
    mo = _mm([(merged, wo)], "nn", F32, "o_proj")
    h1, v = _resnorm_fwd(x2, mo, norm_ffn_w, "norm_ffn")
    hv = _mm([(v, wup_t)], "nt", BF16, "up_proj")
    act = _ffn_fwd(hv, fcw, ffn_conv_b, f, "ffn_act")
    dd = _mm([(act, wdown)], "nn", F32, "down_proj")
    loss11, dh2, dh2b, g_fnw = _final(h1, dd, tgt, final_norm_w.reshape(1, d), "final")

    dact = _mm([(dh2b, wdown)], "nt", BF16, "d_act")
    gw_down = _mm_tn(act, dh2b, "gw_down")
    dh1f, dh3, g_ffn = _ffn_bwd(hv, dact, fcw, ffn_conv_b, f, "ffn_act_bwd")
    dv = _mm([(dh1f, wup_t[:f]), (dh3, wup_t[f:])], "nn", F32, "d_v")
    gw_up_t = jnp.concatenate([_mm_tn(dh1f, v, "gw_up1"), _mm_tn(dh3, v, "gw_up3")], axis=0)
    dh1, dh1b, g_nfw = _rms_bwd(h1, dv, norm_ffn_w, dh2, "norm_ffn_bwd")
    dmerged = _mm([(dh1b, wo)], "nt", BF16, "d_merged")
    gw_o = _mm_tn(merged, dh1b, "gw_o")
    dya, dys, dga, dgs = _merge_bwd(dmerged, gates, y_a, y_s, d, "merge_bwd")
    dyain = _mm([(dya, waout)], "nt", BF16, "d_ya_in")
    gw_aout = _mm_tn(ya_in, dya, "gw_a_out")
    db, dc, dvv, g_caw = _conv_a_bwd(pa, dyain, caw, d, "conv_a_bwd")
    dyn = _mm([(dys, wsout)], "nt", BF16, "d_yn")
    gw_sout = _mm_tn(yn, dys, "gw_s_out")
    dy, dz, g_snw = _gnorm_bwd(y, z, dyn, ssd_norm_w, "gnorm_bwd")
    dxc, ddtr, g_ssd = _ssd_bwd(xc, dtr, dy, states, dtb, alog, dskx, di, "ssd_bwd")
    dxbc, g_scw = _conv_s_bwd(xbc, dxc, scw, ssd_conv_b, "conv_s_bwd")
    dsegs = [dga, dgs, db, dc, dvv, dz, dxbc]
    pairs = [(s, win_t[a:b]) for s, a, b in zip(dsegs, seg_bounds[:-1], seg_bounds[1:])]
    pairs.append((ddtr.astype(BF16), w_dt))
    du = _mm(pairs, "nn", F32, "d_u", tm=512, tn=256)
    gw_in_t = jnp.concatenate([_mm_tn(s, u, "gw_in%d" % i) for i, (s, _) in enumerate(pairs)], axis=0)[:n_in]
    dx, _, g_nmw = _rms_bwd(x2, du, norm_mix_w, dh1, "norm_mix_bwd")

    grads_full = dict(w_a_out=gw_aout, w_s_out=gw_sout, w_o=gw_o, w_up=gw_up_t, w_down=gw_down, w_in=gw_in_t)
    big_parts = jnp.concatenate([grads_full[k].reshape(N_DEV, n, d) for k, n in zip(BIG, nrows)], axis=1)
    big_parts = jnp.pad(big_parts, ((0, 0), (0, rtot - sum(nrows)), (0, 0))).astype(BF16)
    small_grads = dict(norm_mix_w=g_nmw[0], ssd_conv_b=g_scw[4], dt_bias=g_ssd[0, :nh], a_log=g_ssd[1, :nh],
                       d_skip=g_ssd[2, :nh], ssd_norm_w=g_snw[0], norm_ffn_w=g_nfw[0], ffn_conv_b=g_ffn[3],
                       final_norm_w=g_fnw[0], conv_a_w=g_caw[:3], ssd_conv_w=g_scw[:4], ffn_conv_w=g_ffn[:3])
    small_names = REPL + CONVS
    small_parts = _pack([small_grads[k] for k in small_names], d, 8)
    big_recv, small_all = _exchange(big_parts, small_parts, True, "exchange_grads")
    small_sum = _sum_parts(small_all, "sum_small_grads")
    small_g = dict(zip(small_names, _unpack(small_sum, [small_grads[k].shape for k in small_names])))

    big_sum = _sum_parts(big_recv, "sum_big_grads", tile=64)
    res = {}

    def update(k, g):
        outs = _adamw(g.reshape(1, -1, g.shape[-1]), *(src[k].reshape(-1, g.shape[-1]) for src in (wts, mom1, mom2)),
                      "adamw_" + k)
        for kind, a in zip(("g", "d", "m", "v"), outs):
            res[kind, k] = a.reshape(wts[k].shape)

    for k, o, n in zip(BIG, offs, nrows):
        rows = big_sum[o:o + n]
        update(k, rows.T if k in TRANSPOSED else rows)
    local_g = {}
    for k in REPL:
        local_g[k] = small_g[k].reshape(wts[k].shape)
    for k in CONVS:
        n = wts[k].shape[2]
        local_g[k] = lax.dynamic_slice_in_dim(small_g[k], me * n, n, axis=1)[None]
    w_sm, m_sm, v_sm = (_pack([src[k] for k in small_names], d, 8) for src in (wts, mom1, mom2))
    g_sm = _pack([local_g[k] for k in small_names], d, 8)
    outs_sm = _adamw(g_sm[None], w_sm, m_sm, v_sm, "adamw_small")
    for kind, packed in zip(("g", "d", "m", "v"), outs_sm):
        for k, a in zip(small_names, _unpack(packed, [wts[k].shape for k in small_names])):
            res[kind, k] = a

    loss = lax.psum(loss11[0, 0], ("x", "y", "c"))
    return (loss, dx[None], *[res["g", k] for k in ORDER], *[res["d", k] for k in ORDER],
            *[res["m", k] for k in ORDER], *[res["v", k] for k in ORDER])
```

```python
import functools
import math

import jax
import jax.numpy as jnp
from jax import lax
from jax.experimental import pallas as pl
from jax.experimental.pallas import tpu as pltpu

F32 = jnp.float32
BF16 = jnp.bfloat16
EPS = 1e-5
HEAD_DIM = 64
N_GROUPS = 4
D_STATE = 128
CHUNK = 128
DT_LANES = 128
HALO = 16
N_DEV = 8
V7X_VMEM_LIMIT = 56 * 1024 * 1024
ADAM_LR, ADAM_B1, ADAM_B2, ADAM_EPS, ADAM_WD, ADAM_STEP = 0.001, 0.9, 0.999, 1e-08, 0.01, 10
HIGHEST = lax.Precision.HIGHEST
MESH = pl.DeviceIdType.MESH


def _pc(body, **kw):
    return pl.pallas_call(body, **kw)


def _params():
    return pltpu.CompilerParams(vmem_limit_bytes=V7X_VMEM_LIMIT)


def _pick(n, cands):
    for c in cands:
        if n % c == 0:
            return c
    return n


def _dot(a, b, ca, cb, prec=None):
    return lax.dot_general(a, b, (((ca,), (cb,)), ((), ())), preferred_element_type=F32, precision=prec)


def _sigmoid(x):
    return 0.5 * jnp.tanh(0.5 * x) + 0.5


def _sds(shape, dtype):
    return jax.ShapeDtypeStruct(shape, dtype)


def _mm(pairs, mode, out_dtype, name, tm=1024, tn=1024):
    m = pairs[0][0].shape[0]
    n = pairs[0][1].shape[1] if mode == "nn" else pairs[0][1].shape[0]
    tm = min(tm, m)
    tn = _pick(n, (tn, 1408, 512, 256, 128))
    npair = len(pairs)
    cb = 0 if mode == "nn" else 1

    def body(*refs):
        o_ref = refs[2 * npair]
        acc = None
        for p in range(npair):
            part = _dot(refs[2 * p][...], refs[2 * p + 1][...], 1, cb)
            acc = part if acc is None else acc + part
        o_ref[...] = acc.astype(o_ref.dtype)

    in_specs, args = [], []
    for a, b in pairs:
        k = a.shape[1]
        in_specs.append(pl.BlockSpec((tm, k), lambda i, j: (i, 0)))
        if mode == "nn":
            in_specs.append(pl.BlockSpec((k, tn), lambda i, j: (0, j)))
        else:
            in_specs.append(pl.BlockSpec((tn, k), lambda i, j: (j, 0)))
        args += [a, b]
    return _pc(body, name=name, grid=(m // tm, n // tn), in_specs=in_specs,
               out_specs=pl.BlockSpec((tm, tn), lambda i, j: (i, j)),
               out_shape=_sds((m, n), out_dtype), compiler_params=_params())(*args)


def _mm_tn(a, b, name, tm=1024):
    m, ka = a.shape
    nb = b.shape[1]
    tm = min(tm, m)
    tk = _pick(ka, (1024, 1408, 512, 256, 128))
    tn = _pick(nb, (1024, 512, 256, 128))

    def body(a_ref, b_ref, o_ref):
        @pl.when(pl.program_id(2) == 0)
        def _():
            o_ref[...] = jnp.zeros_like(o_ref)
        o_ref[...] += _dot(a_ref[...], b_ref[...], 0, 0)

    return _pc(body, name=name, grid=(ka // tk, nb // tn, m // tm),
               in_specs=[pl.BlockSpec((tm, tk), lambda i, j, t: (t, i)),
                         pl.BlockSpec((tm, tn), lambda i, j, t: (t, j))],
               out_specs=pl.BlockSpec((tk, tn), lambda i, j, t: (i, j)),
               out_shape=_sds((ka, nb), F32), compiler_params=_params())(a, b)


def _rms_fwd(x, w, name):
    t, d = x.shape
    tm = min(512, t)

    def body(x_ref, w_ref, o_ref):
        xv = x_ref[...]
        r = lax.rsqrt(jnp.mean(xv * xv, axis=-1, keepdims=True) + EPS)
        o_ref[...] = (xv * r * w_ref[...]).astype(o_ref.dtype)

    return _pc(body, name=name, grid=(t // tm,),
               in_specs=[pl.BlockSpec((tm, d), lambda i: (i, 0)), pl.BlockSpec((1, d), lambda i: (0, 0))],
               out_specs=pl.BlockSpec((tm, d), lambda i: (i, 0)),
               out_shape=_sds((t, d), BF16), compiler_params=_params())(x, w)


def _resnorm_fwd(x, mo, w, name):
    t, d = x.shape
    tm = min(512, t)

    def body(x_ref, mo_ref, w_ref, h_ref, v_ref):
        h = x_ref[...] + mo_ref[...]
        r = lax.rsqrt(jnp.mean(h * h, axis=-1, keepdims=True) + EPS)
        h_ref[...] = h
        v_ref[...] = (h * r * w_ref[...]).astype(v_ref.dtype)

    row = pl.BlockSpec((tm, d), lambda i: (i, 0))
    return _pc(body, name=name, grid=(t // tm,),
               in_specs=[row, row, pl.BlockSpec((1, d), lambda i: (0, 0))],
               out_specs=[row, row], out_shape=[_sds((t, d), F32), _sds((t, d), BF16)],
               compiler_params=_params())(x, mo, w)


def _rms_bwd(h, dy, w, dres, name):
    t, d = h.shape
    tm = min(512, t)

    def body(h_ref, dy_ref, w_ref, dres_ref, dx_ref, dxb_ref, dw_ref):
        @pl.when(pl.program_id(0) == 0)
        def _():
            dw_ref[...] = jnp.zeros_like(dw_ref)
        hv = h_ref[...]
        dyv = dy_ref[...]
        r = lax.rsqrt(jnp.mean(hv * hv, axis=-1, keepdims=True) + EPS)
        n = hv * r
        dn = dyv * w_ref[...]
        dw_ref[0:1, :] += jnp.sum(dyv * n, axis=0, keepdims=True)
        dx = dres_ref[...] + r * (dn - n * jnp.mean(dn * n, axis=-1, keepdims=True))
        dx_ref[...] = dx
        dxb_ref[...] = dx.astype(BF16)

    row = pl.BlockSpec((tm, d), lambda i: (i, 0))
    return _pc(body, name=name, grid=(t // tm,),
               in_specs=[row, row, pl.BlockSpec((1, d), lambda i: (0, 0)), row],
               out_specs=[row, row, pl.BlockSpec((8, d), lambda i: (0, 0))],
               out_shape=[_sds((t, d), F32), _sds((t, d), BF16), _sds((8, d), F32)],
               compiler_params=_params())(h, dy, w, dres)


def _final(h1, dd, tgt, w, name):
    t, d = h1.shape
    tm = min(512, t)
    nt = t // tm

    def body(h1_ref, dd_ref, tgt_ref, w_ref, loss_ref, dh_ref, dhb_ref, dw_ref, acc):
        i = pl.program_id(0)

        @pl.when(i == 0)
        def _():
            dw_ref[...] = jnp.zeros_like(dw_ref)
            acc[...] = jnp.zeros_like(acc)
        h = h1_ref[...] + dd_ref[...]
        r = lax.rsqrt(jnp.mean(h * h, axis=-1, keepdims=True) + EPS)
        n = h * r
        wv = w_ref[...]
        e = n * wv - tgt_ref[...]
        acc[...] += jnp.sum(e * e, axis=0, keepdims=True)
        dout = e * (1.0 / d)
        dn = dout * wv
        dw_ref[0:1, :] += jnp.sum(dout * n, axis=0, keepdims=True)
        dh = r * (dn - n * jnp.mean(dn * n, axis=-1, keepdims=True))
        dh_ref[...] = dh
        dhb_ref[...] = dh.astype(BF16)

        @pl.when(i == nt - 1)
        def _():
            loss_ref[...] = jnp.sum(acc[...], axis=-1, keepdims=True) * (0.5 / d)

    row = pl.BlockSpec((tm, d), lambda i: (i, 0))
    return _pc(body, name=name, grid=(nt,),
               in_specs=[row, row, row, pl.BlockSpec((1, d), lambda i: (0, 0))],
               out_specs=[pl.BlockSpec((1, 1), lambda i: (0, 0)), row, row, pl.BlockSpec((8, d), lambda i: (0, 0))],
               out_shape=[_sds((1, 1), F32), _sds((t, d), F32), _sds((t, d), BF16), _sds((8, d), F32)],
               scratch_shapes=[pltpu.VMEM((1, d), F32)], compiler_params=_params())(h1, dd, tgt, w)


def _fill(buf, prev, cur, nxt, tm):
    if prev is not None:
        buf[0:HALO, :] = prev
    buf[HALO:HALO + tm, :] = cur
    if nxt is not None:
        buf[HALO + tm:HALO + tm + HALO, :] = nxt


def _taps(xbuf, k, rows):
    return [xbuf[pl.ds(HALO - (k - 1) + j, rows), :] for j in range(k)]


def _conv_rows(taps, w):
    out = None
    for j, xs in enumerate(taps):
        term = w[j:j + 1, :] * xs
        out = term if out is None else out + term
    return out


def _conv_transpose(dbuf, w, k, tm):
    out = None
    for j in range(k):
        term = w[j:j + 1, :] * dbuf[pl.ds(k - 1 - j, tm), :]
        out = term if out is None else out + term
    return out


def _acc_conv_grads(acc_ref, dcur, taps, tm, with_bias):
    for j, xs in enumerate(taps):
        acc_ref[j:j + 1, :] += jnp.sum(dcur * xs[0:tm, :], axis=0, keepdims=True)
    if with_bias:
        acc_ref[len(taps):len(taps) + 1, :] += jnp.sum(dcur, axis=0, keepdims=True)


def _tile_specs(t, tm, tc, col0):
    th = tm // HALO
    last = t // HALO - 1
    cur = pl.BlockSpec((tm, tc), lambda j, i: (i, col0 + j))
    prev = pl.BlockSpec((HALO, tc), lambda j, i: (jnp.maximum(i * th - 1, 0), col0 + j))
    nxt = pl.BlockSpec((HALO, tc), lambda j, i: (jnp.minimum((i + 1) * th, last), col0 + j))
    return cur, prev, nxt


def _conv_a_fwd(pa, w, d, name):
    t = pa.shape[0]
    tm, tc = min(512, t), _pick(d, (512, 256, 128))
    nd = d // tc

    def body(b_ref, c_ref, v_ref, cp_ref, vp_ref, w_ref, o_ref, buf):
        i = pl.program_id(1)
        keep = (i > 0).astype(F32)
        prev = cp_ref[...].astype(F32) * vp_ref[...].astype(F32) * keep
        _fill(buf, prev, c_ref[...].astype(F32) * v_ref[...].astype(F32), None, tm)
        q = _conv_rows(_taps(buf, 3, tm), w_ref[...])
        o_ref[...] = (b_ref[...].astype(F32) * q).astype(o_ref.dtype)

    b_cur, _, _ = _tile_specs(t, tm, tc, 0)
    c_cur, c_prev, _ = _tile_specs(t, tm, tc, nd)
    v_cur, v_prev, _ = _tile_specs(t, tm, tc, 2 * nd)
    return _pc(body, name=name, grid=(nd, t // tm),
               in_specs=[b_cur, c_cur, v_cur, c_prev, v_prev, pl.BlockSpec((3, tc), lambda j, i: (0, j))],
               out_specs=pl.BlockSpec((tm, tc), lambda j, i: (i, j)),
               out_shape=_sds((t, d), BF16),
               scratch_shapes=[pltpu.VMEM((tm + 2 * HALO, tc), F32)],
               compiler_params=_params())(pa, pa, pa, pa, pa, w)


def _conv_a_bwd(pa, dya, w, d, name):
    t = pa.shape[0]
    tm, tc = min(512, t), _pick(d, (512, 256, 128))
    nd, nt = d // tc, t // tm

    def body(b_ref, c_ref, v_ref, cp_ref, vp_ref, bn_ref, g_ref, gn_ref, w_ref,
             db_ref, dc_ref, dv_ref, acc_ref, pbuf, dbuf):
        i = pl.program_id(1)

        @pl.when(i == 0)
        def _():
            acc_ref[...] = jnp.zeros_like(acc_ref)
        wv = w_ref[...]
        cv, vv = c_ref[...].astype(F32), v_ref[...].astype(F32)
        prev = cp_ref[...].astype(F32) * vp_ref[...].astype(F32) * (i > 0).astype(F32)
        _fill(pbuf, prev, cv * vv, None, tm)
        g = g_ref[...].astype(F32)
        dq = g * b_ref[...].astype(F32)
        dqn = gn_ref[...].astype(F32) * bn_ref[...].astype(F32) * (i < nt - 1).astype(F32)
        dbuf[0:tm, :] = dq
        dbuf[tm:tm + HALO, :] = dqn
        taps = _taps(pbuf, 3, tm)
        q = _conv_rows(taps, wv)
        db_ref[...] = (g * q).astype(BF16)
        dp = _conv_transpose(dbuf, wv, 3, tm)
        dc_ref[...] = (dp * vv).astype(BF16)
        dv_ref[...] = (dp * cv).astype(BF16)
        _acc_conv_grads(acc_ref, dq, taps, tm, False)

    b_cur, _, b_next = _tile_specs(t, tm, tc, 0)
    c_cur, c_prev, _ = _tile_specs(t, tm, tc, nd)
    v_cur, v_prev, _ = _tile_specs(t, tm, tc, 2 * nd)
    g_cur, _, g_next = _tile_specs(t, tm, tc, 0)
    out = pl.BlockSpec((tm, tc), lambda j, i: (i, j))
    return _pc(body, name=name, grid=(nd, nt),
               in_specs=[b_cur, c_cur, v_cur, c_prev, v_prev, b_next, g_cur, g_next,
                         pl.BlockSpec((3, tc), lambda j, i: (0, j))],
               out_specs=[out, out, out, pl.BlockSpec((8, tc), lambda j, i: (0, j))],
               out_shape=[_sds((t, d), BF16)] * 3 + [_sds((8, d), F32)],
               scratch_shapes=[pltpu.VMEM((tm + 2 * HALO, tc), F32), pltpu.VMEM((tm + HALO, tc), F32)],
               compiler_params=_params())(pa, pa, pa, pa, pa, pa, dya, dya, w)


def _conv_s_fwd(xbc, w, b, name):
    t, dx = xbc.shape
    tm, tc = min(512, t), _pick(dx, (512, 256, 128))

    def body(x_ref, xp_ref, w_ref, b_ref, o_ref, buf):
        i = pl.program_id(1)
        _fill(buf, xp_ref[...].astype(F32) * (i > 0).astype(F32), x_ref[...].astype(F32), None, tm)
        pre = _conv_rows(_taps(buf, 4, tm), w_ref[...]) + b_ref[...]
        o_ref[...] = (pre * _sigmoid(pre)).astype(o_ref.dtype)

    cur, prev, _ = _tile_specs(t, tm, tc, 0)
    return _pc(body, name=name, grid=(dx // tc, t // tm),
               in_specs=[cur, prev, pl.BlockSpec((4, tc), lambda j, i: (0, j)),
                         pl.BlockSpec((1, tc), lambda j, i: (0, j))],
               out_specs=pl.BlockSpec((tm, tc), lambda j, i: (i, j)),
               out_shape=_sds((t, dx), BF16),
               scratch_shapes=[pltpu.VMEM((tm + 2 * HALO, tc), F32)],
               compiler_params=_params())(xbc, xbc, w, b)


def _dsilu(pre):
    s = _sigmoid(pre)
    return s * (1.0 + pre * (1.0 - s))


def _conv_s_bwd(xbc, dxc, w, b, name):
    t, dx = xbc.shape
    tm, tc = min(512, t), _pick(dx, (512, 256, 128))
    nt = t // tm

    def body(x_ref, xp_ref, xn_ref, g_ref, gn_ref, w_ref, b_ref, dx_ref, acc_ref, xbuf, dbuf):
        i = pl.program_id(1)

        @pl.when(i == 0)
        def _():
            acc_ref[...] = jnp.zeros_like(acc_ref)
        wv = w_ref[...]
        _fill(xbuf, xp_ref[...].astype(F32) * (i > 0).astype(F32), x_ref[...].astype(F32),
              xn_ref[...].astype(F32), tm)
        taps = _taps(xbuf, 4, tm + HALO)
        pre = _conv_rows(taps, wv) + b_ref[...]
        ds = _dsilu(pre)
        dcur = g_ref[...].astype(F32) * ds[0:tm, :]
        dnxt = gn_ref[...].astype(F32) * ds[tm:tm + HALO, :] * (i < nt - 1).astype(F32)
        dbuf[0:tm, :] = dcur
        dbuf[tm:tm + HALO, :] = dnxt
        dx_ref[...] = _conv_transpose(dbuf, wv, 4, tm).astype(BF16)
        _acc_conv_grads(acc_ref, dcur, taps, tm, True)

    cur, prev, nxt = _tile_specs(t, tm, tc, 0)
    return _pc(body, name=name, grid=(dx // tc, nt),
               in_specs=[cur, prev, nxt, cur, nxt, pl.BlockSpec((4, tc), lambda j, i: (0, j)),
                         pl.BlockSpec((1, tc), lambda j, i: (0, j))],
               out_specs=[pl.BlockSpec((tm, tc), lambda j, i: (i, j)), pl.BlockSpec((8, tc), lambda j, i: (0, j))],
               out_shape=[_sds((t, dx), BF16), _sds((8, dx), F32)],
               scratch_shapes=[pltpu.VMEM((tm + 2 * HALO, tc), F32), pltpu.VMEM((tm + HALO, tc), F32)],
               compiler_params=_params())(xbc, xbc, xbc, dxc, dxc, w, b)


def _ffn_fwd(hv, w, b, f, name):
    t = hv.shape[0]
    tm, tc = min(512, t), _pick(f, (512, 256, 128))
    nf = f // tc

    def body(h1_ref, h1p_ref, h3_ref, w_ref, b_ref, o_ref, buf):
        i = pl.program_id(1)
        _fill(buf, h1p_ref[...].astype(F32) * (i > 0).astype(F32), h1_ref[...].astype(F32), None, tm)
        c1 = _conv_rows(_taps(buf, 3, tm), w_ref[...]) + b_ref[...]
        o_ref[...] = (c1 * _sigmoid(c1) * h3_ref[...].astype(F32)).astype(o_ref.dtype)

    h1_cur, h1_prev, _ = _tile_specs(t, tm, tc, 0)
    h3_cur, _, _ = _tile_specs(t, tm, tc, nf)
    return _pc(body, name=name, grid=(nf, t // tm),
               in_specs=[h1_cur, h1_prev, h3_cur, pl.BlockSpec((3, tc), lambda j, i: (0, j)),
                         pl.BlockSpec((1, tc), lambda j, i: (0, j))],
               out_specs=pl.BlockSpec((tm, tc), lambda j, i: (i, j)),
               out_shape=_sds((t, f), BF16),
               scratch_shapes=[pltpu.VMEM((tm + 2 * HALO, tc), F32)],
               compiler_params=_params())(hv, hv, hv, w, b)


def _ffn_bwd(hv, dact, w, b, f, name):
    t = hv.shape[0]
    tm, tc = min(512, t), _pick(f, (512, 256, 128))
    nf, nt = f // tc, t // tm

    def body(h1_ref, h1p_ref, h1n_ref, h3_ref, h3n_ref, g_ref, gn_ref, w_ref, b_ref,
             dh1_ref, dh3_ref, acc_ref, xbuf, dbuf):
        i = pl.program_id(1)

        @pl.when(i == 0)
        def _():
            acc_ref[...] = jnp.zeros_like(acc_ref)
        wv = w_ref[...]
        _fill(xbuf, h1p_ref[...].astype(F32) * (i > 0).astype(F32), h1_ref[...].astype(F32),
              h1n_ref[...].astype(F32), tm)
        taps = _taps(xbuf, 3, tm + HALO)
        c1 = _conv_rows(taps, wv) + b_ref[...]
        s = _sigmoid(c1)
        ds = s * (1.0 + c1 * (1.0 - s))
        g = g_ref[...].astype(F32)
        dh3_ref[...] = (g * c1[0:tm, :] * s[0:tm, :]).astype(BF16)
        dcur = g * h3_ref[...].astype(F32) * ds[0:tm, :]
        dnxt = gn_ref[...].astype(F32) * h3n_ref[...].astype(F32) * ds[tm:tm + HALO, :] * (i < nt - 1).astype(F32)
        dbuf[0:tm, :] = dcur
        dbuf[tm:tm + HALO, :] = dnxt
        dh1_ref[...] = _conv_transpose(dbuf, wv, 3, tm).astype(BF16)
        _acc_conv_grads(acc_ref, dcur, taps, tm, True)

    h1_cur, h1_prev, h1_next = _tile_specs(t, tm, tc, 0)
    h3_cur, _, h3_next = _tile_specs(t, tm, tc, nf)
    g_cur, _, g_next = _tile_specs(t, tm, tc, 0)
    out = pl.BlockSpec((tm, tc), lambda j, i: (i, j))
    return _pc(body, name=name, grid=(nf, nt),
               in_specs=[h1_cur, h1_prev, h1_next, h3_cur, h3_next, g_cur, g_next,
                         pl.BlockSpec((3, tc), lambda j, i: (0, j)), pl.BlockSpec((1, tc), lambda j, i: (0, j))],
               out_specs=[out, out, pl.BlockSpec((8, tc), lambda j, i: (0, j))],
               out_shape=[_sds((t, f), BF16), _sds((t, f), BF16), _sds((8, f), F32)],
               scratch_shapes=[pltpu.VMEM((tm + 2 * HALO, tc), F32), pltpu.VMEM((tm + HALO, tc), F32)],
               compiler_params=_params())(hv, hv, hv, hv, hv, dact, dact, w, b)


def _gnorm_fwd(y, z, w, name):
    t, di = y.shape
    gw = di // N_GROUPS
    tm = min(512, t)

    def body(y_ref, z_ref, w_ref, o_ref):
        zv = z_ref[...].astype(F32)
        yz = y_ref[...].astype(F32) * zv * _sigmoid(zv)
        r = lax.rsqrt(jnp.mean(yz * yz, axis=-1, keepdims=True) + EPS)
        o_ref[...] = (yz * r * w_ref[...]).astype(o_ref.dtype)

    blk = pl.BlockSpec((tm, gw), lambda j, i: (i, j))
    return _pc(body, name=name, grid=(N_GROUPS, t // tm),
               in_specs=[blk, blk, pl.BlockSpec((1, gw), lambda j, i: (0, j))],
               out_specs=blk, out_shape=_sds((t, di), BF16), compiler_params=_params())(y, z, w)


def _gnorm_bwd(y, z, dyn, w, name):
    t, di = y.shape
    gw = di // N_GROUPS
    tm = min(512, t)

    def body(y_ref, z_ref, g_ref, w_ref, dy_ref, dz_ref, dw_ref):
        @pl.when(pl.program_id(1) == 0)
        def _():
            dw_ref[...] = jnp.zeros_like(dw_ref)
        yv, zv, g = y_ref[...].astype(F32), z_ref[...].astype(F32), g_ref[...].astype(F32)
        s = _sigmoid(zv)
        sz = zv * s
        yz = yv * sz
        r = lax.rsqrt(jnp.mean(yz * yz, axis=-1, keepdims=True) + EPS)
        n = yz * r
        dn = g * w_ref[...]
        dw_ref[0:1, :] += jnp.sum(g * n, axis=0, keepdims=True)
        dyz = r * (dn - n * jnp.mean(dn * n, axis=-1, keepdims=True))
        dy_ref[...] = (dyz * sz).astype(BF16)
        dz_ref[...] = (dyz * yv * s * (1.0 + zv * (1.0 - s))).astype(BF16)

    blk = pl.BlockSpec((tm, gw), lambda j, i: (i, j))
    return _pc(body, name=name, grid=(N_GROUPS, t // tm),
               in_specs=[blk, blk, blk, pl.BlockSpec((1, gw), lambda j, i: (0, j))],
               out_specs=[blk, blk, pl.BlockSpec((8, gw), lambda j, i: (0, j))],
               out_shape=[_sds((t, di), BF16), _sds((t, di), BF16), _sds((8, di), F32)],
               compiler_params=_params())(y, z, dyn, w)


def _merge_fwd(gates, ya, ys, d, name):
    t = ya.shape[0]
    tm, tc = min(512, t), _pick(d, (512, 256, 128))
    nd = d // tc

    def body(ga_ref, gs_ref, ya_ref, ys_ref, o_ref):
        o_ref[...] = (_sigmoid(ga_ref[...].astype(F32)) * ya_ref[...].astype(F32)
                      + _sigmoid(gs_ref[...].astype(F32)) * ys_ref[...].astype(F32)).astype(o_ref.dtype)

    blk = pl.BlockSpec((tm, tc), lambda j, i: (i, j))
    return _pc(body, name=name, grid=(nd, t // tm),
               in_specs=[blk, pl.BlockSpec((tm, tc), lambda j, i: (i, nd + j)), blk, blk],
               out_specs=blk, out_shape=_sds((t, d), BF16), compiler_params=_params())(gates, gates, ya, ys)


def _merge_bwd(dm, gates, ya, ys, d, name):
    t = ya.shape[0]
    tm, tc = min(512, t), _pick(d, (512, 256, 128))
    nd = d // tc

    def body(dm_ref, ga_ref, gs_ref, ya_ref, ys_ref, dya_ref, dys_ref, dga_ref, dgs_ref):
        g = dm_ref[...].astype(F32)
        sa, ss = _sigmoid(ga_ref[...].astype(F32)), _sigmoid(gs_ref[...].astype(F32))
        dya_ref[...] = (g * sa).astype(BF16)
        dys_ref[...] = (g * ss).astype(BF16)
        dga_ref[...] = (g * ya_ref[...].astype(F32) * sa * (1.0 - sa)).astype(BF16)
        dgs_ref[...] = (g * ys_ref[...].astype(F32) * ss * (1.0 - ss)).astype(BF16)

    blk = pl.BlockSpec((tm, tc), lambda j, i: (i, j))
    return _pc(body, name=name, grid=(nd, t // tm),
               in_specs=[blk, blk, pl.BlockSpec((tm, tc), lambda j, i: (i, nd + j)), blk, blk],
               out_specs=[blk] * 4, out_shape=[_sds((t, d), BF16)] * 4,
               compiler_params=_params())(dm, gates, gates, ya, ys)


def _ssd_chunk_terms(dtr, dtb, alog):
    xx = dtr + dtb
    dt = jnp.maximum(xx, 0.0) + jnp.log(1.0 + jnp.exp(-jnp.abs(xx)))
    a = -jnp.exp(alog)
    li = lax.broadcasted_iota(jnp.int32, (CHUNK, CHUNK), 0)
    si = lax.broadcasted_iota(jnp.int32, (CHUNK, CHUNK), 1)
    causal = li >= si
    acum = _dot(causal.astype(F32), dt * a, 1, 0, HIGHEST)
    return xx, dt, a, acum, acum.T, causal


def _split3(x):
    hi = x.astype(BF16)
    r = x - hi.astype(F32)
    mid = r.astype(BF16)
    lo = (r - mid.astype(F32)).astype(BF16)
    return hi, mid, lo


def _expand(v, e):
    hi, mid, lo = _split3(v)
    return _dot(hi, e, 1, 0) + _dot(mid, e, 1, 0) + _dot(lo, e, 1, 0)


def _segsum(s, e):
    hi, mid, lo = _split3(s)
    return _dot(hi, e, 1, 1) + _dot(mid, e, 1, 1) + _dot(lo, e, 1, 1)


def _head_maps(di):
    nh = di // HEAD_DIM
    h = jnp.arange(DT_LANES)[:, None]
    e64 = (jnp.arange(di)[None, :] // HEAD_DIM == h).astype(BF16)
    e128 = (jnp.arange(nh * CHUNK)[None, :] // CHUNK == h).astype(BF16)
    return e64, e128


def _pair_blockdiag(p, left):
    zero = jnp.zeros_like(p)
    return jnp.concatenate([jnp.where(left, p, zero), jnp.where(left, zero, p)], axis=0)


def _ssd_fwd(xc, dtr, dtb, alog, dskx, di, name):
    t = xc.shape[0]
    dx = xc.shape[1]
    nc = t // CHUNK
    nh = di // HEAD_DIM
    hpg = nh // N_GROUPS
    gw = hpg * HEAD_DIM
    boff, coff = di, di + N_GROUPS * D_STATE
    e64, e128 = _head_maps(di)

    def body(xc_ref, dtr_ref, dtb_ref, alog_ref, dsk_ref, e64_ref, e128_ref, y_ref, st_ref, state):
        @pl.when(pl.program_id(0) == 0)
        def _():
            state[...] = jnp.zeros_like(state)
        _, dt, _, acum, acum_t, causal = _ssd_chunk_terms(dtr_ref[...], dtb_ref[...], alog_ref[...])
        last = acum[CHUNK - 1:CHUNK, :]
        e64v = e64_ref[...]
        dtx = _expand(dt, e64v)
        eax = _expand(jnp.exp(acum), e64v)
        dex = _expand(dt * jnp.exp(last - acum), e64v)
        acx = _expand(acum, e128_ref[...])
        st_ref[0] = state[...]
        left = lax.broadcasted_iota(jnp.int32, (CHUNK, 2 * HEAD_DIM), 1) < HEAD_DIM
        for g in range(N_GROUPS):
            gs = slice(g * gw, (g + 1) * gw)
            bg = xc_ref[:, boff + g * D_STATE:boff + (g + 1) * D_STATE]
            cg = xc_ref[:, coff + g * D_STATE:coff + (g + 1) * D_STATE]
            gm = _dot(cg, bg, 1, 1)
            xg = xc_ref[:, gs].astype(F32)
            xdb = (xg * dtx[:, gs]).astype(BF16)
            sin = state[:, gs]
            yo = _dot(cg, sin.astype(BF16), 1, 0) * eax[:, gs]
            for jp in range(hpg // 2):
                h0 = g * hpg + 2 * jp
                ps = slice(jp * 2 * HEAD_DIM, (jp + 1) * 2 * HEAD_DIM)
                ms = []
                for hh in (h0, h0 + 1):
                    seg = acx[:, hh * CHUNK:(hh + 1) * CHUNK] - acum_t[hh:hh + 1, :]
                    ms.append((gm * jnp.exp(jnp.where(causal, seg, -1e30))).astype(BF16))
                yd = _dot(jnp.concatenate(ms, axis=1), _pair_blockdiag(xdb[:, ps], left), 1, 0)
                col = slice(g * gw + jp * 2 * HEAD_DIM, g * gw + (jp + 1) * 2 * HEAD_DIM)
                y_ref[:, col] = (yd + yo[:, ps] + dsk_ref[:, col] * xg[:, ps]).astype(y_ref.dtype)
            xe = (xg * dex[:, gs]).astype(BF16)
            state[:, gs] = eax[CHUNK - 1:CHUNK, gs] * sin + _dot(bg, xe, 0, 0)

    small = pl.BlockSpec((1, DT_LANES), lambda c: (0, 0))
    whole = lambda a: pl.BlockSpec(a.shape, lambda c: (0, 0))
    return _pc(body, name=name, grid=(nc,),
               in_specs=[pl.BlockSpec((CHUNK, dx), lambda c: (c, 0)),
                         pl.BlockSpec((CHUNK, DT_LANES), lambda c: (c, 0)), small, small,
                         whole(dskx), whole(e64), whole(e128)],
               out_specs=[pl.BlockSpec((CHUNK, di), lambda c: (c, 0)),
                          pl.BlockSpec((1, D_STATE, di), lambda c: (c, 0, 0))],
               out_shape=[_sds((t, di), BF16), _sds((nc, D_STATE, di), F32)],
               scratch_shapes=[pltpu.VMEM((D_STATE, di), F32)],
               compiler_params=_params())(xc, dtr, dtb, alog, dskx, e64, e128)


def _ssd_bwd(xc, dtr, dy, states, dtb, alog, dskx, di, name):
    t = xc.shape[0]
    dx = xc.shape[1]
    nc = t // CHUNK
    nh = di // HEAD_DIM
    hpg = nh // N_GROUPS
    gw = hpg * HEAD_DIM
    boff, coff = di, di + N_GROUPS * D_STATE
    e64, e128 = _head_maps(di)

    def body(xc_ref, dtr_ref, dy_ref, st_ref, dtb_ref, alog_ref, dsk_ref, e64_ref, e128_ref,
             dxc_ref, ddtr_ref, sm_ref, dstate, darow):
        @pl.when(pl.program_id(0) == 0)
        def _():
            dstate[...] = jnp.zeros_like(dstate)
            sm_ref[...] = jnp.zeros_like(sm_ref)
        darow[...] = jnp.zeros_like(darow)
        xx, dt, a, acum, acum_t, causal = _ssd_chunk_terms(dtr_ref[...], dtb_ref[...], alog_ref[...])
        last = acum[CHUNK - 1:CHUNK, :]
        e64v = e64_ref[...]
        dtx = _expand(dt, e64v)
        eax = _expand(jnp.exp(acum), e64v)
        eex = _expand(jnp.exp(last - acum), e64v)
        acx = _expand(acum, e128_ref[...])
        left = lax.broadcasted_iota(jnp.int32, (CHUNK, 2 * HEAD_DIM), 1) < HEAD_DIM
        lane = lax.broadcasted_iota(jnp.int32, (CHUNK, DT_LANES), 1)
        sub8 = lax.broadcasted_iota(jnp.int32, (8, gw), 0)
        da_col = jnp.zeros((CHUNK, DT_LANES), F32)
        ddt_col = jnp.zeros((CHUNK, DT_LANES), F32)
        rows = jnp.zeros((8, DT_LANES), F32)
        for g in range(N_GROUPS):
            gs = slice(g * gw, (g + 1) * gw)
            bg = xc_ref[:, boff + g * D_STATE:boff + (g + 1) * D_STATE]
            cg = xc_ref[:, coff + g * D_STATE:coff + (g + 1) * D_STATE]
            gm = _dot(cg, bg, 1, 1)
            e64g = e64v[:, gs]
            xg = xc_ref[:, gs].astype(F32)
            dtg, eag, eeg = dtx[:, gs], eax[:, gs], eex[:, gs]
            xd = xg * dtg
            xdb = xd.astype(BF16)
            dyb = dy_ref[:, gs]
            dyf = dyb.astype(F32)
            sin = st_ref[0, :, gs]
            sinb = sin.astype(BF16)
            ds = dstate[:, gs]
            dsb = ds.astype(BF16)
            bds = _dot(bg, dsb, 1, 0)
            dyeb = (dyf * eag).astype(BF16)
            dcg = _dot(dyeb, sinb, 1, 1)
            dstate[:, gs] = eag[CHUNK - 1:CHUNK, :] * ds + _dot(cg, dyeb, 0, 0)
            yo = _dot(cg, sinb, 1, 0) * eag
            xe = xd * eeg
            dbg = _dot(xe.astype(BF16), dsb, 1, 1)
            wterm = bds * xe
            da_col = da_col + _segsum(dyf * yo - wterm, e64g)
            dg = jnp.zeros((CHUNK, CHUNK), F32)
            dxd_parts = []
            for jp in range(hpg // 2):
                h0 = g * hpg + 2 * jp
                ps = slice(jp * 2 * HEAD_DIM, (jp + 1) * 2 * HEAD_DIM)
                lms, mfs = [], []
                for hh in (h0, h0 + 1):
                    seg = acx[:, hh * CHUNK:(hh + 1) * CHUNK] - acum_t[hh:hh + 1, :]
                    lm = jnp.exp(jnp.where(causal, seg, -1e30))
                    lms.append(lm)
                    mfs.append(gm * lm)
                mstack = jnp.concatenate([m.astype(BF16) for m in mfs], axis=0)
                dyp = dyb[:, ps]
                dxd_parts.append(_dot(mstack, _pair_blockdiag(dyp, left), 0, 0))
                dm2 = _dot(dyp, _pair_blockdiag(xdb[:, ps], left), 1, 1)
                for k, hh in enumerate((h0, h0 + 1)):
                    dm = dm2[:, k * CHUNK:(k + 1) * CHUNK]
                    dg = dg + dm * lms[k]
                    q = dm * mfs[k]
                    da_col = da_col + jnp.where(lane == hh, jnp.sum(q, axis=1, keepdims=True), 0.0)
                    darow[hh:hh + 1, :] = -jnp.sum(q, axis=0, keepdims=True)
            dxd = jnp.concatenate(dxd_parts, axis=1) + bds * eeg
            ddt_col = ddt_col + _segsum(dxd * xg, e64g)
            rsum = (jnp.where(sub8 == 0, jnp.sum(wterm, axis=0, keepdims=True), 0.0)
                    + jnp.where(sub8 == 1, jnp.sum(ds * sin, axis=0, keepdims=True), 0.0)
                    + jnp.where(sub8 == 2, jnp.sum(dyf * xg, axis=0, keepdims=True), 0.0))
            rows = rows + _segsum(rsum, e64g)
            dxc_ref[:, gs] = (dxd * dtg + dsk_ref[:, gs] * dyf).astype(dxc_ref.dtype)
            dgb = dg.astype(BF16)
            dxc_ref[:, boff + g * D_STATE:boff + (g + 1) * D_STATE] = (
                dbg + _dot(dgb, cg, 0, 0)).astype(dxc_ref.dtype)
            dxc_ref[:, coff + g * D_STATE:coff + (g + 1) * D_STATE] = (
                dcg + _dot(dgb, bg, 1, 0)).astype(dxc_ref.dtype)
        at_last = rows[0:1, :] + jnp.exp(last) * rows[1:2, :]
        is_last = lax.broadcasted_iota(jnp.int32, (CHUNK, DT_LANES), 0) == CHUNK - 1
        da = da_col + jnp.where(is_last, at_last, 0.0) + darow[...].T
        li = lax.broadcasted_iota(jnp.int32, (CHUNK, CHUNK), 0)
        si = lax.broadcasted_iota(jnp.int32, (CHUNK, CHUNK), 1)
        dla = _dot((si >= li).astype(F32), da, 1, 0, HIGHEST)
        ddtr = (ddt_col + dla * a) * _sigmoid(xx)
        ddtr_ref[...] = ddtr
        sm_ref[0:1, :] += jnp.sum(ddtr, axis=0, keepdims=True)
        sm_ref[1:2, :] += jnp.sum(dla * dt, axis=0, keepdims=True) * a
        sm_ref[2:3, :] += rows[2:3, :]

    small = pl.BlockSpec((1, DT_LANES), lambda c: (0, 0))
    whole = lambda a: pl.BlockSpec(a.shape, lambda c: (0, 0))
    rev = lambda c: (nc - 1 - c, 0)
    return _pc(body, name=name, grid=(nc,),
               in_specs=[pl.BlockSpec((CHUNK, dx), rev), pl.BlockSpec((CHUNK, DT_LANES), rev),
                         pl.BlockSpec((CHUNK, di), rev),
                         pl.BlockSpec((1, D_STATE, di), lambda c: (nc - 1 - c, 0, 0)), small, small,
                         whole(dskx), whole(e64), whole(e128)],
               out_specs=[pl.BlockSpec((CHUNK, dx), rev), pl.BlockSpec((CHUNK, DT_LANES), rev),
                          pl.BlockSpec((8, DT_LANES), lambda c: (0, 0))],
               out_shape=[_sds((t, dx), BF16), _sds((t, DT_LANES), F32), _sds((8, DT_LANES), F32)],
               scratch_shapes=[pltpu.VMEM((D_STATE, di), F32), pltpu.VMEM((DT_LANES, CHUNK), F32)],
               compiler_params=_params())(xc, dtr, dy, states, dtb, alog, dskx, e64, e128)


def _adamw(parts, w, m, v, name):
    npart, rows, width = parts.shape
    tr = _pick(rows, (64, 32, 16, 8))
    c1 = 1.0 - ADAM_B1 ** ADAM_STEP
    c2 = 1.0 - ADAM_B2 ** ADAM_STEP

    def body(p_ref, w_ref, m_ref, v_ref, g_ref, d_ref, nm_ref, nv_ref):
        g = p_ref[0].astype(F32)
        for p in range(1, npart):
            g = g + p_ref[p].astype(F32)
        nm = ADAM_B1 * m_ref[...] + (1.0 - ADAM_B1) * g
        nv = ADAM_B2 * v_ref[...] + (1.0 - ADAM_B2) * (g * g)
        g_ref[...] = g
        nm_ref[...] = nm
        nv_ref[...] = nv
        d_ref[...] = -ADAM_LR * ((nm / c1) / (jnp.sqrt(nv / c2) + ADAM_EPS) + ADAM_WD * w_ref[...])

    blk = pl.BlockSpec((tr, width), lambda i: (i, 0))
    return _pc(body, name=name, grid=(rows // tr,),
               in_specs=[pl.BlockSpec((npart, tr, width), lambda i: (0, i, 0)), blk, blk, blk],
               out_specs=[blk] * 4, out_shape=[_sds((rows, width), F32)] * 4,
               compiler_params=_params())(parts, w, m, v)


def _sum_parts(parts, name, tile=None):
    npart, rows, width = parts.shape
    tile = rows if tile is None else tile

    def body(p_ref, o_ref):
        g = p_ref[0].astype(F32)
        for p in range(1, npart):
            g = g + p_ref[p].astype(F32)
        o_ref[...] = g

    return _pc(body, name=name, grid=(rows // tile,),
               in_specs=[pl.BlockSpec((npart, tile, width), lambda i: (0, i, 0))],
               out_specs=pl.BlockSpec((tile, width), lambda i: (i, 0)),
               out_shape=_sds((rows, width), F32), compiler_params=_params())(parts)


def _peers():
    x, y, c = lax.axis_index("x"), lax.axis_index("y"), lax.axis_index("c")
    out = []
    for k in range(1, N_DEV):
        px = 1 - x if k & 4 else x
        py = 1 - y if k & 2 else y
        pc = 1 - c if k & 1 else c
        out.append(((px, py, pc), 4 * px + 2 * py + pc))
    return 4 * x + 2 * y + c, out


def _exchange(big, small, scatter, name):
    r, w = big.shape[-2:]
    rs = small.shape[0]

    def body(b_ref, s_ref, ob_ref, os_ref, send_sems, recv_sems, local_sems):
        me, peers = _peers()
        mine = b_ref.at[me] if scatter else b_ref
        local = [pltpu.make_async_copy(mine, ob_ref.at[me], local_sems.at[0]),
                 pltpu.make_async_copy(s_ref, os_ref.at[me], local_sems.at[1])]
        for cp in local:
            cp.start()
        sends, recvs = [], []
        for k, (peer, pidx) in enumerate(peers):
            src = b_ref.at[pidx] if scatter else b_ref
            sends.append(pltpu.make_async_remote_copy(src_ref=src, dst_ref=ob_ref.at[me],
                                                      send_sem=send_sems.at[2 * k], recv_sem=recv_sems.at[2 * k],
                                                      device_id=peer, device_id_type=MESH))
            sends.append(pltpu.make_async_remote_copy(src_ref=s_ref, dst_ref=os_ref.at[me],
                                                      send_sem=send_sems.at[2 * k + 1], recv_sem=recv_sems.at[2 * k + 1],
                                                      device_id=peer, device_id_type=MESH))
            recvs.append(pltpu.make_async_remote_copy(src_ref=src, dst_ref=ob_ref.at[pidx],
                                                      send_sem=send_sems.at[2 * k], recv_sem=recv_sems.at[2 * k],
                                                      device_id=peer, device_id_type=MESH))
            recvs.append(pltpu.make_async_remote_copy(src_ref=s_ref, dst_ref=os_ref.at[pidx],
                                                      send_sem=send_sems.at[2 * k + 1], recv_sem=recv_sems.at[2 * k + 1],
                                                      device_id=peer, device_id_type=MESH))
        for cp in sends:
            cp.start()
        for cp in recvs:
            cp.wait_recv()
        for cp in sends:
            cp.wait_send()
        for cp in local:
            cp.wait()

    hbm = pl.BlockSpec(memory_space=pltpu.HBM)
    return _pc(body, name=name, in_specs=[hbm, hbm], out_specs=[hbm, hbm],
               out_shape=[_sds((N_DEV, r, w), big.dtype), _sds((N_DEV, rs, w), small.dtype)],
               scratch_shapes=[pltpu.SemaphoreType.DMA((2 * (N_DEV - 1),)), pltpu.SemaphoreType.DMA((2 * (N_DEV - 1),)),
                               pltpu.SemaphoreType.DMA((2,))],
               compiler_params=pltpu.CompilerParams(has_side_effects=True))(big, small)


def _pack(arrs, width, row_mult):
    flat = jnp.concatenate([a.reshape(-1) for a in arrs])
    n = flat.shape[0]
    rows = -(-n // (width * row_mult)) * row_mult
    return jnp.pad(flat, (0, rows * width - n)).reshape(rows, width)


def _unpack(packed, shapes, lead=None):
    out, off = [], 0
    flat = packed.reshape(-1) if lead is None else packed.reshape(lead, -1)
    for s in shapes:
        n = math.prod(s)
        if lead is None:
            out.append(flat[off:off + n].reshape(s))
        else:
            out.append(flat[:, off:off + n].reshape((lead,) + tuple(s)))
        off += n
    return out


def _blocks_to_cols(blocks):
    nb, rows, n = blocks.shape
    return blocks.transpose(1, 0, 2).reshape(rows, nb * n)


def _pad_rows(a, rows):
    return jnp.pad(a, ((0, rows - a.shape[0]), (0, 0)))


def _pad_lanes(a, lanes):
    return jnp.pad(a, ((0, 0), (0, lanes - a.shape[1])))


BIG = ("w_a_out", "w_s_out", "w_o", "w_up", "w_down", "w_in")
TRANSPOSED = ("w_up", "w_in")
ROW_ALIGN = 32
CONVS = ("conv_a_w", "ssd_conv_w", "ffn_conv_w")
REPL = ("norm_mix_w", "ssd_conv_b", "dt_bias", "a_log", "d_skip", "ssd_norm_w", "norm_ffn_w", "ffn_conv_b",
        "final_norm_w")
ORDER = ("norm_mix_w", "w_in", "conv_a_w", "w_a_out", "ssd_conv_w", "ssd_conv_b", "dt_bias", "a_log", "d_skip",
         "ssd_norm_w", "w_s_out", "w_o", "norm_ffn_w", "w_up", "ffn_conv_w", "ffn_conv_b", "w_down", "final_norm_w")


def _as_rows(name, block):
    return block[0].T if name in TRANSPOSED else block[0]


def kernel(x, norm_mix_w, w_in, conv_a_w, w_a_out, ssd_conv_w, ssd_conv_b, dt_bias, a_log, d_skip, ssd_norm_w, w_s_out, w_o, norm_ffn_w, w_up, ffn_conv_w, ffn_conv_b, w_down, final_norm_w, loss_target, m_norm_mix_w, m_w_in, m_conv_a_w, m_w_a_out, m_ssd_conv_w, m_ssd_conv_b, m_dt_bias, m_a_log, m_d_skip, m_ssd_norm_w, m_w_s_out, m_w_o, m_norm_ffn_w, m_w_up, m_ffn_conv_w, m_ffn_conv_b, m_w_down, m_final_norm_w, v_norm_mix_w, v_w_in, v_conv_a_w, v_w_a_out, v_ssd_conv_w, v_ssd_conv_b, v_dt_bias, v_a_log, v_d_skip, v_ssd_norm_w, v_w_s_out, v_w_o, v_norm_ffn_w, v_w_up, v_ffn_conv_w, v_ffn_conv_b, v_w_down, v_final_norm_w):
    wts = dict(norm_mix_w=norm_mix_w, w_in=w_in, conv_a_w=conv_a_w, w_a_out=w_a_out, ssd_conv_w=ssd_conv_w,
               ssd_conv_b=ssd_conv_b, dt_bias=dt_bias, a_log=a_log, d_skip=d_skip, ssd_norm_w=ssd_norm_w,
               w_s_out=w_s_out, w_o=w_o, norm_ffn_w=norm_ffn_w, w_up=w_up, ffn_conv_w=ffn_conv_w,
               ffn_conv_b=ffn_conv_b, w_down=w_down, final_norm_w=final_norm_w)
    mom1 = dict(norm_mix_w=m_norm_mix_w, w_in=m_w_in, conv_a_w=m_conv_a_w, w_a_out=m_w_a_out,
                ssd_conv_w=m_ssd_conv_w, ssd_conv_b=m_ssd_conv_b, dt_bias=m_dt_bias, a_log=m_a_log, d_skip=m_d_skip,
                ssd_norm_w=m_ssd_norm_w, w_s_out=m_w_s_out, w_o=m_w_o, norm_ffn_w=m_norm_ffn_w, w_up=m_w_up,
                ffn_conv_w=m_ffn_conv_w, ffn_conv_b=m_ffn_conv_b, w_down=m_w_down, final_norm_w=m_final_norm_w)
    mom2 = dict(norm_mix_w=v_norm_mix_w, w_in=v_w_in, conv_a_w=v_conv_a_w, w_a_out=v_w_a_out,
                ssd_conv_w=v_ssd_conv_w, ssd_conv_b=v_ssd_conv_b, dt_bias=v_dt_bias, a_log=v_a_log, d_skip=v_d_skip,
                ssd_norm_w=v_ssd_norm_w, w_s_out=v_w_s_out, w_o=v_w_o, norm_ffn_w=v_norm_ffn_w, w_up=v_w_up,
                ffn_conv_w=v_ffn_conv_w, ffn_conv_b=v_ffn_conv_b, w_down=v_w_down, final_norm_w=v_final_norm_w)

    t, d = x.shape[1], x.shape[2]
    di = 2 * d
    nh = di // HEAD_DIM
    dxw = di + 2 * N_GROUPS * D_STATE
    f = w_down.shape[1] * N_DEV
    n_in = w_in.shape[2] * N_DEV
    me = 4 * lax.axis_index("x") + 2 * lax.axis_index("y") + lax.axis_index("c")

    local_rows = [_as_rows(k, wts[k]) for k in BIG]
    nrows = [a.shape[0] for a in local_rows]
    offs = [sum(nrows[:i]) for i in range(len(BIG))]
    assert all(o % ROW_ALIGN == 0 for o in offs)
    rtot = -(-sum(nrows) // 64) * 64
    big_local = _pad_rows(jnp.concatenate(local_rows, axis=0), rtot).astype(BF16)
    conv_shapes = [wts[k].shape[1:] for k in CONVS]
    conv_local = _pack([wts[k] for k in CONVS], d, 8)
    big_all, conv_all = _exchange(big_local, conv_local, False, "gather_weights")
    full = {k: big_all[:, o:o + n].reshape(N_DEV * n, d) for k, o, n in zip(BIG, offs, nrows)}
    waout, wsout, wo, wup_t, wdown, win_t = (full[k] for k in BIG)
    c_a, c_s, c_f = _unpack(conv_all, conv_shapes, N_DEV)
    caw, scw, fcw = _blocks_to_cols(c_a), _blocks_to_cols(c_s), _blocks_to_cols(c_f)

    o_z, o_x, o_dt = 5 * d, 7 * d, 7 * d + dxw
    seg_bounds = [0, d, 2 * d, 3 * d, 4 * d, o_z, o_x, o_dt]
    w_dt = _pad_rows(win_t[o_dt:], DT_LANES)
    dtb, alog = (_pad_lanes(p[...].reshape(1, nh), DT_LANES) for p in (dt_bias, a_log))
    dskx = jnp.repeat(d_skip.reshape(1, nh), HEAD_DIM, axis=1)

    x2, tgt = x[0], loss_target[0]
    u = _rms_fwd(x2, norm_mix_w, "norm_mix")
    gates = _mm([(u, win_t[:2 * d])], "nt", BF16, "proj_gates")
    pa = _mm([(u, win_t[2 * d:o_z])], "nt", BF16, "proj_a")
    z = _mm([(u, win_t[o_z:o_x])], "nt", BF16, "proj_z")
    xbc = _mm([(u, win_t[o_x:o_dt])], "nt", BF16, "proj_xbc")
    dtr = _mm([(u, w_dt)], "nt", F32, "proj_dt")
    ya_in = _conv_a_fwd(pa, caw, d, "conv_a")
    y_a = _mm([(ya_in, waout)], "nn", BF16, "a_out")
    xc = _conv_s_fwd(xbc, scw, ssd_conv_b, "conv_s")
    y, states = _ssd_fwd(xc, dtr, dtb, alog, dskx, di, "ssd")
    yn = _gnorm_fwd(y, z, ssd_norm_w, "gnorm")
    y_s = _mm([(yn, wsout)], "nn", BF16, "s_out")
    merged = _merge_fwd(gates, y_a, y_s, d, "merge")
    mo = _mm([(merged, wo)], "nn", F32, "o_proj")
    h1, v = _resnorm_fwd(x2, mo, norm_ffn_w, "norm_ffn")
    hv = _mm([(v, wup_t)], "nt", BF16, "up_proj")
    act = _ffn_fwd(hv, fcw, ffn_conv_b, f, "ffn_act")
    dd = _mm([(act, wdown)], "nn", F32, "down_proj")
    loss11, dh2, dh2b, g_fnw = _final(h1, dd, tgt, final_norm_w.reshape(1, d), "final")

    dact = _mm([(dh2b, wdown)], "nt", BF16, "d_act")
    gw_down = _mm_tn(act, dh2b, "gw_down")
    dh1f, dh3, g_ffn = _ffn_bwd(hv, dact, fcw, ffn_conv_b, f, "ffn_act_bwd")
    dv = _mm([(dh1f, wup_t[:f]), (dh3, wup_t[f:])], "nn", F32, "d_v")
    gw_up_t = jnp.concatenate([_mm_tn(dh1f, v, "gw_up1"), _mm_tn(dh3, v, "gw_up3")], axis=0)
    dh1, dh1b, g_nfw = _rms_bwd(h1, dv, norm_ffn_w, dh2, "norm_ffn_bwd")
    dmerged = _mm([(dh1b, wo)], "nt", BF16, "d_merged")
    gw_o = _mm_tn(merged, dh1b, "gw_o")
    dya, dys, dga, dgs = _merge_bwd(dmerged, gates, y_a, y_s, d, "merge_bwd")
    dyain = _mm([(dya, waout)], "nt", BF16, "d_ya_in")
    gw_aout = _mm_tn(ya_in, dya, "gw_a_out")
    db, dc, dvv, g_caw = _conv_a_bwd(pa, dyain, caw, d, "conv_a_bwd")
    dyn = _mm([(dys, wsout)], "nt", BF16, "d_yn")
    gw_sout = _mm_tn(yn, dys, "gw_s_out")
    dy, dz, g_snw = _gnorm_bwd(y, z, dyn, ssd_norm_w, "gnorm_bwd")
    dxc, ddtr, g_ssd = _ssd_bwd(xc, dtr, dy, states, dtb, alog, dskx, di, "ssd_bwd")
    dxbc, g_scw = _conv_s_bwd(xbc, dxc, scw, ssd_conv_b, "conv_s_bwd")
    dsegs = [dga, dgs, db, dc, dvv, dz, dxbc]
    pairs = [(s, win_t[a:b]) for s, a, b in zip(dsegs, seg_bounds[:-1], seg_bounds[1:])]
    pairs.append((ddtr.astype(BF16), w_dt))
    du = _mm(pairs, "nn", F32, "d_u", tm=512, tn=512)
    gw_in_t = jnp.concatenate([_mm_tn(s, u, "gw_in%d" % i) for i, (s, _) in enumerate(pairs)], axis=0)[:n_in]
    dx, _, g_nmw = _rms_bwd(x2, du, norm_mix_w, dh1, "norm_mix_bwd")

    grads_full = dict(w_a_out=gw_aout, w_s_out=gw_sout, w_o=gw_o, w_up=gw_up_t, w_down=gw_down, w_in=gw_in_t)
    big_parts = jnp.concatenate([grads_full[k].reshape(N_DEV, n, d) for k, n in zip(BIG, nrows)], axis=1)
    big_parts = jnp.pad(big_parts, ((0, 0), (0, rtot - sum(nrows)), (0, 0))).astype(BF16)
    small_grads = dict(norm_mix_w=g_nmw[0], ssd_conv_b=g_scw[4], dt_bias=g_ssd[0, :nh], a_log=g_ssd[1, :nh],
                       d_skip=g_ssd[2, :nh], ssd_norm_w=g_snw[0], norm_ffn_w=g_nfw[0], ffn_conv_b=g_ffn[3],
                       final_norm_w=g_fnw[0], conv_a_w=g_caw[:3], ssd_conv_w=g_scw[:4], ffn_conv_w=g_ffn[:3])
    small_names = REPL + CONVS
    small_parts = _pack([small_grads[k] for k in small_names], d, 8)
    big_recv, small_all = _exchange(big_parts, small_parts, True, "exchange_grads")
    small_sum = _sum_parts(small_all, "sum_small_grads")
    small_g = dict(zip(small_names, _unpack(small_sum, [small_grads[k].shape for k in small_names])))

    big_sum = _sum_parts(big_recv, "sum_big_grads", tile=64)
    res = {}

    def update(k, g):
        outs = _adamw(g.reshape(1, -1, g.shape[-1]), *(src[k].reshape(-1, g.shape[-1]) for src in (wts, mom1, mom2)),
                      "adamw_" + k)
        for kind, a in zip(("g", "d", "m", "v"), outs):
            res[kind, k] = a.reshape(wts[k].shape)

    for k, o, n in zip(BIG, offs, nrows):
        rows = big_sum[o:o + n]
        update(k, rows.T if k in TRANSPOSED else rows)
    local_g = {}
    for k in REPL:
        local_g[k] = small_g[k].reshape(wts[k].shape)
    for k in CONVS:
        n = wts[k].shape[2]
        local_g[k] = lax.dynamic_slice_in_dim(small_g[k], me * n, n, axis=1)[None]
    w_sm, m_sm, v_sm = (_pack([src[k] for k in small_names], d, 8) for src in (wts, mom1, mom2))
    g_sm = _pack([local_g[k] for k in small_names], d, 8)
    outs_sm = _adamw(g_sm[None], w_sm, m_sm, v_sm, "adamw_small")
    for kind, packed in zip(("g", "d", "m", "v"), outs_sm):
        for k, a in zip(small_names, _unpack(packed, [wts[k].shape for k in small_names])):
            res[kind, k] = a

    loss = lax.psum(loss11[0, 0], ("x", "y", "c"))
    return (loss, dx[None], *[res["g", k] for k in ORDER], *[res["d", k] for k in ORDER],
            *[res["m", k] for k in ORDER], *[res["v", k] for k in ORDER])
```

```python
import functools
import math

import jax
import jax.numpy as jnp
from jax import lax
from jax.experimental import pallas as pl
from jax.experimental.pallas import tpu as pltpu

F32 = jnp.float32
BF16 = jnp.bfloat16
EPS = 1e-5
HEAD_DIM = 64
N_GROUPS = 4
D_STATE = 128
CHUNK = 128
DT_LANES = 128
HALO = 16
N_DEV = 8
V7X_VMEM_LIMIT = 56 * 1024 * 1024
ADAM_LR, ADAM_B1, ADAM_B2, ADAM_EPS, ADAM_WD, ADAM_STEP = 0.001, 0.9, 0.999, 1e-08, 0.01, 10
HIGHEST = lax.Precision.HIGHEST
MESH = pl.DeviceIdType.MESH


def _pc(body, **kw):
    return pl.pallas_call(body, **kw)


def _params():
    return pltpu.CompilerParams(vmem_limit_bytes=V7X_VMEM_LIMIT)


def _pick(n, cands):
    for c in cands:
        if n % c == 0:
            return c
    return n


def _dot(a, b, ca, cb, prec=None):
    return lax.dot_general(a, b, (((ca,), (cb,)), ((), ())), preferred_element_type=F32, precision=prec)


def _sigmoid(x):
    return 0.5 * jnp.tanh(0.5 * x) + 0.5


def _sds(shape, dtype):
    return jax.ShapeDtypeStruct(shape, dtype)


def _mm(pairs, mode, out_dtype, name, tm=1024, tn=1024, after=None):
    m = pairs[0][0].shape[0]
    n = pairs[0][1].shape[1] if mode == "nn" else pairs[0][1].shape[0]
    tm = min(tm, m)
    tn = _pick(n, (tn, 1408, 512, 256, 128))
    npair = len(pairs)
    cb = 0 if mode == "nn" else 1

    def body(*refs):
        o_ref = refs[-1]
        acc = None
        for p in range(npair):
            part = _dot(refs[2 * p][...], refs[2 * p + 1][...], 1, cb)
            acc = part if acc is None else acc + part
        o_ref[...] = acc.astype(o_ref.dtype)

    in_specs, args = [], []
    for a, b in pairs:
        k = a.shape[1]
        in_specs.append(pl.BlockSpec((tm, k), lambda i, j: (i, 0)))
        if mode == "nn":
            in_specs.append(pl.BlockSpec((k, tn), lambda i, j: (0, j)))
        else:
            in_specs.append(pl.BlockSpec((tn, k), lambda i, j: (j, 0)))
        args += [a, b]
    if after is not None:
        in_specs.append(pl.BlockSpec(memory_space=pl.ANY))
        args.append(after)
    return _pc(body, name=name, grid=(m // tm, n // tn), in_specs=in_specs,
               out_specs=pl.BlockSpec((tm, tn), lambda i, j: (i, j)),
               out_shape=_sds((m, n), out_dtype), compiler_params=_params())(*args)


def _mm_tn(a, b, name, tm=1024):
    m, ka = a.shape
    nb = b.shape[1]
    tm = min(tm, m)
    tk = _pick(ka, (1024, 1408, 512, 256, 128))
    tn = _pick(nb, (1024, 512, 256, 128))

    def body(a_ref, b_ref, o_ref):
        @pl.when(pl.program_id(2) == 0)
        def _():
            o_ref[...] = jnp.zeros_like(o_ref)
        o_ref[...] += _dot(a_ref[...], b_ref[...], 0, 0)

    return _pc(body, name=name, grid=(ka // tk, nb // tn, m // tm),
               in_specs=[pl.BlockSpec((tm, tk), lambda i, j, t: (t, i)),
                         pl.BlockSpec((tm, tn), lambda i, j, t: (t, j))],
               out_specs=pl.BlockSpec((tk, tn), lambda i, j, t: (i, j)),
               out_shape=_sds((ka, nb), F32), compiler_params=_params())(a, b)


def _rms_fwd(x, w, name):
    t, d = x.shape
    tm = min(512, t)

    def body(x_ref, w_ref, o_ref):
        xv = x_ref[...]
        r = lax.rsqrt(jnp.mean(xv * xv, axis=-1, keepdims=True) + EPS)
        o_ref[...] = (xv * r * w_ref[...]).astype(o_ref.dtype)

    return _pc(body, name=name, grid=(t // tm,),
               in_specs=[pl.BlockSpec((tm, d), lambda i: (i, 0)), pl.BlockSpec((1, d), lambda i: (0, 0))],
               out_specs=pl.BlockSpec((tm, d), lambda i: (i, 0)),
               out_shape=_sds((t, d), BF16), compiler_params=_params())(x, w)


def _resnorm_fwd(x, mo, w, name):
    t, d = x.shape
    tm = min(512, t)

    def body(x_ref, mo_ref, w_ref, h_ref, v_ref):
        h = x_ref[...] + mo_ref[...]
        r = lax.rsqrt(jnp.mean(h * h, axis=-1, keepdims=True) + EPS)
        h_ref[...] = h
        v_ref[...] = (h * r * w_ref[...]).astype(v_ref.dtype)

    row = pl.BlockSpec((tm, d), lambda i: (i, 0))
    return _pc(body, name=name, grid=(t // tm,),
               in_specs=[row, row, pl.BlockSpec((1, d), lambda i: (0, 0))],
               out_specs=[row, row], out_shape=[_sds((t, d), F32), _sds((t, d), BF16)],
               compiler_params=_params())(x, mo, w)


def _rms_bwd(h, dy, w, dres, name):
    t, d = h.shape
    tm = min(512, t)

    def body(h_ref, dy_ref, w_ref, dres_ref, dx_ref, dxb_ref, dw_ref):
        @pl.when(pl.program_id(0) == 0)
        def _():
            dw_ref[...] = jnp.zeros_like(dw_ref)
        hv = h_ref[...]
        dyv = dy_ref[...]
        r = lax.rsqrt(jnp.mean(hv * hv, axis=-1, keepdims=True) + EPS)
        n = hv * r
        dn = dyv * w_ref[...]
        dw_ref[0:1, :] += jnp.sum(dyv * n, axis=0, keepdims=True)
        dx = dres_ref[...] + r * (dn - n * jnp.mean(dn * n, axis=-1, keepdims=True))
        dx_ref[...] = dx
        dxb_ref[...] = dx.astype(BF16)

    row = pl.BlockSpec((tm, d), lambda i: (i, 0))
    return _pc(body, name=name, grid=(t // tm,),
               in_specs=[row, row, pl.BlockSpec((1, d), lambda i: (0, 0)), row],
               out_specs=[row, row, pl.BlockSpec((8, d), lambda i: (0, 0))],
               out_shape=[_sds((t, d), F32), _sds((t, d), BF16), _sds((8, d), F32)],
               compiler_params=_params())(h, dy, w, dres)


def _final(h1, dd, tgt, w, name):
    t, d = h1.shape
    tm = min(512, t)
    nt = t // tm

    def body(h1_ref, dd_ref, tgt_ref, w_ref, loss_ref, dh_ref, dhb_ref, dw_ref, acc):
        i = pl.program_id(0)

        @pl.when(i == 0)
        def _():
            dw_ref[...] = jnp.zeros_like(dw_ref)
            acc[...] = jnp.zeros_like(acc)
        h = h1_ref[...] + dd_ref[...]
        r = lax.rsqrt(jnp.mean(h * h, axis=-1, keepdims=True) + EPS)
        n = h * r
        wv = w_ref[...]
        e = n * wv - tgt_ref[...]
        acc[...] += jnp.sum(e * e, axis=0, keepdims=True)
        dout = e * (1.0 / d)
        dn = dout * wv
        dw_ref[0:1, :] += jnp.sum(dout * n, axis=0, keepdims=True)
        dh = r * (dn - n * jnp.mean(dn * n, axis=-1, keepdims=True))
        dh_ref[...] = dh
        dhb_ref[...] = dh.astype(BF16)

        @pl.when(i == nt - 1)
        def _():
            loss_ref[...] = jnp.sum(acc[...], axis=-1, keepdims=True) * (0.5 / d)

    row = pl.BlockSpec((tm, d), lambda i: (i, 0))
    return _pc(body, name=name, grid=(nt,),
               in_specs=[row, row, row, pl.BlockSpec((1, d), lambda i: (0, 0))],
               out_specs=[pl.BlockSpec((1, 1), lambda i: (0, 0)), row, row, pl.BlockSpec((8, d), lambda i: (0, 0))],
               out_shape=[_sds((1, 1), F32), _sds((t, d), F32), _sds((t, d), BF16), _sds((8, d), F32)],
               scratch_shapes=[pltpu.VMEM((1, d), F32)], compiler_params=_params())(h1, dd, tgt, w)


def _fill(buf, prev, cur, nxt, tm):
    if prev is not None:
        buf[0:HALO, :] = prev
    buf[HALO:HALO + tm, :] = cur
    if nxt is not None:
        buf[HALO + tm:HALO + tm + HALO, :] = nxt


def _taps(xbuf, k, rows):
    return [xbuf[pl.ds(HALO - (k - 1) + j, rows), :] for j in range(k)]


def _conv_rows(taps, w):
    out = None
    for j, xs in enumerate(taps):
        term = w[j:j + 1, :] * xs
        out = term if out is None else out + term
    return out


def _conv_transpose(dbuf, w, k, tm):
    out = None
    for j in range(k):
        term = w[j:j + 1, :] * dbuf[pl.ds(k - 1 - j, tm), :]
        out = term if out is None else out + term
    return out


def _acc_conv_grads(acc_ref, dcur, taps, tm, with_bias):
    for j, xs in enumerate(taps):
        acc_ref[j:j + 1, :] += jnp.sum(dcur * xs[0:tm, :], axis=0, keepdims=True)
    if with_bias:
        acc_ref[len(taps):len(taps) + 1, :] += jnp.sum(dcur, axis=0, keepdims=True)


def _tile_specs(t, tm, tc, col0):
    th = tm // HALO
    last = t // HALO - 1
    cur = pl.BlockSpec((tm, tc), lambda j, i: (i, col0 + j))
    prev = pl.BlockSpec((HALO, tc), lambda j, i: (jnp.maximum(i * th - 1, 0), col0 + j))
    nxt = pl.BlockSpec((HALO, tc), lambda j, i: (jnp.minimum((i + 1) * th, last), col0 + j))
    return cur, prev, nxt


def _conv_a_fwd(pa, w, d, name):
    t = pa.shape[0]
    tm, tc = min(512, t), _pick(d, (512, 256, 128))
    nd = d // tc

    def body(b_ref, c_ref, v_ref, cp_ref, vp_ref, w_ref, o_ref, buf):
        i = pl.program_id(1)
        keep = (i > 0).astype(F32)
        prev = cp_ref[...].astype(F32) * vp_ref[...].astype(F32) * keep
        _fill(buf, prev, c_ref[...].astype(F32) * v_ref[...].astype(F32), None, tm)
        q = _conv_rows(_taps(buf, 3, tm), w_ref[...])
        o_ref[...] = (b_ref[...].astype(F32) * q).astype(o_ref.dtype)

    b_cur, _, _ = _tile_specs(t, tm, tc, 0)
    c_cur, c_prev, _ = _tile_specs(t, tm, tc, nd)
    v_cur, v_prev, _ = _tile_specs(t, tm, tc, 2 * nd)
    return _pc(body, name=name, grid=(nd, t // tm),
               in_specs=[b_cur, c_cur, v_cur, c_prev, v_prev, pl.BlockSpec((3, tc), lambda j, i: (0, j))],
               out_specs=pl.BlockSpec((tm, tc), lambda j, i: (i, j)),
               out_shape=_sds((t, d), BF16),
               scratch_shapes=[pltpu.VMEM((tm + 2 * HALO, tc), F32)],
               compiler_params=_params())(pa, pa, pa, pa, pa, w)


def _conv_a_bwd(pa, dya, w, d, name):
    t = pa.shape[0]
    tm, tc = min(512, t), _pick(d, (512, 256, 128))
    nd, nt = d // tc, t // tm

    def body(b_ref, c_ref, v_ref, cp_ref, vp_ref, bn_ref, g_ref, gn_ref, w_ref,
             db_ref, dc_ref, dv_ref, acc_ref, pbuf, dbuf):
        i = pl.program_id(1)

        @pl.when(i == 0)
        def _():
            acc_ref[...] = jnp.zeros_like(acc_ref)
        wv = w_ref[...]
        cv, vv = c_ref[...].astype(F32), v_ref[...].astype(F32)
        prev = cp_ref[...].astype(F32) * vp_ref[...].astype(F32) * (i > 0).astype(F32)
        _fill(pbuf, prev, cv * vv, None, tm)
        g = g_ref[...].astype(F32)
        dq = g * b_ref[...].astype(F32)
        dqn = gn_ref[...].astype(F32) * bn_ref[...].astype(F32) * (i < nt - 1).astype(F32)
        dbuf[0:tm, :] = dq
        dbuf[tm:tm + HALO, :] = dqn
        taps = _taps(pbuf, 3, tm)
        q = _conv_rows(taps, wv)
        db_ref[...] = (g * q).astype(BF16)
        dp = _conv_transpose(dbuf, wv, 3, tm)
        dc_ref[...] = (dp * vv).astype(BF16)
        dv_ref[...] = (dp * cv).astype(BF16)
        _acc_conv_grads(acc_ref, dq, taps, tm, False)

    b_cur, _, b_next = _tile_specs(t, tm, tc, 0)
    c_cur, c_prev, _ = _tile_specs(t, tm, tc, nd)
    v_cur, v_prev, _ = _tile_specs(t, tm, tc, 2 * nd)
    g_cur, _, g_next = _tile_specs(t, tm, tc, 0)
    out = pl.BlockSpec((tm, tc), lambda j, i: (i, j))
    return _pc(body, name=name, grid=(nd, nt),
               in_specs=[b_cur, c_cur, v_cur, c_prev, v_prev, b_next, g_cur, g_next,
                         pl.BlockSpec((3, tc), lambda j, i: (0, j))],
               out_specs=[out, out, out, pl.BlockSpec((8, tc), lambda j, i: (0, j))],
               out_shape=[_sds((t, d), BF16)] * 3 + [_sds((8, d), F32)],
               scratch_shapes=[pltpu.VMEM((tm + 2 * HALO, tc), F32), pltpu.VMEM((tm + HALO, tc), F32)],
               compiler_params=_params())(pa, pa, pa, pa, pa, pa, dya, dya, w)


def _conv_s_fwd(xbc, w, b, name):
    t, dx = xbc.shape
    tm, tc = min(512, t), _pick(dx, (512, 256, 128))

    def body(x_ref, xp_ref, w_ref, b_ref, o_ref, buf):
        i = pl.program_id(1)
        _fill(buf, xp_ref[...].astype(F32) * (i > 0).astype(F32), x_ref[...].astype(F32), None, tm)
        pre = _conv_rows(_taps(buf, 4, tm), w_ref[...]) + b_ref[...]
        o_ref[...] = (pre * _sigmoid(pre)).astype(o_ref.dtype)

    cur, prev, _ = _tile_specs(t, tm, tc, 0)
    return _pc(body, name=name, grid=(dx // tc, t // tm),
               in_specs=[cur, prev, pl.BlockSpec((4, tc), lambda j, i: (0, j)),
                         pl.BlockSpec((1, tc), lambda j, i: (0, j))],
               out_specs=pl.BlockSpec((tm, tc), lambda j, i: (i, j)),
               out_shape=_sds((t, dx), BF16),
               scratch_shapes=[pltpu.VMEM((tm + 2 * HALO, tc), F32)],
               compiler_params=_params())(xbc, xbc, w, b)


def _dsilu(pre):
    s = _sigmoid(pre)
    return s * (1.0 + pre * (1.0 - s))


def _conv_s_bwd(xbc, dxc, w, b, name):
    t, dx = xbc.shape
    tm, tc = min(512, t), _pick(dx, (512, 256, 128))
    nt = t // tm

    def body(x_ref, xp_ref, xn_ref, g_ref, gn_ref, w_ref, b_ref, dx_ref, acc_ref, xbuf, dbuf):
        i = pl.program_id(1)

        @pl.when(i == 0)
        def _():
            acc_ref[...] = jnp.zeros_like(acc_ref)
        wv = w_ref[...]
        _fill(xbuf, xp_ref[...].astype(F32) * (i > 0).astype(F32), x_ref[...].astype(F32),
              xn_ref[...].astype(F32), tm)
        taps = _taps(xbuf, 4, tm + HALO)
        pre = _conv_rows(taps, wv) + b_ref[...]
        ds = _dsilu(pre)
        dcur = g_ref[...].astype(F32) * ds[0:tm, :]
        dnxt = gn_ref[...].astype(F32) * ds[tm:tm + HALO, :] * (i < nt - 1).astype(F32)
        dbuf[0:tm, :] = dcur
        dbuf[tm:tm + HALO, :] = dnxt
        dx_ref[...] = _conv_transpose(dbuf, wv, 4, tm).astype(BF16)
        _acc_conv_grads(acc_ref, dcur, taps, tm, True)

    cur, prev, nxt = _tile_specs(t, tm, tc, 0)
    return _pc(body, name=name, grid=(dx // tc, nt),
               in_specs=[cur, prev, nxt, cur, nxt, pl.BlockSpec((4, tc), lambda j, i: (0, j)),
                         pl.BlockSpec((1, tc), lambda j, i: (0, j))],
               out_specs=[pl.BlockSpec((tm, tc), lambda j, i: (i, j)), pl.BlockSpec((8, tc), lambda j, i: (0, j))],
               out_shape=[_sds((t, dx), BF16), _sds((8, dx), F32)],
               scratch_shapes=[pltpu.VMEM((tm + 2 * HALO, tc), F32), pltpu.VMEM((tm + HALO, tc), F32)],
               compiler_params=_params())(xbc, xbc, xbc, dxc, dxc, w, b)


def _ffn_fwd(hv, w, b, f, name):
    t = hv.shape[0]
    tm, tc = min(512, t), _pick(f, (512, 256, 128))
    nf = f // tc

    def body(h1_ref, h1p_ref, h3_ref, w_ref, b_ref, o_ref, buf):
        i = pl.program_id(1)
        _fill(buf, h1p_ref[...].astype(F32) * (i > 0).astype(F32), h1_ref[...].astype(F32), None, tm)
        c1 = _conv_rows(_taps(buf, 3, tm), w_ref[...]) + b_ref[...]
        o_ref[...] = (c1 * _sigmoid(c1) * h3_ref[...].astype(F32)).astype(o_ref.dtype)

    h1_cur, h1_prev, _ = _tile_specs(t, tm, tc, 0)
    h3_cur, _, _ = _tile_specs(t, tm, tc, nf)
    return _pc(body, name=name, grid=(nf, t // tm),
               in_specs=[h1_cur, h1_prev, h3_cur, pl.BlockSpec((3, tc), lambda j, i: (0, j)),
                         pl.BlockSpec((1, tc), lambda j, i: (0, j))],
               out_specs=pl.BlockSpec((tm, tc), lambda j, i: (i, j)),
               out_shape=_sds((t, f), BF16),
               scratch_shapes=[pltpu.VMEM((tm + 2 * HALO, tc), F32)],
               compiler_params=_params())(hv, hv, hv, w, b)


def _ffn_bwd(hv, dact, w, b, f, name):
    t = hv.shape[0]
    tm, tc = min(512, t), _pick(f, (512, 256, 128))
    nf, nt = f // tc, t // tm

    def body(h1_ref, h1p_ref, h1n_ref, h3_ref, h3n_ref, g_ref, gn_ref, w_ref, b_ref,
             dh1_ref, dh3_ref, acc_ref, xbuf, dbuf):
        i = pl.program_id(1)

        @pl.when(i == 0)
        def _():
            acc_ref[...] = jnp.zeros_like(acc_ref)
        wv = w_ref[...]
        _fill(xbuf, h1p_ref[...].astype(F32) * (i > 0).astype(F32), h1_ref[...].astype(F32),
              h1n_ref[...].astype(F32), tm)
        taps = _taps(xbuf, 3, tm + HALO)
        c1 = _conv_rows(taps, wv) + b_ref[...]
        s = _sigmoid(c1)
        ds = s * (1.0 + c1 * (1.0 - s))
        g = g_ref[...].astype(F32)
        dh3_ref[...] = (g * c1[0:tm, :] * s[0:tm, :]).astype(BF16)
        dcur = g * h3_ref[...].astype(F32) * ds[0:tm, :]
        dnxt = gn_ref[...].astype(F32) * h3n_ref[...].astype(F32) * ds[tm:tm + HALO, :] * (i < nt - 1).astype(F32)
        dbuf[0:tm, :] = dcur
        dbuf[tm:tm + HALO, :] = dnxt
        dh1_ref[...] = _conv_transpose(dbuf, wv, 3, tm).astype(BF16)
        _acc_conv_grads(acc_ref, dcur, taps, tm, True)

    h1_cur, h1_prev, h1_next = _tile_specs(t, tm, tc, 0)
    h3_cur, _, h3_next = _tile_specs(t, tm, tc, nf)
    g_cur, _, g_next = _tile_specs(t, tm, tc, 0)
    out = pl.BlockSpec((tm, tc), lambda j, i: (i, j))
    return _pc(body, name=name, grid=(nf, nt),
               in_specs=[h1_cur, h1_prev, h1_next, h3_cur, h3_next, g_cur, g_next,
                         pl.BlockSpec((3, tc), lambda j, i: (0, j)), pl.BlockSpec((1, tc), lambda j, i: (0, j))],
               out_specs=[out, out, pl.BlockSpec((8, tc), lambda j, i: (0, j))],
               out_shape=[_sds((t, f), BF16), _sds((t, f), BF16), _sds((8, f), F32)],
               scratch_shapes=[pltpu.VMEM((tm + 2 * HALO, tc), F32), pltpu.VMEM((tm + HALO, tc), F32)],
               compiler_params=_params())(hv, hv, hv, hv, hv, dact, dact, w, b)


def _gnorm_fwd(y, z, w, name):
    t, di = y.shape
    gw = di // N_GROUPS
    tm = min(512, t)

    def body(y_ref, z_ref, w_ref, o_ref):
        zv = z_ref[...].astype(F32)
        yz = y_ref[...].astype(F32) * zv * _sigmoid(zv)
        r = lax.rsqrt(jnp.mean(yz * yz, axis=-1, keepdims=True) + EPS)
        o_ref[...] = (yz * r * w_ref[...]).astype(o_ref.dtype)

    blk = pl.BlockSpec((tm, gw), lambda j, i: (i, j))
    return _pc(body, name=name, grid=(N_GROUPS, t // tm),
               in_specs=[blk, blk, pl.BlockSpec((1, gw), lambda j, i: (0, j))],
               out_specs=blk, out_shape=_sds((t, di), BF16), compiler_params=_params())(y, z, w)


def _gnorm_bwd(y, z, dyn, w, name):
    t, di = y.shape
    gw = di // N_GROUPS
    tm = min(512, t)

    def body(y_ref, z_ref, g_ref, w_ref, dy_ref, dz_ref, dw_ref):
        @pl.when(pl.program_id(1) == 0)
        def _():
            dw_ref[...] = jnp.zeros_like(dw_ref)
        yv, zv, g = y_ref[...].astype(F32), z_ref[...].astype(F32), g_ref[...].astype(F32)
        s = _sigmoid(zv)
        sz = zv * s
        yz = yv * sz
        r = lax.rsqrt(jnp.mean(yz * yz, axis=-1, keepdims=True) + EPS)
        n = yz * r
        dn = g * w_ref[...]
        dw_ref[0:1, :] += jnp.sum(g * n, axis=0, keepdims=True)
        dyz = r * (dn - n * jnp.mean(dn * n, axis=-1, keepdims=True))
        dy_ref[...] = (dyz * sz).astype(BF16)
        dz_ref[...] = (dyz * yv * s * (1.0 + zv * (1.0 - s))).astype(BF16)

    blk = pl.BlockSpec((tm, gw), lambda j, i: (i, j))
    return _pc(body, name=name, grid=(N_GROUPS, t // tm),
               in_specs=[blk, blk, blk, pl.BlockSpec((1, gw), lambda j, i: (0, j))],
               out_specs=[blk, blk, pl.BlockSpec((8, gw), lambda j, i: (0, j))],
               out_shape=[_sds((t, di), BF16), _sds((t, di), BF16), _sds((8, di), F32)],
               compiler_params=_params())(y, z, dyn, w)


def _merge_fwd(gates, ya, ys, d, name):
    t = ya.shape[0]
    tm, tc = min(512, t), _pick(d, (512, 256, 128))
    nd = d // tc

    def body(ga_ref, gs_ref, ya_ref, ys_ref, o_ref):
        o_ref[...] = (_sigmoid(ga_ref[...].astype(F32)) * ya_ref[...].astype(F32)
                      + _sigmoid(gs_ref[...].astype(F32)) * ys_ref[...].astype(F32)).astype(o_ref.dtype)

    blk = pl.BlockSpec((tm, tc), lambda j, i: (i, j))
    return _pc(body, name=name, grid=(nd, t // tm),
               in_specs=[blk, pl.BlockSpec((tm, tc), lambda j, i: (i, nd + j)), blk, blk],
               out_specs=blk, out_shape=_sds((t, d), BF16), compiler_params=_params())(gates, gates, ya, ys)


def _merge_bwd(dm, gates, ya, ys, d, name):
    t = ya.shape[0]
    tm, tc = min(512, t), _pick(d, (512, 256, 128))
    nd = d // tc

    def body(dm_ref, ga_ref, gs_ref, ya_ref, ys_ref, dya_ref, dys_ref, dga_ref, dgs_ref):
        g = dm_ref[...].astype(F32)
        sa, ss = _sigmoid(ga_ref[...].astype(F32)), _sigmoid(gs_ref[...].astype(F32))
        dya_ref[...] = (g * sa).astype(BF16)
        dys_ref[...] = (g * ss).astype(BF16)
        dga_ref[...] = (g * ya_ref[...].astype(F32) * sa * (1.0 - sa)).astype(BF16)
        dgs_ref[...] = (g * ys_ref[...].astype(F32) * ss * (1.0 - ss)).astype(BF16)

    blk = pl.BlockSpec((tm, tc), lambda j, i: (i, j))
    return _pc(body, name=name, grid=(nd, t // tm),
               in_specs=[blk, blk, pl.BlockSpec((tm, tc), lambda j, i: (i, nd + j)), blk, blk],
               out_specs=[blk] * 4, out_shape=[_sds((t, d), BF16)] * 4,
               compiler_params=_params())(dm, gates, gates, ya, ys)


def _ssd_chunk_terms(dtr, dtb, alog):
    xx = dtr + dtb
    dt = jnp.maximum(xx, 0.0) + jnp.log(1.0 + jnp.exp(-jnp.abs(xx)))
    a = -jnp.exp(alog)
    li = lax.broadcasted_iota(jnp.int32, (CHUNK, CHUNK), 0)
    si = lax.broadcasted_iota(jnp.int32, (CHUNK, CHUNK), 1)
    causal = li >= si
    acum = _dot(causal.astype(F32), dt * a, 1, 0, HIGHEST)
    return xx, dt, a, acum, acum.T, causal


def _split3(x):
    hi = x.astype(BF16)
    r = x - hi.astype(F32)
    mid = r.astype(BF16)
    lo = (r - mid.astype(F32)).astype(BF16)
    return hi, mid, lo


def _expand(v, e):
    hi, mid, lo = _split3(v)
    return _dot(hi, e, 1, 0) + _dot(mid, e, 1, 0) + _dot(lo, e, 1, 0)


def _segsum(s, e):
    hi, mid, lo = _split3(s)
    return _dot(hi, e, 1, 1) + _dot(mid, e, 1, 1) + _dot(lo, e, 1, 1)


def _head_maps(di):
    nh = di // HEAD_DIM
    h = jnp.arange(DT_LANES)[:, None]
    e64 = (jnp.arange(di)[None, :] // HEAD_DIM == h).astype(BF16)
    e128 = (jnp.arange(nh * CHUNK)[None, :] // CHUNK == h).astype(BF16)
    return e64, e128


def _pair_blockdiag(p, left):
    zero = jnp.zeros_like(p)
    return jnp.concatenate([jnp.where(left, p, zero), jnp.where(left, zero, p)], axis=0)


def _ssd_fwd(xc, dtr, dtb, alog, dskx, di, name):
    t = xc.shape[0]
    dx = xc.shape[1]
    nc = t // CHUNK
    nh = di // HEAD_DIM
    hpg = nh // N_GROUPS
    gw = hpg * HEAD_DIM
    boff, coff = di, di + N_GROUPS * D_STATE
    e64, e128 = _head_maps(di)

    def body(xc_ref, dtr_ref, dtb_ref, alog_ref, dsk_ref, e64_ref, e128_ref, y_ref, st_ref, state):
        @pl.when(pl.program_id(0) == 0)
        def _():
            state[...] = jnp.zeros_like(state)
        _, dt, _, acum, acum_t, causal = _ssd_chunk_terms(dtr_ref[...], dtb_ref[...], alog_ref[...])
        last = acum[CHUNK - 1:CHUNK, :]
        e64v = e64_ref[...]
        dtx = _expand(dt, e64v)
        eax = _expand(jnp.exp(acum), e64v)
        dex = _expand(dt * jnp.exp(last - acum), e64v)
        acx = _expand(acum, e128_ref[...])
        st_ref[0] = state[...]
        left = lax.broadcasted_iota(jnp.int32, (CHUNK, 2 * HEAD_DIM), 1) < HEAD_DIM
        for g in range(N_GROUPS):
            gs = slice(g * gw, (g + 1) * gw)
            bg = xc_ref[:, boff + g * D_STATE:boff + (g + 1) * D_STATE]
            cg = xc_ref[:, coff + g * D_STATE:coff + (g + 1) * D_STATE]
            gm = _dot(cg, bg, 1, 1)
            xg = xc_ref[:, gs].astype(F32)
            xdb = (xg * dtx[:, gs]).astype(BF16)
            sin = state[:, gs]
            yo = _dot(cg, sin.astype(BF16), 1, 0) * eax[:, gs]
            for jp in range(hpg // 2):
                h0 = g * hpg + 2 * jp
                ps = slice(jp * 2 * HEAD_DIM, (jp + 1) * 2 * HEAD_DIM)
                ms = []
                for hh in (h0, h0 + 1):
                    seg = acx[:, hh * CHUNK:(hh + 1) * CHUNK] - acum_t[hh:hh + 1, :]
                    ms.append((gm * jnp.exp(jnp.where(causal, seg, -1e30))).astype(BF16))
                yd = _dot(jnp.concatenate(ms, axis=1), _pair_blockdiag(xdb[:, ps], left), 1, 0)
                col = slice(g * gw + jp * 2 * HEAD_DIM, g * gw + (jp + 1) * 2 * HEAD_DIM)
                y_ref[:, col] = (yd + yo[:, ps] + dsk_ref[:, col] * xg[:, ps]).astype(y_ref.dtype)
            xe = (xg * dex[:, gs]).astype(BF16)
            state[:, gs] = eax[CHUNK - 1:CHUNK, gs] * sin + _dot(bg, xe, 0, 0)

    small = pl.BlockSpec((1, DT_LANES), lambda c: (0, 0))
    whole = lambda a: pl.BlockSpec(a.shape, lambda c: (0, 0))
    return _pc(body, name=name, grid=(nc,),
               in_specs=[pl.BlockSpec((CHUNK, dx), lambda c: (c, 0)),
                         pl.BlockSpec((CHUNK, DT_LANES), lambda c: (c, 0)), small, small,
                         whole(dskx), whole(e64), whole(e128)],
               out_specs=[pl.BlockSpec((CHUNK, di), lambda c: (c, 0)),
                          pl.BlockSpec((1, D_STATE, di), lambda c: (c, 0, 0))],
               out_shape=[_sds((t, di), BF16), _sds((nc, D_STATE, di), F32)],
               scratch_shapes=[pltpu.VMEM((D_STATE, di), F32)],
               compiler_params=_params())(xc, dtr, dtb, alog, dskx, e64, e128)


def _ssd_bwd(xc, dtr, dy, states, dtb, alog, dskx, di, name):
    t = xc.shape[0]
    dx = xc.shape[1]
    nc = t // CHUNK
    nh = di // HEAD_DIM
    hpg = nh // N_GROUPS
    gw = hpg * HEAD_DIM
    boff, coff = di, di + N_GROUPS * D_STATE
    e64, e128 = _head_maps(di)

    def body(xc_ref, dtr_ref, dy_ref, st_ref, dtb_ref, alog_ref, dsk_ref, e64_ref, e128_ref,
             dxc_ref, ddtr_ref, sm_ref, dstate, darow):
        @pl.when(pl.program_id(0) == 0)
        def _():
            dstate[...] = jnp.zeros_like(dstate)
            sm_ref[...] = jnp.zeros_like(sm_ref)
        darow[...] = jnp.zeros_like(darow)
        xx, dt, a, acum, acum_t, causal = _ssd_chunk_terms(dtr_ref[...], dtb_ref[...], alog_ref[...])
        last = acum[CHUNK - 1:CHUNK, :]
        e64v = e64_ref[...]
        dtx = _expand(dt, e64v)
        eax = _expand(jnp.exp(acum), e64v)
        eex = _expand(jnp.exp(last - acum), e64v)
        acx = _expand(acum, e128_ref[...])
        left = lax.broadcasted_iota(jnp.int32, (CHUNK, 2 * HEAD_DIM), 1) < HEAD_DIM
        lane = lax.broadcasted_iota(jnp.int32, (CHUNK, DT_LANES), 1)
        sub8 = lax.broadcasted_iota(jnp.int32, (8, gw), 0)
        da_col = jnp.zeros((CHUNK, DT_LANES), F32)
        ddt_col = jnp.zeros((CHUNK, DT_LANES), F32)
        rows = jnp.zeros((8, DT_LANES), F32)
        for g in range(N_GROUPS):
            gs = slice(g * gw, (g + 1) * gw)
            bg = xc_ref[:, boff + g * D_STATE:boff + (g + 1) * D_STATE]
            cg = xc_ref[:, coff + g * D_STATE:coff + (g + 1) * D_STATE]
            gm = _dot(cg, bg, 1, 1)
            e64g = e64v[:, gs]
            xg = xc_ref[:, gs].astype(F32)
            dtg, eag, eeg = dtx[:, gs], eax[:, gs], eex[:, gs]
            xd = xg * dtg
            xdb = xd.astype(BF16)
            dyb = dy_ref[:, gs]
            dyf = dyb.astype(F32)
            sin = st_ref[0, :, gs]
            sinb = sin.astype(BF16)
            ds = dstate[:, gs]
            dsb = ds.astype(BF16)
            bds = _dot(bg, dsb, 1, 0)
            dyeb = (dyf * eag).astype(BF16)
            dcg = _dot(dyeb, sinb, 1, 1)
            dstate[:, gs] = eag[CHUNK - 1:CHUNK, :] * ds + _dot(cg, dyeb, 0, 0)
            yo = _dot(cg, sinb, 1, 0) * eag
            xe = xd * eeg
            dbg = _dot(xe.astype(BF16), dsb, 1, 1)
            wterm = bds * xe
            da_col = da_col + _segsum(dyf * yo - wterm, e64g)
            dg = jnp.zeros((CHUNK, CHUNK), F32)
            dxd_parts = []
            for jp in range(hpg // 2):
                h0 = g * hpg + 2 * jp
                ps = slice(jp * 2 * HEAD_DIM, (jp + 1) * 2 * HEAD_DIM)
                lms, mfs = [], []
                for hh in (h0, h0 + 1):
                    seg = acx[:, hh * CHUNK:(hh + 1) * CHUNK] - acum_t[hh:hh + 1, :]
                    lm = jnp.exp(jnp.where(causal, seg, -1e30))
                    lms.append(lm)
                    mfs.append(gm * lm)
                mstack = jnp.concatenate([m.astype(BF16) for m in mfs], axis=0)
                dyp = dyb[:, ps]
                dxd_parts.append(_dot(mstack, _pair_blockdiag(dyp, left), 0, 0))
                dm2 = _dot(dyp, _pair_blockdiag(xdb[:, ps], left), 1, 1)
                for k, hh in enumerate((h0, h0 + 1)):
                    dm = dm2[:, k * CHUNK:(k + 1) * CHUNK]
                    dg = dg + dm * lms[k]
                    q = dm * mfs[k]
                    da_col = da_col + jnp.where(lane == hh, jnp.sum(q, axis=1, keepdims=True), 0.0)
                    darow[hh:hh + 1, :] = -jnp.sum(q, axis=0, keepdims=True)
            dxd = jnp.concatenate(dxd_parts, axis=1) + bds * eeg
            ddt_col = ddt_col + _segsum(dxd * xg, e64g)
            rsum = (jnp.where(sub8 == 0, jnp.sum(wterm, axis=0, keepdims=True), 0.0)
                    + jnp.where(sub8 == 1, jnp.sum(ds * sin, axis=0, keepdims=True), 0.0)
                    + jnp.where(sub8 == 2, jnp.sum(dyf * xg, axis=0, keepdims=True), 0.0))
            rows = rows + _segsum(rsum, e64g)
            dxc_ref[:, gs] = (dxd * dtg + dsk_ref[:, gs] * dyf).astype(dxc_ref.dtype)
            dgb = dg.astype(BF16)
            dxc_ref[:, boff + g * D_STATE:boff + (g + 1) * D_STATE] = (
                dbg + _dot(dgb, cg, 0, 0)).astype(dxc_ref.dtype)
            dxc_ref[:, coff + g * D_STATE:coff + (g + 1) * D_STATE] = (
                dcg + _dot(dgb, bg, 1, 0)).astype(dxc_ref.dtype)
        at_last = rows[0:1, :] + jnp.exp(last) * rows[1:2, :]
        is_last = lax.broadcasted_iota(jnp.int32, (CHUNK, DT_LANES), 0) == CHUNK - 1
        da = da_col + jnp.where(is_last, at_last, 0.0) + darow[...].T
        li = lax.broadcasted_iota(jnp.int32, (CHUNK, CHUNK), 0)
        si = lax.broadcasted_iota(jnp.int32, (CHUNK, CHUNK), 1)
        dla = _dot((si >= li).astype(F32), da, 1, 0, HIGHEST)
        ddtr = (ddt_col + dla * a) * _sigmoid(xx)
        ddtr_ref[...] = ddtr
        sm_ref[0:1, :] += jnp.sum(ddtr, axis=0, keepdims=True)
        sm_ref[1:2, :] += jnp.sum(dla * dt, axis=0, keepdims=True) * a
        sm_ref[2:3, :] += rows[2:3, :]

    small = pl.BlockSpec((1, DT_LANES), lambda c: (0, 0))
    whole = lambda a: pl.BlockSpec(a.shape, lambda c: (0, 0))
    rev = lambda c: (nc - 1 - c, 0)
    return _pc(body, name=name, grid=(nc,),
               in_specs=[pl.BlockSpec((CHUNK, dx), rev), pl.BlockSpec((CHUNK, DT_LANES), rev),
                         pl.BlockSpec((CHUNK, di), rev),
                         pl.BlockSpec((1, D_STATE, di), lambda c: (nc - 1 - c, 0, 0)), small, small,
                         whole(dskx), whole(e64), whole(e128)],
               out_specs=[pl.BlockSpec((CHUNK, dx), rev), pl.BlockSpec((CHUNK, DT_LANES), rev),
                          pl.BlockSpec((8, DT_LANES), lambda c: (0, 0))],
               out_shape=[_sds((t, dx), BF16), _sds((t, DT_LANES), F32), _sds((8, DT_LANES), F32)],
               scratch_shapes=[pltpu.VMEM((D_STATE, di), F32), pltpu.VMEM((DT_LANES, CHUNK), F32)],
               compiler_params=_params())(xc, dtr, dy, states, dtb, alog, dskx, e64, e128)


def _adamw(parts, w, m, v, name):
    npart, rows, width = parts.shape
    tr = _pick(rows, (64, 32, 16, 8))
    c1 = 1.0 - ADAM_B1 ** ADAM_STEP
    c2 = 1.0 - ADAM_B2 ** ADAM_STEP

    def body(p_ref, w_ref, m_ref, v_ref, g_ref, d_ref, nm_ref, nv_ref):
        g = p_ref[0].astype(F32)
        for p in range(1, npart):
            g = g + p_ref[p].astype(F32)
        nm = ADAM_B1 * m_ref[...] + (1.0 - ADAM_B1) * g
        nv = ADAM_B2 * v_ref[...] + (1.0 - ADAM_B2) * (g * g)
        g_ref[...] = g
        nm_ref[...] = nm
        nv_ref[...] = nv
        d_ref[...] = -ADAM_LR * ((nm / c1) / (jnp.sqrt(nv / c2) + ADAM_EPS) + ADAM_WD * w_ref[...])

    blk = pl.BlockSpec((tr, width), lambda i: (i, 0))
    return _pc(body, name=name, grid=(rows // tr,),
               in_specs=[pl.BlockSpec((npart, tr, width), lambda i: (0, i, 0)), blk, blk, blk],
               out_specs=[blk] * 4, out_shape=[_sds((rows, width), F32)] * 4,
               compiler_params=_params())(parts, w, m, v)


def _sum_parts(parts, name, tile=None):
    npart, rows, width = parts.shape
    tile = rows if tile is None else tile

    def body(p_ref, o_ref):
        g = p_ref[0].astype(F32)
        for p in range(1, npart):
            g = g + p_ref[p].astype(F32)
        o_ref[...] = g

    return _pc(body, name=name, grid=(rows // tile,),
               in_specs=[pl.BlockSpec((npart, tile, width), lambda i: (0, i, 0))],
               out_specs=pl.BlockSpec((tile, width), lambda i: (i, 0)),
               out_shape=_sds((rows, width), F32), compiler_params=_params())(parts)


def _peers():
    x, y, c = lax.axis_index("x"), lax.axis_index("y"), lax.axis_index("c")
    out = []
    for k in range(1, N_DEV):
        px = 1 - x if k & 4 else x
        py = 1 - y if k & 2 else y
        pc = 1 - c if k & 1 else c
        out.append(((px, py, pc), 4 * px + 2 * py + pc))
    return 4 * x + 2 * y + c, out


def _copies(arrays, lands, send_sems, recv_sems, scatter):
    me, peers = _peers()
    outgoing, incoming = [], []
    for k, (peer, pidx) in enumerate(peers):
        for j, (a_ref, land_ref) in enumerate(zip(arrays, lands)):
            src = a_ref.at[pidx] if scatter[j] else a_ref
            sem = len(arrays) * k + j
            for dst, bucket in ((land_ref.at[me], outgoing), (land_ref.at[pidx], incoming)):
                bucket.append(pltpu.make_async_remote_copy(
                    src_ref=src, dst_ref=dst, send_sem=send_sems.at[sem], recv_sem=recv_sems.at[sem],
                    device_id=peer, device_id_type=MESH))
    return outgoing, incoming


HBM_SPEC = pl.BlockSpec(memory_space=pltpu.HBM)
SEM_SPEC = pl.BlockSpec(memory_space=pltpu.SEMAPHORE)
ANY_SPEC = pl.BlockSpec(memory_space=pl.ANY)
EFFECT = pltpu.SideEffectType.DATAFLOW_SIDE_EFFECTING


def _xchg_start(arrays, scatter, after, name):
    n = len(arrays)
    me = 4 * lax.axis_index("x") + 2 * lax.axis_index("y") + lax.axis_index("c")
    lands = []
    for a, sc in zip(arrays, scatter):
        own = lax.dynamic_index_in_dim(a, me, 0, keepdims=True) if sc else a[None]
        shape = a.shape if sc else (N_DEV,) + a.shape
        lands.append(lax.dynamic_update_slice(lax.empty(shape, a.dtype), own, (me,) + (0,) * (len(shape) - 1)))

    def body(*refs):
        ins, outs = refs[:2 * n], refs[2 * n + 1:]
        outgoing, _ = _copies(ins[:n], ins[n:], outs[0], outs[1], scatter)
        for cp in outgoing:
            cp.start()
        outs[-1][...] = jnp.zeros_like(outs[-1])

    nsem = n * (N_DEV - 1)
    operands = [pltpu.with_memory_space_constraint(a, pltpu.HBM) for a in list(arrays) + lands]
    out = _pc(body, name=name,
              out_shape=(pltpu.SemaphoreType.DMA((nsem,)), pltpu.SemaphoreType.DMA((nsem,)),
                         *[pltpu.HBM(a.shape, a.dtype) for a in operands], _sds((8, 128), F32)),
              in_specs=[HBM_SPEC] * (2 * n) + [ANY_SPEC],
              out_specs=(SEM_SPEC, SEM_SPEC, *[HBM_SPEC] * (2 * n), pl.BlockSpec(memory_space=pltpu.VMEM)),
              input_output_aliases={i: 2 + i for i in range(2 * n)},
              compiler_params=pltpu.CompilerParams(has_side_effects=EFFECT))(*operands, after)
    return dict(sems=out[:2], thru=out[2:2 + 2 * n], token=out[-1], scatter=scatter, n=n)


def _xchg_wait(handle, after, name):
    n, scatter = handle["n"], handle["scatter"]

    def body(*refs):
        ins = refs[:2 * n]
        send_sems, recv_sems = refs[2 * n], refs[2 * n + 1]
        outgoing, incoming = _copies(ins[:n], ins[n:], send_sems, recv_sems, scatter)
        for cp in outgoing:
            cp.wait_send()
        for cp in incoming:
            cp.wait_recv()

    thru = handle["thru"]
    out = _pc(body, name=name, out_shape=tuple(pltpu.HBM(a.shape, a.dtype) for a in thru),
              in_specs=[HBM_SPEC] * (2 * n) + [SEM_SPEC, SEM_SPEC, ANY_SPEC], out_specs=tuple([HBM_SPEC] * (2 * n)),
              input_output_aliases={i: i for i in range(2 * n)},
              compiler_params=pltpu.CompilerParams(has_side_effects=EFFECT))(*thru, *handle["sems"], after)
    return out[n:]


def _pack(arrs, width, row_mult):
    flat = jnp.concatenate([a.reshape(-1) for a in arrs])
    n = flat.shape[0]
    rows = -(-n // (width * row_mult)) * row_mult
    return jnp.pad(flat, (0, rows * width - n)).reshape(rows, width)


def _unpack(packed, shapes, lead=None):
    out, off = [], 0
    flat = packed.reshape(-1) if lead is None else packed.reshape(lead, -1)
    for s in shapes:
        n = math.prod(s)
        if lead is None:
            out.append(flat[off:off + n].reshape(s))
        else:
            out.append(flat[:, off:off + n].reshape((lead,) + tuple(s)))
        off += n
    return out


def _blocks_to_cols(blocks):
    nb, rows, n = blocks.shape
    return blocks.transpose(1, 0, 2).reshape(rows, nb * n)


def _pad_rows(a, rows):
    return jnp.pad(a, ((0, rows - a.shape[0]), (0, 0)))


def _pad_lanes(a, lanes):
    return jnp.pad(a, ((0, 0), (0, lanes - a.shape[1])))


REST = ("w_a_out", "w_s_out", "w_o", "w_up", "w_down")
TRANSPOSED = ("w_up", "w_in")
ROW_ALIGN = 32
CONVS = ("conv_a_w", "ssd_conv_w", "ffn_conv_w")
REPL = ("norm_mix_w", "ssd_conv_b", "dt_bias", "a_log", "d_skip", "ssd_norm_w", "norm_ffn_w", "ffn_conv_b",
        "final_norm_w")
ORDER = ("norm_mix_w", "w_in", "conv_a_w", "w_a_out", "ssd_conv_w", "ssd_conv_b", "dt_bias", "a_log", "d_skip",
         "ssd_norm_w", "w_s_out", "w_o", "norm_ffn_w", "w_up", "ffn_conv_w", "ffn_conv_b", "w_down", "final_norm_w")


def _as_rows(name, block):
    return block[0].T if name in TRANSPOSED else block[0]


def kernel(x, norm_mix_w, w_in, conv_a_w, w_a_out, ssd_conv_w, ssd_conv_b, dt_bias, a_log, d_skip, ssd_norm_w, w_s_out, w_o, norm_ffn_w, w_up, ffn_conv_w, ffn_conv_b, w_down, final_norm_w, loss_target, m_norm_mix_w, m_w_in, m_conv_a_w, m_w_a_out, m_ssd_conv_w, m_ssd_conv_b, m_dt_bias, m_a_log, m_d_skip, m_ssd_norm_w, m_w_s_out, m_w_o, m_norm_ffn_w, m_w_up, m_ffn_conv_w, m_ffn_conv_b, m_w_down, m_final_norm_w, v_norm_mix_w, v_w_in, v_conv_a_w, v_w_a_out, v_ssd_conv_w, v_ssd_conv_b, v_dt_bias, v_a_log, v_d_skip, v_ssd_norm_w, v_w_s_out, v_w_o, v_norm_ffn_w, v_w_up, v_ffn_conv_w, v_ffn_conv_b, v_w_down, v_final_norm_w):
    wts = dict(norm_mix_w=norm_mix_w, w_in=w_in, conv_a_w=conv_a_w, w_a_out=w_a_out, ssd_conv_w=ssd_conv_w,
               ssd_conv_b=ssd_conv_b, dt_bias=dt_bias, a_log=a_log, d_skip=d_skip, ssd_norm_w=ssd_norm_w,
               w_s_out=w_s_out, w_o=w_o, norm_ffn_w=norm_ffn_w, w_up=w_up, ffn_conv_w=ffn_conv_w,
               ffn_conv_b=ffn_conv_b, w_down=w_down, final_norm_w=final_norm_w)
    mom1 = dict(norm_mix_w=m_norm_mix_w, w_in=m_w_in, conv_a_w=m_conv_a_w, w_a_out=m_w_a_out,
                ssd_conv_w=m_ssd_conv_w, ssd_conv_b=m_ssd_conv_b, dt_bias=m_dt_bias, a_log=m_a_log, d_skip=m_d_skip,
                ssd_norm_w=m_ssd_norm_w, w_s_out=m_w_s_out, w_o=m_w_o, norm_ffn_w=m_norm_ffn_w, w_up=m_w_up,
                ffn_conv_w=m_ffn_conv_w, ffn_conv_b=m_ffn_conv_b, w_down=m_w_down, final_norm_w=m_final_norm_w)
    mom2 = dict(norm_mix_w=v_norm_mix_w, w_in=v_w_in, conv_a_w=v_conv_a_w, w_a_out=v_w_a_out,
                ssd_conv_w=v_ssd_conv_w, ssd_conv_b=v_ssd_conv_b, dt_bias=v_dt_bias, a_log=v_a_log, d_skip=v_d_skip,
                ssd_norm_w=v_ssd_norm_w, w_s_out=v_w_s_out, w_o=v_w_o, norm_ffn_w=v_norm_ffn_w, w_up=v_w_up,
                ffn_conv_w=v_ffn_conv_w, ffn_conv_b=v_ffn_conv_b, w_down=v_w_down, final_norm_w=v_final_norm_w)

    t, d = x.shape[1], x.shape[2]
    di = 2 * d
    nh = di // HEAD_DIM
    dxw = di + 2 * N_GROUPS * D_STATE
    f = w_down.shape[1] * N_DEV
    n_in = w_in.shape[2] * N_DEV
    me = 4 * lax.axis_index("x") + 2 * lax.axis_index("y") + lax.axis_index("c")

    rest_rows = [_as_rows(k, wts[k]) for k in REST]
    nrows = [a.shape[0] for a in rest_rows]
    offs = [sum(nrows[:i]) for i in range(len(REST))]
    assert all(o % ROW_ALIGN == 0 for o in offs)
    r_rest = -(-sum(nrows) // ROW_ALIGN) * ROW_ALIGN
    n_blk = w_in.shape[2]
    r_in = -(-n_blk // ROW_ALIGN) * ROW_ALIGN
    in_local = _pad_rows(w_in[0].T, r_in).astype(BF16)
    rest_local = _pad_rows(jnp.concatenate(rest_rows, axis=0), r_rest).astype(BF16)
    conv_shapes = [wts[k].shape[1:] for k in CONVS]
    conv_local = _pack([wts[k] for k in CONVS], d, 8)
    x2, tgt = x[0], loss_target[0]
    h_in = _xchg_start([in_local, conv_local], [False, False], x2, "gather_in_start")
    u = _rms_fwd(x2, norm_mix_w, "norm_mix")
    in_all, conv_all = _xchg_wait(h_in, u, "gather_in_wait")
    win_t = in_all[:, :n_blk].reshape(n_in, d)
    h_rest = _xchg_start([rest_local], [False], in_all, "gather_rest_start")
    c_a, c_s, c_f = _unpack(conv_all, conv_shapes, N_DEV)
    caw, scw, fcw = _blocks_to_cols(c_a), _blocks_to_cols(c_s), _blocks_to_cols(c_f)

    o_z, o_x, o_dt = 5 * d, 7 * d, 7 * d + dxw
    seg_bounds = [0, d, 2 * d, 3 * d, 4 * d, o_z, o_x, o_dt]
    w_dt = _pad_rows(win_t[o_dt:], DT_LANES)
    dtb, alog = (_pad_lanes(p[...].reshape(1, nh), DT_LANES) for p in (dt_bias, a_log))
    dskx = jnp.repeat(d_skip.reshape(1, nh), HEAD_DIM, axis=1)

    tok = h_rest["token"]
    gates = _mm([(u, win_t[:2 * d])], "nt", BF16, "proj_gates", after=tok)
    pa = _mm([(u, win_t[2 * d:o_z])], "nt", BF16, "proj_a", after=tok)
    z = _mm([(u, win_t[o_z:o_x])], "nt", BF16, "proj_z", after=tok)
    xbc = _mm([(u, win_t[o_x:o_dt])], "nt", BF16, "proj_xbc", after=tok)
    dtr = _mm([(u, w_dt)], "nt", F32, "proj_dt", after=tok)
    ya_in = _conv_a_fwd(pa, caw, d, "conv_a")
    xc = _conv_s_fwd(xbc, scw, ssd_conv_b, "conv_s")
    y, states = _ssd_fwd(xc, dtr, dtb, alog, dskx, di, "ssd")
    yn = _gnorm_fwd(y, z, ssd_norm_w, "gnorm")
    (rest_all,) = _xchg_wait(h_rest, yn, "gather_rest_wait")
    full = {k: rest_all[:, o:o + n].reshape(N_DEV * n, d) for k, o, n in zip(REST, offs, nrows)}
    waout, wsout, wo, wup_t, wdown = (full[k] for k in REST)
    y_a = _mm([(ya_in, waout)], "nn", BF16, "a_out")
    y_s = _mm([(yn, wsout)], "nn", BF16, "s_out")
    merged = _merge_fwd(gates, y_a, y_s, d, "merge")
    mo = _mm([(merged, wo)], "nn", F32, "o_proj")
    h1, v = _resnorm_fwd(x2, mo, norm_ffn_w, "norm_ffn")
    hv = _mm([(v, wup_t)], "nt", BF16, "up_proj")
    act = _ffn_fwd(hv, fcw, ffn_conv_b, f, "ffn_act")
    dd = _mm([(act, wdown)], "nn", F32, "down_proj")
    loss11, dh2, dh2b, g_fnw = _final(h1, dd, tgt, final_norm_w.reshape(1, d), "final")

    dact = _mm([(dh2b, wdown)], "nt", BF16, "d_act")
    gw_down = _mm_tn(act, dh2b, "gw_down")
    dh1f, dh3, g_ffn = _ffn_bwd(hv, dact, fcw, ffn_conv_b, f, "ffn_act_bwd")
    dv = _mm([(dh1f, wup_t[:f]), (dh3, wup_t[f:])], "nn", F32, "d_v")
    gw_up_t = jnp.concatenate([_mm_tn(dh1f, v, "gw_up1"), _mm_tn(dh3, v, "gw_up3")], axis=0)
    dh1, dh1b, g_nfw = _rms_bwd(h1, dv, norm_ffn_w, dh2, "norm_ffn_bwd")
    dmerged = _mm([(dh1b, wo)], "nt", BF16, "d_merged")
    gw_o = _mm_tn(merged, dh1b, "gw_o")
    dya, dys, dga, dgs = _merge_bwd(dmerged, gates, y_a, y_s, d, "merge_bwd")
    dyain = _mm([(dya, waout)], "nt", BF16, "d_ya_in")
    gw_aout = _mm_tn(ya_in, dya, "gw_a_out")
    db, dc, dvv, g_caw = _conv_a_bwd(pa, dyain, caw, d, "conv_a_bwd")
    dyn = _mm([(dys, wsout)], "nt", BF16, "d_yn")
    gw_sout = _mm_tn(yn, dys, "gw_s_out")
    grads_rest = dict(w_a_out=gw_aout, w_s_out=gw_sout, w_o=gw_o, w_up=gw_up_t, w_down=gw_down)
    rest_parts = jnp.concatenate([grads_rest[k].reshape(N_DEV, n, d) for k, n in zip(REST, nrows)], axis=1)
    rest_parts = jnp.pad(rest_parts, ((0, 0), (0, r_rest - sum(nrows)), (0, 0))).astype(BF16)
    h_grest = _xchg_start([rest_parts], [True], rest_parts, "scatter_rest_start")
    dy, dz, g_snw = _gnorm_bwd(y, z, dyn, ssd_norm_w, "gnorm_bwd")
    dtb_after = dtb + h_grest["token"][0:1, 0:1]
    dxc, ddtr, g_ssd = _ssd_bwd(xc, dtr, dy, states, dtb_after, alog, dskx, di, "ssd_bwd")
    dxbc, g_scw = _conv_s_bwd(xbc, dxc, scw, ssd_conv_b, "conv_s_bwd")
    dsegs = [dga, dgs, db, dc, dvv, dz, dxbc]
    pairs = [(s, win_t[a:b]) for s, a, b in zip(dsegs, seg_bounds[:-1], seg_bounds[1:])]
    pairs.append((ddtr.astype(BF16), w_dt))
    du = _mm(pairs, "nn", F32, "d_u", tm=512, tn=512)
    gw_in_t = jnp.concatenate([_mm_tn(s, u, "gw_in%d" % i) for i, (s, _) in enumerate(pairs)], axis=0)[:n_in]
    in_parts = jnp.pad(gw_in_t.reshape(N_DEV, n_blk, d), ((0, 0), (0, r_in - n_blk), (0, 0))).astype(BF16)
    h_gin = _xchg_start([in_parts], [True], in_parts, "scatter_in_start")
    dx, _, g_nmw = _rms_bwd(x2, du, norm_mix_w, dh1, "norm_mix_bwd")

    small_grads = dict(norm_mix_w=g_nmw[0], ssd_conv_b=g_scw[4], dt_bias=g_ssd[0, :nh], a_log=g_ssd[1, :nh],
                       d_skip=g_ssd[2, :nh], ssd_norm_w=g_snw[0], norm_ffn_w=g_nfw[0], ffn_conv_b=g_ffn[3],
                       final_norm_w=g_fnw[0], conv_a_w=g_caw[:3], ssd_conv_w=g_scw[:4], ffn_conv_w=g_ffn[:3])
    small_names = REPL + CONVS
    small_parts = _pack([small_grads[k] for k in small_names], d, 8)
    h_small = _xchg_start([small_parts], [False], small_parts, "gather_small_start")
    (rest_recv,) = _xchg_wait(h_grest, dx, "scatter_rest_wait")
    (in_recv,) = _xchg_wait(h_gin, rest_recv, "scatter_in_wait")
    (small_all,) = _xchg_wait(h_small, in_recv, "gather_small_wait")
    small_sum = _sum_parts(small_all, "sum_small_grads")
    small_g = dict(zip(small_names, _unpack(small_sum, [small_grads[k].shape for k in small_names])))

    rest_sum = _sum_parts(rest_recv, "sum_rest_grads", tile=ROW_ALIGN)
    in_sum = _sum_parts(in_recv, "sum_in_grads", tile=ROW_ALIGN)
    res = {}

    def update(k, g):
        outs = _adamw(g.reshape(1, -1, g.shape[-1]), *(src[k].reshape(-1, g.shape[-1]) for src in (wts, mom1, mom2)),
                      "adamw_" + k)
        for kind, a in zip(("g", "d", "m", "v"), outs):
            res[kind, k] = a.reshape(wts[k].shape)

    update("w_in", in_sum[:n_blk].T)
    for k, o, n in zip(REST, offs, nrows):
        rows = rest_sum[o:o + n]
        update(k, rows.T if k in TRANSPOSED else rows)
    local_g = {}
    for k in REPL:
        local_g[k] = small_g[k].reshape(wts[k].shape)
    for k in CONVS:
        n = wts[k].shape[2]
        local_g[k] = lax.dynamic_slice_in_dim(small_g[k], me * n, n, axis=1)[None]
    w_sm, m_sm, v_sm = (_pack([src[k] for k in small_names], d, 8) for src in (wts, mom1, mom2))
    g_sm = _pack([local_g[k] for k in small_names], d, 8)
    outs_sm = _adamw(g_sm[None], w_sm, m_sm, v_sm, "adamw_small")
    for kind, packed in zip(("g", "d", "m", "v"), outs_sm):
        for k, a in zip(small_names, _unpack(packed, [wts[k].shape for k in small_names])):
            res[kind, k] = a

    loss = lax.psum(loss11[0, 0], ("x", "y", "c"))
    return (loss, dx[None], *[res["g", k] for k in ORDER], *[res["d", k] for k in ORDER],
            *[res["m", k] for k in ORDER], *[res["v", k] for k in ORDER])
```

```python
import functools
import math

import jax
import jax.numpy as jnp
from jax import lax
from jax.experimental import pallas as pl
from jax.experimental.pallas import tpu as pltpu

F32 = jnp.float32
BF16 = jnp.bfloat16
EPS = 1e-5
HEAD_DIM = 64
N_GROUPS = 4
D_STATE = 128
CHUNK = 128
DT_LANES = 128
HALO = 16
N_DEV = 8
V7X_VMEM_LIMIT = 56 * 1024 * 1024
ADAM_LR, ADAM_B1, ADAM_B2, ADAM_EPS, ADAM_WD, ADAM_STEP = 0.001, 0.9, 0.999, 1e-08, 0.01, 10
HIGHEST = lax.Precision.HIGHEST
MESH = pl.DeviceIdType.MESH


def _pc(body, **kw):
    return pl.pallas_call(body, **kw)


def _params():
    return pltpu.CompilerParams(vmem_limit_bytes=V7X_VMEM_LIMIT)


def _pick(n, cands):
    for c in cands:
        if n % c == 0:
            return c
    return n


def _dot(a, b, ca, cb, prec=None):
    return lax.dot_general(a, b, (((ca,), (cb,)), ((), ())), preferred_element_type=F32, precision=prec)


def _sigmoid(x):
    return 0.5 * jnp.tanh(0.5 * x) + 0.5


def _sds(shape, dtype):
    return jax.ShapeDtypeStruct(shape, dtype)


def _mm(pairs, mode, out_dtype, name, tm=1024, tn=1024, after=None):
    m = pairs[0][0].shape[0]
    n = pairs[0][1].shape[1] if mode == "nn" else pairs[0][1].shape[0]
    tm = min(tm, m)
    tn = _pick(n, (tn, 1408, 512, 256, 128))
    npair = len(pairs)
    cb = 0 if mode == "nn" else 1

    def body(*refs):
        o_ref = refs[-1]
        acc = None
        for p in range(npair):
            part = _dot(refs[2 * p][...], refs[2 * p + 1][...], 1, cb)
            acc = part if acc is None else acc + part
        o_ref[...] = acc.astype(o_ref.dtype)

    in_specs, args = [], []
    for a, b in pairs:
        k = a.shape[1]
        in_specs.append(pl.BlockSpec((tm, k), lambda i, j: (i, 0)))
        if mode == "nn":
            in_specs.append(pl.BlockSpec((k, tn), lambda i, j: (0, j)))
        else:
            in_specs.append(pl.BlockSpec((tn, k), lambda i, j: (j, 0)))
        args += [a, b]
    if after is not None:
        in_specs.append(pl.BlockSpec(memory_space=pl.ANY))
        args.append(after)
    return _pc(body, name=name, grid=(m // tm, n // tn), in_specs=in_specs,
               out_specs=pl.BlockSpec((tm, tn), lambda i, j: (i, j)),
               out_shape=_sds((m, n), out_dtype), compiler_params=_params())(*args)


def _mm_tn(a, b, name, tm=1024):
    m, ka = a.shape
    nb = b.shape[1]
    tm = min(tm, m)
    tk = _pick(ka, (1024, 1408, 512, 256, 128))
    tn = _pick(nb, (1024, 512, 256, 128))

    def body(a_ref, b_ref, o_ref):
        @pl.when(pl.program_id(2) == 0)
        def _():
            o_ref[...] = jnp.zeros_like(o_ref)
        o_ref[...] += _dot(a_ref[...], b_ref[...], 0, 0)

    return _pc(body, name=name, grid=(ka // tk, nb // tn, m // tm),
               in_specs=[pl.BlockSpec((tm, tk), lambda i, j, t: (t, i)),
                         pl.BlockSpec((tm, tn), lambda i, j, t: (t, j))],
               out_specs=pl.BlockSpec((tk, tn), lambda i, j, t: (i, j)),
               out_shape=_sds((ka, nb), F32), compiler_params=_params())(a, b)


def _rms_fwd(x, w, name):
    t, d = x.shape
    tm = min(512, t)

    def body(x_ref, w_ref, o_ref):
        xv = x_ref[...]
        r = lax.rsqrt(jnp.mean(xv * xv, axis=-1, keepdims=True) + EPS)
        o_ref[...] = (xv * r * w_ref[...]).astype(o_ref.dtype)

    return _pc(body, name=name, grid=(t // tm,),
               in_specs=[pl.BlockSpec((tm, d), lambda i: (i, 0)), pl.BlockSpec((1, d), lambda i: (0, 0))],
               out_specs=pl.BlockSpec((tm, d), lambda i: (i, 0)),
               out_shape=_sds((t, d), BF16), compiler_params=_params())(x, w)


def _resnorm_fwd(x, mo, w, name):
    t, d = x.shape
    tm = min(512, t)

    def body(x_ref, mo_ref, w_ref, h_ref, v_ref):
        h = x_ref[...] + mo_ref[...]
        r = lax.rsqrt(jnp.mean(h * h, axis=-1, keepdims=True) + EPS)
        h_ref[...] = h
        v_ref[...] = (h * r * w_ref[...]).astype(v_ref.dtype)

    row = pl.BlockSpec((tm, d), lambda i: (i, 0))
    return _pc(body, name=name, grid=(t // tm,),
               in_specs=[row, row, pl.BlockSpec((1, d), lambda i: (0, 0))],
               out_specs=[row, row], out_shape=[_sds((t, d), F32), _sds((t, d), BF16)],
               compiler_params=_params())(x, mo, w)


def _rms_bwd(h, dy, w, dres, name):
    t, d = h.shape
    tm = min(512, t)

    def body(h_ref, dy_ref, w_ref, dres_ref, dx_ref, dxb_ref, dw_ref):
        @pl.when(pl.program_id(0) == 0)
        def _():
            dw_ref[...] = jnp.zeros_like(dw_ref)
        hv = h_ref[...]
        dyv = dy_ref[...]
        r = lax.rsqrt(jnp.mean(hv * hv, axis=-1, keepdims=True) + EPS)
        n = hv * r
        dn = dyv * w_ref[...]
        dw_ref[0:1, :] += jnp.sum(dyv * n, axis=0, keepdims=True)
        dx = dres_ref[...] + r * (dn - n * jnp.mean(dn * n, axis=-1, keepdims=True))
        dx_ref[...] = dx
        dxb_ref[...] = dx.astype(BF16)

    row = pl.BlockSpec((tm, d), lambda i: (i, 0))
    return _pc(body, name=name, grid=(t // tm,),
               in_specs=[row, row, pl.BlockSpec((1, d), lambda i: (0, 0)), row],
               out_specs=[row, row, pl.BlockSpec((8, d), lambda i: (0, 0))],
               out_shape=[_sds((t, d), F32), _sds((t, d), BF16), _sds((8, d), F32)],
               compiler_params=_params())(h, dy, w, dres)


def _final(h1, dd, tgt, w, name):
    t, d = h1.shape
    tm = min(512, t)
    nt = t // tm

    def body(h1_ref, dd_ref, tgt_ref, w_ref, loss_ref, dh_ref, dhb_ref, dw_ref, acc):
        i = pl.program_id(0)

        @pl.when(i == 0)
        def _():
            dw_ref[...] = jnp.zeros_like(dw_ref)
            acc[...] = jnp.zeros_like(acc)
        h = h1_ref[...] + dd_ref[...]
        r = lax.rsqrt(jnp.mean(h * h, axis=-1, keepdims=True) + EPS)
        n = h * r
        wv = w_ref[...]
        e = n * wv - tgt_ref[...]
        acc[...] += jnp.sum(e * e, axis=0, keepdims=True)
        dout = e * (1.0 / d)
        dn = dout * wv
        dw_ref[0:1, :] += jnp.sum(dout * n, axis=0, keepdims=True)
        dh = r * (dn - n * jnp.mean(dn * n, axis=-1, keepdims=True))
        dh_ref[...] = dh
        dhb_ref[...] = dh.astype(BF16)

        @pl.when(i == nt - 1)
        def _():
            loss_ref[...] = jnp.sum(acc[...], axis=-1, keepdims=True) * (0.5 / d)

    row = pl.BlockSpec((tm, d), lambda i: (i, 0))
    return _pc(body, name=name, grid=(nt,),
               in_specs=[row, row, row, pl.BlockSpec((1, d), lambda i: (0, 0))],
               out_specs=[pl.BlockSpec((1, 1), lambda i: (0, 0)), row, row, pl.BlockSpec((8, d), lambda i: (0, 0))],
               out_shape=[_sds((1, 1), F32), _sds((t, d), F32), _sds((t, d), BF16), _sds((8, d), F32)],
               scratch_shapes=[pltpu.VMEM((1, d), F32)], compiler_params=_params())(h1, dd, tgt, w)


def _fill(buf, prev, cur, nxt, tm):
    if prev is not None:
        buf[0:HALO, :] = prev
    buf[HALO:HALO + tm, :] = cur
    if nxt is not None:
        buf[HALO + tm:HALO + tm + HALO, :] = nxt


def _taps(xbuf, k, rows):
    return [xbuf[pl.ds(HALO - (k - 1) + j, rows), :] for j in range(k)]


def _conv_rows(taps, w):
    out = None
    for j, xs in enumerate(taps):
        term = w[j:j + 1, :] * xs
        out = term if out is None else out + term
    return out


def _tile_specs(t, tm, tc, col0):
    th = tm // HALO
    last = t // HALO - 1
    cur = pl.BlockSpec((tm, tc), lambda j, i: (i, col0 + j))
    prev = pl.BlockSpec((HALO, tc), lambda j, i: (jnp.maximum(i * th - 1, 0), col0 + j))
    nxt = pl.BlockSpec((HALO, tc), lambda j, i: (jnp.minimum((i + 1) * th, last), col0 + j))
    return cur, prev, nxt


def _conv_a_fwd(pa, w, d, name):
    t = pa.shape[0]
    tm, tc = min(512, t), _pick(d, (512, 256, 128))
    nd = d // tc

    def body(b_ref, c_ref, v_ref, cp_ref, vp_ref, w_ref, o_ref, q_ref, buf):
        i = pl.program_id(1)
        keep = (i > 0).astype(F32)
        prev = cp_ref[...].astype(F32) * vp_ref[...].astype(F32) * keep
        _fill(buf, prev, c_ref[...].astype(F32) * v_ref[...].astype(F32), None, tm)
        q = _conv_rows(_taps(buf, 3, tm), w_ref[...])
        q_ref[...] = q.astype(q_ref.dtype)
        o_ref[...] = (b_ref[...].astype(F32) * q).astype(o_ref.dtype)

    b_cur, _, _ = _tile_specs(t, tm, tc, 0)
    c_cur, c_prev, _ = _tile_specs(t, tm, tc, nd)
    v_cur, v_prev, _ = _tile_specs(t, tm, tc, 2 * nd)
    return _pc(body, name=name, grid=(nd, t // tm),
               in_specs=[b_cur, c_cur, v_cur, c_prev, v_prev, pl.BlockSpec((3, tc), lambda j, i: (0, j))],
               out_specs=[pl.BlockSpec((tm, tc), lambda j, i: (i, j))] * 2,
               out_shape=[_sds((t, d), BF16)] * 2,
               scratch_shapes=[pltpu.VMEM((tm + 2 * HALO, tc), F32)],
               compiler_params=_params())(pa, pa, pa, pa, pa, w)


def _shifted(dbuf, k, tm):
    return [dbuf[pl.ds(k - 1 - j, tm), :] for j in range(k)]


def _conv_backward(dbuf, dcur, dnxt, x_cur, w, acc_ref, k, tm, with_bias):
    dbuf[0:tm, :] = dcur
    dbuf[tm:tm + HALO, :] = dnxt
    dx = None
    for j, ds in enumerate(_shifted(dbuf, k, tm)):
        term = w[j:j + 1, :] * ds
        dx = term if dx is None else dx + term
        acc_ref[j:j + 1, :] += jnp.sum(ds * x_cur, axis=0, keepdims=True)
    if with_bias:
        acc_ref[k:k + 1, :] += jnp.sum(dcur, axis=0, keepdims=True)
    return dx


def _conv_a_bwd(pa, q, dya, w, d, name):
    t = pa.shape[0]
    tm, tc = min(512, t), _pick(d, (512, 256, 128))
    nd, nt = d // tc, t // tm

    def body(b_ref, c_ref, v_ref, bn_ref, q_ref, g_ref, gn_ref, w_ref, db_ref, dc_ref, dv_ref, acc_ref, dbuf):
        i = pl.program_id(1)

        @pl.when(i == 0)
        def _():
            acc_ref[...] = jnp.zeros_like(acc_ref)
        cv, vv = c_ref[...].astype(F32), v_ref[...].astype(F32)
        g = g_ref[...].astype(F32)
        dq = g * b_ref[...].astype(F32)
        dqn = gn_ref[...].astype(F32) * bn_ref[...].astype(F32) * (i < nt - 1).astype(F32)
        db_ref[...] = (g * q_ref[...].astype(F32)).astype(BF16)
        dp = _conv_backward(dbuf, dq, dqn, cv * vv, w_ref[...], acc_ref, 3, tm, False)
        dc_ref[...] = (dp * vv).astype(BF16)
        dv_ref[...] = (dp * cv).astype(BF16)

    b_cur, _, b_next = _tile_specs(t, tm, tc, 0)
    c_cur, _, _ = _tile_specs(t, tm, tc, nd)
    v_cur, _, _ = _tile_specs(t, tm, tc, 2 * nd)
    g_cur, _, g_next = _tile_specs(t, tm, tc, 0)
    out = pl.BlockSpec((tm, tc), lambda j, i: (i, j))
    return _pc(body, name=name, grid=(nd, nt),
               in_specs=[b_cur, c_cur, v_cur, b_next, g_cur, g_cur, g_next,
                         pl.BlockSpec((3, tc), lambda j, i: (0, j))],
               out_specs=[out, out, out, pl.BlockSpec((8, tc), lambda j, i: (0, j))],
               out_shape=[_sds((t, d), BF16)] * 3 + [_sds((8, d), F32)],
               scratch_shapes=[pltpu.VMEM((tm + HALO, tc), F32)],
               compiler_params=_params())(pa, pa, pa, pa, q, dya, dya, w)


def _conv_s_fwd(xbc, w, b, name):
    t, dx = xbc.shape
    tm, tc = min(512, t), _pick(dx, (512, 256, 128))

    def body(x_ref, xp_ref, w_ref, b_ref, o_ref, pre_ref, buf):
        i = pl.program_id(1)
        _fill(buf, xp_ref[...].astype(F32) * (i > 0).astype(F32), x_ref[...].astype(F32), None, tm)
        pre = _conv_rows(_taps(buf, 4, tm), w_ref[...]) + b_ref[...]
        pre_ref[...] = pre.astype(pre_ref.dtype)
        o_ref[...] = (pre * _sigmoid(pre)).astype(o_ref.dtype)

    cur, prev, _ = _tile_specs(t, tm, tc, 0)
    return _pc(body, name=name, grid=(dx // tc, t // tm),
               in_specs=[cur, prev, pl.BlockSpec((4, tc), lambda j, i: (0, j)),
                         pl.BlockSpec((1, tc), lambda j, i: (0, j))],
               out_specs=[pl.BlockSpec((tm, tc), lambda j, i: (i, j))] * 2,
               out_shape=[_sds((t, dx), BF16)] * 2,
               scratch_shapes=[pltpu.VMEM((tm + 2 * HALO, tc), F32)],
               compiler_params=_params())(xbc, xbc, w, b)


def _dsilu(pre):
    s = _sigmoid(pre)
    return s * (1.0 + pre * (1.0 - s))


def _conv_s_bwd(xbc, pre, dxc, w, name):
    t, dx = xbc.shape
    tm, tc = min(512, t), _pick(dx, (512, 256, 128))
    nt = t // tm

    def body(x_ref, p_ref, pn_ref, g_ref, gn_ref, w_ref, dx_ref, acc_ref, dbuf):
        i = pl.program_id(1)

        @pl.when(i == 0)
        def _():
            acc_ref[...] = jnp.zeros_like(acc_ref)
        dcur = g_ref[...].astype(F32) * _dsilu(p_ref[...].astype(F32))
        dnxt = gn_ref[...].astype(F32) * _dsilu(pn_ref[...].astype(F32)) * (i < nt - 1).astype(F32)
        dx_ref[...] = _conv_backward(dbuf, dcur, dnxt, x_ref[...].astype(F32), w_ref[...], acc_ref, 4, tm,
                                     True).astype(BF16)

    cur, _, nxt = _tile_specs(t, tm, tc, 0)
    return _pc(body, name=name, grid=(dx // tc, nt),
               in_specs=[cur, cur, nxt, cur, nxt, pl.BlockSpec((4, tc), lambda j, i: (0, j))],
               out_specs=[pl.BlockSpec((tm, tc), lambda j, i: (i, j)), pl.BlockSpec((8, tc), lambda j, i: (0, j))],
               out_shape=[_sds((t, dx), BF16), _sds((8, dx), F32)],
               scratch_shapes=[pltpu.VMEM((tm + HALO, tc), F32)],
               compiler_params=_params())(xbc, pre, pre, dxc, dxc, w)


def _ffn_fwd(hv, w, b, f, name):
    t = hv.shape[0]
    tm, tc = min(512, t), _pick(f, (512, 256, 128))
    nf = f // tc

    def body(h1_ref, h1p_ref, h3_ref, w_ref, b_ref, o_ref, c1_ref, buf):
        i = pl.program_id(1)
        _fill(buf, h1p_ref[...].astype(F32) * (i > 0).astype(F32), h1_ref[...].astype(F32), None, tm)
        c1 = _conv_rows(_taps(buf, 3, tm), w_ref[...]) + b_ref[...]
        c1_ref[...] = c1.astype(c1_ref.dtype)
        o_ref[...] = (c1 * _sigmoid(c1) * h3_ref[...].astype(F32)).astype(o_ref.dtype)

    h1_cur, h1_prev, _ = _tile_specs(t, tm, tc, 0)
    h3_cur, _, _ = _tile_specs(t, tm, tc, nf)
    return _pc(body, name=name, grid=(nf, t // tm),
               in_specs=[h1_cur, h1_prev, h3_cur, pl.BlockSpec((3, tc), lambda j, i: (0, j)),
                         pl.BlockSpec((1, tc), lambda j, i: (0, j))],
               out_specs=[pl.BlockSpec((tm, tc), lambda j, i: (i, j))] * 2,
               out_shape=[_sds((t, f), BF16)] * 2,
               scratch_shapes=[pltpu.VMEM((tm + 2 * HALO, tc), F32)],
               compiler_params=_params())(hv, hv, hv, w, b)


def _ffn_bwd(hv, c1, dact, w, f, name):
    t = hv.shape[0]
    tm, tc = min(512, t), _pick(f, (512, 256, 128))
    nf, nt = f // tc, t // tm

    def body(h1_ref, h3_ref, h3n_ref, c_ref, cn_ref, g_ref, gn_ref, w_ref, dh1_ref, dh3_ref, acc_ref, dbuf):
        i = pl.program_id(1)

        @pl.when(i == 0)
        def _():
            acc_ref[...] = jnp.zeros_like(acc_ref)
        c1v, g = c_ref[...].astype(F32), g_ref[...].astype(F32)
        s1 = _sigmoid(c1v)
        dh3_ref[...] = (g * c1v * s1).astype(BF16)
        dcur = g * h3_ref[...].astype(F32) * s1 * (1.0 + c1v * (1.0 - s1))
        dnxt = (gn_ref[...].astype(F32) * h3n_ref[...].astype(F32) * _dsilu(cn_ref[...].astype(F32))
                * (i < nt - 1).astype(F32))
        dh1_ref[...] = _conv_backward(dbuf, dcur, dnxt, h1_ref[...].astype(F32), w_ref[...], acc_ref, 3, tm,
                                      True).astype(BF16)

    h1_cur, _, _ = _tile_specs(t, tm, tc, 0)
    h3_cur, _, h3_next = _tile_specs(t, tm, tc, nf)
    g_cur, _, g_next = _tile_specs(t, tm, tc, 0)
    out = pl.BlockSpec((tm, tc), lambda j, i: (i, j))
    return _pc(body, name=name, grid=(nf, nt),
               in_specs=[h1_cur, h3_cur, h3_next, g_cur, g_next, g_cur, g_next,
                         pl.BlockSpec((3, tc), lambda j, i: (0, j))],
               out_specs=[out, out, pl.BlockSpec((8, tc), lambda j, i: (0, j))],
               out_shape=[_sds((t, f), BF16), _sds((t, f), BF16), _sds((8, f), F32)],
               scratch_shapes=[pltpu.VMEM((tm + HALO, tc), F32)],
               compiler_params=_params())(hv, hv, hv, c1, c1, dact, dact, w)


def _gnorm_fwd(y, z, w, name):
    t, di = y.shape
    gw = di // N_GROUPS
    tm = min(512, t)

    def body(y_ref, z_ref, w_ref, o_ref):
        zv = z_ref[...].astype(F32)
        yz = y_ref[...].astype(F32) * zv * _sigmoid(zv)
        r = lax.rsqrt(jnp.mean(yz * yz, axis=-1, keepdims=True) + EPS)
        o_ref[...] = (yz * r * w_ref[...]).astype(o_ref.dtype)

    blk = pl.BlockSpec((tm, gw), lambda j, i: (i, j))
    return _pc(body, name=name, grid=(N_GROUPS, t // tm),
               in_specs=[blk, blk, pl.BlockSpec((1, gw), lambda j, i: (0, j))],
               out_specs=blk, out_shape=_sds((t, di), BF16), compiler_params=_params())(y, z, w)


def _gnorm_bwd(y, z, dyn, w, name):
    t, di = y.shape
    gw = di // N_GROUPS
    tm = min(512, t)

    def body(y_ref, z_ref, g_ref, w_ref, dy_ref, dz_ref, dw_ref):
        @pl.when(pl.program_id(1) == 0)
        def _():
            dw_ref[...] = jnp.zeros_like(dw_ref)
        yv, zv, g = y_ref[...].astype(F32), z_ref[...].astype(F32), g_ref[...].astype(F32)
        s = _sigmoid(zv)
        sz = zv * s
        yz = yv * sz
        r = lax.rsqrt(jnp.mean(yz * yz, axis=-1, keepdims=True) + EPS)
        n = yz * r
        dn = g * w_ref[...]
        dw_ref[0:1, :] += jnp.sum(g * n, axis=0, keepdims=True)
        dyz = r * (dn - n * jnp.mean(dn * n, axis=-1, keepdims=True))
        dy_ref[...] = (dyz * sz).astype(BF16)
        dz_ref[...] = (dyz * yv * s * (1.0 + zv * (1.0 - s))).astype(BF16)

    blk = pl.BlockSpec((tm, gw), lambda j, i: (i, j))
    return _pc(body, name=name, grid=(N_GROUPS, t // tm),
               in_specs=[blk, blk, blk, pl.BlockSpec((1, gw), lambda j, i: (0, j))],
               out_specs=[blk, blk, pl.BlockSpec((8, gw), lambda j, i: (0, j))],
               out_shape=[_sds((t, di), BF16), _sds((t, di), BF16), _sds((8, di), F32)],
               compiler_params=_params())(y, z, dyn, w)


def _merge_fwd(gates, ya, ys, d, name):
    t = ya.shape[0]
    tm, tc = min(512, t), _pick(d, (512, 256, 128))
    nd = d // tc

    def body(ga_ref, gs_ref, ya_ref, ys_ref, o_ref):
        o_ref[...] = (_sigmoid(ga_ref[...].astype(F32)) * ya_ref[...].astype(F32)
                      + _sigmoid(gs_ref[...].astype(F32)) * ys_ref[...].astype(F32)).astype(o_ref.dtype)

    blk = pl.BlockSpec((tm, tc), lambda j, i: (i, j))
    return _pc(body, name=name, grid=(nd, t // tm),
               in_specs=[blk, pl.BlockSpec((tm, tc), lambda j, i: (i, nd + j)), blk, blk],
               out_specs=blk, out_shape=_sds((t, d), BF16), compiler_params=_params())(gates, gates, ya, ys)


def _merge_bwd(dm, gates, ya, ys, d, name):
    t = ya.shape[0]
    tm, tc = min(512, t), _pick(d, (512, 256, 128))
    nd = d // tc

    def body(dm_ref, ga_ref, gs_ref, ya_ref, ys_ref, dya_ref, dys_ref, dga_ref, dgs_ref):
        g = dm_ref[...].astype(F32)
        sa, ss = _sigmoid(ga_ref[...].astype(F32)), _sigmoid(gs_ref[...].astype(F32))
        dya_ref[...] = (g * sa).astype(BF16)
        dys_ref[...] = (g * ss).astype(BF16)
        dga_ref[...] = (g * ya_ref[...].astype(F32) * sa * (1.0 - sa)).astype(BF16)
        dgs_ref[...] = (g * ys_ref[...].astype(F32) * ss * (1.0 - ss)).astype(BF16)

    blk = pl.BlockSpec((tm, tc), lambda j, i: (i, j))
    return _pc(body, name=name, grid=(nd, t // tm),
               in_specs=[blk, blk, pl.BlockSpec((tm, tc), lambda j, i: (i, nd + j)), blk, blk],
               out_specs=[blk] * 4, out_shape=[_sds((t, d), BF16)] * 4,
               compiler_params=_params())(dm, gates, gates, ya, ys)


def _ssd_chunk_terms(dtr, dtb, alog):
    xx = dtr + dtb
    dt = jnp.maximum(xx, 0.0) + jnp.log(1.0 + jnp.exp(-jnp.abs(xx)))
    a = -jnp.exp(alog)
    li = lax.broadcasted_iota(jnp.int32, (CHUNK, CHUNK), 0)
    si = lax.broadcasted_iota(jnp.int32, (CHUNK, CHUNK), 1)
    causal = li >= si
    acum = _dot(causal.astype(F32), dt * a, 1, 0, HIGHEST)
    return xx, dt, a, acum, acum.T, causal


def _split3(x):
    hi = x.astype(BF16)
    r = x - hi.astype(F32)
    mid = r.astype(BF16)
    lo = (r - mid.astype(F32)).astype(BF16)
    return hi, mid, lo


def _expand(v, e):
    hi, mid, lo = _split3(v)
    return _dot(hi, e, 1, 0) + _dot(mid, e, 1, 0) + _dot(lo, e, 1, 0)


def _segsum(s, e):
    hi, mid, lo = _split3(s)
    return _dot(hi, e, 1, 1) + _dot(mid, e, 1, 1) + _dot(lo, e, 1, 1)


def _head_maps(di):
    nh = di // HEAD_DIM
    h = jnp.arange(DT_LANES)[:, None]
    e64 = (jnp.arange(di)[None, :] // HEAD_DIM == h).astype(BF16)
    e128 = (jnp.arange(nh * CHUNK)[None, :] // CHUNK == h).astype(BF16)
    return e64, e128


def _pair_blockdiag(p, left):
    zero = jnp.zeros_like(p)
    return jnp.concatenate([jnp.where(left, p, zero), jnp.where(left, zero, p)], axis=0)


def _ssd_fwd(xc, dtr, dtb, alog, dskx, di, name):
    t = xc.shape[0]
    dx = xc.shape[1]
    nc = t // CHUNK
    nh = di // HEAD_DIM
    hpg = nh // N_GROUPS
    gw = hpg * HEAD_DIM
    boff, coff = di, di + N_GROUPS * D_STATE
    e64, e128 = _head_maps(di)

    def body(xc_ref, dtr_ref, dtb_ref, alog_ref, dsk_ref, e64_ref, e128_ref, y_ref, st_ref, state):
        @pl.when(pl.program_id(0) == 0)
        def _():
            state[...] = jnp.zeros_like(state)
        _, dt, _, acum, acum_t, causal = _ssd_chunk_terms(dtr_ref[...], dtb_ref[...], alog_ref[...])
        last = acum[CHUNK - 1:CHUNK, :]
        e64v = e64_ref[...]
        dtx = _expand(dt, e64v)
        eax = _expand(jnp.exp(acum), e64v)
        dex = _expand(dt * jnp.exp(last - acum), e64v)
        acx = _expand(acum, e128_ref[...])
        st_ref[0] = state[...]
        left = lax.broadcasted_iota(jnp.int32, (CHUNK, 2 * HEAD_DIM), 1) < HEAD_DIM
        for g in range(N_GROUPS):
            gs = slice(g * gw, (g + 1) * gw)
            bg = xc_ref[:, boff + g * D_STATE:boff + (g + 1) * D_STATE]
            cg = xc_ref[:, coff + g * D_STATE:coff + (g + 1) * D_STATE]
            gm = _dot(cg, bg, 1, 1)
            xg = xc_ref[:, gs].astype(F32)
            xdb = (xg * dtx[:, gs]).astype(BF16)
            sin = state[:, gs]
            yo = _dot(cg, sin.astype(BF16), 1, 0) * eax[:, gs]
            for jp in range(hpg // 2):
                h0 = g * hpg + 2 * jp
                ps = slice(jp * 2 * HEAD_DIM, (jp + 1) * 2 * HEAD_DIM)
                ms = []
                for hh in (h0, h0 + 1):
                    seg = acx[:, hh * CHUNK:(hh + 1) * CHUNK] - acum_t[hh:hh + 1, :]
                    ms.append((gm * jnp.exp(jnp.where(causal, seg, -1e30))).astype(BF16))
                yd = _dot(jnp.concatenate(ms, axis=1), _pair_blockdiag(xdb[:, ps], left), 1, 0)
                col = slice(g * gw + jp * 2 * HEAD_DIM, g * gw + (jp + 1) * 2 * HEAD_DIM)
                y_ref[:, col] = (yd + yo[:, ps] + dsk_ref[:, col] * xg[:, ps]).astype(y_ref.dtype)
            xe = (xg * dex[:, gs]).astype(BF16)
            state[:, gs] = eax[CHUNK - 1:CHUNK, gs] * sin + _dot(bg, xe, 0, 0)

    small = pl.BlockSpec((1, DT_LANES), lambda c: (0, 0))
    whole = lambda a: pl.BlockSpec(a.shape, lambda c: (0, 0))
    return _pc(body, name=name, grid=(nc,),
               in_specs=[pl.BlockSpec((CHUNK, dx), lambda c: (c, 0)),
                         pl.BlockSpec((CHUNK, DT_LANES), lambda c: (c, 0)), small, small,
                         whole(dskx), whole(e64), whole(e128)],
               out_specs=[pl.BlockSpec((CHUNK, di), lambda c: (c, 0)),
                          pl.BlockSpec((1, D_STATE, di), lambda c: (c, 0, 0))],
               out_shape=[_sds((t, di), BF16), _sds((nc, D_STATE, di), F32)],
               scratch_shapes=[pltpu.VMEM((D_STATE, di), F32)],
               compiler_params=_params())(xc, dtr, dtb, alog, dskx, e64, e128)


def _ssd_bwd(xc, dtr, dy, states, dtb, alog, dskx, di, name):
    t = xc.shape[0]
    dx = xc.shape[1]
    nc = t // CHUNK
    nh = di // HEAD_DIM
    hpg = nh // N_GROUPS
    gw = hpg * HEAD_DIM
    boff, coff = di, di + N_GROUPS * D_STATE
    e64, e128 = _head_maps(di)

    def body(xc_ref, dtr_ref, dy_ref, st_ref, dtb_ref, alog_ref, dsk_ref, e64_ref, e128_ref,
             dxc_ref, ddtr_ref, sm_ref, dstate, darow):
        @pl.when(pl.program_id(0) == 0)
        def _():
            dstate[...] = jnp.zeros_like(dstate)
            sm_ref[...] = jnp.zeros_like(sm_ref)
        darow[...] = jnp.zeros_like(darow)
        xx, dt, a, acum, acum_t, causal = _ssd_chunk_terms(dtr_ref[...], dtb_ref[...], alog_ref[...])
        last = acum[CHUNK - 1:CHUNK, :]
        e64v = e64_ref[...]
        dtx = _expand(dt, e64v)
        eax = _expand(jnp.exp(acum), e64v)
        eex = _expand(jnp.exp(last - acum), e64v)
        acx = _expand(acum, e128_ref[...])
        left = lax.broadcasted_iota(jnp.int32, (CHUNK, 2 * HEAD_DIM), 1) < HEAD_DIM
        lane = lax.broadcasted_iota(jnp.int32, (CHUNK, DT_LANES), 1)
        sub8 = lax.broadcasted_iota(jnp.int32, (8, gw), 0)
        da_col = jnp.zeros((CHUNK, DT_LANES), F32)
        ddt_col = jnp.zeros((CHUNK, DT_LANES), F32)
        rows = jnp.zeros((8, DT_LANES), F32)
        for g in range(N_GROUPS):
            gs = slice(g * gw, (g + 1) * gw)
            bg = xc_ref[:, boff + g * D_STATE:boff + (g + 1) * D_STATE]
            cg = xc_ref[:, coff + g * D_STATE:coff + (g + 1) * D_STATE]
            gm = _dot(cg, bg, 1, 1)
            e64g = e64v[:, gs]
            xg = xc_ref[:, gs].astype(F32)
            dtg, eag, eeg = dtx[:, gs], eax[:, gs], eex[:, gs]
            xd = xg * dtg
            xdb = xd.astype(BF16)
            dyb = dy_ref[:, gs]
            dyf = dyb.astype(F32)
            sin = st_ref[0, :, gs]
            sinb = sin.astype(BF16)
            ds = dstate[:, gs]
            dsb = ds.astype(BF16)
            bds = _dot(bg, dsb, 1, 0)
            dyeb = (dyf * eag).astype(BF16)
            dcg = _dot(dyeb, sinb, 1, 1)
            dstate[:, gs] = eag[CHUNK - 1:CHUNK, :] * ds + _dot(cg, dyeb, 0, 0)
            yo = _dot(cg, sinb, 1, 0) * eag
            xe = xd * eeg
            dbg = _dot(xe.astype(BF16), dsb, 1, 1)
            wterm = bds * xe
            da_col = da_col + _segsum(dyf * yo - wterm, e64g)
            dg = jnp.zeros((CHUNK, CHUNK), F32)
            dxd_parts = []
            for jp in range(hpg // 2):
                h0 = g * hpg + 2 * jp
                ps = slice(jp * 2 * HEAD_DIM, (jp + 1) * 2 * HEAD_DIM)
                lms, mfs = [], []
                for hh in (h0, h0 + 1):
                    seg = acx[:, hh * CHUNK:(hh + 1) * CHUNK] - acum_t[hh:hh + 1, :]
                    lm = jnp.exp(jnp.where(causal, seg, -1e30))
                    lms.append(lm)
                    mfs.append(gm * lm)
                mstack = jnp.concatenate([m.astype(BF16) for m in mfs], axis=0)
                dyp = dyb[:, ps]
                dxd_parts.append(_dot(mstack, _pair_blockdiag(dyp, left), 0, 0))
                dm2 = _dot(dyp, _pair_blockdiag(xdb[:, ps], left), 1, 1)
                for k, hh in enumerate((h0, h0 + 1)):
                    dm = dm2[:, k * CHUNK:(k + 1) * CHUNK]
                    dg = dg + dm * lms[k]
                    q = dm * mfs[k]
                    da_col = da_col + jnp.where(lane == hh, jnp.sum(q, axis=1, keepdims=True), 0.0)
                    darow[hh:hh + 1, :] = -jnp.sum(q, axis=0, keepdims=True)
            dxd = jnp.concatenate(dxd_parts, axis=1) + bds * eeg
            ddt_col = ddt_col + _segsum(dxd * xg, e64g)
            rsum = (jnp.where(sub8 == 0, jnp.sum(wterm, axis=0, keepdims=True), 0.0)
                    + jnp.where(sub8 == 1, jnp.sum(ds * sin, axis=0, keepdims=True), 0.0)
                    + jnp.where(sub8 == 2, jnp.sum(dyf * xg, axis=0, keepdims=True), 0.0))
            rows = rows + _segsum(rsum, e64g)
            dxc_ref[:, gs] = (dxd * dtg + dsk_ref[:, gs] * dyf).astype(dxc_ref.dtype)
            dgb = dg.astype(BF16)
            dxc_ref[:, boff + g * D_STATE:boff + (g + 1) * D_STATE] = (
                dbg + _dot(dgb, cg, 0, 0)).astype(dxc_ref.dtype)
            dxc_ref[:, coff + g * D_STATE:coff + (g + 1) * D_STATE] = (
                dcg + _dot(dgb, bg, 1, 0)).astype(dxc_ref.dtype)
        at_last = rows[0:1, :] + jnp.exp(last) * rows[1:2, :]
        is_last = lax.broadcasted_iota(jnp.int32, (CHUNK, DT_LANES), 0) == CHUNK - 1
        da = da_col + jnp.where(is_last, at_last, 0.0) + darow[...].T
        li = lax.broadcasted_iota(jnp.int32, (CHUNK, CHUNK), 0)
        si = lax.broadcasted_iota(jnp.int32, (CHUNK, CHUNK), 1)
        dla = _dot((si >= li).astype(F32), da, 1, 0, HIGHEST)
        ddtr = (ddt_col + dla * a) * _sigmoid(xx)
        ddtr_ref[...] = ddtr
        sm_ref[0:1, :] += jnp.sum(ddtr, axis=0, keepdims=True)
        sm_ref[1:2, :] += jnp.sum(dla * dt, axis=0, keepdims=True) * a
        sm_ref[2:3, :] += rows[2:3, :]

    small = pl.BlockSpec((1, DT_LANES), lambda c: (0, 0))
    whole = lambda a: pl.BlockSpec(a.shape, lambda c: (0, 0))
    rev = lambda c: (nc - 1 - c, 0)
    return _pc(body, name=name, grid=(nc,),
               in_specs=[pl.BlockSpec((CHUNK, dx), rev), pl.BlockSpec((CHUNK, DT_LANES), rev),
                         pl.BlockSpec((CHUNK, di), rev),
                         pl.BlockSpec((1, D_STATE, di), lambda c: (nc - 1 - c, 0, 0)), small, small,
                         whole(dskx), whole(e64), whole(e128)],
               out_specs=[pl.BlockSpec((CHUNK, dx), rev), pl.BlockSpec((CHUNK, DT_LANES), rev),
                          pl.BlockSpec((8, DT_LANES), lambda c: (0, 0))],
               out_shape=[_sds((t, dx), BF16), _sds((t, DT_LANES), F32), _sds((8, DT_LANES), F32)],
               scratch_shapes=[pltpu.VMEM((D_STATE, di), F32), pltpu.VMEM((DT_LANES, CHUNK), F32)],
               compiler_params=_params())(xc, dtr, dy, states, dtb, alog, dskx, e64, e128)


def _adamw(parts, w, m, v, name):
    npart, rows, width = parts.shape
    tr = _pick(rows, (64, 32, 16, 8))
    c1 = 1.0 - ADAM_B1 ** ADAM_STEP
    c2 = 1.0 - ADAM_B2 ** ADAM_STEP

    def body(p_ref, w_ref, m_ref, v_ref, g_ref, d_ref, nm_ref, nv_ref):
        g = p_ref[0].astype(F32)
        for p in range(1, npart):
            g = g + p_ref[p].astype(F32)
        nm = ADAM_B1 * m_ref[...] + (1.0 - ADAM_B1) * g
        nv = ADAM_B2 * v_ref[...] + (1.0 - ADAM_B2) * (g * g)
        g_ref[...] = g
        nm_ref[...] = nm
        nv_ref[...] = nv
        d_ref[...] = -ADAM_LR * ((nm / c1) / (jnp.sqrt(nv / c2) + ADAM_EPS) + ADAM_WD * w_ref[...])

    blk = pl.BlockSpec((tr, width), lambda i: (i, 0))
    return _pc(body, name=name, grid=(rows // tr,),
               in_specs=[pl.BlockSpec((npart, tr, width), lambda i: (0, i, 0)), blk, blk, blk],
               out_specs=[blk] * 4, out_shape=[_sds((rows, width), F32)] * 4,
               compiler_params=_params())(parts, w, m, v)


def _sum_parts(parts, name, tile=None):
    npart, rows, width = parts.shape
    tile = rows if tile is None else tile

    def body(p_ref, o_ref):
        g = p_ref[0].astype(F32)
        for p in range(1, npart):
            g = g + p_ref[p].astype(F32)
        o_ref[...] = g

    return _pc(body, name=name, grid=(rows // tile,),
               in_specs=[pl.BlockSpec((npart, tile, width), lambda i: (0, i, 0))],
               out_specs=pl.BlockSpec((tile, width), lambda i: (i, 0)),
               out_shape=_sds((rows, width), F32), compiler_params=_params())(parts)


def _peers():
    x, y, c = lax.axis_index("x"), lax.axis_index("y"), lax.axis_index("c")
    out = []
    for k in range(1, N_DEV):
        px = 1 - x if k & 4 else x
        py = 1 - y if k & 2 else y
        pc = 1 - c if k & 1 else c
        out.append(((px, py, pc), 4 * px + 2 * py + pc))
    return 4 * x + 2 * y + c, out


def _copies(arrays, lands, send_sems, recv_sems, scatter):
    me, peers = _peers()
    outgoing, incoming = [], []
    for k, (peer, pidx) in enumerate(peers):
        for j, (a_ref, land_ref) in enumerate(zip(arrays, lands)):
            src = a_ref.at[pidx] if scatter[j] else a_ref
            sem = len(arrays) * k + j
            for dst, bucket in ((land_ref.at[me], outgoing), (land_ref.at[pidx], incoming)):
                bucket.append(pltpu.make_async_remote_copy(
                    src_ref=src, dst_ref=dst, send_sem=send_sems.at[sem], recv_sem=recv_sems.at[sem],
                    device_id=peer, device_id_type=MESH))
    return outgoing, incoming


HBM_SPEC = pl.BlockSpec(memory_space=pltpu.HBM)
SEM_SPEC = pl.BlockSpec(memory_space=pltpu.SEMAPHORE)
ANY_SPEC = pl.BlockSpec(memory_space=pl.ANY)
EFFECT = pltpu.SideEffectType.DATAFLOW_SIDE_EFFECTING


def _xchg_start(arrays, scatter, after, name):
    n = len(arrays)
    me = 4 * lax.axis_index("x") + 2 * lax.axis_index("y") + lax.axis_index("c")
    lands = []
    for a, sc in zip(arrays, scatter):
        own = lax.dynamic_index_in_dim(a, me, 0, keepdims=True) if sc else a[None]
        shape = a.shape if sc else (N_DEV,) + a.shape
        lands.append(lax.dynamic_update_slice(lax.empty(shape, a.dtype), own, (me,) + (0,) * (len(shape) - 1)))

    def body(*refs):
        ins, outs = refs[:2 * n], refs[2 * n + 1:]
        outgoing, _ = _copies(ins[:n], ins[n:], outs[0], outs[1], scatter)
        for cp in outgoing:
            cp.start()
        outs[-1][...] = jnp.zeros_like(outs[-1])

    nsem = n * (N_DEV - 1)
    operands = [pltpu.with_memory_space_constraint(a, pltpu.HBM) for a in list(arrays) + lands]
    out = _pc(body, name=name,
              out_shape=(pltpu.SemaphoreType.DMA((nsem,)), pltpu.SemaphoreType.DMA((nsem,)),
                         *[pltpu.HBM(a.shape, a.dtype) for a in operands], _sds((8, 128), F32)),
              in_specs=[HBM_SPEC] * (2 * n) + [ANY_SPEC],
              out_specs=(SEM_SPEC, SEM_SPEC, *[HBM_SPEC] * (2 * n), pl.BlockSpec(memory_space=pltpu.VMEM)),
              input_output_aliases={i: 2 + i for i in range(2 * n)},
              compiler_params=pltpu.CompilerParams(has_side_effects=EFFECT))(*operands, after)
    return dict(sems=out[:2], thru=out[2:2 + 2 * n], token=out[-1], scatter=scatter, n=n)


def _xchg_wait(handle, after, name):
    n, scatter = handle["n"], handle["scatter"]

    def body(*refs):
        ins = refs[:2 * n]
        send_sems, recv_sems = refs[2 * n], refs[2 * n + 1]
        outgoing, incoming = _copies(ins[:n], ins[n:], send_sems, recv_sems, scatter)
        for cp in outgoing:
            cp.wait_send()
        for cp in incoming:
            cp.wait_recv()

    thru = handle["thru"]
    out = _pc(body, name=name, out_shape=tuple(pltpu.HBM(a.shape, a.dtype) for a in thru),
              in_specs=[HBM_SPEC] * (2 * n) + [SEM_SPEC, SEM_SPEC, ANY_SPEC], out_specs=tuple([HBM_SPEC] * (2 * n)),
              input_output_aliases={i: i for i in range(2 * n)},
              compiler_params=pltpu.CompilerParams(has_side_effects=EFFECT))(*thru, *handle["sems"], after)
    return out[n:]


def _pack(arrs, width, row_mult):
    flat = jnp.concatenate([a.reshape(-1) for a in arrs])
    n = flat.shape[0]
    rows = -(-n // (width * row_mult)) * row_mult
    return jnp.pad(flat, (0, rows * width - n)).reshape(rows, width)


def _unpack(packed, shapes, lead=None):
    out, off = [], 0
    flat = packed.reshape(-1) if lead is None else packed.reshape(lead, -1)
    for s in shapes:
        n = math.prod(s)
        if lead is None:
            out.append(flat[off:off + n].reshape(s))
        else:
            out.append(flat[:, off:off + n].reshape((lead,) + tuple(s)))
        off += n
    return out


def _blocks_to_cols(blocks):
    nb, rows, n = blocks.shape
    return blocks.transpose(1, 0, 2).reshape(rows, nb * n)


def _pad_rows(a, rows):
    return jnp.pad(a, ((0, rows - a.shape[0]), (0, 0)))


def _pad_lanes(a, lanes):
    return jnp.pad(a, ((0, 0), (0, lanes - a.shape[1])))


REST = ("w_a_out", "w_s_out", "w_o", "w_up", "w_down")
TRANSPOSED = ("w_up", "w_in")
ROW_ALIGN = 32
CONVS = ("conv_a_w", "ssd_conv_w", "ffn_conv_w")
REPL = ("norm_mix_w", "ssd_conv_b", "dt_bias", "a_log", "d_skip", "ssd_norm_w", "norm_ffn_w", "ffn_conv_b",
        "final_norm_w")
ORDER = ("norm_mix_w", "w_in", "conv_a_w", "w_a_out", "ssd_conv_w", "ssd_conv_b", "dt_bias", "a_log", "d_skip",
         "ssd_norm_w", "w_s_out", "w_o", "norm_ffn_w", "w_up", "ffn_conv_w", "ffn_conv_b", "w_down", "final_norm_w")


def _as_rows(name, block):
    return block[0].T if name in TRANSPOSED else block[0]


def kernel(x, norm_mix_w, w_in, conv_a_w, w_a_out, ssd_conv_w, ssd_conv_b, dt_bias, a_log, d_skip, ssd_norm_w, w_s_out, w_o, norm_ffn_w, w_up, ffn_conv_w, ffn_conv_b, w_down, final_norm_w, loss_target, m_norm_mix_w, m_w_in, m_conv_a_w, m_w_a_out, m_ssd_conv_w, m_ssd_conv_b, m_dt_bias, m_a_log, m_d_skip, m_ssd_norm_w, m_w_s_out, m_w_o, m_norm_ffn_w, m_w_up, m_ffn_conv_w, m_ffn_conv_b, m_w_down, m_final_norm_w, v_norm_mix_w, v_w_in, v_conv_a_w, v_w_a_out, v_ssd_conv_w, v_ssd_conv_b, v_dt_bias, v_a_log, v_d_skip, v_ssd_norm_w, v_w_s_out, v_w_o, v_norm_ffn_w, v_w_up, v_ffn_conv_w, v_ffn_conv_b, v_w_down, v_final_norm_w):
    wts = dict(norm_mix_w=norm_mix_w, w_in=w_in, conv_a_w=conv_a_w, w_a_out=w_a_out, ssd_conv_w=ssd_conv_w,
               ssd_conv_b=ssd_conv_b, dt_bias=dt_bias, a_log=a_log, d_skip=d_skip, ssd_norm_w=ssd_norm_w,
               w_s_out=w_s_out, w_o=w_o, norm_ffn_w=norm_ffn_w, w_up=w_up, ffn_conv_w=ffn_conv_w,
               ffn_conv_b=ffn_conv_b, w_down=w_down, final_norm_w=final_norm_w)
    mom1 = dict(norm_mix_w=m_norm_mix_w, w_in=m_w_in, conv_a_w=m_conv_a_w, w_a_out=m_w_a_out,
                ssd_conv_w=m_ssd_conv_w, ssd_conv_b=m_ssd_conv_b, dt_bias=m_dt_bias, a_log=m_a_log, d_skip=m_d_skip,
                ssd_norm_w=m_ssd_norm_w, w_s_out=m_w_s_out, w_o=m_w_o, norm_ffn_w=m_norm_ffn_w, w_up=m_w_up,
                ffn_conv_w=m_ffn_conv_w, ffn_conv_b=m_ffn_conv_b, w_down=m_w_down, final_norm_w=m_final_norm_w)
    mom2 = dict(norm_mix_w=v_norm_mix_w, w_in=v_w_in, conv_a_w=v_conv_a_w, w_a_out=v_w_a_out,
                ssd_conv_w=v_ssd_conv_w, ssd_conv_b=v_ssd_conv_b, dt_bias=v_dt_bias, a_log=v_a_log, d_skip=v_d_skip,
                ssd_norm_w=v_ssd_norm_w, w_s_out=v_w_s_out, w_o=v_w_o, norm_ffn_w=v_norm_ffn_w, w_up=v_w_up,
                ffn_conv_w=v_ffn_conv_w, ffn_conv_b=v_ffn_conv_b, w_down=v_w_down, final_norm_w=v_final_norm_w)

    t, d = x.shape[1], x.shape[2]
    di = 2 * d
    nh = di // HEAD_DIM
    dxw = di + 2 * N_GROUPS * D_STATE
    f = w_down.shape[1] * N_DEV
    n_in = w_in.shape[2] * N_DEV
    me = 4 * lax.axis_index("x") + 2 * lax.axis_index("y") + lax.axis_index("c")

    rest_rows = [_as_rows(k, wts[k]) for k in REST]
    nrows = [a.shape[0] for a in rest_rows]
    offs = [sum(nrows[:i]) for i in range(len(REST))]
    assert all(o % ROW_ALIGN == 0 for o in offs)
    r_rest = -(-sum(nrows) // ROW_ALIGN) * ROW_ALIGN
    n_blk = w_in.shape[2]
    r_in = -(-n_blk // ROW_ALIGN) * ROW_ALIGN
    in_local = _pad_rows(w_in[0].T, r_in).astype(BF16)
    rest_local = _pad_rows(jnp.concatenate(rest_rows, axis=0), r_rest).astype(BF16)
    conv_shapes = [wts[k].shape[1:] for k in CONVS]
    conv_local = _pack([wts[k] for k in CONVS], d, 8)
    x2, tgt = x[0], loss_target[0]
    h_in = _xchg_start([in_local, conv_local], [False, False], x2, "gather_in_start")
    u = _rms_fwd(x2, norm_mix_w, "norm_mix")
    in_all, conv_all = _xchg_wait(h_in, u, "gather_in_wait")
    win_t = in_all[:, :n_blk].reshape(n_in, d)
    h_rest = _xchg_start([rest_local], [False], in_all, "gather_rest_start")
    c_a, c_s, c_f = _unpack(conv_all, conv_shapes, N_DEV)
    caw, scw, fcw = _blocks_to_cols(c_a), _blocks_to_cols(c_s), _blocks_to_cols(c_f)

    o_z, o_x, o_dt = 5 * d, 7 * d, 7 * d + dxw
    seg_bounds = [0, d, 2 * d, 3 * d, 4 * d, o_z, o_x, o_dt]
    w_dt = _pad_rows(win_t[o_dt:], DT_LANES)
    dtb, alog = (_pad_lanes(p[...].reshape(1, nh), DT_LANES) for p in (dt_bias, a_log))
    dskx = jnp.repeat(d_skip.reshape(1, nh), HEAD_DIM, axis=1)

    tok = h_rest["token"]
    gates = _mm([(u, win_t[:2 * d])], "nt", BF16, "proj_gates", after=tok)
    pa = _mm([(u, win_t[2 * d:o_z])], "nt", BF16, "proj_a", after=tok)
    z = _mm([(u, win_t[o_z:o_x])], "nt", BF16, "proj_z", after=tok)
    xbc = _mm([(u, win_t[o_x:o_dt])], "nt", BF16, "proj_xbc", after=tok)
    dtr = _mm([(u, w_dt)], "nt", F32, "proj_dt", after=tok)
    ya_in, q_a = _conv_a_fwd(pa, caw, d, "conv_a")
    xc, pre_s = _conv_s_fwd(xbc, scw, ssd_conv_b, "conv_s")
    y, states = _ssd_fwd(xc, dtr, dtb, alog, dskx, di, "ssd")
    yn = _gnorm_fwd(y, z, ssd_norm_w, "gnorm")
    (rest_all,) = _xchg_wait(h_rest, yn, "gather_rest_wait")
    full = {k: rest_all[:, o:o + n].reshape(N_DEV * n, d) for k, o, n in zip(REST, offs, nrows)}
    waout, wsout, wo, wup_t, wdown = (full[k] for k in REST)
    y_a = _mm([(ya_in, waout)], "nn", BF16, "a_out")
    y_s = _mm([(yn, wsout)], "nn", BF16, "s_out")
    merged = _merge_fwd(gates, y_a, y_s, d, "merge")
    mo = _mm([(merged, wo)], "nn", F32, "o_proj")
    h1, v = _resnorm_fwd(x2, mo, norm_ffn_w, "norm_ffn")
    hv = _mm([(v, wup_t)], "nt", BF16, "up_proj")
    act, c1 = _ffn_fwd(hv, fcw, ffn_conv_b, f, "ffn_act")
    dd = _mm([(act, wdown)], "nn", F32, "down_proj")
    loss11, dh2, dh2b, g_fnw = _final(h1, dd, tgt, final_norm_w.reshape(1, d), "final")

    dact = _mm([(dh2b, wdown)], "nt", BF16, "d_act")
    gw_down = _mm_tn(act, dh2b, "gw_down")
    dh1f, dh3, g_ffn = _ffn_bwd(hv, c1, dact, fcw, f, "ffn_act_bwd")
    dv = _mm([(dh1f, wup_t[:f]), (dh3, wup_t[f:])], "nn", F32, "d_v")
    gw_up_t = jnp.concatenate([_mm_tn(dh1f, v, "gw_up1"), _mm_tn(dh3, v, "gw_up3")], axis=0)
    dh1, dh1b, g_nfw = _rms_bwd(h1, dv, norm_ffn_w, dh2, "norm_ffn_bwd")
    dmerged = _mm([(dh1b, wo)], "nt", BF16, "d_merged")
    gw_o = _mm_tn(merged, dh1b, "gw_o")
    dya, dys, dga, dgs = _merge_bwd(dmerged, gates, y_a, y_s, d, "merge_bwd")
    dyain = _mm([(dya, waout)], "nt", BF16, "d_ya_in")
    gw_aout = _mm_tn(ya_in, dya, "gw_a_out")
    db, dc, dvv, g_caw = _conv_a_bwd(pa, q_a, dyain, caw, d, "conv_a_bwd")
    dyn = _mm([(dys, wsout)], "nt", BF16, "d_yn")
    gw_sout = _mm_tn(yn, dys, "gw_s_out")
    grads_rest = dict(w_a_out=gw_aout, w_s_out=gw_sout, w_o=gw_o, w_up=gw_up_t, w_down=gw_down)
    rest_parts = jnp.concatenate([grads_rest[k].reshape(N_DEV, n, d) for k, n in zip(REST, nrows)], axis=1)
    rest_parts = jnp.pad(rest_parts, ((0, 0), (0, r_rest - sum(nrows)), (0, 0))).astype(BF16)
    h_grest = _xchg_start([rest_parts], [True], rest_parts, "scatter_rest_start")
    dy, dz, g_snw = _gnorm_bwd(y, z, dyn, ssd_norm_w, "gnorm_bwd")
    dtb_after = dtb + h_grest["token"][0:1, 0:1]
    dxc, ddtr, g_ssd = _ssd_bwd(xc, dtr, dy, states, dtb_after, alog, dskx, di, "ssd_bwd")
    dxbc, g_scw = _conv_s_bwd(xbc, pre_s, dxc, scw, "conv_s_bwd")
    dsegs = [dga, dgs, db, dc, dvv, dz, dxbc]
    pairs = [(s, win_t[a:b]) for s, a, b in zip(dsegs, seg_bounds[:-1], seg_bounds[1:])]
    pairs.append((ddtr.astype(BF16), w_dt))
    gw_in_t = jnp.concatenate([_mm_tn(s, u, "gw_in%d" % i) for i, (s, _) in enumerate(pairs)], axis=0)[:n_in]
    in_parts = jnp.pad(gw_in_t.reshape(N_DEV, n_blk, d), ((0, 0), (0, r_in - n_blk), (0, 0))).astype(BF16)
    h_gin = _xchg_start([in_parts], [True], in_parts, "scatter_in_start")
    du = _mm(pairs, "nn", F32, "d_u", tm=512, tn=512, after=h_gin["token"])
    dx, _, g_nmw = _rms_bwd(x2, du, norm_mix_w, dh1, "norm_mix_bwd")

    small_grads = dict(norm_mix_w=g_nmw[0], ssd_conv_b=g_scw[4], dt_bias=g_ssd[0, :nh], a_log=g_ssd[1, :nh],
                       d_skip=g_ssd[2, :nh], ssd_norm_w=g_snw[0], norm_ffn_w=g_nfw[0], ffn_conv_b=g_ffn[3],
                       final_norm_w=g_fnw[0], conv_a_w=g_caw[:3], ssd_conv_w=g_scw[:4], ffn_conv_w=g_ffn[:3])
    small_names = REPL + CONVS
    small_parts = _pack([small_grads[k] for k in small_names], d, 8)
    h_small = _xchg_start([small_parts], [False], small_parts, "gather_small_start")
    (rest_recv,) = _xchg_wait(h_grest, dx, "scatter_rest_wait")
    (in_recv,) = _xchg_wait(h_gin, rest_recv, "scatter_in_wait")
    (small_all,) = _xchg_wait(h_small, in_recv, "gather_small_wait")
    small_sum = _sum_parts(small_all, "sum_small_grads")
    small_g = dict(zip(small_names, _unpack(small_sum, [small_grads[k].shape for k in small_names])))

    rest_sum = _sum_parts(rest_recv, "sum_rest_grads", tile=ROW_ALIGN)
    in_sum = _sum_parts(in_recv, "sum_in_grads", tile=ROW_ALIGN)
    res = {}

    def update(k, g):
        outs = _adamw(g.reshape(1, -1, g.shape[-1]), *(src[k].reshape(-1, g.shape[-1]) for src in (wts, mom1, mom2)),
                      "adamw_" + k)
        for kind, a in zip(("g", "d", "m", "v"), outs):
            res[kind, k] = a.reshape(wts[k].shape)

    update("w_in", in_sum[:n_blk].T)
    for k, o, n in zip(REST, offs, nrows):
        rows = rest_sum[o:o + n]
        update(k, rows.T if k in TRANSPOSED else rows)
    local_g = {}
    for k in REPL:
        local_g[k] = small_g[k].reshape(wts[k].shape)
    for k in CONVS:
        n = wts[k].shape[2]
        local_g[k] = lax.dynamic_slice_in_dim(small_g[k], me * n, n, axis=1)[None]
    w_sm, m_sm, v_sm = (_pack([src[k] for k in small_names], d, 8) for src in (wts, mom1, mom2))
    g_sm = _pack([local_g[k] for k in small_names], d, 8)
    outs_sm = _adamw(g_sm[None], w_sm, m_sm, v_sm, "adamw_small")
    for kind, packed in zip(("g", "d", "m", "v"), outs_sm):
        for k, a in zip(small_names, _unpack(packed, [wts[k].shape for k in small_names])):
            res[kind, k] = a

    loss = lax.psum(loss11[0, 0], ("x", "y", "c"))
    return (loss, dx[None], *[res["g", k] for k in ORDER], *[res["d", k] for k in ORDER],
            *[res["m", k] for k in ORDER], *[res["v", k] for k in ORDER])
```

```python
import functools
import math

import jax
import jax.numpy as jnp
from jax import lax
from jax.experimental import pallas as pl
from jax.experimental.pallas import tpu as pltpu

F32 = jnp.float32
BF16 = jnp.bfloat16
EPS = 1e-5
HEAD_DIM = 64
N_GROUPS = 4
D_STATE = 128
CHUNK = 128
DT_LANES = 128
HALO = 16
STRIP = 16
N_DEV = 8
V7X_VMEM_LIMIT = 56 * 1024 * 1024
ADAM_LR, ADAM_B1, ADAM_B2, ADAM_EPS, ADAM_WD, ADAM_STEP = 0.001, 0.9, 0.999, 1e-08, 0.01, 10
HIGHEST = lax.Precision.HIGHEST
MESH = pl.DeviceIdType.MESH


def _pc(body, **kw):
    return pl.pallas_call(body, **kw)


def _params():
    return pltpu.CompilerParams(vmem_limit_bytes=V7X_VMEM_LIMIT)


def _pick(n, cands):
    for c in cands:
        if n % c == 0:
            return c
    return n


def _dot(a, b, ca, cb, prec=None):
    return lax.dot_general(a, b, (((ca,), (cb,)), ((), ())), preferred_element_type=F32, precision=prec)


def _sigmoid(x):
    return 0.5 * jnp.tanh(0.5 * x) + 0.5


def _sds(shape, dtype):
    return jax.ShapeDtypeStruct(shape, dtype)


def _mm(pairs, mode, out_dtype, name, tm=1024, tn=1024, after=None):
    m = pairs[0][0].shape[0]
    n = pairs[0][1].shape[1] if mode == "nn" else pairs[0][1].shape[0]
    tm = min(tm, m)
    tn = _pick(n, (tn, 1408, 512, 256, 128))
    npair = len(pairs)
    cb = 0 if mode == "nn" else 1

    def body(*refs):
        o_ref = refs[-1]
        acc = None
        for p in range(npair):
            part = _dot(refs[2 * p][...], refs[2 * p + 1][...], 1, cb)
            acc = part if acc is None else acc + part
        o_ref[...] = acc.astype(o_ref.dtype)

    in_specs, args = [], []
    for a, b in pairs:
        k = a.shape[1]
        in_specs.append(pl.BlockSpec((tm, k), lambda i, j: (i, 0)))
        if mode == "nn":
            in_specs.append(pl.BlockSpec((k, tn), lambda i, j: (0, j)))
        else:
            in_specs.append(pl.BlockSpec((tn, k), lambda i, j: (j, 0)))
        args += [a, b]
    if after is not None:
        in_specs.append(pl.BlockSpec(memory_space=pl.ANY))
        args.append(after)
    return _pc(body, name=name, grid=(m // tm, n // tn), in_specs=in_specs,
               out_specs=pl.BlockSpec((tm, tn), lambda i, j: (i, j)),
               out_shape=_sds((m, n), out_dtype), compiler_params=_params())(*args)


def _mm_tn(a, b, name, tm=1024):
    m, ka = a.shape
    nb = b.shape[1]
    tm = min(tm, m)
    tk = _pick(ka, (1024, 1408, 512, 256, 128))
    tn = _pick(nb, (1024, 512, 256, 128))

    def body(a_ref, b_ref, o_ref):
        @pl.when(pl.program_id(2) == 0)
        def _():
            o_ref[...] = jnp.zeros_like(o_ref)
        o_ref[...] += _dot(a_ref[...], b_ref[...], 0, 0)

    return _pc(body, name=name, grid=(ka // tk, nb // tn, m // tm),
               in_specs=[pl.BlockSpec((tm, tk), lambda i, j, t: (t, i)),
                         pl.BlockSpec((tm, tn), lambda i, j, t: (t, j))],
               out_specs=pl.BlockSpec((tk, tn), lambda i, j, t: (i, j)),
               out_shape=_sds((ka, nb), F32), compiler_params=_params())(a, b)


def _strips(tm, strip=STRIP):
    return [slice(r * strip, (r + 1) * strip) for r in range(tm // strip)]


def _fold8(a):
    out = a[0:8, :]
    for r in range(8, a.shape[0], 8):
        out = out + a[r:r + 8, :]
    return out


def _colsum(a8):
    return jnp.sum(a8, axis=0, keepdims=True)


def _rms_fwd(x, w, name):
    t, d = x.shape
    tm = min(512, t)

    def body(x_ref, w_ref, o_ref):
        wv = w_ref[...]
        for rows in _strips(tm):
            xv = x_ref[rows, :]
            r = lax.rsqrt(jnp.mean(xv * xv, axis=-1, keepdims=True) + EPS)
            o_ref[rows, :] = (xv * r * wv).astype(o_ref.dtype)

    return _pc(body, name=name, grid=(t // tm,),
               in_specs=[pl.BlockSpec((tm, d), lambda i: (i, 0)), pl.BlockSpec((1, d), lambda i: (0, 0))],
               out_specs=pl.BlockSpec((tm, d), lambda i: (i, 0)),
               out_shape=_sds((t, d), BF16), compiler_params=_params())(x, w)


def _resnorm_fwd(x, mo, w, name):
    t, d = x.shape
    tm = min(512, t)

    def body(x_ref, mo_ref, w_ref, h_ref, v_ref):
        wv = w_ref[...]
        for rows in _strips(tm):
            h = x_ref[rows, :] + mo_ref[rows, :]
            r = lax.rsqrt(jnp.mean(h * h, axis=-1, keepdims=True) + EPS)
            h_ref[rows, :] = h
            v_ref[rows, :] = (h * r * wv).astype(v_ref.dtype)

    row = pl.BlockSpec((tm, d), lambda i: (i, 0))
    return _pc(body, name=name, grid=(t // tm,),
               in_specs=[row, row, pl.BlockSpec((1, d), lambda i: (0, 0))],
               out_specs=[row, row], out_shape=[_sds((t, d), F32), _sds((t, d), BF16)],
               compiler_params=_params())(x, mo, w)


def _rms_bwd(h, dy, w, dres, name):
    t, d = h.shape
    tm = min(512, t)

    def body(h_ref, dy_ref, w_ref, dres_ref, dx_ref, dxb_ref, dw_ref):
        @pl.when(pl.program_id(0) == 0)
        def _():
            dw_ref[...] = jnp.zeros_like(dw_ref)
        wv = w_ref[...]
        acc = jnp.zeros((8, d), F32)
        for rows in _strips(tm):
            hv = h_ref[rows, :]
            dyv = dy_ref[rows, :]
            r = lax.rsqrt(jnp.mean(hv * hv, axis=-1, keepdims=True) + EPS)
            n = hv * r
            dn = dyv * wv
            acc = acc + _fold8(dyv * n)
            dx = dres_ref[rows, :] + r * (dn - n * jnp.mean(dn * n, axis=-1, keepdims=True))
            dx_ref[rows, :] = dx
            dxb_ref[rows, :] = dx.astype(BF16)
        dw_ref[0:1, :] += _colsum(acc)

    row = pl.BlockSpec((tm, d), lambda i: (i, 0))
    return _pc(body, name=name, grid=(t // tm,),
               in_specs=[row, row, pl.BlockSpec((1, d), lambda i: (0, 0)), row],
               out_specs=[row, row, pl.BlockSpec((8, d), lambda i: (0, 0))],
               out_shape=[_sds((t, d), F32), _sds((t, d), BF16), _sds((8, d), F32)],
               compiler_params=_params())(h, dy, w, dres)


def _final(h1, dd, tgt, w, name):
    t, d = h1.shape
    tm = min(512, t)
    nt = t // tm

    def body(h1_ref, dd_ref, tgt_ref, w_ref, loss_ref, dh_ref, dhb_ref, dw_ref, acc):
        i = pl.program_id(0)

        @pl.when(i == 0)
        def _():
            dw_ref[...] = jnp.zeros_like(dw_ref)
            acc[...] = jnp.zeros_like(acc)
        wv = w_ref[...]
        sq = jnp.zeros((8, d), F32)
        dw = jnp.zeros((8, d), F32)
        for rows in _strips(tm):
            h = h1_ref[rows, :] + dd_ref[rows, :]
            r = lax.rsqrt(jnp.mean(h * h, axis=-1, keepdims=True) + EPS)
            n = h * r
            e = n * wv - tgt_ref[rows, :]
            sq = sq + _fold8(e * e)
            dout = e * (1.0 / d)
            dn = dout * wv
            dw = dw + _fold8(dout * n)
            dh = r * (dn - n * jnp.mean(dn * n, axis=-1, keepdims=True))
            dh_ref[rows, :] = dh
            dhb_ref[rows, :] = dh.astype(BF16)
        acc[...] += _colsum(sq)
        dw_ref[0:1, :] += _colsum(dw)

        @pl.when(i == nt - 1)
        def _():
            loss_ref[...] = jnp.sum(acc[...], axis=-1, keepdims=True) * (0.5 / d)

    row = pl.BlockSpec((tm, d), lambda i: (i, 0))
    return _pc(body, name=name, grid=(nt,),
               in_specs=[row, row, row, pl.BlockSpec((1, d), lambda i: (0, 0))],
               out_specs=[pl.BlockSpec((1, 1), lambda i: (0, 0)), row, row, pl.BlockSpec((8, d), lambda i: (0, 0))],
               out_shape=[_sds((1, 1), F32), _sds((t, d), F32), _sds((t, d), BF16), _sds((8, d), F32)],
               scratch_shapes=[pltpu.VMEM((1, d), F32)], compiler_params=_params())(h1, dd, tgt, w)


def _tile_specs(t, tm, tc, col0):
    th = tm // HALO
    last = t // HALO - 1
    cur = pl.BlockSpec((tm, tc), lambda j, i: (i, col0 + j))
    prev = pl.BlockSpec((HALO, tc), lambda j, i: (jnp.maximum(i * th - 1, 0), col0 + j))
    nxt = pl.BlockSpec((HALO, tc), lambda j, i: (jnp.minimum((i + 1) * th, last), col0 + j))
    return cur, prev, nxt


def _conv_strip(buf, w, k, rows):
    out = None
    for j in range(k):
        term = w[j:j + 1, :] * buf[pl.ds(HALO - (k - 1) + j + rows.start, STRIP), :]
        out = term if out is None else out + term
    return out


def _conv_backward(dbuf, x_strip, emit, w, acc_ref, k, tm, with_bias):
    tc = dbuf.shape[1]
    accs = [jnp.zeros((8, tc), F32) for _ in range(k + int(with_bias))]
    for rows in _strips(tm):
        xs = x_strip(rows)
        dx = None
        for j in range(k):
            ds = dbuf[pl.ds(rows.start + k - 1 - j, STRIP), :]
            term = w[j:j + 1, :] * ds
            dx = term if dx is None else dx + term
            accs[j] = accs[j] + _fold8(ds * xs)
            if with_bias and j == k - 1:
                accs[k] = accs[k] + _fold8(ds)
        emit(rows, dx)
    for j, a in enumerate(accs):
        acc_ref[j:j + 1, :] += _colsum(a)


def _conv_a_fwd(pa, w, d, name):
    t = pa.shape[0]
    tm, tc = min(512, t), _pick(d, (512, 256, 128))
    nd = d // tc

    def body(b_ref, c_ref, v_ref, cp_ref, vp_ref, w_ref, o_ref, q_ref, buf):
        keep = (pl.program_id(1) > 0).astype(F32)
        buf[0:HALO, :] = cp_ref[...].astype(F32) * vp_ref[...].astype(F32) * keep
        for rows in _strips(tm):
            buf[HALO + rows.start:HALO + rows.stop, :] = c_ref[rows, :].astype(F32) * v_ref[rows, :].astype(F32)
        wv = w_ref[...]
        for rows in _strips(tm):
            q = _conv_strip(buf, wv, 3, rows)
            q_ref[rows, :] = q.astype(q_ref.dtype)
            o_ref[rows, :] = (b_ref[rows, :].astype(F32) * q).astype(o_ref.dtype)

    b_cur, _, _ = _tile_specs(t, tm, tc, 0)
    c_cur, c_prev, _ = _tile_specs(t, tm, tc, nd)
    v_cur, v_prev, _ = _tile_specs(t, tm, tc, 2 * nd)
    return _pc(body, name=name, grid=(nd, t // tm),
               in_specs=[b_cur, c_cur, v_cur, c_prev, v_prev, pl.BlockSpec((3, tc), lambda j, i: (0, j))],
               out_specs=[pl.BlockSpec((tm, tc), lambda j, i: (i, j))] * 2,
               out_shape=[_sds((t, d), BF16)] * 2,
               scratch_shapes=[pltpu.VMEM((tm + HALO, tc), F32)],
               compiler_params=_params())(pa, pa, pa, pa, pa, w)


def _conv_a_bwd(pa, q, dya, w, d, name):
    t = pa.shape[0]
    tm, tc = min(512, t), _pick(d, (512, 256, 128))
    nd, nt = d // tc, t // tm

    def body(b_ref, c_ref, v_ref, bn_ref, q_ref, g_ref, gn_ref, w_ref, db_ref, dc_ref, dv_ref, acc_ref, dbuf):
        i = pl.program_id(1)

        @pl.when(i == 0)
        def _():
            acc_ref[...] = jnp.zeros_like(acc_ref)
        for rows in _strips(tm):
            g = g_ref[rows, :].astype(F32)
            dbuf[rows, :] = g * b_ref[rows, :].astype(F32)
            db_ref[rows, :] = (g * q_ref[rows, :].astype(F32)).astype(BF16)
        dbuf[tm:tm + HALO, :] = gn_ref[...].astype(F32) * bn_ref[...].astype(F32) * (i < nt - 1).astype(F32)

        def emit(rows, dp):
            dc_ref[rows, :] = (dp * v_ref[rows, :].astype(F32)).astype(BF16)
            dv_ref[rows, :] = (dp * c_ref[rows, :].astype(F32)).astype(BF16)

        _conv_backward(dbuf, lambda rows: c_ref[rows, :].astype(F32) * v_ref[rows, :].astype(F32), emit,
                       w_ref[...], acc_ref, 3, tm, False)

    b_cur, _, b_next = _tile_specs(t, tm, tc, 0)
    c_cur, _, _ = _tile_specs(t, tm, tc, nd)
    v_cur, _, _ = _tile_specs(t, tm, tc, 2 * nd)
    g_cur, _, g_next = _tile_specs(t, tm, tc, 0)
    out = pl.BlockSpec((tm, tc), lambda j, i: (i, j))
    return _pc(body, name=name, grid=(nd, nt),
               in_specs=[b_cur, c_cur, v_cur, b_next, g_cur, g_cur, g_next,
                         pl.BlockSpec((3, tc), lambda j, i: (0, j))],
               out_specs=[out, out, out, pl.BlockSpec((8, tc), lambda j, i: (0, j))],
               out_shape=[_sds((t, d), BF16)] * 3 + [_sds((8, d), F32)],
               scratch_shapes=[pltpu.VMEM((tm + HALO, tc), F32)],
               compiler_params=_params())(pa, pa, pa, pa, q, dya, dya, w)


def _conv_s_fwd(xbc, w, b, name):
    t, dx = xbc.shape
    tm, tc = min(512, t), _pick(dx, (512, 256, 128))

    def body(x_ref, xp_ref, w_ref, b_ref, o_ref, pre_ref, buf):
        buf[0:HALO, :] = xp_ref[...].astype(F32) * (pl.program_id(1) > 0).astype(F32)
        for rows in _strips(tm):
            buf[HALO + rows.start:HALO + rows.stop, :] = x_ref[rows, :].astype(F32)
        wv, bv = w_ref[...], b_ref[...]
        for rows in _strips(tm):
            pre = _conv_strip(buf, wv, 4, rows) + bv
            pre_ref[rows, :] = pre.astype(pre_ref.dtype)
            o_ref[rows, :] = (pre * _sigmoid(pre)).astype(o_ref.dtype)

    cur, prev, _ = _tile_specs(t, tm, tc, 0)
    return _pc(body, name=name, grid=(dx // tc, t // tm),
               in_specs=[cur, prev, pl.BlockSpec((4, tc), lambda j, i: (0, j)),
                         pl.BlockSpec((1, tc), lambda j, i: (0, j))],
               out_specs=[pl.BlockSpec((tm, tc), lambda j, i: (i, j))] * 2,
               out_shape=[_sds((t, dx), BF16)] * 2,
               scratch_shapes=[pltpu.VMEM((tm + HALO, tc), F32)],
               compiler_params=_params())(xbc, xbc, w, b)


def _dsilu(pre):
    s = _sigmoid(pre)
    return s * (1.0 + pre * (1.0 - s))


def _conv_s_bwd(xbc, pre, dxc, w, name):
    t, dx = xbc.shape
    tm, tc = min(512, t), _pick(dx, (512, 256, 128))
    nt = t // tm

    def body(x_ref, p_ref, pn_ref, g_ref, gn_ref, w_ref, dx_ref, acc_ref, dbuf):
        i = pl.program_id(1)

        @pl.when(i == 0)
        def _():
            acc_ref[...] = jnp.zeros_like(acc_ref)
        for rows in _strips(tm):
            dbuf[rows, :] = g_ref[rows, :].astype(F32) * _dsilu(p_ref[rows, :].astype(F32))
        dbuf[tm:tm + HALO, :] = (gn_ref[...].astype(F32) * _dsilu(pn_ref[...].astype(F32))
                                 * (i < nt - 1).astype(F32))

        def emit(rows, d_in):
            dx_ref[rows, :] = d_in.astype(BF16)

        _conv_backward(dbuf, lambda rows: x_ref[rows, :].astype(F32), emit, w_ref[...], acc_ref, 4, tm, True)

    cur, _, nxt = _tile_specs(t, tm, tc, 0)
    return _pc(body, name=name, grid=(dx // tc, nt),
               in_specs=[cur, cur, nxt, cur, nxt, pl.BlockSpec((4, tc), lambda j, i: (0, j))],
               out_specs=[pl.BlockSpec((tm, tc), lambda j, i: (i, j)), pl.BlockSpec((8, tc), lambda j, i: (0, j))],
               out_shape=[_sds((t, dx), BF16), _sds((8, dx), F32)],
               scratch_shapes=[pltpu.VMEM((tm + HALO, tc), F32)],
               compiler_params=_params())(xbc, pre, pre, dxc, dxc, w)


def _ffn_fwd(hv, w, b, f, name):
    t = hv.shape[0]
    tm, tc = min(512, t), _pick(f, (512, 256, 128))
    nf = f // tc

    def body(h1_ref, h1p_ref, h3_ref, w_ref, b_ref, o_ref, c1_ref, buf):
        buf[0:HALO, :] = h1p_ref[...].astype(F32) * (pl.program_id(1) > 0).astype(F32)
        for rows in _strips(tm):
            buf[HALO + rows.start:HALO + rows.stop, :] = h1_ref[rows, :].astype(F32)
        wv, bv = w_ref[...], b_ref[...]
        for rows in _strips(tm):
            c1 = _conv_strip(buf, wv, 3, rows) + bv
            c1_ref[rows, :] = c1.astype(c1_ref.dtype)
            o_ref[rows, :] = (c1 * _sigmoid(c1) * h3_ref[rows, :].astype(F32)).astype(o_ref.dtype)

    h1_cur, h1_prev, _ = _tile_specs(t, tm, tc, 0)
    h3_cur, _, _ = _tile_specs(t, tm, tc, nf)
    return _pc(body, name=name, grid=(nf, t // tm),
               in_specs=[h1_cur, h1_prev, h3_cur, pl.BlockSpec((3, tc), lambda j, i: (0, j)),
                         pl.BlockSpec((1, tc), lambda j, i: (0, j))],
               out_specs=[pl.BlockSpec((tm, tc), lambda j, i: (i, j))] * 2,
               out_shape=[_sds((t, f), BF16)] * 2,
               scratch_shapes=[pltpu.VMEM((tm + HALO, tc), F32)],
               compiler_params=_params())(hv, hv, hv, w, b)


def _ffn_bwd(hv, c1, dact, w, f, name):
    t = hv.shape[0]
    tm, tc = min(512, t), _pick(f, (512, 256, 128))
    nf, nt = f // tc, t // tm

    def body(h1_ref, h3_ref, h3n_ref, c_ref, cn_ref, g_ref, gn_ref, w_ref, dh1_ref, dh3_ref, acc_ref, dbuf):
        i = pl.program_id(1)

        @pl.when(i == 0)
        def _():
            acc_ref[...] = jnp.zeros_like(acc_ref)
        for rows in _strips(tm):
            c1v, g = c_ref[rows, :].astype(F32), g_ref[rows, :].astype(F32)
            s1 = _sigmoid(c1v)
            dh3_ref[rows, :] = (g * c1v * s1).astype(BF16)
            dbuf[rows, :] = g * h3_ref[rows, :].astype(F32) * s1 * (1.0 + c1v * (1.0 - s1))
        dbuf[tm:tm + HALO, :] = (gn_ref[...].astype(F32) * h3n_ref[...].astype(F32)
                                 * _dsilu(cn_ref[...].astype(F32)) * (i < nt - 1).astype(F32))

        def emit(rows, d_in):
            dh1_ref[rows, :] = d_in.astype(BF16)

        _conv_backward(dbuf, lambda rows: h1_ref[rows, :].astype(F32), emit, w_ref[...], acc_ref, 3, tm, True)

    h1_cur, _, _ = _tile_specs(t, tm, tc, 0)
    h3_cur, _, h3_next = _tile_specs(t, tm, tc, nf)
    g_cur, _, g_next = _tile_specs(t, tm, tc, 0)
    out = pl.BlockSpec((tm, tc), lambda j, i: (i, j))
    return _pc(body, name=name, grid=(nf, nt),
               in_specs=[h1_cur, h3_cur, h3_next, g_cur, g_next, g_cur, g_next,
                         pl.BlockSpec((3, tc), lambda j, i: (0, j))],
               out_specs=[out, out, pl.BlockSpec((8, tc), lambda j, i: (0, j))],
               out_shape=[_sds((t, f), BF16), _sds((t, f), BF16), _sds((8, f), F32)],
               scratch_shapes=[pltpu.VMEM((tm + HALO, tc), F32)],
               compiler_params=_params())(hv, hv, hv, c1, c1, dact, dact, w)


def _gnorm_fwd(y, z, w, name):
    t, di = y.shape
    gw = di // N_GROUPS
    tm = min(512, t)

    def body(y_ref, z_ref, w_ref, o_ref):
        wv = w_ref[...]
        for rows in _strips(tm):
            zv = z_ref[rows, :].astype(F32)
            yz = y_ref[rows, :].astype(F32) * zv * _sigmoid(zv)
            r = lax.rsqrt(jnp.mean(yz * yz, axis=-1, keepdims=True) + EPS)
            o_ref[rows, :] = (yz * r * wv).astype(o_ref.dtype)

    blk = pl.BlockSpec((tm, gw), lambda j, i: (i, j))
    return _pc(body, name=name, grid=(N_GROUPS, t // tm),
               in_specs=[blk, blk, pl.BlockSpec((1, gw), lambda j, i: (0, j))],
               out_specs=blk, out_shape=_sds((t, di), BF16), compiler_params=_params())(y, z, w)


def _gnorm_bwd(y, z, dyn, w, name):
    t, di = y.shape
    gw = di // N_GROUPS
    tm = min(512, t)

    def body(y_ref, z_ref, g_ref, w_ref, dy_ref, dz_ref, dw_ref):
        @pl.when(pl.program_id(1) == 0)
        def _():
            dw_ref[...] = jnp.zeros_like(dw_ref)
        wv = w_ref[...]
        acc = jnp.zeros((8, gw), F32)
        for rows in _strips(tm):
            yv, zv, g = y_ref[rows, :].astype(F32), z_ref[rows, :].astype(F32), g_ref[rows, :].astype(F32)
            s = _sigmoid(zv)
            sz = zv * s
            yz = yv * sz
            r = lax.rsqrt(jnp.mean(yz * yz, axis=-1, keepdims=True) + EPS)
            n = yz * r
            dn = g * wv
            acc = acc + _fold8(g * n)
            dyz = r * (dn - n * jnp.mean(dn * n, axis=-1, keepdims=True))
            dy_ref[rows, :] = (dyz * sz).astype(BF16)
            dz_ref[rows, :] = (dyz * yv * s * (1.0 + zv * (1.0 - s))).astype(BF16)
        dw_ref[0:1, :] += _colsum(acc)

    blk = pl.BlockSpec((tm, gw), lambda j, i: (i, j))
    return _pc(body, name=name, grid=(N_GROUPS, t // tm),
               in_specs=[blk, blk, blk, pl.BlockSpec((1, gw), lambda j, i: (0, j))],
               out_specs=[blk, blk, pl.BlockSpec((8, gw), lambda j, i: (0, j))],
               out_shape=[_sds((t, di), BF16), _sds((t, di), BF16), _sds((8, di), F32)],
               compiler_params=_params())(y, z, dyn, w)


def _merge_fwd(gates, ya, ys, d, name):
    t = ya.shape[0]
    tm, tc = min(512, t), _pick(d, (512, 256, 128))
    nd = d // tc

    def body(ga_ref, gs_ref, ya_ref, ys_ref, o_ref):
        for rows in _strips(tm):
            o_ref[rows, :] = (_sigmoid(ga_ref[rows, :].astype(F32)) * ya_ref[rows, :].astype(F32)
                              + _sigmoid(gs_ref[rows, :].astype(F32)) * ys_ref[rows, :].astype(F32)
                              ).astype(o_ref.dtype)

    blk = pl.BlockSpec((tm, tc), lambda j, i: (i, j))
    return _pc(body, name=name, grid=(nd, t // tm),
               in_specs=[blk, pl.BlockSpec((tm, tc), lambda j, i: (i, nd + j)), blk, blk],
               out_specs=blk, out_shape=_sds((t, d), BF16), compiler_params=_params())(gates, gates, ya, ys)


def _merge_bwd(dm, gates, ya, ys, d, name):
    t = ya.shape[0]
    tm, tc = min(512, t), _pick(d, (512, 256, 128))
    nd = d // tc

    def body(dm_ref, ga_ref, gs_ref, ya_ref, ys_ref, dya_ref, dys_ref, dga_ref, dgs_ref):
        for rows in _strips(tm):
            g = dm_ref[rows, :].astype(F32)
            sa, ss = _sigmoid(ga_ref[rows, :].astype(F32)), _sigmoid(gs_ref[rows, :].astype(F32))
            dya_ref[rows, :] = (g * sa).astype(BF16)
            dys_ref[rows, :] = (g * ss).astype(BF16)
            dga_ref[rows, :] = (g * ya_ref[rows, :].astype(F32) * sa * (1.0 - sa)).astype(BF16)
            dgs_ref[rows, :] = (g * ys_ref[rows, :].astype(F32) * ss * (1.0 - ss)).astype(BF16)

    blk = pl.BlockSpec((tm, tc), lambda j, i: (i, j))
    return _pc(body, name=name, grid=(nd, t // tm),
               in_specs=[blk, blk, pl.BlockSpec((tm, tc), lambda j, i: (i, nd + j)), blk, blk],
               out_specs=[blk] * 4, out_shape=[_sds((t, d), BF16)] * 4,
               compiler_params=_params())(dm, gates, gates, ya, ys)


def _ssd_chunk_terms(dtr, dtb, alog):
    xx = dtr + dtb
    dt = jnp.maximum(xx, 0.0) + jnp.log(1.0 + jnp.exp(-jnp.abs(xx)))
    a = -jnp.exp(alog)
    li = lax.broadcasted_iota(jnp.int32, (CHUNK, CHUNK), 0)
    si = lax.broadcasted_iota(jnp.int32, (CHUNK, CHUNK), 1)
    causal = li >= si
    acum = _dot(causal.astype(F32), dt * a, 1, 0, HIGHEST)
    return xx, dt, a, acum, acum.T, causal


def _split3(x):
    hi = x.astype(BF16)
    r = x - hi.astype(F32)
    mid = r.astype(BF16)
    lo = (r - mid.astype(F32)).astype(BF16)
    return hi, mid, lo


def _expand(v, e):
    hi, mid, lo = _split3(v)
    return _dot(hi, e, 1, 0) + _dot(mid, e, 1, 0) + _dot(lo, e, 1, 0)


def _segsum(s, e):
    hi, mid, lo = _split3(s)
    return _dot(hi, e, 1, 1) + _dot(mid, e, 1, 1) + _dot(lo, e, 1, 1)


def _head_maps(di):
    nh = di // HEAD_DIM
    h = jnp.arange(DT_LANES)[:, None]
    e64 = (jnp.arange(di)[None, :] // HEAD_DIM == h).astype(BF16)
    e128 = (jnp.arange(nh * CHUNK)[None, :] // CHUNK == h).astype(BF16)
    return e64, e128


def _pair_blockdiag(p, left):
    zero = jnp.zeros_like(p)
    return jnp.concatenate([jnp.where(left, p, zero), jnp.where(left, zero, p)], axis=0)


def _ssd_fwd(xc, dtr, dtb, alog, dskx, di, name):
    t = xc.shape[0]
    dx = xc.shape[1]
    nc = t // CHUNK
    nh = di // HEAD_DIM
    hpg = nh // N_GROUPS
    gw = hpg * HEAD_DIM
    boff, coff = di, di + N_GROUPS * D_STATE
    e64, e128 = _head_maps(di)

    def body(xc_ref, dtr_ref, dtb_ref, alog_ref, dsk_ref, e64_ref, e128_ref, y_ref, st_ref, state):
        @pl.when(pl.program_id(0) == 0)
        def _():
            state[...] = jnp.zeros_like(state)
        _, dt, _, acum, acum_t, causal = _ssd_chunk_terms(dtr_ref[...], dtb_ref[...], alog_ref[...])
        last = acum[CHUNK - 1:CHUNK, :]
        e64v = e64_ref[...]
        dtx = _expand(dt, e64v)
        eax = _expand(jnp.exp(acum), e64v)
        dex = _expand(dt * jnp.exp(last - acum), e64v)
        acx = _expand(acum, e128_ref[...])
        st_ref[0] = state[...]
        left = lax.broadcasted_iota(jnp.int32, (CHUNK, 2 * HEAD_DIM), 1) < HEAD_DIM
        for g in range(N_GROUPS):
            gs = slice(g * gw, (g + 1) * gw)
            bg = xc_ref[:, boff + g * D_STATE:boff + (g + 1) * D_STATE]
            cg = xc_ref[:, coff + g * D_STATE:coff + (g + 1) * D_STATE]
            gm = _dot(cg, bg, 1, 1)
            xg = xc_ref[:, gs].astype(F32)
            xdb = (xg * dtx[:, gs]).astype(BF16)
            sin = state[:, gs]
            yo = _dot(cg, sin.astype(BF16), 1, 0) * eax[:, gs]
            for jp in range(hpg // 2):
                h0 = g * hpg + 2 * jp
                ps = slice(jp * 2 * HEAD_DIM, (jp + 1) * 2 * HEAD_DIM)
                ms = []
                for hh in (h0, h0 + 1):
                    seg = acx[:, hh * CHUNK:(hh + 1) * CHUNK] - acum_t[hh:hh + 1, :]
                    ms.append((gm * jnp.exp(jnp.where(causal, seg, -1e30))).astype(BF16))
                yd = _dot(jnp.concatenate(ms, axis=1), _pair_blockdiag(xdb[:, ps], left), 1, 0)
                col = slice(g * gw + jp * 2 * HEAD_DIM, g * gw + (jp + 1) * 2 * HEAD_DIM)
                y_ref[:, col] = (yd + yo[:, ps] + dsk_ref[:, col] * xg[:, ps]).astype(y_ref.dtype)
            xe = (xg * dex[:, gs]).astype(BF16)
            state[:, gs] = eax[CHUNK - 1:CHUNK, gs] * sin + _dot(bg, xe, 0, 0)

    small = pl.BlockSpec((1, DT_LANES), lambda c: (0, 0))
    whole = lambda a: pl.BlockSpec(a.shape, lambda c: (0, 0))
    return _pc(body, name=name, grid=(nc,),
               in_specs=[pl.BlockSpec((CHUNK, dx), lambda c: (c, 0)),
                         pl.BlockSpec((CHUNK, DT_LANES), lambda c: (c, 0)), small, small,
                         whole(dskx), whole(e64), whole(e128)],
               out_specs=[pl.BlockSpec((CHUNK, di), lambda c: (c, 0)),
                          pl.BlockSpec((1, D_STATE, di), lambda c: (c, 0, 0))],
               out_shape=[_sds((t, di), BF16), _sds((nc, D_STATE, di), F32)],
               scratch_shapes=[pltpu.VMEM((D_STATE, di), F32)],
               compiler_params=_params())(xc, dtr, dtb, alog, dskx, e64, e128)


def _ssd_bwd(xc, dtr, dy, states, dtb, alog, dskx, di, name):
    t = xc.shape[0]
    dx = xc.shape[1]
    nc = t // CHUNK
    nh = di // HEAD_DIM
    hpg = nh // N_GROUPS
    gw = hpg * HEAD_DIM
    boff, coff = di, di + N_GROUPS * D_STATE
    e64, e128 = _head_maps(di)

    def body(xc_ref, dtr_ref, dy_ref, st_ref, dtb_ref, alog_ref, dsk_ref, e64_ref, e128_ref,
             dxc_ref, ddtr_ref, sm_ref, dstate, darow):
        @pl.when(pl.program_id(0) == 0)
        def _():
            dstate[...] = jnp.zeros_like(dstate)
            sm_ref[...] = jnp.zeros_like(sm_ref)
        darow[...] = jnp.zeros_like(darow)
        xx, dt, a, acum, acum_t, causal = _ssd_chunk_terms(dtr_ref[...], dtb_ref[...], alog_ref[...])
        last = acum[CHUNK - 1:CHUNK, :]
        e64v = e64_ref[...]
        dtx = _expand(dt, e64v)
        eax = _expand(jnp.exp(acum), e64v)
        eex = _expand(jnp.exp(last - acum), e64v)
        acx = _expand(acum, e128_ref[...])
        left = lax.broadcasted_iota(jnp.int32, (CHUNK, 2 * HEAD_DIM), 1) < HEAD_DIM
        lane = lax.broadcasted_iota(jnp.int32, (CHUNK, DT_LANES), 1)
        sub8 = lax.broadcasted_iota(jnp.int32, (8, gw), 0)
        da_col = jnp.zeros((CHUNK, DT_LANES), F32)
        ddt_col = jnp.zeros((CHUNK, DT_LANES), F32)
        rows = jnp.zeros((8, DT_LANES), F32)
        for g in range(N_GROUPS):
            gs = slice(g * gw, (g + 1) * gw)
            bg = xc_ref[:, boff + g * D_STATE:boff + (g + 1) * D_STATE]
            cg = xc_ref[:, coff + g * D_STATE:coff + (g + 1) * D_STATE]
            gm = _dot(cg, bg, 1, 1)
            e64g = e64v[:, gs]
            xg = xc_ref[:, gs].astype(F32)
            dtg, eag, eeg = dtx[:, gs], eax[:, gs], eex[:, gs]
            xd = xg * dtg
            xdb = xd.astype(BF16)
            dyb = dy_ref[:, gs]
            dyf = dyb.astype(F32)
            sin = st_ref[0, :, gs]
            sinb = sin.astype(BF16)
            ds = dstate[:, gs]
            dsb = ds.astype(BF16)
            bds = _dot(bg, dsb, 1, 0)
            dyeb = (dyf * eag).astype(BF16)
            dcg = _dot(dyeb, sinb, 1, 1)
            dstate[:, gs] = eag[CHUNK - 1:CHUNK, :] * ds + _dot(cg, dyeb, 0, 0)
            yo = _dot(cg, sinb, 1, 0) * eag
            xe = xd * eeg
            dbg = _dot(xe.astype(BF16), dsb, 1, 1)
            wterm = bds * xe
            da_col = da_col + _segsum(dyf * yo - wterm, e64g)
            dg = jnp.zeros((CHUNK, CHUNK), F32)
            dxd_parts = []
            for jp in range(hpg // 2):
                h0 = g * hpg + 2 * jp
                ps = slice(jp * 2 * HEAD_DIM, (jp + 1) * 2 * HEAD_DIM)
                lms, mfs = [], []
                for hh in (h0, h0 + 1):
                    seg = acx[:, hh * CHUNK:(hh + 1) * CHUNK] - acum_t[hh:hh + 1, :]
                    lm = jnp.exp(jnp.where(causal, seg, -1e30))
                    lms.append(lm)
                    mfs.append(gm * lm)
                mstack = jnp.concatenate([m.astype(BF16) for m in mfs], axis=0)
                dyp = dyb[:, ps]
                dxd_parts.append(_dot(mstack, _pair_blockdiag(dyp, left), 0, 0))
                dm2 = _dot(dyp, _pair_blockdiag(xdb[:, ps], left), 1, 1)
                for k, hh in enumerate((h0, h0 + 1)):
                    dm = dm2[:, k * CHUNK:(k + 1) * CHUNK]
                    dg = dg + dm * lms[k]
                    q = dm * mfs[k]
                    da_col = da_col + jnp.where(lane == hh, jnp.sum(q, axis=1, keepdims=True), 0.0)
                    darow[hh:hh + 1, :] = -jnp.sum(q, axis=0, keepdims=True)
            dxd = jnp.concatenate(dxd_parts, axis=1) + bds * eeg
            ddt_col = ddt_col + _segsum(dxd * xg, e64g)
            rsum = (jnp.where(sub8 == 0, jnp.sum(wterm, axis=0, keepdims=True), 0.0)
                    + jnp.where(sub8 == 1, jnp.sum(ds * sin, axis=0, keepdims=True), 0.0)
                    + jnp.where(sub8 == 2, jnp.sum(dyf * xg, axis=0, keepdims=True), 0.0))
            rows = rows + _segsum(rsum, e64g)
            dxc_ref[:, gs] = (dxd * dtg + dsk_ref[:, gs] * dyf).astype(dxc_ref.dtype)
            dgb = dg.astype(BF16)
            dxc_ref[:, boff + g * D_STATE:boff + (g + 1) * D_STATE] = (
                dbg + _dot(dgb, cg, 0, 0)).astype(dxc_ref.dtype)
            dxc_ref[:, coff + g * D_STATE:coff + (g + 1) * D_STATE] = (
                dcg + _dot(dgb, bg, 1, 0)).astype(dxc_ref.dtype)
        at_last = rows[0:1, :] + jnp.exp(last) * rows[1:2, :]
        is_last = lax.broadcasted_iota(jnp.int32, (CHUNK, DT_LANES), 0) == CHUNK - 1
        da = da_col + jnp.where(is_last, at_last, 0.0) + darow[...].T
        li = lax.broadcasted_iota(jnp.int32, (CHUNK, CHUNK), 0)
        si = lax.broadcasted_iota(jnp.int32, (CHUNK, CHUNK), 1)
        dla = _dot((si >= li).astype(F32), da, 1, 0, HIGHEST)
        ddtr = (ddt_col + dla * a) * _sigmoid(xx)
        ddtr_ref[...] = ddtr
        sm_ref[0:1, :] += jnp.sum(ddtr, axis=0, keepdims=True)
        sm_ref[1:2, :] += jnp.sum(dla * dt, axis=0, keepdims=True) * a
        sm_ref[2:3, :] += rows[2:3, :]

    small = pl.BlockSpec((1, DT_LANES), lambda c: (0, 0))
    whole = lambda a: pl.BlockSpec(a.shape, lambda c: (0, 0))
    rev = lambda c: (nc - 1 - c, 0)
    return _pc(body, name=name, grid=(nc,),
               in_specs=[pl.BlockSpec((CHUNK, dx), rev), pl.BlockSpec((CHUNK, DT_LANES), rev),
                         pl.BlockSpec((CHUNK, di), rev),
                         pl.BlockSpec((1, D_STATE, di), lambda c: (nc - 1 - c, 0, 0)), small, small,
                         whole(dskx), whole(e64), whole(e128)],
               out_specs=[pl.BlockSpec((CHUNK, dx), rev), pl.BlockSpec((CHUNK, DT_LANES), rev),
                          pl.BlockSpec((8, DT_LANES), lambda c: (0, 0))],
               out_shape=[_sds((t, dx), BF16), _sds((t, DT_LANES), F32), _sds((8, DT_LANES), F32)],
               scratch_shapes=[pltpu.VMEM((D_STATE, di), F32), pltpu.VMEM((DT_LANES, CHUNK), F32)],
               compiler_params=_params())(xc, dtr, dy, states, dtb, alog, dskx, e64, e128)


def _adamw(parts, w, m, v, name):
    npart, rows, width = parts.shape
    tr = _pick(rows, (64, 32, 16, 8))
    c1 = 1.0 - ADAM_B1 ** ADAM_STEP
    c2 = 1.0 - ADAM_B2 ** ADAM_STEP

    def body(p_ref, w_ref, m_ref, v_ref, g_ref, d_ref, nm_ref, nv_ref):
        for rows in _strips(tr, 8):
            g = p_ref[0, rows, :].astype(F32)
            for p in range(1, npart):
                g = g + p_ref[p, rows, :].astype(F32)
            nm = ADAM_B1 * m_ref[rows, :] + (1.0 - ADAM_B1) * g
            nv = ADAM_B2 * v_ref[rows, :] + (1.0 - ADAM_B2) * (g * g)
            g_ref[rows, :] = g
            nm_ref[rows, :] = nm
            nv_ref[rows, :] = nv
            d_ref[rows, :] = -ADAM_LR * ((nm / c1) / (jnp.sqrt(nv / c2) + ADAM_EPS) + ADAM_WD * w_ref[rows, :])

    blk = pl.BlockSpec((tr, width), lambda i: (i, 0))
    return _pc(body, name=name, grid=(rows // tr,),
               in_specs=[pl.BlockSpec((npart, tr, width), lambda i: (0, i, 0)), blk, blk, blk],
               out_specs=[blk] * 4, out_shape=[_sds((rows, width), F32)] * 4,
               compiler_params=_params())(parts, w, m, v)


def _sum_parts(parts, name, tile=None):
    npart, rows, width = parts.shape
    tile = rows if tile is None else tile

    def body(p_ref, o_ref):
        for rows_ in _strips(tile, 8 if parts.dtype == F32 else STRIP):
            g = p_ref[0, rows_, :].astype(F32)
            for p in range(1, npart):
                g = g + p_ref[p, rows_, :].astype(F32)
            o_ref[rows_, :] = g

    return _pc(body, name=name, grid=(rows // tile,),
               in_specs=[pl.BlockSpec((npart, tile, width), lambda i: (0, i, 0))],
               out_specs=pl.BlockSpec((tile, width), lambda i: (i, 0)),
               out_shape=_sds((rows, width), F32), compiler_params=_params())(parts)


def _peers():
    x, y, c = lax.axis_index("x"), lax.axis_index("y"), lax.axis_index("c")
    out = []
    for k in range(1, N_DEV):
        px = 1 - x if k & 4 else x
        py = 1 - y if k & 2 else y
        pc = 1 - c if k & 1 else c
        out.append(((px, py, pc), 4 * px + 2 * py + pc))
    return 4 * x + 2 * y + c, out


def _copies(arrays, lands, send_sems, recv_sems, scatter):
    me, peers = _peers()
    outgoing, incoming = [], []
    for k, (peer, pidx) in enumerate(peers):
        for j, (a_ref, land_ref) in enumerate(zip(arrays, lands)):
            src = a_ref.at[pidx] if scatter[j] else a_ref
            sem = len(arrays) * k + j
            for dst, bucket in ((land_ref.at[me], outgoing), (land_ref.at[pidx], incoming)):
                bucket.append(pltpu.make_async_remote_copy(
                    src_ref=src, dst_ref=dst, send_sem=send_sems.at[sem], recv_sem=recv_sems.at[sem],
                    device_id=peer, device_id_type=MESH))
    return outgoing, incoming


HBM_SPEC = pl.BlockSpec(memory_space=pltpu.HBM)
SEM_SPEC = pl.BlockSpec(memory_space=pltpu.SEMAPHORE)
ANY_SPEC = pl.BlockSpec(memory_space=pl.ANY)
EFFECT = pltpu.SideEffectType.DATAFLOW_SIDE_EFFECTING


def _xchg_start(arrays, scatter, after, name):
    n = len(arrays)
    me = 4 * lax.axis_index("x") + 2 * lax.axis_index("y") + lax.axis_index("c")
    lands = []
    for a, sc in zip(arrays, scatter):
        own = lax.dynamic_index_in_dim(a, me, 0, keepdims=True) if sc else a[None]
        shape = a.shape if sc else (N_DEV,) + a.shape
        lands.append(lax.dynamic_update_slice(lax.empty(shape, a.dtype), own, (me,) + (0,) * (len(shape) - 1)))

    def body(*refs):
        ins, outs = refs[:2 * n], refs[2 * n + 1:]
        outgoing, _ = _copies(ins[:n], ins[n:], outs[0], outs[1], scatter)
        for cp in outgoing:
            cp.start()
        outs[-1][...] = jnp.zeros_like(outs[-1])

    nsem = n * (N_DEV - 1)
    operands = [pltpu.with_memory_space_constraint(a, pltpu.HBM) for a in list(arrays) + lands]
    out = _pc(body, name=name,
              out_shape=(pltpu.SemaphoreType.DMA((nsem,)), pltpu.SemaphoreType.DMA((nsem,)),
                         *[pltpu.HBM(a.shape, a.dtype) for a in operands], _sds((8, 128), F32)),
              in_specs=[HBM_SPEC] * (2 * n) + [ANY_SPEC],
              out_specs=(SEM_SPEC, SEM_SPEC, *[HBM_SPEC] * (2 * n), pl.BlockSpec(memory_space=pltpu.VMEM)),
              input_output_aliases={i: 2 + i for i in range(2 * n)},
              compiler_params=pltpu.CompilerParams(has_side_effects=EFFECT))(*operands, after)
    return dict(sems=out[:2], thru=out[2:2 + 2 * n], token=out[-1], scatter=scatter, n=n)


def _xchg_wait(handle, after, name):
    n, scatter = handle["n"], handle["scatter"]

    def body(*refs):
        ins = refs[:2 * n]
        send_sems, recv_sems = refs[2 * n], refs[2 * n + 1]
        outgoing, incoming = _copies(ins[:n], ins[n:], send_sems, recv_sems, scatter)
        for cp in outgoing:
            cp.wait_send()
        for cp in incoming:
            cp.wait_recv()

    thru = handle["thru"]
    out = _pc(body, name=name, out_shape=tuple(pltpu.HBM(a.shape, a.dtype) for a in thru),
              in_specs=[HBM_SPEC] * (2 * n) + [SEM_SPEC, SEM_SPEC, ANY_SPEC], out_specs=tuple([HBM_SPEC] * (2 * n)),
              input_output_aliases={i: i for i in range(2 * n)},
              compiler_params=pltpu.CompilerParams(has_side_effects=EFFECT))(*thru, *handle["sems"], after)
    return out[n:]


def _pack(arrs, width, row_mult):
    flat = jnp.concatenate([a.reshape(-1) for a in arrs])
    n = flat.shape[0]
    rows = -(-n // (width * row_mult)) * row_mult
    return jnp.pad(flat, (0, rows * width - n)).reshape(rows, width)


def _unpack(packed, shapes, lead=None):
    out, off = [], 0
    flat = packed.reshape(-1) if lead is None else packed.reshape(lead, -1)
    for s in shapes:
        n = math.prod(s)
        if lead is None:
            out.append(flat[off:off + n].reshape(s))
        else:
            out.append(flat[:, off:off + n].reshape((lead,) + tuple(s)))
        off += n
    return out


def _blocks_to_cols(blocks):
    nb, rows, n = blocks.shape
    return blocks.transpose(1, 0, 2).reshape(rows, nb * n)


def _pad_rows(a, rows):
    return jnp.pad(a, ((0, rows - a.shape[0]), (0, 0)))


def _pad_lanes(a, lanes):
    return jnp.pad(a, ((0, 0), (0, lanes - a.shape[1])))


REST = ("w_a_out", "w_s_out", "w_o", "w_up", "w_down")
TRANSPOSED = ("w_up", "w_in")
ROW_ALIGN = 32
CONVS = ("conv_a_w", "ssd_conv_w", "ffn_conv_w")
REPL = ("norm_mix_w", "ssd_conv_b", "dt_bias", "a_log", "d_skip", "ssd_norm_w", "norm_ffn_w", "ffn_conv_b",
        "final_norm_w")
ORDER = ("norm_mix_w", "w_in", "conv_a_w", "w_a_out", "ssd_conv_w", "ssd_conv_b", "dt_bias", "a_log", "d_skip",
         "ssd_norm_w", "w_s_out", "w_o", "norm_ffn_w", "w_up", "ffn_conv_w", "ffn_conv_b", "w_down", "final_norm_w")


def _as_rows(name, block):
    return block[0].T if name in TRANSPOSED else block[0]


def kernel(x, norm_mix_w, w_in, conv_a_w, w_a_out, ssd_conv_w, ssd_conv_b, dt_bias, a_log, d_skip, ssd_norm_w, w_s_out, w_o, norm_ffn_w, w_up, ffn_conv_w, ffn_conv_b, w_down, final_norm_w, loss_target, m_norm_mix_w, m_w_in, m_conv_a_w, m_w_a_out, m_ssd_conv_w, m_ssd_conv_b, m_dt_bias, m_a_log, m_d_skip, m_ssd_norm_w, m_w_s_out, m_w_o, m_norm_ffn_w, m_w_up, m_ffn_conv_w, m_ffn_conv_b, m_w_down, m_final_norm_w, v_norm_mix_w, v_w_in, v_conv_a_w, v_w_a_out, v_ssd_conv_w, v_ssd_conv_b, v_dt_bias, v_a_log, v_d_skip, v_ssd_norm_w, v_w_s_out, v_w_o, v_norm_ffn_w, v_w_up, v_ffn_conv_w, v_ffn_conv_b, v_w_down, v_final_norm_w):
    wts = dict(norm_mix_w=norm_mix_w, w_in=w_in, conv_a_w=conv_a_w, w_a_out=w_a_out, ssd_conv_w=ssd_conv_w,
               ssd_conv_b=ssd_conv_b, dt_bias=dt_bias, a_log=a_log, d_skip=d_skip, ssd_norm_w=ssd_norm_w,
               w_s_out=w_s_out, w_o=w_o, norm_ffn_w=norm_ffn_w, w_up=w_up, ffn_conv_w=ffn_conv_w,
               ffn_conv_b=ffn_conv_b, w_down=w_down, final_norm_w=final_norm_w)
    mom1 = dict(norm_mix_w=m_norm_mix_w, w_in=m_w_in, conv_a_w=m_conv_a_w, w_a_out=m_w_a_out,
                ssd_conv_w=m_ssd_conv_w, ssd_conv_b=m_ssd_conv_b, dt_bias=m_dt_bias, a_log=m_a_log, d_skip=m_d_skip,
                ssd_norm_w=m_ssd_norm_w, w_s_out=m_w_s_out, w_o=m_w_o, norm_ffn_w=m_norm_ffn_w, w_up=m_w_up,
                ffn_conv_w=m_ffn_conv_w, ffn_conv_b=m_ffn_conv_b, w_down=m_w_down, final_norm_w=m_final_norm_w)
    mom2 = dict(norm_mix_w=v_norm_mix_w, w_in=v_w_in, conv_a_w=v_conv_a_w, w_a_out=v_w_a_out,
                ssd_conv_w=v_ssd_conv_w, ssd_conv_b=v_ssd_conv_b, dt_bias=v_dt_bias, a_log=v_a_log, d_skip=v_d_skip,
                ssd_norm_w=v_ssd_norm_w, w_s_out=v_w_s_out, w_o=v_w_o, norm_ffn_w=v_norm_ffn_w, w_up=v_w_up,
                ffn_conv_w=v_ffn_conv_w, ffn_conv_b=v_ffn_conv_b, w_down=v_w_down, final_norm_w=v_final_norm_w)

    t, d = x.shape[1], x.shape[2]
    di = 2 * d
    nh = di // HEAD_DIM
    dxw = di + 2 * N_GROUPS * D_STATE
    f = w_down.shape[1] * N_DEV
    n_in = w_in.shape[2] * N_DEV
    me = 4 * lax.axis_index("x") + 2 * lax.axis_index("y") + lax.axis_index("c")

    rest_rows = [_as_rows(k, wts[k]) for k in REST]
    nrows = [a.shape[0] for a in rest_rows]
    offs = [sum(nrows[:i]) for i in range(len(REST))]
    assert all(o % ROW_ALIGN == 0 for o in offs)
    r_rest = -(-sum(nrows) // ROW_ALIGN) * ROW_ALIGN
    n_blk = w_in.shape[2]
    r_in = -(-n_blk // ROW_ALIGN) * ROW_ALIGN
    in_local = _pad_rows(w_in[0].T, r_in).astype(BF16)
    rest_local = _pad_rows(jnp.concatenate(rest_rows, axis=0), r_rest).astype(BF16)
    conv_shapes = [wts[k].shape[1:] for k in CONVS]
    conv_local = _pack([wts[k] for k in CONVS], d, 8)
    x2, tgt = x[0], loss_target[0]
    h_in = _xchg_start([in_local, conv_local], [False, False], x2, "gather_in_start")
    u = _rms_fwd(x2, norm_mix_w, "norm_mix")
    in_all, conv_all = _xchg_wait(h_in, u, "gather_in_wait")
    win_t = in_all[:, :n_blk].reshape(n_in, d)
    h_rest = _xchg_start([rest_local], [False], in_all, "gather_rest_start")
    c_a, c_s, c_f = _unpack(conv_all, conv_shapes, N_DEV)
    caw, scw, fcw = _blocks_to_cols(c_a), _blocks_to_cols(c_s), _blocks_to_cols(c_f)

    o_z, o_x, o_dt = 5 * d, 7 * d, 7 * d + dxw
    seg_bounds = [0, d, 2 * d, 3 * d, 4 * d, o_z, o_x, o_dt]
    w_dt = _pad_rows(win_t[o_dt:], DT_LANES)
    dtb, alog = (_pad_lanes(p[...].reshape(1, nh), DT_LANES) for p in (dt_bias, a_log))
    dskx = jnp.repeat(d_skip.reshape(1, nh), HEAD_DIM, axis=1)

    tok = h_rest["token"]
    gates = _mm([(u, win_t[:2 * d])], "nt", BF16, "proj_gates", after=tok)
    pa = _mm([(u, win_t[2 * d:o_z])], "nt", BF16, "proj_a", after=tok)
    z = _mm([(u, win_t[o_z:o_x])], "nt", BF16, "proj_z", after=tok)
    xbc = _mm([(u, win_t[o_x:o_dt])], "nt", BF16, "proj_xbc", after=tok)
    dtr = _mm([(u, w_dt)], "nt", F32, "proj_dt", after=tok)
    ya_in, q_a = _conv_a_fwd(pa, caw, d, "conv_a")
    xc, pre_s = _conv_s_fwd(xbc, scw, ssd_conv_b, "conv_s")
    y, states = _ssd_fwd(xc, dtr, dtb, alog, dskx, di, "ssd")
    yn = _gnorm_fwd(y, z, ssd_norm_w, "gnorm")
    (rest_all,) = _xchg_wait(h_rest, yn, "gather_rest_wait")
    full = {k: rest_all[:, o:o + n].reshape(N_DEV * n, d) for k, o, n in zip(REST, offs, nrows)}
    waout, wsout, wo, wup_t, wdown = (full[k] for k in REST)
    y_a = _mm([(ya_in, waout)], "nn", BF16, "a_out")
    y_s = _mm([(yn, wsout)], "nn", BF16, "s_out")
    merged = _merge_fwd(gates, y_a, y_s, d, "merge")
    mo = _mm([(merged, wo)], "nn", F32, "o_proj")
    h1, v = _resnorm_fwd(x2, mo, norm_ffn_w, "norm_ffn")
    hv = _mm([(v, wup_t)], "nt", BF16, "up_proj")
    act, c1 = _ffn_fwd(hv, fcw, ffn_conv_b, f, "ffn_act")
    dd = _mm([(act, wdown)], "nn", F32, "down_proj")
    loss11, dh2, dh2b, g_fnw = _final(h1, dd, tgt, final_norm_w.reshape(1, d), "final")

    dact = _mm([(dh2b, wdown)], "nt", BF16, "d_act")
    gw_down = _mm_tn(act, dh2b, "gw_down")
    dh1f, dh3, g_ffn = _ffn_bwd(hv, c1, dact, fcw, f, "ffn_act_bwd")
    dv = _mm([(dh1f, wup_t[:f]), (dh3, wup_t[f:])], "nn", F32, "d_v")
    gw_up_t = jnp.concatenate([_mm_tn(dh1f, v, "gw_up1"), _mm_tn(dh3, v, "gw_up3")], axis=0)
    dh1, dh1b, g_nfw = _rms_bwd(h1, dv, norm_ffn_w, dh2, "norm_ffn_bwd")
    dmerged = _mm([(dh1b, wo)], "nt", BF16, "d_merged")
    gw_o = _mm_tn(merged, dh1b, "gw_o")
    dya, dys, dga, dgs = _merge_bwd(dmerged, gates, y_a, y_s, d, "merge_bwd")
    dyain = _mm([(dya, waout)], "nt", BF16, "d_ya_in")
    gw_aout = _mm_tn(ya_in, dya, "gw_a_out")
    db, dc, dvv, g_caw = _conv_a_bwd(pa, q_a, dyain, caw, d, "conv_a_bwd")
    dyn = _mm([(dys, wsout)], "nt", BF16, "d_yn")
    gw_sout = _mm_tn(yn, dys, "gw_s_out")
    grads_rest = dict(w_a_out=gw_aout, w_s_out=gw_sout, w_o=gw_o, w_up=gw_up_t, w_down=gw_down)
    rest_parts = jnp.concatenate([grads_rest[k].reshape(N_DEV, n, d) for k, n in zip(REST, nrows)], axis=1)
    rest_parts = jnp.pad(rest_parts, ((0, 0), (0, r_rest - sum(nrows)), (0, 0))).astype(BF16)
    h_grest = _xchg_start([rest_parts], [True], rest_parts, "scatter_rest_start")
    dy, dz, g_snw = _gnorm_bwd(y, z, dyn, ssd_norm_w, "gnorm_bwd")
    dtb_after = dtb + h_grest["token"][0:1, 0:1]
    dxc, ddtr, g_ssd = _ssd_bwd(xc, dtr, dy, states, dtb_after, alog, dskx, di, "ssd_bwd")
    dxbc, g_scw = _conv_s_bwd(xbc, pre_s, dxc, scw, "conv_s_bwd")
    dsegs = [dga, dgs, db, dc, dvv, dz, dxbc]
    pairs = [(s, win_t[a:b]) for s, a, b in zip(dsegs, seg_bounds[:-1], seg_bounds[1:])]
    pairs.append((ddtr.astype(BF16), w_dt))
    gw_in_t = jnp.concatenate([_mm_tn(s, u, "gw_in%d" % i) for i, (s, _) in enumerate(pairs)], axis=0)[:n_in]
    in_parts = jnp.pad(gw_in_t.reshape(N_DEV, n_blk, d), ((0, 0), (0, r_in - n_blk), (0, 0))).astype(BF16)
    h_gin = _xchg_start([in_parts], [True], in_parts, "scatter_in_start")
    du = _mm(pairs, "nn", F32, "d_u", tm=512, tn=512, after=h_gin["token"])
    dx, _, g_nmw = _rms_bwd(x2, du, norm_mix_w, dh1, "norm_mix_bwd")

    small_grads = dict(norm_mix_w=g_nmw[0], ssd_conv_b=g_scw[4], dt_bias=g_ssd[0, :nh], a_log=g_ssd[1, :nh],
                       d_skip=g_ssd[2, :nh], ssd_norm_w=g_snw[0], norm_ffn_w=g_nfw[0], ffn_conv_b=g_ffn[3],
                       final_norm_w=g_fnw[0], conv_a_w=g_caw[:3], ssd_conv_w=g_scw[:4], ffn_conv_w=g_ffn[:3])
    small_names = REPL + CONVS
    small_parts = _pack([small_grads[k] for k in small_names], d, 8)
    h_small = _xchg_start([small_parts], [False], small_parts, "gather_small_start")
    (rest_recv,) = _xchg_wait(h_grest, dx, "scatter_rest_wait")
    (in_recv,) = _xchg_wait(h_gin, rest_recv, "scatter_in_wait")
    (small_all,) = _xchg_wait(h_small, in_recv, "gather_small_wait")
    small_sum = _sum_parts(small_all, "sum_small_grads")
    small_g = dict(zip(small_names, _unpack(small_sum, [small_grads[k].shape for k in small_names])))

    rest_sum = _sum_parts(rest_recv, "sum_rest_grads", tile=ROW_ALIGN)
    in_sum = _sum_parts(in_recv, "sum_in_grads", tile=ROW_ALIGN)
    res = {}

    def update(k, g):
        outs = _adamw(g.reshape(1, -1, g.shape[-1]), *(src[k].reshape(-1, g.shape[-1]) for src in (wts, mom1, mom2)),
                      "adamw_" + k)
        for kind, a in zip(("g", "d", "m", "v"), outs):
            res[kind, k] = a.reshape(wts[k].shape)

    update("w_in", in_sum[:n_blk].T)
    for k, o, n in zip(REST, offs, nrows):
        rows = rest_sum[o:o + n]
        update(k, rows.T if k in TRANSPOSED else rows)
    local_g = {}
    for k in REPL:
        local_g[k] = small_g[k].reshape(wts[k].shape)
    for k in CONVS:
        n = wts[k].shape[2]
        local_g[k] = lax.dynamic_slice_in_dim(small_g[k], me * n, n, axis=1)[None]
    w_sm, m_sm, v_sm = (_pack([src[k] for k in small_names], d, 8) for src in (wts, mom1, mom2))
    g_sm = _pack([local_g[k] for k in small_names], d, 8)
    outs_sm = _adamw(g_sm[None], w_sm, m_sm, v_sm, "adamw_small")
    for kind, packed in zip(("g", "d", "m", "v"), outs_sm):
        for k, a in zip(small_names, _unpack(packed, [wts[k].shape for k in small_names])):
            res[kind, k] = a

    loss = lax.psum(loss11[0, 0], ("x", "y", "c"))
    return (loss, dx[None], *[res["g", k] for k in ORDER], *[res["d", k] for k in ORDER],
            *[res["m", k] for k in ORDER], *[res["v", k] for k in ORDER])
```

```python
import functools
import math

import jax
import jax.numpy as jnp
from jax import lax
from jax.experimental import pallas as pl
from jax.experimental.pallas import tpu as pltpu

F32 = jnp.float32
BF16 = jnp.bfloat16
EPS = 1e-5
HEAD_DIM = 64
N_GROUPS = 4
D_STATE = 128
CHUNK = 128
DT_LANES = 128
HALO = 16
STRIP = 16
N_DEV = 8
V7X_VMEM_LIMIT = 56 * 1024 * 1024
ADAM_LR, ADAM_B1, ADAM_B2, ADAM_EPS, ADAM_WD, ADAM_STEP = 0.001, 0.9, 0.999, 1e-08, 0.01, 10
HIGHEST = lax.Precision.HIGHEST
MESH = pl.DeviceIdType.MESH


def _pc(body, **kw):
    return pl.pallas_call(body, **kw)


def _params():
    return pltpu.CompilerParams(vmem_limit_bytes=V7X_VMEM_LIMIT)


def _pick(n, cands):
    for c in cands:
        if n % c == 0:
            return c
    return n


def _dot(a, b, ca, cb, prec=None):
    return lax.dot_general(a, b, (((ca,), (cb,)), ((), ())), preferred_element_type=F32, precision=prec)


def _sigmoid(x):
    return 0.5 * jnp.tanh(0.5 * x) + 0.5


def _sds(shape, dtype):
    return jax.ShapeDtypeStruct(shape, dtype)


def _mm(pairs, mode, out_dtype, name, tm=1024, tn=1024, after=None):
    m = pairs[0][0].shape[0]
    n = pairs[0][1].shape[1] if mode == "nn" else pairs[0][1].shape[0]
    tm = min(tm, m)
    tn = _pick(n, (tn, 1408, 512, 256, 128))
    npair = len(pairs)
    cb = 0 if mode == "nn" else 1

    def body(*refs):
        o_ref = refs[-1]
        acc = None
        for p in range(npair):
            part = _dot(refs[2 * p][...], refs[2 * p + 1][...], 1, cb)
            acc = part if acc is None else acc + part
        o_ref[...] = acc.astype(o_ref.dtype)

    in_specs, args = [], []
    for a, b in pairs:
        k = a.shape[1]
        in_specs.append(pl.BlockSpec((tm, k), lambda i, j: (i, 0)))
        if mode == "nn":
            in_specs.append(pl.BlockSpec((k, tn), lambda i, j: (0, j)))
        else:
            in_specs.append(pl.BlockSpec((tn, k), lambda i, j: (j, 0)))
        args += [a, b]
    if after is not None:
        in_specs.append(pl.BlockSpec(memory_space=pl.ANY))
        args.append(after)
    return _pc(body, name=name, grid=(m // tm, n // tn), in_specs=in_specs,
               out_specs=pl.BlockSpec((tm, tn), lambda i, j: (i, j)),
               out_shape=_sds((m, n), out_dtype), compiler_params=_params())(*args)


def _mm_tn(a, b, name, tm=1024):
    m, ka = a.shape
    nb = b.shape[1]
    tm = min(tm, m)
    tk = _pick(ka, (1024, 1408, 512, 256, 128))
    tn = _pick(nb, (1024, 512, 256, 128))

    def body(a_ref, b_ref, o_ref):
        @pl.when(pl.program_id(2) == 0)
        def _():
            o_ref[...] = jnp.zeros_like(o_ref)
        o_ref[...] += _dot(a_ref[...], b_ref[...], 0, 0)

    return _pc(body, name=name, grid=(ka // tk, nb // tn, m // tm),
               in_specs=[pl.BlockSpec((tm, tk), lambda i, j, t: (t, i)),
                         pl.BlockSpec((tm, tn), lambda i, j, t: (t, j))],
               out_specs=pl.BlockSpec((tk, tn), lambda i, j, t: (i, j)),
               out_shape=_sds((ka, nb), F32), compiler_params=_params())(a, b)


def _strips(tm, strip=STRIP):
    return [slice(r * strip, (r + 1) * strip) for r in range(tm // strip)]


def _fold8(a):
    out = a[0:8, :]
    for r in range(8, a.shape[0], 8):
        out = out + a[r:r + 8, :]
    return out


def _colsum(a8):
    return jnp.sum(a8, axis=0, keepdims=True)


def _rms_fwd(x, w, name):
    t, d = x.shape
    tm = min(512, t)

    def body(x_ref, w_ref, o_ref):
        wv = w_ref[...]
        for rows in _strips(tm):
            xv = x_ref[rows, :]
            r = lax.rsqrt(jnp.mean(xv * xv, axis=-1, keepdims=True) + EPS)
            o_ref[rows, :] = (xv * r * wv).astype(o_ref.dtype)

    return _pc(body, name=name, grid=(t // tm,),
               in_specs=[pl.BlockSpec((tm, d), lambda i: (i, 0)), pl.BlockSpec((1, d), lambda i: (0, 0))],
               out_specs=pl.BlockSpec((tm, d), lambda i: (i, 0)),
               out_shape=_sds((t, d), BF16), compiler_params=_params())(x, w)


def _resnorm_fwd(x, mo, w, name):
    t, d = x.shape
    tm = min(512, t)

    def body(x_ref, mo_ref, w_ref, h_ref, v_ref):
        wv = w_ref[...]
        for rows in _strips(tm):
            h = x_ref[rows, :] + mo_ref[rows, :]
            r = lax.rsqrt(jnp.mean(h * h, axis=-1, keepdims=True) + EPS)
            h_ref[rows, :] = h
            v_ref[rows, :] = (h * r * wv).astype(v_ref.dtype)

    row = pl.BlockSpec((tm, d), lambda i: (i, 0))
    return _pc(body, name=name, grid=(t // tm,),
               in_specs=[row, row, pl.BlockSpec((1, d), lambda i: (0, 0))],
               out_specs=[row, row], out_shape=[_sds((t, d), F32), _sds((t, d), BF16)],
               compiler_params=_params())(x, mo, w)


def _rms_bwd(h, dy, w, dres, name):
    t, d = h.shape
    tm = min(512, t)

    def body(h_ref, dy_ref, w_ref, dres_ref, dx_ref, dxb_ref, dw_ref):
        @pl.when(pl.program_id(0) == 0)
        def _():
            dw_ref[...] = jnp.zeros_like(dw_ref)
        wv = w_ref[...]
        acc = jnp.zeros((8, d), F32)
        for rows in _strips(tm):
            hv = h_ref[rows, :]
            dyv = dy_ref[rows, :]
            r = lax.rsqrt(jnp.mean(hv * hv, axis=-1, keepdims=True) + EPS)
            n = hv * r
            dn = dyv * wv
            acc = acc + _fold8(dyv * n)
            dx = dres_ref[rows, :] + r * (dn - n * jnp.mean(dn * n, axis=-1, keepdims=True))
            dx_ref[rows, :] = dx
            dxb_ref[rows, :] = dx.astype(BF16)
        dw_ref[0:1, :] += _colsum(acc)

    row = pl.BlockSpec((tm, d), lambda i: (i, 0))
    return _pc(body, name=name, grid=(t // tm,),
               in_specs=[row, row, pl.BlockSpec((1, d), lambda i: (0, 0)), row],
               out_specs=[row, row, pl.BlockSpec((8, d), lambda i: (0, 0))],
               out_shape=[_sds((t, d), F32), _sds((t, d), BF16), _sds((8, d), F32)],
               compiler_params=_params())(h, dy, w, dres)


def _final(h1, dd, tgt, w, name):
    t, d = h1.shape
    tm = min(512, t)
    nt = t // tm

    def body(h1_ref, dd_ref, tgt_ref, w_ref, loss_ref, dh_ref, dhb_ref, dw_ref, acc):
        i = pl.program_id(0)

        @pl.when(i == 0)
        def _():
            dw_ref[...] = jnp.zeros_like(dw_ref)
            acc[...] = jnp.zeros_like(acc)
        wv = w_ref[...]
        sq = jnp.zeros((8, d), F32)
        dw = jnp.zeros((8, d), F32)
        for rows in _strips(tm):
            h = h1_ref[rows, :] + dd_ref[rows, :]
            r = lax.rsqrt(jnp.mean(h * h, axis=-1, keepdims=True) + EPS)
            n = h * r
            e = n * wv - tgt_ref[rows, :]
            sq = sq + _fold8(e * e)
            dout = e * (1.0 / d)
            dn = dout * wv
            dw = dw + _fold8(dout * n)
            dh = r * (dn - n * jnp.mean(dn * n, axis=-1, keepdims=True))
            dh_ref[rows, :] = dh
            dhb_ref[rows, :] = dh.astype(BF16)
        acc[...] += _colsum(sq)
        dw_ref[0:1, :] += _colsum(dw)

        @pl.when(i == nt - 1)
        def _():
            loss_ref[...] = jnp.sum(acc[...], axis=-1, keepdims=True) * (0.5 / d)

    row = pl.BlockSpec((tm, d), lambda i: (i, 0))
    return _pc(body, name=name, grid=(nt,),
               in_specs=[row, row, row, pl.BlockSpec((1, d), lambda i: (0, 0))],
               out_specs=[pl.BlockSpec((1, 1), lambda i: (0, 0)), row, row, pl.BlockSpec((8, d), lambda i: (0, 0))],
               out_shape=[_sds((1, 1), F32), _sds((t, d), F32), _sds((t, d), BF16), _sds((8, d), F32)],
               scratch_shapes=[pltpu.VMEM((1, d), F32)], compiler_params=_params())(h1, dd, tgt, w)


def _tile_specs(t, tm, tc, col0):
    th = tm // HALO
    last = t // HALO - 1
    cur = pl.BlockSpec((tm, tc), lambda j, i: (i, col0 + j))
    prev = pl.BlockSpec((HALO, tc), lambda j, i: (jnp.maximum(i * th - 1, 0), col0 + j))
    nxt = pl.BlockSpec((HALO, tc), lambda j, i: (jnp.minimum((i + 1) * th, last), col0 + j))
    return cur, prev, nxt


def _conv_strip(buf, w, k, rows):
    out = None
    for j in range(k):
        term = w[j:j + 1, :] * buf[pl.ds(HALO - (k - 1) + j + rows.start, STRIP), :]
        out = term if out is None else out + term
    return out


def _conv_backward(dbuf, x_strip, emit, w, acc_ref, k, tm, with_bias):
    tc = dbuf.shape[1]
    accs = [jnp.zeros((8, tc), F32) for _ in range(k + int(with_bias))]
    for rows in _strips(tm):
        xs = x_strip(rows)
        dx = None
        for j in range(k):
            ds = dbuf[pl.ds(rows.start + k - 1 - j, STRIP), :]
            term = w[j:j + 1, :] * ds
            dx = term if dx is None else dx + term
            accs[j] = accs[j] + _fold8(ds * xs)
            if with_bias and j == k - 1:
                accs[k] = accs[k] + _fold8(ds)
        emit(rows, dx)
    for j, a in enumerate(accs):
        acc_ref[j:j + 1, :] += _colsum(a)


def _conv_a_fwd(pa, w, d, name):
    t = pa.shape[0]
    tm, tc = min(512, t), _pick(d, (512, 256, 128))
    nd = d // tc

    def body(b_ref, c_ref, v_ref, cp_ref, vp_ref, w_ref, o_ref, q_ref, buf):
        keep = (pl.program_id(1) > 0).astype(F32)
        buf[0:HALO, :] = cp_ref[...].astype(F32) * vp_ref[...].astype(F32) * keep
        for rows in _strips(tm):
            buf[HALO + rows.start:HALO + rows.stop, :] = c_ref[rows, :].astype(F32) * v_ref[rows, :].astype(F32)
        wv = w_ref[...]
        for rows in _strips(tm):
            q = _conv_strip(buf, wv, 3, rows)
            q_ref[rows, :] = q.astype(q_ref.dtype)
            o_ref[rows, :] = (b_ref[rows, :].astype(F32) * q).astype(o_ref.dtype)

    b_cur, _, _ = _tile_specs(t, tm, tc, 0)
    c_cur, c_prev, _ = _tile_specs(t, tm, tc, nd)
    v_cur, v_prev, _ = _tile_specs(t, tm, tc, 2 * nd)
    return _pc(body, name=name, grid=(nd, t // tm),
               in_specs=[b_cur, c_cur, v_cur, c_prev, v_prev, pl.BlockSpec((3, tc), lambda j, i: (0, j))],
               out_specs=[pl.BlockSpec((tm, tc), lambda j, i: (i, j))] * 2,
               out_shape=[_sds((t, d), BF16)] * 2,
               scratch_shapes=[pltpu.VMEM((tm + HALO, tc), F32)],
               compiler_params=_params())(pa, pa, pa, pa, pa, w)


def _conv_a_bwd(pa, q, dya, w, d, name):
    t = pa.shape[0]
    tm, tc = min(512, t), _pick(d, (512, 256, 128))
    nd, nt = d // tc, t // tm

    def body(b_ref, c_ref, v_ref, bn_ref, q_ref, g_ref, gn_ref, w_ref, db_ref, dc_ref, dv_ref, acc_ref, dbuf):
        i = pl.program_id(1)

        @pl.when(i == 0)
        def _():
            acc_ref[...] = jnp.zeros_like(acc_ref)
        for rows in _strips(tm):
            g = g_ref[rows, :].astype(F32)
            dbuf[rows, :] = g * b_ref[rows, :].astype(F32)
            db_ref[rows, :] = (g * q_ref[rows, :].astype(F32)).astype(BF16)
        dbuf[tm:tm + HALO, :] = gn_ref[...].astype(F32) * bn_ref[...].astype(F32) * (i < nt - 1).astype(F32)

        def emit(rows, dp):
            dc_ref[rows, :] = (dp * v_ref[rows, :].astype(F32)).astype(BF16)
            dv_ref[rows, :] = (dp * c_ref[rows, :].astype(F32)).astype(BF16)

        _conv_backward(dbuf, lambda rows: c_ref[rows, :].astype(F32) * v_ref[rows, :].astype(F32), emit,
                       w_ref[...], acc_ref, 3, tm, False)

    b_cur, _, b_next = _tile_specs(t, tm, tc, 0)
    c_cur, _, _ = _tile_specs(t, tm, tc, nd)
    v_cur, _, _ = _tile_specs(t, tm, tc, 2 * nd)
    g_cur, _, g_next = _tile_specs(t, tm, tc, 0)
    out = pl.BlockSpec((tm, tc), lambda j, i: (i, j))
    return _pc(body, name=name, grid=(nd, nt),
               in_specs=[b_cur, c_cur, v_cur, b_next, g_cur, g_cur, g_next,
                         pl.BlockSpec((3, tc), lambda j, i: (0, j))],
               out_specs=[out, out, out, pl.BlockSpec((8, tc), lambda j, i: (0, j))],
               out_shape=[_sds((t, d), BF16)] * 3 + [_sds((8, d), F32)],
               scratch_shapes=[pltpu.VMEM((tm + HALO, tc), F32)],
               compiler_params=_params())(pa, pa, pa, pa, q, dya, dya, w)


def _conv_s_fwd(xbc, w, b, name):
    t, dx = xbc.shape
    tm, tc = min(512, t), _pick(dx, (512, 256, 128))

    def body(x_ref, xp_ref, w_ref, b_ref, o_ref, pre_ref, buf):
        buf[0:HALO, :] = xp_ref[...].astype(F32) * (pl.program_id(1) > 0).astype(F32)
        for rows in _strips(tm):
            buf[HALO + rows.start:HALO + rows.stop, :] = x_ref[rows, :].astype(F32)
        wv, bv = w_ref[...], b_ref[...]
        for rows in _strips(tm):
            pre = _conv_strip(buf, wv, 4, rows) + bv
            pre_ref[rows, :] = pre.astype(pre_ref.dtype)
            o_ref[rows, :] = (pre * _sigmoid(pre)).astype(o_ref.dtype)

    cur, prev, _ = _tile_specs(t, tm, tc, 0)
    return _pc(body, name=name, grid=(dx // tc, t // tm),
               in_specs=[cur, prev, pl.BlockSpec((4, tc), lambda j, i: (0, j)),
                         pl.BlockSpec((1, tc), lambda j, i: (0, j))],
               out_specs=[pl.BlockSpec((tm, tc), lambda j, i: (i, j))] * 2,
               out_shape=[_sds((t, dx), BF16)] * 2,
               scratch_shapes=[pltpu.VMEM((tm + HALO, tc), F32)],
               compiler_params=_params())(xbc, xbc, w, b)


def _dsilu(pre):
    s = _sigmoid(pre)
    return s * (1.0 + pre * (1.0 - s))


def _conv_s_bwd(xbc, pre, dxc, w, name):
    t, dx = xbc.shape
    tm, tc = min(512, t), _pick(dx, (512, 256, 128))
    nt = t // tm

    def body(x_ref, p_ref, pn_ref, g_ref, gn_ref, w_ref, dx_ref, acc_ref, dbuf):
        i = pl.program_id(1)

        @pl.when(i == 0)
        def _():
            acc_ref[...] = jnp.zeros_like(acc_ref)
        for rows in _strips(tm):
            dbuf[rows, :] = g_ref[rows, :].astype(F32) * _dsilu(p_ref[rows, :].astype(F32))
        dbuf[tm:tm + HALO, :] = (gn_ref[...].astype(F32) * _dsilu(pn_ref[...].astype(F32))
                                 * (i < nt - 1).astype(F32))

        def emit(rows, d_in):
            dx_ref[rows, :] = d_in.astype(BF16)

        _conv_backward(dbuf, lambda rows: x_ref[rows, :].astype(F32), emit, w_ref[...], acc_ref, 4, tm, True)

    cur, _, nxt = _tile_specs(t, tm, tc, 0)
    return _pc(body, name=name, grid=(dx // tc, nt),
               in_specs=[cur, cur, nxt, cur, nxt, pl.BlockSpec((4, tc), lambda j, i: (0, j))],
               out_specs=[pl.BlockSpec((tm, tc), lambda j, i: (i, j)), pl.BlockSpec((8, tc), lambda j, i: (0, j))],
               out_shape=[_sds((t, dx), BF16), _sds((8, dx), F32)],
               scratch_shapes=[pltpu.VMEM((tm + HALO, tc), F32)],
               compiler_params=_params())(xbc, pre, pre, dxc, dxc, w)


def _ffn_fwd(hv, w, b, f, name):
    t = hv.shape[0]
    tm, tc = min(512, t), _pick(f, (512, 256, 128))
    nf = f // tc

    def body(h1_ref, h1p_ref, h3_ref, w_ref, b_ref, o_ref, c1_ref, buf):
        buf[0:HALO, :] = h1p_ref[...].astype(F32) * (pl.program_id(1) > 0).astype(F32)
        for rows in _strips(tm):
            buf[HALO + rows.start:HALO + rows.stop, :] = h1_ref[rows, :].astype(F32)
        wv, bv = w_ref[...], b_ref[...]
        for rows in _strips(tm):
            c1 = _conv_strip(buf, wv, 3, rows) + bv
            c1_ref[rows, :] = c1.astype(c1_ref.dtype)
            o_ref[rows, :] = (c1 * _sigmoid(c1) * h3_ref[rows, :].astype(F32)).astype(o_ref.dtype)

    h1_cur, h1_prev, _ = _tile_specs(t, tm, tc, 0)
    h3_cur, _, _ = _tile_specs(t, tm, tc, nf)
    return _pc(body, name=name, grid=(nf, t // tm),
               in_specs=[h1_cur, h1_prev, h3_cur, pl.BlockSpec((3, tc), lambda j, i: (0, j)),
                         pl.BlockSpec((1, tc), lambda j, i: (0, j))],
               out_specs=[pl.BlockSpec((tm, tc), lambda j, i: (i, j))] * 2,
               out_shape=[_sds((t, f), BF16)] * 2,
               scratch_shapes=[pltpu.VMEM((tm + HALO, tc), F32)],
               compiler_params=_params())(hv, hv, hv, w, b)


def _ffn_bwd(hv, c1, dact, w, f, name):
    t = hv.shape[0]
    tm, tc = min(512, t), _pick(f, (512, 256, 128))
    nf, nt = f // tc, t // tm

    def body(h1_ref, h3_ref, h3n_ref, c_ref, cn_ref, g_ref, gn_ref, w_ref, dh1_ref, dh3_ref, acc_ref, dbuf):
        i = pl.program_id(1)

        @pl.when(i == 0)
        def _():
            acc_ref[...] = jnp.zeros_like(acc_ref)
        for rows in _strips(tm):
            c1v, g = c_ref[rows, :].astype(F32), g_ref[rows, :].astype(F32)
            s1 = _sigmoid(c1v)
            dh3_ref[rows, :] = (g * c1v * s1).astype(BF16)
            dbuf[rows, :] = g * h3_ref[rows, :].astype(F32) * s1 * (1.0 + c1v * (1.0 - s1))
        dbuf[tm:tm + HALO, :] = (gn_ref[...].astype(F32) * h3n_ref[...].astype(F32)
                                 * _dsilu(cn_ref[...].astype(F32)) * (i < nt - 1).astype(F32))

        def emit(rows, d_in):
            dh1_ref[rows, :] = d_in.astype(BF16)

        _conv_backward(dbuf, lambda rows: h1_ref[rows, :].astype(F32), emit, w_ref[...], acc_ref, 3, tm, True)

    h1_cur, _, _ = _tile_specs(t, tm, tc, 0)
    h3_cur, _, h3_next = _tile_specs(t, tm, tc, nf)
    g_cur, _, g_next = _tile_specs(t, tm, tc, 0)
    out = pl.BlockSpec((tm, tc), lambda j, i: (i, j))
    return _pc(body, name=name, grid=(nf, nt),
               in_specs=[h1_cur, h3_cur, h3_next, g_cur, g_next, g_cur, g_next,
                         pl.BlockSpec((3, tc), lambda j, i: (0, j))],
               out_specs=[out, out, pl.BlockSpec((8, tc), lambda j, i: (0, j))],
               out_shape=[_sds((t, f), BF16), _sds((t, f), BF16), _sds((8, f), F32)],
               scratch_shapes=[pltpu.VMEM((tm + HALO, tc), F32)],
               compiler_params=_params())(hv, hv, hv, c1, c1, dact, dact, w)


def _gnorm_fwd(y, z, w, name):
    t, di = y.shape
    gw = di // N_GROUPS
    tm = min(512, t)

    def body(y_ref, z_ref, w_ref, o_ref):
        wv = w_ref[...]
        for rows in _strips(tm):
            zv = z_ref[rows, :].astype(F32)
            yz = y_ref[rows, :].astype(F32) * zv * _sigmoid(zv)
            r = lax.rsqrt(jnp.mean(yz * yz, axis=-1, keepdims=True) + EPS)
            o_ref[rows, :] = (yz * r * wv).astype(o_ref.dtype)

    blk = pl.BlockSpec((tm, gw), lambda j, i: (i, j))
    return _pc(body, name=name, grid=(N_GROUPS, t // tm),
               in_specs=[blk, blk, pl.BlockSpec((1, gw), lambda j, i: (0, j))],
               out_specs=blk, out_shape=_sds((t, di), BF16), compiler_params=_params())(y, z, w)


def _gnorm_bwd(y, z, dyn, w, name):
    t, di = y.shape
    gw = di // N_GROUPS
    tm = min(512, t)

    def body(y_ref, z_ref, g_ref, w_ref, dy_ref, dz_ref, dw_ref):
        @pl.when(pl.program_id(1) == 0)
        def _():
            dw_ref[...] = jnp.zeros_like(dw_ref)
        wv = w_ref[...]
        acc = jnp.zeros((8, gw), F32)
        for rows in _strips(tm):
            yv, zv, g = y_ref[rows, :].astype(F32), z_ref[rows, :].astype(F32), g_ref[rows, :].astype(F32)
            s = _sigmoid(zv)
            sz = zv * s
            yz = yv * sz
            r = lax.rsqrt(jnp.mean(yz * yz, axis=-1, keepdims=True) + EPS)
            n = yz * r
            dn = g * wv
            acc = acc + _fold8(g * n)
            dyz = r * (dn - n * jnp.mean(dn * n, axis=-1, keepdims=True))
            dy_ref[rows, :] = (dyz * sz).astype(BF16)
            dz_ref[rows, :] = (dyz * yv * s * (1.0 + zv * (1.0 - s))).astype(BF16)
        dw_ref[0:1, :] += _colsum(acc)

    blk = pl.BlockSpec((tm, gw), lambda j, i: (i, j))
    return _pc(body, name=name, grid=(N_GROUPS, t // tm),
               in_specs=[blk, blk, blk, pl.BlockSpec((1, gw), lambda j, i: (0, j))],
               out_specs=[blk, blk, pl.BlockSpec((8, gw), lambda j, i: (0, j))],
               out_shape=[_sds((t, di), BF16), _sds((t, di), BF16), _sds((8, di), F32)],
               compiler_params=_params())(y, z, dyn, w)


def _merge_fwd(gates, ya, ys, d, name):
    t = ya.shape[0]
    tm, tc = min(512, t), _pick(d, (512, 256, 128))
    nd = d // tc

    def body(ga_ref, gs_ref, ya_ref, ys_ref, o_ref):
        for rows in _strips(tm):
            o_ref[rows, :] = (_sigmoid(ga_ref[rows, :].astype(F32)) * ya_ref[rows, :].astype(F32)
                              + _sigmoid(gs_ref[rows, :].astype(F32)) * ys_ref[rows, :].astype(F32)
                              ).astype(o_ref.dtype)

    blk = pl.BlockSpec((tm, tc), lambda j, i: (i, j))
    return _pc(body, name=name, grid=(nd, t // tm),
               in_specs=[blk, pl.BlockSpec((tm, tc), lambda j, i: (i, nd + j)), blk, blk],
               out_specs=blk, out_shape=_sds((t, d), BF16), compiler_params=_params())(gates, gates, ya, ys)


def _merge_bwd(dm, gates, ya, ys, d, name):
    t = ya.shape[0]
    tm, tc = min(512, t), _pick(d, (512, 256, 128))
    nd = d // tc

    def body(dm_ref, ga_ref, gs_ref, ya_ref, ys_ref, dya_ref, dys_ref, dga_ref, dgs_ref):
        for rows in _strips(tm):
            g = dm_ref[rows, :].astype(F32)
            sa, ss = _sigmoid(ga_ref[rows, :].astype(F32)), _sigmoid(gs_ref[rows, :].astype(F32))
            dya_ref[rows, :] = (g * sa).astype(BF16)
            dys_ref[rows, :] = (g * ss).astype(BF16)
            dga_ref[rows, :] = (g * ya_ref[rows, :].astype(F32) * sa * (1.0 - sa)).astype(BF16)
            dgs_ref[rows, :] = (g * ys_ref[rows, :].astype(F32) * ss * (1.0 - ss)).astype(BF16)

    blk = pl.BlockSpec((tm, tc), lambda j, i: (i, j))
    return _pc(body, name=name, grid=(nd, t // tm),
               in_specs=[blk, blk, pl.BlockSpec((tm, tc), lambda j, i: (i, nd + j)), blk, blk],
               out_specs=[blk] * 4, out_shape=[_sds((t, d), BF16)] * 4,
               compiler_params=_params())(dm, gates, gates, ya, ys)


def _ssd_chunk_terms(dtr, dtb, alog):
    xx = dtr + dtb
    dt = jnp.maximum(xx, 0.0) + jnp.log(1.0 + jnp.exp(-jnp.abs(xx)))
    a = -jnp.exp(alog)
    li = lax.broadcasted_iota(jnp.int32, (CHUNK, CHUNK), 0)
    si = lax.broadcasted_iota(jnp.int32, (CHUNK, CHUNK), 1)
    causal = li >= si
    acum = _dot(causal.astype(F32), dt * a, 1, 0, HIGHEST)
    return xx, dt, a, acum, acum.T, causal


def _split3(x):
    hi = x.astype(BF16)
    r = x - hi.astype(F32)
    mid = r.astype(BF16)
    lo = (r - mid.astype(F32)).astype(BF16)
    return hi, mid, lo


def _expand(v, e):
    hi, mid, lo = _split3(v)
    return _dot(hi, e, 1, 0) + _dot(mid, e, 1, 0) + _dot(lo, e, 1, 0)


def _segsum(s, e):
    hi, mid, lo = _split3(s)
    return _dot(hi, e, 1, 1) + _dot(mid, e, 1, 1) + _dot(lo, e, 1, 1)


def _head_maps(di):
    nh = di // HEAD_DIM
    h = jnp.arange(DT_LANES)[:, None]
    e64 = (jnp.arange(di)[None, :] // HEAD_DIM == h).astype(BF16)
    e128 = (jnp.arange(nh * CHUNK)[None, :] // CHUNK == h).astype(BF16)
    return e64, e128


def _pair_blockdiag(p, left):
    zero = jnp.zeros_like(p)
    return jnp.concatenate([jnp.where(left, p, zero), jnp.where(left, zero, p)], axis=0)


def _ssd_fwd(xc, dtr, dtb, alog, dskx, di, name):
    t = xc.shape[0]
    dx = xc.shape[1]
    nc = t // CHUNK
    nh = di // HEAD_DIM
    hpg = nh // N_GROUPS
    gw = hpg * HEAD_DIM
    boff, coff = di, di + N_GROUPS * D_STATE
    e64, e128 = _head_maps(di)

    def body(xc_ref, dtr_ref, dtb_ref, alog_ref, dsk_ref, e64_ref, e128_ref, y_ref, st_ref, state):
        @pl.when(pl.program_id(0) == 0)
        def _():
            state[...] = jnp.zeros_like(state)
        _, dt, _, acum, acum_t, causal = _ssd_chunk_terms(dtr_ref[...], dtb_ref[...], alog_ref[...])
        last = acum[CHUNK - 1:CHUNK, :]
        e64v = e64_ref[...]
        dtx = _expand(dt, e64v)
        eax = _expand(jnp.exp(acum), e64v)
        dex = _expand(dt * jnp.exp(last - acum), e64v)
        acx = _expand(acum, e128_ref[...])
        st_ref[0] = state[...]
        left = lax.broadcasted_iota(jnp.int32, (CHUNK, 2 * HEAD_DIM), 1) < HEAD_DIM
        for g in range(N_GROUPS):
            gs = slice(g * gw, (g + 1) * gw)
            bg = xc_ref[:, boff + g * D_STATE:boff + (g + 1) * D_STATE]
            cg = xc_ref[:, coff + g * D_STATE:coff + (g + 1) * D_STATE]
            gm = _dot(cg, bg, 1, 1)
            xg = xc_ref[:, gs].astype(F32)
            xdb = (xg * dtx[:, gs]).astype(BF16)
            sin = state[:, gs]
            yo = _dot(cg, sin.astype(BF16), 1, 0) * eax[:, gs]
            for jp in range(hpg // 2):
                h0 = g * hpg + 2 * jp
                ps = slice(jp * 2 * HEAD_DIM, (jp + 1) * 2 * HEAD_DIM)
                ms = []
                for hh in (h0, h0 + 1):
                    seg = acx[:, hh * CHUNK:(hh + 1) * CHUNK] - acum_t[hh:hh + 1, :]
                    ms.append((gm * jnp.exp(jnp.where(causal, seg, -1e30))).astype(BF16))
                yd = _dot(jnp.concatenate(ms, axis=1), _pair_blockdiag(xdb[:, ps], left), 1, 0)
                col = slice(g * gw + jp * 2 * HEAD_DIM, g * gw + (jp + 1) * 2 * HEAD_DIM)
                y_ref[:, col] = (yd + yo[:, ps] + dsk_ref[:, col] * xg[:, ps]).astype(y_ref.dtype)
            xe = (xg * dex[:, gs]).astype(BF16)
            state[:, gs] = eax[CHUNK - 1:CHUNK, gs] * sin + _dot(bg, xe, 0, 0)

    small = pl.BlockSpec((1, DT_LANES), lambda c: (0, 0))
    whole = lambda a: pl.BlockSpec(a.shape, lambda c: (0, 0))
    return _pc(body, name=name, grid=(nc,),
               in_specs=[pl.BlockSpec((CHUNK, dx), lambda c: (c, 0)),
                         pl.BlockSpec((CHUNK, DT_LANES), lambda c: (c, 0)), small, small,
                         whole(dskx), whole(e64), whole(e128)],
               out_specs=[pl.BlockSpec((CHUNK, di), lambda c: (c, 0)),
                          pl.BlockSpec((1, D_STATE, di), lambda c: (c, 0, 0))],
               out_shape=[_sds((t, di), BF16), _sds((nc, D_STATE, di), F32)],
               scratch_shapes=[pltpu.VMEM((D_STATE, di), F32)],
               compiler_params=_params())(xc, dtr, dtb, alog, dskx, e64, e128)


def _ssd_bwd(xc, dtr, dy, states, dtb, alog, dskx, di, name):
    t = xc.shape[0]
    dx = xc.shape[1]
    nc = t // CHUNK
    nh = di // HEAD_DIM
    hpg = nh // N_GROUPS
    gw = hpg * HEAD_DIM
    boff, coff = di, di + N_GROUPS * D_STATE
    e64, e128 = _head_maps(di)

    def body(xc_ref, dtr_ref, dy_ref, st_ref, dtb_ref, alog_ref, dsk_ref, e64_ref, e128_ref,
             dxc_ref, ddtr_ref, sm_ref, dstate, darow):
        @pl.when(pl.program_id(0) == 0)
        def _():
            dstate[...] = jnp.zeros_like(dstate)
            sm_ref[...] = jnp.zeros_like(sm_ref)
        darow[...] = jnp.zeros_like(darow)
        xx, dt, a, acum, acum_t, causal = _ssd_chunk_terms(dtr_ref[...], dtb_ref[...], alog_ref[...])
        last = acum[CHUNK - 1:CHUNK, :]
        e64v = e64_ref[...]
        dtx = _expand(dt, e64v)
        eax = _expand(jnp.exp(acum), e64v)
        eex = _expand(jnp.exp(last - acum), e64v)
        acx = _expand(acum, e128_ref[...])
        left = lax.broadcasted_iota(jnp.int32, (CHUNK, 2 * HEAD_DIM), 1) < HEAD_DIM
        lane = lax.broadcasted_iota(jnp.int32, (CHUNK, DT_LANES), 1)
        sub8 = lax.broadcasted_iota(jnp.int32, (8, gw), 0)
        da_col = jnp.zeros((CHUNK, DT_LANES), F32)
        ddt_col = jnp.zeros((CHUNK, DT_LANES), F32)
        rows = jnp.zeros((8, DT_LANES), F32)
        for g in range(N_GROUPS):
            gs = slice(g * gw, (g + 1) * gw)
            bg = xc_ref[:, boff + g * D_STATE:boff + (g + 1) * D_STATE]
            cg = xc_ref[:, coff + g * D_STATE:coff + (g + 1) * D_STATE]
            gm = _dot(cg, bg, 1, 1)
            e64g = e64v[:, gs]
            xg = xc_ref[:, gs].astype(F32)
            dtg, eag, eeg = dtx[:, gs], eax[:, gs], eex[:, gs]
            xd = xg * dtg
            xdb = xd.astype(BF16)
            dyb = dy_ref[:, gs]
            dyf = dyb.astype(F32)
            sin = st_ref[0, :, gs]
            sinb = sin.astype(BF16)
            ds = dstate[:, gs]
            dsb = ds.astype(BF16)
            bds = _dot(bg, dsb, 1, 0)
            dyeb = (dyf * eag).astype(BF16)
            dcg = _dot(dyeb, sinb, 1, 1)
            dstate[:, gs] = eag[CHUNK - 1:CHUNK, :] * ds + _dot(cg, dyeb, 0, 0)
            yo = _dot(cg, sinb, 1, 0) * eag
            xe = xd * eeg
            dbg = _dot(xe.astype(BF16), dsb, 1, 1)
            wterm = bds * xe
            da_col = da_col + _segsum(dyf * yo - wterm, e64g)
            dg = jnp.zeros((CHUNK, CHUNK), F32)
            dxd_parts = []
            for jp in range(hpg // 2):
                h0 = g * hpg + 2 * jp
                ps = slice(jp * 2 * HEAD_DIM, (jp + 1) * 2 * HEAD_DIM)
                lms, mfs = [], []
                for hh in (h0, h0 + 1):
                    seg = acx[:, hh * CHUNK:(hh + 1) * CHUNK] - acum_t[hh:hh + 1, :]
                    lm = jnp.exp(jnp.where(causal, seg, -1e30))
                    lms.append(lm)
                    mfs.append(gm * lm)
                mstack = jnp.concatenate([m.astype(BF16) for m in mfs], axis=0)
                dyp = dyb[:, ps]
                dxd_parts.append(_dot(mstack, _pair_blockdiag(dyp, left), 0, 0))
                dm2 = _dot(dyp, _pair_blockdiag(xdb[:, ps], left), 1, 1)
                for k, hh in enumerate((h0, h0 + 1)):
                    dm = dm2[:, k * CHUNK:(k + 1) * CHUNK]
                    dg = dg + dm * lms[k]
                    q = dm * mfs[k]
                    da_col = da_col + jnp.where(lane == hh, jnp.sum(q, axis=1, keepdims=True), 0.0)
                    darow[hh:hh + 1, :] = -jnp.sum(q, axis=0, keepdims=True)
            dxd = jnp.concatenate(dxd_parts, axis=1) + bds * eeg
            ddt_col = ddt_col + _segsum(dxd * xg, e64g)
            rsum = (jnp.where(sub8 == 0, jnp.sum(wterm, axis=0, keepdims=True), 0.0)
                    + jnp.where(sub8 == 1, jnp.sum(ds * sin, axis=0, keepdims=True), 0.0)
                    + jnp.where(sub8 == 2, jnp.sum(dyf * xg, axis=0, keepdims=True), 0.0))
            rows = rows + _segsum(rsum, e64g)
            dxc_ref[:, gs] = (dxd * dtg + dsk_ref[:, gs] * dyf).astype(dxc_ref.dtype)
            dgb = dg.astype(BF16)
            dxc_ref[:, boff + g * D_STATE:boff + (g + 1) * D_STATE] = (
                dbg + _dot(dgb, cg, 0, 0)).astype(dxc_ref.dtype)
            dxc_ref[:, coff + g * D_STATE:coff + (g + 1) * D_STATE] = (
                dcg + _dot(dgb, bg, 1, 0)).astype(dxc_ref.dtype)
        at_last = rows[0:1, :] + jnp.exp(last) * rows[1:2, :]
        is_last = lax.broadcasted_iota(jnp.int32, (CHUNK, DT_LANES), 0) == CHUNK - 1
        da = da_col + jnp.where(is_last, at_last, 0.0) + darow[...].T
        li = lax.broadcasted_iota(jnp.int32, (CHUNK, CHUNK), 0)
        si = lax.broadcasted_iota(jnp.int32, (CHUNK, CHUNK), 1)
        dla = _dot((si >= li).astype(F32), da, 1, 0, HIGHEST)
        ddtr = (ddt_col + dla * a) * _sigmoid(xx)
        ddtr_ref[...] = ddtr
        sm_ref[0:1, :] += jnp.sum(ddtr, axis=0, keepdims=True)
        sm_ref[1:2, :] += jnp.sum(dla * dt, axis=0, keepdims=True) * a
        sm_ref[2:3, :] += rows[2:3, :]

    small = pl.BlockSpec((1, DT_LANES), lambda c: (0, 0))
    whole = lambda a: pl.BlockSpec(a.shape, lambda c: (0, 0))
    rev = lambda c: (nc - 1 - c, 0)
    return _pc(body, name=name, grid=(nc,),
               in_specs=[pl.BlockSpec((CHUNK, dx), rev), pl.BlockSpec((CHUNK, DT_LANES), rev),
                         pl.BlockSpec((CHUNK, di), rev),
                         pl.BlockSpec((1, D_STATE, di), lambda c: (nc - 1 - c, 0, 0)), small, small,
                         whole(dskx), whole(e64), whole(e128)],
               out_specs=[pl.BlockSpec((CHUNK, dx), rev), pl.BlockSpec((CHUNK, DT_LANES), rev),
                          pl.BlockSpec((8, DT_LANES), lambda c: (0, 0))],
               out_shape=[_sds((t, dx), BF16), _sds((t, DT_LANES), F32), _sds((8, DT_LANES), F32)],
               scratch_shapes=[pltpu.VMEM((D_STATE, di), F32), pltpu.VMEM((DT_LANES, CHUNK), F32)],
               compiler_params=_params())(xc, dtr, dy, states, dtb, alog, dskx, e64, e128)


def _adamw(parts, w, m, v, name):
    npart, rows, width = parts.shape
    tr, tw = (_pick(rows, (64, 32, 16, 8)), width) if rows % 8 == 0 else (rows, 128)
    c1 = 1.0 - ADAM_B1 ** ADAM_STEP
    c2 = 1.0 - ADAM_B2 ** ADAM_STEP

    def body(p_ref, w_ref, m_ref, v_ref, g_ref, d_ref, nm_ref, nv_ref):
        for rows in (_strips(tr, 8) if tr % 8 == 0 else [slice(0, tr)]):
            g = p_ref[0, rows, :].astype(F32)
            for p in range(1, npart):
                g = g + p_ref[p, rows, :].astype(F32)
            nm = ADAM_B1 * m_ref[rows, :] + (1.0 - ADAM_B1) * g
            nv = ADAM_B2 * v_ref[rows, :] + (1.0 - ADAM_B2) * (g * g)
            g_ref[rows, :] = g
            nm_ref[rows, :] = nm
            nv_ref[rows, :] = nv
            d_ref[rows, :] = -ADAM_LR * ((nm / c1) / (jnp.sqrt(nv / c2) + ADAM_EPS) + ADAM_WD * w_ref[rows, :])

    blk = pl.BlockSpec((tr, tw), lambda i, j: (i, j))
    return _pc(body, name=name, grid=(rows // tr, width // tw),
               in_specs=[pl.BlockSpec((npart, tr, tw), lambda i, j: (0, i, j)), blk, blk, blk],
               out_specs=[blk] * 4, out_shape=[_sds((rows, width), F32)] * 4,
               compiler_params=_params())(parts, w, m, v)


def _sum_parts(parts, name, tile=None):
    npart, rows, width = parts.shape
    tile = rows if tile is None else tile

    def body(p_ref, o_ref):
        for rows_ in _strips(tile, 8 if parts.dtype == F32 else STRIP):
            g = p_ref[0, rows_, :].astype(F32)
            for p in range(1, npart):
                g = g + p_ref[p, rows_, :].astype(F32)
            o_ref[rows_, :] = g

    return _pc(body, name=name, grid=(rows // tile,),
               in_specs=[pl.BlockSpec((npart, tile, width), lambda i: (0, i, 0))],
               out_specs=pl.BlockSpec((tile, width), lambda i: (i, 0)),
               out_shape=_sds((rows, width), F32), compiler_params=_params())(parts)


def _flip(k):
    x, y, c = lax.axis_index("x"), lax.axis_index("y"), lax.axis_index("c")
    px = 1 - x if k & 4 else x
    py = 1 - y if k & 2 else y
    pc = 1 - c if k & 1 else c
    return (px, py, pc), 4 * px + 2 * py + pc


DIRECT = tuple((k, 0) for k in range(1, N_DEV))
TO_CHIPS = ((1, 0), (2, 0), (4, 0), (6, 0))
TO_SIBLING = ((1, 2), (1, 4), (1, 6))


def _copies(arrays, lands, send_sems, recv_sems, scatter, moves):
    _, me = _flip(0)
    outgoing, incoming = [], []
    for i, (kd, kb) in enumerate(moves):
        peer, pidx = _flip(kd)
        _, out_slot = _flip(kb)
        _, in_slot = _flip(kd ^ kb)
        for j, land_ref in enumerate(lands):
            if kb:
                src = land_ref.at[out_slot]
            else:
                src = arrays[j].at[pidx] if scatter[j] else arrays[j]
            sem = len(lands) * i + j
            for dst, bucket in ((land_ref.at[out_slot], outgoing), (land_ref.at[in_slot], incoming)):
                bucket.append(pltpu.make_async_remote_copy(
                    src_ref=src, dst_ref=dst, send_sem=send_sems.at[sem], recv_sem=recv_sems.at[sem],
                    device_id=peer, device_id_type=MESH))
    return outgoing, incoming


HBM_SPEC = pl.BlockSpec(memory_space=pltpu.HBM)
SEM_SPEC = pl.BlockSpec(memory_space=pltpu.SEMAPHORE)
ANY_SPEC = pl.BlockSpec(memory_space=pl.ANY)
EFFECT = pltpu.SideEffectType.DATAFLOW_SIDE_EFFECTING


def _landing_zones(arrays, scatter):
    _, me = _flip(0)
    lands = []
    for a, sc in zip(arrays, scatter):
        own = lax.dynamic_index_in_dim(a, me, 0, keepdims=True) if sc else a[None]
        shape = a.shape if sc else (N_DEV,) + a.shape
        lands.append(lax.dynamic_update_slice(lax.empty(shape, a.dtype), own, (me,) + (0,) * (len(shape) - 1)))
    return lands


def _xchg_start(arrays, scatter, after, name, moves=DIRECT, lands=None):
    if lands is None:
        lands = _landing_zones(arrays, scatter)
    na, nl = len(arrays), len(lands)

    def body(*refs):
        ins, outs = refs[:na + nl], refs[na + nl + 1:]
        outgoing, _ = _copies(ins[:na], ins[na:], outs[0], outs[1], scatter, moves)
        for cp in outgoing:
            cp.start()
        outs[-1][...] = jnp.zeros_like(outs[-1])

    nsem = nl * len(moves)
    operands = [pltpu.with_memory_space_constraint(a, pltpu.HBM) for a in list(arrays) + list(lands)]
    out = _pc(body, name=name,
              out_shape=(pltpu.SemaphoreType.DMA((nsem,)), pltpu.SemaphoreType.DMA((nsem,)),
                         *[pltpu.HBM(a.shape, a.dtype) for a in operands], _sds((8, 128), F32)),
              in_specs=[HBM_SPEC] * (na + nl) + [ANY_SPEC],
              out_specs=(SEM_SPEC, SEM_SPEC, *[HBM_SPEC] * (na + nl), pl.BlockSpec(memory_space=pltpu.VMEM)),
              input_output_aliases={i: 2 + i for i in range(na + nl)},
              compiler_params=pltpu.CompilerParams(has_side_effects=EFFECT))(*operands, after)
    return dict(sems=out[:2], thru=out[2:2 + na + nl], token=out[-1], scatter=scatter, na=na, moves=moves)


def _xchg_wait(handle, after, name):
    na, scatter, moves, thru = handle["na"], handle["scatter"], handle["moves"], handle["thru"]
    n = len(thru)

    def body(*refs):
        ins = refs[:n]
        outgoing, incoming = _copies(ins[:na], ins[na:], refs[n], refs[n + 1], scatter, moves)
        for cp in outgoing:
            cp.wait_send()
        for cp in incoming:
            cp.wait_recv()

    out = _pc(body, name=name, out_shape=tuple(pltpu.HBM(a.shape, a.dtype) for a in thru),
              in_specs=[HBM_SPEC] * n + [SEM_SPEC, SEM_SPEC, ANY_SPEC], out_specs=tuple([HBM_SPEC] * n),
              input_output_aliases={i: i for i in range(n)},
              compiler_params=pltpu.CompilerParams(has_side_effects=EFFECT))(*thru, *handle["sems"], after)
    return out[na:]


def _pack(arrs, width, row_mult):
    flat = jnp.concatenate([a.reshape(-1) for a in arrs])
    n = flat.shape[0]
    rows = -(-n // (width * row_mult)) * row_mult
    return jnp.pad(flat, (0, rows * width - n)).reshape(rows, width)


def _unpack(packed, shapes, lead=None):
    out, off = [], 0
    flat = packed.reshape(-1) if lead is None else packed.reshape(lead, -1)
    for s in shapes:
        n = math.prod(s)
        if lead is None:
            out.append(flat[off:off + n].reshape(s))
        else:
            out.append(flat[:, off:off + n].reshape((lead,) + tuple(s)))
        off += n
    return out


def _blocks_to_cols(blocks):
    nb, rows, n = blocks.shape
    return blocks.transpose(1, 0, 2).reshape(rows, nb * n)


def _pad_rows(a, rows):
    return jnp.pad(a, ((0, rows - a.shape[0]), (0, 0)))


def _pad_lanes(a, lanes):
    return jnp.pad(a, ((0, 0), (0, lanes - a.shape[1])))


REST = ("w_a_out", "w_s_out", "w_o", "w_up", "w_down")
TRANSPOSED = ("w_up", "w_in")
ROW_ALIGN = 32
CONVS = ("conv_a_w", "ssd_conv_w", "ffn_conv_w")
REPL = ("norm_mix_w", "ssd_conv_b", "dt_bias", "a_log", "d_skip", "ssd_norm_w", "norm_ffn_w", "ffn_conv_b",
        "final_norm_w")
ORDER = ("norm_mix_w", "w_in", "conv_a_w", "w_a_out", "ssd_conv_w", "ssd_conv_b", "dt_bias", "a_log", "d_skip",
         "ssd_norm_w", "w_s_out", "w_o", "norm_ffn_w", "w_up", "ffn_conv_w", "ffn_conv_b", "w_down", "final_norm_w")


def _as_rows(name, block):
    return block[0].T if name in TRANSPOSED else block[0]


def kernel(x, norm_mix_w, w_in, conv_a_w, w_a_out, ssd_conv_w, ssd_conv_b, dt_bias, a_log, d_skip, ssd_norm_w, w_s_out, w_o, norm_ffn_w, w_up, ffn_conv_w, ffn_conv_b, w_down, final_norm_w, loss_target, m_norm_mix_w, m_w_in, m_conv_a_w, m_w_a_out, m_ssd_conv_w, m_ssd_conv_b, m_dt_bias, m_a_log, m_d_skip, m_ssd_norm_w, m_w_s_out, m_w_o, m_norm_ffn_w, m_w_up, m_ffn_conv_w, m_ffn_conv_b, m_w_down, m_final_norm_w, v_norm_mix_w, v_w_in, v_conv_a_w, v_w_a_out, v_ssd_conv_w, v_ssd_conv_b, v_dt_bias, v_a_log, v_d_skip, v_ssd_norm_w, v_w_s_out, v_w_o, v_norm_ffn_w, v_w_up, v_ffn_conv_w, v_ffn_conv_b, v_w_down, v_final_norm_w):
    wts = dict(norm_mix_w=norm_mix_w, w_in=w_in, conv_a_w=conv_a_w, w_a_out=w_a_out, ssd_conv_w=ssd_conv_w,
               ssd_conv_b=ssd_conv_b, dt_bias=dt_bias, a_log=a_log, d_skip=d_skip, ssd_norm_w=ssd_norm_w,
               w_s_out=w_s_out, w_o=w_o, norm_ffn_w=norm_ffn_w, w_up=w_up, ffn_conv_w=ffn_conv_w,
               ffn_conv_b=ffn_conv_b, w_down=w_down, final_norm_w=final_norm_w)
    mom1 = dict(norm_mix_w=m_norm_mix_w, w_in=m_w_in, conv_a_w=m_conv_a_w, w_a_out=m_w_a_out,
                ssd_conv_w=m_ssd_conv_w, ssd_conv_b=m_ssd_conv_b, dt_bias=m_dt_bias, a_log=m_a_log, d_skip=m_d_skip,
                ssd_norm_w=m_ssd_norm_w, w_s_out=m_w_s_out, w_o=m_w_o, norm_ffn_w=m_norm_ffn_w, w_up=m_w_up,
                ffn_conv_w=m_ffn_conv_w, ffn_conv_b=m_ffn_conv_b, w_down=m_w_down, final_norm_w=m_final_norm_w)
    mom2 = dict(norm_mix_w=v_norm_mix_w, w_in=v_w_in, conv_a_w=v_conv_a_w, w_a_out=v_w_a_out,
                ssd_conv_w=v_ssd_conv_w, ssd_conv_b=v_ssd_conv_b, dt_bias=v_dt_bias, a_log=v_a_log, d_skip=v_d_skip,
                ssd_norm_w=v_ssd_norm_w, w_s_out=v_w_s_out, w_o=v_w_o, norm_ffn_w=v_norm_ffn_w, w_up=v_w_up,
                ffn_conv_w=v_ffn_conv_w, ffn_conv_b=v_ffn_conv_b, w_down=v_w_down, final_norm_w=v_final_norm_w)

    t, d = x.shape[1], x.shape[2]
    di = 2 * d
    nh = di // HEAD_DIM
    dxw = di + 2 * N_GROUPS * D_STATE
    f = w_down.shape[1] * N_DEV
    n_in = w_in.shape[2] * N_DEV
    me = 4 * lax.axis_index("x") + 2 * lax.axis_index("y") + lax.axis_index("c")

    rest_rows = [_as_rows(k, wts[k]) for k in REST]
    nrows = [a.shape[0] for a in rest_rows]
    offs = [sum(nrows[:i]) for i in range(len(REST))]
    assert all(o % ROW_ALIGN == 0 for o in offs)
    r_rest = -(-sum(nrows) // ROW_ALIGN) * ROW_ALIGN
    n_blk = w_in.shape[2]
    r_in = -(-n_blk // ROW_ALIGN) * ROW_ALIGN
    in_local = _pad_rows(w_in[0].T, r_in).astype(BF16)
    rest_local = _pad_rows(jnp.concatenate(rest_rows, axis=0), r_rest).astype(BF16)
    conv_shapes = [wts[k].shape[1:] for k in CONVS]
    conv_local = _pack([wts[k] for k in CONVS], d, 8)
    x2, tgt = x[0], loss_target[0]
    h_in = _xchg_start([in_local, conv_local], [False, False], x2, "gather_in_start", moves=TO_CHIPS)
    u = _rms_fwd(x2, norm_mix_w, "norm_mix")
    part = _xchg_wait(h_in, u, "gather_in_wait")
    h_fwd = _xchg_start([], [False, False], part[0], "gather_in_forward_start", moves=TO_SIBLING, lands=part)
    in_all, conv_all = _xchg_wait(h_fwd, part[1], "gather_in_forward_wait")
    win_t = in_all[:, :n_blk].reshape(n_in, d)
    h_rest = _xchg_start([rest_local], [False], in_all, "gather_rest_start")
    c_a, c_s, c_f = _unpack(conv_all, conv_shapes, N_DEV)
    caw, scw, fcw = _blocks_to_cols(c_a), _blocks_to_cols(c_s), _blocks_to_cols(c_f)

    o_z, o_x, o_dt = 5 * d, 7 * d, 7 * d + dxw
    seg_bounds = [0, d, 2 * d, 3 * d, 4 * d, o_z, o_x, o_dt]
    w_dt = _pad_rows(win_t[o_dt:], DT_LANES)
    dtb, alog = (_pad_lanes(p[...].reshape(1, nh), DT_LANES) for p in (dt_bias, a_log))
    dskx = jnp.repeat(d_skip.reshape(1, nh), HEAD_DIM, axis=1)

    tok = h_rest["token"]
    gates = _mm([(u, win_t[:2 * d])], "nt", BF16, "proj_gates", after=tok)
    pa = _mm([(u, win_t[2 * d:o_z])], "nt", BF16, "proj_a", after=tok)
    z = _mm([(u, win_t[o_z:o_x])], "nt", BF16, "proj_z", after=tok)
    xbc = _mm([(u, win_t[o_x:o_dt])], "nt", BF16, "proj_xbc", after=tok)
    dtr = _mm([(u, w_dt)], "nt", F32, "proj_dt", after=tok)
    ya_in, q_a = _conv_a_fwd(pa, caw, d, "conv_a")
    xc, pre_s = _conv_s_fwd(xbc, scw, ssd_conv_b, "conv_s")
    y, states = _ssd_fwd(xc, dtr, dtb, alog, dskx, di, "ssd")
    yn = _gnorm_fwd(y, z, ssd_norm_w, "gnorm")
    (rest_all,) = _xchg_wait(h_rest, yn, "gather_rest_wait")
    full = {k: rest_all[:, o:o + n].reshape(N_DEV * n, d) for k, o, n in zip(REST, offs, nrows)}
    waout, wsout, wo, wup_t, wdown = (full[k] for k in REST)
    y_a = _mm([(ya_in, waout)], "nn", BF16, "a_out")
    y_s = _mm([(yn, wsout)], "nn", BF16, "s_out")
    merged = _merge_fwd(gates, y_a, y_s, d, "merge")
    mo = _mm([(merged, wo)], "nn", F32, "o_proj")
    h1, v = _resnorm_fwd(x2, mo, norm_ffn_w, "norm_ffn")
    hv = _mm([(v, wup_t)], "nt", BF16, "up_proj")
    act, c1 = _ffn_fwd(hv, fcw, ffn_conv_b, f, "ffn_act")
    dd = _mm([(act, wdown)], "nn", F32, "down_proj")
    loss11, dh2, dh2b, g_fnw = _final(h1, dd, tgt, final_norm_w.reshape(1, d), "final")

    dact = _mm([(dh2b, wdown)], "nt", BF16, "d_act")
    gw_down = _mm_tn(act, dh2b, "gw_down")
    dh1f, dh3, g_ffn = _ffn_bwd(hv, c1, dact, fcw, f, "ffn_act_bwd")
    dv = _mm([(dh1f, wup_t[:f]), (dh3, wup_t[f:])], "nn", F32, "d_v")
    gw_up_t = jnp.concatenate([_mm_tn(dh1f, v, "gw_up1"), _mm_tn(dh3, v, "gw_up3")], axis=0)
    dh1, dh1b, g_nfw = _rms_bwd(h1, dv, norm_ffn_w, dh2, "norm_ffn_bwd")
    dmerged = _mm([(dh1b, wo)], "nt", BF16, "d_merged")
    gw_o = _mm_tn(merged, dh1b, "gw_o")
    dya, dys, dga, dgs = _merge_bwd(dmerged, gates, y_a, y_s, d, "merge_bwd")
    dyain = _mm([(dya, waout)], "nt", BF16, "d_ya_in")
    gw_aout = _mm_tn(ya_in, dya, "gw_a_out")
    db, dc, dvv, g_caw = _conv_a_bwd(pa, q_a, dyain, caw, d, "conv_a_bwd")
    dyn = _mm([(dys, wsout)], "nt", BF16, "d_yn")
    gw_sout = _mm_tn(yn, dys, "gw_s_out")
    grads_rest = dict(w_a_out=gw_aout, w_s_out=gw_sout, w_o=gw_o, w_up=gw_up_t, w_down=gw_down)
    rest_parts = jnp.concatenate([grads_rest[k].reshape(N_DEV, n, d) for k, n in zip(REST, nrows)], axis=1)
    rest_parts = jnp.pad(rest_parts, ((0, 0), (0, r_rest - sum(nrows)), (0, 0))).astype(BF16)
    h_grest = _xchg_start([rest_parts], [True], rest_parts, "scatter_rest_start")
    dy, dz, g_snw = _gnorm_bwd(y, z, dyn, ssd_norm_w, "gnorm_bwd")
    dtb_after = dtb + h_grest["token"][0:1, 0:1]
    dxc, ddtr, g_ssd = _ssd_bwd(xc, dtr, dy, states, dtb_after, alog, dskx, di, "ssd_bwd")
    dxbc, g_scw = _conv_s_bwd(xbc, pre_s, dxc, scw, "conv_s_bwd")
    dsegs = [dga, dgs, db, dc, dvv, dz, dxbc]
    pairs = [(s, win_t[a:b]) for s, a, b in zip(dsegs, seg_bounds[:-1], seg_bounds[1:])]
    pairs.append((ddtr.astype(BF16), w_dt))
    gw_in_t = jnp.concatenate([_mm_tn(s, u, "gw_in%d" % i) for i, (s, _) in enumerate(pairs)], axis=0)[:n_in]
    in_parts = jnp.pad(gw_in_t.reshape(N_DEV, n_blk, d), ((0, 0), (0, r_in - n_blk), (0, 0))).astype(BF16)
    h_gin = _xchg_start([in_parts], [True], in_parts, "scatter_in_start")
    du = _mm(pairs, "nn", F32, "d_u", tm=512, tn=512, after=h_gin["token"])
    dx, _, g_nmw = _rms_bwd(x2, du, norm_mix_w, dh1, "norm_mix_bwd")

    small_grads = dict(norm_mix_w=g_nmw[0], ssd_conv_b=g_scw[4], dt_bias=g_ssd[0, :nh], a_log=g_ssd[1, :nh],
                       d_skip=g_ssd[2, :nh], ssd_norm_w=g_snw[0], norm_ffn_w=g_nfw[0], ffn_conv_b=g_ffn[3],
                       final_norm_w=g_fnw[0], conv_a_w=g_caw[:3], ssd_conv_w=g_scw[:4], ffn_conv_w=g_ffn[:3])
    small_names = REPL + CONVS
    small_parts = _pack([small_grads[k] for k in small_names], d, 8)
    h_small = _xchg_start([small_parts], [False], small_parts, "gather_small_start")
    (rest_recv,) = _xchg_wait(h_grest, dx, "scatter_rest_wait")
    (in_recv,) = _xchg_wait(h_gin, rest_recv, "scatter_in_wait")
    (small_all,) = _xchg_wait(h_small, in_recv, "gather_small_wait")
    small_sum = _sum_parts(small_all, "sum_small_grads")
    small_g = dict(zip(small_names, _unpack(small_sum, [small_grads[k].shape for k in small_names])))

    rest_sum = _sum_parts(rest_recv, "sum_rest_grads", tile=ROW_ALIGN)
    in_sum = _sum_parts(in_recv, "sum_in_grads", tile=ROW_ALIGN)
    res = {}

    def update(k, g):
        outs = _adamw(g[None], *(_as_rows(k, src[k]) for src in (wts, mom1, mom2)), "adamw_" + k)
        for kind, a in zip(("g", "d", "m", "v"), outs):
            res[kind, k] = (a.T if k in TRANSPOSED else a)[None]

    update("w_in", in_sum[:n_blk])
    for k, o, n in zip(REST, offs, nrows):
        update(k, rest_sum[o:o + n])
    local_g = {}
    for k in REPL:
        local_g[k] = small_g[k].reshape(wts[k].shape)
    for k in CONVS:
        n = wts[k].shape[2]
        local_g[k] = lax.dynamic_slice_in_dim(small_g[k], me * n, n, axis=1)[None]
    w_sm, m_sm, v_sm = (_pack([src[k] for k in small_names], d, 8) for src in (wts, mom1, mom2))
    g_sm = _pack([local_g[k] for k in small_names], d, 8)
    outs_sm = _adamw(g_sm[None], w_sm, m_sm, v_sm, "adamw_small")
    for kind, packed in zip(("g", "d", "m", "v"), outs_sm):
        for k, a in zip(small_names, _unpack(packed, [wts[k].shape for k in small_names])):
            res[kind, k] = a

    loss = lax.psum(loss11[0, 0], ("x", "y", "c"))
    return (loss, dx[None], *[res["g", k] for k in ORDER], *[res["d", k] for k in ORDER],
            *[res["m", k] for k in ORDER], *[res["v", k] for k in ORDER])
```

```python
import functools
import math

import jax
import jax.numpy as jnp
from jax import lax
from jax.experimental import pallas as pl
from jax.experimental.pallas import tpu as pltpu

F32 = jnp.float32
BF16 = jnp.bfloat16
EPS = 1e-5
HEAD_DIM = 64
N_GROUPS = 4
D_STATE = 128
CHUNK = 128
DT_LANES = 128
HALO = 16
STRIP = 16
N_DEV = 8
V7X_VMEM_LIMIT = 56 * 1024 * 1024
ADAM_LR, ADAM_B1, ADAM_B2, ADAM_EPS, ADAM_WD, ADAM_STEP = 0.001, 0.9, 0.999, 1e-08, 0.01, 10
HIGHEST = lax.Precision.HIGHEST
MESH = pl.DeviceIdType.MESH


def _pc(body, **kw):
    return pl.pallas_call(body, **kw)


def _params():
    return pltpu.CompilerParams(vmem_limit_bytes=V7X_VMEM_LIMIT)


def _pick(n, cands):
    for c in cands:
        if n % c == 0:
            return c
    return n


def _dot(a, b, ca, cb, prec=None):
    return lax.dot_general(a, b, (((ca,), (cb,)), ((), ())), preferred_element_type=F32, precision=prec)


def _sigmoid(x):
    return 0.5 * jnp.tanh(0.5 * x) + 0.5


def _sds(shape, dtype):
    return jax.ShapeDtypeStruct(shape, dtype)


def _mm(pairs, mode, out_dtype, name, tm=1024, tn=1024, after=None):
    m = pairs[0][0].shape[0]
    n = pairs[0][1].shape[1] if mode == "nn" else pairs[0][1].shape[0]
    tm = min(tm, m)
    tn = _pick(n, (tn, 1408, 512, 256, 128))
    npair = len(pairs)
    cb = 0 if mode == "nn" else 1

    def body(*refs):
        o_ref = refs[-1]
        acc = None
        for p in range(npair):
            part = _dot(refs[2 * p][...], refs[2 * p + 1][...], 1, cb)
            acc = part if acc is None else acc + part
        o_ref[...] = acc.astype(o_ref.dtype)

    in_specs, args = [], []
    for a, b in pairs:
        k = a.shape[1]
        in_specs.append(pl.BlockSpec((tm, k), lambda i, j: (i, 0)))
        if mode == "nn":
            in_specs.append(pl.BlockSpec((k, tn), lambda i, j: (0, j)))
        else:
            in_specs.append(pl.BlockSpec((tn, k), lambda i, j: (j, 0)))
        args += [a, b]
    if after is not None:
        in_specs.append(pl.BlockSpec(memory_space=pl.ANY))
        args.append(after)
    return _pc(body, name=name, grid=(m // tm, n // tn), in_specs=in_specs,
               out_specs=pl.BlockSpec((tm, tn), lambda i, j: (i, j)),
               out_shape=_sds((m, n), out_dtype), compiler_params=_params())(*args)


def _mm_tn(a, b, name, tm=1024):
    m, ka = a.shape
    nb = b.shape[1]
    tm = min(tm, m)
    nm = m // tm
    tk = _pick(ka, (1024, 1408, 512, 256, 128))
    tn = _pick(nb, (1024, 512, 256, 128))

    def body(a_ref, b_ref, o_ref, acc):
        t = pl.program_id(2)

        @pl.when(t == 0)
        def _():
            acc[...] = jnp.zeros_like(acc)
        acc[...] += _dot(a_ref[...], b_ref[...], 0, 0)

        @pl.when(t == nm - 1)
        def _():
            o_ref[...] = acc[...].astype(o_ref.dtype)

    return _pc(body, name=name, grid=(ka // tk, nb // tn, nm),
               in_specs=[pl.BlockSpec((tm, tk), lambda i, j, t: (t, i)),
                         pl.BlockSpec((tm, tn), lambda i, j, t: (t, j))],
               out_specs=pl.BlockSpec((tk, tn), lambda i, j, t: (i, j)),
               out_shape=_sds((ka, nb), BF16), scratch_shapes=[pltpu.VMEM((tk, tn), F32)],
               compiler_params=_params())(a, b)


def _strips(tm, strip=STRIP):
    return [slice(r * strip, (r + 1) * strip) for r in range(tm // strip)]


def _fold8(a):
    out = a[0:8, :]
    for r in range(8, a.shape[0], 8):
        out = out + a[r:r + 8, :]
    return out


def _colsum(a8):
    return jnp.sum(a8, axis=0, keepdims=True)


def _rms_fwd(x, w, name):
    t, d = x.shape
    tm = min(512, t)

    def body(x_ref, w_ref, o_ref):
        wv = w_ref[...]
        for rows in _strips(tm):
            xv = x_ref[rows, :]
            r = lax.rsqrt(jnp.mean(xv * xv, axis=-1, keepdims=True) + EPS)
            o_ref[rows, :] = (xv * r * wv).astype(o_ref.dtype)

    return _pc(body, name=name, grid=(t // tm,),
               in_specs=[pl.BlockSpec((tm, d), lambda i: (i, 0)), pl.BlockSpec((1, d), lambda i: (0, 0))],
               out_specs=pl.BlockSpec((tm, d), lambda i: (i, 0)),
               out_shape=_sds((t, d), BF16), compiler_params=_params())(x, w)


def _resnorm_fwd(x, mo, w, name):
    t, d = x.shape
    tm = min(512, t)

    def body(x_ref, mo_ref, w_ref, h_ref, v_ref):
        wv = w_ref[...]
        for rows in _strips(tm):
            h = x_ref[rows, :] + mo_ref[rows, :]
            r = lax.rsqrt(jnp.mean(h * h, axis=-1, keepdims=True) + EPS)
            h_ref[rows, :] = h
            v_ref[rows, :] = (h * r * wv).astype(v_ref.dtype)

    row = pl.BlockSpec((tm, d), lambda i: (i, 0))
    return _pc(body, name=name, grid=(t // tm,),
               in_specs=[row, row, pl.BlockSpec((1, d), lambda i: (0, 0))],
               out_specs=[row, row], out_shape=[_sds((t, d), F32), _sds((t, d), BF16)],
               compiler_params=_params())(x, mo, w)


def _rms_bwd(h, dy, w, dres, name):
    t, d = h.shape
    tm = min(512, t)

    def body(h_ref, dy_ref, w_ref, dres_ref, dx_ref, dxb_ref, dw_ref):
        @pl.when(pl.program_id(0) == 0)
        def _():
            dw_ref[...] = jnp.zeros_like(dw_ref)
        wv = w_ref[...]
        acc = jnp.zeros((8, d), F32)
        for rows in _strips(tm):
            hv = h_ref[rows, :]
            dyv = dy_ref[rows, :]
            r = lax.rsqrt(jnp.mean(hv * hv, axis=-1, keepdims=True) + EPS)
            n = hv * r
            dn = dyv * wv
            acc = acc + _fold8(dyv * n)
            dx = dres_ref[rows, :] + r * (dn - n * jnp.mean(dn * n, axis=-1, keepdims=True))
            dx_ref[rows, :] = dx
            dxb_ref[rows, :] = dx.astype(BF16)
        dw_ref[0:1, :] += _colsum(acc)

    row = pl.BlockSpec((tm, d), lambda i: (i, 0))
    return _pc(body, name=name, grid=(t // tm,),
               in_specs=[row, row, pl.BlockSpec((1, d), lambda i: (0, 0)), row],
               out_specs=[row, row, pl.BlockSpec((8, d), lambda i: (0, 0))],
               out_shape=[_sds((t, d), F32), _sds((t, d), BF16), _sds((8, d), F32)],
               compiler_params=_params())(h, dy, w, dres)


def _final(h1, dd, tgt, w, name):
    t, d = h1.shape
    tm = min(512, t)
    nt = t // tm

    def body(h1_ref, dd_ref, tgt_ref, w_ref, loss_ref, dh_ref, dhb_ref, dw_ref, acc):
        i = pl.program_id(0)

        @pl.when(i == 0)
        def _():
            dw_ref[...] = jnp.zeros_like(dw_ref)
            acc[...] = jnp.zeros_like(acc)
        wv = w_ref[...]
        sq = jnp.zeros((8, d), F32)
        dw = jnp.zeros((8, d), F32)
        for rows in _strips(tm):
            h = h1_ref[rows, :] + dd_ref[rows, :]
            r = lax.rsqrt(jnp.mean(h * h, axis=-1, keepdims=True) + EPS)
            n = h * r
            e = n * wv - tgt_ref[rows, :]
            sq = sq + _fold8(e * e)
            dout = e * (1.0 / d)
            dn = dout * wv
            dw = dw + _fold8(dout * n)
            dh = r * (dn - n * jnp.mean(dn * n, axis=-1, keepdims=True))
            dh_ref[rows, :] = dh
            dhb_ref[rows, :] = dh.astype(BF16)
        acc[...] += _colsum(sq)
        dw_ref[0:1, :] += _colsum(dw)

        @pl.when(i == nt - 1)
        def _():
            loss_ref[...] = jnp.sum(acc[...], axis=-1, keepdims=True) * (0.5 / d)

    row = pl.BlockSpec((tm, d), lambda i: (i, 0))
    return _pc(body, name=name, grid=(nt,),
               in_specs=[row, row, row, pl.BlockSpec((1, d), lambda i: (0, 0))],
               out_specs=[pl.BlockSpec((1, 1), lambda i: (0, 0)), row, row, pl.BlockSpec((8, d), lambda i: (0, 0))],
               out_shape=[_sds((1, 1), F32), _sds((t, d), F32), _sds((t, d), BF16), _sds((8, d), F32)],
               scratch_shapes=[pltpu.VMEM((1, d), F32)], compiler_params=_params())(h1, dd, tgt, w)


def _tile_specs(t, tm, tc, col0):
    th = tm // HALO
    last = t // HALO - 1
    cur = pl.BlockSpec((tm, tc), lambda j, i: (i, col0 + j))
    prev = pl.BlockSpec((HALO, tc), lambda j, i: (jnp.maximum(i * th - 1, 0), col0 + j))
    nxt = pl.BlockSpec((HALO, tc), lambda j, i: (jnp.minimum((i + 1) * th, last), col0 + j))
    return cur, prev, nxt


def _conv_strip(buf, w, k, rows):
    out = None
    for j in range(k):
        term = w[j:j + 1, :] * buf[pl.ds(HALO - (k - 1) + j + rows.start, STRIP), :]
        out = term if out is None else out + term
    return out


def _conv_backward(dbuf, x_strip, emit, w, acc_ref, k, tm, with_bias):
    tc = dbuf.shape[1]
    accs = [jnp.zeros((8, tc), F32) for _ in range(k + int(with_bias))]
    for rows in _strips(tm):
        xs = x_strip(rows)
        dx = None
        for j in range(k):
            ds = dbuf[pl.ds(rows.start + k - 1 - j, STRIP), :]
            term = w[j:j + 1, :] * ds
            dx = term if dx is None else dx + term
            accs[j] = accs[j] + _fold8(ds * xs)
            if with_bias and j == k - 1:
                accs[k] = accs[k] + _fold8(ds)
        emit(rows, dx)
    for j, a in enumerate(accs):
        acc_ref[j:j + 1, :] += _colsum(a)


def _conv_a_fwd(pa, w, d, name):
    t = pa.shape[0]
    tm, tc = min(512, t), _pick(d, (512, 256, 128))
    nd = d // tc

    def body(b_ref, c_ref, v_ref, cp_ref, vp_ref, w_ref, o_ref, q_ref, buf):
        keep = (pl.program_id(1) > 0).astype(F32)
        buf[0:HALO, :] = cp_ref[...].astype(F32) * vp_ref[...].astype(F32) * keep
        for rows in _strips(tm):
            buf[HALO + rows.start:HALO + rows.stop, :] = c_ref[rows, :].astype(F32) * v_ref[rows, :].astype(F32)
        wv = w_ref[...]
        for rows in _strips(tm):
            q = _conv_strip(buf, wv, 3, rows)
            q_ref[rows, :] = q.astype(q_ref.dtype)
            o_ref[rows, :] = (b_ref[rows, :].astype(F32) * q).astype(o_ref.dtype)

    b_cur, _, _ = _tile_specs(t, tm, tc, 0)
    c_cur, c_prev, _ = _tile_specs(t, tm, tc, nd)
    v_cur, v_prev, _ = _tile_specs(t, tm, tc, 2 * nd)
    return _pc(body, name=name, grid=(nd, t // tm),
               in_specs=[b_cur, c_cur, v_cur, c_prev, v_prev, pl.BlockSpec((3, tc), lambda j, i: (0, j))],
               out_specs=[pl.BlockSpec((tm, tc), lambda j, i: (i, j))] * 2,
               out_shape=[_sds((t, d), BF16)] * 2,
               scratch_shapes=[pltpu.VMEM((tm + HALO, tc), F32)],
               compiler_params=_params())(pa, pa, pa, pa, pa, w)


def _conv_a_bwd(pa, q, dya, w, d, name):
    t = pa.shape[0]
    tm, tc = min(512, t), _pick(d, (512, 256, 128))
    nd, nt = d // tc, t // tm

    def body(b_ref, c_ref, v_ref, bn_ref, q_ref, g_ref, gn_ref, w_ref, db_ref, dc_ref, dv_ref, acc_ref, dbuf):
        i = pl.program_id(1)

        @pl.when(i == 0)
        def _():
            acc_ref[...] = jnp.zeros_like(acc_ref)
        for rows in _strips(tm):
            g = g_ref[rows, :].astype(F32)
            dbuf[rows, :] = g * b_ref[rows, :].astype(F32)
            db_ref[rows, :] = (g * q_ref[rows, :].astype(F32)).astype(BF16)
        dbuf[tm:tm + HALO, :] = gn_ref[...].astype(F32) * bn_ref[...].astype(F32) * (i < nt - 1).astype(F32)

        def emit(rows, dp):
            dc_ref[rows, :] = (dp * v_ref[rows, :].astype(F32)).astype(BF16)
            dv_ref[rows, :] = (dp * c_ref[rows, :].astype(F32)).astype(BF16)

        _conv_backward(dbuf, lambda rows: c_ref[rows, :].astype(F32) * v_ref[rows, :].astype(F32), emit,
                       w_ref[...], acc_ref, 3, tm, False)

    b_cur, _, b_next = _tile_specs(t, tm, tc, 0)
    c_cur, _, _ = _tile_specs(t, tm, tc, nd)
    v_cur, _, _ = _tile_specs(t, tm, tc, 2 * nd)
    g_cur, _, g_next = _tile_specs(t, tm, tc, 0)
    out = pl.BlockSpec((tm, tc), lambda j, i: (i, j))
    return _pc(body, name=name, grid=(nd, nt),
               in_specs=[b_cur, c_cur, v_cur, b_next, g_cur, g_cur, g_next,
                         pl.BlockSpec((3, tc), lambda j, i: (0, j))],
               out_specs=[out, out, out, pl.BlockSpec((8, tc), lambda j, i: (0, j))],
               out_shape=[_sds((t, d), BF16)] * 3 + [_sds((8, d), F32)],
               scratch_shapes=[pltpu.VMEM((tm + HALO, tc), F32)],
               compiler_params=_params())(pa, pa, pa, pa, q, dya, dya, w)


def _conv_s_fwd(xbc, w, b, name):
    t, dx = xbc.shape
    tm, tc = min(512, t), _pick(dx, (512, 256, 128))

    def body(x_ref, xp_ref, w_ref, b_ref, o_ref, pre_ref, buf):
        buf[0:HALO, :] = xp_ref[...].astype(F32) * (pl.program_id(1) > 0).astype(F32)
        for rows in _strips(tm):
            buf[HALO + rows.start:HALO + rows.stop, :] = x_ref[rows, :].astype(F32)
        wv, bv = w_ref[...], b_ref[...]
        for rows in _strips(tm):
            pre = _conv_strip(buf, wv, 4, rows) + bv
            pre_ref[rows, :] = pre.astype(pre_ref.dtype)
            o_ref[rows, :] = (pre * _sigmoid(pre)).astype(o_ref.dtype)

    cur, prev, _ = _tile_specs(t, tm, tc, 0)
    return _pc(body, name=name, grid=(dx // tc, t // tm),
               in_specs=[cur, prev, pl.BlockSpec((4, tc), lambda j, i: (0, j)),
                         pl.BlockSpec((1, tc), lambda j, i: (0, j))],
               out_specs=[pl.BlockSpec((tm, tc), lambda j, i: (i, j))] * 2,
               out_shape=[_sds((t, dx), BF16)] * 2,
               scratch_shapes=[pltpu.VMEM((tm + HALO, tc), F32)],
               compiler_params=_params())(xbc, xbc, w, b)


def _dsilu(pre):
    s = _sigmoid(pre)
    return s * (1.0 + pre * (1.0 - s))


def _conv_s_bwd(xbc, pre, dxc, w, name):
    t, dx = xbc.shape
    tm, tc = min(512, t), _pick(dx, (512, 256, 128))
    nt = t // tm

    def body(x_ref, p_ref, pn_ref, g_ref, gn_ref, w_ref, dx_ref, acc_ref, dbuf):
        i = pl.program_id(1)

        @pl.when(i == 0)
        def _():
            acc_ref[...] = jnp.zeros_like(acc_ref)
        for rows in _strips(tm):
            dbuf[rows, :] = g_ref[rows, :].astype(F32) * _dsilu(p_ref[rows, :].astype(F32))
        dbuf[tm:tm + HALO, :] = (gn_ref[...].astype(F32) * _dsilu(pn_ref[...].astype(F32))
                                 * (i < nt - 1).astype(F32))

        def emit(rows, d_in):
            dx_ref[rows, :] = d_in.astype(BF16)

        _conv_backward(dbuf, lambda rows: x_ref[rows, :].astype(F32), emit, w_ref[...], acc_ref, 4, tm, True)

    cur, _, nxt = _tile_specs(t, tm, tc, 0)
    return _pc(body, name=name, grid=(dx // tc, nt),
               in_specs=[cur, cur, nxt, cur, nxt, pl.BlockSpec((4, tc), lambda j, i: (0, j))],
               out_specs=[pl.BlockSpec((tm, tc), lambda j, i: (i, j)), pl.BlockSpec((8, tc), lambda j, i: (0, j))],
               out_shape=[_sds((t, dx), BF16), _sds((8, dx), F32)],
               scratch_shapes=[pltpu.VMEM((tm + HALO, tc), F32)],
               compiler_params=_params())(xbc, pre, pre, dxc, dxc, w)


def _ffn_fwd(hv, w, b, f, name):
    t = hv.shape[0]
    tm, tc = min(512, t), _pick(f, (512, 256, 128))
    nf = f // tc

    def body(h1_ref, h1p_ref, h3_ref, w_ref, b_ref, o_ref, c1_ref, buf):
        buf[0:HALO, :] = h1p_ref[...].astype(F32) * (pl.program_id(1) > 0).astype(F32)
        for rows in _strips(tm):
            buf[HALO + rows.start:HALO + rows.stop, :] = h1_ref[rows, :].astype(F32)
        wv, bv = w_ref[...], b_ref[...]
        for rows in _strips(tm):
            c1 = _conv_strip(buf, wv, 3, rows) + bv
            c1_ref[rows, :] = c1.astype(c1_ref.dtype)
            o_ref[rows, :] = (c1 * _sigmoid(c1) * h3_ref[rows, :].astype(F32)).astype(o_ref.dtype)

    h1_cur, h1_prev, _ = _tile_specs(t, tm, tc, 0)
    h3_cur, _, _ = _tile_specs(t, tm, tc, nf)
    return _pc(body, name=name, grid=(nf, t // tm),
               in_specs=[h1_cur, h1_prev, h3_cur, pl.BlockSpec((3, tc), lambda j, i: (0, j)),
                         pl.BlockSpec((1, tc), lambda j, i: (0, j))],
               out_specs=[pl.BlockSpec((tm, tc), lambda j, i: (i, j))] * 2,
               out_shape=[_sds((t, f), BF16)] * 2,
               scratch_shapes=[pltpu.VMEM((tm + HALO, tc), F32)],
               compiler_params=_params())(hv, hv, hv, w, b)


def _ffn_bwd(hv, c1, dact, w, f, name):
    t = hv.shape[0]
    tm, tc = min(512, t), _pick(f, (512, 256, 128))
    nf, nt = f // tc, t // tm

    def body(h1_ref, h3_ref, h3n_ref, c_ref, cn_ref, g_ref, gn_ref, w_ref, dh1_ref, dh3_ref, acc_ref, dbuf):
        i = pl.program_id(1)

        @pl.when(i == 0)
        def _():
            acc_ref[...] = jnp.zeros_like(acc_ref)
        for rows in _strips(tm):
            c1v, g = c_ref[rows, :].astype(F32), g_ref[rows, :].astype(F32)
            s1 = _sigmoid(c1v)
            dh3_ref[rows, :] = (g * c1v * s1).astype(BF16)
            dbuf[rows, :] = g * h3_ref[rows, :].astype(F32) * s1 * (1.0 + c1v * (1.0 - s1))
        dbuf[tm:tm + HALO, :] = (gn_ref[...].astype(F32) * h3n_ref[...].astype(F32)
                                 * _dsilu(cn_ref[...].astype(F32)) * (i < nt - 1).astype(F32))

        def emit(rows, d_in):
            dh1_ref[rows, :] = d_in.astype(BF16)

        _conv_backward(dbuf, lambda rows: h1_ref[rows, :].astype(F32), emit, w_ref[...], acc_ref, 3, tm, True)

    h1_cur, _, _ = _tile_specs(t, tm, tc, 0)
    h3_cur, _, h3_next = _tile_specs(t, tm, tc, nf)
    g_cur, _, g_next = _tile_specs(t, tm, tc, 0)
    out = pl.BlockSpec((tm, tc), lambda j, i: (i, j))
    return _pc(body, name=name, grid=(nf, nt),
               in_specs=[h1_cur, h3_cur, h3_next, g_cur, g_next, g_cur, g_next,
                         pl.BlockSpec((3, tc), lambda j, i: (0, j))],
               out_specs=[out, out, pl.BlockSpec((8, tc), lambda j, i: (0, j))],
               out_shape=[_sds((t, f), BF16), _sds((t, f), BF16), _sds((8, f), F32)],
               scratch_shapes=[pltpu.VMEM((tm + HALO, tc), F32)],
               compiler_params=_params())(hv, hv, hv, c1, c1, dact, dact, w)


def _gnorm_fwd(y, z, w, name):
    t, di = y.shape
    gw = di // N_GROUPS
    tm = min(512, t)

    def body(y_ref, z_ref, w_ref, o_ref):
        wv = w_ref[...]
        for rows in _strips(tm):
            zv = z_ref[rows, :].astype(F32)
            yz = y_ref[rows, :].astype(F32) * zv * _sigmoid(zv)
            r = lax.rsqrt(jnp.mean(yz * yz, axis=-1, keepdims=True) + EPS)
            o_ref[rows, :] = (yz * r * wv).astype(o_ref.dtype)

    blk = pl.BlockSpec((tm, gw), lambda j, i: (i, j))
    return _pc(body, name=name, grid=(N_GROUPS, t // tm),
               in_specs=[blk, blk, pl.BlockSpec((1, gw), lambda j, i: (0, j))],
               out_specs=blk, out_shape=_sds((t, di), BF16), compiler_params=_params())(y, z, w)


def _gnorm_bwd(y, z, dyn, w, name):
    t, di = y.shape
    gw = di // N_GROUPS
    tm = min(512, t)

    def body(y_ref, z_ref, g_ref, w_ref, dy_ref, dz_ref, dw_ref):
        @pl.when(pl.program_id(1) == 0)
        def _():
            dw_ref[...] = jnp.zeros_like(dw_ref)
        wv = w_ref[...]
        acc = jnp.zeros((8, gw), F32)
        for rows in _strips(tm):
            yv, zv, g = y_ref[rows, :].astype(F32), z_ref[rows, :].astype(F32), g_ref[rows, :].astype(F32)
            s = _sigmoid(zv)
            sz = zv * s
            yz = yv * sz
            r = lax.rsqrt(jnp.mean(yz * yz, axis=-1, keepdims=True) + EPS)
            n = yz * r
            dn = g * wv
            acc = acc + _fold8(g * n)
            dyz = r * (dn - n * jnp.mean(dn * n, axis=-1, keepdims=True))
            dy_ref[rows, :] = (dyz * sz).astype(BF16)
            dz_ref[rows, :] = (dyz * yv * s * (1.0 + zv * (1.0 - s))).astype(BF16)
        dw_ref[0:1, :] += _colsum(acc)

    blk = pl.BlockSpec((tm, gw), lambda j, i: (i, j))
    return _pc(body, name=name, grid=(N_GROUPS, t // tm),
               in_specs=[blk, blk, blk, pl.BlockSpec((1, gw), lambda j, i: (0, j))],
               out_specs=[blk, blk, pl.BlockSpec((8, gw), lambda j, i: (0, j))],
               out_shape=[_sds((t, di), BF16), _sds((t, di), BF16), _sds((8, di), F32)],
               compiler_params=_params())(y, z, dyn, w)


def _merge_fwd(gates, ya, ys, d, name):
    t = ya.shape[0]
    tm, tc = min(512, t), _pick(d, (512, 256, 128))
    nd = d // tc

    def body(ga_ref, gs_ref, ya_ref, ys_ref, o_ref):
        for rows in _strips(tm):
            o_ref[rows, :] = (_sigmoid(ga_ref[rows, :].astype(F32)) * ya_ref[rows, :].astype(F32)
                              + _sigmoid(gs_ref[rows, :].astype(F32)) * ys_ref[rows, :].astype(F32)
                              ).astype(o_ref.dtype)

    blk = pl.BlockSpec((tm, tc), lambda j, i: (i, j))
    return _pc(body, name=name, grid=(nd, t // tm),
               in_specs=[blk, pl.BlockSpec((tm, tc), lambda j, i: (i, nd + j)), blk, blk],
               out_specs=blk, out_shape=_sds((t, d), BF16), compiler_params=_params())(gates, gates, ya, ys)


def _merge_bwd(dm, gates, ya, ys, d, name):
    t = ya.shape[0]
    tm, tc = min(512, t), _pick(d, (512, 256, 128))
    nd = d // tc

    def body(dm_ref, ga_ref, gs_ref, ya_ref, ys_ref, dya_ref, dys_ref, dga_ref, dgs_ref):
        for rows in _strips(tm):
            g = dm_ref[rows, :].astype(F32)
            sa, ss = _sigmoid(ga_ref[rows, :].astype(F32)), _sigmoid(gs_ref[rows, :].astype(F32))
            dya_ref[rows, :] = (g * sa).astype(BF16)
            dys_ref[rows, :] = (g * ss).astype(BF16)
            dga_ref[rows, :] = (g * ya_ref[rows, :].astype(F32) * sa * (1.0 - sa)).astype(BF16)
            dgs_ref[rows, :] = (g * ys_ref[rows, :].astype(F32) * ss * (1.0 - ss)).astype(BF16)

    blk = pl.BlockSpec((tm, tc), lambda j, i: (i, j))
    return _pc(body, name=name, grid=(nd, t // tm),
               in_specs=[blk, blk, pl.BlockSpec((tm, tc), lambda j, i: (i, nd + j)), blk, blk],
               out_specs=[blk] * 4, out_shape=[_sds((t, d), BF16)] * 4,
               compiler_params=_params())(dm, gates, gates, ya, ys)


def _ssd_chunk_terms(dtr, dtb, alog):
    xx = dtr + dtb
    dt = jnp.maximum(xx, 0.0) + jnp.log(1.0 + jnp.exp(-jnp.abs(xx)))
    a = -jnp.exp(alog)
    li = lax.broadcasted_iota(jnp.int32, (CHUNK, CHUNK), 0)
    si = lax.broadcasted_iota(jnp.int32, (CHUNK, CHUNK), 1)
    causal = li >= si
    acum = _dot(causal.astype(F32), dt * a, 1, 0, HIGHEST)
    return xx, dt, a, acum, acum.T, causal


def _split3(x):
    hi = x.astype(BF16)
    r = x - hi.astype(F32)
    mid = r.astype(BF16)
    lo = (r - mid.astype(F32)).astype(BF16)
    return hi, mid, lo


def _expand(v, e):
    hi, mid, lo = _split3(v)
    return _dot(hi, e, 1, 0) + _dot(mid, e, 1, 0) + _dot(lo, e, 1, 0)


def _segsum(s, e):
    hi, mid, lo = _split3(s)
    return _dot(hi, e, 1, 1) + _dot(mid, e, 1, 1) + _dot(lo, e, 1, 1)


def _head_maps(di):
    nh = di // HEAD_DIM
    h = jnp.arange(DT_LANES)[:, None]
    e64 = (jnp.arange(di)[None, :] // HEAD_DIM == h).astype(BF16)
    e128 = (jnp.arange(nh * CHUNK)[None, :] // CHUNK == h).astype(BF16)
    return e64, e128


def _pair_blockdiag(p, left):
    zero = jnp.zeros_like(p)
    return jnp.concatenate([jnp.where(left, p, zero), jnp.where(left, zero, p)], axis=0)


def _ssd_fwd(xc, dtr, dtb, alog, dskx, di, name):
    t = xc.shape[0]
    dx = xc.shape[1]
    nc = t // CHUNK
    nh = di // HEAD_DIM
    hpg = nh // N_GROUPS
    gw = hpg * HEAD_DIM
    boff, coff = di, di + N_GROUPS * D_STATE
    e64, e128 = _head_maps(di)

    def body(xc_ref, dtr_ref, dtb_ref, alog_ref, dsk_ref, e64_ref, e128_ref, y_ref, st_ref, state):
        @pl.when(pl.program_id(0) == 0)
        def _():
            state[...] = jnp.zeros_like(state)
        _, dt, _, acum, acum_t, causal = _ssd_chunk_terms(dtr_ref[...], dtb_ref[...], alog_ref[...])
        last = acum[CHUNK - 1:CHUNK, :]
        e64v = e64_ref[...]
        dtx = _expand(dt, e64v)
        eax = _expand(jnp.exp(acum), e64v)
        dex = _expand(dt * jnp.exp(last - acum), e64v)
        acx = _expand(acum, e128_ref[...])
        st_ref[0] = state[...]
        left = lax.broadcasted_iota(jnp.int32, (CHUNK, 2 * HEAD_DIM), 1) < HEAD_DIM
        for g in range(N_GROUPS):
            gs = slice(g * gw, (g + 1) * gw)
            bg = xc_ref[:, boff + g * D_STATE:boff + (g + 1) * D_STATE]
            cg = xc_ref[:, coff + g * D_STATE:coff + (g + 1) * D_STATE]
            gm = _dot(cg, bg, 1, 1)
            xg = xc_ref[:, gs].astype(F32)
            xdb = (xg * dtx[:, gs]).astype(BF16)
            sin = state[:, gs]
            yo = _dot(cg, sin.astype(BF16), 1, 0) * eax[:, gs]
            for jp in range(hpg // 2):
                h0 = g * hpg + 2 * jp
                ps = slice(jp * 2 * HEAD_DIM, (jp + 1) * 2 * HEAD_DIM)
                ms = []
                for hh in (h0, h0 + 1):
                    seg = acx[:, hh * CHUNK:(hh + 1) * CHUNK] - acum_t[hh:hh + 1, :]
                    ms.append((gm * jnp.exp(jnp.where(causal, seg, -1e30))).astype(BF16))
                yd = _dot(jnp.concatenate(ms, axis=1), _pair_blockdiag(xdb[:, ps], left), 1, 0)
                col = slice(g * gw + jp * 2 * HEAD_DIM, g * gw + (jp + 1) * 2 * HEAD_DIM)
                y_ref[:, col] = (yd + yo[:, ps] + dsk_ref[:, col] * xg[:, ps]).astype(y_ref.dtype)
            xe = (xg * dex[:, gs]).astype(BF16)
            state[:, gs] = eax[CHUNK - 1:CHUNK, gs] * sin + _dot(bg, xe, 0, 0)

    small = pl.BlockSpec((1, DT_LANES), lambda c: (0, 0))
    whole = lambda a: pl.BlockSpec(a.shape, lambda c: (0, 0))
    return _pc(body, name=name, grid=(nc,),
               in_specs=[pl.BlockSpec((CHUNK, dx), lambda c: (c, 0)),
                         pl.BlockSpec((CHUNK, DT_LANES), lambda c: (c, 0)), small, small,
                         whole(dskx), whole(e64), whole(e128)],
               out_specs=[pl.BlockSpec((CHUNK, di), lambda c: (c, 0)),
                          pl.BlockSpec((1, D_STATE, di), lambda c: (c, 0, 0))],
               out_shape=[_sds((t, di), BF16), _sds((nc, D_STATE, di), F32)],
               scratch_shapes=[pltpu.VMEM((D_STATE, di), F32)],
               compiler_params=_params())(xc, dtr, dtb, alog, dskx, e64, e128)


def _ssd_bwd(xc, dtr, dy, states, dtb, alog, dskx, di, name):
    t = xc.shape[0]
    dx = xc.shape[1]
    nc = t // CHUNK
    nh = di // HEAD_DIM
    hpg = nh // N_GROUPS
    gw = hpg * HEAD_DIM
    boff, coff = di, di + N_GROUPS * D_STATE
    e64, e128 = _head_maps(di)

    def body(xc_ref, dtr_ref, dy_ref, st_ref, dtb_ref, alog_ref, dsk_ref, e64_ref, e128_ref,
             dxc_ref, ddtr_ref, sm_ref, dstate, darow):
        @pl.when(pl.program_id(0) == 0)
        def _():
            dstate[...] = jnp.zeros_like(dstate)
            sm_ref[...] = jnp.zeros_like(sm_ref)
        darow[...] = jnp.zeros_like(darow)
        xx, dt, a, acum, acum_t, causal = _ssd_chunk_terms(dtr_ref[...], dtb_ref[...], alog_ref[...])
        last = acum[CHUNK - 1:CHUNK, :]
        e64v = e64_ref[...]
        dtx = _expand(dt, e64v)
        eax = _expand(jnp.exp(acum), e64v)
        eex = _expand(jnp.exp(last - acum), e64v)
        acx = _expand(acum, e128_ref[...])
        left = lax.broadcasted_iota(jnp.int32, (CHUNK, 2 * HEAD_DIM), 1) < HEAD_DIM
        lane = lax.broadcasted_iota(jnp.int32, (CHUNK, DT_LANES), 1)
        sub8 = lax.broadcasted_iota(jnp.int32, (8, gw), 0)
        da_col = jnp.zeros((CHUNK, DT_LANES), F32)
        ddt_col = jnp.zeros((CHUNK, DT_LANES), F32)
        rows = jnp.zeros((8, DT_LANES), F32)
        for g in range(N_GROUPS):
            gs = slice(g * gw, (g + 1) * gw)
            bg = xc_ref[:, boff + g * D_STATE:boff + (g + 1) * D_STATE]
            cg = xc_ref[:, coff + g * D_STATE:coff + (g + 1) * D_STATE]
            gm = _dot(cg, bg, 1, 1)
            e64g = e64v[:, gs]
            xg = xc_ref[:, gs].astype(F32)
            dtg, eag, eeg = dtx[:, gs], eax[:, gs], eex[:, gs]
            xd = xg * dtg
            xdb = xd.astype(BF16)
            dyb = dy_ref[:, gs]
            dyf = dyb.astype(F32)
            sin = st_ref[0, :, gs]
            sinb = sin.astype(BF16)
            ds = dstate[:, gs]
            dsb = ds.astype(BF16)
            bds = _dot(bg, dsb, 1, 0)
            dyeb = (dyf * eag).astype(BF16)
            dcg = _dot(dyeb, sinb, 1, 1)
            dstate[:, gs] = eag[CHUNK - 1:CHUNK, :] * ds + _dot(cg, dyeb, 0, 0)
            yo = _dot(cg, sinb, 1, 0) * eag
            xe = xd * eeg
            dbg = _dot(xe.astype(BF16), dsb, 1, 1)
            wterm = bds * xe
            da_col = da_col + _segsum(dyf * yo - wterm, e64g)
            dg = jnp.zeros((CHUNK, CHUNK), F32)
            dxd_parts = []
            for jp in range(hpg // 2):
                h0 = g * hpg + 2 * jp
                ps = slice(jp * 2 * HEAD_DIM, (jp + 1) * 2 * HEAD_DIM)
                lms, mfs = [], []
                for hh in (h0, h0 + 1):
                    seg = acx[:, hh * CHUNK:(hh + 1) * CHUNK] - acum_t[hh:hh + 1, :]
                    lm = jnp.exp(jnp.where(causal, seg, -1e30))
                    lms.append(lm)
                    mfs.append(gm * lm)
                mstack = jnp.concatenate([m.astype(BF16) for m in mfs], axis=0)
                dyp = dyb[:, ps]
                dxd_parts.append(_dot(mstack, _pair_blockdiag(dyp, left), 0, 0))
                dm2 = _dot(dyp, _pair_blockdiag(xdb[:, ps], left), 1, 1)
                for k, hh in enumerate((h0, h0 + 1)):
                    dm = dm2[:, k * CHUNK:(k + 1) * CHUNK]
                    dg = dg + dm * lms[k]
                    q = dm * mfs[k]
                    da_col = da_col + jnp.where(lane == hh, jnp.sum(q, axis=1, keepdims=True), 0.0)
                    darow[hh:hh + 1, :] = -jnp.sum(q, axis=0, keepdims=True)
            dxd = jnp.concatenate(dxd_parts, axis=1) + bds * eeg
            ddt_col = ddt_col + _segsum(dxd * xg, e64g)
            rsum = (jnp.where(sub8 == 0, jnp.sum(wterm, axis=0, keepdims=True), 0.0)
                    + jnp.where(sub8 == 1, jnp.sum(ds * sin, axis=0, keepdims=True), 0.0)
                    + jnp.where(sub8 == 2, jnp.sum(dyf * xg, axis=0, keepdims=True), 0.0))
            rows = rows + _segsum(rsum, e64g)
            dxc_ref[:, gs] = (dxd * dtg + dsk_ref[:, gs] * dyf).astype(dxc_ref.dtype)
            dgb = dg.astype(BF16)
            dxc_ref[:, boff + g * D_STATE:boff + (g + 1) * D_STATE] = (
                dbg + _dot(dgb, cg, 0, 0)).astype(dxc_ref.dtype)
            dxc_ref[:, coff + g * D_STATE:coff + (g + 1) * D_STATE] = (
                dcg + _dot(dgb, bg, 1, 0)).astype(dxc_ref.dtype)
        at_last = rows[0:1, :] + jnp.exp(last) * rows[1:2, :]
        is_last = lax.broadcasted_iota(jnp.int32, (CHUNK, DT_LANES), 0) == CHUNK - 1
        da = da_col + jnp.where(is_last, at_last, 0.0) + darow[...].T
        li = lax.broadcasted_iota(jnp.int32, (CHUNK, CHUNK), 0)
        si = lax.broadcasted_iota(jnp.int32, (CHUNK, CHUNK), 1)
        dla = _dot((si >= li).astype(F32), da, 1, 0, HIGHEST)
        ddtr = (ddt_col + dla * a) * _sigmoid(xx)
        ddtr_ref[...] = ddtr
        sm_ref[0:1, :] += jnp.sum(ddtr, axis=0, keepdims=True)
        sm_ref[1:2, :] += jnp.sum(dla * dt, axis=0, keepdims=True) * a
        sm_ref[2:3, :] += rows[2:3, :]

    small = pl.BlockSpec((1, DT_LANES), lambda c: (0, 0))
    whole = lambda a: pl.BlockSpec(a.shape, lambda c: (0, 0))
    rev = lambda c: (nc - 1 - c, 0)
    return _pc(body, name=name, grid=(nc,),
               in_specs=[pl.BlockSpec((CHUNK, dx), rev), pl.BlockSpec((CHUNK, DT_LANES), rev),
                         pl.BlockSpec((CHUNK, di), rev),
                         pl.BlockSpec((1, D_STATE, di), lambda c: (nc - 1 - c, 0, 0)), small, small,
                         whole(dskx), whole(e64), whole(e128)],
               out_specs=[pl.BlockSpec((CHUNK, dx), rev), pl.BlockSpec((CHUNK, DT_LANES), rev),
                          pl.BlockSpec((8, DT_LANES), lambda c: (0, 0))],
               out_shape=[_sds((t, dx), BF16), _sds((t, DT_LANES), F32), _sds((8, DT_LANES), F32)],
               scratch_shapes=[pltpu.VMEM((D_STATE, di), F32), pltpu.VMEM((DT_LANES, CHUNK), F32)],
               compiler_params=_params())(xc, dtr, dy, states, dtb, alog, dskx, e64, e128)


def _adamw(parts, w, m, v, name):
    npart, rows, width = parts.shape
    tr, tw = (_pick(rows, (64, 32, 16, 8)), width) if rows % 8 == 0 else (rows, 128)
    c1 = 1.0 - ADAM_B1 ** ADAM_STEP
    c2 = 1.0 - ADAM_B2 ** ADAM_STEP

    row_strips = _strips(tr) if tr % STRIP == 0 else [slice(0, tr)]
    col_chunks = [slice(c, c + 512) for c in range(0, tw, 512)] if tw % 512 == 0 else [slice(0, tw)]

    def body(p_ref, w_ref, m_ref, v_ref, g_ref, d_ref, nm_ref, nv_ref):
        for rows in row_strips:
            for cols in col_chunks:
                g = p_ref[0, rows, cols].astype(F32)
                for p in range(1, npart):
                    g = g + p_ref[p, rows, cols].astype(F32)
                nm = ADAM_B1 * m_ref[rows, cols] + (1.0 - ADAM_B1) * g
                nv = ADAM_B2 * v_ref[rows, cols] + (1.0 - ADAM_B2) * (g * g)
                g_ref[rows, cols] = g
                nm_ref[rows, cols] = nm
                nv_ref[rows, cols] = nv
                d_ref[rows, cols] = -ADAM_LR * ((nm / c1) / (jnp.sqrt(nv / c2) + ADAM_EPS)
                                                + ADAM_WD * w_ref[rows, cols])

    blk = pl.BlockSpec((tr, tw), lambda i, j: (i, j))
    return _pc(body, name=name, grid=(rows // tr, width // tw),
               in_specs=[pl.BlockSpec((npart, tr, tw), lambda i, j: (0, i, j)), blk, blk, blk],
               out_specs=[blk] * 4, out_shape=[_sds((rows, width), F32)] * 4,
               compiler_params=_params())(parts, w, m, v)


def _sum_parts(parts, name, tile=None):
    npart, rows, width = parts.shape
    tile = rows if tile is None else tile

    def body(p_ref, o_ref):
        for rows_ in _strips(tile, 8 if parts.dtype == F32 else STRIP):
            g = p_ref[0, rows_, :].astype(F32)
            for p in range(1, npart):
                g = g + p_ref[p, rows_, :].astype(F32)
            o_ref[rows_, :] = g

    return _pc(body, name=name, grid=(rows // tile,),
               in_specs=[pl.BlockSpec((npart, tile, width), lambda i: (0, i, 0))],
               out_specs=pl.BlockSpec((tile, width), lambda i: (i, 0)),
               out_shape=_sds((rows, width), F32), compiler_params=_params())(parts)


def _flip(k):
    x, y, c = lax.axis_index("x"), lax.axis_index("y"), lax.axis_index("c")
    px = 1 - x if k & 4 else x
    py = 1 - y if k & 2 else y
    pc = 1 - c if k & 1 else c
    return (px, py, pc), 4 * px + 2 * py + pc


DIRECT = tuple((k, 0) for k in range(1, N_DEV))
TO_CHIPS = ((1, 0), (2, 0), (4, 0), (6, 0))
TO_SIBLING = ((1, 2), (1, 4), (1, 6))


def _copies(arrays, lands, send_sems, recv_sems, scatter, moves):
    _, me = _flip(0)
    outgoing, incoming = [], []
    for i, (kd, kb) in enumerate(moves):
        peer, pidx = _flip(kd)
        _, out_slot = _flip(kb)
        _, in_slot = _flip(kd ^ kb)
        for j, land_ref in enumerate(lands):
            if kb:
                src = land_ref.at[out_slot]
            else:
                src = arrays[j].at[pidx] if scatter[j] else arrays[j]
            sem = len(lands) * i + j
            for dst, bucket in ((land_ref.at[out_slot], outgoing), (land_ref.at[in_slot], incoming)):
                bucket.append(pltpu.make_async_remote_copy(
                    src_ref=src, dst_ref=dst, send_sem=send_sems.at[sem], recv_sem=recv_sems.at[sem],
                    device_id=peer, device_id_type=MESH))
    return outgoing, incoming


HBM_SPEC = pl.BlockSpec(memory_space=pltpu.HBM)
SEM_SPEC = pl.BlockSpec(memory_space=pltpu.SEMAPHORE)
ANY_SPEC = pl.BlockSpec(memory_space=pl.ANY)
EFFECT = pltpu.SideEffectType.DATAFLOW_SIDE_EFFECTING


def _landing_zones(arrays, scatter):
    _, me = _flip(0)
    lands = []
    for a, sc in zip(arrays, scatter):
        own = lax.dynamic_index_in_dim(a, me, 0, keepdims=True) if sc else a[None]
        shape = a.shape if sc else (N_DEV,) + a.shape
        lands.append(lax.dynamic_update_slice(lax.empty(shape, a.dtype), own, (me,) + (0,) * (len(shape) - 1)))
    return lands


def _xchg_start(arrays, scatter, after, name, moves=DIRECT, lands=None):
    if lands is None:
        lands = _landing_zones(arrays, scatter)
    na, nl = len(arrays), len(lands)

    def body(*refs):
        ins, outs = refs[:na + nl], refs[na + nl + 1:]
        outgoing, _ = _copies(ins[:na], ins[na:], outs[0], outs[1], scatter, moves)
        for cp in outgoing:
            cp.start()
        outs[-1][...] = jnp.zeros_like(outs[-1])

    nsem = nl * len(moves)
    operands = [pltpu.with_memory_space_constraint(a, pltpu.HBM) for a in list(arrays) + list(lands)]
    out = _pc(body, name=name,
              out_shape=(pltpu.SemaphoreType.DMA((nsem,)), pltpu.SemaphoreType.DMA((nsem,)),
                         *[pltpu.HBM(a.shape, a.dtype) for a in operands], _sds((8, 128), F32)),
              in_specs=[HBM_SPEC] * (na + nl) + [ANY_SPEC],
              out_specs=(SEM_SPEC, SEM_SPEC, *[HBM_SPEC] * (na + nl), pl.BlockSpec(memory_space=pltpu.VMEM)),
              input_output_aliases={i: 2 + i for i in range(na + nl)},
              compiler_params=pltpu.CompilerParams(has_side_effects=EFFECT))(*operands, after)
    return dict(sems=out[:2], thru=out[2:2 + na + nl], token=out[-1], scatter=scatter, na=na, moves=moves)


def _xchg_wait(handle, after, name):
    na, scatter, moves, thru = handle["na"], handle["scatter"], handle["moves"], handle["thru"]
    n = len(thru)

    def body(*refs):
        ins = refs[:n]
        outgoing, incoming = _copies(ins[:na], ins[na:], refs[n], refs[n + 1], scatter, moves)
        for cp in outgoing:
            cp.wait_send()
        for cp in incoming:
            cp.wait_recv()

    out = _pc(body, name=name, out_shape=tuple(pltpu.HBM(a.shape, a.dtype) for a in thru),
              in_specs=[HBM_SPEC] * n + [SEM_SPEC, SEM_SPEC, ANY_SPEC], out_specs=tuple([HBM_SPEC] * n),
              input_output_aliases={i: i for i in range(n)},
              compiler_params=pltpu.CompilerParams(has_side_effects=EFFECT))(*thru, *handle["sems"], after)
    return out[na:]


def _pack(arrs, width, row_mult):
    flat = jnp.concatenate([a.reshape(-1) for a in arrs])
    n = flat.shape[0]
    rows = -(-n // (width * row_mult)) * row_mult
    return jnp.pad(flat, (0, rows * width - n)).reshape(rows, width)


def _unpack(packed, shapes, lead=None):
    out, off = [], 0
    flat = packed.reshape(-1) if lead is None else packed.reshape(lead, -1)
    for s in shapes:
        n = math.prod(s)
        if lead is None:
            out.append(flat[off:off + n].reshape(s))
        else:
            out.append(flat[:, off:off + n].reshape((lead,) + tuple(s)))
        off += n
    return out


def _blocks_to_cols(blocks):
    nb, rows, n = blocks.shape
    return blocks.transpose(1, 0, 2).reshape(rows, nb * n)


def _pad_rows(a, rows):
    return jnp.pad(a, ((0, rows - a.shape[0]), (0, 0)))


def _pad_lanes(a, lanes):
    return jnp.pad(a, ((0, 0), (0, lanes - a.shape[1])))


REST = ("w_a_out", "w_s_out", "w_o", "w_up", "w_down")
TRANSPOSED = ("w_up", "w_in")
CONVS = ("conv_a_w", "ssd_conv_w", "ffn_conv_w")
REPL = ("norm_mix_w", "ssd_conv_b", "dt_bias", "a_log", "d_skip", "ssd_norm_w", "norm_ffn_w", "ffn_conv_b",
        "final_norm_w")
ORDER = ("norm_mix_w", "w_in", "conv_a_w", "w_a_out", "ssd_conv_w", "ssd_conv_b", "dt_bias", "a_log", "d_skip",
         "ssd_norm_w", "w_s_out", "w_o", "norm_ffn_w", "w_up", "ffn_conv_w", "ffn_conv_b", "w_down", "final_norm_w")


def _as_rows(name, block):
    return block[0].T if name in TRANSPOSED else block[0]


def kernel(x, norm_mix_w, w_in, conv_a_w, w_a_out, ssd_conv_w, ssd_conv_b, dt_bias, a_log, d_skip, ssd_norm_w, w_s_out, w_o, norm_ffn_w, w_up, ffn_conv_w, ffn_conv_b, w_down, final_norm_w, loss_target, m_norm_mix_w, m_w_in, m_conv_a_w, m_w_a_out, m_ssd_conv_w, m_ssd_conv_b, m_dt_bias, m_a_log, m_d_skip, m_ssd_norm_w, m_w_s_out, m_w_o, m_norm_ffn_w, m_w_up, m_ffn_conv_w, m_ffn_conv_b, m_w_down, m_final_norm_w, v_norm_mix_w, v_w_in, v_conv_a_w, v_w_a_out, v_ssd_conv_w, v_ssd_conv_b, v_dt_bias, v_a_log, v_d_skip, v_ssd_norm_w, v_w_s_out, v_w_o, v_norm_ffn_w, v_w_up, v_ffn_conv_w, v_ffn_conv_b, v_w_down, v_final_norm_w):
    wts = dict(norm_mix_w=norm_mix_w, w_in=w_in, conv_a_w=conv_a_w, w_a_out=w_a_out, ssd_conv_w=ssd_conv_w,
               ssd_conv_b=ssd_conv_b, dt_bias=dt_bias, a_log=a_log, d_skip=d_skip, ssd_norm_w=ssd_norm_w,
               w_s_out=w_s_out, w_o=w_o, norm_ffn_w=norm_ffn_w, w_up=w_up, ffn_conv_w=ffn_conv_w,
               ffn_conv_b=ffn_conv_b, w_down=w_down, final_norm_w=final_norm_w)
    mom1 = dict(norm_mix_w=m_norm_mix_w, w_in=m_w_in, conv_a_w=m_conv_a_w, w_a_out=m_w_a_out,
                ssd_conv_w=m_ssd_conv_w, ssd_conv_b=m_ssd_conv_b, dt_bias=m_dt_bias, a_log=m_a_log, d_skip=m_d_skip,
                ssd_norm_w=m_ssd_norm_w, w_s_out=m_w_s_out, w_o=m_w_o, norm_ffn_w=m_norm_ffn_w, w_up=m_w_up,
                ffn_conv_w=m_ffn_conv_w, ffn_conv_b=m_ffn_conv_b, w_down=m_w_down, final_norm_w=m_final_norm_w)
    mom2 = dict(norm_mix_w=v_norm_mix_w, w_in=v_w_in, conv_a_w=v_conv_a_w, w_a_out=v_w_a_out,
                ssd_conv_w=v_ssd_conv_w, ssd_conv_b=v_ssd_conv_b, dt_bias=v_dt_bias, a_log=v_a_log, d_skip=v_d_skip,
                ssd_norm_w=v_ssd_norm_w, w_s_out=v_w_s_out, w_o=v_w_o, norm_ffn_w=v_norm_ffn_w, w_up=v_w_up,
                ffn_conv_w=v_ffn_conv_w, ffn_conv_b=v_ffn_conv_b, w_down=v_w_down, final_norm_w=v_final_norm_w)

    t, d = x.shape[1], x.shape[2]
    di = 2 * d
    nh = di // HEAD_DIM
    dxw = di + 2 * N_GROUPS * D_STATE
    f = w_down.shape[1] * N_DEV
    n_in = w_in.shape[2] * N_DEV
    me = 4 * lax.axis_index("x") + 2 * lax.axis_index("y") + lax.axis_index("c")

    rest_local = [_as_rows(k, wts[k]).astype(BF16) for k in REST]
    nrows = [a.shape[0] for a in rest_local]
    n_blk = w_in.shape[2]
    in_local = w_in[0].T.astype(BF16)
    conv_shapes = [wts[k].shape[1:] for k in CONVS]
    conv_local = _pack([wts[k] for k in CONVS], d, 8)
    x2, tgt = x[0], loss_target[0]
    h_in = _xchg_start([in_local, conv_local], [False, False], x2, "gather_in_start", moves=TO_CHIPS)
    u = _rms_fwd(x2, norm_mix_w, "norm_mix")
    part = _xchg_wait(h_in, u, "gather_in_wait")
    h_fwd = _xchg_start([], [False, False], part[0], "gather_in_forward_start", moves=TO_SIBLING, lands=part)
    in_all, conv_all = _xchg_wait(h_fwd, part[1], "gather_in_forward_wait")
    win_t = in_all.reshape(n_in, d)
    h_rest = _xchg_start(rest_local, [False] * len(REST), in_all, "gather_rest_start")
    c_a, c_s, c_f = _unpack(conv_all, conv_shapes, N_DEV)
    caw, scw, fcw = _blocks_to_cols(c_a), _blocks_to_cols(c_s), _blocks_to_cols(c_f)

    o_z, o_x, o_dt = 5 * d, 7 * d, 7 * d + dxw
    seg_bounds = [0, d, 2 * d, 3 * d, 4 * d, o_z, o_x, o_dt]
    w_dt = _pad_rows(win_t[o_dt:], DT_LANES)
    dtb, alog = (_pad_lanes(p[...].reshape(1, nh), DT_LANES) for p in (dt_bias, a_log))
    dskx = jnp.repeat(d_skip.reshape(1, nh), HEAD_DIM, axis=1)

    tok = h_rest["token"]
    gates = _mm([(u, win_t[:2 * d])], "nt", BF16, "proj_gates", after=tok)
    pa = _mm([(u, win_t[2 * d:o_z])], "nt", BF16, "proj_a", after=tok)
    z = _mm([(u, win_t[o_z:o_x])], "nt", BF16, "proj_z", after=tok)
    xbc = _mm([(u, win_t[o_x:o_dt])], "nt", BF16, "proj_xbc", after=tok)
    dtr = _mm([(u, w_dt)], "nt", F32, "proj_dt", after=tok)
    ya_in, q_a = _conv_a_fwd(pa, caw, d, "conv_a")
    xc, pre_s = _conv_s_fwd(xbc, scw, ssd_conv_b, "conv_s")
    y, states = _ssd_fwd(xc, dtr, dtb, alog, dskx, di, "ssd")
    yn = _gnorm_fwd(y, z, ssd_norm_w, "gnorm")
    rest_all = _xchg_wait(h_rest, yn, "gather_rest_wait")
    waout, wsout, wo, wup_t, wdown = (a.reshape(N_DEV * n, d) for a, n in zip(rest_all, nrows))
    y_a = _mm([(ya_in, waout)], "nn", BF16, "a_out")
    y_s = _mm([(yn, wsout)], "nn", BF16, "s_out")
    merged = _merge_fwd(gates, y_a, y_s, d, "merge")
    mo = _mm([(merged, wo)], "nn", F32, "o_proj")
    h1, v = _resnorm_fwd(x2, mo, norm_ffn_w, "norm_ffn")
    hv = _mm([(v, wup_t)], "nt", BF16, "up_proj")
    act, c1 = _ffn_fwd(hv, fcw, ffn_conv_b, f, "ffn_act")
    dd = _mm([(act, wdown)], "nn", F32, "down_proj")
    loss11, dh2, dh2b, g_fnw = _final(h1, dd, tgt, final_norm_w.reshape(1, d), "final")

    dact = _mm([(dh2b, wdown)], "nt", BF16, "d_act")
    gw_down = _mm_tn(act, dh2b, "gw_down")
    dh1f, dh3, g_ffn = _ffn_bwd(hv, c1, dact, fcw, f, "ffn_act_bwd")
    dv = _mm([(dh1f, wup_t[:f]), (dh3, wup_t[f:])], "nn", F32, "d_v")
    gw_up_t = jnp.concatenate([_mm_tn(dh1f, v, "gw_up1"), _mm_tn(dh3, v, "gw_up3")], axis=0)
    dh1, dh1b, g_nfw = _rms_bwd(h1, dv, norm_ffn_w, dh2, "norm_ffn_bwd")
    dmerged = _mm([(dh1b, wo)], "nt", BF16, "d_merged")
    gw_o = _mm_tn(merged, dh1b, "gw_o")
    dya, dys, dga, dgs = _merge_bwd(dmerged, gates, y_a, y_s, d, "merge_bwd")
    dyain = _mm([(dya, waout)], "nt", BF16, "d_ya_in")
    gw_aout = _mm_tn(ya_in, dya, "gw_a_out")
    db, dc, dvv, g_caw = _conv_a_bwd(pa, q_a, dyain, caw, d, "conv_a_bwd")
    dyn = _mm([(dys, wsout)], "nt", BF16, "d_yn")
    gw_sout = _mm_tn(yn, dys, "gw_s_out")
    grads_rest = dict(w_a_out=gw_aout, w_s_out=gw_sout, w_o=gw_o, w_up=gw_up_t, w_down=gw_down)
    rest_parts = [grads_rest[k].reshape(N_DEV, n, d) for k, n in zip(REST, nrows)]
    h_grest = _xchg_start(rest_parts, [True] * len(REST), rest_parts[0], "scatter_rest_start")
    dy, dz, g_snw = _gnorm_bwd(y, z, dyn, ssd_norm_w, "gnorm_bwd")
    dtb_after = dtb + h_grest["token"][0:1, 0:1]
    dxc, ddtr, g_ssd = _ssd_bwd(xc, dtr, dy, states, dtb_after, alog, dskx, di, "ssd_bwd")
    dxbc, g_scw = _conv_s_bwd(xbc, pre_s, dxc, scw, "conv_s_bwd")
    dsegs = [dga, dgs, db, dc, dvv, dz, dxbc]
    pairs = [(s, win_t[a:b]) for s, a, b in zip(dsegs, seg_bounds[:-1], seg_bounds[1:])]
    pairs.append((ddtr.astype(BF16), w_dt))
    gw_in_t = jnp.concatenate([_mm_tn(s, u, "gw_in%d" % i) for i, (s, _) in enumerate(pairs)], axis=0)[:n_in]
    in_parts = gw_in_t.reshape(N_DEV, n_blk, d)
    h_gin = _xchg_start([in_parts], [True], in_parts, "scatter_in_start")
    du = _mm(pairs, "nn", F32, "d_u", tm=512, tn=512, after=h_gin["token"])
    dx, _, g_nmw = _rms_bwd(x2, du, norm_mix_w, dh1, "norm_mix_bwd")

    small_grads = dict(norm_mix_w=g_nmw[0], ssd_conv_b=g_scw[4], dt_bias=g_ssd[0, :nh], a_log=g_ssd[1, :nh],
                       d_skip=g_ssd[2, :nh], ssd_norm_w=g_snw[0], norm_ffn_w=g_nfw[0], ffn_conv_b=g_ffn[3],
                       final_norm_w=g_fnw[0], conv_a_w=g_caw[:3], ssd_conv_w=g_scw[:4], ffn_conv_w=g_ffn[:3])
    small_names = REPL + CONVS
    small_parts = _pack([small_grads[k] for k in small_names], d, 8)
    h_small = _xchg_start([small_parts], [False], small_parts, "gather_small_start")
    rest_recv = _xchg_wait(h_grest, dx, "scatter_rest_wait")
    (in_recv,) = _xchg_wait(h_gin, rest_recv[0], "scatter_in_wait")
    (small_all,) = _xchg_wait(h_small, in_recv, "gather_small_wait")
    small_sum = _sum_parts(small_all, "sum_small_grads")
    small_g = dict(zip(small_names, _unpack(small_sum, [small_grads[k].shape for k in small_names])))

    res = {}

    def update(k, parts):
        outs = _adamw(parts, *(_as_rows(k, src[k]) for src in (wts, mom1, mom2)), "adamw_" + k)
        for kind, a in zip(("g", "d", "m", "v"), outs):
            res[kind, k] = (a.T if k in TRANSPOSED else a)[None]

    update("w_in", in_recv)
    for k, parts in zip(REST, rest_recv):
        update(k, parts)
    local_g = {}
    for k in REPL:
        local_g[k] = small_g[k].reshape(wts[k].shape)
    for k in CONVS:
        n = wts[k].shape[2]
        local_g[k] = lax.dynamic_slice_in_dim(small_g[k], me * n, n, axis=1)[None]
    w_sm, m_sm, v_sm = (_pack([src[k] for k in small_names], d, 8) for src in (wts, mom1, mom2))
    g_sm = _pack([local_g[k] for k in small_names], d, 8)
    outs_sm = _adamw(g_sm[None], w_sm, m_sm, v_sm, "adamw_small")
    for kind, packed in zip(("g", "d", "m", "v"), outs_sm):
        for k, a in zip(small_names, _unpack(packed, [wts[k].shape for k in small_names])):
            res[kind, k] = a

    loss = lax.psum(loss11[0, 0], ("x", "y", "c"))
    return (loss, dx[None], *[res["g", k] for k in ORDER], *[res["d", k] for k in ORDER],
            *[res["m", k] for k in ORDER], *[res["v", k] for k in ORDER])
```

```python
import functools
import math

import jax
import jax.numpy as jnp
from jax import lax
from jax.experimental import pallas as pl
from jax.experimental.pallas import tpu as pltpu

F32 = jnp.float32
BF16 = jnp.bfloat16
EPS = 1e-5
HEAD_DIM = 64
N_GROUPS = 4
D_STATE = 128
CHUNK = 128
DT_LANES = 128
HALO = 16
STRIP = 16
N_DEV = 8
V7X_VMEM_LIMIT = 56 * 1024 * 1024
ADAM_LR, ADAM_B1, ADAM_B2, ADAM_EPS, ADAM_WD, ADAM_STEP = 0.001, 0.9, 0.999, 1e-08, 0.01, 10
HIGHEST = lax.Precision.HIGHEST
MESH = pl.DeviceIdType.MESH


def _pc(body, **kw):
    return pl.pallas_call(body, **kw)


def _params():
    return pltpu.CompilerParams(vmem_limit_bytes=V7X_VMEM_LIMIT)


def _pick(n, cands):
    for c in cands:
        if n % c == 0:
            return c
    return n


def _dot(a, b, ca, cb, prec=None):
    return lax.dot_general(a, b, (((ca,), (cb,)), ((), ())), preferred_element_type=F32, precision=prec)


def _sigmoid(x):
    return 0.5 * jnp.tanh(0.5 * x) + 0.5


def _sds(shape, dtype):
    return jax.ShapeDtypeStruct(shape, dtype)


def _mm(pairs, mode, out_dtype, name, tm=1024, tn=1024, after=None):
    m = pairs[0][0].shape[0]
    n = pairs[0][1].shape[1] if mode == "nn" else pairs[0][1].shape[0]
    tm = min(tm, m)
    tn = _pick(n, (tn, 1408, 512, 256, 128))
    npair = len(pairs)
    cb = 0 if mode == "nn" else 1

    def body(*refs):
        o_ref = refs[-1]
        acc = None
        for p in range(npair):
            part = _dot(refs[2 * p][...], refs[2 * p + 1][...], 1, cb)
            acc = part if acc is None else acc + part
        o_ref[...] = acc.astype(o_ref.dtype)

    in_specs, args = [], []
    for a, b in pairs:
        k = a.shape[1]
        in_specs.append(pl.BlockSpec((tm, k), lambda i, j: (i, 0)))
        if mode == "nn":
            in_specs.append(pl.BlockSpec((k, tn), lambda i, j: (0, j)))
        else:
            in_specs.append(pl.BlockSpec((tn, k), lambda i, j: (j, 0)))
        args += [a, b]
    if after is not None:
        in_specs.append(pl.BlockSpec(memory_space=pl.ANY))
        args.append(after)
    return _pc(body, name=name, grid=(m // tm, n // tn), in_specs=in_specs,
               out_specs=pl.BlockSpec((tm, tn), lambda i, j: (i, j)),
               out_shape=_sds((m, n), out_dtype), compiler_params=_params())(*args)


def _mm_tn(a, b, name, tm=1024):
    m, ka = a.shape
    nb = b.shape[1]
    tm = min(tm, m)
    nm = m // tm
    tk = _pick(ka, (1024, 1408, 512, 256, 128))
    tn = _pick(nb, (1024, 512, 256, 128))

    def body(a_ref, b_ref, o_ref, acc):
        t = pl.program_id(2)

        @pl.when(t == 0)
        def _():
            acc[...] = jnp.zeros_like(acc)
        acc[...] += _dot(a_ref[...], b_ref[...], 0, 0)

        @pl.when(t == nm - 1)
        def _():
            o_ref[...] = acc[...].astype(o_ref.dtype)

    return _pc(body, name=name, grid=(ka // tk, nb // tn, nm),
               in_specs=[pl.BlockSpec((tm, tk), lambda i, j, t: (t, i)),
                         pl.BlockSpec((tm, tn), lambda i, j, t: (t, j))],
               out_specs=pl.BlockSpec((tk, tn), lambda i, j, t: (i, j)),
               out_shape=_sds((ka, nb), BF16), scratch_shapes=[pltpu.VMEM((tk, tn), F32)],
               compiler_params=_params())(a, b)


def _strips(tm, strip=STRIP):
    return [slice(r * strip, (r + 1) * strip) for r in range(tm // strip)]


def _fold8(a):
    out = a[0:8, :]
    for r in range(8, a.shape[0], 8):
        out = out + a[r:r + 8, :]
    return out


def _colsum(a8):
    return jnp.sum(a8, axis=0, keepdims=True)


def _rms_fwd(x, w, name):
    t, d = x.shape
    tm = min(512, t)

    def body(x_ref, w_ref, o_ref):
        wv = w_ref[...]
        for rows in _strips(tm):
            xv = x_ref[rows, :]
            r = lax.rsqrt(jnp.mean(xv * xv, axis=-1, keepdims=True) + EPS)
            o_ref[rows, :] = (xv * r * wv).astype(o_ref.dtype)

    return _pc(body, name=name, grid=(t // tm,),
               in_specs=[pl.BlockSpec((tm, d), lambda i: (i, 0)), pl.BlockSpec((1, d), lambda i: (0, 0))],
               out_specs=pl.BlockSpec((tm, d), lambda i: (i, 0)),
               out_shape=_sds((t, d), BF16), compiler_params=_params())(x, w)


def _resnorm_fwd(x, mo, w, name):
    t, d = x.shape
    tm = min(512, t)

    def body(x_ref, mo_ref, w_ref, h_ref, v_ref):
        wv = w_ref[...]
        for rows in _strips(tm):
            h = x_ref[rows, :] + mo_ref[rows, :]
            r = lax.rsqrt(jnp.mean(h * h, axis=-1, keepdims=True) + EPS)
            h_ref[rows, :] = h
            v_ref[rows, :] = (h * r * wv).astype(v_ref.dtype)

    row = pl.BlockSpec((tm, d), lambda i: (i, 0))
    return _pc(body, name=name, grid=(t // tm,),
               in_specs=[row, row, pl.BlockSpec((1, d), lambda i: (0, 0))],
               out_specs=[row, row], out_shape=[_sds((t, d), F32), _sds((t, d), BF16)],
               compiler_params=_params())(x, mo, w)


def _rms_bwd(h, dy, w, dres, name):
    t, d = h.shape
    tm = min(512, t)

    def body(h_ref, dy_ref, w_ref, dres_ref, dx_ref, dxb_ref, dw_ref):
        @pl.when(pl.program_id(0) == 0)
        def _():
            dw_ref[...] = jnp.zeros_like(dw_ref)
        wv = w_ref[...]
        acc = jnp.zeros((8, d), F32)
        for rows in _strips(tm):
            hv = h_ref[rows, :]
            dyv = dy_ref[rows, :]
            r = lax.rsqrt(jnp.mean(hv * hv, axis=-1, keepdims=True) + EPS)
            n = hv * r
            dn = dyv * wv
            acc = acc + _fold8(dyv * n)
            dx = dres_ref[rows, :] + r * (dn - n * jnp.mean(dn * n, axis=-1, keepdims=True))
            dx_ref[rows, :] = dx
            dxb_ref[rows, :] = dx.astype(BF16)
        dw_ref[0:1, :] += _colsum(acc)

    row = pl.BlockSpec((tm, d), lambda i: (i, 0))
    return _pc(body, name=name, grid=(t // tm,),
               in_specs=[row, row, pl.BlockSpec((1, d), lambda i: (0, 0)), row],
               out_specs=[row, row, pl.BlockSpec((8, d), lambda i: (0, 0))],
               out_shape=[_sds((t, d), F32), _sds((t, d), BF16), _sds((8, d), F32)],
               compiler_params=_params())(h, dy, w, dres)


def _final(h1, dd, tgt, w, name):
    t, d = h1.shape
    tm = min(512, t)
    nt = t // tm

    def body(h1_ref, dd_ref, tgt_ref, w_ref, loss_ref, dh_ref, dhb_ref, dw_ref, acc):
        i = pl.program_id(0)

        @pl.when(i == 0)
        def _():
            dw_ref[...] = jnp.zeros_like(dw_ref)
            acc[...] = jnp.zeros_like(acc)
        wv = w_ref[...]
        sq = jnp.zeros((8, d), F32)
        dw = jnp.zeros((8, d), F32)
        for rows in _strips(tm):
            h = h1_ref[rows, :] + dd_ref[rows, :]
            r = lax.rsqrt(jnp.mean(h * h, axis=-1, keepdims=True) + EPS)
            n = h * r
            e = n * wv - tgt_ref[rows, :]
            sq = sq + _fold8(e * e)
            dout = e * (1.0 / d)
            dn = dout * wv
            dw = dw + _fold8(dout * n)
            dh = r * (dn - n * jnp.mean(dn * n, axis=-1, keepdims=True))
            dh_ref[rows, :] = dh
            dhb_ref[rows, :] = dh.astype(BF16)
        acc[...] += _colsum(sq)
        dw_ref[0:1, :] += _colsum(dw)

        @pl.when(i == nt - 1)
        def _():
            loss_ref[...] = jnp.sum(acc[...], axis=-1, keepdims=True) * (0.5 / d)

    row = pl.BlockSpec((tm, d), lambda i: (i, 0))
    return _pc(body, name=name, grid=(nt,),
               in_specs=[row, row, row, pl.BlockSpec((1, d), lambda i: (0, 0))],
               out_specs=[pl.BlockSpec((1, 1), lambda i: (0, 0)), row, row, pl.BlockSpec((8, d), lambda i: (0, 0))],
               out_shape=[_sds((1, 1), F32), _sds((t, d), F32), _sds((t, d), BF16), _sds((8, d), F32)],
               scratch_shapes=[pltpu.VMEM((1, d), F32)], compiler_params=_params())(h1, dd, tgt, w)


def _tile_specs(t, tm, tc, col0):
    th = tm // HALO
    last = t // HALO - 1
    cur = pl.BlockSpec((tm, tc), lambda j, i: (i, col0 + j))
    prev = pl.BlockSpec((HALO, tc), lambda j, i: (jnp.maximum(i * th - 1, 0), col0 + j))
    nxt = pl.BlockSpec((HALO, tc), lambda j, i: (jnp.minimum((i + 1) * th, last), col0 + j))
    return cur, prev, nxt


def _conv_strip(buf, w, k, rows):
    out = None
    for j in range(k):
        term = w[j:j + 1, :] * buf[pl.ds(HALO - (k - 1) + j + rows.start, STRIP), :]
        out = term if out is None else out + term
    return out


def _conv_backward(dbuf, x_strip, emit, w, acc_ref, k, tm, with_bias):
    tc = dbuf.shape[1]
    accs = [jnp.zeros((8, tc), F32) for _ in range(k + int(with_bias))]
    for rows in _strips(tm):
        xs = x_strip(rows)
        dx = None
        for j in range(k):
            ds = dbuf[pl.ds(rows.start + k - 1 - j, STRIP), :]
            term = w[j:j + 1, :] * ds
            dx = term if dx is None else dx + term
            accs[j] = accs[j] + _fold8(ds * xs)
            if with_bias and j == k - 1:
                accs[k] = accs[k] + _fold8(ds)
        emit(rows, dx)
    for j, a in enumerate(accs):
        acc_ref[j:j + 1, :] += _colsum(a)


def _conv_a_fwd(pa, w, d, name):
    t = pa.shape[0]
    tm, tc = min(1024, t), _pick(d, (512, 256, 128))
    nd = d // tc

    def body(b_ref, c_ref, v_ref, cp_ref, vp_ref, w_ref, o_ref, q_ref, buf):
        keep = (pl.program_id(1) > 0).astype(F32)
        buf[0:HALO, :] = cp_ref[...].astype(F32) * vp_ref[...].astype(F32) * keep
        for rows in _strips(tm):
            buf[HALO + rows.start:HALO + rows.stop, :] = c_ref[rows, :].astype(F32) * v_ref[rows, :].astype(F32)
        wv = w_ref[...]
        for rows in _strips(tm):
            q = _conv_strip(buf, wv, 3, rows)
            q_ref[rows, :] = q.astype(q_ref.dtype)
            o_ref[rows, :] = (b_ref[rows, :].astype(F32) * q).astype(o_ref.dtype)

    b_cur, _, _ = _tile_specs(t, tm, tc, 0)
    c_cur, c_prev, _ = _tile_specs(t, tm, tc, nd)
    v_cur, v_prev, _ = _tile_specs(t, tm, tc, 2 * nd)
    return _pc(body, name=name, grid=(nd, t // tm),
               in_specs=[b_cur, c_cur, v_cur, c_prev, v_prev, pl.BlockSpec((3, tc), lambda j, i: (0, j))],
               out_specs=[pl.BlockSpec((tm, tc), lambda j, i: (i, j))] * 2,
               out_shape=[_sds((t, d), BF16)] * 2,
               scratch_shapes=[pltpu.VMEM((tm + HALO, tc), F32)],
               compiler_params=_params())(pa, pa, pa, pa, pa, w)


def _conv_a_bwd(pa, q, dya, w, d, name):
    t = pa.shape[0]
    tm, tc = min(1024, t), _pick(d, (512, 256, 128))
    nd, nt = d // tc, t // tm

    def body(b_ref, c_ref, v_ref, bn_ref, q_ref, g_ref, gn_ref, w_ref, db_ref, dc_ref, dv_ref, acc_ref, dbuf):
        i = pl.program_id(1)

        @pl.when(i == 0)
        def _():
            acc_ref[...] = jnp.zeros_like(acc_ref)
        for rows in _strips(tm):
            g = g_ref[rows, :].astype(F32)
            dbuf[rows, :] = g * b_ref[rows, :].astype(F32)
            db_ref[rows, :] = (g * q_ref[rows, :].astype(F32)).astype(BF16)
        dbuf[tm:tm + HALO, :] = gn_ref[...].astype(F32) * bn_ref[...].astype(F32) * (i < nt - 1).astype(F32)

        def emit(rows, dp):
            dc_ref[rows, :] = (dp * v_ref[rows, :].astype(F32)).astype(BF16)
            dv_ref[rows, :] = (dp * c_ref[rows, :].astype(F32)).astype(BF16)

        _conv_backward(dbuf, lambda rows: c_ref[rows, :].astype(F32) * v_ref[rows, :].astype(F32), emit,
                       w_ref[...], acc_ref, 3, tm, False)

    b_cur, _, b_next = _tile_specs(t, tm, tc, 0)
    c_cur, _, _ = _tile_specs(t, tm, tc, nd)
    v_cur, _, _ = _tile_specs(t, tm, tc, 2 * nd)
    g_cur, _, g_next = _tile_specs(t, tm, tc, 0)
    out = pl.BlockSpec((tm, tc), lambda j, i: (i, j))
    return _pc(body, name=name, grid=(nd, nt),
               in_specs=[b_cur, c_cur, v_cur, b_next, g_cur, g_cur, g_next,
                         pl.BlockSpec((3, tc), lambda j, i: (0, j))],
               out_specs=[out, out, out, pl.BlockSpec((8, tc), lambda j, i: (0, j))],
               out_shape=[_sds((t, d), BF16)] * 3 + [_sds((8, d), F32)],
               scratch_shapes=[pltpu.VMEM((tm + HALO, tc), F32)],
               compiler_params=_params())(pa, pa, pa, pa, q, dya, dya, w)


def _conv_s_fwd(xbc, w, b, name):
    t, dx = xbc.shape
    tm, tc = min(1024, t), _pick(dx, (512, 256, 128))

    def body(x_ref, xp_ref, w_ref, b_ref, o_ref, pre_ref, buf):
        buf[0:HALO, :] = xp_ref[...].astype(F32) * (pl.program_id(1) > 0).astype(F32)
        for rows in _strips(tm):
            buf[HALO + rows.start:HALO + rows.stop, :] = x_ref[rows, :].astype(F32)
        wv, bv = w_ref[...], b_ref[...]
        for rows in _strips(tm):
            pre = _conv_strip(buf, wv, 4, rows) + bv
            pre_ref[rows, :] = pre.astype(pre_ref.dtype)
            o_ref[rows, :] = (pre * _sigmoid(pre)).astype(o_ref.dtype)

    cur, prev, _ = _tile_specs(t, tm, tc, 0)
    return _pc(body, name=name, grid=(dx // tc, t // tm),
               in_specs=[cur, prev, pl.BlockSpec((4, tc), lambda j, i: (0, j)),
                         pl.BlockSpec((1, tc), lambda j, i: (0, j))],
               out_specs=[pl.BlockSpec((tm, tc), lambda j, i: (i, j))] * 2,
               out_shape=[_sds((t, dx), BF16)] * 2,
               scratch_shapes=[pltpu.VMEM((tm + HALO, tc), F32)],
               compiler_params=_params())(xbc, xbc, w, b)


def _dsilu(pre):
    s = _sigmoid(pre)
    return s * (1.0 + pre * (1.0 - s))


def _conv_s_bwd(xbc, pre, dxc, w, name):
    t, dx = xbc.shape
    tm, tc = min(1024, t), _pick(dx, (512, 256, 128))
    nt = t // tm

    def body(x_ref, p_ref, pn_ref, g_ref, gn_ref, w_ref, dx_ref, acc_ref, dbuf):
        i = pl.program_id(1)

        @pl.when(i == 0)
        def _():
            acc_ref[...] = jnp.zeros_like(acc_ref)
        for rows in _strips(tm):
            dbuf[rows, :] = g_ref[rows, :].astype(F32) * _dsilu(p_ref[rows, :].astype(F32))
        dbuf[tm:tm + HALO, :] = (gn_ref[...].astype(F32) * _dsilu(pn_ref[...].astype(F32))
                                 * (i < nt - 1).astype(F32))

        def emit(rows, d_in):
            dx_ref[rows, :] = d_in.astype(BF16)

        _conv_backward(dbuf, lambda rows: x_ref[rows, :].astype(F32), emit, w_ref[...], acc_ref, 4, tm, True)

    cur, _, nxt = _tile_specs(t, tm, tc, 0)
    return _pc(body, name=name, grid=(dx // tc, nt),
               in_specs=[cur, cur, nxt, cur, nxt, pl.BlockSpec((4, tc), lambda j, i: (0, j))],
               out_specs=[pl.BlockSpec((tm, tc), lambda j, i: (i, j)), pl.BlockSpec((8, tc), lambda j, i: (0, j))],
               out_shape=[_sds((t, dx), BF16), _sds((8, dx), F32)],
               scratch_shapes=[pltpu.VMEM((tm + HALO, tc), F32)],
               compiler_params=_params())(xbc, pre, pre, dxc, dxc, w)


def _ffn_fwd(hv, w, b, f, name):
    t = hv.shape[0]
    tm, tc = min(1024, t), _pick(f, (512, 256, 128))
    nf = f // tc

    def body(h1_ref, h1p_ref, h3_ref, w_ref, b_ref, o_ref, c1_ref, buf):
        buf[0:HALO, :] = h1p_ref[...].astype(F32) * (pl.program_id(1) > 0).astype(F32)
        for rows in _strips(tm):
            buf[HALO + rows.start:HALO + rows.stop, :] = h1_ref[rows, :].astype(F32)
        wv, bv = w_ref[...], b_ref[...]
        for rows in _strips(tm):
            c1 = _conv_strip(buf, wv, 3, rows) + bv
            c1_ref[rows, :] = c1.astype(c1_ref.dtype)
            o_ref[rows, :] = (c1 * _sigmoid(c1) * h3_ref[rows, :].astype(F32)).astype(o_ref.dtype)

    h1_cur, h1_prev, _ = _tile_specs(t, tm, tc, 0)
    h3_cur, _, _ = _tile_specs(t, tm, tc, nf)
    return _pc(body, name=name, grid=(nf, t // tm),
               in_specs=[h1_cur, h1_prev, h3_cur, pl.BlockSpec((3, tc), lambda j, i: (0, j)),
                         pl.BlockSpec((1, tc), lambda j, i: (0, j))],
               out_specs=[pl.BlockSpec((tm, tc), lambda j, i: (i, j))] * 2,
               out_shape=[_sds((t, f), BF16)] * 2,
               scratch_shapes=[pltpu.VMEM((tm + HALO, tc), F32)],
               compiler_params=_params())(hv, hv, hv, w, b)


def _ffn_bwd(hv, c1, dact, w, f, name):
    t = hv.shape[0]
    tm, tc = min(1024, t), _pick(f, (512, 256, 128))
    nf, nt = f // tc, t // tm

    def body(h1_ref, h3_ref, h3n_ref, c_ref, cn_ref, g_ref, gn_ref, w_ref, dh1_ref, dh3_ref, acc_ref, dbuf):
        i = pl.program_id(1)

        @pl.when(i == 0)
        def _():
            acc_ref[...] = jnp.zeros_like(acc_ref)
        for rows in _strips(tm):
            c1v, g = c_ref[rows, :].astype(F32), g_ref[rows, :].astype(F32)
            s1 = _sigmoid(c1v)
            dh3_ref[rows, :] = (g * c1v * s1).astype(BF16)
            dbuf[rows, :] = g * h3_ref[rows, :].astype(F32) * s1 * (1.0 + c1v * (1.0 - s1))
        dbuf[tm:tm + HALO, :] = (gn_ref[...].astype(F32) * h3n_ref[...].astype(F32)
                                 * _dsilu(cn_ref[...].astype(F32)) * (i < nt - 1).astype(F32))

        def emit(rows, d_in):
            dh1_ref[rows, :] = d_in.astype(BF16)

        _conv_backward(dbuf, lambda rows: h1_ref[rows, :].astype(F32), emit, w_ref[...], acc_ref, 3, tm, True)

    h1_cur, _, _ = _tile_specs(t, tm, tc, 0)
    h3_cur, _, h3_next = _tile_specs(t, tm, tc, nf)
    g_cur, _, g_next = _tile_specs(t, tm, tc, 0)
    out = pl.BlockSpec((tm, tc), lambda j, i: (i, j))
    return _pc(body, name=name, grid=(nf, nt),
               in_specs=[h1_cur, h3_cur, h3_next, g_cur, g_next, g_cur, g_next,
                         pl.BlockSpec((3, tc), lambda j, i: (0, j))],
               out_specs=[out, out, pl.BlockSpec((8, tc), lambda j, i: (0, j))],
               out_shape=[_sds((t, f), BF16), _sds((t, f), BF16), _sds((8, f), F32)],
               scratch_shapes=[pltpu.VMEM((tm + HALO, tc), F32)],
               compiler_params=_params())(hv, hv, hv, c1, c1, dact, dact, w)


def _gnorm_fwd(y, z, w, name):
    t, di = y.shape
    gw = di // N_GROUPS
    tm = min(1024, t)

    def body(y_ref, z_ref, w_ref, o_ref):
        wv = w_ref[...]
        for rows in _strips(tm):
            zv = z_ref[rows, :].astype(F32)
            yz = y_ref[rows, :].astype(F32) * zv * _sigmoid(zv)
            r = lax.rsqrt(jnp.mean(yz * yz, axis=-1, keepdims=True) + EPS)
            o_ref[rows, :] = (yz * r * wv).astype(o_ref.dtype)

    blk = pl.BlockSpec((tm, gw), lambda j, i: (i, j))
    return _pc(body, name=name, grid=(N_GROUPS, t // tm),
               in_specs=[blk, blk, pl.BlockSpec((1, gw), lambda j, i: (0, j))],
               out_specs=blk, out_shape=_sds((t, di), BF16), compiler_params=_params())(y, z, w)


def _gnorm_bwd(y, z, dyn, w, name):
    t, di = y.shape
    gw = di // N_GROUPS
    tm = min(1024, t)

    def body(y_ref, z_ref, g_ref, w_ref, dy_ref, dz_ref, dw_ref):
        @pl.when(pl.program_id(1) == 0)
        def _():
            dw_ref[...] = jnp.zeros_like(dw_ref)
        wv = w_ref[...]
        acc = jnp.zeros((8, gw), F32)
        for rows in _strips(tm):
            yv, zv, g = y_ref[rows, :].astype(F32), z_ref[rows, :].astype(F32), g_ref[rows, :].astype(F32)
            s = _sigmoid(zv)
            sz = zv * s
            yz = yv * sz
            r = lax.rsqrt(jnp.mean(yz * yz, axis=-1, keepdims=True) + EPS)
            n = yz * r
            dn = g * wv
            acc = acc + _fold8(g * n)
            dyz = r * (dn - n * jnp.mean(dn * n, axis=-1, keepdims=True))
            dy_ref[rows, :] = (dyz * sz).astype(BF16)
            dz_ref[rows, :] = (dyz * yv * s * (1.0 + zv * (1.0 - s))).astype(BF16)
        dw_ref[0:1, :] += _colsum(acc)

    blk = pl.BlockSpec((tm, gw), lambda j, i: (i, j))
    return _pc(body, name=name, grid=(N_GROUPS, t // tm),
               in_specs=[blk, blk, blk, pl.BlockSpec((1, gw), lambda j, i: (0, j))],
               out_specs=[blk, blk, pl.BlockSpec((8, gw), lambda j, i: (0, j))],
               out_shape=[_sds((t, di), BF16), _sds((t, di), BF16), _sds((8, di), F32)],
               compiler_params=_params())(y, z, dyn, w)


def _merge_fwd(gates, ya, ys, d, name):
    t = ya.shape[0]
    tm, tc = min(1024, t), _pick(d, (512, 256, 128))
    nd = d // tc

    def body(ga_ref, gs_ref, ya_ref, ys_ref, o_ref):
        for rows in _strips(tm):
            o_ref[rows, :] = (_sigmoid(ga_ref[rows, :].astype(F32)) * ya_ref[rows, :].astype(F32)
                              + _sigmoid(gs_ref[rows, :].astype(F32)) * ys_ref[rows, :].astype(F32)
                              ).astype(o_ref.dtype)

    blk = pl.BlockSpec((tm, tc), lambda j, i: (i, j))
    return _pc(body, name=name, grid=(nd, t // tm),
               in_specs=[blk, pl.BlockSpec((tm, tc), lambda j, i: (i, nd + j)), blk, blk],
               out_specs=blk, out_shape=_sds((t, d), BF16), compiler_params=_params())(gates, gates, ya, ys)


def _merge_bwd(dm, gates, ya, ys, d, name):
    t = ya.shape[0]
    tm, tc = min(1024, t), _pick(d, (512, 256, 128))
    nd = d // tc

    def body(dm_ref, ga_ref, gs_ref, ya_ref, ys_ref, dya_ref, dys_ref, dga_ref, dgs_ref):
        for rows in _strips(tm):
            g = dm_ref[rows, :].astype(F32)
            sa, ss = _sigmoid(ga_ref[rows, :].astype(F32)), _sigmoid(gs_ref[rows, :].astype(F32))
            dya_ref[rows, :] = (g * sa).astype(BF16)
            dys_ref[rows, :] = (g * ss).astype(BF16)
            dga_ref[rows, :] = (g * ya_ref[rows, :].astype(F32) * sa * (1.0 - sa)).astype(BF16)
            dgs_ref[rows, :] = (g * ys_ref[rows, :].astype(F32) * ss * (1.0 - ss)).astype(BF16)

    blk = pl.BlockSpec((tm, tc), lambda j, i: (i, j))
    return _pc(body, name=name, grid=(nd, t // tm),
               in_specs=[blk, blk, pl.BlockSpec((tm, tc), lambda j, i: (i, nd + j)), blk, blk],
               out_specs=[blk] * 4, out_shape=[_sds((t, d), BF16)] * 4,
               compiler_params=_params())(dm, gates, gates, ya, ys)


def _ssd_chunk_terms(dtr, dtb, alog):
    xx = dtr + dtb
    dt = jnp.maximum(xx, 0.0) + jnp.log(1.0 + jnp.exp(-jnp.abs(xx)))
    a = -jnp.exp(alog)
    li = lax.broadcasted_iota(jnp.int32, (CHUNK, CHUNK), 0)
    si = lax.broadcasted_iota(jnp.int32, (CHUNK, CHUNK), 1)
    causal = li >= si
    acum = _dot(causal.astype(F32), dt * a, 1, 0, HIGHEST)
    return xx, dt, a, acum, acum.T, causal


def _split2(x):
    hi = x.astype(BF16)
    return hi, (x - hi.astype(F32)).astype(BF16)


def _expand(v, e):
    hi, lo = _split2(v)
    return _dot(hi, e, 1, 0) + _dot(lo, e, 1, 0)


def _segsum(s, e):
    hi, lo = _split2(s)
    return _dot(hi, e, 1, 1) + _dot(lo, e, 1, 1)


def _head_maps(di):
    nh = di // HEAD_DIM
    h = jnp.arange(DT_LANES)[:, None]
    e64 = (jnp.arange(di)[None, :] // HEAD_DIM == h).astype(BF16)
    e128 = (jnp.arange(nh * CHUNK)[None, :] // CHUNK == h).astype(BF16)
    return e64, e128


def _pair_blockdiag(p, left):
    zero = jnp.zeros_like(p)
    return jnp.concatenate([jnp.where(left, p, zero), jnp.where(left, zero, p)], axis=0)


def _ssd_fwd(xc, dtr, dtb, alog, dskx, di, name):
    t = xc.shape[0]
    dx = xc.shape[1]
    nc = t // CHUNK
    nh = di // HEAD_DIM
    hpg = nh // N_GROUPS
    gw = hpg * HEAD_DIM
    boff, coff = di, di + N_GROUPS * D_STATE
    e64, e128 = _head_maps(di)

    def body(xc_ref, dtr_ref, dtb_ref, alog_ref, dsk_ref, e64_ref, e128_ref, y_ref, st_ref, state):
        @pl.when(pl.program_id(0) == 0)
        def _():
            state[...] = jnp.zeros_like(state)
        _, dt, _, acum, acum_t, causal = _ssd_chunk_terms(dtr_ref[...], dtb_ref[...], alog_ref[...])
        last = acum[CHUNK - 1:CHUNK, :]
        e64v = e64_ref[...]
        dtx = _expand(dt, e64v)
        eax = _expand(jnp.exp(acum), e64v)
        dex = _expand(dt * jnp.exp(last - acum), e64v)
        acx = _expand(acum, e128_ref[...])
        st_ref[0] = state[...]
        left = lax.broadcasted_iota(jnp.int32, (CHUNK, 2 * HEAD_DIM), 1) < HEAD_DIM
        for g in range(N_GROUPS):
            gs = slice(g * gw, (g + 1) * gw)
            bg = xc_ref[:, boff + g * D_STATE:boff + (g + 1) * D_STATE]
            cg = xc_ref[:, coff + g * D_STATE:coff + (g + 1) * D_STATE]
            gm = _dot(cg, bg, 1, 1)
            xg = xc_ref[:, gs].astype(F32)
            xdb = (xg * dtx[:, gs]).astype(BF16)
            sin = state[:, gs]
            yo = _dot(cg, sin.astype(BF16), 1, 0) * eax[:, gs]
            for jp in range(hpg // 2):
                h0 = g * hpg + 2 * jp
                ps = slice(jp * 2 * HEAD_DIM, (jp + 1) * 2 * HEAD_DIM)
                ms = []
                for hh in (h0, h0 + 1):
                    seg = acx[:, hh * CHUNK:(hh + 1) * CHUNK] - acum_t[hh:hh + 1, :]
                    ms.append((gm * jnp.exp(jnp.where(causal, seg, -1e30))).astype(BF16))
                yd = _dot(jnp.concatenate(ms, axis=1), _pair_blockdiag(xdb[:, ps], left), 1, 0)
                col = slice(g * gw + jp * 2 * HEAD_DIM, g * gw + (jp + 1) * 2 * HEAD_DIM)
                y_ref[:, col] = (yd + yo[:, ps] + dsk_ref[:, col] * xg[:, ps]).astype(y_ref.dtype)
            xe = (xg * dex[:, gs]).astype(BF16)
            state[:, gs] = eax[CHUNK - 1:CHUNK, gs] * sin + _dot(bg, xe, 0, 0)

    small = pl.BlockSpec((1, DT_LANES), lambda c: (0, 0))
    whole = lambda a: pl.BlockSpec(a.shape, lambda c: (0, 0))
    return _pc(body, name=name, grid=(nc,),
               in_specs=[pl.BlockSpec((CHUNK, dx), lambda c: (c, 0)),
                         pl.BlockSpec((CHUNK, DT_LANES), lambda c: (c, 0)), small, small,
                         whole(dskx), whole(e64), whole(e128)],
               out_specs=[pl.BlockSpec((CHUNK, di), lambda c: (c, 0)),
                          pl.BlockSpec((1, D_STATE, di), lambda c: (c, 0, 0))],
               out_shape=[_sds((t, di), BF16), _sds((nc, D_STATE, di), F32)],
               scratch_shapes=[pltpu.VMEM((D_STATE, di), F32)],
               compiler_params=_params())(xc, dtr, dtb, alog, dskx, e64, e128)


def _ssd_bwd(xc, dtr, dy, states, dtb, alog, dskx, di, name):
    t = xc.shape[0]
    dx = xc.shape[1]
    nc = t // CHUNK
    nh = di // HEAD_DIM
    hpg = nh // N_GROUPS
    gw = hpg * HEAD_DIM
    boff, coff = di, di + N_GROUPS * D_STATE
    e64, e128 = _head_maps(di)

    def body(xc_ref, dtr_ref, dy_ref, st_ref, dtb_ref, alog_ref, dsk_ref, e64_ref, e128_ref,
             dxc_ref, ddtr_ref, sm_ref, dstate, darow):
        @pl.when(pl.program_id(0) == 0)
        def _():
            dstate[...] = jnp.zeros_like(dstate)
            sm_ref[...] = jnp.zeros_like(sm_ref)
        darow[...] = jnp.zeros_like(darow)
        xx, dt, a, acum, acum_t, causal = _ssd_chunk_terms(dtr_ref[...], dtb_ref[...], alog_ref[...])
        last = acum[CHUNK - 1:CHUNK, :]
        e64v = e64_ref[...]
        dtx = _expand(dt, e64v)
        eax = _expand(jnp.exp(acum), e64v)
        eex = _expand(jnp.exp(last - acum), e64v)
        acx = _expand(acum, e128_ref[...])
        left = lax.broadcasted_iota(jnp.int32, (CHUNK, 2 * HEAD_DIM), 1) < HEAD_DIM
        lane = lax.broadcasted_iota(jnp.int32, (CHUNK, DT_LANES), 1)
        sub8 = lax.broadcasted_iota(jnp.int32, (8, gw), 0)
        da_col = jnp.zeros((CHUNK, DT_LANES), F32)
        ddt_col = jnp.zeros((CHUNK, DT_LANES), F32)
        rows = jnp.zeros((8, DT_LANES), F32)
        for g in range(N_GROUPS):
            gs = slice(g * gw, (g + 1) * gw)
            bg = xc_ref[:, boff + g * D_STATE:boff + (g + 1) * D_STATE]
            cg = xc_ref[:, coff + g * D_STATE:coff + (g + 1) * D_STATE]
            gm = _dot(cg, bg, 1, 1)
            e64g = e64v[:, gs]
            xg = xc_ref[:, gs].astype(F32)
            dtg, eag, eeg = dtx[:, gs], eax[:, gs], eex[:, gs]
            xd = xg * dtg
            xdb = xd.astype(BF16)
            dyb = dy_ref[:, gs]
            dyf = dyb.astype(F32)
            sin = st_ref[0, :, gs]
            sinb = sin.astype(BF16)
            ds = dstate[:, gs]
            dsb = ds.astype(BF16)
            bds = _dot(bg, dsb, 1, 0)
            dyeb = (dyf * eag).astype(BF16)
            dcg = _dot(dyeb, sinb, 1, 1)
            dstate[:, gs] = eag[CHUNK - 1:CHUNK, :] * ds + _dot(cg, dyeb, 0, 0)
            yo = _dot(cg, sinb, 1, 0) * eag
            xe = xd * eeg
            dbg = _dot(xe.astype(BF16), dsb, 1, 1)
            wterm = bds * xe
            da_col = da_col + _segsum(dyf * yo - wterm, e64g)
            dg = jnp.zeros((CHUNK, CHUNK), F32)
            dxd_parts = []
            for jp in range(hpg // 2):
                h0 = g * hpg + 2 * jp
                ps = slice(jp * 2 * HEAD_DIM, (jp + 1) * 2 * HEAD_DIM)
                lms, mfs = [], []
                for hh in (h0, h0 + 1):
                    seg = acx[:, hh * CHUNK:(hh + 1) * CHUNK] - acum_t[hh:hh + 1, :]
                    lm = jnp.exp(jnp.where(causal, seg, -1e30))
                    lms.append(lm)
                    mfs.append(gm * lm)
                mstack = jnp.concatenate([m.astype(BF16) for m in mfs], axis=0)
                dyp = dyb[:, ps]
                dxd_parts.append(_dot(mstack, _pair_blockdiag(dyp, left), 0, 0))
                dm2 = _dot(dyp, _pair_blockdiag(xdb[:, ps], left), 1, 1)
                for k, hh in enumerate((h0, h0 + 1)):
                    dm = dm2[:, k * CHUNK:(k + 1) * CHUNK]
                    dg = dg + dm * lms[k]
                    q = dm * mfs[k]
                    da_col = da_col + jnp.where(lane == hh, jnp.sum(q, axis=1, keepdims=True), 0.0)
                    darow[hh:hh + 1, :] = -jnp.sum(q, axis=0, keepdims=True)
            dxd = jnp.concatenate(dxd_parts, axis=1) + bds * eeg
            ddt_col = ddt_col + _segsum(dxd * xg, e64g)
            rsum = (jnp.where(sub8 == 0, jnp.sum(wterm, axis=0, keepdims=True), 0.0)
                    + jnp.where(sub8 == 1, jnp.sum(ds * sin, axis=0, keepdims=True), 0.0)
                    + jnp.where(sub8 == 2, jnp.sum(dyf * xg, axis=0, keepdims=True), 0.0))
            rows = rows + _segsum(rsum, e64g)
            dxc_ref[:, gs] = (dxd * dtg + dsk_ref[:, gs] * dyf).astype(dxc_ref.dtype)
            dgb = dg.astype(BF16)
            dxc_ref[:, boff + g * D_STATE:boff + (g + 1) * D_STATE] = (
                dbg + _dot(dgb, cg, 0, 0)).astype(dxc_ref.dtype)
            dxc_ref[:, coff + g * D_STATE:coff + (g + 1) * D_STATE] = (
                dcg + _dot(dgb, bg, 1, 0)).astype(dxc_ref.dtype)
        at_last = rows[0:1, :] + jnp.exp(last) * rows[1:2, :]
        is_last = lax.broadcasted_iota(jnp.int32, (CHUNK, DT_LANES), 0) == CHUNK - 1
        da = da_col + jnp.where(is_last, at_last, 0.0) + darow[...].T
        li = lax.broadcasted_iota(jnp.int32, (CHUNK, CHUNK), 0)
        si = lax.broadcasted_iota(jnp.int32, (CHUNK, CHUNK), 1)
        dla = _dot((si >= li).astype(F32), da, 1, 0, HIGHEST)
        ddtr = (ddt_col + dla * a) * _sigmoid(xx)
        ddtr_ref[...] = ddtr
        sm_ref[0:1, :] += jnp.sum(ddtr, axis=0, keepdims=True)
        sm_ref[1:2, :] += jnp.sum(dla * dt, axis=0, keepdims=True) * a
        sm_ref[2:3, :] += rows[2:3, :]

    small = pl.BlockSpec((1, DT_LANES), lambda c: (0, 0))
    whole = lambda a: pl.BlockSpec(a.shape, lambda c: (0, 0))
    rev = lambda c: (nc - 1 - c, 0)
    return _pc(body, name=name, grid=(nc,),
               in_specs=[pl.BlockSpec((CHUNK, dx), rev), pl.BlockSpec((CHUNK, DT_LANES), rev),
                         pl.BlockSpec((CHUNK, di), rev),
                         pl.BlockSpec((1, D_STATE, di), lambda c: (nc - 1 - c, 0, 0)), small, small,
                         whole(dskx), whole(e64), whole(e128)],
               out_specs=[pl.BlockSpec((CHUNK, dx), rev), pl.BlockSpec((CHUNK, DT_LANES), rev),
                          pl.BlockSpec((8, DT_LANES), lambda c: (0, 0))],
               out_shape=[_sds((t, dx), BF16), _sds((t, DT_LANES), F32), _sds((8, DT_LANES), F32)],
               scratch_shapes=[pltpu.VMEM((D_STATE, di), F32), pltpu.VMEM((DT_LANES, CHUNK), F32)],
               compiler_params=_params())(xc, dtr, dy, states, dtb, alog, dskx, e64, e128)


def _adamw(parts, w, m, v, name):
    npart, rows, width = parts.shape
    tr, tw = (_pick(rows, (64, 32, 16, 8)), width) if rows % 8 == 0 else (rows, 128)
    c1 = 1.0 - ADAM_B1 ** ADAM_STEP
    c2 = 1.0 - ADAM_B2 ** ADAM_STEP

    row_strips = _strips(tr) if tr % STRIP == 0 else [slice(0, tr)]
    col_chunks = [slice(c, c + 512) for c in range(0, tw, 512)] if tw % 512 == 0 else [slice(0, tw)]

    def body(p_ref, w_ref, m_ref, v_ref, g_ref, d_ref, nm_ref, nv_ref):
        for rows in row_strips:
            for cols in col_chunks:
                g = p_ref[0, rows, cols].astype(F32)
                for p in range(1, npart):
                    g = g + p_ref[p, rows, cols].astype(F32)
                nm = ADAM_B1 * m_ref[rows, cols] + (1.0 - ADAM_B1) * g
                nv = ADAM_B2 * v_ref[rows, cols] + (1.0 - ADAM_B2) * (g * g)
                g_ref[rows, cols] = g
                nm_ref[rows, cols] = nm
                nv_ref[rows, cols] = nv
                d_ref[rows, cols] = -ADAM_LR * ((nm / c1) / (jnp.sqrt(nv / c2) + ADAM_EPS)
                                                + ADAM_WD * w_ref[rows, cols])

    blk = pl.BlockSpec((tr, tw), lambda i, j: (i, j))
    return _pc(body, name=name, grid=(rows // tr, width // tw),
               in_specs=[pl.BlockSpec((npart, tr, tw), lambda i, j: (0, i, j)), blk, blk, blk],
               out_specs=[blk] * 4, out_shape=[_sds((rows, width), F32)] * 4,
               compiler_params=_params())(parts, w, m, v)


def _sum_parts(parts, name, tile=None):
    npart, rows, width = parts.shape
    tile = rows if tile is None else tile

    def body(p_ref, o_ref):
        for rows_ in _strips(tile, 8 if parts.dtype == F32 else STRIP):
            g = p_ref[0, rows_, :].astype(F32)
            for p in range(1, npart):
                g = g + p_ref[p, rows_, :].astype(F32)
            o_ref[rows_, :] = g

    return _pc(body, name=name, grid=(rows // tile,),
               in_specs=[pl.BlockSpec((npart, tile, width), lambda i: (0, i, 0))],
               out_specs=pl.BlockSpec((tile, width), lambda i: (i, 0)),
               out_shape=_sds((rows, width), F32), compiler_params=_params())(parts)


def _flip(k):
    x, y, c = lax.axis_index("x"), lax.axis_index("y"), lax.axis_index("c")
    px = 1 - x if k & 4 else x
    py = 1 - y if k & 2 else y
    pc = 1 - c if k & 1 else c
    return (px, py, pc), 4 * px + 2 * py + pc


DIRECT = tuple((k, 0) for k in range(1, N_DEV))
TO_CHIPS = ((1, 0), (2, 0), (4, 0), (6, 0))
TO_SIBLING = ((1, 2), (1, 4), (1, 6))


def _copies(arrays, lands, send_sems, recv_sems, scatter, moves):
    _, me = _flip(0)
    outgoing, incoming = [], []
    for i, (kd, kb) in enumerate(moves):
        peer, pidx = _flip(kd)
        _, out_slot = _flip(kb)
        _, in_slot = _flip(kd ^ kb)
        for j, land_ref in enumerate(lands):
            if kb:
                src = land_ref.at[out_slot]
            else:
                src = arrays[j].at[pidx] if scatter[j] else arrays[j]
            sem = len(lands) * i + j
            for dst, bucket in ((land_ref.at[out_slot], outgoing), (land_ref.at[in_slot], incoming)):
                bucket.append(pltpu.make_async_remote_copy(
                    src_ref=src, dst_ref=dst, send_sem=send_sems.at[sem], recv_sem=recv_sems.at[sem],
                    device_id=peer, device_id_type=MESH))
    return outgoing, incoming


HBM_SPEC = pl.BlockSpec(memory_space=pltpu.HBM)
SEM_SPEC = pl.BlockSpec(memory_space=pltpu.SEMAPHORE)
ANY_SPEC = pl.BlockSpec(memory_space=pl.ANY)
EFFECT = pltpu.SideEffectType.DATAFLOW_SIDE_EFFECTING


def _landing_zones(arrays, scatter):
    _, me = _flip(0)
    lands = []
    for a, sc in zip(arrays, scatter):
        own = lax.dynamic_index_in_dim(a, me, 0, keepdims=True) if sc else a[None]
        shape = a.shape if sc else (N_DEV,) + a.shape
        lands.append(lax.dynamic_update_slice(lax.empty(shape, a.dtype), own, (me,) + (0,) * (len(shape) - 1)))
    return lands


def _xchg_start(arrays, scatter, after, name, moves=DIRECT, lands=None):
    if lands is None:
        lands = _landing_zones(arrays, scatter)
    na, nl = len(arrays), len(lands)

    def body(*refs):
        ins, outs = refs[:na + nl], refs[na + nl + 1:]
        outgoing, _ = _copies(ins[:na], ins[na:], outs[0], outs[1], scatter, moves)
        for cp in outgoing:
            cp.start()
        outs[-1][...] = jnp.zeros_like(outs[-1])

    nsem = nl * len(moves)
    operands = [pltpu.with_memory_space_constraint(a, pltpu.HBM) for a in list(arrays) + list(lands)]
    out = _pc(body, name=name,
              out_shape=(pltpu.SemaphoreType.DMA((nsem,)), pltpu.SemaphoreType.DMA((nsem,)),
                         *[pltpu.HBM(a.shape, a.dtype) for a in operands], _sds((8, 128), F32)),
              in_specs=[HBM_SPEC] * (na + nl) + [ANY_SPEC],
              out_specs=(SEM_SPEC, SEM_SPEC, *[HBM_SPEC] * (na + nl), pl.BlockSpec(memory_space=pltpu.VMEM)),
              input_output_aliases={i: 2 + i for i in range(na + nl)},
              compiler_params=pltpu.CompilerParams(has_side_effects=EFFECT))(*operands, after)
    return dict(sems=out[:2], thru=out[2:2 + na + nl], token=out[-1], scatter=scatter, na=na, moves=moves)


def _xchg_wait(handle, after, name):
    na, scatter, moves, thru = handle["na"], handle["scatter"], handle["moves"], handle["thru"]
    n = len(thru)

    def body(*refs):
        ins = refs[:n]
        outgoing, incoming = _copies(ins[:na], ins[na:], refs[n], refs[n + 1], scatter, moves)
        for cp in outgoing:
            cp.wait_send()
        for cp in incoming:
            cp.wait_recv()

    out = _pc(body, name=name, out_shape=tuple(pltpu.HBM(a.shape, a.dtype) for a in thru),
              in_specs=[HBM_SPEC] * n + [SEM_SPEC, SEM_SPEC, ANY_SPEC], out_specs=tuple([HBM_SPEC] * n),
              input_output_aliases={i: i for i in range(n)},
              compiler_params=pltpu.CompilerParams(has_side_effects=EFFECT))(*thru, *handle["sems"], after)
    return out[na:]


def _pack(arrs, width, row_mult):
    flat = jnp.concatenate([a.reshape(-1) for a in arrs])
    n = flat.shape[0]
    rows = -(-n // (width * row_mult)) * row_mult
    return jnp.pad(flat, (0, rows * width - n)).reshape(rows, width)


def _unpack(packed, shapes, lead=None):
    out, off = [], 0
    flat = packed.reshape(-1) if lead is None else packed.reshape(lead, -1)
    for s in shapes:
        n = math.prod(s)
        if lead is None:
            out.append(flat[off:off + n].reshape(s))
        else:
            out.append(flat[:, off:off + n].reshape((lead,) + tuple(s)))
        off += n
    return out


def _blocks_to_cols(blocks):
    nb, rows, n = blocks.shape
    return blocks.transpose(1, 0, 2).reshape(rows, nb * n)


def _pad_rows(a, rows):
    return jnp.pad(a, ((0, rows - a.shape[0]), (0, 0)))


def _pad_lanes(a, lanes):
    return jnp.pad(a, ((0, 0), (0, lanes - a.shape[1])))


REST = ("w_a_out", "w_s_out", "w_o", "w_up", "w_down")
TRANSPOSED = ("w_up", "w_in")
CONVS = ("conv_a_w", "ssd_conv_w", "ffn_conv_w")
REPL = ("norm_mix_w", "ssd_conv_b", "dt_bias", "a_log", "d_skip", "ssd_norm_w", "norm_ffn_w", "ffn_conv_b",
        "final_norm_w")
ORDER = ("norm_mix_w", "w_in", "conv_a_w", "w_a_out", "ssd_conv_w", "ssd_conv_b", "dt_bias", "a_log", "d_skip",
         "ssd_norm_w", "w_s_out", "w_o", "norm_ffn_w", "w_up", "ffn_conv_w", "ffn_conv_b", "w_down", "final_norm_w")


def _as_rows(name, block):
    return block[0].T if name in TRANSPOSED else block[0]


def kernel(x, norm_mix_w, w_in, conv_a_w, w_a_out, ssd_conv_w, ssd_conv_b, dt_bias, a_log, d_skip, ssd_norm_w, w_s_out, w_o, norm_ffn_w, w_up, ffn_conv_w, ffn_conv_b, w_down, final_norm_w, loss_target, m_norm_mix_w, m_w_in, m_conv_a_w, m_w_a_out, m_ssd_conv_w, m_ssd_conv_b, m_dt_bias, m_a_log, m_d_skip, m_ssd_norm_w, m_w_s_out, m_w_o, m_norm_ffn_w, m_w_up, m_ffn_conv_w, m_ffn_conv_b, m_w_down, m_final_norm_w, v_norm_mix_w, v_w_in, v_conv_a_w, v_w_a_out, v_ssd_conv_w, v_ssd_conv_b, v_dt_bias, v_a_log, v_d_skip, v_ssd_norm_w, v_w_s_out, v_w_o, v_norm_ffn_w, v_w_up, v_ffn_conv_w, v_ffn_conv_b, v_w_down, v_final_norm_w):
    wts = dict(norm_mix_w=norm_mix_w, w_in=w_in, conv_a_w=conv_a_w, w_a_out=w_a_out, ssd_conv_w=ssd_conv_w,
               ssd_conv_b=ssd_conv_b, dt_bias=dt_bias, a_log=a_log, d_skip=d_skip, ssd_norm_w=ssd_norm_w,
               w_s_out=w_s_out, w_o=w_o, norm_ffn_w=norm_ffn_w, w_up=w_up, ffn_conv_w=ffn_conv_w,
               ffn_conv_b=ffn_conv_b, w_down=w_down, final_norm_w=final_norm_w)
    mom1 = dict(norm_mix_w=m_norm_mix_w, w_in=m_w_in, conv_a_w=m_conv_a_w, w_a_out=m_w_a_out,
                ssd_conv_w=m_ssd_conv_w, ssd_conv_b=m_ssd_conv_b, dt_bias=m_dt_bias, a_log=m_a_log, d_skip=m_d_skip,
                ssd_norm_w=m_ssd_norm_w, w_s_out=m_w_s_out, w_o=m_w_o, norm_ffn_w=m_norm_ffn_w, w_up=m_w_up,
                ffn_conv_w=m_ffn_conv_w, ffn_conv_b=m_ffn_conv_b, w_down=m_w_down, final_norm_w=m_final_norm_w)
    mom2 = dict(norm_mix_w=v_norm_mix_w, w_in=v_w_in, conv_a_w=v_conv_a_w, w_a_out=v_w_a_out,
                ssd_conv_w=v_ssd_conv_w, ssd_conv_b=v_ssd_conv_b, dt_bias=v_dt_bias, a_log=v_a_log, d_skip=v_d_skip,
                ssd_norm_w=v_ssd_norm_w, w_s_out=v_w_s_out, w_o=v_w_o, norm_ffn_w=v_norm_ffn_w, w_up=v_w_up,
                ffn_conv_w=v_ffn_conv_w, ffn_conv_b=v_ffn_conv_b, w_down=v_w_down, final_norm_w=v_final_norm_w)

    t, d = x.shape[1], x.shape[2]
    di = 2 * d
    nh = di // HEAD_DIM
    dxw = di + 2 * N_GROUPS * D_STATE
    f = w_down.shape[1] * N_DEV
    n_in = w_in.shape[2] * N_DEV
    me = 4 * lax.axis_index("x") + 2 * lax.axis_index("y") + lax.axis_index("c")

    rest_local = [_as_rows(k, wts[k]).astype(BF16) for k in REST]
    nrows = [a.shape[0] for a in rest_local]
    n_blk = w_in.shape[2]
    in_local = w_in[0].T.astype(BF16)
    conv_shapes = [wts[k].shape[1:] for k in CONVS]
    conv_local = _pack([wts[k] for k in CONVS], d, 8)
    x2, tgt = x[0], loss_target[0]
    h_in = _xchg_start([in_local, conv_local], [False, False], x2, "gather_in_start", moves=TO_CHIPS)
    u = _rms_fwd(x2, norm_mix_w, "norm_mix")
    part = _xchg_wait(h_in, u, "gather_in_wait")
    h_fwd = _xchg_start([], [False, False], part[0], "gather_in_forward_start", moves=TO_SIBLING, lands=part)
    in_all, conv_all = _xchg_wait(h_fwd, part[1], "gather_in_forward_wait")
    win_t = in_all.reshape(n_in, d)
    h_rest = _xchg_start(rest_local, [False] * len(REST), in_all, "gather_rest_start")
    c_a, c_s, c_f = _unpack(conv_all, conv_shapes, N_DEV)
    caw, scw, fcw = _blocks_to_cols(c_a), _blocks_to_cols(c_s), _blocks_to_cols(c_f)

    o_z, o_x, o_dt = 5 * d, 7 * d, 7 * d + dxw
    seg_bounds = [0, d, 2 * d, 3 * d, 4 * d, o_z, o_x, o_dt]
    w_dt = _pad_rows(win_t[o_dt:], DT_LANES)
    dtb, alog = (_pad_lanes(p[...].reshape(1, nh), DT_LANES) for p in (dt_bias, a_log))
    dskx = jnp.repeat(d_skip.reshape(1, nh), HEAD_DIM, axis=1)

    tok = h_rest["token"]
    gates = _mm([(u, win_t[:2 * d])], "nt", BF16, "proj_gates", after=tok)
    pa = _mm([(u, win_t[2 * d:o_z])], "nt", BF16, "proj_a", after=tok)
    z = _mm([(u, win_t[o_z:o_x])], "nt", BF16, "proj_z", after=tok)
    xbc = _mm([(u, win_t[o_x:o_dt])], "nt", BF16, "proj_xbc", after=tok)
    dtr = _mm([(u, w_dt)], "nt", F32, "proj_dt", after=tok)
    ya_in, q_a = _conv_a_fwd(pa, caw, d, "conv_a")
    xc, pre_s = _conv_s_fwd(xbc, scw, ssd_conv_b, "conv_s")
    y, states = _ssd_fwd(xc, dtr, dtb, alog, dskx, di, "ssd")
    yn = _gnorm_fwd(y, z, ssd_norm_w, "gnorm")
    rest_all = _xchg_wait(h_rest, yn, "gather_rest_wait")
    waout, wsout, wo, wup_t, wdown = (a.reshape(N_DEV * n, d) for a, n in zip(rest_all, nrows))
    y_a = _mm([(ya_in, waout)], "nn", BF16, "a_out")
    y_s = _mm([(yn, wsout)], "nn", BF16, "s_out")
    merged = _merge_fwd(gates, y_a, y_s, d, "merge")
    mo = _mm([(merged, wo)], "nn", F32, "o_proj")
    h1, v = _resnorm_fwd(x2, mo, norm_ffn_w, "norm_ffn")
    hv = _mm([(v, wup_t)], "nt", BF16, "up_proj")
    act, c1 = _ffn_fwd(hv, fcw, ffn_conv_b, f, "ffn_act")
    dd = _mm([(act, wdown)], "nn", F32, "down_proj")
    loss11, dh2, dh2b, g_fnw = _final(h1, dd, tgt, final_norm_w.reshape(1, d), "final")

    dact = _mm([(dh2b, wdown)], "nt", BF16, "d_act")
    gw_down = _mm_tn(act, dh2b, "gw_down")
    dh1f, dh3, g_ffn = _ffn_bwd(hv, c1, dact, fcw, f, "ffn_act_bwd")
    dv = _mm([(dh1f, wup_t[:f]), (dh3, wup_t[f:])], "nn", F32, "d_v")
    gw_up_t = jnp.concatenate([_mm_tn(dh1f, v, "gw_up1"), _mm_tn(dh3, v, "gw_up3")], axis=0)
    dh1, dh1b, g_nfw = _rms_bwd(h1, dv, norm_ffn_w, dh2, "norm_ffn_bwd")
    dmerged = _mm([(dh1b, wo)], "nt", BF16, "d_merged")
    gw_o = _mm_tn(merged, dh1b, "gw_o")
    dya, dys, dga, dgs = _merge_bwd(dmerged, gates, y_a, y_s, d, "merge_bwd")
    dyain = _mm([(dya, waout)], "nt", BF16, "d_ya_in")
    gw_aout = _mm_tn(ya_in, dya, "gw_a_out")
    db, dc, dvv, g_caw = _conv_a_bwd(pa, q_a, dyain, caw, d, "conv_a_bwd")
    dyn = _mm([(dys, wsout)], "nt", BF16, "d_yn")
    gw_sout = _mm_tn(yn, dys, "gw_s_out")
    grads_rest = dict(w_a_out=gw_aout, w_s_out=gw_sout, w_o=gw_o, w_up=gw_up_t, w_down=gw_down)
    rest_parts = [grads_rest[k].reshape(N_DEV, n, d) for k, n in zip(REST, nrows)]
    h_grest = _xchg_start(rest_parts, [True] * len(REST), rest_parts[0], "scatter_rest_start")
    dy, dz, g_snw = _gnorm_bwd(y, z, dyn, ssd_norm_w, "gnorm_bwd")
    dtb_after = dtb + h_grest["token"][0:1, 0:1]
    dxc, ddtr, g_ssd = _ssd_bwd(xc, dtr, dy, states, dtb_after, alog, dskx, di, "ssd_bwd")
    dxbc, g_scw = _conv_s_bwd(xbc, pre_s, dxc, scw, "conv_s_bwd")
    dsegs = [dga, dgs, db, dc, dvv, dz, dxbc]
    pairs = [(s, win_t[a:b]) for s, a, b in zip(dsegs, seg_bounds[:-1], seg_bounds[1:])]
    pairs.append((ddtr.astype(BF16), w_dt))
    gw_in_t = jnp.concatenate([_mm_tn(s, u, "gw_in%d" % i) for i, (s, _) in enumerate(pairs)], axis=0)[:n_in]
    in_parts = gw_in_t.reshape(N_DEV, n_blk, d)
    h_gin = _xchg_start([in_parts], [True], in_parts, "scatter_in_start")
    du = _mm(pairs, "nn", F32, "d_u", tm=512, tn=512, after=h_gin["token"])
    dx, _, g_nmw = _rms_bwd(x2, du, norm_mix_w, dh1, "norm_mix_bwd")

    small_grads = dict(norm_mix_w=g_nmw[0], ssd_conv_b=g_scw[4], dt_bias=g_ssd[0, :nh], a_log=g_ssd[1, :nh],
                       d_skip=g_ssd[2, :nh], ssd_norm_w=g_snw[0], norm_ffn_w=g_nfw[0], ffn_conv_b=g_ffn[3],
                       final_norm_w=g_fnw[0], conv_a_w=g_caw[:3], ssd_conv_w=g_scw[:4], ffn_conv_w=g_ffn[:3])
    small_names = REPL + CONVS
    small_parts = _pack([small_grads[k] for k in small_names] + [loss11], d, 8)
    h_small = _xchg_start([small_parts], [False], small_parts, "gather_small_start")
    rest_recv = _xchg_wait(h_grest, dx, "scatter_rest_wait")
    (in_recv,) = _xchg_wait(h_gin, rest_recv[0], "scatter_in_wait")
    (small_all,) = _xchg_wait(h_small, in_recv, "gather_small_wait")
    small_sum = _sum_parts(small_all, "sum_small_grads")
    *small_list, loss = _unpack(small_sum, [small_grads[k].shape for k in small_names] + [()])
    small_g = dict(zip(small_names, small_list))

    res = {}

    def update(k, parts):
        outs = _adamw(parts, *(_as_rows(k, src[k]) for src in (wts, mom1, mom2)), "adamw_" + k)
        for kind, a in zip(("g", "d", "m", "v"), outs):
            res[kind, k] = (a.T if k in TRANSPOSED else a)[None]

    update("w_in", in_recv)
    for k, parts in zip(REST, rest_recv):
        update(k, parts)
    local_g = {}
    for k in REPL:
        local_g[k] = small_g[k].reshape(wts[k].shape)
    for k in CONVS:
        n = wts[k].shape[2]
        local_g[k] = lax.dynamic_slice_in_dim(small_g[k], me * n, n, axis=1)[None]
    w_sm, m_sm, v_sm = (_pack([src[k] for k in small_names], d, 8) for src in (wts, mom1, mom2))
    g_sm = _pack([local_g[k] for k in small_names], d, 8)
    outs_sm = _adamw(g_sm[None], w_sm, m_sm, v_sm, "adamw_small")
    for kind, packed in zip(("g", "d", "m", "v"), outs_sm):
        for k, a in zip(small_names, _unpack(packed, [wts[k].shape for k in small_names])):
            res[kind, k] = a

    return (loss, dx[None], *[res["g", k] for k in ORDER], *[res["d", k] for k in ORDER],
            *[res["m", k] for k in ORDER], *[res["v", k] for k in ORDER])
```

```python
import functools
import math

import jax
import jax.numpy as jnp
from jax import lax
from jax.experimental import pallas as pl
from jax.experimental.pallas import tpu as pltpu

F32 = jnp.float32
BF16 = jnp.bfloat16
EPS = 1e-5
HEAD_DIM = 64
N_GROUPS = 4
D_STATE = 128
CHUNK = 128
DT_LANES = 128
HALO = 16
STRIP = 16
N_DEV = 8
V7X_VMEM_LIMIT = 56 * 1024 * 1024
ADAM_LR, ADAM_B1, ADAM_B2, ADAM_EPS, ADAM_WD, ADAM_STEP = 0.001, 0.9, 0.999, 1e-08, 0.01, 10
HIGHEST = lax.Precision.HIGHEST
MESH = pl.DeviceIdType.MESH


def _pc(body, **kw):
    return pl.pallas_call(body, **kw)


def _params():
    return pltpu.CompilerParams(vmem_limit_bytes=V7X_VMEM_LIMIT)


def _pick(n, cands):
    for c in cands:
        if n % c == 0:
            return c
    return n


def _dot(a, b, ca, cb, prec=None):
    return lax.dot_general(a, b, (((ca,), (cb,)), ((), ())), preferred_element_type=F32, precision=prec)


def _sigmoid(x):
    return 0.5 * jnp.tanh(0.5 * x) + 0.5


def _sds(shape, dtype):
    return jax.ShapeDtypeStruct(shape, dtype)


def _mm(pairs, mode, out_dtype, name, tm=1024, tn=1024, after=None):
    m = pairs[0][0].shape[0]
    n = pairs[0][1].shape[1] if mode == "nn" else pairs[0][1].shape[0]
    tm = min(tm, m)
    tn = _pick(n, (tn, 1408, 512, 256, 128))
    npair = len(pairs)
    cb = 0 if mode == "nn" else 1

    def body(*refs):
        o_ref = refs[-1]
        acc = None
        for p in range(npair):
            part = _dot(refs[2 * p][...], refs[2 * p + 1][...], 1, cb)
            acc = part if acc is None else acc + part
        o_ref[...] = acc.astype(o_ref.dtype)

    in_specs, args = [], []
    for a, b in pairs:
        k = a.shape[1]
        in_specs.append(pl.BlockSpec((tm, k), lambda i, j: (i, 0)))
        if mode == "nn":
            in_specs.append(pl.BlockSpec((k, tn), lambda i, j: (0, j)))
        else:
            in_specs.append(pl.BlockSpec((tn, k), lambda i, j: (j, 0)))
        args += [a, b]
    if after is not None:
        in_specs.append(pl.BlockSpec(memory_space=pl.ANY))
        args.append(after)
    return _pc(body, name=name, grid=(m // tm, n // tn), in_specs=in_specs,
               out_specs=pl.BlockSpec((tm, tn), lambda i, j: (i, j)),
               out_shape=_sds((m, n), out_dtype), compiler_params=_params())(*args)


def _mm_tn(a, b, name, tm=1024):
    m, ka = a.shape
    nb = b.shape[1]
    tm = min(tm, m)
    nm = m // tm
    tk = _pick(ka, (1024, 1408, 512, 256, 128))
    tn = _pick(nb, (1024, 512, 256, 128))

    def body(a_ref, b_ref, o_ref, acc):
        t = pl.program_id(2)

        @pl.when(t == 0)
        def _():
            acc[...] = jnp.zeros_like(acc)
        acc[...] += _dot(a_ref[...], b_ref[...], 0, 0)

        @pl.when(t == nm - 1)
        def _():
            o_ref[...] = acc[...].astype(o_ref.dtype)

    return _pc(body, name=name, grid=(ka // tk, nb // tn, nm),
               in_specs=[pl.BlockSpec((tm, tk), lambda i, j, t: (t, i)),
                         pl.BlockSpec((tm, tn), lambda i, j, t: (t, j))],
               out_specs=pl.BlockSpec((tk, tn), lambda i, j, t: (i, j)),
               out_shape=_sds((ka, nb), BF16), scratch_shapes=[pltpu.VMEM((tk, tn), F32)],
               compiler_params=_params())(a, b)


def _strips(tm, strip=STRIP):
    return [slice(r * strip, (r + 1) * strip) for r in range(tm // strip)]


def _fold8(a):
    out = a[0:8, :]
    for r in range(8, a.shape[0], 8):
        out = out + a[r:r + 8, :]
    return out


def _colsum(a8):
    return jnp.sum(a8, axis=0, keepdims=True)


def _rms_fwd(x, w, after, name):
    t, d = x.shape
    tm = min(512, t)

    def body(x_ref, w_ref, after_ref, o_ref):
        wv = w_ref[...]
        for rows in _strips(tm):
            xv = x_ref[rows, :]
            r = lax.rsqrt(jnp.mean(xv * xv, axis=-1, keepdims=True) + EPS)
            o_ref[rows, :] = (xv * r * wv).astype(o_ref.dtype)

    return _pc(body, name=name, grid=(t // tm,),
               in_specs=[pl.BlockSpec((tm, d), lambda i: (i, 0)), pl.BlockSpec((1, d), lambda i: (0, 0)),
                         pl.BlockSpec(memory_space=pl.ANY)],
               out_specs=pl.BlockSpec((tm, d), lambda i: (i, 0)),
               out_shape=_sds((t, d), BF16), compiler_params=_params())(x, w, after)


def _resnorm_fwd(x, mo, w, name):
    t, d = x.shape
    tm = min(512, t)

    def body(x_ref, mo_ref, w_ref, h_ref, v_ref):
        wv = w_ref[...]
        for rows in _strips(tm):
            h = x_ref[rows, :] + mo_ref[rows, :]
            r = lax.rsqrt(jnp.mean(h * h, axis=-1, keepdims=True) + EPS)
            h_ref[rows, :] = h
            v_ref[rows, :] = (h * r * wv).astype(v_ref.dtype)

    row = pl.BlockSpec((tm, d), lambda i: (i, 0))
    return _pc(body, name=name, grid=(t // tm,),
               in_specs=[row, row, pl.BlockSpec((1, d), lambda i: (0, 0))],
               out_specs=[row, row], out_shape=[_sds((t, d), F32), _sds((t, d), BF16)],
               compiler_params=_params())(x, mo, w)


def _rms_bwd(h, dy, w, dres, name):
    t, d = h.shape
    tm = min(512, t)

    def body(h_ref, dy_ref, w_ref, dres_ref, dx_ref, dxb_ref, dw_ref):
        @pl.when(pl.program_id(0) == 0)
        def _():
            dw_ref[...] = jnp.zeros_like(dw_ref)
        wv = w_ref[...]
        acc = jnp.zeros((8, d), F32)
        for rows in _strips(tm):
            hv = h_ref[rows, :]
            dyv = dy_ref[rows, :]
            r = lax.rsqrt(jnp.mean(hv * hv, axis=-1, keepdims=True) + EPS)
            n = hv * r
            dn = dyv * wv
            acc = acc + _fold8(dyv * n)
            dx = dres_ref[rows, :] + r * (dn - n * jnp.mean(dn * n, axis=-1, keepdims=True))
            dx_ref[rows, :] = dx
            dxb_ref[rows, :] = dx.astype(BF16)
        dw_ref[0:1, :] += _colsum(acc)

    row = pl.BlockSpec((tm, d), lambda i: (i, 0))
    return _pc(body, name=name, grid=(t // tm,),
               in_specs=[row, row, pl.BlockSpec((1, d), lambda i: (0, 0)), row],
               out_specs=[row, row, pl.BlockSpec((8, d), lambda i: (0, 0))],
               out_shape=[_sds((t, d), F32), _sds((t, d), BF16), _sds((8, d), F32)],
               compiler_params=_params())(h, dy, w, dres)


def _final(h1, dd, tgt, w, name):
    t, d = h1.shape
    tm = min(512, t)
    nt = t // tm

    def body(h1_ref, dd_ref, tgt_ref, w_ref, loss_ref, dh_ref, dhb_ref, dw_ref, acc):
        i = pl.program_id(0)

        @pl.when(i == 0)
        def _():
            dw_ref[...] = jnp.zeros_like(dw_ref)
            acc[...] = jnp.zeros_like(acc)
        wv = w_ref[...]
        sq = jnp.zeros((8, d), F32)
        dw = jnp.zeros((8, d), F32)
        for rows in _strips(tm):
            h = h1_ref[rows, :] + dd_ref[rows, :]
            r = lax.rsqrt(jnp.mean(h * h, axis=-1, keepdims=True) + EPS)
            n = h * r
            e = n * wv - tgt_ref[rows, :]
            sq = sq + _fold8(e * e)
            dout = e * (1.0 / d)
            dn = dout * wv
            dw = dw + _fold8(dout * n)
            dh = r * (dn - n * jnp.mean(dn * n, axis=-1, keepdims=True))
            dh_ref[rows, :] = dh
            dhb_ref[rows, :] = dh.astype(BF16)
        acc[...] += _colsum(sq)
        dw_ref[0:1, :] += _colsum(dw)

        @pl.when(i == nt - 1)
        def _():
            loss_ref[...] = jnp.sum(acc[...], axis=-1, keepdims=True) * (0.5 / d)

    row = pl.BlockSpec((tm, d), lambda i: (i, 0))
    return _pc(body, name=name, grid=(nt,),
               in_specs=[row, row, row, pl.BlockSpec((1, d), lambda i: (0, 0))],
               out_specs=[pl.BlockSpec((1, 1), lambda i: (0, 0)), row, row, pl.BlockSpec((8, d), lambda i: (0, 0))],
               out_shape=[_sds((1, 1), F32), _sds((t, d), F32), _sds((t, d), BF16), _sds((8, d), F32)],
               scratch_shapes=[pltpu.VMEM((1, d), F32)], compiler_params=_params())(h1, dd, tgt, w)


def _tile_specs(t, tm, tc, col0):
    th = tm // HALO
    last = t // HALO - 1
    cur = pl.BlockSpec((tm, tc), lambda j, i: (i, col0 + j))
    prev = pl.BlockSpec((HALO, tc), lambda j, i: (jnp.maximum(i * th - 1, 0), col0 + j))
    nxt = pl.BlockSpec((HALO, tc), lambda j, i: (jnp.minimum((i + 1) * th, last), col0 + j))
    return cur, prev, nxt


def _conv_strip(buf, w, k, rows):
    out = None
    for j in range(k):
        term = w[j:j + 1, :] * buf[pl.ds(HALO - (k - 1) + j + rows.start, STRIP), :]
        out = term if out is None else out + term
    return out


def _conv_backward(dbuf, x_strip, emit, w, acc_ref, k, tm, with_bias):
    tc = dbuf.shape[1]
    accs = [jnp.zeros((8, tc), F32) for _ in range(k + int(with_bias))]
    for rows in _strips(tm):
        xs = x_strip(rows)
        dx = None
        for j in range(k):
            ds = dbuf[pl.ds(rows.start + k - 1 - j, STRIP), :]
            term = w[j:j + 1, :] * ds
            dx = term if dx is None else dx + term
            accs[j] = accs[j] + _fold8(ds * xs)
            if with_bias and j == k - 1:
                accs[k] = accs[k] + _fold8(ds)
        emit(rows, dx)
    for j, a in enumerate(accs):
        acc_ref[j:j + 1, :] += _colsum(a)


def _conv_a_fwd(pa, w, d, name):
    t = pa.shape[0]
    tm, tc = min(1024, t), _pick(d, (512, 256, 128))
    nd = d // tc

    def body(b_ref, c_ref, v_ref, cp_ref, vp_ref, w_ref, o_ref, q_ref, buf):
        keep = (pl.program_id(1) > 0).astype(F32)
        buf[0:HALO, :] = cp_ref[...].astype(F32) * vp_ref[...].astype(F32) * keep
        for rows in _strips(tm):
            buf[HALO + rows.start:HALO + rows.stop, :] = c_ref[rows, :].astype(F32) * v_ref[rows, :].astype(F32)
        wv = w_ref[...]
        for rows in _strips(tm):
            q = _conv_strip(buf, wv, 3, rows)
            q_ref[rows, :] = q.astype(q_ref.dtype)
            o_ref[rows, :] = (b_ref[rows, :].astype(F32) * q).astype(o_ref.dtype)

    b_cur, _, _ = _tile_specs(t, tm, tc, 0)
    c_cur, c_prev, _ = _tile_specs(t, tm, tc, nd)
    v_cur, v_prev, _ = _tile_specs(t, tm, tc, 2 * nd)
    return _pc(body, name=name, grid=(nd, t // tm),
               in_specs=[b_cur, c_cur, v_cur, c_prev, v_prev, pl.BlockSpec((3, tc), lambda j, i: (0, j))],
               out_specs=[pl.BlockSpec((tm, tc), lambda j, i: (i, j))] * 2,
               out_shape=[_sds((t, d), BF16)] * 2,
               scratch_shapes=[pltpu.VMEM((tm + HALO, tc), F32)],
               compiler_params=_params())(pa, pa, pa, pa, pa, w)


def _conv_a_bwd(pa, q, dya, w, d, name):
    t = pa.shape[0]
    tm, tc = min(1024, t), _pick(d, (512, 256, 128))
    nd, nt = d // tc, t // tm

    def body(b_ref, c_ref, v_ref, bn_ref, q_ref, g_ref, gn_ref, w_ref, db_ref, dc_ref, dv_ref, acc_ref, dbuf):
        i = pl.program_id(1)

        @pl.when(i == 0)
        def _():
            acc_ref[...] = jnp.zeros_like(acc_ref)
        for rows in _strips(tm):
            g = g_ref[rows, :].astype(F32)
            dbuf[rows, :] = g * b_ref[rows, :].astype(F32)
            db_ref[rows, :] = (g * q_ref[rows, :].astype(F32)).astype(BF16)
        dbuf[tm:tm + HALO, :] = gn_ref[...].astype(F32) * bn_ref[...].astype(F32) * (i < nt - 1).astype(F32)

        def emit(rows, dp):
            dc_ref[rows, :] = (dp * v_ref[rows, :].astype(F32)).astype(BF16)
            dv_ref[rows, :] = (dp * c_ref[rows, :].astype(F32)).astype(BF16)

        _conv_backward(dbuf, lambda rows: c_ref[rows, :].astype(F32) * v_ref[rows, :].astype(F32), emit,
                       w_ref[...], acc_ref, 3, tm, False)

    b_cur, _, b_next = _tile_specs(t, tm, tc, 0)
    c_cur, _, _ = _tile_specs(t, tm, tc, nd)
    v_cur, _, _ = _tile_specs(t, tm, tc, 2 * nd)
    g_cur, _, g_next = _tile_specs(t, tm, tc, 0)
    out = pl.BlockSpec((tm, tc), lambda j, i: (i, j))
    return _pc(body, name=name, grid=(nd, nt),
               in_specs=[b_cur, c_cur, v_cur, b_next, g_cur, g_cur, g_next,
                         pl.BlockSpec((3, tc), lambda j, i: (0, j))],
               out_specs=[out, out, out, pl.BlockSpec((8, tc), lambda j, i: (0, j))],
               out_shape=[_sds((t, d), BF16)] * 3 + [_sds((8, d), F32)],
               scratch_shapes=[pltpu.VMEM((tm + HALO, tc), F32)],
               compiler_params=_params())(pa, pa, pa, pa, q, dya, dya, w)


def _conv_s_fwd(xbc, w, b, name):
    t, dx = xbc.shape
    tm, tc = min(1024, t), _pick(dx, (512, 256, 128))

    def body(x_ref, xp_ref, w_ref, b_ref, o_ref, pre_ref, buf):
        buf[0:HALO, :] = xp_ref[...].astype(F32) * (pl.program_id(1) > 0).astype(F32)
        for rows in _strips(tm):
            buf[HALO + rows.start:HALO + rows.stop, :] = x_ref[rows, :].astype(F32)
        wv, bv = w_ref[...], b_ref[...]
        for rows in _strips(tm):
            pre = _conv_strip(buf, wv, 4, rows) + bv
            pre_ref[rows, :] = pre.astype(pre_ref.dtype)
            o_ref[rows, :] = (pre * _sigmoid(pre)).astype(o_ref.dtype)

    cur, prev, _ = _tile_specs(t, tm, tc, 0)
    return _pc(body, name=name, grid=(dx // tc, t // tm),
               in_specs=[cur, prev, pl.BlockSpec((4, tc), lambda j, i: (0, j)),
                         pl.BlockSpec((1, tc), lambda j, i: (0, j))],
               out_specs=[pl.BlockSpec((tm, tc), lambda j, i: (i, j))] * 2,
               out_shape=[_sds((t, dx), BF16)] * 2,
               scratch_shapes=[pltpu.VMEM((tm + HALO, tc), F32)],
               compiler_params=_params())(xbc, xbc, w, b)


def _dsilu(pre):
    s = _sigmoid(pre)
    return s * (1.0 + pre * (1.0 - s))


def _conv_s_bwd(xbc, pre, dxc, w, name):
    t, dx = xbc.shape
    tm, tc = min(1024, t), _pick(dx, (512, 256, 128))
    nt = t // tm

    def body(x_ref, p_ref, pn_ref, g_ref, gn_ref, w_ref, dx_ref, acc_ref, dbuf):
        i = pl.program_id(1)

        @pl.when(i == 0)
        def _():
            acc_ref[...] = jnp.zeros_like(acc_ref)
        for rows in _strips(tm):
            dbuf[rows, :] = g_ref[rows, :].astype(F32) * _dsilu(p_ref[rows, :].astype(F32))
        dbuf[tm:tm + HALO, :] = (gn_ref[...].astype(F32) * _dsilu(pn_ref[...].astype(F32))
                                 * (i < nt - 1).astype(F32))

        def emit(rows, d_in):
            dx_ref[rows, :] = d_in.astype(BF16)

        _conv_backward(dbuf, lambda rows: x_ref[rows, :].astype(F32), emit, w_ref[...], acc_ref, 4, tm, True)

    cur, _, nxt = _tile_specs(t, tm, tc, 0)
    return _pc(body, name=name, grid=(dx // tc, nt),
               in_specs=[cur, cur, nxt, cur, nxt, pl.BlockSpec((4, tc), lambda j, i: (0, j))],
               out_specs=[pl.BlockSpec((tm, tc), lambda j, i: (i, j)), pl.BlockSpec((8, tc), lambda j, i: (0, j))],
               out_shape=[_sds((t, dx), BF16), _sds((8, dx), F32)],
               scratch_shapes=[pltpu.VMEM((tm + HALO, tc), F32)],
               compiler_params=_params())(xbc, pre, pre, dxc, dxc, w)


def _ffn_fwd(hv, w, b, f, name):
    t = hv.shape[0]
    tm, tc = min(1024, t), _pick(f, (512, 256, 128))
    nf = f // tc

    def body(h1_ref, h1p_ref, h3_ref, w_ref, b_ref, o_ref, c1_ref, buf):
        buf[0:HALO, :] = h1p_ref[...].astype(F32) * (pl.program_id(1) > 0).astype(F32)
        for rows in _strips(tm):
            buf[HALO + rows.start:HALO + rows.stop, :] = h1_ref[rows, :].astype(F32)
        wv, bv = w_ref[...], b_ref[...]
        for rows in _strips(tm):
            c1 = _conv_strip(buf, wv, 3, rows) + bv
            c1_ref[rows, :] = c1.astype(c1_ref.dtype)
            o_ref[rows, :] = (c1 * _sigmoid(c1) * h3_ref[rows, :].astype(F32)).astype(o_ref.dtype)

    h1_cur, h1_prev, _ = _tile_specs(t, tm, tc, 0)
    h3_cur, _, _ = _tile_specs(t, tm, tc, nf)
    return _pc(body, name=name, grid=(nf, t // tm),
               in_specs=[h1_cur, h1_prev, h3_cur, pl.BlockSpec((3, tc), lambda j, i: (0, j)),
                         pl.BlockSpec((1, tc), lambda j, i: (0, j))],
               out_specs=[pl.BlockSpec((tm, tc), lambda j, i: (i, j))] * 2,
               out_shape=[_sds((t, f), BF16)] * 2,
               scratch_shapes=[pltpu.VMEM((tm + HALO, tc), F32)],
               compiler_params=_params())(hv, hv, hv, w, b)


def _ffn_bwd(hv, c1, dact, w, f, name):
    t = hv.shape[0]
    tm, tc = min(1024, t), _pick(f, (512, 256, 128))
    nf, nt = f // tc, t // tm

    def body(h1_ref, h3_ref, h3n_ref, c_ref, cn_ref, g_ref, gn_ref, w_ref, dh1_ref, dh3_ref, acc_ref, dbuf):
        i = pl.program_id(1)

        @pl.when(i == 0)
        def _():
            acc_ref[...] = jnp.zeros_like(acc_ref)
        for rows in _strips(tm):
            c1v, g = c_ref[rows, :].astype(F32), g_ref[rows, :].astype(F32)
            s1 = _sigmoid(c1v)
            dh3_ref[rows, :] = (g * c1v * s1).astype(BF16)
            dbuf[rows, :] = g * h3_ref[rows, :].astype(F32) * s1 * (1.0 + c1v * (1.0 - s1))
        dbuf[tm:tm + HALO, :] = (gn_ref[...].astype(F32) * h3n_ref[...].astype(F32)
                                 * _dsilu(cn_ref[...].astype(F32)) * (i < nt - 1).astype(F32))

        def emit(rows, d_in):
            dh1_ref[rows, :] = d_in.astype(BF16)

        _conv_backward(dbuf, lambda rows: h1_ref[rows, :].astype(F32), emit, w_ref[...], acc_ref, 3, tm, True)

    h1_cur, _, _ = _tile_specs(t, tm, tc, 0)
    h3_cur, _, h3_next = _tile_specs(t, tm, tc, nf)
    g_cur, _, g_next = _tile_specs(t, tm, tc, 0)
    out = pl.BlockSpec((tm, tc), lambda j, i: (i, j))
    return _pc(body, name=name, grid=(nf, nt),
               in_specs=[h1_cur, h3_cur, h3_next, g_cur, g_next, g_cur, g_next,
                         pl.BlockSpec((3, tc), lambda j, i: (0, j))],
               out_specs=[out, out, pl.BlockSpec((8, tc), lambda j, i: (0, j))],
               out_shape=[_sds((t, f), BF16), _sds((t, f), BF16), _sds((8, f), F32)],
               scratch_shapes=[pltpu.VMEM((tm + HALO, tc), F32)],
               compiler_params=_params())(hv, hv, hv, c1, c1, dact, dact, w)


def _gnorm_fwd(y, z, w, name):
    t, di = y.shape
    gw = di // N_GROUPS
    tm = min(1024, t)

    def body(y_ref, z_ref, w_ref, o_ref):
        wv = w_ref[...]
        for rows in _strips(tm):
            zv = z_ref[rows, :].astype(F32)
            yz = y_ref[rows, :].astype(F32) * zv * _sigmoid(zv)
            r = lax.rsqrt(jnp.mean(yz * yz, axis=-1, keepdims=True) + EPS)
            o_ref[rows, :] = (yz * r * wv).astype(o_ref.dtype)

    blk = pl.BlockSpec((tm, gw), lambda j, i: (i, j))
    return _pc(body, name=name, grid=(N_GROUPS, t // tm),
               in_specs=[blk, blk, pl.BlockSpec((1, gw), lambda j, i: (0, j))],
               out_specs=blk, out_shape=_sds((t, di), BF16), compiler_params=_params())(y, z, w)


def _gnorm_bwd(y, z, dyn, w, name):
    t, di = y.shape
    gw = di // N_GROUPS
    tm = min(1024, t)

    def body(y_ref, z_ref, g_ref, w_ref, dy_ref, dz_ref, dw_ref):
        @pl.when(pl.program_id(1) == 0)
        def _():
            dw_ref[...] = jnp.zeros_like(dw_ref)
        wv = w_ref[...]
        acc = jnp.zeros((8, gw), F32)
        for rows in _strips(tm):
            yv, zv, g = y_ref[rows, :].astype(F32), z_ref[rows, :].astype(F32), g_ref[rows, :].astype(F32)
            s = _sigmoid(zv)
            sz = zv * s
            yz = yv * sz
            r = lax.rsqrt(jnp.mean(yz * yz, axis=-1, keepdims=True) + EPS)
            n = yz * r
            dn = g * wv
            acc = acc + _fold8(g * n)
            dyz = r * (dn - n * jnp.mean(dn * n, axis=-1, keepdims=True))
            dy_ref[rows, :] = (dyz * sz).astype(BF16)
            dz_ref[rows, :] = (dyz * yv * s * (1.0 + zv * (1.0 - s))).astype(BF16)
        dw_ref[0:1, :] += _colsum(acc)

    blk = pl.BlockSpec((tm, gw), lambda j, i: (i, j))
    return _pc(body, name=name, grid=(N_GROUPS, t // tm),
               in_specs=[blk, blk, blk, pl.BlockSpec((1, gw), lambda j, i: (0, j))],
               out_specs=[blk, blk, pl.BlockSpec((8, gw), lambda j, i: (0, j))],
               out_shape=[_sds((t, di), BF16), _sds((t, di), BF16), _sds((8, di), F32)],
               compiler_params=_params())(y, z, dyn, w)


def _merge_fwd(gates, ya, ys, d, name):
    t = ya.shape[0]
    tm, tc = min(1024, t), _pick(d, (512, 256, 128))
    nd = d // tc

    def body(ga_ref, gs_ref, ya_ref, ys_ref, o_ref):
        for rows in _strips(tm):
            o_ref[rows, :] = (_sigmoid(ga_ref[rows, :].astype(F32)) * ya_ref[rows, :].astype(F32)
                              + _sigmoid(gs_ref[rows, :].astype(F32)) * ys_ref[rows, :].astype(F32)
                              ).astype(o_ref.dtype)

    blk = pl.BlockSpec((tm, tc), lambda j, i: (i, j))
    return _pc(body, name=name, grid=(nd, t // tm),
               in_specs=[blk, pl.BlockSpec((tm, tc), lambda j, i: (i, nd + j)), blk, blk],
               out_specs=blk, out_shape=_sds((t, d), BF16), compiler_params=_params())(gates, gates, ya, ys)


def _merge_bwd(dm, gates, ya, ys, d, name):
    t = ya.shape[0]
    tm, tc = min(1024, t), _pick(d, (512, 256, 128))
    nd = d // tc

    def body(dm_ref, ga_ref, gs_ref, ya_ref, ys_ref, dya_ref, dys_ref, dga_ref, dgs_ref):
        for rows in _strips(tm):
            g = dm_ref[rows, :].astype(F32)
            sa, ss = _sigmoid(ga_ref[rows, :].astype(F32)), _sigmoid(gs_ref[rows, :].astype(F32))
            dya_ref[rows, :] = (g * sa).astype(BF16)
            dys_ref[rows, :] = (g * ss).astype(BF16)
            dga_ref[rows, :] = (g * ya_ref[rows, :].astype(F32) * sa * (1.0 - sa)).astype(BF16)
            dgs_ref[rows, :] = (g * ys_ref[rows, :].astype(F32) * ss * (1.0 - ss)).astype(BF16)

    blk = pl.BlockSpec((tm, tc), lambda j, i: (i, j))
    return _pc(body, name=name, grid=(nd, t // tm),
               in_specs=[blk, blk, pl.BlockSpec((tm, tc), lambda j, i: (i, nd + j)), blk, blk],
               out_specs=[blk] * 4, out_shape=[_sds((t, d), BF16)] * 4,
               compiler_params=_params())(dm, gates, gates, ya, ys)


def _ssd_chunk_terms(dtr, dtb, alog):
    xx = dtr + dtb
    dt = jnp.maximum(xx, 0.0) + jnp.log(1.0 + jnp.exp(-jnp.abs(xx)))
    a = -jnp.exp(alog)
    li = lax.broadcasted_iota(jnp.int32, (CHUNK, CHUNK), 0)
    si = lax.broadcasted_iota(jnp.int32, (CHUNK, CHUNK), 1)
    causal = li >= si
    acum = _dot(causal.astype(F32), dt * a, 1, 0, HIGHEST)
    return xx, dt, a, acum, acum.T, causal


def _split2(x):
    hi = x.astype(BF16)
    return hi, (x - hi.astype(F32)).astype(BF16)


def _expand(v, e, exact=True):
    hi, lo = _split2(v)
    out = _dot(hi, e, 1, 0)
    return out + _dot(lo, e, 1, 0) if exact else out


def _segsum(s, e):
    hi, lo = _split2(s)
    return _dot(hi, e, 1, 1) + _dot(lo, e, 1, 1)


def _head_maps(di):
    nh = di // HEAD_DIM
    h = jnp.arange(DT_LANES)[:, None]
    e64 = (jnp.arange(di)[None, :] // HEAD_DIM == h).astype(BF16)
    e128 = (jnp.arange(nh * CHUNK)[None, :] // CHUNK == h).astype(BF16)
    return e64, e128


def _pair_blockdiag(p, left):
    zero = jnp.zeros_like(p)
    return jnp.concatenate([jnp.where(left, p, zero), jnp.where(left, zero, p)], axis=0)


def _ssd_fwd(xc, dtr, dtb, alog, dskx, di, name):
    t = xc.shape[0]
    dx = xc.shape[1]
    nc = t // CHUNK
    nh = di // HEAD_DIM
    hpg = nh // N_GROUPS
    gw = hpg * HEAD_DIM
    boff, coff = di, di + N_GROUPS * D_STATE
    e64, e128 = _head_maps(di)

    def body(xc_ref, dtr_ref, dtb_ref, alog_ref, dsk_ref, e64_ref, e128_ref, y_ref, st_ref, state):
        @pl.when(pl.program_id(0) == 0)
        def _():
            state[...] = jnp.zeros_like(state)
        _, dt, _, acum, acum_t, causal = _ssd_chunk_terms(dtr_ref[...], dtb_ref[...], alog_ref[...])
        last = acum[CHUNK - 1:CHUNK, :]
        e64v = e64_ref[...]
        dtx = _expand(dt, e64v, False)
        eax = _expand(jnp.exp(acum), e64v)
        dex = _expand(dt * jnp.exp(last - acum), e64v, False)
        acx = _expand(acum, e128_ref[...])
        st_ref[0] = state[...]
        left = lax.broadcasted_iota(jnp.int32, (CHUNK, 2 * HEAD_DIM), 1) < HEAD_DIM
        for g in range(N_GROUPS):
            gs = slice(g * gw, (g + 1) * gw)
            bg = xc_ref[:, boff + g * D_STATE:boff + (g + 1) * D_STATE]
            cg = xc_ref[:, coff + g * D_STATE:coff + (g + 1) * D_STATE]
            gm = _dot(cg, bg, 1, 1)
            xg = xc_ref[:, gs].astype(F32)
            xdb = (xg * dtx[:, gs]).astype(BF16)
            sin = state[:, gs]
            yo = _dot(cg, sin.astype(BF16), 1, 0) * eax[:, gs]
            for jp in range(hpg // 2):
                h0 = g * hpg + 2 * jp
                ps = slice(jp * 2 * HEAD_DIM, (jp + 1) * 2 * HEAD_DIM)
                ms = []
                for hh in (h0, h0 + 1):
                    seg = acx[:, hh * CHUNK:(hh + 1) * CHUNK] - acum_t[hh:hh + 1, :]
                    ms.append((gm * jnp.exp(jnp.where(causal, seg, -1e30))).astype(BF16))
                yd = _dot(jnp.concatenate(ms, axis=1), _pair_blockdiag(xdb[:, ps], left), 1, 0)
                col = slice(g * gw + jp * 2 * HEAD_DIM, g * gw + (jp + 1) * 2 * HEAD_DIM)
                y_ref[:, col] = (yd + yo[:, ps] + dsk_ref[:, col] * xg[:, ps]).astype(y_ref.dtype)
            xe = (xg * dex[:, gs]).astype(BF16)
            state[:, gs] = eax[CHUNK - 1:CHUNK, gs] * sin + _dot(bg, xe, 0, 0)

    small = pl.BlockSpec((1, DT_LANES), lambda c: (0, 0))
    whole = lambda a: pl.BlockSpec(a.shape, lambda c: (0, 0))
    return _pc(body, name=name, grid=(nc,),
               in_specs=[pl.BlockSpec((CHUNK, dx), lambda c: (c, 0)),
                         pl.BlockSpec((CHUNK, DT_LANES), lambda c: (c, 0)), small, small,
                         whole(dskx), whole(e64), whole(e128)],
               out_specs=[pl.BlockSpec((CHUNK, di), lambda c: (c, 0)),
                          pl.BlockSpec((1, D_STATE, di), lambda c: (c, 0, 0))],
               out_shape=[_sds((t, di), BF16), _sds((nc, D_STATE, di), F32)],
               scratch_shapes=[pltpu.VMEM((D_STATE, di), F32)],
               compiler_params=_params())(xc, dtr, dtb, alog, dskx, e64, e128)


def _ssd_bwd(xc, dtr, dy, states, dtb, alog, dskx, di, name):
    t = xc.shape[0]
    dx = xc.shape[1]
    nc = t // CHUNK
    nh = di // HEAD_DIM
    hpg = nh // N_GROUPS
    gw = hpg * HEAD_DIM
    boff, coff = di, di + N_GROUPS * D_STATE
    e64, e128 = _head_maps(di)

    def body(xc_ref, dtr_ref, dy_ref, st_ref, dtb_ref, alog_ref, dsk_ref, e64_ref, e128_ref,
             dxc_ref, ddtr_ref, sm_ref, dstate, darow):
        @pl.when(pl.program_id(0) == 0)
        def _():
            dstate[...] = jnp.zeros_like(dstate)
            sm_ref[...] = jnp.zeros_like(sm_ref)
        darow[...] = jnp.zeros_like(darow)
        xx, dt, a, acum, acum_t, causal = _ssd_chunk_terms(dtr_ref[...], dtb_ref[...], alog_ref[...])
        last = acum[CHUNK - 1:CHUNK, :]
        e64v = e64_ref[...]
        dtx = _expand(dt, e64v, False)
        eax = _expand(jnp.exp(acum), e64v)
        eex = _expand(jnp.exp(last - acum), e64v, False)
        acx = _expand(acum, e128_ref[...])
        left = lax.broadcasted_iota(jnp.int32, (CHUNK, 2 * HEAD_DIM), 1) < HEAD_DIM
        lane = lax.broadcasted_iota(jnp.int32, (CHUNK, DT_LANES), 1)
        sub8 = lax.broadcasted_iota(jnp.int32, (8, gw), 0)
        da_col = jnp.zeros((CHUNK, DT_LANES), F32)
        ddt_col = jnp.zeros((CHUNK, DT_LANES), F32)
        rows = jnp.zeros((8, DT_LANES), F32)
        for g in range(N_GROUPS):
            gs = slice(g * gw, (g + 1) * gw)
            bg = xc_ref[:, boff + g * D_STATE:boff + (g + 1) * D_STATE]
            cg = xc_ref[:, coff + g * D_STATE:coff + (g + 1) * D_STATE]
            gm = _dot(cg, bg, 1, 1)
            e64g = e64v[:, gs]
            xg = xc_ref[:, gs].astype(F32)
            dtg, eag, eeg = dtx[:, gs], eax[:, gs], eex[:, gs]
            xd = xg * dtg
            xdb = xd.astype(BF16)
            dyb = dy_ref[:, gs]
            dyf = dyb.astype(F32)
            sin = st_ref[0, :, gs]
            sinb = sin.astype(BF16)
            ds = dstate[:, gs]
            dsb = ds.astype(BF16)
            bds = _dot(bg, dsb, 1, 0)
            dyeb = (dyf * eag).astype(BF16)
            dcg = _dot(dyeb, sinb, 1, 1)
            dstate[:, gs] = eag[CHUNK - 1:CHUNK, :] * ds + _dot(cg, dyeb, 0, 0)
            yo = _dot(cg, sinb, 1, 0) * eag
            xe = xd * eeg
            dbg = _dot(xe.astype(BF16), dsb, 1, 1)
            wterm = bds * xe
            da_col = da_col + _segsum(dyf * yo - wterm, e64g)
            dg = jnp.zeros((CHUNK, CHUNK), F32)
            dxd_parts = []
            for jp in range(hpg // 2):
                h0 = g * hpg + 2 * jp
                ps = slice(jp * 2 * HEAD_DIM, (jp + 1) * 2 * HEAD_DIM)
                lms, mfs = [], []
                for hh in (h0, h0 + 1):
                    seg = acx[:, hh * CHUNK:(hh + 1) * CHUNK] - acum_t[hh:hh + 1, :]
                    lm = jnp.exp(jnp.where(causal, seg, -1e30))
                    lms.append(lm)
                    mfs.append(gm * lm)
                mstack = jnp.concatenate([m.astype(BF16) for m in mfs], axis=0)
                dyp = dyb[:, ps]
                dxd_parts.append(_dot(mstack, _pair_blockdiag(dyp, left), 0, 0))
                dm2 = _dot(dyp, _pair_blockdiag(xdb[:, ps], left), 1, 1)
                for k, hh in enumerate((h0, h0 + 1)):
                    dm = dm2[:, k * CHUNK:(k + 1) * CHUNK]
                    dg = dg + dm * lms[k]
                    q = dm * mfs[k]
                    da_col = da_col + jnp.where(lane == hh, jnp.sum(q, axis=1, keepdims=True), 0.0)
                    darow[hh:hh + 1, :] = -jnp.sum(q, axis=0, keepdims=True)
            dxd = jnp.concatenate(dxd_parts, axis=1) + bds * eeg
            ddt_col = ddt_col + _segsum(dxd * xg, e64g)
            rsum = (jnp.where(sub8 == 0, jnp.sum(wterm, axis=0, keepdims=True), 0.0)
                    + jnp.where(sub8 == 1, jnp.sum(ds * sin, axis=0, keepdims=True), 0.0)
                    + jnp.where(sub8 == 2, jnp.sum(dyf * xg, axis=0, keepdims=True), 0.0))
            rows = rows + _segsum(rsum, e64g)
            dxc_ref[:, gs] = (dxd * dtg + dsk_ref[:, gs] * dyf).astype(dxc_ref.dtype)
            dgb = dg.astype(BF16)
            dxc_ref[:, boff + g * D_STATE:boff + (g + 1) * D_STATE] = (
                dbg + _dot(dgb, cg, 0, 0)).astype(dxc_ref.dtype)
            dxc_ref[:, coff + g * D_STATE:coff + (g + 1) * D_STATE] = (
                dcg + _dot(dgb, bg, 1, 0)).astype(dxc_ref.dtype)
        at_last = rows[0:1, :] + jnp.exp(last) * rows[1:2, :]
        is_last = lax.broadcasted_iota(jnp.int32, (CHUNK, DT_LANES), 0) == CHUNK - 1
        da = da_col + jnp.where(is_last, at_last, 0.0) + darow[...].T
        li = lax.broadcasted_iota(jnp.int32, (CHUNK, CHUNK), 0)
        si = lax.broadcasted_iota(jnp.int32, (CHUNK, CHUNK), 1)
        dla = _dot((si >= li).astype(F32), da, 1, 0, HIGHEST)
        ddtr = (ddt_col + dla * a) * _sigmoid(xx)
        ddtr_ref[...] = ddtr
        sm_ref[0:1, :] += jnp.sum(ddtr, axis=0, keepdims=True)
        sm_ref[1:2, :] += jnp.sum(dla * dt, axis=0, keepdims=True) * a
        sm_ref[2:3, :] += rows[2:3, :]

    small = pl.BlockSpec((1, DT_LANES), lambda c: (0, 0))
    whole = lambda a: pl.BlockSpec(a.shape, lambda c: (0, 0))
    rev = lambda c: (nc - 1 - c, 0)
    return _pc(body, name=name, grid=(nc,),
               in_specs=[pl.BlockSpec((CHUNK, dx), rev), pl.BlockSpec((CHUNK, DT_LANES), rev),
                         pl.BlockSpec((CHUNK, di), rev),
                         pl.BlockSpec((1, D_STATE, di), lambda c: (nc - 1 - c, 0, 0)), small, small,
                         whole(dskx), whole(e64), whole(e128)],
               out_specs=[pl.BlockSpec((CHUNK, dx), rev), pl.BlockSpec((CHUNK, DT_LANES), rev),
                          pl.BlockSpec((8, DT_LANES), lambda c: (0, 0))],
               out_shape=[_sds((t, dx), BF16), _sds((t, DT_LANES), F32), _sds((8, DT_LANES), F32)],
               scratch_shapes=[pltpu.VMEM((D_STATE, di), F32), pltpu.VMEM((DT_LANES, CHUNK), F32)],
               compiler_params=_params())(xc, dtr, dy, states, dtb, alog, dskx, e64, e128)


def _adamw(parts, w, m, v, name):
    npart, rows, width = parts.shape
    tr, tw = (_pick(rows, (64, 32, 16, 8)), width) if rows % 8 == 0 else (rows, 128)
    c1 = 1.0 - ADAM_B1 ** ADAM_STEP
    c2 = 1.0 - ADAM_B2 ** ADAM_STEP

    row_strips = _strips(tr) if tr % STRIP == 0 else [slice(0, tr)]
    col_chunks = [slice(c, c + 512) for c in range(0, tw, 512)] if tw % 512 == 0 else [slice(0, tw)]

    def body(p_ref, w_ref, m_ref, v_ref, g_ref, d_ref, nm_ref, nv_ref):
        for rows in row_strips:
            for cols in col_chunks:
                g = p_ref[0, rows, cols].astype(F32)
                for p in range(1, npart):
                    g = g + p_ref[p, rows, cols].astype(F32)
                nm = ADAM_B1 * m_ref[rows, cols] + (1.0 - ADAM_B1) * g
                nv = ADAM_B2 * v_ref[rows, cols] + (1.0 - ADAM_B2) * (g * g)
                g_ref[rows, cols] = g
                nm_ref[rows, cols] = nm
                nv_ref[rows, cols] = nv
                d_ref[rows, cols] = -ADAM_LR * ((nm / c1) / (jnp.sqrt(nv / c2) + ADAM_EPS)
                                                + ADAM_WD * w_ref[rows, cols])

    blk = pl.BlockSpec((tr, tw), lambda i, j: (i, j))
    return _pc(body, name=name, grid=(rows // tr, width // tw),
               in_specs=[pl.BlockSpec((npart, tr, tw), lambda i, j: (0, i, j)), blk, blk, blk],
               out_specs=[blk] * 4, out_shape=[_sds((rows, width), F32)] * 4,
               compiler_params=_params())(parts, w, m, v)


def _sum_parts(parts, name, tile=None):
    npart, rows, width = parts.shape
    tile = rows if tile is None else tile

    def body(p_ref, o_ref):
        for rows_ in _strips(tile, 8 if parts.dtype == F32 else STRIP):
            g = p_ref[0, rows_, :].astype(F32)
            for p in range(1, npart):
                g = g + p_ref[p, rows_, :].astype(F32)
            o_ref[rows_, :] = g

    return _pc(body, name=name, grid=(rows // tile,),
               in_specs=[pl.BlockSpec((npart, tile, width), lambda i: (0, i, 0))],
               out_specs=pl.BlockSpec((tile, width), lambda i: (i, 0)),
               out_shape=_sds((rows, width), F32), compiler_params=_params())(parts)


def _flip(k):
    x, y, c = lax.axis_index("x"), lax.axis_index("y"), lax.axis_index("c")
    px = 1 - x if k & 4 else x
    py = 1 - y if k & 2 else y
    pc = 1 - c if k & 1 else c
    return (px, py, pc), 4 * px + 2 * py + pc


DIRECT = tuple((k, 0) for k in range(1, N_DEV))
TO_CHIPS = ((1, 0), (2, 0), (4, 0), (6, 0))
TO_SIBLING = ((1, 2), (1, 4), (1, 6))


def _copies(arrays, lands, send_sems, recv_sems, scatter, moves):
    _, me = _flip(0)
    outgoing, incoming = [], []
    for i, (kd, kb) in enumerate(moves):
        peer, pidx = _flip(kd)
        _, out_slot = _flip(kb)
        _, in_slot = _flip(kd ^ kb)
        for j, land_ref in enumerate(lands):
            if kb:
                src = land_ref.at[out_slot]
            else:
                src = arrays[j].at[pidx] if scatter[j] else arrays[j]
            sem = len(lands) * i + j
            for dst, bucket in ((land_ref.at[out_slot], outgoing), (land_ref.at[in_slot], incoming)):
                bucket.append(pltpu.make_async_remote_copy(
                    src_ref=src, dst_ref=dst, send_sem=send_sems.at[sem], recv_sem=recv_sems.at[sem],
                    device_id=peer, device_id_type=MESH))
    return outgoing, incoming


HBM_SPEC = pl.BlockSpec(memory_space=pltpu.HBM)
SEM_SPEC = pl.BlockSpec(memory_space=pltpu.SEMAPHORE)
ANY_SPEC = pl.BlockSpec(memory_space=pl.ANY)
EFFECT = pltpu.SideEffectType.DATAFLOW_SIDE_EFFECTING


def _landing_zones(arrays, scatter):
    _, me = _flip(0)
    lands = []
    for a, sc in zip(arrays, scatter):
        own = lax.dynamic_index_in_dim(a, me, 0, keepdims=True) if sc else a[None]
        shape = a.shape if sc else (N_DEV,) + a.shape
        lands.append(lax.dynamic_update_slice(lax.empty(shape, a.dtype), own, (me,) + (0,) * (len(shape) - 1)))
    return lands


def _xchg_start(arrays, scatter, after, name, moves=DIRECT, lands=None):
    if lands is None:
        lands = _landing_zones(arrays, scatter)
    na, nl = len(arrays), len(lands)

    def body(*refs):
        ins, outs = refs[:na + nl], refs[na + nl + 1:]
        outgoing, _ = _copies(ins[:na], ins[na:], outs[0], outs[1], scatter, moves)
        for cp in outgoing:
            cp.start()
        outs[-1][...] = jnp.zeros_like(outs[-1])

    nsem = nl * len(moves)
    operands = [pltpu.with_memory_space_constraint(a, pltpu.HBM) for a in list(arrays) + list(lands)]
    out = _pc(body, name=name,
              out_shape=(pltpu.SemaphoreType.DMA((nsem,)), pltpu.SemaphoreType.DMA((nsem,)),
                         *[pltpu.HBM(a.shape, a.dtype) for a in operands], _sds((8, 128), F32)),
              in_specs=[HBM_SPEC] * (na + nl) + [ANY_SPEC],
              out_specs=(SEM_SPEC, SEM_SPEC, *[HBM_SPEC] * (na + nl), pl.BlockSpec(memory_space=pltpu.VMEM)),
              input_output_aliases={i: 2 + i for i in range(na + nl)},
              compiler_params=pltpu.CompilerParams(has_side_effects=EFFECT))(*operands, after)
    return dict(sems=out[:2], thru=out[2:2 + na + nl], token=out[-1], scatter=scatter, na=na, moves=moves)


def _xchg_wait(handle, after, name):
    na, scatter, moves, thru = handle["na"], handle["scatter"], handle["moves"], handle["thru"]
    n = len(thru)

    def body(*refs):
        ins = refs[:n]
        outgoing, incoming = _copies(ins[:na], ins[na:], refs[n], refs[n + 1], scatter, moves)
        for cp in outgoing:
            cp.wait_send()
        for cp in incoming:
            cp.wait_recv()

    out = _pc(body, name=name, out_shape=tuple(pltpu.HBM(a.shape, a.dtype) for a in thru),
              in_specs=[HBM_SPEC] * n + [SEM_SPEC, SEM_SPEC, ANY_SPEC], out_specs=tuple([HBM_SPEC] * n),
              input_output_aliases={i: i for i in range(n)},
              compiler_params=pltpu.CompilerParams(has_side_effects=EFFECT))(*thru, *handle["sems"], after)
    return out[na:]


def _pack(arrs, width, row_mult):
    flat = jnp.concatenate([a.reshape(-1) for a in arrs])
    n = flat.shape[0]
    rows = -(-n // (width * row_mult)) * row_mult
    return jnp.pad(flat, (0, rows * width - n)).reshape(rows, width)


def _unpack(packed, shapes, lead=None):
    out, off = [], 0
    flat = packed.reshape(-1) if lead is None else packed.reshape(lead, -1)
    for s in shapes:
        n = math.prod(s)
        if lead is None:
            out.append(flat[off:off + n].reshape(s))
        else:
            out.append(flat[:, off:off + n].reshape((lead,) + tuple(s)))
        off += n
    return out


def _blocks_to_cols(blocks):
    nb, rows, n = blocks.shape
    return blocks.transpose(1, 0, 2).reshape(rows, nb * n)


def _pad_rows(a, rows):
    return jnp.pad(a, ((0, rows - a.shape[0]), (0, 0)))


def _pad_lanes(a, lanes):
    return jnp.pad(a, ((0, 0), (0, lanes - a.shape[1])))


REST = ("w_a_out", "w_s_out", "w_o", "w_up", "w_down")
TRANSPOSED = ("w_up", "w_in")
CONVS = ("conv_a_w", "ssd_conv_w", "ffn_conv_w")
REPL = ("norm_mix_w", "ssd_conv_b", "dt_bias", "a_log", "d_skip", "ssd_norm_w", "norm_ffn_w", "ffn_conv_b",
        "final_norm_w")
ORDER = ("norm_mix_w", "w_in", "conv_a_w", "w_a_out", "ssd_conv_w", "ssd_conv_b", "dt_bias", "a_log", "d_skip",
         "ssd_norm_w", "w_s_out", "w_o", "norm_ffn_w", "w_up", "ffn_conv_w", "ffn_conv_b", "w_down", "final_norm_w")


def _as_rows(name, block):
    return block[0].T if name in TRANSPOSED else block[0]


def kernel(x, norm_mix_w, w_in, conv_a_w, w_a_out, ssd_conv_w, ssd_conv_b, dt_bias, a_log, d_skip, ssd_norm_w, w_s_out, w_o, norm_ffn_w, w_up, ffn_conv_w, ffn_conv_b, w_down, final_norm_w, loss_target, m_norm_mix_w, m_w_in, m_conv_a_w, m_w_a_out, m_ssd_conv_w, m_ssd_conv_b, m_dt_bias, m_a_log, m_d_skip, m_ssd_norm_w, m_w_s_out, m_w_o, m_norm_ffn_w, m_w_up, m_ffn_conv_w, m_ffn_conv_b, m_w_down, m_final_norm_w, v_norm_mix_w, v_w_in, v_conv_a_w, v_w_a_out, v_ssd_conv_w, v_ssd_conv_b, v_dt_bias, v_a_log, v_d_skip, v_ssd_norm_w, v_w_s_out, v_w_o, v_norm_ffn_w, v_w_up, v_ffn_conv_w, v_ffn_conv_b, v_w_down, v_final_norm_w):
    wts = dict(norm_mix_w=norm_mix_w, w_in=w_in, conv_a_w=conv_a_w, w_a_out=w_a_out, ssd_conv_w=ssd_conv_w,
               ssd_conv_b=ssd_conv_b, dt_bias=dt_bias, a_log=a_log, d_skip=d_skip, ssd_norm_w=ssd_norm_w,
               w_s_out=w_s_out, w_o=w_o, norm_ffn_w=norm_ffn_w, w_up=w_up, ffn_conv_w=ffn_conv_w,
               ffn_conv_b=ffn_conv_b, w_down=w_down, final_norm_w=final_norm_w)
    mom1 = dict(norm_mix_w=m_norm_mix_w, w_in=m_w_in, conv_a_w=m_conv_a_w, w_a_out=m_w_a_out,
                ssd_conv_w=m_ssd_conv_w, ssd_conv_b=m_ssd_conv_b, dt_bias=m_dt_bias, a_log=m_a_log, d_skip=m_d_skip,
                ssd_norm_w=m_ssd_norm_w, w_s_out=m_w_s_out, w_o=m_w_o, norm_ffn_w=m_norm_ffn_w, w_up=m_w_up,
                ffn_conv_w=m_ffn_conv_w, ffn_conv_b=m_ffn_conv_b, w_down=m_w_down, final_norm_w=m_final_norm_w)
    mom2 = dict(norm_mix_w=v_norm_mix_w, w_in=v_w_in, conv_a_w=v_conv_a_w, w_a_out=v_w_a_out,
                ssd_conv_w=v_ssd_conv_w, ssd_conv_b=v_ssd_conv_b, dt_bias=v_dt_bias, a_log=v_a_log, d_skip=v_d_skip,
                ssd_norm_w=v_ssd_norm_w, w_s_out=v_w_s_out, w_o=v_w_o, norm_ffn_w=v_norm_ffn_w, w_up=v_w_up,
                ffn_conv_w=v_ffn_conv_w, ffn_conv_b=v_ffn_conv_b, w_down=v_w_down, final_norm_w=v_final_norm_w)

    t, d = x.shape[1], x.shape[2]
    di = 2 * d
    nh = di // HEAD_DIM
    dxw = di + 2 * N_GROUPS * D_STATE
    f = w_down.shape[1] * N_DEV
    n_in = w_in.shape[2] * N_DEV
    me = 4 * lax.axis_index("x") + 2 * lax.axis_index("y") + lax.axis_index("c")

    rest_local = [_as_rows(k, wts[k]).astype(BF16) for k in REST]
    nrows = [a.shape[0] for a in rest_local]
    n_blk = w_in.shape[2]
    in_local = w_in[0].T.astype(BF16)
    conv_shapes = [wts[k].shape[1:] for k in CONVS]
    conv_local = _pack([wts[k] for k in CONVS], d, 8)
    x2, tgt = x[0], loss_target[0]
    h_in = _xchg_start([in_local, conv_local], [False, False], x2, "gather_in_start", moves=TO_CHIPS)
    u = _rms_fwd(x2, norm_mix_w, h_in["token"], "norm_mix")
    part = _xchg_wait(h_in, u, "gather_in_wait")
    h_fwd = _xchg_start([], [False, False], u, "gather_in_forward_start", moves=TO_SIBLING, lands=part)
    in_all, conv_all = _xchg_wait(h_fwd, u, "gather_in_forward_wait")
    win_t = in_all.reshape(n_in, d)
    h_rest = _xchg_start(rest_local, [False] * len(REST), in_all, "gather_rest_start")
    c_a, c_s, c_f = _unpack(conv_all, conv_shapes, N_DEV)
    caw, scw, fcw = _blocks_to_cols(c_a), _blocks_to_cols(c_s), _blocks_to_cols(c_f)

    o_z, o_x, o_dt = 5 * d, 7 * d, 7 * d + dxw
    seg_bounds = [0, d, 2 * d, 3 * d, 4 * d, o_z, o_x, o_dt]
    w_dt = _pad_rows(win_t[o_dt:], DT_LANES)
    dtb, alog = (_pad_lanes(p[...].reshape(1, nh), DT_LANES) for p in (dt_bias, a_log))
    dskx = jnp.repeat(d_skip.reshape(1, nh), HEAD_DIM, axis=1)

    tok = h_rest["token"]
    gates = _mm([(u, win_t[:2 * d])], "nt", BF16, "proj_gates", after=tok)
    pa = _mm([(u, win_t[2 * d:o_z])], "nt", BF16, "proj_a", after=tok)
    z = _mm([(u, win_t[o_z:o_x])], "nt", BF16, "proj_z", after=tok)
    xbc = _mm([(u, win_t[o_x:o_dt])], "nt", BF16, "proj_xbc", after=tok)
    dtr = _mm([(u, w_dt)], "nt", F32, "proj_dt", after=tok)
    ya_in, q_a = _conv_a_fwd(pa, caw, d, "conv_a")
    xc, pre_s = _conv_s_fwd(xbc, scw, ssd_conv_b, "conv_s")
    y, states = _ssd_fwd(xc, dtr, dtb, alog, dskx, di, "ssd")
    yn = _gnorm_fwd(y, z, ssd_norm_w, "gnorm")
    rest_all = _xchg_wait(h_rest, yn, "gather_rest_wait")
    waout, wsout, wo, wup_t, wdown = (a.reshape(N_DEV * n, d) for a, n in zip(rest_all, nrows))
    y_a = _mm([(ya_in, waout)], "nn", BF16, "a_out")
    y_s = _mm([(yn, wsout)], "nn", BF16, "s_out")
    merged = _merge_fwd(gates, y_a, y_s, d, "merge")
    mo = _mm([(merged, wo)], "nn", F32, "o_proj")
    h1, v = _resnorm_fwd(x2, mo, norm_ffn_w, "norm_ffn")
    hv = _mm([(v, wup_t)], "nt", BF16, "up_proj")
    act, c1 = _ffn_fwd(hv, fcw, ffn_conv_b, f, "ffn_act")
    dd = _mm([(act, wdown)], "nn", F32, "down_proj")
    loss11, dh2, dh2b, g_fnw = _final(h1, dd, tgt, final_norm_w.reshape(1, d), "final")

    dact = _mm([(dh2b, wdown)], "nt", BF16, "d_act")
    gw_down = _mm_tn(act, dh2b, "gw_down")
    dh1f, dh3, g_ffn = _ffn_bwd(hv, c1, dact, fcw, f, "ffn_act_bwd")
    dv = _mm([(dh1f, wup_t[:f]), (dh3, wup_t[f:])], "nn", F32, "d_v")
    gw_up_t = jnp.concatenate([_mm_tn(dh1f, v, "gw_up1"), _mm_tn(dh3, v, "gw_up3")], axis=0)
    dh1, dh1b, g_nfw = _rms_bwd(h1, dv, norm_ffn_w, dh2, "norm_ffn_bwd")
    dmerged = _mm([(dh1b, wo)], "nt", BF16, "d_merged")
    gw_o = _mm_tn(merged, dh1b, "gw_o")
    dya, dys, dga, dgs = _merge_bwd(dmerged, gates, y_a, y_s, d, "merge_bwd")
    dyain = _mm([(dya, waout)], "nt", BF16, "d_ya_in")
    gw_aout = _mm_tn(ya_in, dya, "gw_a_out")
    db, dc, dvv, g_caw = _conv_a_bwd(pa, q_a, dyain, caw, d, "conv_a_bwd")
    dyn = _mm([(dys, wsout)], "nt", BF16, "d_yn")
    gw_sout = _mm_tn(yn, dys, "gw_s_out")
    grads_rest = dict(w_a_out=gw_aout, w_s_out=gw_sout, w_o=gw_o, w_up=gw_up_t, w_down=gw_down)
    rest_parts = [grads_rest[k].reshape(N_DEV, n, d) for k, n in zip(REST, nrows)]
    h_grest = _xchg_start(rest_parts, [True] * len(REST), rest_parts[0], "scatter_rest_start")
    dy, dz, g_snw = _gnorm_bwd(y, z, dyn, ssd_norm_w, "gnorm_bwd")
    dtb_after = dtb + h_grest["token"][0:1, 0:1]
    dxc, ddtr, g_ssd = _ssd_bwd(xc, dtr, dy, states, dtb_after, alog, dskx, di, "ssd_bwd")
    dxbc, g_scw = _conv_s_bwd(xbc, pre_s, dxc, scw, "conv_s_bwd")
    dsegs = [dga, dgs, db, dc, dvv, dz, dxbc]
    pairs = [(s, win_t[a:b]) for s, a, b in zip(dsegs, seg_bounds[:-1], seg_bounds[1:])]
    pairs.append((ddtr.astype(BF16), w_dt))
    gw_in_t = jnp.concatenate([_mm_tn(s, u, "gw_in%d" % i) for i, (s, _) in enumerate(pairs)], axis=0)[:n_in]
    in_parts = gw_in_t.reshape(N_DEV, n_blk, d)
    h_gin = _xchg_start([in_parts], [True], in_parts, "scatter_in_start")
    du = _mm(pairs, "nn", F32, "d_u", tm=512, tn=512, after=h_gin["token"])
    dx, _, g_nmw = _rms_bwd(x2, du, norm_mix_w, dh1, "norm_mix_bwd")

    small_grads = dict(norm_mix_w=g_nmw[0], ssd_conv_b=g_scw[4], dt_bias=g_ssd[0, :nh], a_log=g_ssd[1, :nh],
                       d_skip=g_ssd[2, :nh], ssd_norm_w=g_snw[0], norm_ffn_w=g_nfw[0], ffn_conv_b=g_ffn[3],
                       final_norm_w=g_fnw[0], conv_a_w=g_caw[:3], ssd_conv_w=g_scw[:4], ffn_conv_w=g_ffn[:3])
    small_names = REPL + CONVS
    small_parts = _pack([small_grads[k] for k in small_names] + [loss11], d, 8)
    h_small = _xchg_start([small_parts], [False], small_parts, "gather_small_start")
    rest_recv = _xchg_wait(h_grest, dx, "scatter_rest_wait")
    (in_recv,) = _xchg_wait(h_gin, rest_recv[0], "scatter_in_wait")
    (small_all,) = _xchg_wait(h_small, in_recv, "gather_small_wait")
    small_sum = _sum_parts(small_all, "sum_small_grads")
    *small_list, loss = _unpack(small_sum, [small_grads[k].shape for k in small_names] + [()])
    small_g = dict(zip(small_names, small_list))

    res = {}

    def update(k, parts):
        outs = _adamw(parts, *(_as_rows(k, src[k]) for src in (wts, mom1, mom2)), "adamw_" + k)
        for kind, a in zip(("g", "d", "m", "v"), outs):
            res[kind, k] = (a.T if k in TRANSPOSED else a)[None]

    update("w_in", in_recv)
    for k, parts in zip(REST, rest_recv):
        update(k, parts)
    local_g = {}
    for k in REPL:
        local_g[k] = small_g[k].reshape(wts[k].shape)
    for k in CONVS:
        n = wts[k].shape[2]
        local_g[k] = lax.dynamic_slice_in_dim(small_g[k], me * n, n, axis=1)[None]
    w_sm, m_sm, v_sm = (_pack([src[k] for k in small_names], d, 8) for src in (wts, mom1, mom2))
    g_sm = _pack([local_g[k] for k in small_names], d, 8)
    outs_sm = _adamw(g_sm[None], w_sm, m_sm, v_sm, "adamw_small")
    for kind, packed in zip(("g", "d", "m", "v"), outs_sm):
        for k, a in zip(small_names, _unpack(packed, [wts[k].shape for k in small_names])):
            res[kind, k] = a

    return (loss, dx[None], *[res["g", k] for k in ORDER], *[res["d", k] for k in ORDER],
            *[res["m", k] for k in ORDER], *[res["v", k] for k in ORDER])
```

```python
import functools
import math

import jax
import jax.numpy as jnp
from jax import lax
from jax.experimental import pallas as pl
from jax.experimental.pallas import tpu as pltpu

F32 = jnp.float32
BF16 = jnp.bfloat16
EPS = 1e-5
HEAD_DIM = 64
N_GROUPS = 4
D_STATE = 128
CHUNK = 128
DT_LANES = 128
HALO = 16
STRIP = 16
N_DEV = 8
V7X_VMEM_LIMIT = 56 * 1024 * 1024
ADAM_LR, ADAM_B1, ADAM_B2, ADAM_EPS, ADAM_WD, ADAM_STEP = 0.001, 0.9, 0.999, 1e-08, 0.01, 10
HIGHEST = lax.Precision.HIGHEST
MESH = pl.DeviceIdType.MESH


def _pc(body, **kw):
    return pl.pallas_call(body, **kw)


def _params():
    return pltpu.CompilerParams(vmem_limit_bytes=V7X_VMEM_LIMIT)


def _pick(n, cands):
    for c in cands:
        if n % c == 0:
            return c
    return n


def _dot(a, b, ca, cb, prec=None):
    return lax.dot_general(a, b, (((ca,), (cb,)), ((), ())), preferred_element_type=F32, precision=prec)


def _sigmoid(x):
    return 0.5 * jnp.tanh(0.5 * x) + 0.5


def _sds(shape, dtype):
    return jax.ShapeDtypeStruct(shape, dtype)


def _mm(pairs, mode, out_dtype, name, n=None, tm=1024, tn=1024, after=None):
    pairs = [p if len(p) == 5 else (p[0], 0, p[0].shape[1], p[1], 0) for p in pairs]
    m = pairs[0][0].shape[0]
    if n is None:
        n = pairs[0][3].shape[1] if mode == "nn" else pairs[0][3].shape[0]
    tm = min(tm, m)
    rows_nt = [p[4] for p in pairs] if mode == "nt" else []
    tn = next(c for c in (tn, 1408, 512, 256, 128) if n % c == 0 and all(r % c == 0 for r in rows_nt))
    npair = len(pairs)
    cb = 0 if mode == "nn" else 1

    def body(*refs):
        o_ref = refs[-1]
        acc = None
        for p in range(npair):
            part = _dot(refs[2 * p][...], refs[2 * p + 1][...], 1, cb)
            acc = part if acc is None else acc + part
        o_ref[...] = acc.astype(o_ref.dtype)

    in_specs, args = [], []
    for a, a_col, kk, b, b_row in pairs:
        in_specs.append(pl.BlockSpec((tm, kk), lambda i, j, c=a_col: (i, c)))
        if mode == "nn":
            assert b_row % kk == 0
            in_specs.append(pl.BlockSpec((kk, tn), lambda i, j, r=b_row // kk: (r, j)))
        else:
            in_specs.append(pl.BlockSpec((tn, kk), lambda i, j, r=b_row // tn: (r + j, 0)))
        args += [a, b]
    if after is not None:
        in_specs.append(pl.BlockSpec(memory_space=pl.ANY))
        args.append(after)
    return _pc(body, name=name, grid=(m // tm, n // tn), in_specs=in_specs,
               out_specs=pl.BlockSpec((tm, tn), lambda i, j: (i, j)),
               out_shape=_sds((m, n), out_dtype), compiler_params=_params())(*args)


def _mm_tn(a, b, name, tm=1024):
    m, ka = a.shape
    nb = b.shape[1]
    tm = min(tm, m)
    nm = m // tm
    tk = _pick(ka, (1024, 1408, 512, 256, 128))
    tn = _pick(nb, (1024, 512, 256, 128))

    def body(a_ref, b_ref, o_ref, acc):
        t = pl.program_id(2)

        @pl.when(t == 0)
        def _():
            acc[...] = jnp.zeros_like(acc)
        acc[...] += _dot(a_ref[...], b_ref[...], 0, 0)

        @pl.when(t == nm - 1)
        def _():
            o_ref[...] = acc[...].astype(o_ref.dtype)

    return _pc(body, name=name, grid=(ka // tk, nb // tn, nm),
               in_specs=[pl.BlockSpec((tm, tk), lambda i, j, t: (t, i)),
                         pl.BlockSpec((tm, tn), lambda i, j, t: (t, j))],
               out_specs=pl.BlockSpec((tk, tn), lambda i, j, t: (i, j)),
               out_shape=_sds((ka, nb), BF16), scratch_shapes=[pltpu.VMEM((tk, tn), F32)],
               compiler_params=_params())(a, b)


def _strips(tm, strip=STRIP):
    return [slice(r * strip, (r + 1) * strip) for r in range(tm // strip)]


def _fold8(a):
    out = a[0:8, :]
    for r in range(8, a.shape[0], 8):
        out = out + a[r:r + 8, :]
    return out


def _colsum(a8):
    return jnp.sum(a8, axis=0, keepdims=True)


def _rms_fwd(x, w, after, name):
    t, d = x.shape
    tm = min(512, t)

    def body(x_ref, w_ref, after_ref, o_ref):
        wv = w_ref[...]
        for rows in _strips(tm):
            xv = x_ref[rows, :]
            r = lax.rsqrt(jnp.mean(xv * xv, axis=-1, keepdims=True) + EPS)
            o_ref[rows, :] = (xv * r * wv).astype(o_ref.dtype)

    return _pc(body, name=name, grid=(t // tm,),
               in_specs=[pl.BlockSpec((tm, d), lambda i: (i, 0)), pl.BlockSpec((1, d), lambda i: (0, 0)),
                         pl.BlockSpec(memory_space=pl.ANY)],
               out_specs=pl.BlockSpec((tm, d), lambda i: (i, 0)),
               out_shape=_sds((t, d), BF16), compiler_params=_params())(x, w, after)


def _resnorm_fwd(x, mo, w, name):
    t, d = x.shape
    tm = min(512, t)

    def body(x_ref, mo_ref, w_ref, h_ref, v_ref):
        wv = w_ref[...]
        for rows in _strips(tm):
            h = x_ref[rows, :] + mo_ref[rows, :].astype(F32)
            r = lax.rsqrt(jnp.mean(h * h, axis=-1, keepdims=True) + EPS)
            h_ref[rows, :] = h
            v_ref[rows, :] = (h * r * wv).astype(v_ref.dtype)

    row = pl.BlockSpec((tm, d), lambda i: (i, 0))
    return _pc(body, name=name, grid=(t // tm,),
               in_specs=[row, row, pl.BlockSpec((1, d), lambda i: (0, 0))],
               out_specs=[row, row], out_shape=[_sds((t, d), F32), _sds((t, d), BF16)],
               compiler_params=_params())(x, mo, w)


def _rms_bwd(h, dy, w, dres, name):
    t, d = h.shape
    tm = min(512, t)

    def body(h_ref, dy_ref, w_ref, dres_ref, dx_ref, dxb_ref, dw_ref):
        @pl.when(pl.program_id(0) == 0)
        def _():
            dw_ref[...] = jnp.zeros_like(dw_ref)
        wv = w_ref[...]
        acc = jnp.zeros((8, d), F32)
        for rows in _strips(tm):
            hv = h_ref[rows, :]
            dyv = dy_ref[rows, :].astype(F32)
            r = lax.rsqrt(jnp.mean(hv * hv, axis=-1, keepdims=True) + EPS)
            n = hv * r
            dn = dyv * wv
            acc = acc + _fold8(dyv * n)
            dx = dres_ref[rows, :] + r * (dn - n * jnp.mean(dn * n, axis=-1, keepdims=True))
            dx_ref[rows, :] = dx
            dxb_ref[rows, :] = dx.astype(BF16)
        dw_ref[0:1, :] += _colsum(acc)

    row = pl.BlockSpec((tm, d), lambda i: (i, 0))
    return _pc(body, name=name, grid=(t // tm,),
               in_specs=[row, row, pl.BlockSpec((1, d), lambda i: (0, 0)), row],
               out_specs=[row, row, pl.BlockSpec((8, d), lambda i: (0, 0))],
               out_shape=[_sds((t, d), F32), _sds((t, d), BF16), _sds((8, d), F32)],
               compiler_params=_params())(h, dy, w, dres)


def _final(h1, dd, tgt, w, name):
    t, d = h1.shape
    tm = min(512, t)
    nt = t // tm

    def body(h1_ref, dd_ref, tgt_ref, w_ref, loss_ref, dh_ref, dhb_ref, dw_ref, acc):
        i = pl.program_id(0)

        @pl.when(i == 0)
        def _():
            dw_ref[...] = jnp.zeros_like(dw_ref)
            acc[...] = jnp.zeros_like(acc)
        wv = w_ref[...]
        sq = jnp.zeros((8, d), F32)
        dw = jnp.zeros((8, d), F32)
        for rows in _strips(tm):
            h = h1_ref[rows, :] + dd_ref[rows, :].astype(F32)
            r = lax.rsqrt(jnp.mean(h * h, axis=-1, keepdims=True) + EPS)
            n = h * r
            e = n * wv - tgt_ref[rows, :]
            sq = sq + _fold8(e * e)
            dout = e * (1.0 / d)
            dn = dout * wv
            dw = dw + _fold8(dout * n)
            dh = r * (dn - n * jnp.mean(dn * n, axis=-1, keepdims=True))
            dh_ref[rows, :] = dh
            dhb_ref[rows, :] = dh.astype(BF16)
        acc[...] += _colsum(sq)
        dw_ref[0:1, :] += _colsum(dw)

        @pl.when(i == nt - 1)
        def _():
            loss_ref[...] = jnp.sum(acc[...], axis=-1, keepdims=True) * (0.5 / d)

    row = pl.BlockSpec((tm, d), lambda i: (i, 0))
    return _pc(body, name=name, grid=(nt,),
               in_specs=[row, row, row, pl.BlockSpec((1, d), lambda i: (0, 0))],
               out_specs=[pl.BlockSpec((1, 1), lambda i: (0, 0)), row, row, pl.BlockSpec((8, d), lambda i: (0, 0))],
               out_shape=[_sds((1, 1), F32), _sds((t, d), F32), _sds((t, d), BF16), _sds((8, d), F32)],
               scratch_shapes=[pltpu.VMEM((1, d), F32)], compiler_params=_params())(h1, dd, tgt, w)


def _tile_specs(t, tm, tc, col0):
    th = tm // HALO
    last = t // HALO - 1
    cur = pl.BlockSpec((tm, tc), lambda j, i: (i, col0 + j))
    prev = pl.BlockSpec((HALO, tc), lambda j, i: (jnp.maximum(i * th - 1, 0), col0 + j))
    nxt = pl.BlockSpec((HALO, tc), lambda j, i: (jnp.minimum((i + 1) * th, last), col0 + j))
    return cur, prev, nxt


def _conv_strip(buf, w, k, rows):
    out = None
    for j in range(k):
        term = w[j:j + 1, :] * buf[pl.ds(HALO - (k - 1) + j + rows.start, STRIP), :]
        out = term if out is None else out + term
    return out


def _conv_backward(dbuf, x_strip, emit, w, acc_ref, k, tm, with_bias):
    tc = dbuf.shape[1]
    accs = [jnp.zeros((8, tc), F32) for _ in range(k + int(with_bias))]
    for rows in _strips(tm):
        xs = x_strip(rows)
        dx = None
        for j in range(k):
            ds = dbuf[pl.ds(rows.start + k - 1 - j, STRIP), :]
            term = w[j:j + 1, :] * ds
            dx = term if dx is None else dx + term
            accs[j] = accs[j] + _fold8(ds * xs)
            if with_bias and j == k - 1:
                accs[k] = accs[k] + _fold8(ds)
        emit(rows, dx)
    for j, a in enumerate(accs):
        acc_ref[j:j + 1, :] += _colsum(a)


def _conv_a_fwd(pa, w, d, name):
    t = pa.shape[0]
    tm, tc = min(1024, t), _pick(d, (512, 256, 128))
    nd = d // tc

    def body(b_ref, c_ref, v_ref, cp_ref, vp_ref, w_ref, o_ref, q_ref, buf):
        keep = (pl.program_id(1) > 0).astype(F32)
        buf[0:HALO, :] = cp_ref[...].astype(F32) * vp_ref[...].astype(F32) * keep
        for rows in _strips(tm):
            buf[HALO + rows.start:HALO + rows.stop, :] = c_ref[rows, :].astype(F32) * v_ref[rows, :].astype(F32)
        wv = w_ref[...]
        for rows in _strips(tm):
            q = _conv_strip(buf, wv, 3, rows)
            q_ref[rows, :] = q.astype(q_ref.dtype)
            o_ref[rows, :] = (b_ref[rows, :].astype(F32) * q).astype(o_ref.dtype)

    b_cur, _, _ = _tile_specs(t, tm, tc, 0)
    c_cur, c_prev, _ = _tile_specs(t, tm, tc, nd)
    v_cur, v_prev, _ = _tile_specs(t, tm, tc, 2 * nd)
    return _pc(body, name=name, grid=(nd, t // tm),
               in_specs=[b_cur, c_cur, v_cur, c_prev, v_prev, pl.BlockSpec((3, tc), lambda j, i: (0, j))],
               out_specs=[pl.BlockSpec((tm, tc), lambda j, i: (i, j))] * 2,
               out_shape=[_sds((t, d), BF16)] * 2,
               scratch_shapes=[pltpu.VMEM((tm + HALO, tc), F32)],
               compiler_params=_params())(pa, pa, pa, pa, pa, w)


def _conv_a_bwd(pa, q, dya, w, d, name):
    t = pa.shape[0]
    tm, tc = min(1024, t), _pick(d, (512, 256, 128))
    nd, nt = d // tc, t // tm

    def body(b_ref, c_ref, v_ref, bn_ref, q_ref, g_ref, gn_ref, w_ref, db_ref, dc_ref, dv_ref, acc_ref, dbuf):
        i = pl.program_id(1)

        @pl.when(i == 0)
        def _():
            acc_ref[...] = jnp.zeros_like(acc_ref)
        for rows in _strips(tm):
            g = g_ref[rows, :].astype(F32)
            dbuf[rows, :] = g * b_ref[rows, :].astype(F32)
            db_ref[rows, :] = (g * q_ref[rows, :].astype(F32)).astype(BF16)
        dbuf[tm:tm + HALO, :] = gn_ref[...].astype(F32) * bn_ref[...].astype(F32) * (i < nt - 1).astype(F32)

        def emit(rows, dp):
            dc_ref[rows, :] = (dp * v_ref[rows, :].astype(F32)).astype(BF16)
            dv_ref[rows, :] = (dp * c_ref[rows, :].astype(F32)).astype(BF16)

        _conv_backward(dbuf, lambda rows: c_ref[rows, :].astype(F32) * v_ref[rows, :].astype(F32), emit,
                       w_ref[...], acc_ref, 3, tm, False)

    b_cur, _, b_next = _tile_specs(t, tm, tc, 0)
    c_cur, _, _ = _tile_specs(t, tm, tc, nd)
    v_cur, _, _ = _tile_specs(t, tm, tc, 2 * nd)
    g_cur, _, g_next = _tile_specs(t, tm, tc, 0)
    out = pl.BlockSpec((tm, tc), lambda j, i: (i, j))
    return _pc(body, name=name, grid=(nd, nt),
               in_specs=[b_cur, c_cur, v_cur, b_next, g_cur, g_cur, g_next,
                         pl.BlockSpec((3, tc), lambda j, i: (0, j))],
               out_specs=[out, out, out, pl.BlockSpec((8, tc), lambda j, i: (0, j))],
               out_shape=[_sds((t, d), BF16)] * 3 + [_sds((8, d), F32)],
               scratch_shapes=[pltpu.VMEM((tm + HALO, tc), F32)],
               compiler_params=_params())(pa, pa, pa, pa, q, dya, dya, w)


def _conv_s_fwd(xbc, w, b, name):
    t, dx = xbc.shape
    tm, tc = min(1024, t), _pick(dx, (512, 256, 128))

    def body(x_ref, xp_ref, w_ref, b_ref, o_ref, pre_ref, buf):
        buf[0:HALO, :] = xp_ref[...].astype(F32) * (pl.program_id(1) > 0).astype(F32)
        for rows in _strips(tm):
            buf[HALO + rows.start:HALO + rows.stop, :] = x_ref[rows, :].astype(F32)
        wv, bv = w_ref[...], b_ref[...]
        for rows in _strips(tm):
            pre = _conv_strip(buf, wv, 4, rows) + bv
            pre_ref[rows, :] = pre.astype(pre_ref.dtype)
            o_ref[rows, :] = (pre * _sigmoid(pre)).astype(o_ref.dtype)

    cur, prev, _ = _tile_specs(t, tm, tc, 0)
    return _pc(body, name=name, grid=(dx // tc, t // tm),
               in_specs=[cur, prev, pl.BlockSpec((4, tc), lambda j, i: (0, j)),
                         pl.BlockSpec((1, tc), lambda j, i: (0, j))],
               out_specs=[pl.BlockSpec((tm, tc), lambda j, i: (i, j))] * 2,
               out_shape=[_sds((t, dx), BF16)] * 2,
               scratch_shapes=[pltpu.VMEM((tm + HALO, tc), F32)],
               compiler_params=_params())(xbc, xbc, w, b)


def _dsilu(pre):
    s = _sigmoid(pre)
    return s * (1.0 + pre * (1.0 - s))


def _conv_s_bwd(xbc, pre, dxc, w, name):
    t, dx = xbc.shape
    tm, tc = min(1024, t), _pick(dx, (512, 256, 128))
    nt = t // tm

    def body(x_ref, p_ref, pn_ref, g_ref, gn_ref, w_ref, dx_ref, acc_ref, dbuf):
        i = pl.program_id(1)

        @pl.when(i == 0)
        def _():
            acc_ref[...] = jnp.zeros_like(acc_ref)
        for rows in _strips(tm):
            dbuf[rows, :] = g_ref[rows, :].astype(F32) * _dsilu(p_ref[rows, :].astype(F32))
        dbuf[tm:tm + HALO, :] = (gn_ref[...].astype(F32) * _dsilu(pn_ref[...].astype(F32))
                                 * (i < nt - 1).astype(F32))

        def emit(rows, d_in):
            dx_ref[rows, :] = d_in.astype(BF16)

        _conv_backward(dbuf, lambda rows: x_ref[rows, :].astype(F32), emit, w_ref[...], acc_ref, 4, tm, True)

    cur, _, nxt = _tile_specs(t, tm, tc, 0)
    return _pc(body, name=name, grid=(dx // tc, nt),
               in_specs=[cur, cur, nxt, cur, nxt, pl.BlockSpec((4, tc), lambda j, i: (0, j))],
               out_specs=[pl.BlockSpec((tm, tc), lambda j, i: (i, j)), pl.BlockSpec((8, tc), lambda j, i: (0, j))],
               out_shape=[_sds((t, dx), BF16), _sds((8, dx), F32)],
               scratch_shapes=[pltpu.VMEM((tm + HALO, tc), F32)],
               compiler_params=_params())(xbc, pre, pre, dxc, dxc, w)


def _ffn_fwd(hv, w, b, f, name):
    t = hv.shape[0]
    tm, tc = min(1024, t), _pick(f, (512, 256, 128))
    nf = f // tc

    def body(h1_ref, h1p_ref, h3_ref, w_ref, b_ref, o_ref, c1_ref, buf):
        buf[0:HALO, :] = h1p_ref[...].astype(F32) * (pl.program_id(1) > 0).astype(F32)
        for rows in _strips(tm):
            buf[HALO + rows.start:HALO + rows.stop, :] = h1_ref[rows, :].astype(F32)
        wv, bv = w_ref[...], b_ref[...]
        for rows in _strips(tm):
            c1 = _conv_strip(buf, wv, 3, rows) + bv
            c1_ref[rows, :] = c1.astype(c1_ref.dtype)
            o_ref[rows, :] = (c1 * _sigmoid(c1) * h3_ref[rows, :].astype(F32)).astype(o_ref.dtype)

    h1_cur, h1_prev, _ = _tile_specs(t, tm, tc, 0)
    h3_cur, _, _ = _tile_specs(t, tm, tc, nf)
    return _pc(body, name=name, grid=(nf, t // tm),
               in_specs=[h1_cur, h1_prev, h3_cur, pl.BlockSpec((3, tc), lambda j, i: (0, j)),
                         pl.BlockSpec((1, tc), lambda j, i: (0, j))],
               out_specs=[pl.BlockSpec((tm, tc), lambda j, i: (i, j))] * 2,
               out_shape=[_sds((t, f), BF16)] * 2,
               scratch_shapes=[pltpu.VMEM((tm + HALO, tc), F32)],
               compiler_params=_params())(hv, hv, hv, w, b)


def _ffn_bwd(hv, c1, dact, w, f, name):
    t = hv.shape[0]
    tm, tc = min(1024, t), _pick(f, (512, 256, 128))
    nf, nt = f // tc, t // tm

    def body(h1_ref, h3_ref, h3n_ref, c_ref, cn_ref, g_ref, gn_ref, w_ref, dh1_ref, dh3_ref, acc_ref, dbuf):
        i = pl.program_id(1)

        @pl.when(i == 0)
        def _():
            acc_ref[...] = jnp.zeros_like(acc_ref)
        for rows in _strips(tm):
            c1v, g = c_ref[rows, :].astype(F32), g_ref[rows, :].astype(F32)
            s1 = _sigmoid(c1v)
            dh3_ref[rows, :] = (g * c1v * s1).astype(BF16)
            dbuf[rows, :] = g * h3_ref[rows, :].astype(F32) * s1 * (1.0 + c1v * (1.0 - s1))
        dbuf[tm:tm + HALO, :] = (gn_ref[...].astype(F32) * h3n_ref[...].astype(F32)
                                 * _dsilu(cn_ref[...].astype(F32)) * (i < nt - 1).astype(F32))

        def emit(rows, d_in):
            dh1_ref[rows, :] = d_in.astype(BF16)

        _conv_backward(dbuf, lambda rows: h1_ref[rows, :].astype(F32), emit, w_ref[...], acc_ref, 3, tm, True)

    h1_cur, _, _ = _tile_specs(t, tm, tc, 0)
    h3_cur, _, h3_next = _tile_specs(t, tm, tc, nf)
    g_cur, _, g_next = _tile_specs(t, tm, tc, 0)
    out = pl.BlockSpec((tm, tc), lambda j, i: (i, j))
    return _pc(body, name=name, grid=(nf, nt),
               in_specs=[h1_cur, h3_cur, h3_next, g_cur, g_next, g_cur, g_next,
                         pl.BlockSpec((3, tc), lambda j, i: (0, j))],
               out_specs=[out, out, pl.BlockSpec((8, tc), lambda j, i: (0, j))],
               out_shape=[_sds((t, f), BF16), _sds((t, f), BF16), _sds((8, f), F32)],
               scratch_shapes=[pltpu.VMEM((tm + HALO, tc), F32)],
               compiler_params=_params())(hv, hv, hv, c1, c1, dact, dact, w)


def _gnorm_fwd(y, z, w, name):
    t, di = y.shape
    gw = di // N_GROUPS
    tm = min(1024, t)

    def body(y_ref, z_ref, w_ref, o_ref):
        wv = w_ref[...]
        for rows in _strips(tm):
            zv = z_ref[rows, :].astype(F32)
            yz = y_ref[rows, :].astype(F32) * zv * _sigmoid(zv)
            r = lax.rsqrt(jnp.mean(yz * yz, axis=-1, keepdims=True) + EPS)
            o_ref[rows, :] = (yz * r * wv).astype(o_ref.dtype)

    blk = pl.BlockSpec((tm, gw), lambda j, i: (i, j))
    return _pc(body, name=name, grid=(N_GROUPS, t // tm),
               in_specs=[blk, blk, pl.BlockSpec((1, gw), lambda j, i: (0, j))],
               out_specs=blk, out_shape=_sds((t, di), BF16), compiler_params=_params())(y, z, w)


def _gnorm_bwd(y, z, dyn, w, name):
    t, di = y.shape
    gw = di // N_GROUPS
    tm = min(1024, t)

    def body(y_ref, z_ref, g_ref, w_ref, dy_ref, dz_ref, dw_ref):
        @pl.when(pl.program_id(1) == 0)
        def _():
            dw_ref[...] = jnp.zeros_like(dw_ref)
        wv = w_ref[...]
        acc = jnp.zeros((8, gw), F32)
        for rows in _strips(tm):
            yv, zv, g = y_ref[rows, :].astype(F32), z_ref[rows, :].astype(F32), g_ref[rows, :].astype(F32)
            s = _sigmoid(zv)
            sz = zv * s
            yz = yv * sz
            r = lax.rsqrt(jnp.mean(yz * yz, axis=-1, keepdims=True) + EPS)
            n = yz * r
            dn = g * wv
            acc = acc + _fold8(g * n)
            dyz = r * (dn - n * jnp.mean(dn * n, axis=-1, keepdims=True))
            dy_ref[rows, :] = (dyz * sz).astype(BF16)
            dz_ref[rows, :] = (dyz * yv * s * (1.0 + zv * (1.0 - s))).astype(BF16)
        dw_ref[0:1, :] += _colsum(acc)

    blk = pl.BlockSpec((tm, gw), lambda j, i: (i, j))
    return _pc(body, name=name, grid=(N_GROUPS, t // tm),
               in_specs=[blk, blk, blk, pl.BlockSpec((1, gw), lambda j, i: (0, j))],
               out_specs=[blk, blk, pl.BlockSpec((8, gw), lambda j, i: (0, j))],
               out_shape=[_sds((t, di), BF16), _sds((t, di), BF16), _sds((8, di), F32)],
               compiler_params=_params())(y, z, dyn, w)


def _merge_fwd(gates, ya, ys, d, name):
    t = ya.shape[0]
    tm, tc = min(1024, t), _pick(d, (512, 256, 128))
    nd = d // tc

    def body(ga_ref, gs_ref, ya_ref, ys_ref, o_ref):
        for rows in _strips(tm):
            o_ref[rows, :] = (_sigmoid(ga_ref[rows, :].astype(F32)) * ya_ref[rows, :].astype(F32)
                              + _sigmoid(gs_ref[rows, :].astype(F32)) * ys_ref[rows, :].astype(F32)
                              ).astype(o_ref.dtype)

    blk = pl.BlockSpec((tm, tc), lambda j, i: (i, j))
    return _pc(body, name=name, grid=(nd, t // tm),
               in_specs=[blk, pl.BlockSpec((tm, tc), lambda j, i: (i, nd + j)), blk, blk],
               out_specs=blk, out_shape=_sds((t, d), BF16), compiler_params=_params())(gates, gates, ya, ys)


def _merge_bwd(dm, gates, ya, ys, d, name):
    t = ya.shape[0]
    tm, tc = min(1024, t), _pick(d, (512, 256, 128))
    nd = d // tc

    def body(dm_ref, ga_ref, gs_ref, ya_ref, ys_ref, dya_ref, dys_ref, dga_ref, dgs_ref):
        for rows in _strips(tm):
            g = dm_ref[rows, :].astype(F32)
            sa, ss = _sigmoid(ga_ref[rows, :].astype(F32)), _sigmoid(gs_ref[rows, :].astype(F32))
            dya_ref[rows, :] = (g * sa).astype(BF16)
            dys_ref[rows, :] = (g * ss).astype(BF16)
            dga_ref[rows, :] = (g * ya_ref[rows, :].astype(F32) * sa * (1.0 - sa)).astype(BF16)
            dgs_ref[rows, :] = (g * ys_ref[rows, :].astype(F32) * ss * (1.0 - ss)).astype(BF16)

    blk = pl.BlockSpec((tm, tc), lambda j, i: (i, j))
    return _pc(body, name=name, grid=(nd, t // tm),
               in_specs=[blk, blk, pl.BlockSpec((tm, tc), lambda j, i: (i, nd + j)), blk, blk],
               out_specs=[blk] * 4, out_shape=[_sds((t, d), BF16)] * 4,
               compiler_params=_params())(dm, gates, gates, ya, ys)


def _ssd_chunk_terms(dtr, dtb, alog):
    xx = dtr + dtb
    dt = jnp.maximum(xx, 0.0) + jnp.log(1.0 + jnp.exp(-jnp.abs(xx)))
    a = -jnp.exp(alog)
    li = lax.broadcasted_iota(jnp.int32, (CHUNK, CHUNK), 0)
    si = lax.broadcasted_iota(jnp.int32, (CHUNK, CHUNK), 1)
    causal = li >= si
    acum = _dot(causal.astype(F32), dt * a, 1, 0, HIGHEST)
    return xx, dt, a, acum, acum.T, causal


def _split2(x):
    hi = x.astype(BF16)
    return hi, (x - hi.astype(F32)).astype(BF16)


def _expand(v, e, exact=True):
    hi, lo = _split2(v)
    out = _dot(hi, e, 1, 0)
    return out + _dot(lo, e, 1, 0) if exact else out


def _segsum(s, e):
    hi, lo = _split2(s)
    return _dot(hi, e, 1, 1) + _dot(lo, e, 1, 1)


def _head_maps(di):
    nh = di // HEAD_DIM
    h = jnp.arange(DT_LANES)[:, None]
    e64 = (jnp.arange(di)[None, :] // HEAD_DIM == h).astype(BF16)
    e128 = (jnp.arange(nh * CHUNK)[None, :] // CHUNK == h).astype(BF16)
    return e64, e128


def _pair_blockdiag(p, left):
    zero = jnp.zeros_like(p)
    return jnp.concatenate([jnp.where(left, p, zero), jnp.where(left, zero, p)], axis=0)


def _ssd_fwd(xc, dtr, dtb, alog, dskx, di, name):
    t = xc.shape[0]
    dx = xc.shape[1]
    nc = t // CHUNK
    nh = di // HEAD_DIM
    hpg = nh // N_GROUPS
    gw = hpg * HEAD_DIM
    boff, coff = di, di + N_GROUPS * D_STATE
    e64, e128 = _head_maps(di)

    def body(xc_ref, dtr_ref, dtb_ref, alog_ref, dsk_ref, e64_ref, e128_ref, y_ref, st_ref, state):
        @pl.when(pl.program_id(0) == 0)
        def _():
            state[...] = jnp.zeros_like(state)
        _, dt, _, acum, acum_t, causal = _ssd_chunk_terms(dtr_ref[...], dtb_ref[...], alog_ref[...])
        last = acum[CHUNK - 1:CHUNK, :]
        e64v = e64_ref[...]
        dtx = _expand(dt, e64v, False)
        eax = _expand(jnp.exp(acum), e64v)
        dex = _expand(dt * jnp.exp(last - acum), e64v, False)
        acx = _expand(acum, e128_ref[...])
        st_ref[0] = state[...]
        left = lax.broadcasted_iota(jnp.int32, (CHUNK, 2 * HEAD_DIM), 1) < HEAD_DIM
        for g in range(N_GROUPS):
            gs = slice(g * gw, (g + 1) * gw)
            bg = xc_ref[:, boff + g * D_STATE:boff + (g + 1) * D_STATE]
            cg = xc_ref[:, coff + g * D_STATE:coff + (g + 1) * D_STATE]
            gm = _dot(cg, bg, 1, 1)
            xg = xc_ref[:, gs].astype(F32)
            xdb = (xg * dtx[:, gs]).astype(BF16)
            sin = state[:, gs]
            yo = _dot(cg, sin.astype(BF16), 1, 0) * eax[:, gs]
            for jp in range(hpg // 2):
                h0 = g * hpg + 2 * jp
                ps = slice(jp * 2 * HEAD_DIM, (jp + 1) * 2 * HEAD_DIM)
                ms = []
                for hh in (h0, h0 + 1):
                    seg = acx[:, hh * CHUNK:(hh + 1) * CHUNK] - acum_t[hh:hh + 1, :]
                    ms.append((gm * jnp.exp(jnp.where(causal, seg, -1e30))).astype(BF16))
                yd = _dot(jnp.concatenate(ms, axis=1), _pair_blockdiag(xdb[:, ps], left), 1, 0)
                col = slice(g * gw + jp * 2 * HEAD_DIM, g * gw + (jp + 1) * 2 * HEAD_DIM)
                y_ref[:, col] = (yd + yo[:, ps] + dsk_ref[:, col] * xg[:, ps]).astype(y_ref.dtype)
            xe = (xg * dex[:, gs]).astype(BF16)
            state[:, gs] = eax[CHUNK - 1:CHUNK, gs] * sin + _dot(bg, xe, 0, 0)

    small = pl.BlockSpec((1, DT_LANES), lambda c: (0, 0))
    whole = lambda a: pl.BlockSpec(a.shape, lambda c: (0, 0))
    return _pc(body, name=name, grid=(nc,),
               in_specs=[pl.BlockSpec((CHUNK, dx), lambda c: (c, 0)),
                         pl.BlockSpec((CHUNK, DT_LANES), lambda c: (c, 0)), small, small,
                         whole(dskx), whole(e64), whole(e128)],
               out_specs=[pl.BlockSpec((CHUNK, di), lambda c: (c, 0)),
                          pl.BlockSpec((1, D_STATE, di), lambda c: (c, 0, 0))],
               out_shape=[_sds((t, di), BF16), _sds((nc, D_STATE, di), F32)],
               scratch_shapes=[pltpu.VMEM((D_STATE, di), F32)],
               compiler_params=_params())(xc, dtr, dtb, alog, dskx, e64, e128)


def _ssd_bwd(xc, dtr, dy, states, dtb, alog, dskx, di, name):
    t = xc.shape[0]
    dx = xc.shape[1]
    nc = t // CHUNK
    nh = di // HEAD_DIM
    hpg = nh // N_GROUPS
    gw = hpg * HEAD_DIM
    boff, coff = di, di + N_GROUPS * D_STATE
    e64, e128 = _head_maps(di)

    def body(xc_ref, dtr_ref, dy_ref, st_ref, dtb_ref, alog_ref, dsk_ref, e64_ref, e128_ref,
             dxc_ref, ddtr_ref, sm_ref, dstate, darow):
        @pl.when(pl.program_id(0) == 0)
        def _():
            dstate[...] = jnp.zeros_like(dstate)
            sm_ref[...] = jnp.zeros_like(sm_ref)
        darow[...] = jnp.zeros_like(darow)
        xx, dt, a, acum, acum_t, causal = _ssd_chunk_terms(dtr_ref[...], dtb_ref[...], alog_ref[...])
        last = acum[CHUNK - 1:CHUNK, :]
        e64v = e64_ref[...]
        dtx = _expand(dt, e64v, False)
        eax = _expand(jnp.exp(acum), e64v)
        eex = _expand(jnp.exp(last - acum), e64v, False)
        acx = _expand(acum, e128_ref[...])
        left = lax.broadcasted_iota(jnp.int32, (CHUNK, 2 * HEAD_DIM), 1) < HEAD_DIM
        lane = lax.broadcasted_iota(jnp.int32, (CHUNK, DT_LANES), 1)
        sub8 = lax.broadcasted_iota(jnp.int32, (8, gw), 0)
        da_col = jnp.zeros((CHUNK, DT_LANES), F32)
        ddt_col = jnp.zeros((CHUNK, DT_LANES), F32)
        rows = jnp.zeros((8, DT_LANES), F32)
        for g in range(N_GROUPS):
            gs = slice(g * gw, (g + 1) * gw)
            bg = xc_ref[:, boff + g * D_STATE:boff + (g + 1) * D_STATE]
            cg = xc_ref[:, coff + g * D_STATE:coff + (g + 1) * D_STATE]
            gm = _dot(cg, bg, 1, 1)
            e64g = e64v[:, gs]
            xg = xc_ref[:, gs].astype(F32)
            dtg, eag, eeg = dtx[:, gs], eax[:, gs], eex[:, gs]
            xd = xg * dtg
            xdb = xd.astype(BF16)
            dyb = dy_ref[:, gs]
            dyf = dyb.astype(F32)
            sin = st_ref[0, :, gs]
            sinb = sin.astype(BF16)
            ds = dstate[:, gs]
            dsb = ds.astype(BF16)
            bds = _dot(bg, dsb, 1, 0)
            dyeb = (dyf * eag).astype(BF16)
            dcg = _dot(dyeb, sinb, 1, 1)
            dstate[:, gs] = eag[CHUNK - 1:CHUNK, :] * ds + _dot(cg, dyeb, 0, 0)
            yo = _dot(cg, sinb, 1, 0) * eag
            xe = xd * eeg
            dbg = _dot(xe.astype(BF16), dsb, 1, 1)
            wterm = bds * xe
            da_col = da_col + _segsum(dyf * yo - wterm, e64g)
            dg = jnp.zeros((CHUNK, CHUNK), F32)
            dxd_parts = []
            for jp in range(hpg // 2):
                h0 = g * hpg + 2 * jp
                ps = slice(jp * 2 * HEAD_DIM, (jp + 1) * 2 * HEAD_DIM)
                lms, mfs = [], []
                for hh in (h0, h0 + 1):
                    seg = acx[:, hh * CHUNK:(hh + 1) * CHUNK] - acum_t[hh:hh + 1, :]
                    lm = jnp.exp(jnp.where(causal, seg, -1e30))
                    lms.append(lm)
                    mfs.append(gm * lm)
                mstack = jnp.concatenate([m.astype(BF16) for m in mfs], axis=0)
                dyp = dyb[:, ps]
                dxd_parts.append(_dot(mstack, _pair_blockdiag(dyp, left), 0, 0))
                dm2 = _dot(dyp, _pair_blockdiag(xdb[:, ps], left), 1, 1)
                for k, hh in enumerate((h0, h0 + 1)):
                    dm = dm2[:, k * CHUNK:(k + 1) * CHUNK]
                    dg = dg + dm * lms[k]
                    q = dm * mfs[k]
                    da_col = da_col + jnp.where(lane == hh, jnp.sum(q, axis=1, keepdims=True), 0.0)
                    darow[hh:hh + 1, :] = -jnp.sum(q, axis=0, keepdims=True)
            dxd = jnp.concatenate(dxd_parts, axis=1) + bds * eeg
            ddt_col = ddt_col + _segsum(dxd * xg, e64g)
            rsum = (jnp.where(sub8 == 0, jnp.sum(wterm, axis=0, keepdims=True), 0.0)
                    + jnp.where(sub8 == 1, jnp.sum(ds * sin, axis=0, keepdims=True), 0.0)
                    + jnp.where(sub8 == 2, jnp.sum(dyf * xg, axis=0, keepdims=True), 0.0))
            rows = rows + _segsum(rsum, e64g)
            dxc_ref[:, gs] = (dxd * dtg + dsk_ref[:, gs] * dyf).astype(dxc_ref.dtype)
            dgb = dg.astype(BF16)
            dxc_ref[:, boff + g * D_STATE:boff + (g + 1) * D_STATE] = (
                dbg + _dot(dgb, cg, 0, 0)).astype(dxc_ref.dtype)
            dxc_ref[:, coff + g * D_STATE:coff + (g + 1) * D_STATE] = (
                dcg + _dot(dgb, bg, 1, 0)).astype(dxc_ref.dtype)
        at_last = rows[0:1, :] + jnp.exp(last) * rows[1:2, :]
        is_last = lax.broadcasted_iota(jnp.int32, (CHUNK, DT_LANES), 0) == CHUNK - 1
        da = da_col + jnp.where(is_last, at_last, 0.0) + darow[...].T
        li = lax.broadcasted_iota(jnp.int32, (CHUNK, CHUNK), 0)
        si = lax.broadcasted_iota(jnp.int32, (CHUNK, CHUNK), 1)
        dla = _dot((si >= li).astype(F32), da, 1, 0, HIGHEST)
        ddtr = (ddt_col + dla * a) * _sigmoid(xx)
        ddtr_ref[...] = ddtr
        sm_ref[0:1, :] += jnp.sum(ddtr, axis=0, keepdims=True)
        sm_ref[1:2, :] += jnp.sum(dla * dt, axis=0, keepdims=True) * a
        sm_ref[2:3, :] += rows[2:3, :]

    small = pl.BlockSpec((1, DT_LANES), lambda c: (0, 0))
    whole = lambda a: pl.BlockSpec(a.shape, lambda c: (0, 0))
    rev = lambda c: (nc - 1 - c, 0)
    return _pc(body, name=name, grid=(nc,),
               in_specs=[pl.BlockSpec((CHUNK, dx), rev), pl.BlockSpec((CHUNK, DT_LANES), rev),
                         pl.BlockSpec((CHUNK, di), rev),
                         pl.BlockSpec((1, D_STATE, di), lambda c: (nc - 1 - c, 0, 0)), small, small,
                         whole(dskx), whole(e64), whole(e128)],
               out_specs=[pl.BlockSpec((CHUNK, dx), rev), pl.BlockSpec((CHUNK, DT_LANES), rev),
                          pl.BlockSpec((8, DT_LANES), lambda c: (0, 0))],
               out_shape=[_sds((t, dx), BF16), _sds((t, DT_LANES), F32), _sds((8, DT_LANES), F32)],
               scratch_shapes=[pltpu.VMEM((D_STATE, di), F32), pltpu.VMEM((DT_LANES, CHUNK), F32)],
               compiler_params=_params())(xc, dtr, dy, states, dtb, alog, dskx, e64, e128)


def _adamw(parts, w, m, v, name):
    npart, rows, width = parts.shape
    tr, tw = (_pick(rows, (64, 32, 16, 8)), width) if rows % 8 == 0 else (rows, 128)
    c1 = 1.0 - ADAM_B1 ** ADAM_STEP
    c2 = 1.0 - ADAM_B2 ** ADAM_STEP

    row_strips = _strips(tr) if tr % STRIP == 0 else [slice(0, tr)]
    col_chunks = [slice(c, c + 512) for c in range(0, tw, 512)] if tw % 512 == 0 else [slice(0, tw)]

    def body(p_ref, w_ref, m_ref, v_ref, g_ref, d_ref, nm_ref, nv_ref):
        for rows in row_strips:
            for cols in col_chunks:
                g = p_ref[0, rows, cols].astype(F32)
                for p in range(1, npart):
                    g = g + p_ref[p, rows, cols].astype(F32)
                nm = ADAM_B1 * m_ref[rows, cols] + (1.0 - ADAM_B1) * g
                nv = ADAM_B2 * v_ref[rows, cols] + (1.0 - ADAM_B2) * (g * g)
                g_ref[rows, cols] = g
                nm_ref[rows, cols] = nm
                nv_ref[rows, cols] = nv
                d_ref[rows, cols] = -ADAM_LR * ((nm / c1) / (jnp.sqrt(nv / c2) + ADAM_EPS)
                                                + ADAM_WD * w_ref[rows, cols])

    blk = pl.BlockSpec((tr, tw), lambda i, j: (i, j))
    return _pc(body, name=name, grid=(rows // tr, width // tw),
               in_specs=[pl.BlockSpec((npart, tr, tw), lambda i, j: (0, i, j)), blk, blk, blk],
               out_specs=[blk] * 4, out_shape=[_sds((rows, width), F32)] * 4,
               compiler_params=_params())(parts, w, m, v)


def _sum_parts(parts, name, tile=None):
    npart, rows, width = parts.shape
    tile = rows if tile is None else tile

    def body(p_ref, o_ref):
        for rows_ in _strips(tile, 8 if parts.dtype == F32 else STRIP):
            g = p_ref[0, rows_, :].astype(F32)
            for p in range(1, npart):
                g = g + p_ref[p, rows_, :].astype(F32)
            o_ref[rows_, :] = g

    return _pc(body, name=name, grid=(rows // tile,),
               in_specs=[pl.BlockSpec((npart, tile, width), lambda i: (0, i, 0))],
               out_specs=pl.BlockSpec((tile, width), lambda i: (i, 0)),
               out_shape=_sds((rows, width), F32), compiler_params=_params())(parts)


def _flip(k):
    x, y, c = lax.axis_index("x"), lax.axis_index("y"), lax.axis_index("c")
    px = 1 - x if k & 4 else x
    py = 1 - y if k & 2 else y
    pc = 1 - c if k & 1 else c
    return (px, py, pc), 4 * px + 2 * py + pc


DIRECT = tuple((k, 0) for k in range(1, N_DEV))
TO_CHIPS = ((1, 0), (2, 0), (4, 0), (6, 0))
TO_SIBLING = ((1, 2), (1, 4), (1, 6))


def _copies(arrays, lands, send_sems, recv_sems, scatter, moves):
    _, me = _flip(0)
    outgoing, incoming = [], []
    for i, (kd, kb) in enumerate(moves):
        peer, pidx = _flip(kd)
        _, out_slot = _flip(kb)
        _, in_slot = _flip(kd ^ kb)
        for j, land_ref in enumerate(lands):
            if kb:
                src = land_ref.at[out_slot]
            else:
                src = arrays[j].at[pidx] if scatter[j] else arrays[j]
            sem = len(lands) * i + j
            for dst, bucket in ((land_ref.at[out_slot], outgoing), (land_ref.at[in_slot], incoming)):
                bucket.append(pltpu.make_async_remote_copy(
                    src_ref=src, dst_ref=dst, send_sem=send_sems.at[sem], recv_sem=recv_sems.at[sem],
                    device_id=peer, device_id_type=MESH))
    return outgoing, incoming


HBM_SPEC = pl.BlockSpec(memory_space=pltpu.HBM)
SEM_SPEC = pl.BlockSpec(memory_space=pltpu.SEMAPHORE)
ANY_SPEC = pl.BlockSpec(memory_space=pl.ANY)
EFFECT = pltpu.SideEffectType.DATAFLOW_SIDE_EFFECTING


def _landing_zones(arrays, scatter):
    _, me = _flip(0)
    lands = []
    for a, sc in zip(arrays, scatter):
        own = lax.dynamic_index_in_dim(a, me, 0, keepdims=True) if sc else a[None]
        shape = a.shape if sc else (N_DEV,) + a.shape
        lands.append(lax.dynamic_update_slice(lax.empty(shape, a.dtype), own, (me,) + (0,) * (len(shape) - 1)))
    return lands


def _xchg_start(arrays, scatter, after, name, moves=DIRECT, lands=None):
    if lands is None:
        lands = _landing_zones(arrays, scatter)
    na, nl = len(arrays), len(lands)

    def body(*refs):
        ins, outs = refs[:na + nl], refs[na + nl + 1:]
        outgoing, _ = _copies(ins[:na], ins[na:], outs[0], outs[1], scatter, moves)
        for cp in outgoing:
            cp.start()
        outs[-1][...] = jnp.zeros_like(outs[-1])

    nsem = nl * len(moves)
    operands = [pltpu.with_memory_space_constraint(a, pltpu.HBM) for a in list(arrays) + list(lands)]
    out = _pc(body, name=name,
              out_shape=(pltpu.SemaphoreType.DMA((nsem,)), pltpu.SemaphoreType.DMA((nsem,)),
                         *[pltpu.HBM(a.shape, a.dtype) for a in operands], _sds((8, 128), F32)),
              in_specs=[HBM_SPEC] * (na + nl) + [ANY_SPEC],
              out_specs=(SEM_SPEC, SEM_SPEC, *[HBM_SPEC] * (na + nl), pl.BlockSpec(memory_space=pltpu.VMEM)),
              input_output_aliases={i: 2 + i for i in range(na + nl)},
              compiler_params=pltpu.CompilerParams(has_side_effects=EFFECT))(*operands, after)
    return dict(sems=out[:2], thru=out[2:2 + na + nl], token=out[-1], scatter=scatter, na=na, moves=moves)


def _xchg_wait(handle, after, name):
    na, scatter, moves, thru = handle["na"], handle["scatter"], handle["moves"], handle["thru"]
    n = len(thru)

    def body(*refs):
        ins = refs[:n]
        outgoing, incoming = _copies(ins[:na], ins[na:], refs[n], refs[n + 1], scatter, moves)
        for cp in outgoing:
            cp.wait_send()
        for cp in incoming:
            cp.wait_recv()

    out = _pc(body, name=name, out_shape=tuple(pltpu.HBM(a.shape, a.dtype) for a in thru),
              in_specs=[HBM_SPEC] * n + [SEM_SPEC, SEM_SPEC, ANY_SPEC], out_specs=tuple([HBM_SPEC] * n),
              input_output_aliases={i: i for i in range(n)},
              compiler_params=pltpu.CompilerParams(has_side_effects=EFFECT))(*thru, *handle["sems"], after)
    return out[na:]


def _pack(arrs, width, row_mult):
    flat = jnp.concatenate([a.reshape(-1) for a in arrs])
    n = flat.shape[0]
    rows = -(-n // (width * row_mult)) * row_mult
    return jnp.pad(flat, (0, rows * width - n)).reshape(rows, width)


def _unpack(packed, shapes, lead=None):
    out, off = [], 0
    flat = packed.reshape(-1) if lead is None else packed.reshape(lead, -1)
    for s in shapes:
        n = math.prod(s)
        if lead is None:
            out.append(flat[off:off + n].reshape(s))
        else:
            out.append(flat[:, off:off + n].reshape((lead,) + tuple(s)))
        off += n
    return out


def _blocks_to_cols(blocks):
    nb, rows, n = blocks.shape
    return blocks.transpose(1, 0, 2).reshape(rows, nb * n)


def _pad_rows(a, rows):
    return jnp.pad(a, ((0, rows - a.shape[0]), (0, 0)))


def _pad_lanes(a, lanes):
    return jnp.pad(a, ((0, 0), (0, lanes - a.shape[1])))


REST = ("w_a_out", "w_s_out", "w_o", "w_up", "w_down")
TRANSPOSED = ("w_up", "w_in")
CONVS = ("conv_a_w", "ssd_conv_w", "ffn_conv_w")
REPL = ("norm_mix_w", "ssd_conv_b", "dt_bias", "a_log", "d_skip", "ssd_norm_w", "norm_ffn_w", "ffn_conv_b",
        "final_norm_w")
ORDER = ("norm_mix_w", "w_in", "conv_a_w", "w_a_out", "ssd_conv_w", "ssd_conv_b", "dt_bias", "a_log", "d_skip",
         "ssd_norm_w", "w_s_out", "w_o", "norm_ffn_w", "w_up", "ffn_conv_w", "ffn_conv_b", "w_down", "final_norm_w")


def _as_rows(name, block):
    return block[0].T if name in TRANSPOSED else block[0]


def kernel(x, norm_mix_w, w_in, conv_a_w, w_a_out, ssd_conv_w, ssd_conv_b, dt_bias, a_log, d_skip, ssd_norm_w, w_s_out, w_o, norm_ffn_w, w_up, ffn_conv_w, ffn_conv_b, w_down, final_norm_w, loss_target, m_norm_mix_w, m_w_in, m_conv_a_w, m_w_a_out, m_ssd_conv_w, m_ssd_conv_b, m_dt_bias, m_a_log, m_d_skip, m_ssd_norm_w, m_w_s_out, m_w_o, m_norm_ffn_w, m_w_up, m_ffn_conv_w, m_ffn_conv_b, m_w_down, m_final_norm_w, v_norm_mix_w, v_w_in, v_conv_a_w, v_w_a_out, v_ssd_conv_w, v_ssd_conv_b, v_dt_bias, v_a_log, v_d_skip, v_ssd_norm_w, v_w_s_out, v_w_o, v_norm_ffn_w, v_w_up, v_ffn_conv_w, v_ffn_conv_b, v_w_down, v_final_norm_w):
    wts = dict(norm_mix_w=norm_mix_w, w_in=w_in, conv_a_w=conv_a_w, w_a_out=w_a_out, ssd_conv_w=ssd_conv_w,
               ssd_conv_b=ssd_conv_b, dt_bias=dt_bias, a_log=a_log, d_skip=d_skip, ssd_norm_w=ssd_norm_w,
               w_s_out=w_s_out, w_o=w_o, norm_ffn_w=norm_ffn_w, w_up=w_up, ffn_conv_w=ffn_conv_w,
               ffn_conv_b=ffn_conv_b, w_down=w_down, final_norm_w=final_norm_w)
    mom1 = dict(norm_mix_w=m_norm_mix_w, w_in=m_w_in, conv_a_w=m_conv_a_w, w_a_out=m_w_a_out,
                ssd_conv_w=m_ssd_conv_w, ssd_conv_b=m_ssd_conv_b, dt_bias=m_dt_bias, a_log=m_a_log, d_skip=m_d_skip,
                ssd_norm_w=m_ssd_norm_w, w_s_out=m_w_s_out, w_o=m_w_o, norm_ffn_w=m_norm_ffn_w, w_up=m_w_up,
                ffn_conv_w=m_ffn_conv_w, ffn_conv_b=m_ffn_conv_b, w_down=m_w_down, final_norm_w=m_final_norm_w)
    mom2 = dict(norm_mix_w=v_norm_mix_w, w_in=v_w_in, conv_a_w=v_conv_a_w, w_a_out=v_w_a_out,
                ssd_conv_w=v_ssd_conv_w, ssd_conv_b=v_ssd_conv_b, dt_bias=v_dt_bias, a_log=v_a_log, d_skip=v_d_skip,
                ssd_norm_w=v_ssd_norm_w, w_s_out=v_w_s_out, w_o=v_w_o, norm_ffn_w=v_norm_ffn_w, w_up=v_w_up,
                ffn_conv_w=v_ffn_conv_w, ffn_conv_b=v_ffn_conv_b, w_down=v_w_down, final_norm_w=v_final_norm_w)

    t, d = x.shape[1], x.shape[2]
    di = 2 * d
    nh = di // HEAD_DIM
    dxw = di + 2 * N_GROUPS * D_STATE
    f = w_down.shape[1] * N_DEV
    n_in = w_in.shape[2] * N_DEV
    me = 4 * lax.axis_index("x") + 2 * lax.axis_index("y") + lax.axis_index("c")

    rest_local = [_as_rows(k, wts[k]).astype(BF16) for k in REST]
    nrows = [a.shape[0] for a in rest_local]
    n_blk = w_in.shape[2]
    in_local = w_in[0].T.astype(BF16)
    conv_shapes = [wts[k].shape[1:] for k in CONVS]
    conv_local = _pack([wts[k] for k in CONVS], d, 8)
    x2, tgt = x[0], loss_target[0]
    h_in = _xchg_start([in_local, conv_local], [False, False], x2, "gather_in_start", moves=TO_CHIPS)
    u = _rms_fwd(x2, norm_mix_w, h_in["token"], "norm_mix")
    part = _xchg_wait(h_in, u, "gather_in_wait")
    h_fwd = _xchg_start([], [False, False], u, "gather_in_forward_start", moves=TO_SIBLING, lands=part)
    in_all, conv_all = _xchg_wait(h_fwd, u, "gather_in_forward_wait")
    win_t = in_all.reshape(n_in, d)
    h_rest = _xchg_start(rest_local, [False] * len(REST), in_all, "gather_rest_start")
    c_a, c_s, c_f = _unpack(conv_all, conv_shapes, N_DEV)
    caw, scw, fcw = _blocks_to_cols(c_a), _blocks_to_cols(c_s), _blocks_to_cols(c_f)

    o_z, o_x, o_dt = 5 * d, 7 * d, 7 * d + dxw
    seg_bounds = [0, d, 2 * d, 3 * d, 4 * d, o_z, o_x, o_dt]
    w_dt = _pad_rows(win_t[o_dt:], DT_LANES)
    dtb, alog = (_pad_lanes(p[...].reshape(1, nh), DT_LANES) for p in (dt_bias, a_log))
    dskx = jnp.repeat(d_skip.reshape(1, nh), HEAD_DIM, axis=1)

    tok = h_rest["token"]
    gates = _mm([(u, 0, d, win_t, 0)], "nt", BF16, "proj_gates", n=2 * d, after=tok)
    pa = _mm([(u, 0, d, win_t, 2 * d)], "nt", BF16, "proj_a", n=3 * d, after=tok)
    z = _mm([(u, 0, d, win_t, o_z)], "nt", BF16, "proj_z", n=2 * d, after=tok)
    xbc = _mm([(u, 0, d, win_t, o_x)], "nt", BF16, "proj_xbc", n=dxw, after=tok)
    dtr = _mm([(u, w_dt)], "nt", F32, "proj_dt", after=tok)
    ya_in, q_a = _conv_a_fwd(pa, caw, d, "conv_a")
    xc, pre_s = _conv_s_fwd(xbc, scw, ssd_conv_b, "conv_s")
    y, states = _ssd_fwd(xc, dtr, dtb, alog, dskx, di, "ssd")
    yn = _gnorm_fwd(y, z, ssd_norm_w, "gnorm")
    rest_all = _xchg_wait(h_rest, yn, "gather_rest_wait")
    waout, wsout, wo, wup_t, wdown = (a.reshape(N_DEV * n, d) for a, n in zip(rest_all, nrows))
    y_a = _mm([(ya_in, waout)], "nn", BF16, "a_out")
    y_s = _mm([(yn, wsout)], "nn", BF16, "s_out")
    merged = _merge_fwd(gates, y_a, y_s, d, "merge")
    mo = _mm([(merged, wo)], "nn", BF16, "o_proj")
    h1, v = _resnorm_fwd(x2, mo, norm_ffn_w, "norm_ffn")
    hv = _mm([(v, wup_t)], "nt", BF16, "up_proj")
    act, c1 = _ffn_fwd(hv, fcw, ffn_conv_b, f, "ffn_act")
    dd = _mm([(act, wdown)], "nn", BF16, "down_proj")
    loss11, dh2, dh2b, g_fnw = _final(h1, dd, tgt, final_norm_w.reshape(1, d), "final")

    dact = _mm([(dh2b, wdown)], "nt", BF16, "d_act")
    gw_down = _mm_tn(act, dh2b, "gw_down")
    dh1f, dh3, g_ffn = _ffn_bwd(hv, c1, dact, fcw, f, "ffn_act_bwd")
    dv = _mm([(dh1f, 0, f, wup_t, 0), (dh3, 0, f, wup_t, f)], "nn", BF16, "d_v")
    gw_up_t = jnp.concatenate([_mm_tn(dh1f, v, "gw_up1"), _mm_tn(dh3, v, "gw_up3")], axis=0)
    dh1, dh1b, g_nfw = _rms_bwd(h1, dv, norm_ffn_w, dh2, "norm_ffn_bwd")
    dmerged = _mm([(dh1b, wo)], "nt", BF16, "d_merged")
    gw_o = _mm_tn(merged, dh1b, "gw_o")
    dya, dys, dga, dgs = _merge_bwd(dmerged, gates, y_a, y_s, d, "merge_bwd")
    dyain = _mm([(dya, waout)], "nt", BF16, "d_ya_in")
    gw_aout = _mm_tn(ya_in, dya, "gw_a_out")
    db, dc, dvv, g_caw = _conv_a_bwd(pa, q_a, dyain, caw, d, "conv_a_bwd")
    dyn = _mm([(dys, wsout)], "nt", BF16, "d_yn")
    gw_sout = _mm_tn(yn, dys, "gw_s_out")
    grads_rest = dict(w_a_out=gw_aout, w_s_out=gw_sout, w_o=gw_o, w_up=gw_up_t, w_down=gw_down)
    rest_parts = [grads_rest[k].reshape(N_DEV, n, d) for k, n in zip(REST, nrows)]
    h_grest = _xchg_start(rest_parts, [True] * len(REST), rest_parts[0], "scatter_rest_start")
    dy, dz, g_snw = _gnorm_bwd(y, z, dyn, ssd_norm_w, "gnorm_bwd")
    dtb_after = dtb + h_grest["token"][0:1, 0:1]
    dxc, ddtr, g_ssd = _ssd_bwd(xc, dtr, dy, states, dtb_after, alog, dskx, di, "ssd_bwd")
    dxbc, g_scw = _conv_s_bwd(xbc, pre_s, dxc, scw, "conv_s_bwd")
    dsegs = [dga, dgs, db, dc, dvv, dz, dxbc, ddtr.astype(BF16)]
    pairs = [(s, c, d, win_t, a + c * d) for s, a in zip(dsegs[:-1], seg_bounds) for c in range(s.shape[1] // d)]
    pairs.append((dsegs[-1], w_dt))
    gw_in_t = jnp.concatenate([_mm_tn(s, u, "gw_in%d" % i) for i, s in enumerate(dsegs)], axis=0)[:n_in]
    in_parts = gw_in_t.reshape(N_DEV, n_blk, d)
    h_gin = _xchg_start([in_parts], [True], in_parts, "scatter_in_start")
    du = _mm(pairs, "nn", BF16, "d_u", tm=512, tn=512, after=h_gin["token"])
    dx, _, g_nmw = _rms_bwd(x2, du, norm_mix_w, dh1, "norm_mix_bwd")

    small_grads = dict(norm_mix_w=g_nmw[0], ssd_conv_b=g_scw[4], dt_bias=g_ssd[0, :nh], a_log=g_ssd[1, :nh],
                       d_skip=g_ssd[2, :nh], ssd_norm_w=g_snw[0], norm_ffn_w=g_nfw[0], ffn_conv_b=g_ffn[3],
                       final_norm_w=g_fnw[0], conv_a_w=g_caw[:3], ssd_conv_w=g_scw[:4], ffn_conv_w=g_ffn[:3])
    small_names = REPL + CONVS
    small_parts = _pack([small_grads[k] for k in small_names] + [loss11], d, 8)
    h_small = _xchg_start([small_parts], [False], small_parts, "gather_small_start")
    rest_recv = _xchg_wait(h_grest, dx, "scatter_rest_wait")
    (in_recv,) = _xchg_wait(h_gin, rest_recv[0], "scatter_in_wait")
    (small_all,) = _xchg_wait(h_small, in_recv, "gather_small_wait")
    small_sum = _sum_parts(small_all, "sum_small_grads")
    *small_list, loss = _unpack(small_sum, [small_grads[k].shape for k in small_names] + [()])
    small_g = dict(zip(small_names, small_list))

    res = {}

    def update(k, parts):
        outs = _adamw(parts, *(_as_rows(k, src[k]) for src in (wts, mom1, mom2)), "adamw_" + k)
        for kind, a in zip(("g", "d", "m", "v"), outs):
            res[kind, k] = (a.T if k in TRANSPOSED else a)[None]

    update("w_in", in_recv)
    for k, parts in zip(REST, rest_recv):
        update(k, parts)
    local_g = {}
    for k in REPL:
        local_g[k] = small_g[k].reshape(wts[k].shape)
    for k in CONVS:
        n = wts[k].shape[2]
        local_g[k] = lax.dynamic_slice_in_dim(small_g[k], me * n, n, axis=1)[None]
    w_sm, m_sm, v_sm = (_pack([src[k] for k in small_names], d, 8) for src in (wts, mom1, mom2))
    g_sm = _pack([local_g[k] for k in small_names], d, 8)
    outs_sm = _adamw(g_sm[None], w_sm, m_sm, v_sm, "adamw_small")
    for kind, packed in zip(("g", "d", "m", "v"), outs_sm):
        for k, a in zip(small_names, _unpack(packed, [wts[k].shape for k in small_names])):
            res[kind, k] = a

    return (loss, dx[None], *[res["g", k] for k in ORDER], *[res["d", k] for k in ORDER],
            *[res["m", k] for k in ORDER], *[res["v", k] for k in ORDER])
```

```python
import functools
import math

import jax
import jax.numpy as jnp
from jax import lax
from jax.experimental import pallas as pl
from jax.experimental.pallas import tpu as pltpu

F32 = jnp.float32
BF16 = jnp.bfloat16
EPS = 1e-5
HEAD_DIM = 64
N_GROUPS = 4
D_STATE = 128
CHUNK = 128
DT_LANES = 128
HALO = 16
STRIP = 16
SMALL_PARAM = 16 * 1024
N_DEV = 8
V7X_VMEM_LIMIT = 56 * 1024 * 1024
ADAM_LR, ADAM_B1, ADAM_B2, ADAM_EPS, ADAM_WD, ADAM_STEP = 0.001, 0.9, 0.999, 1e-08, 0.01, 10
HIGHEST = lax.Precision.HIGHEST
MESH = pl.DeviceIdType.MESH


def _pc(body, **kw):
    return pl.pallas_call(body, **kw)


def _params():
    return pltpu.CompilerParams(vmem_limit_bytes=V7X_VMEM_LIMIT)


def _pick(n, cands):
    for c in cands:
        if n % c == 0:
            return c
    return n


def _dot(a, b, ca, cb, prec=None):
    return lax.dot_general(a, b, (((ca,), (cb,)), ((), ())), preferred_element_type=F32, precision=prec)


def _sigmoid(x):
    return 0.5 * jnp.tanh(0.5 * x) + 0.5


def _sds(shape, dtype):
    return jax.ShapeDtypeStruct(shape, dtype)


def _mm(pairs, mode, out_dtype, name, n=None, tm=1024, tn=1024, after=None):
    pairs = [p if len(p) == 5 else (p[0], 0, p[0].shape[1], p[1], 0) for p in pairs]
    m = pairs[0][0].shape[0]
    if n is None:
        n = pairs[0][3].shape[1] if mode == "nn" else pairs[0][3].shape[0]
    tm = min(tm, m)
    rows_nt = [p[4] for p in pairs] if mode == "nt" else []
    tn = next(c for c in (tn, 1408, 512, 256, 128) if n % c == 0 and all(r % c == 0 for r in rows_nt))
    npair = len(pairs)
    cb = 0 if mode == "nn" else 1

    def body(*refs):
        o_ref = refs[-1]
        acc = None
        for p in range(npair):
            part = _dot(refs[2 * p][...], refs[2 * p + 1][...], 1, cb)
            acc = part if acc is None else acc + part
        o_ref[...] = acc.astype(o_ref.dtype)

    in_specs, args = [], []
    for a, a_col, kk, b, b_row in pairs:
        in_specs.append(pl.BlockSpec((tm, kk), lambda i, j, c=a_col: (i, c)))
        if mode == "nn":
            assert b_row % kk == 0
            in_specs.append(pl.BlockSpec((kk, tn), lambda i, j, r=b_row // kk: (r, j)))
        else:
            in_specs.append(pl.BlockSpec((tn, kk), lambda i, j, r=b_row // tn: (r + j, 0)))
        args += [a, b]
    if after is not None:
        in_specs.append(pl.BlockSpec(memory_space=pl.ANY))
        args.append(after)
    return _pc(body, name=name, grid=(m // tm, n // tn), in_specs=in_specs,
               out_specs=pl.BlockSpec((tm, tn), lambda i, j: (i, j)),
               out_shape=_sds((m, n), out_dtype), compiler_params=_params())(*args)


def _mm_tn(a, b, name, tm=1024):
    m, ka = a.shape
    nb = b.shape[1]
    tm = min(tm, m)
    nm = m // tm
    tk = _pick(ka, (1024, 1408, 512, 256, 128))
    tn = _pick(nb, (1024, 512, 256, 128))

    def body(a_ref, b_ref, o_ref, acc):
        t = pl.program_id(2)

        @pl.when(t == 0)
        def _():
            acc[...] = jnp.zeros_like(acc)
        acc[...] += _dot(a_ref[...], b_ref[...], 0, 0)

        @pl.when(t == nm - 1)
        def _():
            o_ref[...] = acc[...].astype(o_ref.dtype)

    return _pc(body, name=name, grid=(ka // tk, nb // tn, nm),
               in_specs=[pl.BlockSpec((tm, tk), lambda i, j, t: (t, i)),
                         pl.BlockSpec((tm, tn), lambda i, j, t: (t, j))],
               out_specs=pl.BlockSpec((tk, tn), lambda i, j, t: (i, j)),
               out_shape=_sds((ka, nb), BF16), scratch_shapes=[pltpu.VMEM((tk, tn), F32)],
               compiler_params=_params())(a, b)


def _strips(tm, strip=STRIP):
    return [slice(r * strip, (r + 1) * strip) for r in range(tm // strip)]


def _fold8(a):
    out = a[0:8, :]
    for r in range(8, a.shape[0], 8):
        out = out + a[r:r + 8, :]
    return out


def _colsum(a8):
    return jnp.sum(a8, axis=0, keepdims=True)


def _rms_fwd(x, w, after, name):
    t, d = x.shape
    tm = min(512, t)

    def body(x_ref, w_ref, after_ref, o_ref):
        wv = w_ref[...]
        for rows in _strips(tm):
            xv = x_ref[rows, :]
            r = lax.rsqrt(jnp.mean(xv * xv, axis=-1, keepdims=True) + EPS)
            o_ref[rows, :] = (xv * r * wv).astype(o_ref.dtype)

    return _pc(body, name=name, grid=(t // tm,),
               in_specs=[pl.BlockSpec((tm, d), lambda i: (i, 0)), pl.BlockSpec((1, d), lambda i: (0, 0)),
                         pl.BlockSpec(memory_space=pl.ANY)],
               out_specs=pl.BlockSpec((tm, d), lambda i: (i, 0)),
               out_shape=_sds((t, d), BF16), compiler_params=_params())(x, w, after)


def _resnorm_fwd(x, mo, w, name):
    t, d = x.shape
    tm = min(512, t)

    def body(x_ref, mo_ref, w_ref, h_ref, v_ref):
        wv = w_ref[...]
        for rows in _strips(tm):
            h = x_ref[rows, :] + mo_ref[rows, :].astype(F32)
            r = lax.rsqrt(jnp.mean(h * h, axis=-1, keepdims=True) + EPS)
            h_ref[rows, :] = h
            v_ref[rows, :] = (h * r * wv).astype(v_ref.dtype)

    row = pl.BlockSpec((tm, d), lambda i: (i, 0))
    return _pc(body, name=name, grid=(t // tm,),
               in_specs=[row, row, pl.BlockSpec((1, d), lambda i: (0, 0))],
               out_specs=[row, row], out_shape=[_sds((t, d), F32), _sds((t, d), BF16)],
               compiler_params=_params())(x, mo, w)


def _rms_bwd(h, dy, w, dres, name):
    t, d = h.shape
    tm = min(512, t)

    def body(h_ref, dy_ref, w_ref, dres_ref, dx_ref, dxb_ref, dw_ref):
        @pl.when(pl.program_id(0) == 0)
        def _():
            dw_ref[...] = jnp.zeros_like(dw_ref)
        wv = w_ref[...]
        acc = jnp.zeros((8, d), F32)
        for rows in _strips(tm):
            hv = h_ref[rows, :]
            dyv = dy_ref[rows, :].astype(F32)
            r = lax.rsqrt(jnp.mean(hv * hv, axis=-1, keepdims=True) + EPS)
            n = hv * r
            dn = dyv * wv
            acc = acc + _fold8(dyv * n)
            dx = dres_ref[rows, :] + r * (dn - n * jnp.mean(dn * n, axis=-1, keepdims=True))
            dx_ref[rows, :] = dx
            dxb_ref[rows, :] = dx.astype(BF16)
        dw_ref[0:1, :] += _colsum(acc)

    row = pl.BlockSpec((tm, d), lambda i: (i, 0))
    return _pc(body, name=name, grid=(t // tm,),
               in_specs=[row, row, pl.BlockSpec((1, d), lambda i: (0, 0)), row],
               out_specs=[row, row, pl.BlockSpec((8, d), lambda i: (0, 0))],
               out_shape=[_sds((t, d), F32), _sds((t, d), BF16), _sds((8, d), F32)],
               compiler_params=_params())(h, dy, w, dres)


def _final(h1, dd, tgt, w, name):
    t, d = h1.shape
    tm = min(512, t)
    nt = t // tm

    def body(h1_ref, dd_ref, tgt_ref, w_ref, loss_ref, dh_ref, dhb_ref, dw_ref, acc):
        i = pl.program_id(0)

        @pl.when(i == 0)
        def _():
            dw_ref[...] = jnp.zeros_like(dw_ref)
            acc[...] = jnp.zeros_like(acc)
        wv = w_ref[...]
        sq = jnp.zeros((8, d), F32)
        dw = jnp.zeros((8, d), F32)
        for rows in _strips(tm):
            h = h1_ref[rows, :] + dd_ref[rows, :].astype(F32)
            r = lax.rsqrt(jnp.mean(h * h, axis=-1, keepdims=True) + EPS)
            n = h * r
            e = n * wv - tgt_ref[rows, :]
            sq = sq + _fold8(e * e)
            dout = e * (1.0 / d)
            dn = dout * wv
            dw = dw + _fold8(dout * n)
            dh = r * (dn - n * jnp.mean(dn * n, axis=-1, keepdims=True))
            dh_ref[rows, :] = dh
            dhb_ref[rows, :] = dh.astype(BF16)
        acc[...] += _colsum(sq)
        dw_ref[0:1, :] += _colsum(dw)

        @pl.when(i == nt - 1)
        def _():
            loss_ref[...] = jnp.sum(acc[...], axis=-1, keepdims=True) * (0.5 / d)

    row = pl.BlockSpec((tm, d), lambda i: (i, 0))
    return _pc(body, name=name, grid=(nt,),
               in_specs=[row, row, row, pl.BlockSpec((1, d), lambda i: (0, 0))],
               out_specs=[pl.BlockSpec((1, 1), lambda i: (0, 0)), row, row, pl.BlockSpec((8, d), lambda i: (0, 0))],
               out_shape=[_sds((1, 1), F32), _sds((t, d), F32), _sds((t, d), BF16), _sds((8, d), F32)],
               scratch_shapes=[pltpu.VMEM((1, d), F32)], compiler_params=_params())(h1, dd, tgt, w)


def _tile_specs(t, tm, tc, col0):
    th = tm // HALO
    last = t // HALO - 1
    cur = pl.BlockSpec((tm, tc), lambda j, i: (i, col0 + j))
    prev = pl.BlockSpec((HALO, tc), lambda j, i: (jnp.maximum(i * th - 1, 0), col0 + j))
    nxt = pl.BlockSpec((HALO, tc), lambda j, i: (jnp.minimum((i + 1) * th, last), col0 + j))
    return cur, prev, nxt


def _conv_strip(buf, w, k, rows):
    out = None
    for j in range(k):
        term = w[j:j + 1, :] * buf[pl.ds(HALO - (k - 1) + j + rows.start, STRIP), :]
        out = term if out is None else out + term
    return out


def _conv_backward(dbuf, x_strip, emit, w, acc_ref, k, tm, with_bias):
    tc = dbuf.shape[1]
    accs = [jnp.zeros((8, tc), F32) for _ in range(k + int(with_bias))]
    for rows in _strips(tm):
        xs = x_strip(rows)
        dx = None
        for j in range(k):
            ds = dbuf[pl.ds(rows.start + k - 1 - j, STRIP), :]
            term = w[j:j + 1, :] * ds
            dx = term if dx is None else dx + term
            accs[j] = accs[j] + _fold8(ds * xs)
            if with_bias and j == k - 1:
                accs[k] = accs[k] + _fold8(ds)
        emit(rows, dx)
    for j, a in enumerate(accs):
        acc_ref[j:j + 1, :] += _colsum(a)


def _conv_a_fwd(pa, w, d, name):
    t = pa.shape[0]
    tm, tc = min(1024, t), _pick(d, (512, 256, 128))
    nd = d // tc

    def body(b_ref, c_ref, v_ref, cp_ref, vp_ref, w_ref, o_ref, q_ref, buf):
        keep = (pl.program_id(1) > 0).astype(F32)
        buf[0:HALO, :] = cp_ref[...].astype(F32) * vp_ref[...].astype(F32) * keep
        for rows in _strips(tm):
            buf[HALO + rows.start:HALO + rows.stop, :] = c_ref[rows, :].astype(F32) * v_ref[rows, :].astype(F32)
        wv = w_ref[...]
        for rows in _strips(tm):
            q = _conv_strip(buf, wv, 3, rows)
            q_ref[rows, :] = q.astype(q_ref.dtype)
            o_ref[rows, :] = (b_ref[rows, :].astype(F32) * q).astype(o_ref.dtype)

    b_cur, _, _ = _tile_specs(t, tm, tc, 0)
    c_cur, c_prev, _ = _tile_specs(t, tm, tc, nd)
    v_cur, v_prev, _ = _tile_specs(t, tm, tc, 2 * nd)
    return _pc(body, name=name, grid=(nd, t // tm),
               in_specs=[b_cur, c_cur, v_cur, c_prev, v_prev, pl.BlockSpec((3, tc), lambda j, i: (0, j))],
               out_specs=[pl.BlockSpec((tm, tc), lambda j, i: (i, j))] * 2,
               out_shape=[_sds((t, d), BF16)] * 2,
               scratch_shapes=[pltpu.VMEM((tm + HALO, tc), F32)],
               compiler_params=_params())(pa, pa, pa, pa, pa, w)


def _conv_a_bwd(pa, q, dya, w, d, name):
    t = pa.shape[0]
    tm, tc = min(1024, t), _pick(d, (512, 256, 128))
    nd, nt = d // tc, t // tm

    def body(b_ref, c_ref, v_ref, bn_ref, q_ref, g_ref, gn_ref, w_ref, db_ref, dc_ref, dv_ref, acc_ref, dbuf):
        i = pl.program_id(1)

        @pl.when(i == 0)
        def _():
            acc_ref[...] = jnp.zeros_like(acc_ref)
        for rows in _strips(tm):
            g = g_ref[rows, :].astype(F32)
            dbuf[rows, :] = g * b_ref[rows, :].astype(F32)
            db_ref[rows, :] = (g * q_ref[rows, :].astype(F32)).astype(BF16)
        dbuf[tm:tm + HALO, :] = gn_ref[...].astype(F32) * bn_ref[...].astype(F32) * (i < nt - 1).astype(F32)

        def emit(rows, dp):
            dc_ref[rows, :] = (dp * v_ref[rows, :].astype(F32)).astype(BF16)
            dv_ref[rows, :] = (dp * c_ref[rows, :].astype(F32)).astype(BF16)

        _conv_backward(dbuf, lambda rows: c_ref[rows, :].astype(F32) * v_ref[rows, :].astype(F32), emit,
                       w_ref[...], acc_ref, 3, tm, False)

    b_cur, _, b_next = _tile_specs(t, tm, tc, 0)
    c_cur, _, _ = _tile_specs(t, tm, tc, nd)
    v_cur, _, _ = _tile_specs(t, tm, tc, 2 * nd)
    g_cur, _, g_next = _tile_specs(t, tm, tc, 0)
    out = pl.BlockSpec((tm, tc), lambda j, i: (i, j))
    return _pc(body, name=name, grid=(nd, nt),
               in_specs=[b_cur, c_cur, v_cur, b_next, g_cur, g_cur, g_next,
                         pl.BlockSpec((3, tc), lambda j, i: (0, j))],
               out_specs=[out, out, out, pl.BlockSpec((8, tc), lambda j, i: (0, j))],
               out_shape=[_sds((t, d), BF16)] * 3 + [_sds((8, d), F32)],
               scratch_shapes=[pltpu.VMEM((tm + HALO, tc), F32)],
               compiler_params=_params())(pa, pa, pa, pa, q, dya, dya, w)


def _conv_s_fwd(xbc, w, b, name):
    t, dx = xbc.shape
    tm, tc = min(1024, t), _pick(dx, (512, 256, 128))

    def body(x_ref, xp_ref, w_ref, b_ref, o_ref, pre_ref, buf):
        buf[0:HALO, :] = xp_ref[...].astype(F32) * (pl.program_id(1) > 0).astype(F32)
        for rows in _strips(tm):
            buf[HALO + rows.start:HALO + rows.stop, :] = x_ref[rows, :].astype(F32)
        wv, bv = w_ref[...], b_ref[...]
        for rows in _strips(tm):
            pre = _conv_strip(buf, wv, 4, rows) + bv
            pre_ref[rows, :] = pre.astype(pre_ref.dtype)
            o_ref[rows, :] = (pre * _sigmoid(pre)).astype(o_ref.dtype)

    cur, prev, _ = _tile_specs(t, tm, tc, 0)
    return _pc(body, name=name, grid=(dx // tc, t // tm),
               in_specs=[cur, prev, pl.BlockSpec((4, tc), lambda j, i: (0, j)),
                         pl.BlockSpec((1, tc), lambda j, i: (0, j))],
               out_specs=[pl.BlockSpec((tm, tc), lambda j, i: (i, j))] * 2,
               out_shape=[_sds((t, dx), BF16)] * 2,
               scratch_shapes=[pltpu.VMEM((tm + HALO, tc), F32)],
               compiler_params=_params())(xbc, xbc, w, b)


def _dsilu(pre):
    s = _sigmoid(pre)
    return s * (1.0 + pre * (1.0 - s))


def _conv_s_bwd(xbc, pre, dxc, w, name):
    t, dx = xbc.shape
    tm, tc = min(1024, t), _pick(dx, (512, 256, 128))
    nt = t // tm

    def body(x_ref, p_ref, pn_ref, g_ref, gn_ref, w_ref, dx_ref, acc_ref, dbuf):
        i = pl.program_id(1)

        @pl.when(i == 0)
        def _():
            acc_ref[...] = jnp.zeros_like(acc_ref)
        for rows in _strips(tm):
            dbuf[rows, :] = g_ref[rows, :].astype(F32) * _dsilu(p_ref[rows, :].astype(F32))
        dbuf[tm:tm + HALO, :] = (gn_ref[...].astype(F32) * _dsilu(pn_ref[...].astype(F32))
                                 * (i < nt - 1).astype(F32))

        def emit(rows, d_in):
            dx_ref[rows, :] = d_in.astype(BF16)

        _conv_backward(dbuf, lambda rows: x_ref[rows, :].astype(F32), emit, w_ref[...], acc_ref, 4, tm, True)

    cur, _, nxt = _tile_specs(t, tm, tc, 0)
    return _pc(body, name=name, grid=(dx // tc, nt),
               in_specs=[cur, cur, nxt, cur, nxt, pl.BlockSpec((4, tc), lambda j, i: (0, j))],
               out_specs=[pl.BlockSpec((tm, tc), lambda j, i: (i, j)), pl.BlockSpec((8, tc), lambda j, i: (0, j))],
               out_shape=[_sds((t, dx), BF16), _sds((8, dx), F32)],
               scratch_shapes=[pltpu.VMEM((tm + HALO, tc), F32)],
               compiler_params=_params())(xbc, pre, pre, dxc, dxc, w)


def _ffn_fwd(hv, w, b, f, name):
    t = hv.shape[0]
    tm, tc = min(1024, t), _pick(f, (512, 256, 128))
    nf = f // tc

    def body(h1_ref, h1p_ref, h3_ref, w_ref, b_ref, o_ref, c1_ref, buf):
        buf[0:HALO, :] = h1p_ref[...].astype(F32) * (pl.program_id(1) > 0).astype(F32)
        for rows in _strips(tm):
            buf[HALO + rows.start:HALO + rows.stop, :] = h1_ref[rows, :].astype(F32)
        wv, bv = w_ref[...], b_ref[...]
        for rows in _strips(tm):
            c1 = _conv_strip(buf, wv, 3, rows) + bv
            c1_ref[rows, :] = c1.astype(c1_ref.dtype)
            o_ref[rows, :] = (c1 * _sigmoid(c1) * h3_ref[rows, :].astype(F32)).astype(o_ref.dtype)

    h1_cur, h1_prev, _ = _tile_specs(t, tm, tc, 0)
    h3_cur, _, _ = _tile_specs(t, tm, tc, nf)
    return _pc(body, name=name, grid=(nf, t // tm),
               in_specs=[h1_cur, h1_prev, h3_cur, pl.BlockSpec((3, tc), lambda j, i: (0, j)),
                         pl.BlockSpec((1, tc), lambda j, i: (0, j))],
               out_specs=[pl.BlockSpec((tm, tc), lambda j, i: (i, j))] * 2,
               out_shape=[_sds((t, f), BF16)] * 2,
               scratch_shapes=[pltpu.VMEM((tm + HALO, tc), F32)],
               compiler_params=_params())(hv, hv, hv, w, b)


def _ffn_bwd(hv, c1, dact, w, f, name):
    t = hv.shape[0]
    tm, tc = min(1024, t), _pick(f, (512, 256, 128))
    nf, nt = f // tc, t // tm

    def body(h1_ref, h3_ref, h3n_ref, c_ref, cn_ref, g_ref, gn_ref, w_ref, dh1_ref, dh3_ref, acc_ref, dbuf):
        i = pl.program_id(1)

        @pl.when(i == 0)
        def _():
            acc_ref[...] = jnp.zeros_like(acc_ref)
        for rows in _strips(tm):
            c1v, g = c_ref[rows, :].astype(F32), g_ref[rows, :].astype(F32)
            s1 = _sigmoid(c1v)
            dh3_ref[rows, :] = (g * c1v * s1).astype(BF16)
            dbuf[rows, :] = g * h3_ref[rows, :].astype(F32) * s1 * (1.0 + c1v * (1.0 - s1))
        dbuf[tm:tm + HALO, :] = (gn_ref[...].astype(F32) * h3n_ref[...].astype(F32)
                                 * _dsilu(cn_ref[...].astype(F32)) * (i < nt - 1).astype(F32))

        def emit(rows, d_in):
            dh1_ref[rows, :] = d_in.astype(BF16)

        _conv_backward(dbuf, lambda rows: h1_ref[rows, :].astype(F32), emit, w_ref[...], acc_ref, 3, tm, True)

    h1_cur, _, _ = _tile_specs(t, tm, tc, 0)
    h3_cur, _, h3_next = _tile_specs(t, tm, tc, nf)
    g_cur, _, g_next = _tile_specs(t, tm, tc, 0)
    out = pl.BlockSpec((tm, tc), lambda j, i: (i, j))
    return _pc(body, name=name, grid=(nf, nt),
               in_specs=[h1_cur, h3_cur, h3_next, g_cur, g_next, g_cur, g_next,
                         pl.BlockSpec((3, tc), lambda j, i: (0, j))],
               out_specs=[out, out, pl.BlockSpec((8, tc), lambda j, i: (0, j))],
               out_shape=[_sds((t, f), BF16), _sds((t, f), BF16), _sds((8, f), F32)],
               scratch_shapes=[pltpu.VMEM((tm + HALO, tc), F32)],
               compiler_params=_params())(hv, hv, hv, c1, c1, dact, dact, w)


def _gnorm_fwd(y, z, w, name):
    t, di = y.shape
    gw = di // N_GROUPS
    tm = min(1024, t)

    def body(y_ref, z_ref, w_ref, o_ref):
        wv = w_ref[...]
        for rows in _strips(tm):
            zv = z_ref[rows, :].astype(F32)
            yz = y_ref[rows, :].astype(F32) * zv * _sigmoid(zv)
            r = lax.rsqrt(jnp.mean(yz * yz, axis=-1, keepdims=True) + EPS)
            o_ref[rows, :] = (yz * r * wv).astype(o_ref.dtype)

    blk = pl.BlockSpec((tm, gw), lambda j, i: (i, j))
    return _pc(body, name=name, grid=(N_GROUPS, t // tm),
               in_specs=[blk, blk, pl.BlockSpec((1, gw), lambda j, i: (0, j))],
               out_specs=blk, out_shape=_sds((t, di), BF16), compiler_params=_params())(y, z, w)


def _gnorm_bwd(y, z, dyn, w, name):
    t, di = y.shape
    gw = di // N_GROUPS
    tm = min(1024, t)

    def body(y_ref, z_ref, g_ref, w_ref, dy_ref, dz_ref, dw_ref):
        @pl.when(pl.program_id(1) == 0)
        def _():
            dw_ref[...] = jnp.zeros_like(dw_ref)
        wv = w_ref[...]
        acc = jnp.zeros((8, gw), F32)
        for rows in _strips(tm):
            yv, zv, g = y_ref[rows, :].astype(F32), z_ref[rows, :].astype(F32), g_ref[rows, :].astype(F32)
            s = _sigmoid(zv)
            sz = zv * s
            yz = yv * sz
            r = lax.rsqrt(jnp.mean(yz * yz, axis=-1, keepdims=True) + EPS)
            n = yz * r
            dn = g * wv
            acc = acc + _fold8(g * n)
            dyz = r * (dn - n * jnp.mean(dn * n, axis=-1, keepdims=True))
            dy_ref[rows, :] = (dyz * sz).astype(BF16)
            dz_ref[rows, :] = (dyz * yv * s * (1.0 + zv * (1.0 - s))).astype(BF16)
        dw_ref[0:1, :] += _colsum(acc)

    blk = pl.BlockSpec((tm, gw), lambda j, i: (i, j))
    return _pc(body, name=name, grid=(N_GROUPS, t // tm),
               in_specs=[blk, blk, blk, pl.BlockSpec((1, gw), lambda j, i: (0, j))],
               out_specs=[blk, blk, pl.BlockSpec((8, gw), lambda j, i: (0, j))],
               out_shape=[_sds((t, di), BF16), _sds((t, di), BF16), _sds((8, di), F32)],
               compiler_params=_params())(y, z, dyn, w)


def _merge_fwd(gates, ya, ys, d, name):
    t = ya.shape[0]
    tm, tc = min(1024, t), _pick(d, (512, 256, 128))
    nd = d // tc

    def body(ga_ref, gs_ref, ya_ref, ys_ref, o_ref):
        for rows in _strips(tm):
            o_ref[rows, :] = (_sigmoid(ga_ref[rows, :].astype(F32)) * ya_ref[rows, :].astype(F32)
                              + _sigmoid(gs_ref[rows, :].astype(F32)) * ys_ref[rows, :].astype(F32)
                              ).astype(o_ref.dtype)

    blk = pl.BlockSpec((tm, tc), lambda j, i: (i, j))
    return _pc(body, name=name, grid=(nd, t // tm),
               in_specs=[blk, pl.BlockSpec((tm, tc), lambda j, i: (i, nd + j)), blk, blk],
               out_specs=blk, out_shape=_sds((t, d), BF16), compiler_params=_params())(gates, gates, ya, ys)


def _merge_bwd(dm, gates, ya, ys, d, name):
    t = ya.shape[0]
    tm, tc = min(1024, t), _pick(d, (512, 256, 128))
    nd = d // tc

    def body(dm_ref, ga_ref, gs_ref, ya_ref, ys_ref, dya_ref, dys_ref, dga_ref, dgs_ref):
        for rows in _strips(tm):
            g = dm_ref[rows, :].astype(F32)
            sa, ss = _sigmoid(ga_ref[rows, :].astype(F32)), _sigmoid(gs_ref[rows, :].astype(F32))
            dya_ref[rows, :] = (g * sa).astype(BF16)
            dys_ref[rows, :] = (g * ss).astype(BF16)
            dga_ref[rows, :] = (g * ya_ref[rows, :].astype(F32) * sa * (1.0 - sa)).astype(BF16)
            dgs_ref[rows, :] = (g * ys_ref[rows, :].astype(F32) * ss * (1.0 - ss)).astype(BF16)

    blk = pl.BlockSpec((tm, tc), lambda j, i: (i, j))
    return _pc(body, name=name, grid=(nd, t // tm),
               in_specs=[blk, blk, pl.BlockSpec((tm, tc), lambda j, i: (i, nd + j)), blk, blk],
               out_specs=[blk] * 4, out_shape=[_sds((t, d), BF16)] * 4,
               compiler_params=_params())(dm, gates, gates, ya, ys)


def _ssd_chunk_terms(dtr, dtb, alog):
    xx = dtr + dtb
    dt = jnp.maximum(xx, 0.0) + jnp.log(1.0 + jnp.exp(-jnp.abs(xx)))
    a = -jnp.exp(alog)
    li = lax.broadcasted_iota(jnp.int32, (CHUNK, CHUNK), 0)
    si = lax.broadcasted_iota(jnp.int32, (CHUNK, CHUNK), 1)
    causal = li >= si
    acum = _dot(causal.astype(F32), dt * a, 1, 0, HIGHEST)
    return xx, dt, a, acum, acum.T, causal


def _split2(x):
    hi = x.astype(BF16)
    return hi, (x - hi.astype(F32)).astype(BF16)


def _expand(v, e, exact=True):
    hi, lo = _split2(v)
    out = _dot(hi, e, 1, 0)
    return out + _dot(lo, e, 1, 0) if exact else out


def _segsum(s, e):
    hi, lo = _split2(s)
    return _dot(hi, e, 1, 1) + _dot(lo, e, 1, 1)


def _head_maps(di):
    nh = di // HEAD_DIM
    h = jnp.arange(DT_LANES)[:, None]
    e64 = (jnp.arange(di)[None, :] // HEAD_DIM == h).astype(BF16)
    e128 = (jnp.arange(nh * CHUNK)[None, :] // CHUNK == h).astype(BF16)
    return e64, e128


def _pair_blockdiag(p, left):
    zero = jnp.zeros_like(p)
    return jnp.concatenate([jnp.where(left, p, zero), jnp.where(left, zero, p)], axis=0)


def _ssd_fwd(xc, dtr, dtb, alog, dskx, di, name):
    t = xc.shape[0]
    dx = xc.shape[1]
    nc = t // CHUNK
    nh = di // HEAD_DIM
    hpg = nh // N_GROUPS
    gw = hpg * HEAD_DIM
    boff, coff = di, di + N_GROUPS * D_STATE
    e64, e128 = _head_maps(di)

    def body(xc_ref, dtr_ref, dtb_ref, alog_ref, dsk_ref, e64_ref, e128_ref, y_ref, st_ref, state):
        @pl.when(pl.program_id(0) == 0)
        def _():
            state[...] = jnp.zeros_like(state)
        _, dt, _, acum, acum_t, causal = _ssd_chunk_terms(dtr_ref[...], dtb_ref[...], alog_ref[...])
        last = acum[CHUNK - 1:CHUNK, :]
        e64v = e64_ref[...]
        dtx = _expand(dt, e64v, False)
        eax = _expand(jnp.exp(acum), e64v)
        dex = _expand(dt * jnp.exp(last - acum), e64v, False)
        acx = _expand(acum, e128_ref[...])
        st_ref[0] = state[...]
        left = lax.broadcasted_iota(jnp.int32, (CHUNK, 2 * HEAD_DIM), 1) < HEAD_DIM
        for g in range(N_GROUPS):
            gs = slice(g * gw, (g + 1) * gw)
            bg = xc_ref[:, boff + g * D_STATE:boff + (g + 1) * D_STATE]
            cg = xc_ref[:, coff + g * D_STATE:coff + (g + 1) * D_STATE]
            gm = _dot(cg, bg, 1, 1)
            xg = xc_ref[:, gs].astype(F32)
            xdb = (xg * dtx[:, gs]).astype(BF16)
            sin = state[:, gs]
            yo = _dot(cg, sin.astype(BF16), 1, 0) * eax[:, gs]
            for jp in range(hpg // 2):
                h0 = g * hpg + 2 * jp
                ps = slice(jp * 2 * HEAD_DIM, (jp + 1) * 2 * HEAD_DIM)
                ms = []
                for hh in (h0, h0 + 1):
                    seg = acx[:, hh * CHUNK:(hh + 1) * CHUNK] - acum_t[hh:hh + 1, :]
                    ms.append((gm * jnp.exp(jnp.where(causal, seg, -1e30))).astype(BF16))
                yd = _dot(jnp.concatenate(ms, axis=1), _pair_blockdiag(xdb[:, ps], left), 1, 0)
                col = slice(g * gw + jp * 2 * HEAD_DIM, g * gw + (jp + 1) * 2 * HEAD_DIM)
                y_ref[:, col] = (yd + yo[:, ps] + dsk_ref[:, col] * xg[:, ps]).astype(y_ref.dtype)
            xe = (xg * dex[:, gs]).astype(BF16)
            state[:, gs] = eax[CHUNK - 1:CHUNK, gs] * sin + _dot(bg, xe, 0, 0)

    small = pl.BlockSpec((1, DT_LANES), lambda c: (0, 0))
    whole = lambda a: pl.BlockSpec(a.shape, lambda c: (0, 0))
    return _pc(body, name=name, grid=(nc,),
               in_specs=[pl.BlockSpec((CHUNK, dx), lambda c: (c, 0)),
                         pl.BlockSpec((CHUNK, DT_LANES), lambda c: (c, 0)), small, small,
                         whole(dskx), whole(e64), whole(e128)],
               out_specs=[pl.BlockSpec((CHUNK, di), lambda c: (c, 0)),
                          pl.BlockSpec((1, D_STATE, di), lambda c: (c, 0, 0))],
               out_shape=[_sds((t, di), BF16), _sds((nc, D_STATE, di), F32)],
               scratch_shapes=[pltpu.VMEM((D_STATE, di), F32)],
               compiler_params=_params())(xc, dtr, dtb, alog, dskx, e64, e128)


def _ssd_bwd(xc, dtr, dy, states, dtb, alog, dskx, di, name):
    t = xc.shape[0]
    dx = xc.shape[1]
    nc = t // CHUNK
    nh = di // HEAD_DIM
    hpg = nh // N_GROUPS
    gw = hpg * HEAD_DIM
    boff, coff = di, di + N_GROUPS * D_STATE
    e64, e128 = _head_maps(di)

    def body(xc_ref, dtr_ref, dy_ref, st_ref, dtb_ref, alog_ref, dsk_ref, e64_ref, e128_ref,
             dxc_ref, ddtr_ref, sm_ref, dstate, darow):
        @pl.when(pl.program_id(0) == 0)
        def _():
            dstate[...] = jnp.zeros_like(dstate)
            sm_ref[...] = jnp.zeros_like(sm_ref)
        darow[...] = jnp.zeros_like(darow)
        xx, dt, a, acum, acum_t, causal = _ssd_chunk_terms(dtr_ref[...], dtb_ref[...], alog_ref[...])
        last = acum[CHUNK - 1:CHUNK, :]
        e64v = e64_ref[...]
        dtx = _expand(dt, e64v, False)
        eax = _expand(jnp.exp(acum), e64v)
        eex = _expand(jnp.exp(last - acum), e64v, False)
        acx = _expand(acum, e128_ref[...])
        left = lax.broadcasted_iota(jnp.int32, (CHUNK, 2 * HEAD_DIM), 1) < HEAD_DIM
        lane = lax.broadcasted_iota(jnp.int32, (CHUNK, DT_LANES), 1)
        sub8 = lax.broadcasted_iota(jnp.int32, (8, gw), 0)
        da_col = jnp.zeros((CHUNK, DT_LANES), F32)
        ddt_col = jnp.zeros((CHUNK, DT_LANES), F32)
        rows = jnp.zeros((8, DT_LANES), F32)
        for g in range(N_GROUPS):
            gs = slice(g * gw, (g + 1) * gw)
            bg = xc_ref[:, boff + g * D_STATE:boff + (g + 1) * D_STATE]
            cg = xc_ref[:, coff + g * D_STATE:coff + (g + 1) * D_STATE]
            gm = _dot(cg, bg, 1, 1)
            e64g = e64v[:, gs]
            xg = xc_ref[:, gs].astype(F32)
            dtg, eag, eeg = dtx[:, gs], eax[:, gs], eex[:, gs]
            xd = xg * dtg
            xdb = xd.astype(BF16)
            dyb = dy_ref[:, gs]
            dyf = dyb.astype(F32)
            sin = st_ref[0, :, gs]
            sinb = sin.astype(BF16)
            ds = dstate[:, gs]
            dsb = ds.astype(BF16)
            bds = _dot(bg, dsb, 1, 0)
            dyeb = (dyf * eag).astype(BF16)
            dcg = _dot(dyeb, sinb, 1, 1)
            dstate[:, gs] = eag[CHUNK - 1:CHUNK, :] * ds + _dot(cg, dyeb, 0, 0)
            yo = _dot(cg, sinb, 1, 0) * eag
            xe = xd * eeg
            dbg = _dot(xe.astype(BF16), dsb, 1, 1)
            wterm = bds * xe
            da_col = da_col + _segsum(dyf * yo - wterm, e64g)
            dg = jnp.zeros((CHUNK, CHUNK), F32)
            dxd_parts = []
            for jp in range(hpg // 2):
                h0 = g * hpg + 2 * jp
                ps = slice(jp * 2 * HEAD_DIM, (jp + 1) * 2 * HEAD_DIM)
                lms, mfs = [], []
                for hh in (h0, h0 + 1):
                    seg = acx[:, hh * CHUNK:(hh + 1) * CHUNK] - acum_t[hh:hh + 1, :]
                    lm = jnp.exp(jnp.where(causal, seg, -1e30))
                    lms.append(lm)
                    mfs.append(gm * lm)
                mstack = jnp.concatenate([m.astype(BF16) for m in mfs], axis=0)
                dyp = dyb[:, ps]
                dxd_parts.append(_dot(mstack, _pair_blockdiag(dyp, left), 0, 0))
                dm2 = _dot(dyp, _pair_blockdiag(xdb[:, ps], left), 1, 1)
                for k, hh in enumerate((h0, h0 + 1)):
                    dm = dm2[:, k * CHUNK:(k + 1) * CHUNK]
                    dg = dg + dm * lms[k]
                    q = dm * mfs[k]
                    da_col = da_col + jnp.where(lane == hh, jnp.sum(q, axis=1, keepdims=True), 0.0)
                    darow[hh:hh + 1, :] = -jnp.sum(q, axis=0, keepdims=True)
            dxd = jnp.concatenate(dxd_parts, axis=1) + bds * eeg
            ddt_col = ddt_col + _segsum(dxd * xg, e64g)
            rsum = (jnp.where(sub8 == 0, jnp.sum(wterm, axis=0, keepdims=True), 0.0)
                    + jnp.where(sub8 == 1, jnp.sum(ds * sin, axis=0, keepdims=True), 0.0)
                    + jnp.where(sub8 == 2, jnp.sum(dyf * xg, axis=0, keepdims=True), 0.0))
            rows = rows + _segsum(rsum, e64g)
            dxc_ref[:, gs] = (dxd * dtg + dsk_ref[:, gs] * dyf).astype(dxc_ref.dtype)
            dgb = dg.astype(BF16)
            dxc_ref[:, boff + g * D_STATE:boff + (g + 1) * D_STATE] = (
                dbg + _dot(dgb, cg, 0, 0)).astype(dxc_ref.dtype)
            dxc_ref[:, coff + g * D_STATE:coff + (g + 1) * D_STATE] = (
                dcg + _dot(dgb, bg, 1, 0)).astype(dxc_ref.dtype)
        at_last = rows[0:1, :] + jnp.exp(last) * rows[1:2, :]
        is_last = lax.broadcasted_iota(jnp.int32, (CHUNK, DT_LANES), 0) == CHUNK - 1
        da = da_col + jnp.where(is_last, at_last, 0.0) + darow[...].T
        li = lax.broadcasted_iota(jnp.int32, (CHUNK, CHUNK), 0)
        si = lax.broadcasted_iota(jnp.int32, (CHUNK, CHUNK), 1)
        dla = _dot((si >= li).astype(F32), da, 1, 0, HIGHEST)
        ddtr = (ddt_col + dla * a) * _sigmoid(xx)
        ddtr_ref[...] = ddtr
        sm_ref[0:1, :] += jnp.sum(ddtr, axis=0, keepdims=True)
        sm_ref[1:2, :] += jnp.sum(dla * dt, axis=0, keepdims=True) * a
        sm_ref[2:3, :] += rows[2:3, :]

    small = pl.BlockSpec((1, DT_LANES), lambda c: (0, 0))
    whole = lambda a: pl.BlockSpec(a.shape, lambda c: (0, 0))
    rev = lambda c: (nc - 1 - c, 0)
    return _pc(body, name=name, grid=(nc,),
               in_specs=[pl.BlockSpec((CHUNK, dx), rev), pl.BlockSpec((CHUNK, DT_LANES), rev),
                         pl.BlockSpec((CHUNK, di), rev),
                         pl.BlockSpec((1, D_STATE, di), lambda c: (nc - 1 - c, 0, 0)), small, small,
                         whole(dskx), whole(e64), whole(e128)],
               out_specs=[pl.BlockSpec((CHUNK, dx), rev), pl.BlockSpec((CHUNK, DT_LANES), rev),
                          pl.BlockSpec((8, DT_LANES), lambda c: (0, 0))],
               out_shape=[_sds((t, dx), BF16), _sds((t, DT_LANES), F32), _sds((8, DT_LANES), F32)],
               scratch_shapes=[pltpu.VMEM((D_STATE, di), F32), pltpu.VMEM((DT_LANES, CHUNK), F32)],
               compiler_params=_params())(xc, dtr, dy, states, dtb, alog, dskx, e64, e128)


def _adamw(parts, w, m, v, name):
    npart, rows, width = parts.shape
    if rows * width <= SMALL_PARAM:
        tr, tw = rows, width
    else:
        tr, tw = (_pick(rows, (64, 32, 16, 8)), width) if rows % 8 == 0 else (rows, 128)
    c1 = 1.0 - ADAM_B1 ** ADAM_STEP
    c2 = 1.0 - ADAM_B2 ** ADAM_STEP

    row_strips = _strips(tr) if tr % STRIP == 0 else [slice(0, tr)]
    col_chunks = [slice(c, c + 512) for c in range(0, tw, 512)] if tw % 512 == 0 else [slice(0, tw)]

    def body(p_ref, w_ref, m_ref, v_ref, g_ref, d_ref, nm_ref, nv_ref):
        for rows in row_strips:
            for cols in col_chunks:
                g = p_ref[0, rows, cols].astype(F32)
                for p in range(1, npart):
                    g = g + p_ref[p, rows, cols].astype(F32)
                nm = ADAM_B1 * m_ref[rows, cols] + (1.0 - ADAM_B1) * g
                nv = ADAM_B2 * v_ref[rows, cols] + (1.0 - ADAM_B2) * (g * g)
                g_ref[rows, cols] = g
                nm_ref[rows, cols] = nm
                nv_ref[rows, cols] = nv
                d_ref[rows, cols] = -ADAM_LR * ((nm / c1) / (jnp.sqrt(nv / c2) + ADAM_EPS)
                                                + ADAM_WD * w_ref[rows, cols])

    blk = pl.BlockSpec((tr, tw), lambda i, j: (i, j))
    return _pc(body, name=name, grid=(rows // tr, width // tw),
               in_specs=[pl.BlockSpec((npart, tr, tw), lambda i, j: (0, i, j)), blk, blk, blk],
               out_specs=[blk] * 4, out_shape=[_sds((rows, width), F32)] * 4,
               compiler_params=_params())(parts, w, m, v)


def _sum_parts(parts, name, tile=None):
    npart, rows, width = parts.shape
    tile = rows if tile is None else tile

    def body(p_ref, o_ref):
        for rows_ in _strips(tile, 8 if parts.dtype == F32 else STRIP):
            g = p_ref[0, rows_, :].astype(F32)
            for p in range(1, npart):
                g = g + p_ref[p, rows_, :].astype(F32)
            o_ref[rows_, :] = g

    return _pc(body, name=name, grid=(rows // tile,),
               in_specs=[pl.BlockSpec((npart, tile, width), lambda i: (0, i, 0))],
               out_specs=pl.BlockSpec((tile, width), lambda i: (i, 0)),
               out_shape=_sds((rows, width), F32), compiler_params=_params())(parts)


def _flip(k):
    x, y, c = lax.axis_index("x"), lax.axis_index("y"), lax.axis_index("c")
    px = 1 - x if k & 4 else x
    py = 1 - y if k & 2 else y
    pc = 1 - c if k & 1 else c
    return (px, py, pc), 4 * px + 2 * py + pc


DIRECT = tuple((k, 0) for k in range(1, N_DEV))
TO_CHIPS = ((1, 0), (2, 0), (4, 0), (6, 0))
TO_SIBLING = ((1, 2), (1, 4), (1, 6))


def _copies(arrays, lands, send_sems, recv_sems, scatter, moves):
    _, me = _flip(0)
    outgoing, incoming = [], []
    for i, (kd, kb) in enumerate(moves):
        peer, pidx = _flip(kd)
        _, out_slot = _flip(kb)
        _, in_slot = _flip(kd ^ kb)
        for j, land_ref in enumerate(lands):
            if kb:
                src = land_ref.at[out_slot]
            else:
                src = arrays[j].at[pidx] if scatter[j] else arrays[j]
            sem = len(lands) * i + j
            for dst, bucket in ((land_ref.at[out_slot], outgoing), (land_ref.at[in_slot], incoming)):
                bucket.append(pltpu.make_async_remote_copy(
                    src_ref=src, dst_ref=dst, send_sem=send_sems.at[sem], recv_sem=recv_sems.at[sem],
                    device_id=peer, device_id_type=MESH))
    return outgoing, incoming


HBM_SPEC = pl.BlockSpec(memory_space=pltpu.HBM)
SEM_SPEC = pl.BlockSpec(memory_space=pltpu.SEMAPHORE)
ANY_SPEC = pl.BlockSpec(memory_space=pl.ANY)
EFFECT = pltpu.SideEffectType.DATAFLOW_SIDE_EFFECTING


def _landing_zones(arrays, scatter):
    _, me = _flip(0)
    lands = []
    for a, sc in zip(arrays, scatter):
        own = lax.dynamic_index_in_dim(a, me, 0, keepdims=True) if sc else a[None]
        shape = a.shape if sc else (N_DEV,) + a.shape
        lands.append(lax.dynamic_update_slice(lax.empty(shape, a.dtype), own, (me,) + (0,) * (len(shape) - 1)))
    return lands


def _xchg_start(arrays, scatter, after, name, moves=DIRECT, lands=None):
    if lands is None:
        lands = _landing_zones(arrays, scatter)
    na, nl = len(arrays), len(lands)

    def body(*refs):
        ins, outs = refs[:na + nl], refs[na + nl + 1:]
        outgoing, _ = _copies(ins[:na], ins[na:], outs[0], outs[1], scatter, moves)
        for cp in outgoing:
            cp.start()
        outs[-1][...] = jnp.zeros_like(outs[-1])

    nsem = nl * len(moves)
    operands = [pltpu.with_memory_space_constraint(a, pltpu.HBM) for a in list(arrays) + list(lands)]
    out = _pc(body, name=name,
              out_shape=(pltpu.SemaphoreType.DMA((nsem,)), pltpu.SemaphoreType.DMA((nsem,)),
                         *[pltpu.HBM(a.shape, a.dtype) for a in operands], _sds((8, 128), F32)),
              in_specs=[HBM_SPEC] * (na + nl) + [ANY_SPEC],
              out_specs=(SEM_SPEC, SEM_SPEC, *[HBM_SPEC] * (na + nl), pl.BlockSpec(memory_space=pltpu.VMEM)),
              input_output_aliases={i: 2 + i for i in range(na + nl)},
              compiler_params=pltpu.CompilerParams(has_side_effects=EFFECT))(*operands, after)
    return dict(sems=out[:2], thru=out[2:2 + na + nl], token=out[-1], scatter=scatter, na=na, moves=moves)


def _xchg_wait(handle, after, name):
    na, scatter, moves, thru = handle["na"], handle["scatter"], handle["moves"], handle["thru"]
    n = len(thru)

    def body(*refs):
        ins = refs[:n]
        outgoing, incoming = _copies(ins[:na], ins[na:], refs[n], refs[n + 1], scatter, moves)
        for cp in outgoing:
            cp.wait_send()
        for cp in incoming:
            cp.wait_recv()

    out = _pc(body, name=name, out_shape=tuple(pltpu.HBM(a.shape, a.dtype) for a in thru),
              in_specs=[HBM_SPEC] * n + [SEM_SPEC, SEM_SPEC, ANY_SPEC], out_specs=tuple([HBM_SPEC] * n),
              input_output_aliases={i: i for i in range(n)},
              compiler_params=pltpu.CompilerParams(has_side_effects=EFFECT))(*thru, *handle["sems"], after)
    return out[na:]


def _pack(arrs, width, row_mult):
    flat = jnp.concatenate([a.reshape(-1) for a in arrs])
    n = flat.shape[0]
    rows = -(-n // (width * row_mult)) * row_mult
    return jnp.pad(flat, (0, rows * width - n)).reshape(rows, width)


def _unpack(packed, shapes, lead=None):
    out, off = [], 0
    flat = packed.reshape(-1) if lead is None else packed.reshape(lead, -1)
    for s in shapes:
        n = math.prod(s)
        if lead is None:
            out.append(flat[off:off + n].reshape(s))
        else:
            out.append(flat[:, off:off + n].reshape((lead,) + tuple(s)))
        off += n
    return out


def _blocks_to_cols(blocks):
    nb, rows, n = blocks.shape
    return blocks.transpose(1, 0, 2).reshape(rows, nb * n)


def _pad_rows(a, rows):
    return jnp.pad(a, ((0, rows - a.shape[0]), (0, 0)))


def _pad_lanes(a, lanes):
    return jnp.pad(a, ((0, 0), (0, lanes - a.shape[1])))


REST = ("w_a_out", "w_s_out", "w_o", "w_up", "w_down")
TRANSPOSED = ("w_up", "w_in")
CONVS = ("conv_a_w", "ssd_conv_w", "ffn_conv_w")
REPL = ("norm_mix_w", "ssd_conv_b", "dt_bias", "a_log", "d_skip", "ssd_norm_w", "norm_ffn_w", "ffn_conv_b",
        "final_norm_w")
ORDER = ("norm_mix_w", "w_in", "conv_a_w", "w_a_out", "ssd_conv_w", "ssd_conv_b", "dt_bias", "a_log", "d_skip",
         "ssd_norm_w", "w_s_out", "w_o", "norm_ffn_w", "w_up", "ffn_conv_w", "ffn_conv_b", "w_down", "final_norm_w")


def _as_rows(name, block):
    return block[0].T if name in TRANSPOSED else block[0]


def kernel(x, norm_mix_w, w_in, conv_a_w, w_a_out, ssd_conv_w, ssd_conv_b, dt_bias, a_log, d_skip, ssd_norm_w, w_s_out, w_o, norm_ffn_w, w_up, ffn_conv_w, ffn_conv_b, w_down, final_norm_w, loss_target, m_norm_mix_w, m_w_in, m_conv_a_w, m_w_a_out, m_ssd_conv_w, m_ssd_conv_b, m_dt_bias, m_a_log, m_d_skip, m_ssd_norm_w, m_w_s_out, m_w_o, m_norm_ffn_w, m_w_up, m_ffn_conv_w, m_ffn_conv_b, m_w_down, m_final_norm_w, v_norm_mix_w, v_w_in, v_conv_a_w, v_w_a_out, v_ssd_conv_w, v_ssd_conv_b, v_dt_bias, v_a_log, v_d_skip, v_ssd_norm_w, v_w_s_out, v_w_o, v_norm_ffn_w, v_w_up, v_ffn_conv_w, v_ffn_conv_b, v_w_down, v_final_norm_w):
    wts = dict(norm_mix_w=norm_mix_w, w_in=w_in, conv_a_w=conv_a_w, w_a_out=w_a_out, ssd_conv_w=ssd_conv_w,
               ssd_conv_b=ssd_conv_b, dt_bias=dt_bias, a_log=a_log, d_skip=d_skip, ssd_norm_w=ssd_norm_w,
               w_s_out=w_s_out, w_o=w_o, norm_ffn_w=norm_ffn_w, w_up=w_up, ffn_conv_w=ffn_conv_w,
               ffn_conv_b=ffn_conv_b, w_down=w_down, final_norm_w=final_norm_w)
    mom1 = dict(norm_mix_w=m_norm_mix_w, w_in=m_w_in, conv_a_w=m_conv_a_w, w_a_out=m_w_a_out,
                ssd_conv_w=m_ssd_conv_w, ssd_conv_b=m_ssd_conv_b, dt_bias=m_dt_bias, a_log=m_a_log, d_skip=m_d_skip,
                ssd_norm_w=m_ssd_norm_w, w_s_out=m_w_s_out, w_o=m_w_o, norm_ffn_w=m_norm_ffn_w, w_up=m_w_up,
                ffn_conv_w=m_ffn_conv_w, ffn_conv_b=m_ffn_conv_b, w_down=m_w_down, final_norm_w=m_final_norm_w)
    mom2 = dict(norm_mix_w=v_norm_mix_w, w_in=v_w_in, conv_a_w=v_conv_a_w, w_a_out=v_w_a_out,
                ssd_conv_w=v_ssd_conv_w, ssd_conv_b=v_ssd_conv_b, dt_bias=v_dt_bias, a_log=v_a_log, d_skip=v_d_skip,
                ssd_norm_w=v_ssd_norm_w, w_s_out=v_w_s_out, w_o=v_w_o, norm_ffn_w=v_norm_ffn_w, w_up=v_w_up,
                ffn_conv_w=v_ffn_conv_w, ffn_conv_b=v_ffn_conv_b, w_down=v_w_down, final_norm_w=v_final_norm_w)

    t, d = x.shape[1], x.shape[2]
    di = 2 * d
    nh = di // HEAD_DIM
    dxw = di + 2 * N_GROUPS * D_STATE
    f = w_down.shape[1] * N_DEV
    n_in = w_in.shape[2] * N_DEV
    me = 4 * lax.axis_index("x") + 2 * lax.axis_index("y") + lax.axis_index("c")

    rest_local = [_as_rows(k, wts[k]).astype(BF16) for k in REST]
    nrows = [a.shape[0] for a in rest_local]
    n_blk = w_in.shape[2]
    in_local = w_in[0].T.astype(BF16)
    conv_shapes = [wts[k].shape[1:] for k in CONVS]
    conv_local = _pack([wts[k] for k in CONVS], d, 8)
    x2, tgt = x[0], loss_target[0]
    h_in = _xchg_start([in_local, conv_local], [False, False], x2, "gather_in_start", moves=TO_CHIPS)
    u = _rms_fwd(x2, norm_mix_w, h_in["token"], "norm_mix")
    part = _xchg_wait(h_in, u, "gather_in_wait")
    h_fwd = _xchg_start([], [False, False], u, "gather_in_forward_start", moves=TO_SIBLING, lands=part)
    in_all, conv_all = _xchg_wait(h_fwd, u, "gather_in_forward_wait")
    win_t = in_all.reshape(n_in, d)
    h_rest = _xchg_start(rest_local, [False] * len(REST), in_all, "gather_rest_start")
    c_a, c_s, c_f = _unpack(conv_all, conv_shapes, N_DEV)
    caw, scw, fcw = _blocks_to_cols(c_a), _blocks_to_cols(c_s), _blocks_to_cols(c_f)

    o_z, o_x, o_dt = 5 * d, 7 * d, 7 * d + dxw
    seg_bounds = [0, d, 2 * d, 3 * d, 4 * d, o_z, o_x, o_dt]
    w_dt = _pad_rows(win_t[o_dt:], DT_LANES)
    dtb, alog = (_pad_lanes(p[...].reshape(1, nh), DT_LANES) for p in (dt_bias, a_log))
    dskx = jnp.repeat(d_skip.reshape(1, nh), HEAD_DIM, axis=1)

    tok = h_rest["token"]
    gates = _mm([(u, 0, d, win_t, 0)], "nt", BF16, "proj_gates", n=2 * d, after=tok)
    pa = _mm([(u, 0, d, win_t, 2 * d)], "nt", BF16, "proj_a", n=3 * d, after=tok)
    z = _mm([(u, 0, d, win_t, o_z)], "nt", BF16, "proj_z", n=2 * d, after=tok)
    xbc = _mm([(u, 0, d, win_t, o_x)], "nt", BF16, "proj_xbc", n=dxw, after=tok)
    dtr = _mm([(u, w_dt)], "nt", F32, "proj_dt", after=tok)
    ya_in, q_a = _conv_a_fwd(pa, caw, d, "conv_a")
    xc, pre_s = _conv_s_fwd(xbc, scw, ssd_conv_b, "conv_s")
    y, states = _ssd_fwd(xc, dtr, dtb, alog, dskx, di, "ssd")
    yn = _gnorm_fwd(y, z, ssd_norm_w, "gnorm")
    rest_all = _xchg_wait(h_rest, yn, "gather_rest_wait")
    waout, wsout, wo, wup_t, wdown = (a.reshape(N_DEV * n, d) for a, n in zip(rest_all, nrows))
    y_a = _mm([(ya_in, waout)], "nn", BF16, "a_out")
    y_s = _mm([(yn, wsout)], "nn", BF16, "s_out")
    merged = _merge_fwd(gates, y_a, y_s, d, "merge")
    mo = _mm([(merged, wo)], "nn", BF16, "o_proj")
    h1, v = _resnorm_fwd(x2, mo, norm_ffn_w, "norm_ffn")
    hv = _mm([(v, wup_t)], "nt", BF16, "up_proj")
    act, c1 = _ffn_fwd(hv, fcw, ffn_conv_b, f, "ffn_act")
    dd = _mm([(act, wdown)], "nn", BF16, "down_proj")
    loss11, dh2, dh2b, g_fnw = _final(h1, dd, tgt, final_norm_w.reshape(1, d), "final")

    dact = _mm([(dh2b, wdown)], "nt", BF16, "d_act")
    gw_down = _mm_tn(act, dh2b, "gw_down")
    dh1f, dh3, g_ffn = _ffn_bwd(hv, c1, dact, fcw, f, "ffn_act_bwd")
    dv = _mm([(dh1f, 0, f, wup_t, 0), (dh3, 0, f, wup_t, f)], "nn", BF16, "d_v")
    gw_up_t = jnp.concatenate([_mm_tn(dh1f, v, "gw_up1"), _mm_tn(dh3, v, "gw_up3")], axis=0)
    dh1, dh1b, g_nfw = _rms_bwd(h1, dv, norm_ffn_w, dh2, "norm_ffn_bwd")
    dmerged = _mm([(dh1b, wo)], "nt", BF16, "d_merged")
    gw_o = _mm_tn(merged, dh1b, "gw_o")
    dya, dys, dga, dgs = _merge_bwd(dmerged, gates, y_a, y_s, d, "merge_bwd")
    dyain = _mm([(dya, waout)], "nt", BF16, "d_ya_in")
    gw_aout = _mm_tn(ya_in, dya, "gw_a_out")
    db, dc, dvv, g_caw = _conv_a_bwd(pa, q_a, dyain, caw, d, "conv_a_bwd")
    dyn = _mm([(dys, wsout)], "nt", BF16, "d_yn")
    gw_sout = _mm_tn(yn, dys, "gw_s_out")
    grads_rest = dict(w_a_out=gw_aout, w_s_out=gw_sout, w_o=gw_o, w_up=gw_up_t, w_down=gw_down)
    rest_parts = [grads_rest[k].reshape(N_DEV, n, d) for k, n in zip(REST, nrows)]
    h_grest = _xchg_start(rest_parts, [True] * len(REST), rest_parts[0], "scatter_rest_start")
    dy, dz, g_snw = _gnorm_bwd(y, z, dyn, ssd_norm_w, "gnorm_bwd")
    dtb_after = dtb + h_grest["token"][0:1, 0:1]
    dxc, ddtr, g_ssd = _ssd_bwd(xc, dtr, dy, states, dtb_after, alog, dskx, di, "ssd_bwd")
    dxbc, g_scw = _conv_s_bwd(xbc, pre_s, dxc, scw, "conv_s_bwd")
    dsegs = [dga, dgs, db, dc, dvv, dz, dxbc, ddtr.astype(BF16)]
    pairs = [(s, c, d, win_t, a + c * d) for s, a in zip(dsegs[:-1], seg_bounds) for c in range(s.shape[1] // d)]
    pairs.append((dsegs[-1], w_dt))
    gw_in = [_mm_tn(s, u, "gw_in%d" % i) for i, s in enumerate(dsegs)]
    gw_in_t = jnp.concatenate(gw_in[:-1] + [gw_in[-1][:nh]], axis=0)
    in_parts = gw_in_t.reshape(N_DEV, n_blk, d)
    h_gin = _xchg_start([in_parts], [True], in_parts, "scatter_in_start")
    du = _mm(pairs, "nn", BF16, "d_u", tm=512, tn=512, after=h_gin["token"])
    dx, _, g_nmw = _rms_bwd(x2, du, norm_mix_w, dh1, "norm_mix_bwd")

    small_grads = dict(norm_mix_w=g_nmw[0], ssd_conv_b=g_scw[4], dt_bias=g_ssd[0, :nh], a_log=g_ssd[1, :nh],
                       d_skip=g_ssd[2, :nh], ssd_norm_w=g_snw[0], norm_ffn_w=g_nfw[0], ffn_conv_b=g_ffn[3],
                       final_norm_w=g_fnw[0], conv_a_w=g_caw[:3], ssd_conv_w=g_scw[:4], ffn_conv_w=g_ffn[:3])
    small_names = REPL + CONVS
    small_parts = _pack([small_grads[k] for k in small_names] + [loss11], d, 8)
    h_small = _xchg_start([small_parts], [False], small_parts, "gather_small_start")
    rest_recv = _xchg_wait(h_grest, dx, "scatter_rest_wait")
    (in_recv,) = _xchg_wait(h_gin, rest_recv[0], "scatter_in_wait")
    (small_all,) = _xchg_wait(h_small, in_recv, "gather_small_wait")
    small_sum = _sum_parts(small_all, "sum_small_grads")
    *small_list, loss = _unpack(small_sum, [small_grads[k].shape for k in small_names] + [()])
    small_g = dict(zip(small_names, small_list))

    res = {}

    def update(k, parts):
        outs = _adamw(parts, *(_as_rows(k, src[k]) for src in (wts, mom1, mom2)), "adamw_" + k)
        for kind, a in zip(("g", "d", "m", "v"), outs):
            res[kind, k] = (a.T if k in TRANSPOSED else a)[None]

    update("w_in", in_recv)
    for k, parts in zip(REST, rest_recv):
        update(k, parts)
    local_g = {}
    for k in REPL:
        local_g[k] = small_g[k].reshape(wts[k].shape)
    for k in CONVS:
        n = wts[k].shape[2]
        local_g[k] = lax.dynamic_slice_in_dim(small_g[k], me * n, n, axis=1)[None]
    for k in small_names:
        as2d = lambda a: a.reshape(-1, a.shape[-1])
        outs = _adamw(as2d(local_g[k])[None], *(as2d(src[k]) for src in (wts, mom1, mom2)), "adamw_" + k)
        for kind, a in zip(("g", "d", "m", "v"), outs):
            res[kind, k] = a.reshape(wts[k].shape)

    return (loss, dx[None], *[res["g", k] for k in ORDER], *[res["d", k] for k in ORDER],
            *[res["m", k] for k in ORDER], *[res["v", k] for k in ORDER])
```

```python
import functools
import math

import jax
import jax.numpy as jnp
from jax import lax
from jax.experimental import pallas as pl
from jax.experimental.pallas import tpu as pltpu

F32 = jnp.float32
BF16 = jnp.bfloat16
EPS = 1e-5
HEAD_DIM = 64
N_GROUPS = 4
D_STATE = 128
CHUNK = 128
DT_LANES = 128
HALO = 16
STRIP = 16
SMALL_PARAM = 16 * 1024
N_DEV = 8
V7X_VMEM_LIMIT = 56 * 1024 * 1024
ADAM_LR, ADAM_B1, ADAM_B2, ADAM_EPS, ADAM_WD, ADAM_STEP = 0.001, 0.9, 0.999, 1e-08, 0.01, 10
HIGHEST = lax.Precision.HIGHEST
MESH = pl.DeviceIdType.MESH


def _pc(body, **kw):
    return pl.pallas_call(body, **kw)


def _params():
    return pltpu.CompilerParams(vmem_limit_bytes=V7X_VMEM_LIMIT)


def _pick(n, cands):
    for c in cands:
        if n % c == 0:
            return c
    return n


def _dot(a, b, ca, cb, prec=None):
    return lax.dot_general(a, b, (((ca,), (cb,)), ((), ())), preferred_element_type=F32, precision=prec)


def _sigmoid(x):
    return 0.5 * jnp.tanh(0.5 * x) + 0.5


def _sds(shape, dtype):
    return jax.ShapeDtypeStruct(shape, dtype)


def _mm(pairs, mode, out_dtype, name, n=None, tm=1024, tn=1024, after=None, resident_b=False):
    pairs = [p if len(p) == 5 else (p[0], 0, p[0].shape[1], p[1], 0) for p in pairs]
    m = pairs[0][0].shape[0]
    if n is None:
        n = pairs[0][3].shape[1] if mode == "nn" else pairs[0][3].shape[0]
    tm = min(tm, m)
    rows_nt = [p[4] for p in pairs] if mode == "nt" else []
    tn = next(c for c in (tn, 1408, 512, 256, 128) if n % c == 0 and all(r % c == 0 for r in rows_nt))
    npair = len(pairs)
    cb = 0 if mode == "nn" else 1

    def body(*refs):
        o_ref = refs[-1]
        acc = None
        for p in range(npair):
            part = _dot(refs[2 * p][...], refs[2 * p + 1][...], 1, cb)
            acc = part if acc is None else acc + part
        o_ref[...] = acc.astype(o_ref.dtype)

    in_specs, args = [], []
    for a, a_col, kk, b, b_row in pairs:
        in_specs.append(pl.BlockSpec((tm, kk), lambda i, j, c=a_col: (i, c)))
        if mode == "nn":
            assert b_row % kk == 0 and (not resident_b or n == tn)
            in_specs.append(pl.BlockSpec((kk, tn), lambda i, j, r=b_row // kk: (r, j),
                                         pipeline_mode=pl.Buffered(1) if resident_b else None))
        else:
            in_specs.append(pl.BlockSpec((tn, kk), lambda i, j, r=b_row // tn: (r + j, 0)))
        args += [a, b]
    if after is not None:
        in_specs.append(pl.BlockSpec(memory_space=pl.ANY))
        args.append(after)
    return _pc(body, name=name, grid=(m // tm, n // tn), in_specs=in_specs,
               out_specs=pl.BlockSpec((tm, tn), lambda i, j: (i, j)),
               out_shape=_sds((m, n), out_dtype), compiler_params=_params())(*args)


def _mm_tn(a, b, name, tm=1024):
    m, ka = a.shape
    nb = b.shape[1]
    tm = min(tm, m)
    nm = m // tm
    tk = _pick(ka, (1024, 1408, 512, 256, 128))
    tn = _pick(nb, (1024, 512, 256, 128))

    def body(a_ref, b_ref, o_ref, acc):
        t = pl.program_id(2)

        @pl.when(t == 0)
        def _():
            acc[...] = jnp.zeros_like(acc)
        acc[...] += _dot(a_ref[...], b_ref[...], 0, 0)

        @pl.when(t == nm - 1)
        def _():
            o_ref[...] = acc[...].astype(o_ref.dtype)

    return _pc(body, name=name, grid=(ka // tk, nb // tn, nm),
               in_specs=[pl.BlockSpec((tm, tk), lambda i, j, t: (t, i)),
                         pl.BlockSpec((tm, tn), lambda i, j, t: (t, j))],
               out_specs=pl.BlockSpec((tk, tn), lambda i, j, t: (i, j)),
               out_shape=_sds((ka, nb), BF16), scratch_shapes=[pltpu.VMEM((tk, tn), F32)],
               compiler_params=_params())(a, b)


def _strips(tm, strip=STRIP):
    return [slice(r * strip, (r + 1) * strip) for r in range(tm // strip)]


def _fold8(a):
    out = a[0:8, :]
    for r in range(8, a.shape[0], 8):
        out = out + a[r:r + 8, :]
    return out


def _colsum(a8):
    return jnp.sum(a8, axis=0, keepdims=True)


def _rms_fwd(x, w, after, name):
    t, d = x.shape
    tm = min(512, t)

    def body(x_ref, w_ref, after_ref, o_ref):
        wv = w_ref[...]
        for rows in _strips(tm):
            xv = x_ref[rows, :]
            r = lax.rsqrt(jnp.mean(xv * xv, axis=-1, keepdims=True) + EPS)
            o_ref[rows, :] = (xv * r * wv).astype(o_ref.dtype)

    return _pc(body, name=name, grid=(t // tm,),
               in_specs=[pl.BlockSpec((tm, d), lambda i: (i, 0)), pl.BlockSpec((1, d), lambda i: (0, 0)),
                         pl.BlockSpec(memory_space=pl.ANY)],
               out_specs=pl.BlockSpec((tm, d), lambda i: (i, 0)),
               out_shape=_sds((t, d), BF16), compiler_params=_params())(x, w, after)


def _resnorm_fwd(x, mo, w, name):
    t, d = x.shape
    tm = min(512, t)

    def body(x_ref, mo_ref, w_ref, h_ref, v_ref):
        wv = w_ref[...]
        for rows in _strips(tm):
            h = x_ref[rows, :] + mo_ref[rows, :].astype(F32)
            r = lax.rsqrt(jnp.mean(h * h, axis=-1, keepdims=True) + EPS)
            h_ref[rows, :] = h
            v_ref[rows, :] = (h * r * wv).astype(v_ref.dtype)

    row = pl.BlockSpec((tm, d), lambda i: (i, 0))
    return _pc(body, name=name, grid=(t // tm,),
               in_specs=[row, row, pl.BlockSpec((1, d), lambda i: (0, 0))],
               out_specs=[row, row], out_shape=[_sds((t, d), F32), _sds((t, d), BF16)],
               compiler_params=_params())(x, mo, w)


def _rms_bwd(h, dy, w, dres, name):
    t, d = h.shape
    tm = min(512, t)

    def body(h_ref, dy_ref, w_ref, dres_ref, dx_ref, dxb_ref, dw_ref):
        @pl.when(pl.program_id(0) == 0)
        def _():
            dw_ref[...] = jnp.zeros_like(dw_ref)
        wv = w_ref[...]
        acc = jnp.zeros((8, d), F32)
        for rows in _strips(tm):
            hv = h_ref[rows, :]
            dyv = dy_ref[rows, :].astype(F32)
            r = lax.rsqrt(jnp.mean(hv * hv, axis=-1, keepdims=True) + EPS)
            n = hv * r
            dn = dyv * wv
            acc = acc + _fold8(dyv * n)
            dx = dres_ref[rows, :] + r * (dn - n * jnp.mean(dn * n, axis=-1, keepdims=True))
            dx_ref[rows, :] = dx
            dxb_ref[rows, :] = dx.astype(BF16)
        dw_ref[0:1, :] += _colsum(acc)

    row = pl.BlockSpec((tm, d), lambda i: (i, 0))
    return _pc(body, name=name, grid=(t // tm,),
               in_specs=[row, row, pl.BlockSpec((1, d), lambda i: (0, 0)), row],
               out_specs=[row, row, pl.BlockSpec((8, d), lambda i: (0, 0))],
               out_shape=[_sds((t, d), F32), _sds((t, d), BF16), _sds((8, d), F32)],
               compiler_params=_params())(h, dy, w, dres)


def _final(h1, dd, tgt, w, name):
    t, d = h1.shape
    tm = min(512, t)
    nt = t // tm

    def body(h1_ref, dd_ref, tgt_ref, w_ref, loss_ref, dh_ref, dhb_ref, dw_ref, acc):
        i = pl.program_id(0)

        @pl.when(i == 0)
        def _():
            dw_ref[...] = jnp.zeros_like(dw_ref)
            acc[...] = jnp.zeros_like(acc)
        wv = w_ref[...]
        sq = jnp.zeros((8, d), F32)
        dw = jnp.zeros((8, d), F32)
        for rows in _strips(tm):
            h = h1_ref[rows, :] + dd_ref[rows, :].astype(F32)
            r = lax.rsqrt(jnp.mean(h * h, axis=-1, keepdims=True) + EPS)
            n = h * r
            e = n * wv - tgt_ref[rows, :]
            sq = sq + _fold8(e * e)
            dout = e * (1.0 / d)
            dn = dout * wv
            dw = dw + _fold8(dout * n)
            dh = r * (dn - n * jnp.mean(dn * n, axis=-1, keepdims=True))
            dh_ref[rows, :] = dh
            dhb_ref[rows, :] = dh.astype(BF16)
        acc[...] += _colsum(sq)
        dw_ref[0:1, :] += _colsum(dw)

        @pl.when(i == nt - 1)
        def _():
            loss_ref[...] = jnp.sum(acc[...], axis=-1, keepdims=True) * (0.5 / d)

    row = pl.BlockSpec((tm, d), lambda i: (i, 0))
    return _pc(body, name=name, grid=(nt,),
               in_specs=[row, row, row, pl.BlockSpec((1, d), lambda i: (0, 0))],
               out_specs=[pl.BlockSpec((1, 1), lambda i: (0, 0)), row, row, pl.BlockSpec((8, d), lambda i: (0, 0))],
               out_shape=[_sds((1, 1), F32), _sds((t, d), F32), _sds((t, d), BF16), _sds((8, d), F32)],
               scratch_shapes=[pltpu.VMEM((1, d), F32)], compiler_params=_params())(h1, dd, tgt, w)


def _tile_specs(t, tm, tc, col0):
    th = tm // HALO
    last = t // HALO - 1
    cur = pl.BlockSpec((tm, tc), lambda j, i: (i, col0 + j))
    prev = pl.BlockSpec((HALO, tc), lambda j, i: (jnp.maximum(i * th - 1, 0), col0 + j))
    nxt = pl.BlockSpec((HALO, tc), lambda j, i: (jnp.minimum((i + 1) * th, last), col0 + j))
    return cur, prev, nxt


def _conv_strip(buf, w, k, rows):
    out = None
    for j in range(k):
        term = w[j:j + 1, :] * buf[pl.ds(HALO - (k - 1) + j + rows.start, STRIP), :]
        out = term if out is None else out + term
    return out


def _conv_backward(dbuf, x_strip, emit, w, acc_ref, k, tm, with_bias):
    tc = dbuf.shape[1]
    accs = [jnp.zeros((8, tc), F32) for _ in range(k + int(with_bias))]
    for rows in _strips(tm):
        xs = x_strip(rows)
        dx = None
        for j in range(k):
            ds = dbuf[pl.ds(rows.start + k - 1 - j, STRIP), :]
            term = w[j:j + 1, :] * ds
            dx = term if dx is None else dx + term
            accs[j] = accs[j] + _fold8(ds * xs)
            if with_bias and j == k - 1:
                accs[k] = accs[k] + _fold8(ds)
        emit(rows, dx)
    for j, a in enumerate(accs):
        acc_ref[j:j + 1, :] += _colsum(a)


def _conv_a_fwd(pa, w, d, name):
    t = pa.shape[0]
    tm, tc = min(1024, t), _pick(d, (512, 256, 128))
    nd = d // tc

    def body(b_ref, c_ref, v_ref, cp_ref, vp_ref, w_ref, o_ref, q_ref, buf):
        keep = (pl.program_id(1) > 0).astype(F32)
        buf[0:HALO, :] = cp_ref[...].astype(F32) * vp_ref[...].astype(F32) * keep
        for rows in _strips(tm):
            buf[HALO + rows.start:HALO + rows.stop, :] = c_ref[rows, :].astype(F32) * v_ref[rows, :].astype(F32)
        wv = w_ref[...]
        for rows in _strips(tm):
            q = _conv_strip(buf, wv, 3, rows)
            q_ref[rows, :] = q.astype(q_ref.dtype)
            o_ref[rows, :] = (b_ref[rows, :].astype(F32) * q).astype(o_ref.dtype)

    b_cur, _, _ = _tile_specs(t, tm, tc, 0)
    c_cur, c_prev, _ = _tile_specs(t, tm, tc, nd)
    v_cur, v_prev, _ = _tile_specs(t, tm, tc, 2 * nd)
    return _pc(body, name=name, grid=(nd, t // tm),
               in_specs=[b_cur, c_cur, v_cur, c_prev, v_prev, pl.BlockSpec((3, tc), lambda j, i: (0, j))],
               out_specs=[pl.BlockSpec((tm, tc), lambda j, i: (i, j))] * 2,
               out_shape=[_sds((t, d), BF16)] * 2,
               scratch_shapes=[pltpu.VMEM((tm + HALO, tc), F32)],
               compiler_params=_params())(pa, pa, pa, pa, pa, w)


def _conv_a_bwd(pa, q, dya, w, d, name):
    t = pa.shape[0]
    tm, tc = min(1024, t), _pick(d, (512, 256, 128))
    nd, nt = d // tc, t // tm

    def body(b_ref, c_ref, v_ref, bn_ref, q_ref, g_ref, gn_ref, w_ref, db_ref, dc_ref, dv_ref, acc_ref, dbuf):
        i = pl.program_id(1)

        @pl.when(i == 0)
        def _():
            acc_ref[...] = jnp.zeros_like(acc_ref)
        for rows in _strips(tm):
            g = g_ref[rows, :].astype(F32)
            dbuf[rows, :] = g * b_ref[rows, :].astype(F32)
            db_ref[rows, :] = (g * q_ref[rows, :].astype(F32)).astype(BF16)
        dbuf[tm:tm + HALO, :] = gn_ref[...].astype(F32) * bn_ref[...].astype(F32) * (i < nt - 1).astype(F32)

        def emit(rows, dp):
            dc_ref[rows, :] = (dp * v_ref[rows, :].astype(F32)).astype(BF16)
            dv_ref[rows, :] = (dp * c_ref[rows, :].astype(F32)).astype(BF16)

        _conv_backward(dbuf, lambda rows: c_ref[rows, :].astype(F32) * v_ref[rows, :].astype(F32), emit,
                       w_ref[...], acc_ref, 3, tm, False)

    b_cur, _, b_next = _tile_specs(t, tm, tc, 0)
    c_cur, _, _ = _tile_specs(t, tm, tc, nd)
    v_cur, _, _ = _tile_specs(t, tm, tc, 2 * nd)
    g_cur, _, g_next = _tile_specs(t, tm, tc, 0)
    out = pl.BlockSpec((tm, tc), lambda j, i: (i, j))
    return _pc(body, name=name, grid=(nd, nt),
               in_specs=[b_cur, c_cur, v_cur, b_next, g_cur, g_cur, g_next,
                         pl.BlockSpec((3, tc), lambda j, i: (0, j))],
               out_specs=[out, out, out, pl.BlockSpec((8, tc), lambda j, i: (0, j))],
               out_shape=[_sds((t, d), BF16)] * 3 + [_sds((8, d), F32)],
               scratch_shapes=[pltpu.VMEM((tm + HALO, tc), F32)],
               compiler_params=_params())(pa, pa, pa, pa, q, dya, dya, w)


def _conv_s_fwd(xbc, w, b, name):
    t, dx = xbc.shape
    tm, tc = min(1024, t), _pick(dx, (512, 256, 128))

    def body(x_ref, xp_ref, w_ref, b_ref, o_ref, pre_ref, buf):
        buf[0:HALO, :] = xp_ref[...].astype(F32) * (pl.program_id(1) > 0).astype(F32)
        for rows in _strips(tm):
            buf[HALO + rows.start:HALO + rows.stop, :] = x_ref[rows, :].astype(F32)
        wv, bv = w_ref[...], b_ref[...]
        for rows in _strips(tm):
            pre = _conv_strip(buf, wv, 4, rows) + bv
            pre_ref[rows, :] = pre.astype(pre_ref.dtype)
            o_ref[rows, :] = (pre * _sigmoid(pre)).astype(o_ref.dtype)

    cur, prev, _ = _tile_specs(t, tm, tc, 0)
    return _pc(body, name=name, grid=(dx // tc, t // tm),
               in_specs=[cur, prev, pl.BlockSpec((4, tc), lambda j, i: (0, j)),
                         pl.BlockSpec((1, tc), lambda j, i: (0, j))],
               out_specs=[pl.BlockSpec((tm, tc), lambda j, i: (i, j))] * 2,
               out_shape=[_sds((t, dx), BF16)] * 2,
               scratch_shapes=[pltpu.VMEM((tm + HALO, tc), F32)],
               compiler_params=_params())(xbc, xbc, w, b)


def _dsilu(pre):
    s = _sigmoid(pre)
    return s * (1.0 + pre * (1.0 - s))


def _conv_s_bwd(xbc, pre, dxc, w, name):
    t, dx = xbc.shape
    tm, tc = min(1024, t), _pick(dx, (512, 256, 128))
    nt = t // tm

    def body(x_ref, p_ref, pn_ref, g_ref, gn_ref, w_ref, dx_ref, acc_ref, dbuf):
        i = pl.program_id(1)

        @pl.when(i == 0)
        def _():
            acc_ref[...] = jnp.zeros_like(acc_ref)
        for rows in _strips(tm):
            dbuf[rows, :] = g_ref[rows, :].astype(F32) * _dsilu(p_ref[rows, :].astype(F32))
        dbuf[tm:tm + HALO, :] = (gn_ref[...].astype(F32) * _dsilu(pn_ref[...].astype(F32))
                                 * (i < nt - 1).astype(F32))

        def emit(rows, d_in):
            dx_ref[rows, :] = d_in.astype(BF16)

        _conv_backward(dbuf, lambda rows: x_ref[rows, :].astype(F32), emit, w_ref[...], acc_ref, 4, tm, True)

    cur, _, nxt = _tile_specs(t, tm, tc, 0)
    return _pc(body, name=name, grid=(dx // tc, nt),
               in_specs=[cur, cur, nxt, cur, nxt, pl.BlockSpec((4, tc), lambda j, i: (0, j))],
               out_specs=[pl.BlockSpec((tm, tc), lambda j, i: (i, j)), pl.BlockSpec((8, tc), lambda j, i: (0, j))],
               out_shape=[_sds((t, dx), BF16), _sds((8, dx), F32)],
               scratch_shapes=[pltpu.VMEM((tm + HALO, tc), F32)],
               compiler_params=_params())(xbc, pre, pre, dxc, dxc, w)


def _ffn_fwd(hv, w, b, f, name):
    t = hv.shape[0]
    tm, tc = min(1024, t), _pick(f, (512, 256, 128))
    nf = f // tc

    def body(h1_ref, h1p_ref, h3_ref, w_ref, b_ref, o_ref, c1_ref, buf):
        buf[0:HALO, :] = h1p_ref[...].astype(F32) * (pl.program_id(1) > 0).astype(F32)
        for rows in _strips(tm):
            buf[HALO + rows.start:HALO + rows.stop, :] = h1_ref[rows, :].astype(F32)
        wv, bv = w_ref[...], b_ref[...]
        for rows in _strips(tm):
            c1 = _conv_strip(buf, wv, 3, rows) + bv
            c1_ref[rows, :] = c1.astype(c1_ref.dtype)
            o_ref[rows, :] = (c1 * _sigmoid(c1) * h3_ref[rows, :].astype(F32)).astype(o_ref.dtype)

    h1_cur, h1_prev, _ = _tile_specs(t, tm, tc, 0)
    h3_cur, _, _ = _tile_specs(t, tm, tc, nf)
    return _pc(body, name=name, grid=(nf, t // tm),
               in_specs=[h1_cur, h1_prev, h3_cur, pl.BlockSpec((3, tc), lambda j, i: (0, j)),
                         pl.BlockSpec((1, tc), lambda j, i: (0, j))],
               out_specs=[pl.BlockSpec((tm, tc), lambda j, i: (i, j))] * 2,
               out_shape=[_sds((t, f), BF16)] * 2,
               scratch_shapes=[pltpu.VMEM((tm + HALO, tc), F32)],
               compiler_params=_params())(hv, hv, hv, w, b)


def _ffn_bwd(hv, c1, dact, w, f, name):
    t = hv.shape[0]
    tm, tc = min(1024, t), _pick(f, (512, 256, 128))
    nf, nt = f // tc, t // tm

    def body(h1_ref, h3_ref, h3n_ref, c_ref, cn_ref, g_ref, gn_ref, w_ref, dh1_ref, dh3_ref, acc_ref, dbuf):
        i = pl.program_id(1)

        @pl.when(i == 0)
        def _():
            acc_ref[...] = jnp.zeros_like(acc_ref)
        for rows in _strips(tm):
            c1v, g = c_ref[rows, :].astype(F32), g_ref[rows, :].astype(F32)
            s1 = _sigmoid(c1v)
            dh3_ref[rows, :] = (g * c1v * s1).astype(BF16)
            dbuf[rows, :] = g * h3_ref[rows, :].astype(F32) * s1 * (1.0 + c1v * (1.0 - s1))
        dbuf[tm:tm + HALO, :] = (gn_ref[...].astype(F32) * h3n_ref[...].astype(F32)
                                 * _dsilu(cn_ref[...].astype(F32)) * (i < nt - 1).astype(F32))

        def emit(rows, d_in):
            dh1_ref[rows, :] = d_in.astype(BF16)

        _conv_backward(dbuf, lambda rows: h1_ref[rows, :].astype(F32), emit, w_ref[...], acc_ref, 3, tm, True)

    h1_cur, _, _ = _tile_specs(t, tm, tc, 0)
    h3_cur, _, h3_next = _tile_specs(t, tm, tc, nf)
    g_cur, _, g_next = _tile_specs(t, tm, tc, 0)
    out = pl.BlockSpec((tm, tc), lambda j, i: (i, j))
    return _pc(body, name=name, grid=(nf, nt),
               in_specs=[h1_cur, h3_cur, h3_next, g_cur, g_next, g_cur, g_next,
                         pl.BlockSpec((3, tc), lambda j, i: (0, j))],
               out_specs=[out, out, pl.BlockSpec((8, tc), lambda j, i: (0, j))],
               out_shape=[_sds((t, f), BF16), _sds((t, f), BF16), _sds((8, f), F32)],
               scratch_shapes=[pltpu.VMEM((tm + HALO, tc), F32)],
               compiler_params=_params())(hv, hv, hv, c1, c1, dact, dact, w)


def _gnorm_fwd(y, z, w, name):
    t, di = y.shape
    gw = di // N_GROUPS
    tm = min(1024, t)

    def body(y_ref, z_ref, w_ref, o_ref):
        wv = w_ref[...]
        for rows in _strips(tm):
            zv = z_ref[rows, :].astype(F32)
            yz = y_ref[rows, :].astype(F32) * zv * _sigmoid(zv)
            r = lax.rsqrt(jnp.mean(yz * yz, axis=-1, keepdims=True) + EPS)
            o_ref[rows, :] = (yz * r * wv).astype(o_ref.dtype)

    blk = pl.BlockSpec((tm, gw), lambda j, i: (i, j))
    return _pc(body, name=name, grid=(N_GROUPS, t // tm),
               in_specs=[blk, blk, pl.BlockSpec((1, gw), lambda j, i: (0, j))],
               out_specs=blk, out_shape=_sds((t, di), BF16), compiler_params=_params())(y, z, w)


def _gnorm_bwd(y, z, dyn, w, name):
    t, di = y.shape
    gw = di // N_GROUPS
    tm = min(1024, t)

    def body(y_ref, z_ref, g_ref, w_ref, dy_ref, dz_ref, dw_ref):
        @pl.when(pl.program_id(1) == 0)
        def _():
            dw_ref[...] = jnp.zeros_like(dw_ref)
        wv = w_ref[...]
        acc = jnp.zeros((8, gw), F32)
        for rows in _strips(tm):
            yv, zv, g = y_ref[rows, :].astype(F32), z_ref[rows, :].astype(F32), g_ref[rows, :].astype(F32)
            s = _sigmoid(zv)
            sz = zv * s
            yz = yv * sz
            r = lax.rsqrt(jnp.mean(yz * yz, axis=-1, keepdims=True) + EPS)
            n = yz * r
            dn = g * wv
            acc = acc + _fold8(g * n)
            dyz = r * (dn - n * jnp.mean(dn * n, axis=-1, keepdims=True))
            dy_ref[rows, :] = (dyz * sz).astype(BF16)
            dz_ref[rows, :] = (dyz * yv * s * (1.0 + zv * (1.0 - s))).astype(BF16)
        dw_ref[0:1, :] += _colsum(acc)

    blk = pl.BlockSpec((tm, gw), lambda j, i: (i, j))
    return _pc(body, name=name, grid=(N_GROUPS, t // tm),
               in_specs=[blk, blk, blk, pl.BlockSpec((1, gw), lambda j, i: (0, j))],
               out_specs=[blk, blk, pl.BlockSpec((8, gw), lambda j, i: (0, j))],
               out_shape=[_sds((t, di), BF16), _sds((t, di), BF16), _sds((8, di), F32)],
               compiler_params=_params())(y, z, dyn, w)


def _merge_fwd(gates, ya, ys, d, name):
    t = ya.shape[0]
    tm, tc = min(1024, t), _pick(d, (512, 256, 128))
    nd = d // tc

    def body(ga_ref, gs_ref, ya_ref, ys_ref, o_ref):
        for rows in _strips(tm):
            o_ref[rows, :] = (_sigmoid(ga_ref[rows, :].astype(F32)) * ya_ref[rows, :].astype(F32)
                              + _sigmoid(gs_ref[rows, :].astype(F32)) * ys_ref[rows, :].astype(F32)
                              ).astype(o_ref.dtype)

    blk = pl.BlockSpec((tm, tc), lambda j, i: (i, j))
    return _pc(body, name=name, grid=(nd, t // tm),
               in_specs=[blk, pl.BlockSpec((tm, tc), lambda j, i: (i, nd + j)), blk, blk],
               out_specs=blk, out_shape=_sds((t, d), BF16), compiler_params=_params())(gates, gates, ya, ys)


def _merge_bwd(dm, gates, ya, ys, d, name):
    t = ya.shape[0]
    tm, tc = min(1024, t), _pick(d, (512, 256, 128))
    nd = d // tc

    def body(dm_ref, ga_ref, gs_ref, ya_ref, ys_ref, dya_ref, dys_ref, dga_ref, dgs_ref):
        for rows in _strips(tm):
            g = dm_ref[rows, :].astype(F32)
            sa, ss = _sigmoid(ga_ref[rows, :].astype(F32)), _sigmoid(gs_ref[rows, :].astype(F32))
            dya_ref[rows, :] = (g * sa).astype(BF16)
            dys_ref[rows, :] = (g * ss).astype(BF16)
            dga_ref[rows, :] = (g * ya_ref[rows, :].astype(F32) * sa * (1.0 - sa)).astype(BF16)
            dgs_ref[rows, :] = (g * ys_ref[rows, :].astype(F32) * ss * (1.0 - ss)).astype(BF16)

    blk = pl.BlockSpec((tm, tc), lambda j, i: (i, j))
    return _pc(body, name=name, grid=(nd, t // tm),
               in_specs=[blk, blk, pl.BlockSpec((tm, tc), lambda j, i: (i, nd + j)), blk, blk],
               out_specs=[blk] * 4, out_shape=[_sds((t, d), BF16)] * 4,
               compiler_params=_params())(dm, gates, gates, ya, ys)


def _ssd_chunk_terms(dtr, dtb, alog):
    xx = dtr + dtb
    dt = jnp.maximum(xx, 0.0) + jnp.log(1.0 + jnp.exp(-jnp.abs(xx)))
    a = -jnp.exp(alog)
    li = lax.broadcasted_iota(jnp.int32, (CHUNK, CHUNK), 0)
    si = lax.broadcasted_iota(jnp.int32, (CHUNK, CHUNK), 1)
    causal = li >= si
    acum = _dot(causal.astype(F32), dt * a, 1, 0, HIGHEST)
    return xx, dt, a, acum, acum.T, causal


def _split2(x):
    hi = x.astype(BF16)
    return hi, (x - hi.astype(F32)).astype(BF16)


def _expand(v, e, exact=True):
    hi, lo = _split2(v)
    out = _dot(hi, e, 1, 0)
    return out + _dot(lo, e, 1, 0) if exact else out


def _segsum(s, e):
    hi, lo = _split2(s)
    return _dot(hi, e, 1, 1) + _dot(lo, e, 1, 1)


def _head_maps(di):
    nh = di // HEAD_DIM
    h = jnp.arange(DT_LANES)[:, None]
    e64 = (jnp.arange(di)[None, :] // HEAD_DIM == h).astype(BF16)
    e128 = (jnp.arange(nh * CHUNK)[None, :] // CHUNK == h).astype(BF16)
    return e64, e128


def _pair_blockdiag(p, left):
    zero = jnp.zeros_like(p)
    return jnp.concatenate([jnp.where(left, p, zero), jnp.where(left, zero, p)], axis=0)


def _ssd_fwd(xc, dtr, dtb, alog, dskx, di, name):
    t = xc.shape[0]
    dx = xc.shape[1]
    nc = t // CHUNK
    nh = di // HEAD_DIM
    hpg = nh // N_GROUPS
    gw = hpg * HEAD_DIM
    boff, coff = di, di + N_GROUPS * D_STATE
    e64, e128 = _head_maps(di)

    def body(xc_ref, dtr_ref, dtb_ref, alog_ref, dsk_ref, e64_ref, e128_ref, y_ref, st_ref, state):
        @pl.when(pl.program_id(0) == 0)
        def _():
            state[...] = jnp.zeros_like(state)
        _, dt, _, acum, acum_t, causal = _ssd_chunk_terms(dtr_ref[...], dtb_ref[...], alog_ref[...])
        last = acum[CHUNK - 1:CHUNK, :]
        e64v = e64_ref[...]
        dtx = _expand(dt, e64v, False)
        eax = _expand(jnp.exp(acum), e64v)
        dex = _expand(dt * jnp.exp(last - acum), e64v, False)
        acx = _expand(acum, e128_ref[...])
        st_ref[0] = state[...]
        left = lax.broadcasted_iota(jnp.int32, (CHUNK, 2 * HEAD_DIM), 1) < HEAD_DIM
        for g in range(N_GROUPS):
            gs = slice(g * gw, (g + 1) * gw)
            bg = xc_ref[:, boff + g * D_STATE:boff + (g + 1) * D_STATE]
            cg = xc_ref[:, coff + g * D_STATE:coff + (g + 1) * D_STATE]
            gm = _dot(cg, bg, 1, 1)
            xg = xc_ref[:, gs].astype(F32)
            xdb = (xg * dtx[:, gs]).astype(BF16)
            sin = state[:, gs]
            yo = _dot(cg, sin.astype(BF16), 1, 0) * eax[:, gs]
            for jp in range(hpg // 2):
                h0 = g * hpg + 2 * jp
                ps = slice(jp * 2 * HEAD_DIM, (jp + 1) * 2 * HEAD_DIM)
                ms = []
                for hh in (h0, h0 + 1):
                    seg = acx[:, hh * CHUNK:(hh + 1) * CHUNK] - acum_t[hh:hh + 1, :]
                    ms.append((gm * jnp.exp(jnp.where(causal, seg, -1e30))).astype(BF16))
                yd = _dot(jnp.concatenate(ms, axis=1), _pair_blockdiag(xdb[:, ps], left), 1, 0)
                col = slice(g * gw + jp * 2 * HEAD_DIM, g * gw + (jp + 1) * 2 * HEAD_DIM)
                y_ref[:, col] = (yd + yo[:, ps] + dsk_ref[:, col] * xg[:, ps]).astype(y_ref.dtype)
            xe = (xg * dex[:, gs]).astype(BF16)
            state[:, gs] = eax[CHUNK - 1:CHUNK, gs] * sin + _dot(bg, xe, 0, 0)

    small = pl.BlockSpec((1, DT_LANES), lambda c: (0, 0))
    whole = lambda a: pl.BlockSpec(a.shape, lambda c: (0, 0))
    return _pc(body, name=name, grid=(nc,),
               in_specs=[pl.BlockSpec((CHUNK, dx), lambda c: (c, 0)),
                         pl.BlockSpec((CHUNK, DT_LANES), lambda c: (c, 0)), small, small,
                         whole(dskx), whole(e64), whole(e128)],
               out_specs=[pl.BlockSpec((CHUNK, di), lambda c: (c, 0)),
                          pl.BlockSpec((1, D_STATE, di), lambda c: (c, 0, 0))],
               out_shape=[_sds((t, di), BF16), _sds((nc, D_STATE, di), F32)],
               scratch_shapes=[pltpu.VMEM((D_STATE, di), F32)],
               compiler_params=_params())(xc, dtr, dtb, alog, dskx, e64, e128)


def _ssd_bwd(xc, dtr, dy, states, dtb, alog, dskx, di, name):
    t = xc.shape[0]
    dx = xc.shape[1]
    nc = t // CHUNK
    nh = di // HEAD_DIM
    hpg = nh // N_GROUPS
    gw = hpg * HEAD_DIM
    boff, coff = di, di + N_GROUPS * D_STATE
    e64, e128 = _head_maps(di)

    def body(xc_ref, dtr_ref, dy_ref, st_ref, dtb_ref, alog_ref, dsk_ref, e64_ref, e128_ref,
             dxc_ref, ddtr_ref, sm_ref, dstate, darow):
        @pl.when(pl.program_id(0) == 0)
        def _():
            dstate[...] = jnp.zeros_like(dstate)
            sm_ref[...] = jnp.zeros_like(sm_ref)
        darow[...] = jnp.zeros_like(darow)
        xx, dt, a, acum, acum_t, causal = _ssd_chunk_terms(dtr_ref[...], dtb_ref[...], alog_ref[...])
        last = acum[CHUNK - 1:CHUNK, :]
        e64v = e64_ref[...]
        dtx = _expand(dt, e64v, False)
        eax = _expand(jnp.exp(acum), e64v)
        eex = _expand(jnp.exp(last - acum), e64v, False)
        acx = _expand(acum, e128_ref[...])
        left = lax.broadcasted_iota(jnp.int32, (CHUNK, 2 * HEAD_DIM), 1) < HEAD_DIM
        lane = lax.broadcasted_iota(jnp.int32, (CHUNK, DT_LANES), 1)
        sub8 = lax.broadcasted_iota(jnp.int32, (8, gw), 0)
        da_col = jnp.zeros((CHUNK, DT_LANES), F32)
        ddt_col = jnp.zeros((CHUNK, DT_LANES), F32)
        rows = jnp.zeros((8, DT_LANES), F32)
        for g in range(N_GROUPS):
            gs = slice(g * gw, (g + 1) * gw)
            bg = xc_ref[:, boff + g * D_STATE:boff + (g + 1) * D_STATE]
            cg = xc_ref[:, coff + g * D_STATE:coff + (g + 1) * D_STATE]
            gm = _dot(cg, bg, 1, 1)
            e64g = e64v[:, gs]
            xg = xc_ref[:, gs].astype(F32)
            dtg, eag, eeg = dtx[:, gs], eax[:, gs], eex[:, gs]
            xd = xg * dtg
            xdb = xd.astype(BF16)
            dyb = dy_ref[:, gs]
            dyf = dyb.astype(F32)
            sin = st_ref[0, :, gs]
            sinb = sin.astype(BF16)
            ds = dstate[:, gs]
            dsb = ds.astype(BF16)
            bds = _dot(bg, dsb, 1, 0)
            dyeb = (dyf * eag).astype(BF16)
            dcg = _dot(dyeb, sinb, 1, 1)
            dstate[:, gs] = eag[CHUNK - 1:CHUNK, :] * ds + _dot(cg, dyeb, 0, 0)
            yo = _dot(cg, sinb, 1, 0) * eag
            xe = xd * eeg
            dbg = _dot(xe.astype(BF16), dsb, 1, 1)
            wterm = bds * xe
            da_col = da_col + _segsum(dyf * yo - wterm, e64g)
            dg = jnp.zeros((CHUNK, CHUNK), F32)
            dxd_parts = []
            for jp in range(hpg // 2):
                h0 = g * hpg + 2 * jp
                ps = slice(jp * 2 * HEAD_DIM, (jp + 1) * 2 * HEAD_DIM)
                lms, mfs = [], []
                for hh in (h0, h0 + 1):
                    seg = acx[:, hh * CHUNK:(hh + 1) * CHUNK] - acum_t[hh:hh + 1, :]
                    lm = jnp.exp(jnp.where(causal, seg, -1e30))
                    lms.append(lm)
                    mfs.append(gm * lm)
                mstack = jnp.concatenate([m.astype(BF16) for m in mfs], axis=0)
                dyp = dyb[:, ps]
                dxd_parts.append(_dot(mstack, _pair_blockdiag(dyp, left), 0, 0))
                dm2 = _dot(dyp, _pair_blockdiag(xdb[:, ps], left), 1, 1)
                for k, hh in enumerate((h0, h0 + 1)):
                    dm = dm2[:, k * CHUNK:(k + 1) * CHUNK]
                    dg = dg + dm * lms[k]
                    q = dm * mfs[k]
                    da_col = da_col + jnp.where(lane == hh, jnp.sum(q, axis=1, keepdims=True), 0.0)
                    darow[hh:hh + 1, :] = -jnp.sum(q, axis=0, keepdims=True)
            dxd = jnp.concatenate(dxd_parts, axis=1) + bds * eeg
            ddt_col = ddt_col + _segsum(dxd * xg, e64g)
            rsum = (jnp.where(sub8 == 0, jnp.sum(wterm, axis=0, keepdims=True), 0.0)
                    + jnp.where(sub8 == 1, jnp.sum(ds * sin, axis=0, keepdims=True), 0.0)
                    + jnp.where(sub8 == 2, jnp.sum(dyf * xg, axis=0, keepdims=True), 0.0))
            rows = rows + _segsum(rsum, e64g)
            dxc_ref[:, gs] = (dxd * dtg + dsk_ref[:, gs] * dyf).astype(dxc_ref.dtype)
            dgb = dg.astype(BF16)
            dxc_ref[:, boff + g * D_STATE:boff + (g + 1) * D_STATE] = (
                dbg + _dot(dgb, cg, 0, 0)).astype(dxc_ref.dtype)
            dxc_ref[:, coff + g * D_STATE:coff + (g + 1) * D_STATE] = (
                dcg + _dot(dgb, bg, 1, 0)).astype(dxc_ref.dtype)
        at_last = rows[0:1, :] + jnp.exp(last) * rows[1:2, :]
        is_last = lax.broadcasted_iota(jnp.int32, (CHUNK, DT_LANES), 0) == CHUNK - 1
        da = da_col + jnp.where(is_last, at_last, 0.0) + darow[...].T
        li = lax.broadcasted_iota(jnp.int32, (CHUNK, CHUNK), 0)
        si = lax.broadcasted_iota(jnp.int32, (CHUNK, CHUNK), 1)
        dla = _dot((si >= li).astype(F32), da, 1, 0, HIGHEST)
        ddtr = (ddt_col + dla * a) * _sigmoid(xx)
        ddtr_ref[...] = ddtr
        sm_ref[0:1, :] += jnp.sum(ddtr, axis=0, keepdims=True)
        sm_ref[1:2, :] += jnp.sum(dla * dt, axis=0, keepdims=True) * a
        sm_ref[2:3, :] += rows[2:3, :]

    small = pl.BlockSpec((1, DT_LANES), lambda c: (0, 0))
    whole = lambda a: pl.BlockSpec(a.shape, lambda c: (0, 0))
    rev = lambda c: (nc - 1 - c, 0)
    return _pc(body, name=name, grid=(nc,),
               in_specs=[pl.BlockSpec((CHUNK, dx), rev), pl.BlockSpec((CHUNK, DT_LANES), rev),
                         pl.BlockSpec((CHUNK, di), rev),
                         pl.BlockSpec((1, D_STATE, di), lambda c: (nc - 1 - c, 0, 0)), small, small,
                         whole(dskx), whole(e64), whole(e128)],
               out_specs=[pl.BlockSpec((CHUNK, dx), rev), pl.BlockSpec((CHUNK, DT_LANES), rev),
                          pl.BlockSpec((8, DT_LANES), lambda c: (0, 0))],
               out_shape=[_sds((t, dx), BF16), _sds((t, DT_LANES), F32), _sds((8, DT_LANES), F32)],
               scratch_shapes=[pltpu.VMEM((D_STATE, di), F32), pltpu.VMEM((DT_LANES, CHUNK), F32)],
               compiler_params=_params())(xc, dtr, dy, states, dtb, alog, dskx, e64, e128)


def _adamw(parts, w, m, v, name):
    npart, rows, width = parts.shape
    if rows * width <= SMALL_PARAM:
        tr, tw = rows, width
    else:
        tr, tw = (_pick(rows, (64, 32, 16, 8)), width) if rows % 8 == 0 else (rows, 128)
    c1 = 1.0 - ADAM_B1 ** ADAM_STEP
    c2 = 1.0 - ADAM_B2 ** ADAM_STEP

    row_strips = _strips(tr) if tr % STRIP == 0 else [slice(0, tr)]
    col_chunks = [slice(c, c + 512) for c in range(0, tw, 512)] if tw % 512 == 0 else [slice(0, tw)]

    def body(p_ref, w_ref, m_ref, v_ref, g_ref, d_ref, nm_ref, nv_ref):
        for rows in row_strips:
            for cols in col_chunks:
                g = p_ref[0, rows, cols].astype(F32)
                for p in range(1, npart):
                    g = g + p_ref[p, rows, cols].astype(F32)
                nm = ADAM_B1 * m_ref[rows, cols] + (1.0 - ADAM_B1) * g
                nv = ADAM_B2 * v_ref[rows, cols] + (1.0 - ADAM_B2) * (g * g)
                g_ref[rows, cols] = g
                nm_ref[rows, cols] = nm
                nv_ref[rows, cols] = nv
                d_ref[rows, cols] = -ADAM_LR * ((nm / c1) / (jnp.sqrt(nv / c2) + ADAM_EPS)
                                                + ADAM_WD * w_ref[rows, cols])

    blk = pl.BlockSpec((tr, tw), lambda i, j: (i, j))
    return _pc(body, name=name, grid=(rows // tr, width // tw),
               in_specs=[pl.BlockSpec((npart, tr, tw), lambda i, j: (0, i, j)), blk, blk, blk],
               out_specs=[blk] * 4, out_shape=[_sds((rows, width), F32)] * 4,
               compiler_params=_params())(parts, w, m, v)


def _sum_parts(parts, name, tile=None):
    npart, rows, width = parts.shape
    tile = rows if tile is None else tile

    def body(p_ref, o_ref):
        for rows_ in _strips(tile, 8 if parts.dtype == F32 else STRIP):
            g = p_ref[0, rows_, :].astype(F32)
            for p in range(1, npart):
                g = g + p_ref[p, rows_, :].astype(F32)
            o_ref[rows_, :] = g

    return _pc(body, name=name, grid=(rows // tile,),
               in_specs=[pl.BlockSpec((npart, tile, width), lambda i: (0, i, 0))],
               out_specs=pl.BlockSpec((tile, width), lambda i: (i, 0)),
               out_shape=_sds((rows, width), F32), compiler_params=_params())(parts)


def _flip(k):
    x, y, c = lax.axis_index("x"), lax.axis_index("y"), lax.axis_index("c")
    px = 1 - x if k & 4 else x
    py = 1 - y if k & 2 else y
    pc = 1 - c if k & 1 else c
    return (px, py, pc), 4 * px + 2 * py + pc


DIRECT = tuple((k, 0) for k in range(1, N_DEV))
TO_CHIPS = ((1, 0), (2, 0), (4, 0), (6, 0))
TO_SIBLING = ((1, 2), (1, 4), (1, 6))


def _copies(arrays, lands, send_sems, recv_sems, scatter, moves):
    _, me = _flip(0)
    outgoing, incoming = [], []
    for i, (kd, kb) in enumerate(moves):
        peer, pidx = _flip(kd)
        _, out_slot = _flip(kb)
        _, in_slot = _flip(kd ^ kb)
        for j, land_ref in enumerate(lands):
            if kb:
                src = land_ref.at[out_slot]
            else:
                src = arrays[j].at[pidx] if scatter[j] else arrays[j]
            sem = len(lands) * i + j
            for dst, bucket in ((land_ref.at[out_slot], outgoing), (land_ref.at[in_slot], incoming)):
                bucket.append(pltpu.make_async_remote_copy(
                    src_ref=src, dst_ref=dst, send_sem=send_sems.at[sem], recv_sem=recv_sems.at[sem],
                    device_id=peer, device_id_type=MESH))
    return outgoing, incoming


HBM_SPEC = pl.BlockSpec(memory_space=pltpu.HBM)
SEM_SPEC = pl.BlockSpec(memory_space=pltpu.SEMAPHORE)
ANY_SPEC = pl.BlockSpec(memory_space=pl.ANY)
EFFECT = pltpu.SideEffectType.DATAFLOW_SIDE_EFFECTING


def _landing_zones(arrays, scatter):
    _, me = _flip(0)
    lands = []
    for a, sc in zip(arrays, scatter):
        own = lax.dynamic_index_in_dim(a, me, 0, keepdims=True) if sc else a[None]
        shape = a.shape if sc else (N_DEV,) + a.shape
        lands.append(lax.dynamic_update_slice(lax.empty(shape, a.dtype), own, (me,) + (0,) * (len(shape) - 1)))
    return lands


def _xchg_start(arrays, scatter, after, name, moves=DIRECT, lands=None):
    if lands is None:
        lands = _landing_zones(arrays, scatter)
    na, nl = len(arrays), len(lands)

    def body(*refs):
        ins, outs = refs[:na + nl], refs[na + nl + 1:]
        outgoing, _ = _copies(ins[:na], ins[na:], outs[0], outs[1], scatter, moves)
        for cp in outgoing:
            cp.start()
        outs[-1][...] = jnp.zeros_like(outs[-1])

    nsem = nl * len(moves)
    operands = [pltpu.with_memory_space_constraint(a, pltpu.HBM) for a in list(arrays) + list(lands)]
    out = _pc(body, name=name,
              out_shape=(pltpu.SemaphoreType.DMA((nsem,)), pltpu.SemaphoreType.DMA((nsem,)),
                         *[pltpu.HBM(a.shape, a.dtype) for a in operands], _sds((8, 128), F32)),
              in_specs=[HBM_SPEC] * (na + nl) + [ANY_SPEC],
              out_specs=(SEM_SPEC, SEM_SPEC, *[HBM_SPEC] * (na + nl), pl.BlockSpec(memory_space=pltpu.VMEM)),
              input_output_aliases={i: 2 + i for i in range(na + nl)},
              compiler_params=pltpu.CompilerParams(has_side_effects=EFFECT))(*operands, after)
    return dict(sems=out[:2], thru=out[2:2 + na + nl], token=out[-1], scatter=scatter, na=na, moves=moves)


def _xchg_wait(handle, after, name):
    na, scatter, moves, thru = handle["na"], handle["scatter"], handle["moves"], handle["thru"]
    n = len(thru)

    def body(*refs):
        ins = refs[:n]
        outgoing, incoming = _copies(ins[:na], ins[na:], refs[n], refs[n + 1], scatter, moves)
        for cp in outgoing:
            cp.wait_send()
        for cp in incoming:
            cp.wait_recv()

    out = _pc(body, name=name, out_shape=tuple(pltpu.HBM(a.shape, a.dtype) for a in thru),
              in_specs=[HBM_SPEC] * n + [SEM_SPEC, SEM_SPEC, ANY_SPEC], out_specs=tuple([HBM_SPEC] * n),
              input_output_aliases={i: i for i in range(n)},
              compiler_params=pltpu.CompilerParams(has_side_effects=EFFECT))(*thru, *handle["sems"], after)
    return out[na:]


def _pack(arrs, width, row_mult):
    flat = jnp.concatenate([a.reshape(-1) for a in arrs])
    n = flat.shape[0]
    rows = -(-n // (width * row_mult)) * row_mult
    return jnp.pad(flat, (0, rows * width - n)).reshape(rows, width)


def _unpack(packed, shapes, lead=None):
    out, off = [], 0
    flat = packed.reshape(-1) if lead is None else packed.reshape(lead, -1)
    for s in shapes:
        n = math.prod(s)
        if lead is None:
            out.append(flat[off:off + n].reshape(s))
        else:
            out.append(flat[:, off:off + n].reshape((lead,) + tuple(s)))
        off += n
    return out


def _blocks_to_cols(blocks):
    nb, rows, n = blocks.shape
    return blocks.transpose(1, 0, 2).reshape(rows, nb * n)


def _pad_rows(a, rows):
    return jnp.pad(a, ((0, rows - a.shape[0]), (0, 0)))


def _pad_lanes(a, lanes):
    return jnp.pad(a, ((0, 0), (0, lanes - a.shape[1])))


REST = ("w_a_out", "w_s_out", "w_o", "w_up", "w_down")
TRANSPOSED = ("w_up", "w_in")
CONVS = ("conv_a_w", "ssd_conv_w", "ffn_conv_w")
REPL = ("norm_mix_w", "ssd_conv_b", "dt_bias", "a_log", "d_skip", "ssd_norm_w", "norm_ffn_w", "ffn_conv_b",
        "final_norm_w")
ORDER = ("norm_mix_w", "w_in", "conv_a_w", "w_a_out", "ssd_conv_w", "ssd_conv_b", "dt_bias", "a_log", "d_skip",
         "ssd_norm_w", "w_s_out", "w_o", "norm_ffn_w", "w_up", "ffn_conv_w", "ffn_conv_b", "w_down", "final_norm_w")


def _as_rows(name, block):
    return block[0].T if name in TRANSPOSED else block[0]


def kernel(x, norm_mix_w, w_in, conv_a_w, w_a_out, ssd_conv_w, ssd_conv_b, dt_bias, a_log, d_skip, ssd_norm_w, w_s_out, w_o, norm_ffn_w, w_up, ffn_conv_w, ffn_conv_b, w_down, final_norm_w, loss_target, m_norm_mix_w, m_w_in, m_conv_a_w, m_w_a_out, m_ssd_conv_w, m_ssd_conv_b, m_dt_bias, m_a_log, m_d_skip, m_ssd_norm_w, m_w_s_out, m_w_o, m_norm_ffn_w, m_w_up, m_ffn_conv_w, m_ffn_conv_b, m_w_down, m_final_norm_w, v_norm_mix_w, v_w_in, v_conv_a_w, v_w_a_out, v_ssd_conv_w, v_ssd_conv_b, v_dt_bias, v_a_log, v_d_skip, v_ssd_norm_w, v_w_s_out, v_w_o, v_norm_ffn_w, v_w_up, v_ffn_conv_w, v_ffn_conv_b, v_w_down, v_final_norm_w):
    wts = dict(norm_mix_w=norm_mix_w, w_in=w_in, conv_a_w=conv_a_w, w_a_out=w_a_out, ssd_conv_w=ssd_conv_w,
               ssd_conv_b=ssd_conv_b, dt_bias=dt_bias, a_log=a_log, d_skip=d_skip, ssd_norm_w=ssd_norm_w,
               w_s_out=w_s_out, w_o=w_o, norm_ffn_w=norm_ffn_w, w_up=w_up, ffn_conv_w=ffn_conv_w,
               ffn_conv_b=ffn_conv_b, w_down=w_down, final_norm_w=final_norm_w)
    mom1 = dict(norm_mix_w=m_norm_mix_w, w_in=m_w_in, conv_a_w=m_conv_a_w, w_a_out=m_w_a_out,
                ssd_conv_w=m_ssd_conv_w, ssd_conv_b=m_ssd_conv_b, dt_bias=m_dt_bias, a_log=m_a_log, d_skip=m_d_skip,
                ssd_norm_w=m_ssd_norm_w, w_s_out=m_w_s_out, w_o=m_w_o, norm_ffn_w=m_norm_ffn_w, w_up=m_w_up,
                ffn_conv_w=m_ffn_conv_w, ffn_conv_b=m_ffn_conv_b, w_down=m_w_down, final_norm_w=m_final_norm_w)
    mom2 = dict(norm_mix_w=v_norm_mix_w, w_in=v_w_in, conv_a_w=v_conv_a_w, w_a_out=v_w_a_out,
                ssd_conv_w=v_ssd_conv_w, ssd_conv_b=v_ssd_conv_b, dt_bias=v_dt_bias, a_log=v_a_log, d_skip=v_d_skip,
                ssd_norm_w=v_ssd_norm_w, w_s_out=v_w_s_out, w_o=v_w_o, norm_ffn_w=v_norm_ffn_w, w_up=v_w_up,
                ffn_conv_w=v_ffn_conv_w, ffn_conv_b=v_ffn_conv_b, w_down=v_w_down, final_norm_w=v_final_norm_w)

    t, d = x.shape[1], x.shape[2]
    di = 2 * d
    nh = di // HEAD_DIM
    dxw = di + 2 * N_GROUPS * D_STATE
    f = w_down.shape[1] * N_DEV
    n_in = w_in.shape[2] * N_DEV
    me = 4 * lax.axis_index("x") + 2 * lax.axis_index("y") + lax.axis_index("c")

    rest_local = [_as_rows(k, wts[k]).astype(BF16) for k in REST]
    nrows = [a.shape[0] for a in rest_local]
    n_blk = w_in.shape[2]
    in_local = w_in[0].T.astype(BF16)
    conv_shapes = [wts[k].shape[1:] for k in CONVS]
    conv_local = _pack([wts[k] for k in CONVS], d, 8)
    x2, tgt = x[0], loss_target[0]
    h_in = _xchg_start([in_local, conv_local], [False, False], x2, "gather_in_start", moves=TO_CHIPS)
    u = _rms_fwd(x2, norm_mix_w, h_in["token"], "norm_mix")
    part = _xchg_wait(h_in, u, "gather_in_wait")
    h_fwd = _xchg_start([], [False, False], u, "gather_in_forward_start", moves=TO_SIBLING, lands=part)
    in_all, conv_all = _xchg_wait(h_fwd, u, "gather_in_forward_wait")
    win_t = in_all.reshape(n_in, d)
    h_rest = _xchg_start(rest_local, [False] * len(REST), in_all, "gather_rest_start")
    c_a, c_s, c_f = _unpack(conv_all, conv_shapes, N_DEV)
    caw, scw, fcw = _blocks_to_cols(c_a), _blocks_to_cols(c_s), _blocks_to_cols(c_f)

    o_z, o_x, o_dt = 5 * d, 7 * d, 7 * d + dxw
    seg_bounds = [0, d, 2 * d, 3 * d, 4 * d, o_z, o_x, o_dt]
    w_dt = _pad_rows(win_t[o_dt:], DT_LANES)
    dtb, alog = (_pad_lanes(p[...].reshape(1, nh), DT_LANES) for p in (dt_bias, a_log))
    dskx = jnp.repeat(d_skip.reshape(1, nh), HEAD_DIM, axis=1)

    tok = h_rest["token"]
    gates = _mm([(u, 0, d, win_t, 0)], "nt", BF16, "proj_gates", n=2 * d, after=tok)
    pa = _mm([(u, 0, d, win_t, 2 * d)], "nt", BF16, "proj_a", n=3 * d, after=tok)
    z = _mm([(u, 0, d, win_t, o_z)], "nt", BF16, "proj_z", n=2 * d, after=tok)
    xbc = _mm([(u, 0, d, win_t, o_x)], "nt", BF16, "proj_xbc", n=dxw, after=tok)
    dtr = _mm([(u, w_dt)], "nt", F32, "proj_dt", after=tok)
    ya_in, q_a = _conv_a_fwd(pa, caw, d, "conv_a")
    xc, pre_s = _conv_s_fwd(xbc, scw, ssd_conv_b, "conv_s")
    y, states = _ssd_fwd(xc, dtr, dtb, alog, dskx, di, "ssd")
    yn = _gnorm_fwd(y, z, ssd_norm_w, "gnorm")
    rest_all = _xchg_wait(h_rest, yn, "gather_rest_wait")
    waout, wsout, wo, wup_t, wdown = (a.reshape(N_DEV * n, d) for a, n in zip(rest_all, nrows))
    y_a = _mm([(ya_in, waout)], "nn", BF16, "a_out")
    y_s = _mm([(yn, wsout)], "nn", BF16, "s_out")
    merged = _merge_fwd(gates, y_a, y_s, d, "merge")
    mo = _mm([(merged, wo)], "nn", BF16, "o_proj")
    h1, v = _resnorm_fwd(x2, mo, norm_ffn_w, "norm_ffn")
    hv = _mm([(v, wup_t)], "nt", BF16, "up_proj")
    act, c1 = _ffn_fwd(hv, fcw, ffn_conv_b, f, "ffn_act")
    dd = _mm([(act, wdown)], "nn", BF16, "down_proj")
    loss11, dh2, dh2b, g_fnw = _final(h1, dd, tgt, final_norm_w.reshape(1, d), "final")

    dact = _mm([(dh2b, wdown)], "nt", BF16, "d_act")
    gw_down = _mm_tn(act, dh2b, "gw_down")
    dh1f, dh3, g_ffn = _ffn_bwd(hv, c1, dact, fcw, f, "ffn_act_bwd")
    dv = _mm([(dh1f, 0, f, wup_t, 0), (dh3, 0, f, wup_t, f)], "nn", BF16, "d_v")
    gw_up_t = jnp.concatenate([_mm_tn(dh1f, v, "gw_up1"), _mm_tn(dh3, v, "gw_up3")], axis=0)
    dh1, dh1b, g_nfw = _rms_bwd(h1, dv, norm_ffn_w, dh2, "norm_ffn_bwd")
    dmerged = _mm([(dh1b, wo)], "nt", BF16, "d_merged")
    gw_o = _mm_tn(merged, dh1b, "gw_o")
    dya, dys, dga, dgs = _merge_bwd(dmerged, gates, y_a, y_s, d, "merge_bwd")
    dyain = _mm([(dya, waout)], "nt", BF16, "d_ya_in")
    gw_aout = _mm_tn(ya_in, dya, "gw_a_out")
    db, dc, dvv, g_caw = _conv_a_bwd(pa, q_a, dyain, caw, d, "conv_a_bwd")
    dyn = _mm([(dys, wsout)], "nt", BF16, "d_yn")
    gw_sout = _mm_tn(yn, dys, "gw_s_out")
    grads_rest = dict(w_a_out=gw_aout, w_s_out=gw_sout, w_o=gw_o, w_up=gw_up_t, w_down=gw_down)
    rest_parts = [grads_rest[k].reshape(N_DEV, n, d) for k, n in zip(REST, nrows)]
    h_grest = _xchg_start(rest_parts, [True] * len(REST), rest_parts[0], "scatter_rest_start")
    dy, dz, g_snw = _gnorm_bwd(y, z, dyn, ssd_norm_w, "gnorm_bwd")
    dtb_after = dtb + h_grest["token"][0:1, 0:1]
    dxc, ddtr, g_ssd = _ssd_bwd(xc, dtr, dy, states, dtb_after, alog, dskx, di, "ssd_bwd")
    dxbc, g_scw = _conv_s_bwd(xbc, pre_s, dxc, scw, "conv_s_bwd")
    dsegs = [dga, dgs, db, dc, dvv, dz, dxbc, ddtr.astype(BF16)]
    pairs = [(s, c, d, win_t, a + c * d) for s, a in zip(dsegs[:-1], seg_bounds) for c in range(s.shape[1] // d)]
    pairs.append((dsegs[-1], w_dt))
    gw_in = [_mm_tn(s, u, "gw_in%d" % i) for i, s in enumerate(dsegs)]
    gw_in_t = jnp.concatenate(gw_in[:-1] + [gw_in[-1][:nh]], axis=0)
    in_parts = gw_in_t.reshape(N_DEV, n_blk, d)
    h_gin = _xchg_start([in_parts], [True], in_parts, "scatter_in_start")
    du = _mm(pairs, "nn", BF16, "d_u", tm=512, tn=1024, after=h_gin["token"], resident_b=True)
    dx, _, g_nmw = _rms_bwd(x2, du, norm_mix_w, dh1, "norm_mix_bwd")

    small_grads = dict(norm_mix_w=g_nmw[0], ssd_conv_b=g_scw[4], dt_bias=g_ssd[0, :nh], a_log=g_ssd[1, :nh],
                       d_skip=g_ssd[2, :nh], ssd_norm_w=g_snw[0], norm_ffn_w=g_nfw[0], ffn_conv_b=g_ffn[3],
                       final_norm_w=g_fnw[0], conv_a_w=g_caw[:3], ssd_conv_w=g_scw[:4], ffn_conv_w=g_ffn[:3])
    small_names = REPL + CONVS
    small_parts = _pack([small_grads[k] for k in small_names] + [loss11], d, 8)
    h_small = _xchg_start([small_parts], [False], small_parts, "gather_small_start")
    rest_recv = _xchg_wait(h_grest, dx, "scatter_rest_wait")
    (in_recv,) = _xchg_wait(h_gin, rest_recv[0], "scatter_in_wait")
    (small_all,) = _xchg_wait(h_small, in_recv, "gather_small_wait")
    small_sum = _sum_parts(small_all, "sum_small_grads")
    *small_list, loss = _unpack(small_sum, [small_grads[k].shape for k in small_names] + [()])
    small_g = dict(zip(small_names, small_list))

    res = {}

    def update(k, parts):
        outs = _adamw(parts, *(_as_rows(k, src[k]) for src in (wts, mom1, mom2)), "adamw_" + k)
        for kind, a in zip(("g", "d", "m", "v"), outs):
            res[kind, k] = (a.T if k in TRANSPOSED else a)[None]

    update("w_in", in_recv)
    for k, parts in zip(REST, rest_recv):
        update(k, parts)
    local_g = {}
    for k in REPL:
        local_g[k] = small_g[k].reshape(wts[k].shape)
    for k in CONVS:
        n = wts[k].shape[2]
        local_g[k] = lax.dynamic_slice_in_dim(small_g[k], me * n, n, axis=1)[None]
    for k in small_names:
        as2d = lambda a: a.reshape(-1, a.shape[-1])
        outs = _adamw(as2d(local_g[k])[None], *(as2d(src[k]) for src in (wts, mom1, mom2)), "adamw_" + k)
        for kind, a in zip(("g", "d", "m", "v"), outs):
            res[kind, k] = a.reshape(wts[k].shape)

    return (loss, dx[None], *[res["g", k] for k in ORDER], *[res["d", k] for k in ORDER],
            *[res["m", k] for k in ORDER], *[res["v", k] for k in ORDER])
```

```python
import functools
import math

import jax
import jax.numpy as jnp
from jax import lax
from jax.experimental import pallas as pl
from jax.experimental.pallas import tpu as pltpu

F32 = jnp.float32
BF16 = jnp.bfloat16
EPS = 1e-5
HEAD_DIM = 64
N_GROUPS = 4
D_STATE = 128
CHUNK = 128
DT_LANES = 128
HALO = 16
STRIP = 16
SMALL_PARAM = 16 * 1024
N_DEV = 8
V7X_VMEM_LIMIT = 56 * 1024 * 1024
ADAM_LR, ADAM_B1, ADAM_B2, ADAM_EPS, ADAM_WD, ADAM_STEP = 0.001, 0.9, 0.999, 1e-08, 0.01, 10
HIGHEST = lax.Precision.HIGHEST
MESH = pl.DeviceIdType.MESH


def _pc(body, **kw):
    return pl.pallas_call(body, **kw)


def _params():
    return pltpu.CompilerParams(vmem_limit_bytes=V7X_VMEM_LIMIT)


def _pick(n, cands):
    for c in cands:
        if n % c == 0:
            return c
    return n


def _dot(a, b, ca, cb, prec=None):
    return lax.dot_general(a, b, (((ca,), (cb,)), ((), ())), preferred_element_type=F32, precision=prec)


def _sigmoid(x):
    return 0.5 * jnp.tanh(0.5 * x) + 0.5


def _sds(shape, dtype):
    return jax.ShapeDtypeStruct(shape, dtype)


def _mm(pairs, mode, out_dtype, name, n=None, tm=1024, tn=1024, after=None, resident_b=False):
    pairs = [p if len(p) == 5 else (p[0], 0, p[0].shape[1], p[1], 0) for p in pairs]
    m = pairs[0][0].shape[0]
    if n is None:
        n = pairs[0][3].shape[1] if mode == "nn" else pairs[0][3].shape[0]
    tm = min(tm, m)
    rows_nt = [p[4] for p in pairs] if mode == "nt" else []
    tn = next(c for c in ((n,) if resident_b else ()) + (tn, 1408, 512, 256, 128)
              if n % c == 0 and all(r % c == 0 for r in rows_nt))
    npair = len(pairs)
    cb = 0 if mode == "nn" else 1

    def body(*refs):
        o_ref = refs[-1]
        acc = None
        for p in range(npair):
            part = _dot(refs[2 * p][...], refs[2 * p + 1][...], 1, cb)
            acc = part if acc is None else acc + part
        o_ref[...] = acc.astype(o_ref.dtype)

    in_specs, args = [], []
    for a, a_col, kk, b, b_row in pairs:
        in_specs.append(pl.BlockSpec((tm, kk), lambda i, j, c=a_col: (i, c)))
        if mode == "nn":
            assert b_row % kk == 0 and (not resident_b or n == tn)
            in_specs.append(pl.BlockSpec((kk, tn), lambda i, j, r=b_row // kk: (r, j),
                                         pipeline_mode=pl.Buffered(1) if resident_b else None))
        else:
            in_specs.append(pl.BlockSpec((tn, kk), lambda i, j, r=b_row // tn: (r + j, 0),
                                         pipeline_mode=pl.Buffered(1) if resident_b and tn == n else None))
        args += [a, b]
    if after is not None:
        in_specs.append(pl.BlockSpec(memory_space=pl.ANY))
        args.append(after)
    return _pc(body, name=name, grid=(m // tm, n // tn), in_specs=in_specs,
               out_specs=pl.BlockSpec((tm, tn), lambda i, j: (i, j)),
               out_shape=_sds((m, n), out_dtype), compiler_params=_params())(*args)


def _mm_tn(a, b, name, tm=1024):
    m, ka = a.shape
    nb = b.shape[1]
    tm = min(tm, m)
    nm = m // tm
    tk = _pick(ka, (1024, 1408, 512, 256, 128))
    tn = _pick(nb, (1024, 512, 256, 128))

    def body(a_ref, b_ref, o_ref, acc):
        t = pl.program_id(2)

        @pl.when(t == 0)
        def _():
            acc[...] = jnp.zeros_like(acc)
        acc[...] += _dot(a_ref[...], b_ref[...], 0, 0)

        @pl.when(t == nm - 1)
        def _():
            o_ref[...] = acc[...].astype(o_ref.dtype)

    return _pc(body, name=name, grid=(ka // tk, nb // tn, nm),
               in_specs=[pl.BlockSpec((tm, tk), lambda i, j, t: (t, i)),
                         pl.BlockSpec((tm, tn), lambda i, j, t: (t, j))],
               out_specs=pl.BlockSpec((tk, tn), lambda i, j, t: (i, j)),
               out_shape=_sds((ka, nb), BF16), scratch_shapes=[pltpu.VMEM((tk, tn), F32)],
               compiler_params=_params())(a, b)


def _strips(tm, strip=STRIP):
    return [slice(r * strip, (r + 1) * strip) for r in range(tm // strip)]


def _fold8(a):
    out = a[0:8, :]
    for r in range(8, a.shape[0], 8):
        out = out + a[r:r + 8, :]
    return out


def _colsum(a8):
    return jnp.sum(a8, axis=0, keepdims=True)


def _rms_fwd(x, w, after, name):
    t, d = x.shape
    tm = min(512, t)

    def body(x_ref, w_ref, after_ref, o_ref):
        wv = w_ref[...]
        for rows in _strips(tm):
            xv = x_ref[rows, :]
            r = lax.rsqrt(jnp.mean(xv * xv, axis=-1, keepdims=True) + EPS)
            o_ref[rows, :] = (xv * r * wv).astype(o_ref.dtype)

    return _pc(body, name=name, grid=(t // tm,),
               in_specs=[pl.BlockSpec((tm, d), lambda i: (i, 0)), pl.BlockSpec((1, d), lambda i: (0, 0)),
                         pl.BlockSpec(memory_space=pl.ANY)],
               out_specs=pl.BlockSpec((tm, d), lambda i: (i, 0)),
               out_shape=_sds((t, d), BF16), compiler_params=_params())(x, w, after)


def _resnorm_fwd(x, mo, w, name):
    t, d = x.shape
    tm = min(512, t)

    def body(x_ref, mo_ref, w_ref, h_ref, v_ref):
        wv = w_ref[...]
        for rows in _strips(tm):
            h = x_ref[rows, :] + mo_ref[rows, :].astype(F32)
            r = lax.rsqrt(jnp.mean(h * h, axis=-1, keepdims=True) + EPS)
            h_ref[rows, :] = h
            v_ref[rows, :] = (h * r * wv).astype(v_ref.dtype)

    row = pl.BlockSpec((tm, d), lambda i: (i, 0))
    return _pc(body, name=name, grid=(t // tm,),
               in_specs=[row, row, pl.BlockSpec((1, d), lambda i: (0, 0))],
               out_specs=[row, row], out_shape=[_sds((t, d), F32), _sds((t, d), BF16)],
               compiler_params=_params())(x, mo, w)


def _rms_bwd(h, dy, w, dres, name):
    t, d = h.shape
    tm = min(512, t)

    def body(h_ref, dy_ref, w_ref, dres_ref, dx_ref, dxb_ref, dw_ref):
        @pl.when(pl.program_id(0) == 0)
        def _():
            dw_ref[...] = jnp.zeros_like(dw_ref)
        wv = w_ref[...]
        acc = jnp.zeros((8, d), F32)
        for rows in _strips(tm):
            hv = h_ref[rows, :]
            dyv = dy_ref[rows, :].astype(F32)
            r = lax.rsqrt(jnp.mean(hv * hv, axis=-1, keepdims=True) + EPS)
            n = hv * r
            dn = dyv * wv
            acc = acc + _fold8(dyv * n)
            dx = dres_ref[rows, :] + r * (dn - n * jnp.mean(dn * n, axis=-1, keepdims=True))
            dx_ref[rows, :] = dx
            dxb_ref[rows, :] = dx.astype(BF16)
        dw_ref[0:1, :] += _colsum(acc)

    row = pl.BlockSpec((tm, d), lambda i: (i, 0))
    return _pc(body, name=name, grid=(t // tm,),
               in_specs=[row, row, pl.BlockSpec((1, d), lambda i: (0, 0)), row],
               out_specs=[row, row, pl.BlockSpec((8, d), lambda i: (0, 0))],
               out_shape=[_sds((t, d), F32), _sds((t, d), BF16), _sds((8, d), F32)],
               compiler_params=_params())(h, dy, w, dres)


def _final(h1, dd, tgt, w, name):
    t, d = h1.shape
    tm = min(512, t)
    nt = t // tm

    def body(h1_ref, dd_ref, tgt_ref, w_ref, loss_ref, dh_ref, dhb_ref, dw_ref, acc):
        i = pl.program_id(0)

        @pl.when(i == 0)
        def _():
            dw_ref[...] = jnp.zeros_like(dw_ref)
            acc[...] = jnp.zeros_like(acc)
        wv = w_ref[...]
        sq = jnp.zeros((8, d), F32)
        dw = jnp.zeros((8, d), F32)
        for rows in _strips(tm):
            h = h1_ref[rows, :] + dd_ref[rows, :].astype(F32)
            r = lax.rsqrt(jnp.mean(h * h, axis=-1, keepdims=True) + EPS)
            n = h * r
            e = n * wv - tgt_ref[rows, :]
            sq = sq + _fold8(e * e)
            dout = e * (1.0 / d)
            dn = dout * wv
            dw = dw + _fold8(dout * n)
            dh = r * (dn - n * jnp.mean(dn * n, axis=-1, keepdims=True))
            dh_ref[rows, :] = dh
            dhb_ref[rows, :] = dh.astype(BF16)
        acc[...] += _colsum(sq)
        dw_ref[0:1, :] += _colsum(dw)

        @pl.when(i == nt - 1)
        def _():
            loss_ref[...] = jnp.sum(acc[...], axis=-1, keepdims=True) * (0.5 / d)

    row = pl.BlockSpec((tm, d), lambda i: (i, 0))
    return _pc(body, name=name, grid=(nt,),
               in_specs=[row, row, row, pl.BlockSpec((1, d), lambda i: (0, 0))],
               out_specs=[pl.BlockSpec((1, 1), lambda i: (0, 0)), row, row, pl.BlockSpec((8, d), lambda i: (0, 0))],
               out_shape=[_sds((1, 1), F32), _sds((t, d), F32), _sds((t, d), BF16), _sds((8, d), F32)],
               scratch_shapes=[pltpu.VMEM((1, d), F32)], compiler_params=_params())(h1, dd, tgt, w)


def _tile_specs(t, tm, tc, col0):
    th = tm // HALO
    last = t // HALO - 1
    cur = pl.BlockSpec((tm, tc), lambda j, i: (i, col0 + j))
    prev = pl.BlockSpec((HALO, tc), lambda j, i: (jnp.maximum(i * th - 1, 0), col0 + j))
    nxt = pl.BlockSpec((HALO, tc), lambda j, i: (jnp.minimum((i + 1) * th, last), col0 + j))
    return cur, prev, nxt


def _conv_strip(buf, w, k, rows):
    out = None
    for j in range(k):
        term = w[j:j + 1, :] * buf[pl.ds(HALO - (k - 1) + j + rows.start, STRIP), :]
        out = term if out is None else out + term
    return out


def _conv_backward(dbuf, x_strip, emit, w, acc_ref, k, tm, with_bias):
    tc = dbuf.shape[1]
    accs = [jnp.zeros((8, tc), F32) for _ in range(k + int(with_bias))]
    for rows in _strips(tm):
        xs = x_strip(rows)
        dx = None
        for j in range(k):
            ds = dbuf[pl.ds(rows.start + k - 1 - j, STRIP), :]
            term = w[j:j + 1, :] * ds
            dx = term if dx is None else dx + term
            accs[j] = accs[j] + _fold8(ds * xs)
            if with_bias and j == k - 1:
                accs[k] = accs[k] + _fold8(ds)
        emit(rows, dx)
    for j, a in enumerate(accs):
        acc_ref[j:j + 1, :] += _colsum(a)


def _conv_a_fwd(pa, w, d, name):
    t = pa.shape[0]
    tm, tc = min(1024, t), _pick(d, (512, 256, 128))
    nd = d // tc

    def body(b_ref, c_ref, v_ref, cp_ref, vp_ref, w_ref, o_ref, q_ref, buf):
        keep = (pl.program_id(1) > 0).astype(F32)
        buf[0:HALO, :] = cp_ref[...].astype(F32) * vp_ref[...].astype(F32) * keep
        for rows in _strips(tm):
            buf[HALO + rows.start:HALO + rows.stop, :] = c_ref[rows, :].astype(F32) * v_ref[rows, :].astype(F32)
        wv = w_ref[...]
        for rows in _strips(tm):
            q = _conv_strip(buf, wv, 3, rows)
            q_ref[rows, :] = q.astype(q_ref.dtype)
            o_ref[rows, :] = (b_ref[rows, :].astype(F32) * q).astype(o_ref.dtype)

    b_cur, _, _ = _tile_specs(t, tm, tc, 0)
    c_cur, c_prev, _ = _tile_specs(t, tm, tc, nd)
    v_cur, v_prev, _ = _tile_specs(t, tm, tc, 2 * nd)
    return _pc(body, name=name, grid=(nd, t // tm),
               in_specs=[b_cur, c_cur, v_cur, c_prev, v_prev, pl.BlockSpec((3, tc), lambda j, i: (0, j))],
               out_specs=[pl.BlockSpec((tm, tc), lambda j, i: (i, j))] * 2,
               out_shape=[_sds((t, d), BF16)] * 2,
               scratch_shapes=[pltpu.VMEM((tm + HALO, tc), F32)],
               compiler_params=_params())(pa, pa, pa, pa, pa, w)


def _conv_a_bwd(pa, q, dya, w, d, name):
    t = pa.shape[0]
    tm, tc = min(1024, t), _pick(d, (512, 256, 128))
    nd, nt = d // tc, t // tm

    def body(b_ref, c_ref, v_ref, bn_ref, q_ref, g_ref, gn_ref, w_ref, db_ref, dc_ref, dv_ref, acc_ref, dbuf):
        i = pl.program_id(1)

        @pl.when(i == 0)
        def _():
            acc_ref[...] = jnp.zeros_like(acc_ref)
        for rows in _strips(tm):
            g = g_ref[rows, :].astype(F32)
            dbuf[rows, :] = g * b_ref[rows, :].astype(F32)
            db_ref[rows, :] = (g * q_ref[rows, :].astype(F32)).astype(BF16)
        dbuf[tm:tm + HALO, :] = gn_ref[...].astype(F32) * bn_ref[...].astype(F32) * (i < nt - 1).astype(F32)

        def emit(rows, dp):
            dc_ref[rows, :] = (dp * v_ref[rows, :].astype(F32)).astype(BF16)
            dv_ref[rows, :] = (dp * c_ref[rows, :].astype(F32)).astype(BF16)

        _conv_backward(dbuf, lambda rows: c_ref[rows, :].astype(F32) * v_ref[rows, :].astype(F32), emit,
                       w_ref[...], acc_ref, 3, tm, False)

    b_cur, _, b_next = _tile_specs(t, tm, tc, 0)
    c_cur, _, _ = _tile_specs(t, tm, tc, nd)
    v_cur, _, _ = _tile_specs(t, tm, tc, 2 * nd)
    g_cur, _, g_next = _tile_specs(t, tm, tc, 0)
    out = pl.BlockSpec((tm, tc), lambda j, i: (i, j))
    return _pc(body, name=name, grid=(nd, nt),
               in_specs=[b_cur, c_cur, v_cur, b_next, g_cur, g_cur, g_next,
                         pl.BlockSpec((3, tc), lambda j, i: (0, j))],
               out_specs=[out, out, out, pl.BlockSpec((8, tc), lambda j, i: (0, j))],
               out_shape=[_sds((t, d), BF16)] * 3 + [_sds((8, d), F32)],
               scratch_shapes=[pltpu.VMEM((tm + HALO, tc), F32)],
               compiler_params=_params())(pa, pa, pa, pa, q, dya, dya, w)


def _conv_s_fwd(xbc, w, b, name):
    t, dx = xbc.shape
    tm, tc = min(1024, t), _pick(dx, (512, 256, 128))

    def body(x_ref, xp_ref, w_ref, b_ref, o_ref, pre_ref, buf):
        buf[0:HALO, :] = xp_ref[...].astype(F32) * (pl.program_id(1) > 0).astype(F32)
        for rows in _strips(tm):
            buf[HALO + rows.start:HALO + rows.stop, :] = x_ref[rows, :].astype(F32)
        wv, bv = w_ref[...], b_ref[...]
        for rows in _strips(tm):
            pre = _conv_strip(buf, wv, 4, rows) + bv
            pre_ref[rows, :] = pre.astype(pre_ref.dtype)
            o_ref[rows, :] = (pre * _sigmoid(pre)).astype(o_ref.dtype)

    cur, prev, _ = _tile_specs(t, tm, tc, 0)
    return _pc(body, name=name, grid=(dx // tc, t // tm),
               in_specs=[cur, prev, pl.BlockSpec((4, tc), lambda j, i: (0, j)),
                         pl.BlockSpec((1, tc), lambda j, i: (0, j))],
               out_specs=[pl.BlockSpec((tm, tc), lambda j, i: (i, j))] * 2,
               out_shape=[_sds((t, dx), BF16)] * 2,
               scratch_shapes=[pltpu.VMEM((tm + HALO, tc), F32)],
               compiler_params=_params())(xbc, xbc, w, b)


def _dsilu(pre):
    s = _sigmoid(pre)
    return s * (1.0 + pre * (1.0 - s))


def _conv_s_bwd(xbc, pre, dxc, w, name):
    t, dx = xbc.shape
    tm, tc = min(1024, t), _pick(dx, (512, 256, 128))
    nt = t // tm

    def body(x_ref, p_ref, pn_ref, g_ref, gn_ref, w_ref, dx_ref, acc_ref, dbuf):
        i = pl.program_id(1)

        @pl.when(i == 0)
        def _():
            acc_ref[...] = jnp.zeros_like(acc_ref)
        for rows in _strips(tm):
            dbuf[rows, :] = g_ref[rows, :].astype(F32) * _dsilu(p_ref[rows, :].astype(F32))
        dbuf[tm:tm + HALO, :] = (gn_ref[...].astype(F32) * _dsilu(pn_ref[...].astype(F32))
                                 * (i < nt - 1).astype(F32))

        def emit(rows, d_in):
            dx_ref[rows, :] = d_in.astype(BF16)

        _conv_backward(dbuf, lambda rows: x_ref[rows, :].astype(F32), emit, w_ref[...], acc_ref, 4, tm, True)

    cur, _, nxt = _tile_specs(t, tm, tc, 0)
    return _pc(body, name=name, grid=(dx // tc, nt),
               in_specs=[cur, cur, nxt, cur, nxt, pl.BlockSpec((4, tc), lambda j, i: (0, j))],
               out_specs=[pl.BlockSpec((tm, tc), lambda j, i: (i, j)), pl.BlockSpec((8, tc), lambda j, i: (0, j))],
               out_shape=[_sds((t, dx), BF16), _sds((8, dx), F32)],
               scratch_shapes=[pltpu.VMEM((tm + HALO, tc), F32)],
               compiler_params=_params())(xbc, pre, pre, dxc, dxc, w)


def _ffn_fwd(hv, w, b, f, name):
    t = hv.shape[0]
    tm, tc = min(1024, t), _pick(f, (512, 256, 128))
    nf = f // tc

    def body(h1_ref, h1p_ref, h3_ref, w_ref, b_ref, o_ref, c1_ref, buf):
        buf[0:HALO, :] = h1p_ref[...].astype(F32) * (pl.program_id(1) > 0).astype(F32)
        for rows in _strips(tm):
            buf[HALO + rows.start:HALO + rows.stop, :] = h1_ref[rows, :].astype(F32)
        wv, bv = w_ref[...], b_ref[...]
        for rows in _strips(tm):
            c1 = _conv_strip(buf, wv, 3, rows) + bv
            c1_ref[rows, :] = c1.astype(c1_ref.dtype)
            o_ref[rows, :] = (c1 * _sigmoid(c1) * h3_ref[rows, :].astype(F32)).astype(o_ref.dtype)

    h1_cur, h1_prev, _ = _tile_specs(t, tm, tc, 0)
    h3_cur, _, _ = _tile_specs(t, tm, tc, nf)
    return _pc(body, name=name, grid=(nf, t // tm),
               in_specs=[h1_cur, h1_prev, h3_cur, pl.BlockSpec((3, tc), lambda j, i: (0, j)),
                         pl.BlockSpec((1, tc), lambda j, i: (0, j))],
               out_specs=[pl.BlockSpec((tm, tc), lambda j, i: (i, j))] * 2,
               out_shape=[_sds((t, f), BF16)] * 2,
               scratch_shapes=[pltpu.VMEM((tm + HALO, tc), F32)],
               compiler_params=_params())(hv, hv, hv, w, b)


def _ffn_bwd(hv, c1, dact, w, f, name):
    t = hv.shape[0]
    tm, tc = min(1024, t), _pick(f, (512, 256, 128))
    nf, nt = f // tc, t // tm

    def body(h1_ref, h3_ref, h3n_ref, c_ref, cn_ref, g_ref, gn_ref, w_ref, dh1_ref, dh3_ref, acc_ref, dbuf):
        i = pl.program_id(1)

        @pl.when(i == 0)
        def _():
            acc_ref[...] = jnp.zeros_like(acc_ref)
        for rows in _strips(tm):
            c1v, g = c_ref[rows, :].astype(F32), g_ref[rows, :].astype(F32)
            s1 = _sigmoid(c1v)
            dh3_ref[rows, :] = (g * c1v * s1).astype(BF16)
            dbuf[rows, :] = g * h3_ref[rows, :].astype(F32) * s1 * (1.0 + c1v * (1.0 - s1))
        dbuf[tm:tm + HALO, :] = (gn_ref[...].astype(F32) * h3n_ref[...].astype(F32)
                                 * _dsilu(cn_ref[...].astype(F32)) * (i < nt - 1).astype(F32))

        def emit(rows, d_in):
            dh1_ref[rows, :] = d_in.astype(BF16)

        _conv_backward(dbuf, lambda rows: h1_ref[rows, :].astype(F32), emit, w_ref[...], acc_ref, 3, tm, True)

    h1_cur, _, _ = _tile_specs(t, tm, tc, 0)
    h3_cur, _, h3_next = _tile_specs(t, tm, tc, nf)
    g_cur, _, g_next = _tile_specs(t, tm, tc, 0)
    out = pl.BlockSpec((tm, tc), lambda j, i: (i, j))
    return _pc(body, name=name, grid=(nf, nt),
               in_specs=[h1_cur, h3_cur, h3_next, g_cur, g_next, g_cur, g_next,
                         pl.BlockSpec((3, tc), lambda j, i: (0, j))],
               out_specs=[out, out, pl.BlockSpec((8, tc), lambda j, i: (0, j))],
               out_shape=[_sds((t, f), BF16), _sds((t, f), BF16), _sds((8, f), F32)],
               scratch_shapes=[pltpu.VMEM((tm + HALO, tc), F32)],
               compiler_params=_params())(hv, hv, hv, c1, c1, dact, dact, w)


def _gnorm_fwd(y, z, w, name):
    t, di = y.shape
    gw = di // N_GROUPS
    tm = min(1024, t)

    def body(y_ref, z_ref, w_ref, o_ref):
        wv = w_ref[...]
        for rows in _strips(tm):
            zv = z_ref[rows, :].astype(F32)
            yz = y_ref[rows, :].astype(F32) * zv * _sigmoid(zv)
            r = lax.rsqrt(jnp.mean(yz * yz, axis=-1, keepdims=True) + EPS)
            o_ref[rows, :] = (yz * r * wv).astype(o_ref.dtype)

    blk = pl.BlockSpec((tm, gw), lambda j, i: (i, j))
    return _pc(body, name=name, grid=(N_GROUPS, t // tm),
               in_specs=[blk, blk, pl.BlockSpec((1, gw), lambda j, i: (0, j))],
               out_specs=blk, out_shape=_sds((t, di), BF16), compiler_params=_params())(y, z, w)


def _gnorm_bwd(y, z, dyn, w, name):
    t, di = y.shape
    gw = di // N_GROUPS
    tm = min(1024, t)

    def body(y_ref, z_ref, g_ref, w_ref, dy_ref, dz_ref, dw_ref):
        @pl.when(pl.program_id(1) == 0)
        def _():
            dw_ref[...] = jnp.zeros_like(dw_ref)
        wv = w_ref[...]
        acc = jnp.zeros((8, gw), F32)
        for rows in _strips(tm):
            yv, zv, g = y_ref[rows, :].astype(F32), z_ref[rows, :].astype(F32), g_ref[rows, :].astype(F32)
            s = _sigmoid(zv)
            sz = zv * s
            yz = yv * sz
            r = lax.rsqrt(jnp.mean(yz * yz, axis=-1, keepdims=True) + EPS)
            n = yz * r
            dn = g * wv
            acc = acc + _fold8(g * n)
            dyz = r * (dn - n * jnp.mean(dn * n, axis=-1, keepdims=True))
            dy_ref[rows, :] = (dyz * sz).astype(BF16)
            dz_ref[rows, :] = (dyz * yv * s * (1.0 + zv * (1.0 - s))).astype(BF16)
        dw_ref[0:1, :] += _colsum(acc)

    blk = pl.BlockSpec((tm, gw), lambda j, i: (i, j))
    return _pc(body, name=name, grid=(N_GROUPS, t // tm),
               in_specs=[blk, blk, blk, pl.BlockSpec((1, gw), lambda j, i: (0, j))],
               out_specs=[blk, blk, pl.BlockSpec((8, gw), lambda j, i: (0, j))],
               out_shape=[_sds((t, di), BF16), _sds((t, di), BF16), _sds((8, di), F32)],
               compiler_params=_params())(y, z, dyn, w)


def _merge_fwd(gates, ya, ys, d, name):
    t = ya.shape[0]
    tm, tc = min(1024, t), _pick(d, (512, 256, 128))
    nd = d // tc

    def body(ga_ref, gs_ref, ya_ref, ys_ref, o_ref):
        for rows in _strips(tm):
            o_ref[rows, :] = (_sigmoid(ga_ref[rows, :].astype(F32)) * ya_ref[rows, :].astype(F32)
                              + _sigmoid(gs_ref[rows, :].astype(F32)) * ys_ref[rows, :].astype(F32)
                              ).astype(o_ref.dtype)

    blk = pl.BlockSpec((tm, tc), lambda j, i: (i, j))
    return _pc(body, name=name, grid=(nd, t // tm),
               in_specs=[blk, pl.BlockSpec((tm, tc), lambda j, i: (i, nd + j)), blk, blk],
               out_specs=blk, out_shape=_sds((t, d), BF16), compiler_params=_params())(gates, gates, ya, ys)


def _merge_bwd(dm, gates, ya, ys, d, name):
    t = ya.shape[0]
    tm, tc = min(1024, t), _pick(d, (512, 256, 128))
    nd = d // tc

    def body(dm_ref, ga_ref, gs_ref, ya_ref, ys_ref, dya_ref, dys_ref, dga_ref, dgs_ref):
        for rows in _strips(tm):
            g = dm_ref[rows, :].astype(F32)
            sa, ss = _sigmoid(ga_ref[rows, :].astype(F32)), _sigmoid(gs_ref[rows, :].astype(F32))
            dya_ref[rows, :] = (g * sa).astype(BF16)
            dys_ref[rows, :] = (g * ss).astype(BF16)
            dga_ref[rows, :] = (g * ya_ref[rows, :].astype(F32) * sa * (1.0 - sa)).astype(BF16)
            dgs_ref[rows, :] = (g * ys_ref[rows, :].astype(F32) * ss * (1.0 - ss)).astype(BF16)

    blk = pl.BlockSpec((tm, tc), lambda j, i: (i, j))
    return _pc(body, name=name, grid=(nd, t // tm),
               in_specs=[blk, blk, pl.BlockSpec((tm, tc), lambda j, i: (i, nd + j)), blk, blk],
               out_specs=[blk] * 4, out_shape=[_sds((t, d), BF16)] * 4,
               compiler_params=_params())(dm, gates, gates, ya, ys)


def _ssd_chunk_terms(dtr, dtb, alog):
    xx = dtr + dtb
    dt = jnp.maximum(xx, 0.0) + jnp.log(1.0 + jnp.exp(-jnp.abs(xx)))
    a = -jnp.exp(alog)
    li = lax.broadcasted_iota(jnp.int32, (CHUNK, CHUNK), 0)
    si = lax.broadcasted_iota(jnp.int32, (CHUNK, CHUNK), 1)
    causal = li >= si
    acum = _dot(causal.astype(F32), dt * a, 1, 0, HIGHEST)
    return xx, dt, a, acum, acum.T, causal


def _split2(x):
    hi = x.astype(BF16)
    return hi, (x - hi.astype(F32)).astype(BF16)


def _expand(v, e, exact=True):
    hi, lo = _split2(v)
    out = _dot(hi, e, 1, 0)
    return out + _dot(lo, e, 1, 0) if exact else out


def _segsum(s, e):
    hi, lo = _split2(s)
    return _dot(hi, e, 1, 1) + _dot(lo, e, 1, 1)


def _head_maps(di):
    nh = di // HEAD_DIM
    h = jnp.arange(DT_LANES)[:, None]
    e64 = (jnp.arange(di)[None, :] // HEAD_DIM == h).astype(BF16)
    e128 = (jnp.arange(nh * CHUNK)[None, :] // CHUNK == h).astype(BF16)
    return e64, e128


def _pair_blockdiag(p, left):
    zero = jnp.zeros_like(p)
    return jnp.concatenate([jnp.where(left, p, zero), jnp.where(left, zero, p)], axis=0)


def _ssd_fwd(xc, dtr, dtb, alog, dskx, di, name):
    t = xc.shape[0]
    dx = xc.shape[1]
    nc = t // CHUNK
    nh = di // HEAD_DIM
    hpg = nh // N_GROUPS
    gw = hpg * HEAD_DIM
    boff, coff = di, di + N_GROUPS * D_STATE
    e64, e128 = _head_maps(di)

    def body(xc_ref, dtr_ref, dtb_ref, alog_ref, dsk_ref, e64_ref, e128_ref, y_ref, st_ref, state):
        @pl.when(pl.program_id(0) == 0)
        def _():
            state[...] = jnp.zeros_like(state)
        _, dt, _, acum, acum_t, causal = _ssd_chunk_terms(dtr_ref[...], dtb_ref[...], alog_ref[...])
        last = acum[CHUNK - 1:CHUNK, :]
        e64v = e64_ref[...]
        dtx = _expand(dt, e64v, False)
        eax = _expand(jnp.exp(acum), e64v)
        dex = _expand(dt * jnp.exp(last - acum), e64v, False)
        acx = _expand(acum, e128_ref[...])
        st_ref[0] = state[...]
        left = lax.broadcasted_iota(jnp.int32, (CHUNK, 2 * HEAD_DIM), 1) < HEAD_DIM
        for g in range(N_GROUPS):
            gs = slice(g * gw, (g + 1) * gw)
            bg = xc_ref[:, boff + g * D_STATE:boff + (g + 1) * D_STATE]
            cg = xc_ref[:, coff + g * D_STATE:coff + (g + 1) * D_STATE]
            gm = _dot(cg, bg, 1, 1)
            xg = xc_ref[:, gs].astype(F32)
            xdb = (xg * dtx[:, gs]).astype(BF16)
            sin = state[:, gs]
            yo = _dot(cg, sin.astype(BF16), 1, 0) * eax[:, gs]
            for jp in range(hpg // 2):
                h0 = g * hpg + 2 * jp
                ps = slice(jp * 2 * HEAD_DIM, (jp + 1) * 2 * HEAD_DIM)
                ms = []
                for hh in (h0, h0 + 1):
                    seg = acx[:, hh * CHUNK:(hh + 1) * CHUNK] - acum_t[hh:hh + 1, :]
                    ms.append((gm * jnp.exp(jnp.where(causal, seg, -1e30))).astype(BF16))
                yd = _dot(jnp.concatenate(ms, axis=1), _pair_blockdiag(xdb[:, ps], left), 1, 0)
                col = slice(g * gw + jp * 2 * HEAD_DIM, g * gw + (jp + 1) * 2 * HEAD_DIM)
                y_ref[:, col] = (yd + yo[:, ps] + dsk_ref[:, col] * xg[:, ps]).astype(y_ref.dtype)
            xe = (xg * dex[:, gs]).astype(BF16)
            state[:, gs] = eax[CHUNK - 1:CHUNK, gs] * sin + _dot(bg, xe, 0, 0)

    small = pl.BlockSpec((1, DT_LANES), lambda c: (0, 0))
    whole = lambda a: pl.BlockSpec(a.shape, lambda c: (0, 0))
    return _pc(body, name=name, grid=(nc,),
               in_specs=[pl.BlockSpec((CHUNK, dx), lambda c: (c, 0)),
                         pl.BlockSpec((CHUNK, DT_LANES), lambda c: (c, 0)), small, small,
                         whole(dskx), whole(e64), whole(e128)],
               out_specs=[pl.BlockSpec((CHUNK, di), lambda c: (c, 0)),
                          pl.BlockSpec((1, D_STATE, di), lambda c: (c, 0, 0))],
               out_shape=[_sds((t, di), BF16), _sds((nc, D_STATE, di), F32)],
               scratch_shapes=[pltpu.VMEM((D_STATE, di), F32)],
               compiler_params=_params())(xc, dtr, dtb, alog, dskx, e64, e128)


def _ssd_bwd(xc, dtr, dy, states, dtb, alog, dskx, di, name):
    t = xc.shape[0]
    dx = xc.shape[1]
    nc = t // CHUNK
    nh = di // HEAD_DIM
    hpg = nh // N_GROUPS
    gw = hpg * HEAD_DIM
    boff, coff = di, di + N_GROUPS * D_STATE
    e64, e128 = _head_maps(di)

    def body(xc_ref, dtr_ref, dy_ref, st_ref, dtb_ref, alog_ref, dsk_ref, e64_ref, e128_ref,
             dxc_ref, ddtr_ref, sm_ref, dstate, darow):
        @pl.when(pl.program_id(0) == 0)
        def _():
            dstate[...] = jnp.zeros_like(dstate)
            sm_ref[...] = jnp.zeros_like(sm_ref)
        darow[...] = jnp.zeros_like(darow)
        xx, dt, a, acum, acum_t, causal = _ssd_chunk_terms(dtr_ref[...], dtb_ref[...], alog_ref[...])
        last = acum[CHUNK - 1:CHUNK, :]
        e64v = e64_ref[...]
        dtx = _expand(dt, e64v, False)
        eax = _expand(jnp.exp(acum), e64v)
        eex = _expand(jnp.exp(last - acum), e64v, False)
        acx = _expand(acum, e128_ref[...])
        left = lax.broadcasted_iota(jnp.int32, (CHUNK, 2 * HEAD_DIM), 1) < HEAD_DIM
        lane = lax.broadcasted_iota(jnp.int32, (CHUNK, DT_LANES), 1)
        sub8 = lax.broadcasted_iota(jnp.int32, (8, gw), 0)
        da_col = jnp.zeros((CHUNK, DT_LANES), F32)
        ddt_col = jnp.zeros((CHUNK, DT_LANES), F32)
        rows = jnp.zeros((8, DT_LANES), F32)
        for g in range(N_GROUPS):
            gs = slice(g * gw, (g + 1) * gw)
            bg = xc_ref[:, boff + g * D_STATE:boff + (g + 1) * D_STATE]
            cg = xc_ref[:, coff + g * D_STATE:coff + (g + 1) * D_STATE]
            gm = _dot(cg, bg, 1, 1)
            e64g = e64v[:, gs]
            xg = xc_ref[:, gs].astype(F32)
            dtg, eag, eeg = dtx[:, gs], eax[:, gs], eex[:, gs]
            xd = xg * dtg
            xdb = xd.astype(BF16)
            dyb = dy_ref[:, gs]
            dyf = dyb.astype(F32)
            sin = st_ref[0, :, gs]
            sinb = sin.astype(BF16)
            ds = dstate[:, gs]
            dsb = ds.astype(BF16)
            bds = _dot(bg, dsb, 1, 0)
            dyeb = (dyf * eag).astype(BF16)
            dcg = _dot(dyeb, sinb, 1, 1)
            dstate[:, gs] = eag[CHUNK - 1:CHUNK, :] * ds + _dot(cg, dyeb, 0, 0)
            yo = _dot(cg, sinb, 1, 0) * eag
            xe = xd * eeg
            dbg = _dot(xe.astype(BF16), dsb, 1, 1)
            wterm = bds * xe
            da_col = da_col + _segsum(dyf * yo - wterm, e64g)
            dg = jnp.zeros((CHUNK, CHUNK), F32)
            dxd_parts = []
            for jp in range(hpg // 2):
                h0 = g * hpg + 2 * jp
                ps = slice(jp * 2 * HEAD_DIM, (jp + 1) * 2 * HEAD_DIM)
                lms, mfs = [], []
                for hh in (h0, h0 + 1):
                    seg = acx[:, hh * CHUNK:(hh + 1) * CHUNK] - acum_t[hh:hh + 1, :]
                    lm = jnp.exp(jnp.where(causal, seg, -1e30))
                    lms.append(lm)
                    mfs.append(gm * lm)
                mstack = jnp.concatenate([m.astype(BF16) for m in mfs], axis=0)
                dyp = dyb[:, ps]
                dxd_parts.append(_dot(mstack, _pair_blockdiag(dyp, left), 0, 0))
                dm2 = _dot(dyp, _pair_blockdiag(xdb[:, ps], left), 1, 1)
                for k, hh in enumerate((h0, h0 + 1)):
                    dm = dm2[:, k * CHUNK:(k + 1) * CHUNK]
                    dg = dg + dm * lms[k]
                    q = dm * mfs[k]
                    da_col = da_col + jnp.where(lane == hh, jnp.sum(q, axis=1, keepdims=True), 0.0)
                    darow[hh:hh + 1, :] = -jnp.sum(q, axis=0, keepdims=True)
            dxd = jnp.concatenate(dxd_parts, axis=1) + bds * eeg
            ddt_col = ddt_col + _segsum(dxd * xg, e64g)
            rsum = (jnp.where(sub8 == 0, jnp.sum(wterm, axis=0, keepdims=True), 0.0)
                    + jnp.where(sub8 == 1, jnp.sum(ds * sin, axis=0, keepdims=True), 0.0)
                    + jnp.where(sub8 == 2, jnp.sum(dyf * xg, axis=0, keepdims=True), 0.0))
            rows = rows + _segsum(rsum, e64g)
            dxc_ref[:, gs] = (dxd * dtg + dsk_ref[:, gs] * dyf).astype(dxc_ref.dtype)
            dgb = dg.astype(BF16)
            dxc_ref[:, boff + g * D_STATE:boff + (g + 1) * D_STATE] = (
                dbg + _dot(dgb, cg, 0, 0)).astype(dxc_ref.dtype)
            dxc_ref[:, coff + g * D_STATE:coff + (g + 1) * D_STATE] = (
                dcg + _dot(dgb, bg, 1, 0)).astype(dxc_ref.dtype)
        at_last = rows[0:1, :] + jnp.exp(last) * rows[1:2, :]
        is_last = lax.broadcasted_iota(jnp.int32, (CHUNK, DT_LANES), 0) == CHUNK - 1
        da = da_col + jnp.where(is_last, at_last, 0.0) + darow[...].T
        li = lax.broadcasted_iota(jnp.int32, (CHUNK, CHUNK), 0)
        si = lax.broadcasted_iota(jnp.int32, (CHUNK, CHUNK), 1)
        dla = _dot((si >= li).astype(F32), da, 1, 0, HIGHEST)
        ddtr = (ddt_col + dla * a) * _sigmoid(xx)
        ddtr_ref[...] = ddtr
        sm_ref[0:1, :] += jnp.sum(ddtr, axis=0, keepdims=True)
        sm_ref[1:2, :] += jnp.sum(dla * dt, axis=0, keepdims=True) * a
        sm_ref[2:3, :] += rows[2:3, :]

    small = pl.BlockSpec((1, DT_LANES), lambda c: (0, 0))
    whole = lambda a: pl.BlockSpec(a.shape, lambda c: (0, 0))
    rev = lambda c: (nc - 1 - c, 0)
    return _pc(body, name=name, grid=(nc,),
               in_specs=[pl.BlockSpec((CHUNK, dx), rev), pl.BlockSpec((CHUNK, DT_LANES), rev),
                         pl.BlockSpec((CHUNK, di), rev),
                         pl.BlockSpec((1, D_STATE, di), lambda c: (nc - 1 - c, 0, 0)), small, small,
                         whole(dskx), whole(e64), whole(e128)],
               out_specs=[pl.BlockSpec((CHUNK, dx), rev), pl.BlockSpec((CHUNK, DT_LANES), rev),
                          pl.BlockSpec((8, DT_LANES), lambda c: (0, 0))],
               out_shape=[_sds((t, dx), BF16), _sds((t, DT_LANES), F32), _sds((8, DT_LANES), F32)],
               scratch_shapes=[pltpu.VMEM((D_STATE, di), F32), pltpu.VMEM((DT_LANES, CHUNK), F32)],
               compiler_params=_params())(xc, dtr, dy, states, dtb, alog, dskx, e64, e128)


def _adamw(parts, w, m, v, name):
    npart, rows, width = parts.shape
    if rows * width <= SMALL_PARAM:
        tr, tw = rows, width
    else:
        tr, tw = (_pick(rows, (64, 32, 16, 8)), width) if rows % 8 == 0 else (rows, 128)
    c1 = 1.0 - ADAM_B1 ** ADAM_STEP
    c2 = 1.0 - ADAM_B2 ** ADAM_STEP

    row_strips = _strips(tr) if tr % STRIP == 0 else [slice(0, tr)]
    col_chunks = [slice(c, c + 512) for c in range(0, tw, 512)] if tw % 512 == 0 else [slice(0, tw)]

    def body(p_ref, w_ref, m_ref, v_ref, g_ref, d_ref, nm_ref, nv_ref):
        for rows in row_strips:
            for cols in col_chunks:
                g = p_ref[0, rows, cols].astype(F32)
                for p in range(1, npart):
                    g = g + p_ref[p, rows, cols].astype(F32)
                nm = ADAM_B1 * m_ref[rows, cols] + (1.0 - ADAM_B1) * g
                nv = ADAM_B2 * v_ref[rows, cols] + (1.0 - ADAM_B2) * (g * g)
                g_ref[rows, cols] = g
                nm_ref[rows, cols] = nm
                nv_ref[rows, cols] = nv
                d_ref[rows, cols] = -ADAM_LR * ((nm / c1) / (jnp.sqrt(nv / c2) + ADAM_EPS)
                                                + ADAM_WD * w_ref[rows, cols])

    blk = pl.BlockSpec((tr, tw), lambda i, j: (i, j))
    return _pc(body, name=name, grid=(rows // tr, width // tw),
               in_specs=[pl.BlockSpec((npart, tr, tw), lambda i, j: (0, i, j)), blk, blk, blk],
               out_specs=[blk] * 4, out_shape=[_sds((rows, width), F32)] * 4,
               compiler_params=_params())(parts, w, m, v)


def _sum_parts(parts, name, tile=None):
    npart, rows, width = parts.shape
    tile = rows if tile is None else tile

    def body(p_ref, o_ref):
        for rows_ in _strips(tile, 8 if parts.dtype == F32 else STRIP):
            g = p_ref[0, rows_, :].astype(F32)
            for p in range(1, npart):
                g = g + p_ref[p, rows_, :].astype(F32)
            o_ref[rows_, :] = g

    return _pc(body, name=name, grid=(rows // tile,),
               in_specs=[pl.BlockSpec((npart, tile, width), lambda i: (0, i, 0))],
               out_specs=pl.BlockSpec((tile, width), lambda i: (i, 0)),
               out_shape=_sds((rows, width), F32), compiler_params=_params())(parts)


def _flip(k):
    x, y, c = lax.axis_index("x"), lax.axis_index("y"), lax.axis_index("c")
    px = 1 - x if k & 4 else x
    py = 1 - y if k & 2 else y
    pc = 1 - c if k & 1 else c
    return (px, py, pc), 4 * px + 2 * py + pc


DIRECT = tuple((k, 0) for k in range(1, N_DEV))
TO_CHIPS = ((1, 0), (2, 0), (4, 0), (6, 0))
TO_SIBLING = ((1, 2), (1, 4), (1, 6))


def _copies(arrays, lands, send_sems, recv_sems, scatter, moves):
    _, me = _flip(0)
    outgoing, incoming = [], []
    for i, (kd, kb) in enumerate(moves):
        peer, pidx = _flip(kd)
        _, out_slot = _flip(kb)
        _, in_slot = _flip(kd ^ kb)
        for j, land_ref in enumerate(lands):
            if kb:
                src = land_ref.at[out_slot]
            else:
                src = arrays[j].at[pidx] if scatter[j] else arrays[j]
            sem = len(lands) * i + j
            for dst, bucket in ((land_ref.at[out_slot], outgoing), (land_ref.at[in_slot], incoming)):
                bucket.append(pltpu.make_async_remote_copy(
                    src_ref=src, dst_ref=dst, send_sem=send_sems.at[sem], recv_sem=recv_sems.at[sem],
                    device_id=peer, device_id_type=MESH))
    return outgoing, incoming


HBM_SPEC = pl.BlockSpec(memory_space=pltpu.HBM)
SEM_SPEC = pl.BlockSpec(memory_space=pltpu.SEMAPHORE)
ANY_SPEC = pl.BlockSpec(memory_space=pl.ANY)
EFFECT = pltpu.SideEffectType.DATAFLOW_SIDE_EFFECTING


def _landing_zones(arrays, scatter):
    _, me = _flip(0)
    lands = []
    for a, sc in zip(arrays, scatter):
        own = lax.dynamic_index_in_dim(a, me, 0, keepdims=True) if sc else a[None]
        shape = a.shape if sc else (N_DEV,) + a.shape
        lands.append(lax.dynamic_update_slice(lax.empty(shape, a.dtype), own, (me,) + (0,) * (len(shape) - 1)))
    return lands


def _xchg_start(arrays, scatter, after, name, moves=DIRECT, lands=None):
    if lands is None:
        lands = _landing_zones(arrays, scatter)
    na, nl = len(arrays), len(lands)

    def body(*refs):
        ins, outs = refs[:na + nl], refs[na + nl + 1:]
        outgoing, _ = _copies(ins[:na], ins[na:], outs[0], outs[1], scatter, moves)
        for cp in outgoing:
            cp.start()
        outs[-1][...] = jnp.zeros_like(outs[-1])

    nsem = nl * len(moves)
    operands = [pltpu.with_memory_space_constraint(a, pltpu.HBM) for a in list(arrays) + list(lands)]
    out = _pc(body, name=name,
              out_shape=(pltpu.SemaphoreType.DMA((nsem,)), pltpu.SemaphoreType.DMA((nsem,)),
                         *[pltpu.HBM(a.shape, a.dtype) for a in operands], _sds((8, 128), F32)),
              in_specs=[HBM_SPEC] * (na + nl) + [ANY_SPEC],
              out_specs=(SEM_SPEC, SEM_SPEC, *[HBM_SPEC] * (na + nl), pl.BlockSpec(memory_space=pltpu.VMEM)),
              input_output_aliases={i: 2 + i for i in range(na + nl)},
              compiler_params=pltpu.CompilerParams(has_side_effects=EFFECT))(*operands, after)
    return dict(sems=out[:2], thru=out[2:2 + na + nl], token=out[-1], scatter=scatter, na=na, moves=moves)


def _xchg_wait(handle, after, name):
    na, scatter, moves, thru = handle["na"], handle["scatter"], handle["moves"], handle["thru"]
    n = len(thru)

    def body(*refs):
        ins = refs[:n]
        outgoing, incoming = _copies(ins[:na], ins[na:], refs[n], refs[n + 1], scatter, moves)
        for cp in outgoing:
            cp.wait_send()
        for cp in incoming:
            cp.wait_recv()

    out = _pc(body, name=name, out_shape=tuple(pltpu.HBM(a.shape, a.dtype) for a in thru),
              in_specs=[HBM_SPEC] * n + [SEM_SPEC, SEM_SPEC, ANY_SPEC], out_specs=tuple([HBM_SPEC] * n),
              input_output_aliases={i: i for i in range(n)},
              compiler_params=pltpu.CompilerParams(has_side_effects=EFFECT))(*thru, *handle["sems"], after)
    return out[na:]


def _pack(arrs, width, row_mult):
    flat = jnp.concatenate([a.reshape(-1) for a in arrs])
    n = flat.shape[0]
    rows = -(-n // (width * row_mult)) * row_mult
    return jnp.pad(flat, (0, rows * width - n)).reshape(rows, width)


def _unpack(packed, shapes, lead=None):
    out, off = [], 0
    flat = packed.reshape(-1) if lead is None else packed.reshape(lead, -1)
    for s in shapes:
        n = math.prod(s)
        if lead is None:
            out.append(flat[off:off + n].reshape(s))
        else:
            out.append(flat[:, off:off + n].reshape((lead,) + tuple(s)))
        off += n
    return out


def _blocks_to_cols(blocks):
    nb, rows, n = blocks.shape
    return blocks.transpose(1, 0, 2).reshape(rows, nb * n)


def _pad_rows(a, rows):
    return jnp.pad(a, ((0, rows - a.shape[0]), (0, 0)))


def _pad_lanes(a, lanes):
    return jnp.pad(a, ((0, 0), (0, lanes - a.shape[1])))


REST = ("w_a_out", "w_s_out", "w_o", "w_up", "w_down")
TRANSPOSED = ("w_up", "w_in")
CONVS = ("conv_a_w", "ssd_conv_w", "ffn_conv_w")
REPL = ("norm_mix_w", "ssd_conv_b", "dt_bias", "a_log", "d_skip", "ssd_norm_w", "norm_ffn_w", "ffn_conv_b",
        "final_norm_w")
ORDER = ("norm_mix_w", "w_in", "conv_a_w", "w_a_out", "ssd_conv_w", "ssd_conv_b", "dt_bias", "a_log", "d_skip",
         "ssd_norm_w", "w_s_out", "w_o", "norm_ffn_w", "w_up", "ffn_conv_w", "ffn_conv_b", "w_down", "final_norm_w")


def _as_rows(name, block):
    return block[0].T if name in TRANSPOSED else block[0]


def kernel(x, norm_mix_w, w_in, conv_a_w, w_a_out, ssd_conv_w, ssd_conv_b, dt_bias, a_log, d_skip, ssd_norm_w, w_s_out, w_o, norm_ffn_w, w_up, ffn_conv_w, ffn_conv_b, w_down, final_norm_w, loss_target, m_norm_mix_w, m_w_in, m_conv_a_w, m_w_a_out, m_ssd_conv_w, m_ssd_conv_b, m_dt_bias, m_a_log, m_d_skip, m_ssd_norm_w, m_w_s_out, m_w_o, m_norm_ffn_w, m_w_up, m_ffn_conv_w, m_ffn_conv_b, m_w_down, m_final_norm_w, v_norm_mix_w, v_w_in, v_conv_a_w, v_w_a_out, v_ssd_conv_w, v_ssd_conv_b, v_dt_bias, v_a_log, v_d_skip, v_ssd_norm_w, v_w_s_out, v_w_o, v_norm_ffn_w, v_w_up, v_ffn_conv_w, v_ffn_conv_b, v_w_down, v_final_norm_w):
    wts = dict(norm_mix_w=norm_mix_w, w_in=w_in, conv_a_w=conv_a_w, w_a_out=w_a_out, ssd_conv_w=ssd_conv_w,
               ssd_conv_b=ssd_conv_b, dt_bias=dt_bias, a_log=a_log, d_skip=d_skip, ssd_norm_w=ssd_norm_w,
               w_s_out=w_s_out, w_o=w_o, norm_ffn_w=norm_ffn_w, w_up=w_up, ffn_conv_w=ffn_conv_w,
               ffn_conv_b=ffn_conv_b, w_down=w_down, final_norm_w=final_norm_w)
    mom1 = dict(norm_mix_w=m_norm_mix_w, w_in=m_w_in, conv_a_w=m_conv_a_w, w_a_out=m_w_a_out,
                ssd_conv_w=m_ssd_conv_w, ssd_conv_b=m_ssd_conv_b, dt_bias=m_dt_bias, a_log=m_a_log, d_skip=m_d_skip,
                ssd_norm_w=m_ssd_norm_w, w_s_out=m_w_s_out, w_o=m_w_o, norm_ffn_w=m_norm_ffn_w, w_up=m_w_up,
                ffn_conv_w=m_ffn_conv_w, ffn_conv_b=m_ffn_conv_b, w_down=m_w_down, final_norm_w=m_final_norm_w)
    mom2 = dict(norm_mix_w=v_norm_mix_w, w_in=v_w_in, conv_a_w=v_conv_a_w, w_a_out=v_w_a_out,
                ssd_conv_w=v_ssd_conv_w, ssd_conv_b=v_ssd_conv_b, dt_bias=v_dt_bias, a_log=v_a_log, d_skip=v_d_skip,
                ssd_norm_w=v_ssd_norm_w, w_s_out=v_w_s_out, w_o=v_w_o, norm_ffn_w=v_norm_ffn_w, w_up=v_w_up,
                ffn_conv_w=v_ffn_conv_w, ffn_conv_b=v_ffn_conv_b, w_down=v_w_down, final_norm_w=v_final_norm_w)

    t, d = x.shape[1], x.shape[2]
    di = 2 * d
    nh = di // HEAD_DIM
    dxw = di + 2 * N_GROUPS * D_STATE
    f = w_down.shape[1] * N_DEV
    n_in = w_in.shape[2] * N_DEV
    me = 4 * lax.axis_index("x") + 2 * lax.axis_index("y") + lax.axis_index("c")

    rest_local = [_as_rows(k, wts[k]).astype(BF16) for k in REST]
    nrows = [a.shape[0] for a in rest_local]
    n_blk = w_in.shape[2]
    in_local = w_in[0].T.astype(BF16)
    conv_shapes = [wts[k].shape[1:] for k in CONVS]
    conv_local = _pack([wts[k] for k in CONVS], d, 8)
    x2, tgt = x[0], loss_target[0]
    h_in = _xchg_start([in_local, conv_local], [False, False], x2, "gather_in_start", moves=TO_CHIPS)
    u = _rms_fwd(x2, norm_mix_w, h_in["token"], "norm_mix")
    part = _xchg_wait(h_in, u, "gather_in_wait")
    h_fwd = _xchg_start([], [False, False], u, "gather_in_forward_start", moves=TO_SIBLING, lands=part)
    in_all, conv_all = _xchg_wait(h_fwd, u, "gather_in_forward_wait")
    win_t = in_all.reshape(n_in, d)
    h_rest = _xchg_start(rest_local, [False] * len(REST), in_all, "gather_rest_start")
    c_a, c_s, c_f = _unpack(conv_all, conv_shapes, N_DEV)
    caw, scw, fcw = _blocks_to_cols(c_a), _blocks_to_cols(c_s), _blocks_to_cols(c_f)

    o_z, o_x, o_dt = 5 * d, 7 * d, 7 * d + dxw
    seg_bounds = [0, d, 2 * d, 3 * d, 4 * d, o_z, o_x, o_dt]
    w_dt = _pad_rows(win_t[o_dt:], DT_LANES)
    dtb, alog = (_pad_lanes(p[...].reshape(1, nh), DT_LANES) for p in (dt_bias, a_log))
    dskx = jnp.repeat(d_skip.reshape(1, nh), HEAD_DIM, axis=1)

    tok = h_rest["token"]
    gates = _mm([(u, 0, d, win_t, 0)], "nt", BF16, "proj_gates", n=2 * d, after=tok)
    pa = _mm([(u, 0, d, win_t, 2 * d)], "nt", BF16, "proj_a", n=3 * d, after=tok)
    z = _mm([(u, 0, d, win_t, o_z)], "nt", BF16, "proj_z", n=2 * d, after=tok)
    xbc = _mm([(u, 0, d, win_t, o_x)], "nt", BF16, "proj_xbc", n=dxw, after=tok)
    dtr = _mm([(u, w_dt)], "nt", F32, "proj_dt", after=tok)
    ya_in, q_a = _conv_a_fwd(pa, caw, d, "conv_a")
    xc, pre_s = _conv_s_fwd(xbc, scw, ssd_conv_b, "conv_s")
    y, states = _ssd_fwd(xc, dtr, dtb, alog, dskx, di, "ssd")
    yn = _gnorm_fwd(y, z, ssd_norm_w, "gnorm")
    rest_all = _xchg_wait(h_rest, yn, "gather_rest_wait")
    waout, wsout, wo, wup_t, wdown = (a.reshape(N_DEV * n, d) for a, n in zip(rest_all, nrows))
    y_a = _mm([(ya_in, waout)], "nn", BF16, "a_out")
    y_s = _mm([(yn, wsout)], "nn", BF16, "s_out")
    merged = _merge_fwd(gates, y_a, y_s, d, "merge")
    mo = _mm([(merged, wo)], "nn", BF16, "o_proj")
    h1, v = _resnorm_fwd(x2, mo, norm_ffn_w, "norm_ffn")
    hv = _mm([(v, wup_t)], "nt", BF16, "up_proj", tm=512, resident_b=True)
    act, c1 = _ffn_fwd(hv, fcw, ffn_conv_b, f, "ffn_act")
    dd = _mm([(act, wdown)], "nn", BF16, "down_proj")
    loss11, dh2, dh2b, g_fnw = _final(h1, dd, tgt, final_norm_w.reshape(1, d), "final")

    dact = _mm([(dh2b, wdown)], "nt", BF16, "d_act", resident_b=True)
    gw_down = _mm_tn(act, dh2b, "gw_down")
    dh1f, dh3, g_ffn = _ffn_bwd(hv, c1, dact, fcw, f, "ffn_act_bwd")
    dv = _mm([(dh1f, 0, f, wup_t, 0), (dh3, 0, f, wup_t, f)], "nn", BF16, "d_v")
    gw_up_t = jnp.concatenate([_mm_tn(dh1f, v, "gw_up1"), _mm_tn(dh3, v, "gw_up3")], axis=0)
    dh1, dh1b, g_nfw = _rms_bwd(h1, dv, norm_ffn_w, dh2, "norm_ffn_bwd")
    dmerged = _mm([(dh1b, wo)], "nt", BF16, "d_merged")
    gw_o = _mm_tn(merged, dh1b, "gw_o")
    dya, dys, dga, dgs = _merge_bwd(dmerged, gates, y_a, y_s, d, "merge_bwd")
    dyain = _mm([(dya, waout)], "nt", BF16, "d_ya_in")
    gw_aout = _mm_tn(ya_in, dya, "gw_a_out")
    db, dc, dvv, g_caw = _conv_a_bwd(pa, q_a, dyain, caw, d, "conv_a_bwd")
    dyn = _mm([(dys, wsout)], "nt", BF16, "d_yn", resident_b=True)
    gw_sout = _mm_tn(yn, dys, "gw_s_out")
    grads_rest = dict(w_a_out=gw_aout, w_s_out=gw_sout, w_o=gw_o, w_up=gw_up_t, w_down=gw_down)
    rest_parts = [grads_rest[k].reshape(N_DEV, n, d) for k, n in zip(REST, nrows)]
    h_grest = _xchg_start(rest_parts, [True] * len(REST), rest_parts[0], "scatter_rest_start")
    dy, dz, g_snw = _gnorm_bwd(y, z, dyn, ssd_norm_w, "gnorm_bwd")
    dtb_after = dtb + h_grest["token"][0:1, 0:1]
    dxc, ddtr, g_ssd = _ssd_bwd(xc, dtr, dy, states, dtb_after, alog, dskx, di, "ssd_bwd")
    dxbc, g_scw = _conv_s_bwd(xbc, pre_s, dxc, scw, "conv_s_bwd")
    dsegs = [dga, dgs, db, dc, dvv, dz, dxbc, ddtr.astype(BF16)]
    pairs = [(s, c, d, win_t, a + c * d) for s, a in zip(dsegs[:-1], seg_bounds) for c in range(s.shape[1] // d)]
    pairs.append((dsegs[-1], w_dt))
    gw_in = [_mm_tn(s, u, "gw_in%d" % i) for i, s in enumerate(dsegs)]
    gw_in_t = jnp.concatenate(gw_in[:-1] + [gw_in[-1][:nh]], axis=0)
    in_parts = gw_in_t.reshape(N_DEV, n_blk, d)
    h_gin = _xchg_start([in_parts], [True], in_parts, "scatter_in_start")
    du = _mm(pairs, "nn", BF16, "d_u", tm=512, tn=1024, after=h_gin["token"], resident_b=True)
    dx, _, g_nmw = _rms_bwd(x2, du, norm_mix_w, dh1, "norm_mix_bwd")

    small_grads = dict(norm_mix_w=g_nmw[0], ssd_conv_b=g_scw[4], dt_bias=g_ssd[0, :nh], a_log=g_ssd[1, :nh],
                       d_skip=g_ssd[2, :nh], ssd_norm_w=g_snw[0], norm_ffn_w=g_nfw[0], ffn_conv_b=g_ffn[3],
                       final_norm_w=g_fnw[0], conv_a_w=g_caw[:3], ssd_conv_w=g_scw[:4], ffn_conv_w=g_ffn[:3])
    small_names = REPL + CONVS
    small_parts = _pack([small_grads[k] for k in small_names] + [loss11], d, 8)
    h_small = _xchg_start([small_parts], [False], small_parts, "gather_small_start")
    rest_recv = _xchg_wait(h_grest, dx, "scatter_rest_wait")
    (in_recv,) = _xchg_wait(h_gin, rest_recv[0], "scatter_in_wait")
    (small_all,) = _xchg_wait(h_small, in_recv, "gather_small_wait")
    small_sum = _sum_parts(small_all, "sum_small_grads")
    *small_list, loss = _unpack(small_sum, [small_grads[k].shape for k in small_names] + [()])
    small_g = dict(zip(small_names, small_list))

    res = {}

    def update(k, parts):
        outs = _adamw(parts, *(_as_rows(k, src[k]) for src in (wts, mom1, mom2)), "adamw_" + k)
        for kind, a in zip(("g", "d", "m", "v"), outs):
            res[kind, k] = (a.T if k in TRANSPOSED else a)[None]

    update("w_in", in_recv)
    for k, parts in zip(REST, rest_recv):
        update(k, parts)
    local_g = {}
    for k in REPL:
        local_g[k] = small_g[k].reshape(wts[k].shape)
    for k in CONVS:
        n = wts[k].shape[2]
        local_g[k] = lax.dynamic_slice_in_dim(small_g[k], me * n, n, axis=1)[None]
    for k in small_names:
        as2d = lambda a: a.reshape(-1, a.shape[-1])
        outs = _adamw(as2d(local_g[k])[None], *(as2d(src[k]) for src in (wts, mom1, mom2)), "adamw_" + k)
        for kind, a in zip(("g", "d", "m", "v"), outs):
            res[kind, k] = a.reshape(wts[k].shape)

    return (loss, dx[None], *[res["g", k] for k in ORDER], *[res["d", k] for k in ORDER],
            *[res["m", k] for k in ORDER], *[res["v", k] for k in ORDER])
```

```python
import functools
import math

import jax
import jax.numpy as jnp
from jax import lax
from jax.experimental import pallas as pl
from jax.experimental.pallas import tpu as pltpu

F32 = jnp.float32
BF16 = jnp.bfloat16
EPS = 1e-5
HEAD_DIM = 64
N_GROUPS = 4
D_STATE = 128
CHUNK = 128
DT_LANES = 128
HALO = 16
STRIP = 16
SMALL_PARAM = 16 * 1024
N_DEV = 8
V7X_VMEM_LIMIT = 56 * 1024 * 1024
ADAM_LR, ADAM_B1, ADAM_B2, ADAM_EPS, ADAM_WD, ADAM_STEP = 0.001, 0.9, 0.999, 1e-08, 0.01, 10
HIGHEST = lax.Precision.HIGHEST
MESH = pl.DeviceIdType.MESH


def _pc(body, **kw):
    return pl.pallas_call(body, **kw)


def _params():
    return pltpu.CompilerParams(vmem_limit_bytes=V7X_VMEM_LIMIT)


def _pick(n, cands):
    for c in cands:
        if n % c == 0:
            return c
    return n


def _dot(a, b, ca, cb, prec=None):
    return lax.dot_general(a, b, (((ca,), (cb,)), ((), ())), preferred_element_type=F32, precision=prec)


def _sigmoid(x):
    return 0.5 * jnp.tanh(0.5 * x) + 0.5


def _sds(shape, dtype):
    return jax.ShapeDtypeStruct(shape, dtype)


def _mm(pairs, mode, out_dtype, name, n=None, tm=1024, tn=1024, after=None, resident_b=False):
    pairs = [p if len(p) == 5 else (p[0], 0, p[0].shape[1], p[1], 0) for p in pairs]
    m = pairs[0][0].shape[0]
    if n is None:
        n = pairs[0][3].shape[1] if mode == "nn" else pairs[0][3].shape[0]
    tm = min(tm, m)
    rows_nt = [p[4] for p in pairs] if mode == "nt" else []
    tn = next(c for c in ((n,) if resident_b else ()) + (tn, 1408, 512, 256, 128)
              if n % c == 0 and all(r % c == 0 for r in rows_nt))
    npair = len(pairs)
    cb = 0 if mode == "nn" else 1

    def body(*refs):
        o_ref = refs[-1]
        acc = None
        for p in range(npair):
            part = _dot(refs[2 * p][...], refs[2 * p + 1][...], 1, cb)
            acc = part if acc is None else acc + part
        o_ref[...] = acc.astype(o_ref.dtype)

    in_specs, args = [], []
    for a, a_col, kk, b, b_row in pairs:
        in_specs.append(pl.BlockSpec((tm, kk), lambda i, j, c=a_col: (i, c)))
        if mode == "nn":
            assert b_row % kk == 0 and (not resident_b or n == tn)
            in_specs.append(pl.BlockSpec((kk, tn), lambda i, j, r=b_row // kk: (r, j),
                                         pipeline_mode=pl.Buffered(1) if resident_b else None))
        else:
            in_specs.append(pl.BlockSpec((tn, kk), lambda i, j, r=b_row // tn: (r + j, 0),
                                         pipeline_mode=pl.Buffered(1) if resident_b and tn == n else None))
        args += [a, b]
    if after is not None:
        in_specs.append(pl.BlockSpec(memory_space=pl.ANY))
        args.append(after)
    return _pc(body, name=name, grid=(m // tm, n // tn), in_specs=in_specs,
               out_specs=pl.BlockSpec((tm, tn), lambda i, j: (i, j)),
               out_shape=_sds((m, n), out_dtype), compiler_params=_params())(*args)


def _mm_tn(a, b, name, tm=2048):
    m, ka = a.shape
    nb = b.shape[1]
    tm = min(tm, m)
    nm = m // tm
    tk = _pick(ka, (1024, 1408, 512, 256, 128))
    tn = _pick(nb, (1024, 512, 256, 128))

    def body(a_ref, b_ref, o_ref, acc):
        t = pl.program_id(2)

        @pl.when(t == 0)
        def _():
            acc[...] = jnp.zeros_like(acc)
        acc[...] += _dot(a_ref[...], b_ref[...], 0, 0)

        @pl.when(t == nm - 1)
        def _():
            o_ref[...] = acc[...].astype(o_ref.dtype)

    return _pc(body, name=name, grid=(ka // tk, nb // tn, nm),
               in_specs=[pl.BlockSpec((tm, tk), lambda i, j, t: (t, i)),
                         pl.BlockSpec((tm, tn), lambda i, j, t: (t, j))],
               out_specs=pl.BlockSpec((tk, tn), lambda i, j, t: (i, j)),
               out_shape=_sds((ka, nb), BF16), scratch_shapes=[pltpu.VMEM((tk, tn), F32)],
               compiler_params=_params())(a, b)


def _strips(tm, strip=STRIP):
    return [slice(r * strip, (r + 1) * strip) for r in range(tm // strip)]


def _fold8(a):
    out = a[0:8, :]
    for r in range(8, a.shape[0], 8):
        out = out + a[r:r + 8, :]
    return out


def _colsum(a8):
    return jnp.sum(a8, axis=0, keepdims=True)


def _rms_fwd(x, w, after, name):
    t, d = x.shape
    tm = min(1024, t)

    def body(x_ref, w_ref, after_ref, o_ref):
        wv = w_ref[...]
        for rows in _strips(tm):
            xv = x_ref[rows, :]
            r = lax.rsqrt(jnp.mean(xv * xv, axis=-1, keepdims=True) + EPS)
            o_ref[rows, :] = (xv * r * wv).astype(o_ref.dtype)

    return _pc(body, name=name, grid=(t // tm,),
               in_specs=[pl.BlockSpec((tm, d), lambda i: (i, 0)), pl.BlockSpec((1, d), lambda i: (0, 0)),
                         pl.BlockSpec(memory_space=pl.ANY)],
               out_specs=pl.BlockSpec((tm, d), lambda i: (i, 0)),
               out_shape=_sds((t, d), BF16), compiler_params=_params())(x, w, after)


def _resnorm_fwd(x, mo, w, name):
    t, d = x.shape
    tm = min(1024, t)

    def body(x_ref, mo_ref, w_ref, h_ref, v_ref):
        wv = w_ref[...]
        for rows in _strips(tm):
            h = x_ref[rows, :] + mo_ref[rows, :].astype(F32)
            r = lax.rsqrt(jnp.mean(h * h, axis=-1, keepdims=True) + EPS)
            h_ref[rows, :] = h
            v_ref[rows, :] = (h * r * wv).astype(v_ref.dtype)

    row = pl.BlockSpec((tm, d), lambda i: (i, 0))
    return _pc(body, name=name, grid=(t // tm,),
               in_specs=[row, row, pl.BlockSpec((1, d), lambda i: (0, 0))],
               out_specs=[row, row], out_shape=[_sds((t, d), F32), _sds((t, d), BF16)],
               compiler_params=_params())(x, mo, w)


def _rms_bwd(h, dy, w, dres, name):
    t, d = h.shape
    tm = min(1024, t)

    def body(h_ref, dy_ref, w_ref, dres_ref, dx_ref, dxb_ref, dw_ref):
        @pl.when(pl.program_id(0) == 0)
        def _():
            dw_ref[...] = jnp.zeros_like(dw_ref)
        wv = w_ref[...]
        acc = jnp.zeros((8, d), F32)
        for rows in _strips(tm):
            hv = h_ref[rows, :]
            dyv = dy_ref[rows, :].astype(F32)
            r = lax.rsqrt(jnp.mean(hv * hv, axis=-1, keepdims=True) + EPS)
            n = hv * r
            dn = dyv * wv
            acc = acc + _fold8(dyv * n)
            dx = dres_ref[rows, :] + r * (dn - n * jnp.mean(dn * n, axis=-1, keepdims=True))
            dx_ref[rows, :] = dx
            dxb_ref[rows, :] = dx.astype(BF16)
        dw_ref[0:1, :] += _colsum(acc)

    row = pl.BlockSpec((tm, d), lambda i: (i, 0))
    return _pc(body, name=name, grid=(t // tm,),
               in_specs=[row, row, pl.BlockSpec((1, d), lambda i: (0, 0)), row],
               out_specs=[row, row, pl.BlockSpec((8, d), lambda i: (0, 0))],
               out_shape=[_sds((t, d), F32), _sds((t, d), BF16), _sds((8, d), F32)],
               compiler_params=_params())(h, dy, w, dres)


def _final(h1, dd, tgt, w, name):
    t, d = h1.shape
    tm = min(1024, t)
    nt = t // tm

    def body(h1_ref, dd_ref, tgt_ref, w_ref, loss_ref, dh_ref, dhb_ref, dw_ref, acc):
        i = pl.program_id(0)

        @pl.when(i == 0)
        def _():
            dw_ref[...] = jnp.zeros_like(dw_ref)
            acc[...] = jnp.zeros_like(acc)
        wv = w_ref[...]
        sq = jnp.zeros((8, d), F32)
        dw = jnp.zeros((8, d), F32)
        for rows in _strips(tm):
            h = h1_ref[rows, :] + dd_ref[rows, :].astype(F32)
            r = lax.rsqrt(jnp.mean(h * h, axis=-1, keepdims=True) + EPS)
            n = h * r
            e = n * wv - tgt_ref[rows, :]
            sq = sq + _fold8(e * e)
            dout = e * (1.0 / d)
            dn = dout * wv
            dw = dw + _fold8(dout * n)
            dh = r * (dn - n * jnp.mean(dn * n, axis=-1, keepdims=True))
            dh_ref[rows, :] = dh
            dhb_ref[rows, :] = dh.astype(BF16)
        acc[...] += _colsum(sq)
        dw_ref[0:1, :] += _colsum(dw)

        @pl.when(i == nt - 1)
        def _():
            loss_ref[...] = jnp.sum(acc[...], axis=-1, keepdims=True) * (0.5 / d)

    row = pl.BlockSpec((tm, d), lambda i: (i, 0))
    return _pc(body, name=name, grid=(nt,),
               in_specs=[row, row, row, pl.BlockSpec((1, d), lambda i: (0, 0))],
               out_specs=[pl.BlockSpec((1, 1), lambda i: (0, 0)), row, row, pl.BlockSpec((8, d), lambda i: (0, 0))],
               out_shape=[_sds((1, 1), F32), _sds((t, d), F32), _sds((t, d), BF16), _sds((8, d), F32)],
               scratch_shapes=[pltpu.VMEM((1, d), F32)], compiler_params=_params())(h1, dd, tgt, w)


def _tile_specs(t, tm, tc, col0):
    th = tm // HALO
    last = t // HALO - 1
    cur = pl.BlockSpec((tm, tc), lambda j, i: (i, col0 + j))
    prev = pl.BlockSpec((HALO, tc), lambda j, i: (jnp.maximum(i * th - 1, 0), col0 + j))
    nxt = pl.BlockSpec((HALO, tc), lambda j, i: (jnp.minimum((i + 1) * th, last), col0 + j))
    return cur, prev, nxt


def _conv_strip(buf, w, k, rows):
    out = None
    for j in range(k):
        term = w[j:j + 1, :] * buf[pl.ds(HALO - (k - 1) + j + rows.start, STRIP), :]
        out = term if out is None else out + term
    return out


def _conv_backward(dbuf, x_strip, emit, w, acc_ref, k, tm, with_bias):
    tc = dbuf.shape[1]
    accs = [jnp.zeros((8, tc), F32) for _ in range(k + int(with_bias))]
    for rows in _strips(tm):
        xs = x_strip(rows)
        dx = None
        for j in range(k):
            ds = dbuf[pl.ds(rows.start + k - 1 - j, STRIP), :]
            term = w[j:j + 1, :] * ds
            dx = term if dx is None else dx + term
            accs[j] = accs[j] + _fold8(ds * xs)
            if with_bias and j == k - 1:
                accs[k] = accs[k] + _fold8(ds)
        emit(rows, dx)
    for j, a in enumerate(accs):
        acc_ref[j:j + 1, :] += _colsum(a)


def _conv_a_fwd(pa, w, d, name):
    t = pa.shape[0]
    tm, tc = min(1024, t), _pick(d, (512, 256, 128))
    nd = d // tc

    def body(b_ref, c_ref, v_ref, cp_ref, vp_ref, w_ref, o_ref, q_ref, buf):
        keep = (pl.program_id(1) > 0).astype(F32)
        buf[0:HALO, :] = cp_ref[...].astype(F32) * vp_ref[...].astype(F32) * keep
        for rows in _strips(tm):
            buf[HALO + rows.start:HALO + rows.stop, :] = c_ref[rows, :].astype(F32) * v_ref[rows, :].astype(F32)
        wv = w_ref[...]
        for rows in _strips(tm):
            q = _conv_strip(buf, wv, 3, rows)
            q_ref[rows, :] = q.astype(q_ref.dtype)
            o_ref[rows, :] = (b_ref[rows, :].astype(F32) * q).astype(o_ref.dtype)

    b_cur, _, _ = _tile_specs(t, tm, tc, 0)
    c_cur, c_prev, _ = _tile_specs(t, tm, tc, nd)
    v_cur, v_prev, _ = _tile_specs(t, tm, tc, 2 * nd)
    return _pc(body, name=name, grid=(nd, t // tm),
               in_specs=[b_cur, c_cur, v_cur, c_prev, v_prev, pl.BlockSpec((3, tc), lambda j, i: (0, j))],
               out_specs=[pl.BlockSpec((tm, tc), lambda j, i: (i, j))] * 2,
               out_shape=[_sds((t, d), BF16)] * 2,
               scratch_shapes=[pltpu.VMEM((tm + HALO, tc), F32)],
               compiler_params=_params())(pa, pa, pa, pa, pa, w)


def _conv_a_bwd(pa, q, dya, w, d, name):
    t = pa.shape[0]
    tm, tc = min(1024, t), _pick(d, (512, 256, 128))
    nd, nt = d // tc, t // tm

    def body(b_ref, c_ref, v_ref, bn_ref, q_ref, g_ref, gn_ref, w_ref, db_ref, dc_ref, dv_ref, acc_ref, dbuf):
        i = pl.program_id(1)

        @pl.when(i == 0)
        def _():
            acc_ref[...] = jnp.zeros_like(acc_ref)
        for rows in _strips(tm):
            g = g_ref[rows, :].astype(F32)
            dbuf[rows, :] = g * b_ref[rows, :].astype(F32)
            db_ref[rows, :] = (g * q_ref[rows, :].astype(F32)).astype(BF16)
        dbuf[tm:tm + HALO, :] = gn_ref[...].astype(F32) * bn_ref[...].astype(F32) * (i < nt - 1).astype(F32)

        def emit(rows, dp):
            dc_ref[rows, :] = (dp * v_ref[rows, :].astype(F32)).astype(BF16)
            dv_ref[rows, :] = (dp * c_ref[rows, :].astype(F32)).astype(BF16)

        _conv_backward(dbuf, lambda rows: c_ref[rows, :].astype(F32) * v_ref[rows, :].astype(F32), emit,
                       w_ref[...], acc_ref, 3, tm, False)

    b_cur, _, b_next = _tile_specs(t, tm, tc, 0)
    c_cur, _, _ = _tile_specs(t, tm, tc, nd)
    v_cur, _, _ = _tile_specs(t, tm, tc, 2 * nd)
    g_cur, _, g_next = _tile_specs(t, tm, tc, 0)
    out = pl.BlockSpec((tm, tc), lambda j, i: (i, j))
    return _pc(body, name=name, grid=(nd, nt),
               in_specs=[b_cur, c_cur, v_cur, b_next, g_cur, g_cur, g_next,
                         pl.BlockSpec((3, tc), lambda j, i: (0, j))],
               out_specs=[out, out, out, pl.BlockSpec((8, tc), lambda j, i: (0, j))],
               out_shape=[_sds((t, d), BF16)] * 3 + [_sds((8, d), F32)],
               scratch_shapes=[pltpu.VMEM((tm + HALO, tc), F32)],
               compiler_params=_params())(pa, pa, pa, pa, q, dya, dya, w)


def _conv_s_fwd(xbc, w, b, name):
    t, dx = xbc.shape
    tm, tc = min(1024, t), _pick(dx, (512, 256, 128))

    def body(x_ref, xp_ref, w_ref, b_ref, o_ref, pre_ref, buf):
        buf[0:HALO, :] = xp_ref[...].astype(F32) * (pl.program_id(1) > 0).astype(F32)
        for rows in _strips(tm):
            buf[HALO + rows.start:HALO + rows.stop, :] = x_ref[rows, :].astype(F32)
        wv, bv = w_ref[...], b_ref[...]
        for rows in _strips(tm):
            pre = _conv_strip(buf, wv, 4, rows) + bv
            pre_ref[rows, :] = pre.astype(pre_ref.dtype)
            o_ref[rows, :] = (pre * _sigmoid(pre)).astype(o_ref.dtype)

    cur, prev, _ = _tile_specs(t, tm, tc, 0)
    return _pc(body, name=name, grid=(dx // tc, t // tm),
               in_specs=[cur, prev, pl.BlockSpec((4, tc), lambda j, i: (0, j)),
                         pl.BlockSpec((1, tc), lambda j, i: (0, j))],
               out_specs=[pl.BlockSpec((tm, tc), lambda j, i: (i, j))] * 2,
               out_shape=[_sds((t, dx), BF16)] * 2,
               scratch_shapes=[pltpu.VMEM((tm + HALO, tc), F32)],
               compiler_params=_params())(xbc, xbc, w, b)


def _dsilu(pre):
    s = _sigmoid(pre)
    return s * (1.0 + pre * (1.0 - s))


def _conv_s_bwd(xbc, pre, dxc, w, name):
    t, dx = xbc.shape
    tm, tc = min(1024, t), _pick(dx, (512, 256, 128))
    nt = t // tm

    def body(x_ref, p_ref, pn_ref, g_ref, gn_ref, w_ref, dx_ref, acc_ref, dbuf):
        i = pl.program_id(1)

        @pl.when(i == 0)
        def _():
            acc_ref[...] = jnp.zeros_like(acc_ref)
        for rows in _strips(tm):
            dbuf[rows, :] = g_ref[rows, :].astype(F32) * _dsilu(p_ref[rows, :].astype(F32))
        dbuf[tm:tm + HALO, :] = (gn_ref[...].astype(F32) * _dsilu(pn_ref[...].astype(F32))
                                 * (i < nt - 1).astype(F32))

        def emit(rows, d_in):
            dx_ref[rows, :] = d_in.astype(BF16)

        _conv_backward(dbuf, lambda rows: x_ref[rows, :].astype(F32), emit, w_ref[...], acc_ref, 4, tm, True)

    cur, _, nxt = _tile_specs(t, tm, tc, 0)
    return _pc(body, name=name, grid=(dx // tc, nt),
               in_specs=[cur, cur, nxt, cur, nxt, pl.BlockSpec((4, tc), lambda j, i: (0, j))],
               out_specs=[pl.BlockSpec((tm, tc), lambda j, i: (i, j)), pl.BlockSpec((8, tc), lambda j, i: (0, j))],
               out_shape=[_sds((t, dx), BF16), _sds((8, dx), F32)],
               scratch_shapes=[pltpu.VMEM((tm + HALO, tc), F32)],
               compiler_params=_params())(xbc, pre, pre, dxc, dxc, w)


def _ffn_fwd(hv, w, b, f, name):
    t = hv.shape[0]
    tm, tc = min(1024, t), _pick(f, (512, 256, 128))
    nf = f // tc

    def body(h1_ref, h1p_ref, h3_ref, w_ref, b_ref, o_ref, c1_ref, buf):
        buf[0:HALO, :] = h1p_ref[...].astype(F32) * (pl.program_id(1) > 0).astype(F32)
        for rows in _strips(tm):
            buf[HALO + rows.start:HALO + rows.stop, :] = h1_ref[rows, :].astype(F32)
        wv, bv = w_ref[...], b_ref[...]
        for rows in _strips(tm):
            c1 = _conv_strip(buf, wv, 3, rows) + bv
            c1_ref[rows, :] = c1.astype(c1_ref.dtype)
            o_ref[rows, :] = (c1 * _sigmoid(c1) * h3_ref[rows, :].astype(F32)).astype(o_ref.dtype)

    h1_cur, h1_prev, _ = _tile_specs(t, tm, tc, 0)
    h3_cur, _, _ = _tile_specs(t, tm, tc, nf)
    return _pc(body, name=name, grid=(nf, t // tm),
               in_specs=[h1_cur, h1_prev, h3_cur, pl.BlockSpec((3, tc), lambda j, i: (0, j)),
                         pl.BlockSpec((1, tc), lambda j, i: (0, j))],
               out_specs=[pl.BlockSpec((tm, tc), lambda j, i: (i, j))] * 2,
               out_shape=[_sds((t, f), BF16)] * 2,
               scratch_shapes=[pltpu.VMEM((tm + HALO, tc), F32)],
               compiler_params=_params())(hv, hv, hv, w, b)


def _ffn_bwd(hv, c1, dact, w, f, name):
    t = hv.shape[0]
    tm, tc = min(1024, t), _pick(f, (512, 256, 128))
    nf, nt = f // tc, t // tm

    def body(h1_ref, h3_ref, h3n_ref, c_ref, cn_ref, g_ref, gn_ref, w_ref, dh1_ref, dh3_ref, acc_ref, dbuf):
        i = pl.program_id(1)

        @pl.when(i == 0)
        def _():
            acc_ref[...] = jnp.zeros_like(acc_ref)
        for rows in _strips(tm):
            c1v, g = c_ref[rows, :].astype(F32), g_ref[rows, :].astype(F32)
            s1 = _sigmoid(c1v)
            dh3_ref[rows, :] = (g * c1v * s1).astype(BF16)
            dbuf[rows, :] = g * h3_ref[rows, :].astype(F32) * s1 * (1.0 + c1v * (1.0 - s1))
        dbuf[tm:tm + HALO, :] = (gn_ref[...].astype(F32) * h3n_ref[...].astype(F32)
                                 * _dsilu(cn_ref[...].astype(F32)) * (i < nt - 1).astype(F32))

        def emit(rows, d_in):
            dh1_ref[rows, :] = d_in.astype(BF16)

        _conv_backward(dbuf, lambda rows: h1_ref[rows, :].astype(F32), emit, w_ref[...], acc_ref, 3, tm, True)

    h1_cur, _, _ = _tile_specs(t, tm, tc, 0)
    h3_cur, _, h3_next = _tile_specs(t, tm, tc, nf)
    g_cur, _, g_next = _tile_specs(t, tm, tc, 0)
    out = pl.BlockSpec((tm, tc), lambda j, i: (i, j))
    return _pc(body, name=name, grid=(nf, nt),
               in_specs=[h1_cur, h3_cur, h3_next, g_cur, g_next, g_cur, g_next,
                         pl.BlockSpec((3, tc), lambda j, i: (0, j))],
               out_specs=[out, out, pl.BlockSpec((8, tc), lambda j, i: (0, j))],
               out_shape=[_sds((t, f), BF16), _sds((t, f), BF16), _sds((8, f), F32)],
               scratch_shapes=[pltpu.VMEM((tm + HALO, tc), F32)],
               compiler_params=_params())(hv, hv, hv, c1, c1, dact, dact, w)


def _gnorm_fwd(y, z, w, name):
    t, di = y.shape
    gw = di // N_GROUPS
    tm = min(1024, t)

    def body(y_ref, z_ref, w_ref, o_ref):
        wv = w_ref[...]
        for rows in _strips(tm):
            zv = z_ref[rows, :].astype(F32)
            yz = y_ref[rows, :].astype(F32) * zv * _sigmoid(zv)
            r = lax.rsqrt(jnp.mean(yz * yz, axis=-1, keepdims=True) + EPS)
            o_ref[rows, :] = (yz * r * wv).astype(o_ref.dtype)

    blk = pl.BlockSpec((tm, gw), lambda j, i: (i, j))
    return _pc(body, name=name, grid=(N_GROUPS, t // tm),
               in_specs=[blk, blk, pl.BlockSpec((1, gw), lambda j, i: (0, j))],
               out_specs=blk, out_shape=_sds((t, di), BF16), compiler_params=_params())(y, z, w)


def _gnorm_bwd(y, z, dyn, w, name):
    t, di = y.shape
    gw = di // N_GROUPS
    tm = min(1024, t)

    def body(y_ref, z_ref, g_ref, w_ref, dy_ref, dz_ref, dw_ref):
        @pl.when(pl.program_id(1) == 0)
        def _():
            dw_ref[...] = jnp.zeros_like(dw_ref)
        wv = w_ref[...]
        acc = jnp.zeros((8, gw), F32)
        for rows in _strips(tm):
            yv, zv, g = y_ref[rows, :].astype(F32), z_ref[rows, :].astype(F32), g_ref[rows, :].astype(F32)
            s = _sigmoid(zv)
            sz = zv * s
            yz = yv * sz
            r = lax.rsqrt(jnp.mean(yz * yz, axis=-1, keepdims=True) + EPS)
            n = yz * r
            dn = g * wv
            acc = acc + _fold8(g * n)
            dyz = r * (dn - n * jnp.mean(dn * n, axis=-1, keepdims=True))
            dy_ref[rows, :] = (dyz * sz).astype(BF16)
            dz_ref[rows, :] = (dyz * yv * s * (1.0 + zv * (1.0 - s))).astype(BF16)
        dw_ref[0:1, :] += _colsum(acc)

    blk = pl.BlockSpec((tm, gw), lambda j, i: (i, j))
    return _pc(body, name=name, grid=(N_GROUPS, t // tm),
               in_specs=[blk, blk, blk, pl.BlockSpec((1, gw), lambda j, i: (0, j))],
               out_specs=[blk, blk, pl.BlockSpec((8, gw), lambda j, i: (0, j))],
               out_shape=[_sds((t, di), BF16), _sds((t, di), BF16), _sds((8, di), F32)],
               compiler_params=_params())(y, z, dyn, w)


def _merge_fwd(gates, ya, ys, d, name):
    t = ya.shape[0]
    tm, tc = min(1024, t), _pick(d, (512, 256, 128))
    nd = d // tc

    def body(ga_ref, gs_ref, ya_ref, ys_ref, o_ref):
        for rows in _strips(tm):
            o_ref[rows, :] = (_sigmoid(ga_ref[rows, :].astype(F32)) * ya_ref[rows, :].astype(F32)
                              + _sigmoid(gs_ref[rows, :].astype(F32)) * ys_ref[rows, :].astype(F32)
                              ).astype(o_ref.dtype)

    blk = pl.BlockSpec((tm, tc), lambda j, i: (i, j))
    return _pc(body, name=name, grid=(nd, t // tm),
               in_specs=[blk, pl.BlockSpec((tm, tc), lambda j, i: (i, nd + j)), blk, blk],
               out_specs=blk, out_shape=_sds((t, d), BF16), compiler_params=_params())(gates, gates, ya, ys)


def _merge_bwd(dm, gates, ya, ys, d, name):
    t = ya.shape[0]
    tm, tc = min(1024, t), _pick(d, (512, 256, 128))
    nd = d // tc

    def body(dm_ref, ga_ref, gs_ref, ya_ref, ys_ref, dya_ref, dys_ref, dga_ref, dgs_ref):
        for rows in _strips(tm):
            g = dm_ref[rows, :].astype(F32)
            sa, ss = _sigmoid(ga_ref[rows, :].astype(F32)), _sigmoid(gs_ref[rows, :].astype(F32))
            dya_ref[rows, :] = (g * sa).astype(BF16)
            dys_ref[rows, :] = (g * ss).astype(BF16)
            dga_ref[rows, :] = (g * ya_ref[rows, :].astype(F32) * sa * (1.0 - sa)).astype(BF16)
            dgs_ref[rows, :] = (g * ys_ref[rows, :].astype(F32) * ss * (1.0 - ss)).astype(BF16)

    blk = pl.BlockSpec((tm, tc), lambda j, i: (i, j))
    return _pc(body, name=name, grid=(nd, t // tm),
               in_specs=[blk, blk, pl.BlockSpec((tm, tc), lambda j, i: (i, nd + j)), blk, blk],
               out_specs=[blk] * 4, out_shape=[_sds((t, d), BF16)] * 4,
               compiler_params=_params())(dm, gates, gates, ya, ys)


def _ssd_chunk_terms(dtr, dtb, alog):
    xx = dtr + dtb
    dt = jnp.maximum(xx, 0.0) + jnp.log(1.0 + jnp.exp(-jnp.abs(xx)))
    a = -jnp.exp(alog)
    li = lax.broadcasted_iota(jnp.int32, (CHUNK, CHUNK), 0)
    si = lax.broadcasted_iota(jnp.int32, (CHUNK, CHUNK), 1)
    causal = li >= si
    acum = _dot(causal.astype(F32), dt * a, 1, 0, HIGHEST)
    return xx, dt, a, acum, acum.T, causal


def _split2(x):
    hi = x.astype(BF16)
    return hi, (x - hi.astype(F32)).astype(BF16)


def _expand(v, e, exact=True):
    hi, lo = _split2(v)
    out = _dot(hi, e, 1, 0)
    return out + _dot(lo, e, 1, 0) if exact else out


def _segsum(s, e):
    hi, lo = _split2(s)
    return _dot(hi, e, 1, 1) + _dot(lo, e, 1, 1)


def _head_maps(di):
    nh = di // HEAD_DIM
    h = jnp.arange(DT_LANES)[:, None]
    e64 = (jnp.arange(di)[None, :] // HEAD_DIM == h).astype(BF16)
    e128 = (jnp.arange(nh * CHUNK)[None, :] // CHUNK == h).astype(BF16)
    return e64, e128


def _pair_blockdiag(p, left):
    zero = jnp.zeros_like(p)
    return jnp.concatenate([jnp.where(left, p, zero), jnp.where(left, zero, p)], axis=0)


def _ssd_fwd(xc, dtr, dtb, alog, dskx, di, name):
    t = xc.shape[0]
    dx = xc.shape[1]
    nc = t // CHUNK
    nh = di // HEAD_DIM
    hpg = nh // N_GROUPS
    gw = hpg * HEAD_DIM
    boff, coff = di, di + N_GROUPS * D_STATE
    e64, e128 = _head_maps(di)

    def body(xc_ref, dtr_ref, dtb_ref, alog_ref, dsk_ref, e64_ref, e128_ref, y_ref, st_ref, state):
        @pl.when(pl.program_id(0) == 0)
        def _():
            state[...] = jnp.zeros_like(state)
        _, dt, _, acum, acum_t, causal = _ssd_chunk_terms(dtr_ref[...], dtb_ref[...], alog_ref[...])
        last = acum[CHUNK - 1:CHUNK, :]
        e64v = e64_ref[...]
        dtx = _expand(dt, e64v, False)
        eax = _expand(jnp.exp(acum), e64v)
        dex = _expand(dt * jnp.exp(last - acum), e64v, False)
        acx = _expand(acum, e128_ref[...])
        st_ref[0] = state[...]
        left = lax.broadcasted_iota(jnp.int32, (CHUNK, 2 * HEAD_DIM), 1) < HEAD_DIM
        for g in range(N_GROUPS):
            gs = slice(g * gw, (g + 1) * gw)
            bg = xc_ref[:, boff + g * D_STATE:boff + (g + 1) * D_STATE]
            cg = xc_ref[:, coff + g * D_STATE:coff + (g + 1) * D_STATE]
            gm = _dot(cg, bg, 1, 1)
            xg = xc_ref[:, gs].astype(F32)
            xdb = (xg * dtx[:, gs]).astype(BF16)
            sin = state[:, gs]
            yo = _dot(cg, sin.astype(BF16), 1, 0) * eax[:, gs]
            for jp in range(hpg // 2):
                h0 = g * hpg + 2 * jp
                ps = slice(jp * 2 * HEAD_DIM, (jp + 1) * 2 * HEAD_DIM)
                ms = []
                for hh in (h0, h0 + 1):
                    seg = acx[:, hh * CHUNK:(hh + 1) * CHUNK] - acum_t[hh:hh + 1, :]
                    ms.append((gm * jnp.exp(jnp.where(causal, seg, -1e30))).astype(BF16))
                yd = _dot(jnp.concatenate(ms, axis=1), _pair_blockdiag(xdb[:, ps], left), 1, 0)
                col = slice(g * gw + jp * 2 * HEAD_DIM, g * gw + (jp + 1) * 2 * HEAD_DIM)
                y_ref[:, col] = (yd + yo[:, ps] + dsk_ref[:, col] * xg[:, ps]).astype(y_ref.dtype)
            xe = (xg * dex[:, gs]).astype(BF16)
            state[:, gs] = eax[CHUNK - 1:CHUNK, gs] * sin + _dot(bg, xe, 0, 0)

    small = pl.BlockSpec((1, DT_LANES), lambda c: (0, 0))
    whole = lambda a: pl.BlockSpec(a.shape, lambda c: (0, 0))
    return _pc(body, name=name, grid=(nc,),
               in_specs=[pl.BlockSpec((CHUNK, dx), lambda c: (c, 0)),
                         pl.BlockSpec((CHUNK, DT_LANES), lambda c: (c, 0)), small, small,
                         whole(dskx), whole(e64), whole(e128)],
               out_specs=[pl.BlockSpec((CHUNK, di), lambda c: (c, 0)),
                          pl.BlockSpec((1, D_STATE, di), lambda c: (c, 0, 0))],
               out_shape=[_sds((t, di), BF16), _sds((nc, D_STATE, di), F32)],
               scratch_shapes=[pltpu.VMEM((D_STATE, di), F32)],
               compiler_params=_params())(xc, dtr, dtb, alog, dskx, e64, e128)


def _ssd_bwd(xc, dtr, dy, states, dtb, alog, dskx, di, name):
    t = xc.shape[0]
    dx = xc.shape[1]
    nc = t // CHUNK
    nh = di // HEAD_DIM
    hpg = nh // N_GROUPS
    gw = hpg * HEAD_DIM
    boff, coff = di, di + N_GROUPS * D_STATE
    e64, e128 = _head_maps(di)

    def body(xc_ref, dtr_ref, dy_ref, st_ref, dtb_ref, alog_ref, dsk_ref, e64_ref, e128_ref,
             dxc_ref, ddtr_ref, sm_ref, dstate, darow):
        @pl.when(pl.program_id(0) == 0)
        def _():
            dstate[...] = jnp.zeros_like(dstate)
            sm_ref[...] = jnp.zeros_like(sm_ref)
        darow[...] = jnp.zeros_like(darow)
        xx, dt, a, acum, acum_t, causal = _ssd_chunk_terms(dtr_ref[...], dtb_ref[...], alog_ref[...])
        last = acum[CHUNK - 1:CHUNK, :]
        e64v = e64_ref[...]
        dtx = _expand(dt, e64v, False)
        eax = _expand(jnp.exp(acum), e64v)
        eex = _expand(jnp.exp(last - acum), e64v, False)
        acx = _expand(acum, e128_ref[...])
        left = lax.broadcasted_iota(jnp.int32, (CHUNK, 2 * HEAD_DIM), 1) < HEAD_DIM
        lane = lax.broadcasted_iota(jnp.int32, (CHUNK, DT_LANES), 1)
        sub8 = lax.broadcasted_iota(jnp.int32, (8, gw), 0)
        da_col = jnp.zeros((CHUNK, DT_LANES), F32)
        ddt_col = jnp.zeros((CHUNK, DT_LANES), F32)
        rows = jnp.zeros((8, DT_LANES), F32)
        for g in range(N_GROUPS):
            gs = slice(g * gw, (g + 1) * gw)
            bg = xc_ref[:, boff + g * D_STATE:boff + (g + 1) * D_STATE]
            cg = xc_ref[:, coff + g * D_STATE:coff + (g + 1) * D_STATE]
            gm = _dot(cg, bg, 1, 1)
            e64g = e64v[:, gs]
            xg = xc_ref[:, gs].astype(F32)
            dtg, eag, eeg = dtx[:, gs], eax[:, gs], eex[:, gs]
            xd = xg * dtg
            xdb = xd.astype(BF16)
            dyb = dy_ref[:, gs]
            dyf = dyb.astype(F32)
            sin = st_ref[0, :, gs]
            sinb = sin.astype(BF16)
            ds = dstate[:, gs]
            dsb = ds.astype(BF16)
            bds = _dot(bg, dsb, 1, 0)
            dyeb = (dyf * eag).astype(BF16)
            dcg = _dot(dyeb, sinb, 1, 1)
            dstate[:, gs] = eag[CHUNK - 1:CHUNK, :] * ds + _dot(cg, dyeb, 0, 0)
            yo = _dot(cg, sinb, 1, 0) * eag
            xe = xd * eeg
            dbg = _dot(xe.astype(BF16), dsb, 1, 1)
            wterm = bds * xe
            da_col = da_col + _segsum(dyf * yo - wterm, e64g)
            dg = jnp.zeros((CHUNK, CHUNK), F32)
            dxd_parts = []
            for jp in range(hpg // 2):
                h0 = g * hpg + 2 * jp
                ps = slice(jp * 2 * HEAD_DIM, (jp + 1) * 2 * HEAD_DIM)
                lms, mfs = [], []
                for hh in (h0, h0 + 1):
                    seg = acx[:, hh * CHUNK:(hh + 1) * CHUNK] - acum_t[hh:hh + 1, :]
                    lm = jnp.exp(jnp.where(causal, seg, -1e30))
                    lms.append(lm)
                    mfs.append(gm * lm)
                mstack = jnp.concatenate([m.astype(BF16) for m in mfs], axis=0)
                dyp = dyb[:, ps]
                dxd_parts.append(_dot(mstack, _pair_blockdiag(dyp, left), 0, 0))
                dm2 = _dot(dyp, _pair_blockdiag(xdb[:, ps], left), 1, 1)
                for k, hh in enumerate((h0, h0 + 1)):
                    dm = dm2[:, k * CHUNK:(k + 1) * CHUNK]
                    dg = dg + dm * lms[k]
                    q = dm * mfs[k]
                    da_col = da_col + jnp.where(lane == hh, jnp.sum(q, axis=1, keepdims=True), 0.0)
                    darow[hh:hh + 1, :] = -jnp.sum(q, axis=0, keepdims=True)
            dxd = jnp.concatenate(dxd_parts, axis=1) + bds * eeg
            ddt_col = ddt_col + _segsum(dxd * xg, e64g)
            rsum = (jnp.where(sub8 == 0, jnp.sum(wterm, axis=0, keepdims=True), 0.0)
                    + jnp.where(sub8 == 1, jnp.sum(ds * sin, axis=0, keepdims=True), 0.0)
                    + jnp.where(sub8 == 2, jnp.sum(dyf * xg, axis=0, keepdims=True), 0.0))
            rows = rows + _segsum(rsum, e64g)
            dxc_ref[:, gs] = (dxd * dtg + dsk_ref[:, gs] * dyf).astype(dxc_ref.dtype)
            dgb = dg.astype(BF16)
            dxc_ref[:, boff + g * D_STATE:boff + (g + 1) * D_STATE] = (
                dbg + _dot(dgb, cg, 0, 0)).astype(dxc_ref.dtype)
            dxc_ref[:, coff + g * D_STATE:coff + (g + 1) * D_STATE] = (
                dcg + _dot(dgb, bg, 1, 0)).astype(dxc_ref.dtype)
        at_last = rows[0:1, :] + jnp.exp(last) * rows[1:2, :]
        is_last = lax.broadcasted_iota(jnp.int32, (CHUNK, DT_LANES), 0) == CHUNK - 1
        da = da_col + jnp.where(is_last, at_last, 0.0) + darow[...].T
        li = lax.broadcasted_iota(jnp.int32, (CHUNK, CHUNK), 0)
        si = lax.broadcasted_iota(jnp.int32, (CHUNK, CHUNK), 1)
        dla = _dot((si >= li).astype(F32), da, 1, 0, HIGHEST)
        ddtr = (ddt_col + dla * a) * _sigmoid(xx)
        ddtr_ref[...] = ddtr
        sm_ref[0:1, :] += jnp.sum(ddtr, axis=0, keepdims=True)
        sm_ref[1:2, :] += jnp.sum(dla * dt, axis=0, keepdims=True) * a
        sm_ref[2:3, :] += rows[2:3, :]

    small = pl.BlockSpec((1, DT_LANES), lambda c: (0, 0))
    whole = lambda a: pl.BlockSpec(a.shape, lambda c: (0, 0))
    rev = lambda c: (nc - 1 - c, 0)
    return _pc(body, name=name, grid=(nc,),
               in_specs=[pl.BlockSpec((CHUNK, dx), rev), pl.BlockSpec((CHUNK, DT_LANES), rev),
                         pl.BlockSpec((CHUNK, di), rev),
                         pl.BlockSpec((1, D_STATE, di), lambda c: (nc - 1 - c, 0, 0)), small, small,
                         whole(dskx), whole(e64), whole(e128)],
               out_specs=[pl.BlockSpec((CHUNK, dx), rev), pl.BlockSpec((CHUNK, DT_LANES), rev),
                          pl.BlockSpec((8, DT_LANES), lambda c: (0, 0))],
               out_shape=[_sds((t, dx), BF16), _sds((t, DT_LANES), F32), _sds((8, DT_LANES), F32)],
               scratch_shapes=[pltpu.VMEM((D_STATE, di), F32), pltpu.VMEM((DT_LANES, CHUNK), F32)],
               compiler_params=_params())(xc, dtr, dy, states, dtb, alog, dskx, e64, e128)


def _adamw(parts, w, m, v, name):
    npart, rows, width = parts.shape
    if rows * width <= SMALL_PARAM:
        tr, tw = rows, width
    else:
        tr, tw = (_pick(rows, (64, 32, 16, 8)), width) if rows % 8 == 0 else (rows, 128)
    c1 = 1.0 - ADAM_B1 ** ADAM_STEP
    c2 = 1.0 - ADAM_B2 ** ADAM_STEP

    row_strips = _strips(tr) if tr % STRIP == 0 else [slice(0, tr)]
    col_chunks = [slice(c, c + 512) for c in range(0, tw, 512)] if tw % 512 == 0 else [slice(0, tw)]

    def body(p_ref, w_ref, m_ref, v_ref, g_ref, d_ref, nm_ref, nv_ref):
        for rows in row_strips:
            for cols in col_chunks:
                g = p_ref[0, rows, cols].astype(F32)
                for p in range(1, npart):
                    g = g + p_ref[p, rows, cols].astype(F32)
                nm = ADAM_B1 * m_ref[rows, cols] + (1.0 - ADAM_B1) * g
                nv = ADAM_B2 * v_ref[rows, cols] + (1.0 - ADAM_B2) * (g * g)
                g_ref[rows, cols] = g
                nm_ref[rows, cols] = nm
                nv_ref[rows, cols] = nv
                d_ref[rows, cols] = -ADAM_LR * ((nm / c1) / (jnp.sqrt(nv / c2) + ADAM_EPS)
                                                + ADAM_WD * w_ref[rows, cols])

    blk = pl.BlockSpec((tr, tw), lambda i, j: (i, j))
    return _pc(body, name=name, grid=(rows // tr, width // tw),
               in_specs=[pl.BlockSpec((npart, tr, tw), lambda i, j: (0, i, j)), blk, blk, blk],
               out_specs=[blk] * 4, out_shape=[_sds((rows, width), F32)] * 4,
               compiler_params=_params())(parts, w, m, v)


def _sum_parts(parts, name, tile=None):
    npart, rows, width = parts.shape
    tile = rows if tile is None else tile

    def body(p_ref, o_ref):
        for rows_ in _strips(tile, 8 if parts.dtype == F32 else STRIP):
            g = p_ref[0, rows_, :].astype(F32)
            for p in range(1, npart):
                g = g + p_ref[p, rows_, :].astype(F32)
            o_ref[rows_, :] = g

    return _pc(body, name=name, grid=(rows // tile,),
               in_specs=[pl.BlockSpec((npart, tile, width), lambda i: (0, i, 0))],
               out_specs=pl.BlockSpec((tile, width), lambda i: (i, 0)),
               out_shape=_sds((rows, width), F32), compiler_params=_params())(parts)


def _flip(k):
    x, y, c = lax.axis_index("x"), lax.axis_index("y"), lax.axis_index("c")
    px = 1 - x if k & 4 else x
    py = 1 - y if k & 2 else y
    pc = 1 - c if k & 1 else c
    return (px, py, pc), 4 * px + 2 * py + pc


DIRECT = tuple((k, 0) for k in range(1, N_DEV))
TO_CHIPS = ((1, 0), (2, 0), (4, 0), (6, 0))
TO_SIBLING = ((1, 2), (1, 4), (1, 6))


def _copies(arrays, lands, send_sems, recv_sems, scatter, moves):
    _, me = _flip(0)
    outgoing, incoming = [], []
    for i, (kd, kb) in enumerate(moves):
        peer, pidx = _flip(kd)
        _, out_slot = _flip(kb)
        _, in_slot = _flip(kd ^ kb)
        for j, land_ref in enumerate(lands):
            if kb:
                src = land_ref.at[out_slot]
            else:
                src = arrays[j].at[pidx] if scatter[j] else arrays[j]
            sem = len(lands) * i + j
            for dst, bucket in ((land_ref.at[out_slot], outgoing), (land_ref.at[in_slot], incoming)):
                bucket.append(pltpu.make_async_remote_copy(
                    src_ref=src, dst_ref=dst, send_sem=send_sems.at[sem], recv_sem=recv_sems.at[sem],
                    device_id=peer, device_id_type=MESH))
    return outgoing, incoming


HBM_SPEC = pl.BlockSpec(memory_space=pltpu.HBM)
SEM_SPEC = pl.BlockSpec(memory_space=pltpu.SEMAPHORE)
ANY_SPEC = pl.BlockSpec(memory_space=pl.ANY)
EFFECT = pltpu.SideEffectType.DATAFLOW_SIDE_EFFECTING


def _landing_zones(arrays, scatter):
    _, me = _flip(0)
    lands = []
    for a, sc in zip(arrays, scatter):
        own = lax.dynamic_index_in_dim(a, me, 0, keepdims=True) if sc else a[None]
        shape = a.shape if sc else (N_DEV,) + a.shape
        lands.append(lax.dynamic_update_slice(lax.empty(shape, a.dtype), own, (me,) + (0,) * (len(shape) - 1)))
    return lands


def _xchg_start(arrays, scatter, after, name, moves=DIRECT, lands=None):
    if lands is None:
        lands = _landing_zones(arrays, scatter)
    na, nl = len(arrays), len(lands)

    def body(*refs):
        ins, outs = refs[:na + nl], refs[na + nl + 1:]
        outgoing, _ = _copies(ins[:na], ins[na:], outs[0], outs[1], scatter, moves)
        for cp in outgoing:
            cp.start()
        outs[-1][...] = jnp.zeros_like(outs[-1])

    nsem = nl * len(moves)
    operands = [pltpu.with_memory_space_constraint(a, pltpu.HBM) for a in list(arrays) + list(lands)]
    out = _pc(body, name=name,
              out_shape=(pltpu.SemaphoreType.DMA((nsem,)), pltpu.SemaphoreType.DMA((nsem,)),
                         *[pltpu.HBM(a.shape, a.dtype) for a in operands], _sds((8, 128), F32)),
              in_specs=[HBM_SPEC] * (na + nl) + [ANY_SPEC],
              out_specs=(SEM_SPEC, SEM_SPEC, *[HBM_SPEC] * (na + nl), pl.BlockSpec(memory_space=pltpu.VMEM)),
              input_output_aliases={i: 2 + i for i in range(na + nl)},
              compiler_params=pltpu.CompilerParams(has_side_effects=EFFECT))(*operands, after)
    return dict(sems=out[:2], thru=out[2:2 + na + nl], token=out[-1], scatter=scatter, na=na, moves=moves)


def _xchg_wait(handle, after, name):
    na, scatter, moves, thru = handle["na"], handle["scatter"], handle["moves"], handle["thru"]
    n = len(thru)

    def body(*refs):
        ins = refs[:n]
        outgoing, incoming = _copies(ins[:na], ins[na:], refs[n], refs[n + 1], scatter, moves)
        for cp in outgoing:
            cp.wait_send()
        for cp in incoming:
            cp.wait_recv()

    out = _pc(body, name=name, out_shape=tuple(pltpu.HBM(a.shape, a.dtype) for a in thru),
              in_specs=[HBM_SPEC] * n + [SEM_SPEC, SEM_SPEC, ANY_SPEC], out_specs=tuple([HBM_SPEC] * n),
              input_output_aliases={i: i for i in range(n)},
              compiler_params=pltpu.CompilerParams(has_side_effects=EFFECT))(*thru, *handle["sems"], after)
    return out[na:]


def _pack(arrs, width, row_mult):
    flat = jnp.concatenate([a.reshape(-1) for a in arrs])
    n = flat.shape[0]
    rows = -(-n // (width * row_mult)) * row_mult
    return jnp.pad(flat, (0, rows * width - n)).reshape(rows, width)


def _unpack(packed, shapes, lead=None):
    out, off = [], 0
    flat = packed.reshape(-1) if lead is None else packed.reshape(lead, -1)
    for s in shapes:
        n = math.prod(s)
        if lead is None:
            out.append(flat[off:off + n].reshape(s))
        else:
            out.append(flat[:, off:off + n].reshape((lead,) + tuple(s)))
        off += n
    return out


def _blocks_to_cols(blocks):
    nb, rows, n = blocks.shape
    return blocks.transpose(1, 0, 2).reshape(rows, nb * n)


def _pad_rows(a, rows):
    return jnp.pad(a, ((0, rows - a.shape[0]), (0, 0)))


def _pad_lanes(a, lanes):
    return jnp.pad(a, ((0, 0), (0, lanes - a.shape[1])))


REST = ("w_a_out", "w_s_out", "w_o", "w_up", "w_down")
TRANSPOSED = ("w_up", "w_in")
CONVS = ("conv_a_w", "ssd_conv_w", "ffn_conv_w")
REPL = ("norm_mix_w", "ssd_conv_b", "dt_bias", "a_log", "d_skip", "ssd_norm_w", "norm_ffn_w", "ffn_conv_b",
        "final_norm_w")
ORDER = ("norm_mix_w", "w_in", "conv_a_w", "w_a_out", "ssd_conv_w", "ssd_conv_b", "dt_bias", "a_log", "d_skip",
         "ssd_norm_w", "w_s_out", "w_o", "norm_ffn_w", "w_up", "ffn_conv_w", "ffn_conv_b", "w_down", "final_norm_w")


def _as_rows(name, block):
    return block[0].T if name in TRANSPOSED else block[0]


def kernel(x, norm_mix_w, w_in, conv_a_w, w_a_out, ssd_conv_w, ssd_conv_b, dt_bias, a_log, d_skip, ssd_norm_w, w_s_out, w_o, norm_ffn_w, w_up, ffn_conv_w, ffn_conv_b, w_down, final_norm_w, loss_target, m_norm_mix_w, m_w_in, m_conv_a_w, m_w_a_out, m_ssd_conv_w, m_ssd_conv_b, m_dt_bias, m_a_log, m_d_skip, m_ssd_norm_w, m_w_s_out, m_w_o, m_norm_ffn_w, m_w_up, m_ffn_conv_w, m_ffn_conv_b, m_w_down, m_final_norm_w, v_norm_mix_w, v_w_in, v_conv_a_w, v_w_a_out, v_ssd_conv_w, v_ssd_conv_b, v_dt_bias, v_a_log, v_d_skip, v_ssd_norm_w, v_w_s_out, v_w_o, v_norm_ffn_w, v_w_up, v_ffn_conv_w, v_ffn_conv_b, v_w_down, v_final_norm_w):
    wts = dict(norm_mix_w=norm_mix_w, w_in=w_in, conv_a_w=conv_a_w, w_a_out=w_a_out, ssd_conv_w=ssd_conv_w,
               ssd_conv_b=ssd_conv_b, dt_bias=dt_bias, a_log=a_log, d_skip=d_skip, ssd_norm_w=ssd_norm_w,
               w_s_out=w_s_out, w_o=w_o, norm_ffn_w=norm_ffn_w, w_up=w_up, ffn_conv_w=ffn_conv_w,
               ffn_conv_b=ffn_conv_b, w_down=w_down, final_norm_w=final_norm_w)
    mom1 = dict(norm_mix_w=m_norm_mix_w, w_in=m_w_in, conv_a_w=m_conv_a_w, w_a_out=m_w_a_out,
                ssd_conv_w=m_ssd_conv_w, ssd_conv_b=m_ssd_conv_b, dt_bias=m_dt_bias, a_log=m_a_log, d_skip=m_d_skip,
                ssd_norm_w=m_ssd_norm_w, w_s_out=m_w_s_out, w_o=m_w_o, norm_ffn_w=m_norm_ffn_w, w_up=m_w_up,
                ffn_conv_w=m_ffn_conv_w, ffn_conv_b=m_ffn_conv_b, w_down=m_w_down, final_norm_w=m_final_norm_w)
    mom2 = dict(norm_mix_w=v_norm_mix_w, w_in=v_w_in, conv_a_w=v_conv_a_w, w_a_out=v_w_a_out,
                ssd_conv_w=v_ssd_conv_w, ssd_conv_b=v_ssd_conv_b, dt_bias=v_dt_bias, a_log=v_a_log, d_skip=v_d_skip,
                ssd_norm_w=v_ssd_norm_w, w_s_out=v_w_s_out, w_o=v_w_o, norm_ffn_w=v_norm_ffn_w, w_up=v_w_up,
                ffn_conv_w=v_ffn_conv_w, ffn_conv_b=v_ffn_conv_b, w_down=v_w_down, final_norm_w=v_final_norm_w)

    t, d = x.shape[1], x.shape[2]
    di = 2 * d
    nh = di // HEAD_DIM
    dxw = di + 2 * N_GROUPS * D_STATE
    f = w_down.shape[1] * N_DEV
    n_in = w_in.shape[2] * N_DEV
    me = 4 * lax.axis_index("x") + 2 * lax.axis_index("y") + lax.axis_index("c")

    rest_local = [_as_rows(k, wts[k]).astype(BF16) for k in REST]
    nrows = [a.shape[0] for a in rest_local]
    n_blk = w_in.shape[2]
    in_local = w_in[0].T.astype(BF16)
    conv_shapes = [wts[k].shape[1:] for k in CONVS]
    conv_local = _pack([wts[k] for k in CONVS], d, 8)
    x2, tgt = x[0], loss_target[0]
    h_in = _xchg_start([in_local, conv_local], [False, False], x2, "gather_in_start", moves=TO_CHIPS)
    u = _rms_fwd(x2, norm_mix_w, h_in["token"], "norm_mix")
    part = _xchg_wait(h_in, u, "gather_in_wait")
    h_fwd = _xchg_start([], [False, False], u, "gather_in_forward_start", moves=TO_SIBLING, lands=part)
    in_all, conv_all = _xchg_wait(h_fwd, u, "gather_in_forward_wait")
    win_t = in_all.reshape(n_in, d)
    h_rest = _xchg_start(rest_local, [False] * len(REST), in_all, "gather_rest_start")
    c_a, c_s, c_f = _unpack(conv_all, conv_shapes, N_DEV)
    caw, scw, fcw = _blocks_to_cols(c_a), _blocks_to_cols(c_s), _blocks_to_cols(c_f)

    o_z, o_x, o_dt = 5 * d, 7 * d, 7 * d + dxw
    seg_bounds = [0, d, 2 * d, 3 * d, 4 * d, o_z, o_x, o_dt]
    w_dt = _pad_rows(win_t[o_dt:], DT_LANES)
    dtb, alog = (_pad_lanes(p[...].reshape(1, nh), DT_LANES) for p in (dt_bias, a_log))
    dskx = jnp.repeat(d_skip.reshape(1, nh), HEAD_DIM, axis=1)

    tok = h_rest["token"]
    gates = _mm([(u, 0, d, win_t, 0)], "nt", BF16, "proj_gates", n=2 * d, after=tok)
    pa = _mm([(u, 0, d, win_t, 2 * d)], "nt", BF16, "proj_a", n=3 * d, after=tok)
    z = _mm([(u, 0, d, win_t, o_z)], "nt", BF16, "proj_z", n=2 * d, after=tok)
    xbc = _mm([(u, 0, d, win_t, o_x)], "nt", BF16, "proj_xbc", n=dxw, after=tok)
    dtr = _mm([(u, w_dt)], "nt", F32, "proj_dt", after=tok)
    ya_in, q_a = _conv_a_fwd(pa, caw, d, "conv_a")
    xc, pre_s = _conv_s_fwd(xbc, scw, ssd_conv_b, "conv_s")
    y, states = _ssd_fwd(xc, dtr, dtb, alog, dskx, di, "ssd")
    yn = _gnorm_fwd(y, z, ssd_norm_w, "gnorm")
    rest_all = _xchg_wait(h_rest, yn, "gather_rest_wait")
    waout, wsout, wo, wup_t, wdown = (a.reshape(N_DEV * n, d) for a, n in zip(rest_all, nrows))
    y_a = _mm([(ya_in, waout)], "nn", BF16, "a_out")
    y_s = _mm([(yn, wsout)], "nn", BF16, "s_out")
    merged = _merge_fwd(gates, y_a, y_s, d, "merge")
    mo = _mm([(merged, wo)], "nn", BF16, "o_proj")
    h1, v = _resnorm_fwd(x2, mo, norm_ffn_w, "norm_ffn")
    hv = _mm([(v, wup_t)], "nt", BF16, "up_proj", tm=512, resident_b=True)
    act, c1 = _ffn_fwd(hv, fcw, ffn_conv_b, f, "ffn_act")
    dd = _mm([(act, wdown)], "nn", BF16, "down_proj")
    loss11, dh2, dh2b, g_fnw = _final(h1, dd, tgt, final_norm_w.reshape(1, d), "final")

    dact = _mm([(dh2b, wdown)], "nt", BF16, "d_act", resident_b=True)
    gw_down = _mm_tn(act, dh2b, "gw_down")
    dh1f, dh3, g_ffn = _ffn_bwd(hv, c1, dact, fcw, f, "ffn_act_bwd")
    dv = _mm([(dh1f, 0, f, wup_t, 0), (dh3, 0, f, wup_t, f)], "nn", BF16, "d_v")
    gw_up_t = jnp.concatenate([_mm_tn(dh1f, v, "gw_up1"), _mm_tn(dh3, v, "gw_up3")], axis=0)
    dh1, dh1b, g_nfw = _rms_bwd(h1, dv, norm_ffn_w, dh2, "norm_ffn_bwd")
    dmerged = _mm([(dh1b, wo)], "nt", BF16, "d_merged")
    gw_o = _mm_tn(merged, dh1b, "gw_o")
    dya, dys, dga, dgs = _merge_bwd(dmerged, gates, y_a, y_s, d, "merge_bwd")
    dyain = _mm([(dya, waout)], "nt", BF16, "d_ya_in")
    gw_aout = _mm_tn(ya_in, dya, "gw_a_out")
    db, dc, dvv, g_caw = _conv_a_bwd(pa, q_a, dyain, caw, d, "conv_a_bwd")
    dyn = _mm([(dys, wsout)], "nt", BF16, "d_yn", resident_b=True)
    gw_sout = _mm_tn(yn, dys, "gw_s_out")
    grads_rest = dict(w_a_out=gw_aout, w_s_out=gw_sout, w_o=gw_o, w_up=gw_up_t, w_down=gw_down)
    rest_parts = [grads_rest[k].reshape(N_DEV, n, d) for k, n in zip(REST, nrows)]
    h_grest = _xchg_start(rest_parts, [True] * len(REST), rest_parts[0], "scatter_rest_start")
    dy, dz, g_snw = _gnorm_bwd(y, z, dyn, ssd_norm_w, "gnorm_bwd")
    dtb_after = dtb + h_grest["token"][0:1, 0:1]
    dxc, ddtr, g_ssd = _ssd_bwd(xc, dtr, dy, states, dtb_after, alog, dskx, di, "ssd_bwd")
    dxbc, g_scw = _conv_s_bwd(xbc, pre_s, dxc, scw, "conv_s_bwd")
    dsegs = [dga, dgs, db, dc, dvv, dz, dxbc, ddtr.astype(BF16)]
    pairs = [(s, c, d, win_t, a + c * d) for s, a in zip(dsegs[:-1], seg_bounds) for c in range(s.shape[1] // d)]
    pairs.append((dsegs[-1], w_dt))
    gw_in = [_mm_tn(s, u, "gw_in%d" % i) for i, s in enumerate(dsegs)]
    gw_in_t = jnp.concatenate(gw_in[:-1] + [gw_in[-1][:nh]], axis=0)
    in_parts = gw_in_t.reshape(N_DEV, n_blk, d)
    h_gin = _xchg_start([in_parts], [True], in_parts, "scatter_in_start")
    du = _mm(pairs, "nn", BF16, "d_u", tm=512, tn=1024, after=h_gin["token"], resident_b=True)
    dx, _, g_nmw = _rms_bwd(x2, du, norm_mix_w, dh1, "norm_mix_bwd")

    small_grads = dict(norm_mix_w=g_nmw[0], ssd_conv_b=g_scw[4], dt_bias=g_ssd[0, :nh], a_log=g_ssd[1, :nh],
                       d_skip=g_ssd[2, :nh], ssd_norm_w=g_snw[0], norm_ffn_w=g_nfw[0], ffn_conv_b=g_ffn[3],
                       final_norm_w=g_fnw[0], conv_a_w=g_caw[:3], ssd_conv_w=g_scw[:4], ffn_conv_w=g_ffn[:3])
    small_names = REPL + CONVS
    small_parts = _pack([small_grads[k] for k in small_names] + [loss11], d, 8)
    h_small = _xchg_start([small_parts], [False], small_parts, "gather_small_start")
    rest_recv = _xchg_wait(h_grest, dx, "scatter_rest_wait")
    (in_recv,) = _xchg_wait(h_gin, rest_recv[0], "scatter_in_wait")
    (small_all,) = _xchg_wait(h_small, in_recv, "gather_small_wait")
    small_sum = _sum_parts(small_all, "sum_small_grads")
    *small_list, loss = _unpack(small_sum, [small_grads[k].shape for k in small_names] + [()])
    small_g = dict(zip(small_names, small_list))

    res = {}

    def update(k, parts):
        outs = _adamw(parts, *(_as_rows(k, src[k]) for src in (wts, mom1, mom2)), "adamw_" + k)
        for kind, a in zip(("g", "d", "m", "v"), outs):
            res[kind, k] = (a.T if k in TRANSPOSED else a)[None]

    update("w_in", in_recv)
    for k, parts in zip(REST, rest_recv):
        update(k, parts)
    local_g = {}
    for k in REPL:
        local_g[k] = small_g[k].reshape(wts[k].shape)
    for k in CONVS:
        n = wts[k].shape[2]
        local_g[k] = lax.dynamic_slice_in_dim(small_g[k], me * n, n, axis=1)[None]
    for k in small_names:
        as2d = lambda a: a.reshape(-1, a.shape[-1])
        outs = _adamw(as2d(local_g[k])[None], *(as2d(src[k]) for src in (wts, mom1, mom2)), "adamw_" + k)
        for kind, a in zip(("g", "d", "m", "v"), outs):
            res[kind, k] = a.reshape(wts[k].shape)

    return (loss, dx[None], *[res["g", k] for k in ORDER], *[res["d", k] for k in ORDER],
            *[res["m", k] for k in ORDER], *[res["v", k] for k in ORDER])
```

```python
import functools
import math

import jax
import jax.numpy as jnp
from jax import lax
from jax.experimental import pallas as pl
from jax.experimental.pallas import tpu as pltpu

F32 = jnp.float32
BF16 = jnp.bfloat16
EPS = 1e-5
HEAD_DIM = 64
N_GROUPS = 4
D_STATE = 128
CHUNK = 128
DT_LANES = 128
HALO = 16
STRIP = 16
SMALL_PARAM = 16 * 1024
N_DEV = 8
V7X_VMEM_LIMIT = 56 * 1024 * 1024
ADAM_LR, ADAM_B1, ADAM_B2, ADAM_EPS, ADAM_WD, ADAM_STEP = 0.001, 0.9, 0.999, 1e-08, 0.01, 10
HIGHEST = lax.Precision.HIGHEST
MESH = pl.DeviceIdType.MESH


def _pc(body, **kw):
    return pl.pallas_call(body, **kw)


def _params():
    return pltpu.CompilerParams(vmem_limit_bytes=V7X_VMEM_LIMIT)


def _pick(n, cands):
    for c in cands:
        if n % c == 0:
            return c
    return n


def _dot(a, b, ca, cb, prec=None):
    return lax.dot_general(a, b, (((ca,), (cb,)), ((), ())), preferred_element_type=F32, precision=prec)


def _sigmoid(x):
    return 0.5 * jnp.tanh(0.5 * x) + 0.5


def _sds(shape, dtype):
    return jax.ShapeDtypeStruct(shape, dtype)


def _mm(pairs, mode, out_dtype, name, n=None, tm=1024, tn=1024, after=None, resident_b=False):
    pairs = [p if len(p) == 5 else (p[0], 0, p[0].shape[1], p[1], 0) for p in pairs]
    m = pairs[0][0].shape[0]
    if n is None:
        n = pairs[0][3].shape[1] if mode == "nn" else pairs[0][3].shape[0]
    tm = min(tm, m)
    rows_nt = [p[4] for p in pairs] if mode == "nt" else []
    tn = next(c for c in ((n,) if resident_b else ()) + (tn, 1408, 512, 256, 128)
              if n % c == 0 and all(r % c == 0 for r in rows_nt))
    npair = len(pairs)
    cb = 0 if mode == "nn" else 1

    def body(*refs):
        o_ref = refs[-1]
        acc = None
        for p in range(npair):
            part = _dot(refs[2 * p][...], refs[2 * p + 1][...], 1, cb)
            acc = part if acc is None else acc + part
        o_ref[...] = acc.astype(o_ref.dtype)

    in_specs, args = [], []
    for a, a_col, kk, b, b_row in pairs:
        in_specs.append(pl.BlockSpec((tm, kk), lambda i, j, c=a_col: (i, c)))
        if mode == "nn":
            assert b_row % kk == 0 and (not resident_b or n == tn)
            in_specs.append(pl.BlockSpec((kk, tn), lambda i, j, r=b_row // kk: (r, j),
                                         pipeline_mode=pl.Buffered(1) if resident_b else None))
        else:
            in_specs.append(pl.BlockSpec((tn, kk), lambda i, j, r=b_row // tn: (r + j, 0),
                                         pipeline_mode=pl.Buffered(1) if resident_b and tn == n else None))
        args += [a, b]
    if after is not None:
        in_specs.append(pl.BlockSpec(memory_space=pl.ANY))
        args.append(after)
    return _pc(body, name=name, grid=(m // tm, n // tn), in_specs=in_specs,
               out_specs=pl.BlockSpec((tm, tn), lambda i, j: (i, j)),
               out_shape=_sds((m, n), out_dtype), compiler_params=_params())(*args)


def _mm_tn(a, b, name, tm=2048):
    m, ka = a.shape
    nb = b.shape[1]
    tm = min(tm, m)
    nm = m // tm
    tk = _pick(ka, (1024, 1408, 512, 256, 128))
    tn = _pick(nb, (1024, 512, 256, 128))

    def body(a_ref, b_ref, o_ref, acc):
        t = pl.program_id(2)

        @pl.when(t == 0)
        def _():
            acc[...] = jnp.zeros_like(acc)
        acc[...] += _dot(a_ref[...], b_ref[...], 0, 0)

        @pl.when(t == nm - 1)
        def _():
            o_ref[...] = acc[...].astype(o_ref.dtype)

    return _pc(body, name=name, grid=(ka // tk, nb // tn, nm),
               in_specs=[pl.BlockSpec((tm, tk), lambda i, j, t: (t, i)),
                         pl.BlockSpec((tm, tn), lambda i, j, t: (t, j))],
               out_specs=pl.BlockSpec((tk, tn), lambda i, j, t: (i, j)),
               out_shape=_sds((ka, nb), BF16), scratch_shapes=[pltpu.VMEM((tk, tn), F32)],
               compiler_params=_params())(a, b)


def _strips(tm, strip=STRIP):
    return [slice(r * strip, (r + 1) * strip) for r in range(tm // strip)]


def _fold8(a):
    out = a[0:8, :]
    for r in range(8, a.shape[0], 8):
        out = out + a[r:r + 8, :]
    return out


def _colsum(a8):
    return jnp.sum(a8, axis=0, keepdims=True)


def _rms_fwd(x, w, after, name):
    t, d = x.shape
    tm = min(1024, t)

    def body(x_ref, w_ref, after_ref, o_ref):
        wv = w_ref[...]
        for rows in _strips(tm):
            xv = x_ref[rows, :]
            r = lax.rsqrt(jnp.mean(xv * xv, axis=-1, keepdims=True) + EPS)
            o_ref[rows, :] = (xv * r * wv).astype(o_ref.dtype)

    return _pc(body, name=name, grid=(t // tm,),
               in_specs=[pl.BlockSpec((tm, d), lambda i: (i, 0)), pl.BlockSpec((1, d), lambda i: (0, 0)),
                         pl.BlockSpec(memory_space=pl.ANY)],
               out_specs=pl.BlockSpec((tm, d), lambda i: (i, 0)),
               out_shape=_sds((t, d), BF16), compiler_params=_params())(x, w, after)


def _resnorm_fwd(x, mo, w, name):
    t, d = x.shape
    tm = min(1024, t)

    def body(x_ref, mo_ref, w_ref, h_ref, v_ref):
        wv = w_ref[...]
        for rows in _strips(tm):
            h = x_ref[rows, :] + mo_ref[rows, :].astype(F32)
            r = lax.rsqrt(jnp.mean(h * h, axis=-1, keepdims=True) + EPS)
            h_ref[rows, :] = h
            v_ref[rows, :] = (h * r * wv).astype(v_ref.dtype)

    row = pl.BlockSpec((tm, d), lambda i: (i, 0))
    return _pc(body, name=name, grid=(t // tm,),
               in_specs=[row, row, pl.BlockSpec((1, d), lambda i: (0, 0))],
               out_specs=[row, row], out_shape=[_sds((t, d), F32), _sds((t, d), BF16)],
               compiler_params=_params())(x, mo, w)


def _rms_bwd(h, dy, w, dres, name):
    t, d = h.shape
    tm = min(1024, t)

    def body(h_ref, dy_ref, w_ref, dres_ref, dx_ref, dxb_ref, dw_ref):
        @pl.when(pl.program_id(0) == 0)
        def _():
            dw_ref[...] = jnp.zeros_like(dw_ref)
        wv = w_ref[...]
        acc = jnp.zeros((8, d), F32)
        for rows in _strips(tm):
            hv = h_ref[rows, :]
            dyv = dy_ref[rows, :].astype(F32)
            r = lax.rsqrt(jnp.mean(hv * hv, axis=-1, keepdims=True) + EPS)
            n = hv * r
            dn = dyv * wv
            acc = acc + _fold8(dyv * n)
            dx = dres_ref[rows, :] + r * (dn - n * jnp.mean(dn * n, axis=-1, keepdims=True))
            dx_ref[rows, :] = dx
            dxb_ref[rows, :] = dx.astype(BF16)
        dw_ref[0:1, :] += _colsum(acc)

    row = pl.BlockSpec((tm, d), lambda i: (i, 0))
    return _pc(body, name=name, grid=(t // tm,),
               in_specs=[row, row, pl.BlockSpec((1, d), lambda i: (0, 0)), row],
               out_specs=[row, row, pl.BlockSpec((8, d), lambda i: (0, 0))],
               out_shape=[_sds((t, d), F32), _sds((t, d), BF16), _sds((8, d), F32)],
               compiler_params=_params())(h, dy, w, dres)


def _final(h1, dd, tgt, w, name):
    t, d = h1.shape
    tm = min(1024, t)
    nt = t // tm

    def body(h1_ref, dd_ref, tgt_ref, w_ref, loss_ref, dh_ref, dhb_ref, dw_ref, acc):
        i = pl.program_id(0)

        @pl.when(i == 0)
        def _():
            dw_ref[...] = jnp.zeros_like(dw_ref)
            acc[...] = jnp.zeros_like(acc)
        wv = w_ref[...]
        sq = jnp.zeros((8, d), F32)
        dw = jnp.zeros((8, d), F32)
        for rows in _strips(tm):
            h = h1_ref[rows, :] + dd_ref[rows, :].astype(F32)
            r = lax.rsqrt(jnp.mean(h * h, axis=-1, keepdims=True) + EPS)
            n = h * r
            e = n * wv - tgt_ref[rows, :]
            sq = sq + _fold8(e * e)
            dout = e * (1.0 / d)
            dn = dout * wv
            dw = dw + _fold8(dout * n)
            dh = r * (dn - n * jnp.mean(dn * n, axis=-1, keepdims=True))
            dh_ref[rows, :] = dh
            dhb_ref[rows, :] = dh.astype(BF16)
        acc[...] += _colsum(sq)
        dw_ref[0:1, :] += _colsum(dw)

        @pl.when(i == nt - 1)
        def _():
            loss_ref[...] = jnp.sum(acc[...], axis=-1, keepdims=True) * (0.5 / d)

    row = pl.BlockSpec((tm, d), lambda i: (i, 0))
    return _pc(body, name=name, grid=(nt,),
               in_specs=[row, row, row, pl.BlockSpec((1, d), lambda i: (0, 0))],
               out_specs=[pl.BlockSpec((1, 1), lambda i: (0, 0)), row, row, pl.BlockSpec((8, d), lambda i: (0, 0))],
               out_shape=[_sds((1, 1), F32), _sds((t, d), F32), _sds((t, d), BF16), _sds((8, d), F32)],
               scratch_shapes=[pltpu.VMEM((1, d), F32)], compiler_params=_params())(h1, dd, tgt, w)


def _tile_specs(t, tm, tc, col0):
    th = tm // HALO
    last = t // HALO - 1
    cur = pl.BlockSpec((tm, tc), lambda j, i: (i, col0 + j))
    prev = pl.BlockSpec((HALO, tc), lambda j, i: (jnp.maximum(i * th - 1, 0), col0 + j))
    nxt = pl.BlockSpec((HALO, tc), lambda j, i: (jnp.minimum((i + 1) * th, last), col0 + j))
    return cur, prev, nxt


def _conv_strip(buf, w, k, rows):
    out = None
    for j in range(k):
        term = w[j:j + 1, :] * buf[pl.ds(HALO - (k - 1) + j + rows.start, STRIP), :]
        out = term if out is None else out + term
    return out


def _conv_backward(dbuf, x_strip, emit, w, acc_ref, k, tm, with_bias):
    tc = dbuf.shape[1]
    accs = [jnp.zeros((8, tc), F32) for _ in range(k + int(with_bias))]
    for rows in _strips(tm):
        xs = x_strip(rows)
        dx = None
        for j in range(k):
            ds = dbuf[pl.ds(rows.start + k - 1 - j, STRIP), :]
            term = w[j:j + 1, :] * ds
            dx = term if dx is None else dx + term
            accs[j] = accs[j] + _fold8(ds * xs)
            if with_bias and j == k - 1:
                accs[k] = accs[k] + _fold8(ds)
        emit(rows, dx)
    for j, a in enumerate(accs):
        acc_ref[j:j + 1, :] += _colsum(a)


def _conv_a_fwd(pa, w, d, name):
    t = pa.shape[0]
    tm, tc = min(1024, t), _pick(d, (512, 256, 128))
    nd = d // tc

    def body(b_ref, c_ref, v_ref, cp_ref, vp_ref, w_ref, o_ref, q_ref, buf):
        keep = (pl.program_id(1) > 0).astype(F32)
        buf[0:HALO, :] = cp_ref[...].astype(F32) * vp_ref[...].astype(F32) * keep
        for rows in _strips(tm):
            buf[HALO + rows.start:HALO + rows.stop, :] = c_ref[rows, :].astype(F32) * v_ref[rows, :].astype(F32)
        wv = w_ref[...]
        for rows in _strips(tm):
            q = _conv_strip(buf, wv, 3, rows)
            q_ref[rows, :] = q.astype(q_ref.dtype)
            o_ref[rows, :] = (b_ref[rows, :].astype(F32) * q).astype(o_ref.dtype)

    b_cur, _, _ = _tile_specs(t, tm, tc, 0)
    c_cur, c_prev, _ = _tile_specs(t, tm, tc, nd)
    v_cur, v_prev, _ = _tile_specs(t, tm, tc, 2 * nd)
    return _pc(body, name=name, grid=(nd, t // tm),
               in_specs=[b_cur, c_cur, v_cur, c_prev, v_prev, pl.BlockSpec((3, tc), lambda j, i: (0, j))],
               out_specs=[pl.BlockSpec((tm, tc), lambda j, i: (i, j))] * 2,
               out_shape=[_sds((t, d), BF16)] * 2,
               scratch_shapes=[pltpu.VMEM((tm + HALO, tc), F32)],
               compiler_params=_params())(pa, pa, pa, pa, pa, w)


def _conv_a_bwd(pa, q, dya, w, d, name):
    t = pa.shape[0]
    tm, tc = min(1024, t), _pick(d, (512, 256, 128))
    nd, nt = d // tc, t // tm

    def body(b_ref, c_ref, v_ref, bn_ref, q_ref, g_ref, gn_ref, w_ref, db_ref, dc_ref, dv_ref, acc_ref, dbuf):
        i = pl.program_id(1)

        @pl.when(i == 0)
        def _():
            acc_ref[...] = jnp.zeros_like(acc_ref)
        for rows in _strips(tm):
            g = g_ref[rows, :].astype(F32)
            dbuf[rows, :] = g * b_ref[rows, :].astype(F32)
            db_ref[rows, :] = (g * q_ref[rows, :].astype(F32)).astype(BF16)
        dbuf[tm:tm + HALO, :] = gn_ref[...].astype(F32) * bn_ref[...].astype(F32) * (i < nt - 1).astype(F32)

        def emit(rows, dp):
            dc_ref[rows, :] = (dp * v_ref[rows, :].astype(F32)).astype(BF16)
            dv_ref[rows, :] = (dp * c_ref[rows, :].astype(F32)).astype(BF16)

        _conv_backward(dbuf, lambda rows: c_ref[rows, :].astype(F32) * v_ref[rows, :].astype(F32), emit,
                       w_ref[...], acc_ref, 3, tm, False)

    b_cur, _, b_next = _tile_specs(t, tm, tc, 0)
    c_cur, _, _ = _tile_specs(t, tm, tc, nd)
    v_cur, _, _ = _tile_specs(t, tm, tc, 2 * nd)
    g_cur, _, g_next = _tile_specs(t, tm, tc, 0)
    out = pl.BlockSpec((tm, tc), lambda j, i: (i, j))
    return _pc(body, name=name, grid=(nd, nt),
               in_specs=[b_cur, c_cur, v_cur, b_next, g_cur, g_cur, g_next,
                         pl.BlockSpec((3, tc), lambda j, i: (0, j))],
               out_specs=[out, out, out, pl.BlockSpec((8, tc), lambda j, i: (0, j))],
               out_shape=[_sds((t, d), BF16)] * 3 + [_sds((8, d), F32)],
               scratch_shapes=[pltpu.VMEM((tm + HALO, tc), F32)],
               compiler_params=_params())(pa, pa, pa, pa, q, dya, dya, w)


def _conv_s_fwd(xbc, w, b, name):
    t, dx = xbc.shape
    tm, tc = min(1024, t), _pick(dx, (512, 256, 128))

    def body(x_ref, xp_ref, w_ref, b_ref, o_ref, pre_ref, buf):
        buf[0:HALO, :] = xp_ref[...].astype(F32) * (pl.program_id(1) > 0).astype(F32)
        for rows in _strips(tm):
            buf[HALO + rows.start:HALO + rows.stop, :] = x_ref[rows, :].astype(F32)
        wv, bv = w_ref[...], b_ref[...]
        for rows in _strips(tm):
            pre = _conv_strip(buf, wv, 4, rows) + bv
            pre_ref[rows, :] = pre.astype(pre_ref.dtype)
            o_ref[rows, :] = (pre * _sigmoid(pre)).astype(o_ref.dtype)

    cur, prev, _ = _tile_specs(t, tm, tc, 0)
    return _pc(body, name=name, grid=(dx // tc, t // tm),
               in_specs=[cur, prev, pl.BlockSpec((4, tc), lambda j, i: (0, j)),
                         pl.BlockSpec((1, tc), lambda j, i: (0, j))],
               out_specs=[pl.BlockSpec((tm, tc), lambda j, i: (i, j))] * 2,
               out_shape=[_sds((t, dx), BF16)] * 2,
               scratch_shapes=[pltpu.VMEM((tm + HALO, tc), F32)],
               compiler_params=_params())(xbc, xbc, w, b)


def _dsilu(pre):
    s = _sigmoid(pre)
    return s * (1.0 + pre * (1.0 - s))


def _conv_s_bwd(xbc, pre, dxc, w, name):
    t, dx = xbc.shape
    tm, tc = min(1024, t), _pick(dx, (512, 256, 128))
    nt = t // tm

    def body(x_ref, p_ref, pn_ref, g_ref, gn_ref, w_ref, dx_ref, acc_ref, dbuf):
        i = pl.program_id(1)

        @pl.when(i == 0)
        def _():
            acc_ref[...] = jnp.zeros_like(acc_ref)
        for rows in _strips(tm):
            dbuf[rows, :] = g_ref[rows, :].astype(F32) * _dsilu(p_ref[rows, :].astype(F32))
        dbuf[tm:tm + HALO, :] = (gn_ref[...].astype(F32) * _dsilu(pn_ref[...].astype(F32))
                                 * (i < nt - 1).astype(F32))

        def emit(rows, d_in):
            dx_ref[rows, :] = d_in.astype(BF16)

        _conv_backward(dbuf, lambda rows: x_ref[rows, :].astype(F32), emit, w_ref[...], acc_ref, 4, tm, True)

    cur, _, nxt = _tile_specs(t, tm, tc, 0)
    return _pc(body, name=name, grid=(dx // tc, nt),
               in_specs=[cur, cur, nxt, cur, nxt, pl.BlockSpec((4, tc), lambda j, i: (0, j))],
               out_specs=[pl.BlockSpec((tm, tc), lambda j, i: (i, j)), pl.BlockSpec((8, tc), lambda j, i: (0, j))],
               out_shape=[_sds((t, dx), BF16), _sds((8, dx), F32)],
               scratch_shapes=[pltpu.VMEM((tm + HALO, tc), F32)],
               compiler_params=_params())(xbc, pre, pre, dxc, dxc, w)


def _ffn_fwd(hv, w, b, f, name):
    t = hv.shape[0]
    tm, tc = min(1024, t), _pick(f, (512, 256, 128))
    nf = f // tc

    def body(h1_ref, h1p_ref, h3_ref, w_ref, b_ref, o_ref, c1_ref, buf):
        buf[0:HALO, :] = h1p_ref[...].astype(F32) * (pl.program_id(1) > 0).astype(F32)
        for rows in _strips(tm):
            buf[HALO + rows.start:HALO + rows.stop, :] = h1_ref[rows, :].astype(F32)
        wv, bv = w_ref[...], b_ref[...]
        for rows in _strips(tm):
            c1 = _conv_strip(buf, wv, 3, rows) + bv
            c1_ref[rows, :] = c1.astype(c1_ref.dtype)
            o_ref[rows, :] = (c1 * _sigmoid(c1) * h3_ref[rows, :].astype(F32)).astype(o_ref.dtype)

    h1_cur, h1_prev, _ = _tile_specs(t, tm, tc, 0)
    h3_cur, _, _ = _tile_specs(t, tm, tc, nf)
    return _pc(body, name=name, grid=(nf, t // tm),
               in_specs=[h1_cur, h1_prev, h3_cur, pl.BlockSpec((3, tc), lambda j, i: (0, j)),
                         pl.BlockSpec((1, tc), lambda j, i: (0, j))],
               out_specs=[pl.BlockSpec((tm, tc), lambda j, i: (i, j))] * 2,
               out_shape=[_sds((t, f), BF16)] * 2,
               scratch_shapes=[pltpu.VMEM((tm + HALO, tc), F32)],
               compiler_params=_params())(hv, hv, hv, w, b)


def _ffn_bwd(hv, c1, dact, w, f, name):
    t = hv.shape[0]
    tm, tc = min(1024, t), _pick(f, (512, 256, 128))
    nf, nt = f // tc, t // tm

    def body(h1_ref, h3_ref, h3n_ref, c_ref, cn_ref, g_ref, gn_ref, w_ref, dh1_ref, dh3_ref, acc_ref, dbuf):
        i = pl.program_id(1)

        @pl.when(i == 0)
        def _():
            acc_ref[...] = jnp.zeros_like(acc_ref)
        for rows in _strips(tm):
            c1v, g = c_ref[rows, :].astype(F32), g_ref[rows, :].astype(F32)
            s1 = _sigmoid(c1v)
            dh3_ref[rows, :] = (g * c1v * s1).astype(BF16)
            dbuf[rows, :] = g * h3_ref[rows, :].astype(F32) * s1 * (1.0 + c1v * (1.0 - s1))
        dbuf[tm:tm + HALO, :] = (gn_ref[...].astype(F32) * h3n_ref[...].astype(F32)
                                 * _dsilu(cn_ref[...].astype(F32)) * (i < nt - 1).astype(F32))

        def emit(rows, d_in):
            dh1_ref[rows, :] = d_in.astype(BF16)

        _conv_backward(dbuf, lambda rows: h1_ref[rows, :].astype(F32), emit, w_ref[...], acc_ref, 3, tm, True)

    h1_cur, _, _ = _tile_specs(t, tm, tc, 0)
    h3_cur, _, h3_next = _tile_specs(t, tm, tc, nf)
    g_cur, _, g_next = _tile_specs(t, tm, tc, 0)
    out = pl.BlockSpec((tm, tc), lambda j, i: (i, j))
    return _pc(body, name=name, grid=(nf, nt),
               in_specs=[h1_cur, h3_cur, h3_next, g_cur, g_next, g_cur, g_next,
                         pl.BlockSpec((3, tc), lambda j, i: (0, j))],
               out_specs=[out, out, pl.BlockSpec((8, tc), lambda j, i: (0, j))],
               out_shape=[_sds((t, f), BF16), _sds((t, f), BF16), _sds((8, f), F32)],
               scratch_shapes=[pltpu.VMEM((tm + HALO, tc), F32)],
               compiler_params=_params())(hv, hv, hv, c1, c1, dact, dact, w)


def _gnorm_fwd(y, z, w, name):
    t, di = y.shape
    gw = di // N_GROUPS
    tm = min(1024, t)

    def body(y_ref, z_ref, w_ref, o_ref):
        wv = w_ref[...]
        for rows in _strips(tm):
            zv = z_ref[rows, :].astype(F32)
            yz = y_ref[rows, :].astype(F32) * zv * _sigmoid(zv)
            r = lax.rsqrt(jnp.mean(yz * yz, axis=-1, keepdims=True) + EPS)
            o_ref[rows, :] = (yz * r * wv).astype(o_ref.dtype)

    blk = pl.BlockSpec((tm, gw), lambda j, i: (i, j))
    return _pc(body, name=name, grid=(N_GROUPS, t // tm),
               in_specs=[blk, blk, pl.BlockSpec((1, gw), lambda j, i: (0, j))],
               out_specs=blk, out_shape=_sds((t, di), BF16), compiler_params=_params())(y, z, w)


def _gnorm_bwd(y, z, dyn, w, name):
    t, di = y.shape
    gw = di // N_GROUPS
    tm = min(1024, t)

    def body(y_ref, z_ref, g_ref, w_ref, dy_ref, dz_ref, dw_ref):
        @pl.when(pl.program_id(1) == 0)
        def _():
            dw_ref[...] = jnp.zeros_like(dw_ref)
        wv = w_ref[...]
        acc = jnp.zeros((8, gw), F32)
        for rows in _strips(tm):
            yv, zv, g = y_ref[rows, :].astype(F32), z_ref[rows, :].astype(F32), g_ref[rows, :].astype(F32)
            s = _sigmoid(zv)
            sz = zv * s
            yz = yv * sz
            r = lax.rsqrt(jnp.mean(yz * yz, axis=-1, keepdims=True) + EPS)
            n = yz * r
            dn = g * wv
            acc = acc + _fold8(g * n)
            dyz = r * (dn - n * jnp.mean(dn * n, axis=-1, keepdims=True))
            dy_ref[rows, :] = (dyz * sz).astype(BF16)
            dz_ref[rows, :] = (dyz * yv * s * (1.0 + zv * (1.0 - s))).astype(BF16)
        dw_ref[0:1, :] += _colsum(acc)

    blk = pl.BlockSpec((tm, gw), lambda j, i: (i, j))
    return _pc(body, name=name, grid=(N_GROUPS, t // tm),
               in_specs=[blk, blk, blk, pl.BlockSpec((1, gw), lambda j, i: (0, j))],
               out_specs=[blk, blk, pl.BlockSpec((8, gw), lambda j, i: (0, j))],
               out_shape=[_sds((t, di), BF16), _sds((t, di), BF16), _sds((8, di), F32)],
               compiler_params=_params())(y, z, dyn, w)


def _merge_fwd(gates, ya, ys, d, name):
    t = ya.shape[0]
    tm, tc = min(1024, t), _pick(d, (512, 256, 128))
    nd = d // tc

    def body(ga_ref, gs_ref, ya_ref, ys_ref, o_ref):
        for rows in _strips(tm):
            o_ref[rows, :] = (_sigmoid(ga_ref[rows, :].astype(F32)) * ya_ref[rows, :].astype(F32)
                              + _sigmoid(gs_ref[rows, :].astype(F32)) * ys_ref[rows, :].astype(F32)
                              ).astype(o_ref.dtype)

    blk = pl.BlockSpec((tm, tc), lambda j, i: (i, j))
    return _pc(body, name=name, grid=(nd, t // tm),
               in_specs=[blk, pl.BlockSpec((tm, tc), lambda j, i: (i, nd + j)), blk, blk],
               out_specs=blk, out_shape=_sds((t, d), BF16), compiler_params=_params())(gates, gates, ya, ys)


def _merge_bwd(dm, gates, ya, ys, d, name):
    t = ya.shape[0]
    tm, tc = min(1024, t), _pick(d, (512, 256, 128))
    nd = d // tc

    def body(dm_ref, ga_ref, gs_ref, ya_ref, ys_ref, dya_ref, dys_ref, dga_ref, dgs_ref):
        for rows in _strips(tm):
            g = dm_ref[rows, :].astype(F32)
            sa, ss = _sigmoid(ga_ref[rows, :].astype(F32)), _sigmoid(gs_ref[rows, :].astype(F32))
            dya_ref[rows, :] = (g * sa).astype(BF16)
            dys_ref[rows, :] = (g * ss).astype(BF16)
            dga_ref[rows, :] = (g * ya_ref[rows, :].astype(F32) * sa * (1.0 - sa)).astype(BF16)
            dgs_ref[rows, :] = (g * ys_ref[rows, :].astype(F32) * ss * (1.0 - ss)).astype(BF16)

    blk = pl.BlockSpec((tm, tc), lambda j, i: (i, j))
    return _pc(body, name=name, grid=(nd, t // tm),
               in_specs=[blk, blk, pl.BlockSpec((tm, tc), lambda j, i: (i, nd + j)), blk, blk],
               out_specs=[blk] * 4, out_shape=[_sds((t, d), BF16)] * 4,
               compiler_params=_params())(dm, gates, gates, ya, ys)


def _ssd_chunk_terms(dtr, dtb, alog):
    xx = dtr + dtb
    dt = jnp.maximum(xx, 0.0) + jnp.log(1.0 + jnp.exp(-jnp.abs(xx)))
    a = -jnp.exp(alog)
    li = lax.broadcasted_iota(jnp.int32, (CHUNK, CHUNK), 0)
    si = lax.broadcasted_iota(jnp.int32, (CHUNK, CHUNK), 1)
    causal = li >= si
    acum = _dot(causal.astype(F32), dt * a, 1, 0, HIGHEST)
    return xx, dt, a, acum, acum.T, causal


def _split2(x):
    hi = x.astype(BF16)
    return hi, (x - hi.astype(F32)).astype(BF16)


def _expand(v, e, exact=True):
    hi, lo = _split2(v)
    out = _dot(hi, e, 1, 0)
    return out + _dot(lo, e, 1, 0) if exact else out


def _segsum(s, e):
    hi, lo = _split2(s)
    return _dot(hi, e, 1, 1) + _dot(lo, e, 1, 1)


def _head_maps(di):
    nh = di // HEAD_DIM
    h = jnp.arange(DT_LANES)[:, None]
    e64 = (jnp.arange(di)[None, :] // HEAD_DIM == h).astype(BF16)
    e128 = (jnp.arange(nh * CHUNK)[None, :] // CHUNK == h).astype(BF16)
    return e64, e128


def _pair_blockdiag(p, left):
    zero = jnp.zeros_like(p)
    return jnp.concatenate([jnp.where(left, p, zero), jnp.where(left, zero, p)], axis=0)


def _ssd_fwd(xc, dtr, dtb, alog, dskx, di, name):
    t = xc.shape[0]
    dx = xc.shape[1]
    nc = t // CHUNK
    nh = di // HEAD_DIM
    hpg = nh // N_GROUPS
    gw = hpg * HEAD_DIM
    boff, coff = di, di + N_GROUPS * D_STATE
    e64, e128 = _head_maps(di)

    def body(xc_ref, dtr_ref, dtb_ref, alog_ref, dsk_ref, e64_ref, e128_ref, y_ref, st_ref, state):
        @pl.when(pl.program_id(0) == 0)
        def _():
            state[...] = jnp.zeros_like(state)
        _, dt, _, acum, acum_t, causal = _ssd_chunk_terms(dtr_ref[...], dtb_ref[...], alog_ref[...])
        last = acum[CHUNK - 1:CHUNK, :]
        e64v = e64_ref[...]
        dtx = _expand(dt, e64v, False)
        eax = _expand(jnp.exp(acum), e64v)
        dex = _expand(dt * jnp.exp(last - acum), e64v, False)
        acx = _expand(acum, e128_ref[...])
        st_ref[0] = state[...]
        left = lax.broadcasted_iota(jnp.int32, (CHUNK, 2 * HEAD_DIM), 1) < HEAD_DIM
        for g in range(N_GROUPS):
            gs = slice(g * gw, (g + 1) * gw)
            bg = xc_ref[:, boff + g * D_STATE:boff + (g + 1) * D_STATE]
            cg = xc_ref[:, coff + g * D_STATE:coff + (g + 1) * D_STATE]
            gm = _dot(cg, bg, 1, 1)
            xg = xc_ref[:, gs].astype(F32)
            xdb = (xg * dtx[:, gs]).astype(BF16)
            sin = state[:, gs]
            yo = _dot(cg, sin.astype(BF16), 1, 0) * eax[:, gs]
            for jp in range(hpg // 2):
                h0 = g * hpg + 2 * jp
                ps = slice(jp * 2 * HEAD_DIM, (jp + 1) * 2 * HEAD_DIM)
                ms = []
                for hh in (h0, h0 + 1):
                    seg = acx[:, hh * CHUNK:(hh + 1) * CHUNK] - acum_t[hh:hh + 1, :]
                    ms.append((gm * jnp.exp(jnp.where(causal, seg, -1e30))).astype(BF16))
                yd = _dot(jnp.concatenate(ms, axis=1), _pair_blockdiag(xdb[:, ps], left), 1, 0)
                col = slice(g * gw + jp * 2 * HEAD_DIM, g * gw + (jp + 1) * 2 * HEAD_DIM)
                y_ref[:, col] = (yd + yo[:, ps] + dsk_ref[:, col] * xg[:, ps]).astype(y_ref.dtype)
            xe = (xg * dex[:, gs]).astype(BF16)
            state[:, gs] = eax[CHUNK - 1:CHUNK, gs] * sin + _dot(bg, xe, 0, 0)

    small = pl.BlockSpec((1, DT_LANES), lambda c: (0, 0))
    whole = lambda a: pl.BlockSpec(a.shape, lambda c: (0, 0))
    return _pc(body, name=name, grid=(nc,),
               in_specs=[pl.BlockSpec((CHUNK, dx), lambda c: (c, 0)),
                         pl.BlockSpec((CHUNK, DT_LANES), lambda c: (c, 0)), small, small,
                         whole(dskx), whole(e64), whole(e128)],
               out_specs=[pl.BlockSpec((CHUNK, di), lambda c: (c, 0)),
                          pl.BlockSpec((1, D_STATE, di), lambda c: (c, 0, 0))],
               out_shape=[_sds((t, di), BF16), _sds((nc, D_STATE, di), F32)],
               scratch_shapes=[pltpu.VMEM((D_STATE, di), F32)],
               compiler_params=_params())(xc, dtr, dtb, alog, dskx, e64, e128)


def _ssd_bwd(xc, dtr, dy, states, dtb, alog, dskx, di, name):
    t = xc.shape[0]
    dx = xc.shape[1]
    nc = t // CHUNK
    nh = di // HEAD_DIM
    hpg = nh // N_GROUPS
    gw = hpg * HEAD_DIM
    boff, coff = di, di + N_GROUPS * D_STATE
    e64, e128 = _head_maps(di)

    def body(xc_ref, dtr_ref, dy_ref, st_ref, dtb_ref, alog_ref, dsk_ref, e64_ref, e128_ref,
             dxc_ref, ddtr_ref, sm_ref, dstate, darow):
        @pl.when(pl.program_id(0) == 0)
        def _():
            dstate[...] = jnp.zeros_like(dstate)
            sm_ref[...] = jnp.zeros_like(sm_ref)
        darow[...] = jnp.zeros_like(darow)
        xx, dt, a, acum, acum_t, causal = _ssd_chunk_terms(dtr_ref[...], dtb_ref[...], alog_ref[...])
        last = acum[CHUNK - 1:CHUNK, :]
        e64v = e64_ref[...]
        dtx = _expand(dt, e64v, False)
        eax = _expand(jnp.exp(acum), e64v)
        eex = _expand(jnp.exp(last - acum), e64v, False)
        acx = _expand(acum, e128_ref[...])
        left = lax.broadcasted_iota(jnp.int32, (CHUNK, 2 * HEAD_DIM), 1) < HEAD_DIM
        lane = lax.broadcasted_iota(jnp.int32, (CHUNK, DT_LANES), 1)
        sub8 = lax.broadcasted_iota(jnp.int32, (8, gw), 0)
        da_col = jnp.zeros((CHUNK, DT_LANES), F32)
        ddt_col = jnp.zeros((CHUNK, DT_LANES), F32)
        rows = jnp.zeros((8, DT_LANES), F32)
        for g in range(N_GROUPS):
            gs = slice(g * gw, (g + 1) * gw)
            bg = xc_ref[:, boff + g * D_STATE:boff + (g + 1) * D_STATE]
            cg = xc_ref[:, coff + g * D_STATE:coff + (g + 1) * D_STATE]
            gm = _dot(cg, bg, 1, 1)
            e64g = e64v[:, gs]
            xg = xc_ref[:, gs].astype(F32)
            dtg, eag, eeg = dtx[:, gs], eax[:, gs], eex[:, gs]
            xd = xg * dtg
            xdb = xd.astype(BF16)
            dyb = dy_ref[:, gs]
            dyf = dyb.astype(F32)
            sin = st_ref[0, :, gs]
            sinb = sin.astype(BF16)
            ds = dstate[:, gs]
            dsb = ds.astype(BF16)
            bds = _dot(bg, dsb, 1, 0)
            dyeb = (dyf * eag).astype(BF16)
            dcg = _dot(dyeb, sinb, 1, 1)
            dstate[:, gs] = eag[CHUNK - 1:CHUNK, :] * ds + _dot(cg, dyeb, 0, 0)
            yo = _dot(cg, sinb, 1, 0) * eag
            xe = xd * eeg
            dbg = _dot(xe.astype(BF16), dsb, 1, 1)
            wterm = bds * xe
            da_col = da_col + _segsum(dyf * yo - wterm, e64g)
            dg = jnp.zeros((CHUNK, CHUNK), F32)
            dxd_parts = []
            for jp in range(hpg // 2):
                h0 = g * hpg + 2 * jp
                ps = slice(jp * 2 * HEAD_DIM, (jp + 1) * 2 * HEAD_DIM)
                lms, mfs = [], []
                for hh in (h0, h0 + 1):
                    seg = acx[:, hh * CHUNK:(hh + 1) * CHUNK] - acum_t[hh:hh + 1, :]
                    lm = jnp.exp(jnp.where(causal, seg, -1e30))
                    lms.append(lm)
                    mfs.append(gm * lm)
                mstack = jnp.concatenate([m.astype(BF16) for m in mfs], axis=0)
                dyp = dyb[:, ps]
                dxd_parts.append(_dot(mstack, _pair_blockdiag(dyp, left), 0, 0))
                dm2 = _dot(dyp, _pair_blockdiag(xdb[:, ps], left), 1, 1)
                for k, hh in enumerate((h0, h0 + 1)):
                    dm = dm2[:, k * CHUNK:(k + 1) * CHUNK]
                    dg = dg + dm * lms[k]
                    q = dm * mfs[k]
                    da_col = da_col + jnp.where(lane == hh, jnp.sum(q, axis=1, keepdims=True), 0.0)
                    darow[hh:hh + 1, :] = -jnp.sum(q, axis=0, keepdims=True)
            dxd = jnp.concatenate(dxd_parts, axis=1) + bds * eeg
            ddt_col = ddt_col + _segsum(dxd * xg, e64g)
            rsum = (jnp.where(sub8 == 0, jnp.sum(wterm, axis=0, keepdims=True), 0.0)
                    + jnp.where(sub8 == 1, jnp.sum(ds * sin, axis=0, keepdims=True), 0.0)
                    + jnp.where(sub8 == 2, jnp.sum(dyf * xg, axis=0, keepdims=True), 0.0))
            rows = rows + _segsum(rsum, e64g)
            dxc_ref[:, gs] = (dxd * dtg + dsk_ref[:, gs] * dyf).astype(dxc_ref.dtype)
            dgb = dg.astype(BF16)
            dxc_ref[:, boff + g * D_STATE:boff + (g + 1) * D_STATE] = (
                dbg + _dot(dgb, cg, 0, 0)).astype(dxc_ref.dtype)
            dxc_ref[:, coff + g * D_STATE:coff + (g + 1) * D_STATE] = (
                dcg + _dot(dgb, bg, 1, 0)).astype(dxc_ref.dtype)
        at_last = rows[0:1, :] + jnp.exp(last) * rows[1:2, :]
        is_last = lax.broadcasted_iota(jnp.int32, (CHUNK, DT_LANES), 0) == CHUNK - 1
        da = da_col + jnp.where(is_last, at_last, 0.0) + darow[...].T
        li = lax.broadcasted_iota(jnp.int32, (CHUNK, CHUNK), 0)
        si = lax.broadcasted_iota(jnp.int32, (CHUNK, CHUNK), 1)
        dla = _dot((si >= li).astype(F32), da, 1, 0, HIGHEST)
        ddtr = (ddt_col + dla * a) * _sigmoid(xx)
        ddtr_ref[...] = ddtr
        sm_ref[0:1, :] += jnp.sum(ddtr, axis=0, keepdims=True)
        sm_ref[1:2, :] += jnp.sum(dla * dt, axis=0, keepdims=True) * a
        sm_ref[2:3, :] += rows[2:3, :]

    small = pl.BlockSpec((1, DT_LANES), lambda c: (0, 0))
    whole = lambda a: pl.BlockSpec(a.shape, lambda c: (0, 0))
    rev = lambda c: (nc - 1 - c, 0)
    return _pc(body, name=name, grid=(nc,),
               in_specs=[pl.BlockSpec((CHUNK, dx), rev), pl.BlockSpec((CHUNK, DT_LANES), rev),
                         pl.BlockSpec((CHUNK, di), rev),
                         pl.BlockSpec((1, D_STATE, di), lambda c: (nc - 1 - c, 0, 0)), small, small,
                         whole(dskx), whole(e64), whole(e128)],
               out_specs=[pl.BlockSpec((CHUNK, dx), rev), pl.BlockSpec((CHUNK, DT_LANES), rev),
                          pl.BlockSpec((8, DT_LANES), lambda c: (0, 0))],
               out_shape=[_sds((t, dx), BF16), _sds((t, DT_LANES), F32), _sds((8, DT_LANES), F32)],
               scratch_shapes=[pltpu.VMEM((D_STATE, di), F32), pltpu.VMEM((DT_LANES, CHUNK), F32)],
               compiler_params=_params())(xc, dtr, dy, states, dtb, alog, dskx, e64, e128)


def _adamw(parts, w, m, v, name, tile_rows=None):
    npart, rows, width = parts.shape
    if tile_rows is not None:
        tr, tw = tile_rows, width
    elif rows * width <= SMALL_PARAM:
        tr, tw = rows, width
    else:
        tr, tw = (_pick(rows, (64, 32, 16, 8)), width) if rows % 8 == 0 else (rows, 128)
    c1 = 1.0 - ADAM_B1 ** ADAM_STEP
    c2 = 1.0 - ADAM_B2 ** ADAM_STEP

    row_strips = _strips(tr) if tr % STRIP == 0 else [slice(0, tr)]
    col_chunks = [slice(c, c + 512) for c in range(0, tw, 512)] if tw % 512 == 0 else [slice(0, tw)]

    def body(p_ref, w_ref, m_ref, v_ref, g_ref, d_ref, nm_ref, nv_ref):
        for rows in row_strips:
            for cols in col_chunks:
                g = p_ref[0, rows, cols].astype(F32)
                for p in range(1, npart):
                    g = g + p_ref[p, rows, cols].astype(F32)
                nm = ADAM_B1 * m_ref[rows, cols] + (1.0 - ADAM_B1) * g
                nv = ADAM_B2 * v_ref[rows, cols] + (1.0 - ADAM_B2) * (g * g)
                g_ref[rows, cols] = g
                nm_ref[rows, cols] = nm
                nv_ref[rows, cols] = nv
                d_ref[rows, cols] = -ADAM_LR * ((nm / c1) / (jnp.sqrt(nv / c2) + ADAM_EPS)
                                                + ADAM_WD * w_ref[rows, cols])

    blk = pl.BlockSpec((tr, tw), lambda i, j: (i, j))
    return _pc(body, name=name, grid=(rows // tr, width // tw),
               in_specs=[pl.BlockSpec((npart, tr, tw), lambda i, j: (0, i, j)), blk, blk, blk],
               out_specs=[blk] * 4, out_shape=[_sds((rows, width), F32)] * 4,
               compiler_params=_params())(parts, w, m, v)


def _sum_parts(parts, name, tile=None):
    npart, rows, width = parts.shape
    tile = rows if tile is None else tile
    tw = width if tile < rows or rows * width <= SMALL_PARAM * N_DEV else 128
    strip = 8 if parts.dtype == F32 else STRIP
    row_strips = _strips(tile, strip) if tile % strip == 0 else [slice(0, tile)]

    def body(p_ref, o_ref):
        for rows_ in row_strips:
            g = p_ref[0, rows_, :].astype(F32)
            for p in range(1, npart):
                g = g + p_ref[p, rows_, :].astype(F32)
            o_ref[rows_, :] = g

    return _pc(body, name=name, grid=(rows // tile, width // tw),
               in_specs=[pl.BlockSpec((npart, tile, tw), lambda i, j: (0, i, j))],
               out_specs=pl.BlockSpec((tile, tw), lambda i, j: (i, j)),
               out_shape=_sds((rows, width), F32), compiler_params=_params())(parts)


def _flip(k):
    x, y, c = lax.axis_index("x"), lax.axis_index("y"), lax.axis_index("c")
    px = 1 - x if k & 4 else x
    py = 1 - y if k & 2 else y
    pc = 1 - c if k & 1 else c
    return (px, py, pc), 4 * px + 2 * py + pc


DIRECT = tuple((k, 0) for k in range(1, N_DEV))
TO_CHIPS = ((1, 0), (2, 0), (4, 0), (6, 0))
TO_SIBLING = ((1, 2), (1, 4), (1, 6))


def _copies(arrays, lands, send_sems, recv_sems, scatter, moves):
    _, me = _flip(0)
    outgoing, incoming = [], []
    for i, (kd, kb) in enumerate(moves):
        peer, pidx = _flip(kd)
        _, out_slot = _flip(kb)
        _, in_slot = _flip(kd ^ kb)
        for j, land_ref in enumerate(lands):
            if kb:
                src = land_ref.at[out_slot]
            else:
                src = arrays[j].at[pidx] if scatter[j] else arrays[j]
            sem = len(lands) * i + j
            for dst, bucket in ((land_ref.at[out_slot], outgoing), (land_ref.at[in_slot], incoming)):
                bucket.append(pltpu.make_async_remote_copy(
                    src_ref=src, dst_ref=dst, send_sem=send_sems.at[sem], recv_sem=recv_sems.at[sem],
                    device_id=peer, device_id_type=MESH))
    return outgoing, incoming


HBM_SPEC = pl.BlockSpec(memory_space=pltpu.HBM)
SEM_SPEC = pl.BlockSpec(memory_space=pltpu.SEMAPHORE)
ANY_SPEC = pl.BlockSpec(memory_space=pl.ANY)
EFFECT = pltpu.SideEffectType.DATAFLOW_SIDE_EFFECTING


def _landing_zones(arrays, scatter):
    _, me = _flip(0)
    lands = []
    for a, sc in zip(arrays, scatter):
        own = lax.dynamic_index_in_dim(a, me, 0, keepdims=True) if sc else a[None]
        shape = a.shape if sc else (N_DEV,) + a.shape
        lands.append(lax.dynamic_update_slice(lax.empty(shape, a.dtype), own, (me,) + (0,) * (len(shape) - 1)))
    return lands


def _xchg_start(arrays, scatter, after, name, moves=DIRECT, lands=None):
    if lands is None:
        lands = _landing_zones(arrays, scatter)
    na, nl = len(arrays), len(lands)

    def body(*refs):
        ins, outs = refs[:na + nl], refs[na + nl + 1:]
        outgoing, _ = _copies(ins[:na], ins[na:], outs[0], outs[1], scatter, moves)
        for cp in outgoing:
            cp.start()
        outs[-1][...] = jnp.zeros_like(outs[-1])

    nsem = nl * len(moves)
    operands = [pltpu.with_memory_space_constraint(a, pltpu.HBM) for a in list(arrays) + list(lands)]
    out = _pc(body, name=name,
              out_shape=(pltpu.SemaphoreType.DMA((nsem,)), pltpu.SemaphoreType.DMA((nsem,)),
                         *[pltpu.HBM(a.shape, a.dtype) for a in operands], _sds((8, 128), F32)),
              in_specs=[HBM_SPEC] * (na + nl) + [ANY_SPEC],
              out_specs=(SEM_SPEC, SEM_SPEC, *[HBM_SPEC] * (na + nl), pl.BlockSpec(memory_space=pltpu.VMEM)),
              input_output_aliases={i: 2 + i for i in range(na + nl)},
              compiler_params=pltpu.CompilerParams(has_side_effects=EFFECT))(*operands, after)
    return dict(sems=out[:2], thru=out[2:2 + na + nl], token=out[-1], scatter=scatter, na=na, moves=moves)


def _xchg_wait(handle, after, name):
    na, scatter, moves, thru = handle["na"], handle["scatter"], handle["moves"], handle["thru"]
    n = len(thru)

    def body(*refs):
        ins = refs[:n]
        outgoing, incoming = _copies(ins[:na], ins[na:], refs[n], refs[n + 1], scatter, moves)
        for cp in outgoing:
            cp.wait_send()
        for cp in incoming:
            cp.wait_recv()

    out = _pc(body, name=name, out_shape=tuple(pltpu.HBM(a.shape, a.dtype) for a in thru),
              in_specs=[HBM_SPEC] * n + [SEM_SPEC, SEM_SPEC, ANY_SPEC], out_specs=tuple([HBM_SPEC] * n),
              input_output_aliases={i: i for i in range(n)},
              compiler_params=pltpu.CompilerParams(has_side_effects=EFFECT))(*thru, *handle["sems"], after)
    return out[na:]


def _pack(arrs, width, row_mult):
    flat = jnp.concatenate([a.reshape(-1) for a in arrs])
    n = flat.shape[0]
    rows = -(-n // (width * row_mult)) * row_mult
    return jnp.pad(flat, (0, rows * width - n)).reshape(rows, width)


def _unpack(packed, shapes, lead=None):
    out, off = [], 0
    flat = packed.reshape(-1) if lead is None else packed.reshape(lead, -1)
    for s in shapes:
        n = math.prod(s)
        if lead is None:
            out.append(flat[off:off + n].reshape(s))
        else:
            out.append(flat[:, off:off + n].reshape((lead,) + tuple(s)))
        off += n
    return out


def _blocks_to_cols(blocks):
    nb, rows, n = blocks.shape
    return blocks.transpose(1, 0, 2).reshape(rows, nb * n)


def _pad_rows(a, rows):
    return jnp.pad(a, ((0, rows - a.shape[0]), (0, 0)))


def _pad_lanes(a, lanes):
    return jnp.pad(a, ((0, 0), (0, lanes - a.shape[1])))


REST = ("w_a_out", "w_s_out", "w_o", "w_up", "w_down")
TRANSPOSED = ("w_up", "w_in")
CONVS = ("conv_a_w", "ssd_conv_w", "ffn_conv_w")
REPL = ("norm_mix_w", "ssd_conv_b", "dt_bias", "a_log", "d_skip", "ssd_norm_w", "norm_ffn_w", "ffn_conv_b",
        "final_norm_w")
ORDER = ("norm_mix_w", "w_in", "conv_a_w", "w_a_out", "ssd_conv_w", "ssd_conv_b", "dt_bias", "a_log", "d_skip",
         "ssd_norm_w", "w_s_out", "w_o", "norm_ffn_w", "w_up", "ffn_conv_w", "ffn_conv_b", "w_down", "final_norm_w")


def _as_rows(name, block):
    return block[0].T if name in TRANSPOSED else block[0]


def kernel(x, norm_mix_w, w_in, conv_a_w, w_a_out, ssd_conv_w, ssd_conv_b, dt_bias, a_log, d_skip, ssd_norm_w, w_s_out, w_o, norm_ffn_w, w_up, ffn_conv_w, ffn_conv_b, w_down, final_norm_w, loss_target, m_norm_mix_w, m_w_in, m_conv_a_w, m_w_a_out, m_ssd_conv_w, m_ssd_conv_b, m_dt_bias, m_a_log, m_d_skip, m_ssd_norm_w, m_w_s_out, m_w_o, m_norm_ffn_w, m_w_up, m_ffn_conv_w, m_ffn_conv_b, m_w_down, m_final_norm_w, v_norm_mix_w, v_w_in, v_conv_a_w, v_w_a_out, v_ssd_conv_w, v_ssd_conv_b, v_dt_bias, v_a_log, v_d_skip, v_ssd_norm_w, v_w_s_out, v_w_o, v_norm_ffn_w, v_w_up, v_ffn_conv_w, v_ffn_conv_b, v_w_down, v_final_norm_w):
    wts = dict(norm_mix_w=norm_mix_w, w_in=w_in, conv_a_w=conv_a_w, w_a_out=w_a_out, ssd_conv_w=ssd_conv_w,
               ssd_conv_b=ssd_conv_b, dt_bias=dt_bias, a_log=a_log, d_skip=d_skip, ssd_norm_w=ssd_norm_w,
               w_s_out=w_s_out, w_o=w_o, norm_ffn_w=norm_ffn_w, w_up=w_up, ffn_conv_w=ffn_conv_w,
               ffn_conv_b=ffn_conv_b, w_down=w_down, final_norm_w=final_norm_w)
    mom1 = dict(norm_mix_w=m_norm_mix_w, w_in=m_w_in, conv_a_w=m_conv_a_w, w_a_out=m_w_a_out,
                ssd_conv_w=m_ssd_conv_w, ssd_conv_b=m_ssd_conv_b, dt_bias=m_dt_bias, a_log=m_a_log, d_skip=m_d_skip,
                ssd_norm_w=m_ssd_norm_w, w_s_out=m_w_s_out, w_o=m_w_o, norm_ffn_w=m_norm_ffn_w, w_up=m_w_up,
                ffn_conv_w=m_ffn_conv_w, ffn_conv_b=m_ffn_conv_b, w_down=m_w_down, final_norm_w=m_final_norm_w)
    mom2 = dict(norm_mix_w=v_norm_mix_w, w_in=v_w_in, conv_a_w=v_conv_a_w, w_a_out=v_w_a_out,
                ssd_conv_w=v_ssd_conv_w, ssd_conv_b=v_ssd_conv_b, dt_bias=v_dt_bias, a_log=v_a_log, d_skip=v_d_skip,
                ssd_norm_w=v_ssd_norm_w, w_s_out=v_w_s_out, w_o=v_w_o, norm_ffn_w=v_norm_ffn_w, w_up=v_w_up,
                ffn_conv_w=v_ffn_conv_w, ffn_conv_b=v_ffn_conv_b, w_down=v_w_down, final_norm_w=v_final_norm_w)

    t, d = x.shape[1], x.shape[2]
    di = 2 * d
    nh = di // HEAD_DIM
    dxw = di + 2 * N_GROUPS * D_STATE
    f = w_down.shape[1] * N_DEV
    n_in = w_in.shape[2] * N_DEV
    me = 4 * lax.axis_index("x") + 2 * lax.axis_index("y") + lax.axis_index("c")

    rest_local = [_as_rows(k, wts[k]).astype(BF16) for k in REST]
    nrows = [a.shape[0] for a in rest_local]
    n_blk = w_in.shape[2]
    in_local = w_in[0].T.astype(BF16)
    conv_shapes = [wts[k].shape[1:] for k in CONVS]
    conv_local = _pack([wts[k] for k in CONVS], d, 8)
    x2, tgt = x[0], loss_target[0]
    h_in = _xchg_start([in_local, conv_local], [False, False], x2, "gather_in_start", moves=TO_CHIPS)
    u = _rms_fwd(x2, norm_mix_w, h_in["token"], "norm_mix")
    part = _xchg_wait(h_in, u, "gather_in_wait")
    h_fwd = _xchg_start([], [False, False], u, "gather_in_forward_start", moves=TO_SIBLING, lands=part)
    in_all, conv_all = _xchg_wait(h_fwd, u, "gather_in_forward_wait")
    win_t = in_all.reshape(n_in, d)
    h_rest = _xchg_start(rest_local, [False] * len(REST), in_all, "gather_rest_start")
    c_a, c_s, c_f = _unpack(conv_all, conv_shapes, N_DEV)
    caw, scw, fcw = _blocks_to_cols(c_a), _blocks_to_cols(c_s), _blocks_to_cols(c_f)

    o_z, o_x, o_dt = 5 * d, 7 * d, 7 * d + dxw
    seg_bounds = [0, d, 2 * d, 3 * d, 4 * d, o_z, o_x, o_dt]
    w_dt = _pad_rows(win_t[o_dt:], DT_LANES)
    dtb, alog = (_pad_lanes(p[...].reshape(1, nh), DT_LANES) for p in (dt_bias, a_log))
    dskx = jnp.repeat(d_skip.reshape(1, nh), HEAD_DIM, axis=1)

    tok = h_rest["token"]
    gates = _mm([(u, 0, d, win_t, 0)], "nt", BF16, "proj_gates", n=2 * d, after=tok)
    pa = _mm([(u, 0, d, win_t, 2 * d)], "nt", BF16, "proj_a", n=3 * d, after=tok)
    z = _mm([(u, 0, d, win_t, o_z)], "nt", BF16, "proj_z", n=2 * d, after=tok)
    xbc = _mm([(u, 0, d, win_t, o_x)], "nt", BF16, "proj_xbc", n=dxw, after=tok)
    dtr = _mm([(u, w_dt)], "nt", F32, "proj_dt", after=tok)
    ya_in, q_a = _conv_a_fwd(pa, caw, d, "conv_a")
    xc, pre_s = _conv_s_fwd(xbc, scw, ssd_conv_b, "conv_s")
    y, states = _ssd_fwd(xc, dtr, dtb, alog, dskx, di, "ssd")
    yn = _gnorm_fwd(y, z, ssd_norm_w, "gnorm")
    rest_all = _xchg_wait(h_rest, yn, "gather_rest_wait")
    waout, wsout, wo, wup_t, wdown = (a.reshape(N_DEV * n, d) for a, n in zip(rest_all, nrows))
    y_a = _mm([(ya_in, waout)], "nn", BF16, "a_out")
    y_s = _mm([(yn, wsout)], "nn", BF16, "s_out")
    merged = _merge_fwd(gates, y_a, y_s, d, "merge")
    mo = _mm([(merged, wo)], "nn", BF16, "o_proj")
    h1, v = _resnorm_fwd(x2, mo, norm_ffn_w, "norm_ffn")
    hv = _mm([(v, wup_t)], "nt", BF16, "up_proj", tm=512, resident_b=True)
    act, c1 = _ffn_fwd(hv, fcw, ffn_conv_b, f, "ffn_act")
    dd = _mm([(act, wdown)], "nn", BF16, "down_proj")
    loss11, dh2, dh2b, g_fnw = _final(h1, dd, tgt, final_norm_w.reshape(1, d), "final")

    dact = _mm([(dh2b, wdown)], "nt", BF16, "d_act", resident_b=True)
    gw_down = _mm_tn(act, dh2b, "gw_down")
    dh1f, dh3, g_ffn = _ffn_bwd(hv, c1, dact, fcw, f, "ffn_act_bwd")
    dv = _mm([(dh1f, 0, f, wup_t, 0), (dh3, 0, f, wup_t, f)], "nn", BF16, "d_v")
    gw_up_t = jnp.concatenate([_mm_tn(dh1f, v, "gw_up1"), _mm_tn(dh3, v, "gw_up3")], axis=0)
    dh1, dh1b, g_nfw = _rms_bwd(h1, dv, norm_ffn_w, dh2, "norm_ffn_bwd")
    dmerged = _mm([(dh1b, wo)], "nt", BF16, "d_merged")
    gw_o = _mm_tn(merged, dh1b, "gw_o")
    dya, dys, dga, dgs = _merge_bwd(dmerged, gates, y_a, y_s, d, "merge_bwd")
    dyain = _mm([(dya, waout)], "nt", BF16, "d_ya_in")
    gw_aout = _mm_tn(ya_in, dya, "gw_a_out")
    db, dc, dvv, g_caw = _conv_a_bwd(pa, q_a, dyain, caw, d, "conv_a_bwd")
    dyn = _mm([(dys, wsout)], "nt", BF16, "d_yn", resident_b=True)
    gw_sout = _mm_tn(yn, dys, "gw_s_out")
    grads_rest = dict(w_a_out=gw_aout, w_s_out=gw_sout, w_o=gw_o, w_up=gw_up_t, w_down=gw_down)
    rest_parts = [grads_rest[k].reshape(N_DEV, n, d) for k, n in zip(REST, nrows)]
    h_grest = _xchg_start(rest_parts, [True] * len(REST), rest_parts[0], "scatter_rest_start")
    dy, dz, g_snw = _gnorm_bwd(y, z, dyn, ssd_norm_w, "gnorm_bwd")
    dtb_after = dtb + h_grest["token"][0:1, 0:1]
    dxc, ddtr, g_ssd = _ssd_bwd(xc, dtr, dy, states, dtb_after, alog, dskx, di, "ssd_bwd")
    dxbc, g_scw = _conv_s_bwd(xbc, pre_s, dxc, scw, "conv_s_bwd")
    dsegs = [dga, dgs, db, dc, dvv, dz, dxbc, ddtr.astype(BF16)]
    pairs = [(s, c, d, win_t, a + c * d) for s, a in zip(dsegs[:-1], seg_bounds) for c in range(s.shape[1] // d)]
    pairs.append((dsegs[-1], w_dt))
    gw_in = [_mm_tn(s, u, "gw_in%d" % i) for i, s in enumerate(dsegs)]
    gw_in_t = jnp.concatenate(gw_in[:-1] + [gw_in[-1][:nh]], axis=0)
    in_parts = gw_in_t.reshape(N_DEV, n_blk, d)
    h_gin = _xchg_start([in_parts], [True], in_parts, "scatter_in_start")
    du = _mm(pairs, "nn", BF16, "d_u", tm=512, tn=1024, after=h_gin["token"], resident_b=True)
    dx, _, g_nmw = _rms_bwd(x2, du, norm_mix_w, dh1, "norm_mix_bwd")

    small_grads = dict(norm_mix_w=g_nmw[0], ssd_conv_b=g_scw[4], dt_bias=g_ssd[0, :nh], a_log=g_ssd[1, :nh],
                       d_skip=g_ssd[2, :nh], ssd_norm_w=g_snw[0], norm_ffn_w=g_nfw[0], ffn_conv_b=g_ffn[3],
                       final_norm_w=g_fnw[0], conv_a_w=g_caw[:3], ssd_conv_w=g_scw[:4], ffn_conv_w=g_ffn[:3])
    small_names = REPL + CONVS
    small_parts = _pack([small_grads[k] for k in small_names] + [loss11], d, 8)
    h_small = _xchg_start([small_parts], [False], small_parts, "gather_small_start")
    rest_recv = _xchg_wait(h_grest, dx, "scatter_rest_wait")
    (in_recv,) = _xchg_wait(h_gin, rest_recv[0], "scatter_in_wait")
    (small_all,) = _xchg_wait(h_small, in_recv, "gather_small_wait")
    small_sum = _sum_parts(small_all, "sum_small_grads")
    *small_list, loss = _unpack(small_sum, [small_grads[k].shape for k in small_names] + [()])
    small_g = dict(zip(small_names, small_list))

    res = {}

    def update(k, parts):
        outs = _adamw(parts, *(_as_rows(k, src[k]) for src in (wts, mom1, mom2)), "adamw_" + k)
        for kind, a in zip(("g", "d", "m", "v"), outs):
            res[kind, k] = (a.T if k in TRANSPOSED else a)[None]

    for k, parts in zip(REST, rest_recv):
        update(k, parts)
    lanes_rows = n_blk * d // 128
    as_lanes = lambda a: a[0].T.reshape(lanes_rows, 128)
    tile = max([c for c in range(8, 4097, 8) if lanes_rows % c == 0] or [lanes_rows])
    g_in = _sum_parts(in_recv, "sum_in_grads").reshape(1, lanes_rows, 128)
    outs = _adamw(g_in, *(as_lanes(src["w_in"]) for src in (wts, mom1, mom2)), "adamw_w_in", tile_rows=tile)
    for kind, a in zip(("g", "d", "m", "v"), outs):
        res[kind, "w_in"] = a.reshape(n_blk, d).T[None]
    local_g = {}
    for k in REPL:
        local_g[k] = small_g[k].reshape(wts[k].shape)
    for k in CONVS:
        n = wts[k].shape[2]
        local_g[k] = lax.dynamic_slice_in_dim(small_g[k], me * n, n, axis=1)[None]
    for k in small_names:
        as2d = lambda a: a.reshape(-1, a.shape[-1])
        outs = _adamw(as2d(local_g[k])[None], *(as2d(src[k]) for src in (wts, mom1, mom2)), "adamw_" + k)
        for kind, a in zip(("g", "d", "m", "v"), outs):
            res[kind, k] = a.reshape(wts[k].shape)

    return (loss, dx[None], *[res["g", k] for k in ORDER], *[res["d", k] for k in ORDER],
            *[res["m", k] for k in ORDER], *[res["v", k] for k in ORDER])
```

```python
import functools
import math

import jax
import jax.numpy as jnp
from jax import lax
from jax.experimental import pallas as pl
from jax.experimental.pallas import tpu as pltpu

F32 = jnp.float32
BF16 = jnp.bfloat16
EPS = 1e-5
HEAD_DIM = 64
N_GROUPS = 4
D_STATE = 128
CHUNK = 128
DT_LANES = 128
HALO = 16
STRIP = 16
SMALL_PARAM = 16 * 1024
N_DEV = 8
V7X_VMEM_LIMIT = 56 * 1024 * 1024
ADAM_LR, ADAM_B1, ADAM_B2, ADAM_EPS, ADAM_WD, ADAM_STEP = 0.001, 0.9, 0.999, 1e-08, 0.01, 10
HIGHEST = lax.Precision.HIGHEST
MESH = pl.DeviceIdType.MESH


def _pc(body, **kw):
    return pl.pallas_call(body, **kw)


def _params():
    return pltpu.CompilerParams(vmem_limit_bytes=V7X_VMEM_LIMIT)


def _pick(n, cands):
    for c in cands:
        if n % c == 0:
            return c
    return n


def _dot(a, b, ca, cb, prec=None):
    return lax.dot_general(a, b, (((ca,), (cb,)), ((), ())), preferred_element_type=F32, precision=prec)


def _sigmoid(x):
    return 0.5 * jnp.tanh(0.5 * x) + 0.5


def _sds(shape, dtype):
    return jax.ShapeDtypeStruct(shape, dtype)


def _mm(pairs, mode, out_dtype, name, n=None, tm=1024, tn=1024, after=None, resident_b=False):
    pairs = [p if len(p) == 5 else (p[0], 0, p[0].shape[1], p[1], 0) for p in pairs]
    m = pairs[0][0].shape[0]
    if n is None:
        n = pairs[0][3].shape[1] if mode == "nn" else pairs[0][3].shape[0]
    tm = min(tm, m)
    rows_nt = [p[4] for p in pairs] if mode == "nt" else []
    tn = next(c for c in ((n,) if resident_b else ()) + (tn, 1408, 512, 256, 128)
              if n % c == 0 and all(r % c == 0 for r in rows_nt))
    npair = len(pairs)
    cb = 0 if mode == "nn" else 1

    def body(*refs):
        o_ref = refs[-1]
        acc = None
        for p in range(npair):
            part = _dot(refs[2 * p][...], refs[2 * p + 1][...], 1, cb)
            acc = part if acc is None else acc + part
        o_ref[...] = acc.astype(o_ref.dtype)

    in_specs, args = [], []
    for a, a_col, kk, b, b_row in pairs:
        in_specs.append(pl.BlockSpec((tm, kk), lambda i, j, c=a_col: (i, c)))
        if mode == "nn":
            assert b_row % kk == 0 and (not resident_b or n == tn)
            in_specs.append(pl.BlockSpec((kk, tn), lambda i, j, r=b_row // kk: (r, j),
                                         pipeline_mode=pl.Buffered(1) if resident_b else None))
        else:
            in_specs.append(pl.BlockSpec((tn, kk), lambda i, j, r=b_row // tn: (r + j, 0),
                                         pipeline_mode=pl.Buffered(1) if resident_b and tn == n else None))
        args += [a, b]
    if after is not None:
        in_specs.append(pl.BlockSpec(memory_space=pl.ANY))
        args.append(after)
    return _pc(body, name=name, grid=(m // tm, n // tn), in_specs=in_specs,
               out_specs=pl.BlockSpec((tm, tn), lambda i, j: (i, j)),
               out_shape=_sds((m, n), out_dtype), compiler_params=_params())(*args)


def _mm_tn(a, b, name, tm=2048):
    m, ka = a.shape
    nb = b.shape[1]
    tm = min(tm, m)
    nm = m // tm
    tk = _pick(ka, (1024, 1408, 512, 256, 128))
    tn = _pick(nb, (1024, 512, 256, 128))

    def body(a_ref, b_ref, o_ref, acc):
        t = pl.program_id(2)

        @pl.when(t == 0)
        def _():
            acc[...] = jnp.zeros_like(acc)
        acc[...] += _dot(a_ref[...], b_ref[...], 0, 0)

        @pl.when(t == nm - 1)
        def _():
            o_ref[...] = acc[...].astype(o_ref.dtype)

    return _pc(body, name=name, grid=(ka // tk, nb // tn, nm),
               in_specs=[pl.BlockSpec((tm, tk), lambda i, j, t: (t, i)),
                         pl.BlockSpec((tm, tn), lambda i, j, t: (t, j))],
               out_specs=pl.BlockSpec((tk, tn), lambda i, j, t: (i, j)),
               out_shape=_sds((ka, nb), BF16), scratch_shapes=[pltpu.VMEM((tk, tn), F32)],
               compiler_params=_params())(a, b)


def _strips(tm, strip=STRIP):
    return [slice(r * strip, (r + 1) * strip) for r in range(tm // strip)]


def _fold8(a):
    out = a[0:8, :]
    for r in range(8, a.shape[0], 8):
        out = out + a[r:r + 8, :]
    return out


def _colsum(a8):
    return jnp.sum(a8, axis=0, keepdims=True)


def _rms_fwd(x, w, after, name):
    t, d = x.shape
    tm = min(1024, t)

    def body(x_ref, w_ref, after_ref, o_ref):
        wv = w_ref[...]
        for rows in _strips(tm):
            xv = x_ref[rows, :]
            r = lax.rsqrt(jnp.mean(xv * xv, axis=-1, keepdims=True) + EPS)
            o_ref[rows, :] = (xv * r * wv).astype(o_ref.dtype)

    return _pc(body, name=name, grid=(t // tm,),
               in_specs=[pl.BlockSpec((tm, d), lambda i: (i, 0)), pl.BlockSpec((1, d), lambda i: (0, 0)),
                         pl.BlockSpec(memory_space=pl.ANY)],
               out_specs=pl.BlockSpec((tm, d), lambda i: (i, 0)),
               out_shape=_sds((t, d), BF16), compiler_params=_params())(x, w, after)


def _resnorm_fwd(x, mo, w, name):
    t, d = x.shape
    tm = min(1024, t)

    def body(x_ref, mo_ref, w_ref, h_ref, v_ref):
        wv = w_ref[...]
        for rows in _strips(tm):
            h = x_ref[rows, :] + mo_ref[rows, :].astype(F32)
            r = lax.rsqrt(jnp.mean(h * h, axis=-1, keepdims=True) + EPS)
            h_ref[rows, :] = h
            v_ref[rows, :] = (h * r * wv).astype(v_ref.dtype)

    row = pl.BlockSpec((tm, d), lambda i: (i, 0))
    return _pc(body, name=name, grid=(t // tm,),
               in_specs=[row, row, pl.BlockSpec((1, d), lambda i: (0, 0))],
               out_specs=[row, row], out_shape=[_sds((t, d), F32), _sds((t, d), BF16)],
               compiler_params=_params())(x, mo, w)


def _rms_bwd(h, dy, w, dres, name):
    t, d = h.shape
    tm = min(1024, t)

    def body(h_ref, dy_ref, w_ref, dres_ref, dx_ref, dxb_ref, dw_ref):
        @pl.when(pl.program_id(0) == 0)
        def _():
            dw_ref[...] = jnp.zeros_like(dw_ref)
        wv = w_ref[...]
        acc = jnp.zeros((8, d), F32)
        for rows in _strips(tm):
            hv = h_ref[rows, :]
            dyv = dy_ref[rows, :].astype(F32)
            r = lax.rsqrt(jnp.mean(hv * hv, axis=-1, keepdims=True) + EPS)
            n = hv * r
            dn = dyv * wv
            acc = acc + _fold8(dyv * n)
            dx = dres_ref[rows, :] + r * (dn - n * jnp.mean(dn * n, axis=-1, keepdims=True))
            dx_ref[rows, :] = dx
            dxb_ref[rows, :] = dx.astype(BF16)
        dw_ref[0:1, :] += _colsum(acc)

    row = pl.BlockSpec((tm, d), lambda i: (i, 0))
    return _pc(body, name=name, grid=(t // tm,),
               in_specs=[row, row, pl.BlockSpec((1, d), lambda i: (0, 0)), row],
               out_specs=[row, row, pl.BlockSpec((8, d), lambda i: (0, 0))],
               out_shape=[_sds((t, d), F32), _sds((t, d), BF16), _sds((8, d), F32)],
               compiler_params=_params())(h, dy, w, dres)


def _final(h1, dd, tgt, w, name):
    t, d = h1.shape
    tm = min(1024, t)
    nt = t // tm

    def body(h1_ref, dd_ref, tgt_ref, w_ref, loss_ref, dh_ref, dhb_ref, dw_ref, acc):
        i = pl.program_id(0)

        @pl.when(i == 0)
        def _():
            dw_ref[...] = jnp.zeros_like(dw_ref)
            acc[...] = jnp.zeros_like(acc)
        wv = w_ref[...]
        sq = jnp.zeros((8, d), F32)
        dw = jnp.zeros((8, d), F32)
        for rows in _strips(tm):
            h = h1_ref[rows, :] + dd_ref[rows, :].astype(F32)
            r = lax.rsqrt(jnp.mean(h * h, axis=-1, keepdims=True) + EPS)
            n = h * r
            e = n * wv - tgt_ref[rows, :]
            sq = sq + _fold8(e * e)
            dout = e * (1.0 / d)
            dn = dout * wv
            dw = dw + _fold8(dout * n)
            dh = r * (dn - n * jnp.mean(dn * n, axis=-1, keepdims=True))
            dh_ref[rows, :] = dh
            dhb_ref[rows, :] = dh.astype(BF16)
        acc[...] += _colsum(sq)
        dw_ref[0:1, :] += _colsum(dw)

        @pl.when(i == nt - 1)
        def _():
            loss_ref[...] = jnp.sum(acc[...], axis=-1, keepdims=True) * (0.5 / d)

    row = pl.BlockSpec((tm, d), lambda i: (i, 0))
    return _pc(body, name=name, grid=(nt,),
               in_specs=[row, row, row, pl.BlockSpec((1, d), lambda i: (0, 0))],
               out_specs=[pl.BlockSpec((1, 1), lambda i: (0, 0)), row, row, pl.BlockSpec((8, d), lambda i: (0, 0))],
               out_shape=[_sds((1, 1), F32), _sds((t, d), F32), _sds((t, d), BF16), _sds((8, d), F32)],
               scratch_shapes=[pltpu.VMEM((1, d), F32)], compiler_params=_params())(h1, dd, tgt, w)


def _tile_specs(t, tm, tc, col0):
    th = tm // HALO
    last = t // HALO - 1
    cur = pl.BlockSpec((tm, tc), lambda j, i: (i, col0 + j))
    prev = pl.BlockSpec((HALO, tc), lambda j, i: (jnp.maximum(i * th - 1, 0), col0 + j))
    nxt = pl.BlockSpec((HALO, tc), lambda j, i: (jnp.minimum((i + 1) * th, last), col0 + j))
    return cur, prev, nxt


def _conv_strip(buf, w, k, rows):
    out = None
    for j in range(k):
        term = w[j:j + 1, :] * buf[pl.ds(HALO - (k - 1) + j + rows.start, STRIP), :]
        out = term if out is None else out + term
    return out


def _conv_backward(dbuf, x_strip, emit, w, acc_ref, k, tm, with_bias):
    tc = dbuf.shape[1]
    accs = [jnp.zeros((8, tc), F32) for _ in range(k + int(with_bias))]
    for rows in _strips(tm):
        xs = x_strip(rows)
        dx = None
        for j in range(k):
            ds = dbuf[pl.ds(rows.start + k - 1 - j, STRIP), :]
            term = w[j:j + 1, :] * ds
            dx = term if dx is None else dx + term
            accs[j] = accs[j] + _fold8(ds * xs)
            if with_bias and j == k - 1:
                accs[k] = accs[k] + _fold8(ds)
        emit(rows, dx)
    for j, a in enumerate(accs):
        acc_ref[j:j + 1, :] += _colsum(a)


def _conv_a_fwd(pa, w, d, name):
    t = pa.shape[0]
    tm, tc = min(1024, t), _pick(d, (512, 256, 128))
    nd = d // tc

    def body(b_ref, c_ref, v_ref, cp_ref, vp_ref, w_ref, o_ref, q_ref, buf):
        keep = (pl.program_id(1) > 0).astype(F32)
        buf[0:HALO, :] = cp_ref[...].astype(F32) * vp_ref[...].astype(F32) * keep
        for rows in _strips(tm):
            buf[HALO + rows.start:HALO + rows.stop, :] = c_ref[rows, :].astype(F32) * v_ref[rows, :].astype(F32)
        wv = w_ref[...]
        for rows in _strips(tm):
            q = _conv_strip(buf, wv, 3, rows)
            q_ref[rows, :] = q.astype(q_ref.dtype)
            o_ref[rows, :] = (b_ref[rows, :].astype(F32) * q).astype(o_ref.dtype)

    b_cur, _, _ = _tile_specs(t, tm, tc, 0)
    c_cur, c_prev, _ = _tile_specs(t, tm, tc, nd)
    v_cur, v_prev, _ = _tile_specs(t, tm, tc, 2 * nd)
    return _pc(body, name=name, grid=(nd, t // tm),
               in_specs=[b_cur, c_cur, v_cur, c_prev, v_prev, pl.BlockSpec((3, tc), lambda j, i: (0, j))],
               out_specs=[pl.BlockSpec((tm, tc), lambda j, i: (i, j))] * 2,
               out_shape=[_sds((t, d), BF16)] * 2,
               scratch_shapes=[pltpu.VMEM((tm + HALO, tc), F32)],
               compiler_params=_params())(pa, pa, pa, pa, pa, w)


def _conv_a_bwd(pa, q, dya, w, d, name):
    t = pa.shape[0]
    tm, tc = min(1024, t), _pick(d, (512, 256, 128))
    nd, nt = d // tc, t // tm

    def body(b_ref, c_ref, v_ref, bn_ref, q_ref, g_ref, gn_ref, w_ref, db_ref, dc_ref, dv_ref, acc_ref, dbuf):
        i = pl.program_id(1)

        @pl.when(i == 0)
        def _():
            acc_ref[...] = jnp.zeros_like(acc_ref)
        for rows in _strips(tm):
            g = g_ref[rows, :].astype(F32)
            dbuf[rows, :] = g * b_ref[rows, :].astype(F32)
            db_ref[rows, :] = (g * q_ref[rows, :].astype(F32)).astype(BF16)
        dbuf[tm:tm + HALO, :] = gn_ref[...].astype(F32) * bn_ref[...].astype(F32) * (i < nt - 1).astype(F32)

        def emit(rows, dp):
            dc_ref[rows, :] = (dp * v_ref[rows, :].astype(F32)).astype(BF16)
            dv_ref[rows, :] = (dp * c_ref[rows, :].astype(F32)).astype(BF16)

        _conv_backward(dbuf, lambda rows: c_ref[rows, :].astype(F32) * v_ref[rows, :].astype(F32), emit,
                       w_ref[...], acc_ref, 3, tm, False)

    b_cur, _, b_next = _tile_specs(t, tm, tc, 0)
    c_cur, _, _ = _tile_specs(t, tm, tc, nd)
    v_cur, _, _ = _tile_specs(t, tm, tc, 2 * nd)
    g_cur, _, g_next = _tile_specs(t, tm, tc, 0)
    out = pl.BlockSpec((tm, tc), lambda j, i: (i, j))
    return _pc(body, name=name, grid=(nd, nt),
               in_specs=[b_cur, c_cur, v_cur, b_next, g_cur, g_cur, g_next,
                         pl.BlockSpec((3, tc), lambda j, i: (0, j))],
               out_specs=[out, out, out, pl.BlockSpec((8, tc), lambda j, i: (0, j))],
               out_shape=[_sds((t, d), BF16)] * 3 + [_sds((8, d), F32)],
               scratch_shapes=[pltpu.VMEM((tm + HALO, tc), F32)],
               compiler_params=_params())(pa, pa, pa, pa, q, dya, dya, w)


def _conv_s_fwd(xbc, w, b, name):
    t, dx = xbc.shape
    tm, tc = min(1024, t), _pick(dx, (512, 256, 128))

    def body(x_ref, xp_ref, w_ref, b_ref, o_ref, pre_ref, buf):
        buf[0:HALO, :] = xp_ref[...].astype(F32) * (pl.program_id(1) > 0).astype(F32)
        for rows in _strips(tm):
            buf[HALO + rows.start:HALO + rows.stop, :] = x_ref[rows, :].astype(F32)
        wv, bv = w_ref[...], b_ref[...]
        for rows in _strips(tm):
            pre = _conv_strip(buf, wv, 4, rows) + bv
            pre_ref[rows, :] = pre.astype(pre_ref.dtype)
            o_ref[rows, :] = (pre * _sigmoid(pre)).astype(o_ref.dtype)

    cur, prev, _ = _tile_specs(t, tm, tc, 0)
    return _pc(body, name=name, grid=(dx // tc, t // tm),
               in_specs=[cur, prev, pl.BlockSpec((4, tc), lambda j, i: (0, j)),
                         pl.BlockSpec((1, tc), lambda j, i: (0, j))],
               out_specs=[pl.BlockSpec((tm, tc), lambda j, i: (i, j))] * 2,
               out_shape=[_sds((t, dx), BF16)] * 2,
               scratch_shapes=[pltpu.VMEM((tm + HALO, tc), F32)],
               compiler_params=_params())(xbc, xbc, w, b)


def _dsilu(pre):
    s = _sigmoid(pre)
    return s * (1.0 + pre * (1.0 - s))


def _conv_s_bwd(xbc, pre, dxc, w, name):
    t, dx = xbc.shape
    tm, tc = min(1024, t), _pick(dx, (512, 256, 128))
    nt = t // tm

    def body(x_ref, p_ref, pn_ref, g_ref, gn_ref, w_ref, dx_ref, acc_ref, dbuf):
        i = pl.program_id(1)

        @pl.when(i == 0)
        def _():
            acc_ref[...] = jnp.zeros_like(acc_ref)
        for rows in _strips(tm):
            dbuf[rows, :] = g_ref[rows, :].astype(F32) * _dsilu(p_ref[rows, :].astype(F32))
        dbuf[tm:tm + HALO, :] = (gn_ref[...].astype(F32) * _dsilu(pn_ref[...].astype(F32))
                                 * (i < nt - 1).astype(F32))

        def emit(rows, d_in):
            dx_ref[rows, :] = d_in.astype(BF16)

        _conv_backward(dbuf, lambda rows: x_ref[rows, :].astype(F32), emit, w_ref[...], acc_ref, 4, tm, True)

    cur, _, nxt = _tile_specs(t, tm, tc, 0)
    return _pc(body, name=name, grid=(dx // tc, nt),
               in_specs=[cur, cur, nxt, cur, nxt, pl.BlockSpec((4, tc), lambda j, i: (0, j))],
               out_specs=[pl.BlockSpec((tm, tc), lambda j, i: (i, j)), pl.BlockSpec((8, tc), lambda j, i: (0, j))],
               out_shape=[_sds((t, dx), BF16), _sds((8, dx), F32)],
               scratch_shapes=[pltpu.VMEM((tm + HALO, tc), F32)],
               compiler_params=_params())(xbc, pre, pre, dxc, dxc, w)


def _ffn_fwd(hv, w, b, f, name):
    t = hv.shape[0]
    tm, tc = min(1024, t), _pick(f, (512, 256, 128))
    nf = f // tc

    def body(h1_ref, h1p_ref, h3_ref, w_ref, b_ref, o_ref, c1_ref, buf):
        buf[0:HALO, :] = h1p_ref[...].astype(F32) * (pl.program_id(1) > 0).astype(F32)
        for rows in _strips(tm):
            buf[HALO + rows.start:HALO + rows.stop, :] = h1_ref[rows, :].astype(F32)
        wv, bv = w_ref[...], b_ref[...]
        for rows in _strips(tm):
            c1 = _conv_strip(buf, wv, 3, rows) + bv
            c1_ref[rows, :] = c1.astype(c1_ref.dtype)
            o_ref[rows, :] = (c1 * _sigmoid(c1) * h3_ref[rows, :].astype(F32)).astype(o_ref.dtype)

    h1_cur, h1_prev, _ = _tile_specs(t, tm, tc, 0)
    h3_cur, _, _ = _tile_specs(t, tm, tc, nf)
    return _pc(body, name=name, grid=(nf, t // tm),
               in_specs=[h1_cur, h1_prev, h3_cur, pl.BlockSpec((3, tc), lambda j, i: (0, j)),
                         pl.BlockSpec((1, tc), lambda j, i: (0, j))],
               out_specs=[pl.BlockSpec((tm, tc), lambda j, i: (i, j))] * 2,
               out_shape=[_sds((t, f), BF16)] * 2,
               scratch_shapes=[pltpu.VMEM((tm + HALO, tc), F32)],
               compiler_params=_params())(hv, hv, hv, w, b)


def _ffn_bwd(hv, c1, dact, w, f, name):
    t = hv.shape[0]
    tm, tc = min(1024, t), _pick(f, (512, 256, 128))
    nf, nt = f // tc, t // tm

    def body(h1_ref, h3_ref, h3n_ref, c_ref, cn_ref, g_ref, gn_ref, w_ref, dh1_ref, dh3_ref, acc_ref, dbuf):
        i = pl.program_id(1)

        @pl.when(i == 0)
        def _():
            acc_ref[...] = jnp.zeros_like(acc_ref)
        for rows in _strips(tm):
            c1v, g = c_ref[rows, :].astype(F32), g_ref[rows, :].astype(F32)
            s1 = _sigmoid(c1v)
            dh3_ref[rows, :] = (g * c1v * s1).astype(BF16)
            dbuf[rows, :] = g * h3_ref[rows, :].astype(F32) * s1 * (1.0 + c1v * (1.0 - s1))
        dbuf[tm:tm + HALO, :] = (gn_ref[...].astype(F32) * h3n_ref[...].astype(F32)
                                 * _dsilu(cn_ref[...].astype(F32)) * (i < nt - 1).astype(F32))

        def emit(rows, d_in):
            dh1_ref[rows, :] = d_in.astype(BF16)

        _conv_backward(dbuf, lambda rows: h1_ref[rows, :].astype(F32), emit, w_ref[...], acc_ref, 3, tm, True)

    h1_cur, _, _ = _tile_specs(t, tm, tc, 0)
    h3_cur, _, h3_next = _tile_specs(t, tm, tc, nf)
    g_cur, _, g_next = _tile_specs(t, tm, tc, 0)
    out = pl.BlockSpec((tm, tc), lambda j, i: (i, j))
    return _pc(body, name=name, grid=(nf, nt),
               in_specs=[h1_cur, h3_cur, h3_next, g_cur, g_next, g_cur, g_next,
                         pl.BlockSpec((3, tc), lambda j, i: (0, j))],
               out_specs=[out, out, pl.BlockSpec((8, tc), lambda j, i: (0, j))],
               out_shape=[_sds((t, f), BF16), _sds((t, f), BF16), _sds((8, f), F32)],
               scratch_shapes=[pltpu.VMEM((tm + HALO, tc), F32)],
               compiler_params=_params())(hv, hv, hv, c1, c1, dact, dact, w)


def _gnorm_fwd(y, z, w, name):
    t, di = y.shape
    gw = di // N_GROUPS
    tm = min(1024, t)

    def body(y_ref, z_ref, w_ref, o_ref):
        wv = w_ref[...]
        for rows in _strips(tm):
            zv = z_ref[rows, :].astype(F32)
            yz = y_ref[rows, :].astype(F32) * zv * _sigmoid(zv)
            r = lax.rsqrt(jnp.mean(yz * yz, axis=-1, keepdims=True) + EPS)
            o_ref[rows, :] = (yz * r * wv).astype(o_ref.dtype)

    blk = pl.BlockSpec((tm, gw), lambda j, i: (i, j))
    return _pc(body, name=name, grid=(N_GROUPS, t // tm),
               in_specs=[blk, blk, pl.BlockSpec((1, gw), lambda j, i: (0, j))],
               out_specs=blk, out_shape=_sds((t, di), BF16), compiler_params=_params())(y, z, w)


def _gnorm_bwd(y, z, dyn, w, name):
    t, di = y.shape
    gw = di // N_GROUPS
    tm = min(1024, t)

    def body(y_ref, z_ref, g_ref, w_ref, dy_ref, dz_ref, dw_ref):
        @pl.when(pl.program_id(1) == 0)
        def _():
            dw_ref[...] = jnp.zeros_like(dw_ref)
        wv = w_ref[...]
        acc = jnp.zeros((8, gw), F32)
        for rows in _strips(tm):
            yv, zv, g = y_ref[rows, :].astype(F32), z_ref[rows, :].astype(F32), g_ref[rows, :].astype(F32)
            s = _sigmoid(zv)
            sz = zv * s
            yz = yv * sz
            r = lax.rsqrt(jnp.mean(yz * yz, axis=-1, keepdims=True) + EPS)
            n = yz * r
            dn = g * wv
            acc = acc + _fold8(g * n)
            dyz = r * (dn - n * jnp.mean(dn * n, axis=-1, keepdims=True))
            dy_ref[rows, :] = (dyz * sz).astype(BF16)
            dz_ref[rows, :] = (dyz * yv * s * (1.0 + zv * (1.0 - s))).astype(BF16)
        dw_ref[0:1, :] += _colsum(acc)

    blk = pl.BlockSpec((tm, gw), lambda j, i: (i, j))
    return _pc(body, name=name, grid=(N_GROUPS, t // tm),
               in_specs=[blk, blk, blk, pl.BlockSpec((1, gw), lambda j, i: (0, j))],
               out_specs=[blk, blk, pl.BlockSpec((8, gw), lambda j, i: (0, j))],
               out_shape=[_sds((t, di), BF16), _sds((t, di), BF16), _sds((8, di), F32)],
               compiler_params=_params())(y, z, dyn, w)


def _merge_fwd(gates, ya, ys, d, name):
    t = ya.shape[0]
    tm, tc = min(1024, t), _pick(d, (512, 256, 128))
    nd = d // tc

    def body(ga_ref, gs_ref, ya_ref, ys_ref, o_ref):
        for rows in _strips(tm):
            o_ref[rows, :] = (_sigmoid(ga_ref[rows, :].astype(F32)) * ya_ref[rows, :].astype(F32)
                              + _sigmoid(gs_ref[rows, :].astype(F32)) * ys_ref[rows, :].astype(F32)
                              ).astype(o_ref.dtype)

    blk = pl.BlockSpec((tm, tc), lambda j, i: (i, j))
    return _pc(body, name=name, grid=(nd, t // tm),
               in_specs=[blk, pl.BlockSpec((tm, tc), lambda j, i: (i, nd + j)), blk, blk],
               out_specs=blk, out_shape=_sds((t, d), BF16), compiler_params=_params())(gates, gates, ya, ys)


def _merge_bwd(dm, gates, ya, ys, d, name):
    t = ya.shape[0]
    tm, tc = min(1024, t), _pick(d, (512, 256, 128))
    nd = d // tc

    def body(dm_ref, ga_ref, gs_ref, ya_ref, ys_ref, dya_ref, dys_ref, dga_ref, dgs_ref):
        for rows in _strips(tm):
            g = dm_ref[rows, :].astype(F32)
            sa, ss = _sigmoid(ga_ref[rows, :].astype(F32)), _sigmoid(gs_ref[rows, :].astype(F32))
            dya_ref[rows, :] = (g * sa).astype(BF16)
            dys_ref[rows, :] = (g * ss).astype(BF16)
            dga_ref[rows, :] = (g * ya_ref[rows, :].astype(F32) * sa * (1.0 - sa)).astype(BF16)
            dgs_ref[rows, :] = (g * ys_ref[rows, :].astype(F32) * ss * (1.0 - ss)).astype(BF16)

    blk = pl.BlockSpec((tm, tc), lambda j, i: (i, j))
    return _pc(body, name=name, grid=(nd, t // tm),
               in_specs=[blk, blk, pl.BlockSpec((tm, tc), lambda j, i: (i, nd + j)), blk, blk],
               out_specs=[blk] * 4, out_shape=[_sds((t, d), BF16)] * 4,
               compiler_params=_params())(dm, gates, gates, ya, ys)


def _ssd_chunk_terms(dtr, dtb, alog):
    xx = dtr + dtb
    dt = jnp.maximum(xx, 0.0) + jnp.log(1.0 + jnp.exp(-jnp.abs(xx)))
    a = -jnp.exp(alog)
    li = lax.broadcasted_iota(jnp.int32, (CHUNK, CHUNK), 0)
    si = lax.broadcasted_iota(jnp.int32, (CHUNK, CHUNK), 1)
    causal = li >= si
    acum = _dot(causal.astype(F32), dt * a, 1, 0, HIGHEST)
    return xx, dt, a, acum, acum.T, causal


def _split2(x):
    hi = x.astype(BF16)
    return hi, (x - hi.astype(F32)).astype(BF16)


def _expand(v, e, exact=True):
    hi, lo = _split2(v)
    out = _dot(hi, e, 1, 0)
    return out + _dot(lo, e, 1, 0) if exact else out


def _segsum(s, e):
    hi, lo = _split2(s)
    return _dot(hi, e, 1, 1) + _dot(lo, e, 1, 1)


def _head_maps(di):
    nh = di // HEAD_DIM
    h = jnp.arange(DT_LANES)[:, None]
    e64 = (jnp.arange(di)[None, :] // HEAD_DIM == h).astype(BF16)
    e128 = (jnp.arange(nh * CHUNK)[None, :] // CHUNK == h).astype(BF16)
    return e64, e128


def _pair_blockdiag(p, left):
    zero = jnp.zeros_like(p)
    return jnp.concatenate([jnp.where(left, p, zero), jnp.where(left, zero, p)], axis=0)


def _ssd_fwd(xc, dtr, dtb, alog, dskx, di, name):
    t = xc.shape[0]
    dx = xc.shape[1]
    nc = t // CHUNK
    nh = di // HEAD_DIM
    hpg = nh // N_GROUPS
    gw = hpg * HEAD_DIM
    boff, coff = di, di + N_GROUPS * D_STATE
    e64, e128 = _head_maps(di)

    def body(xc_ref, dtr_ref, dtb_ref, alog_ref, dsk_ref, e64_ref, e128_ref, y_ref, st_ref, state):
        @pl.when(pl.program_id(0) == 0)
        def _():
            state[...] = jnp.zeros_like(state)
        _, dt, _, acum, acum_t, causal = _ssd_chunk_terms(dtr_ref[...], dtb_ref[...], alog_ref[...])
        last = acum[CHUNK - 1:CHUNK, :]
        e64v = e64_ref[...]
        dtx = _expand(dt, e64v, False)
        eax = _expand(jnp.exp(acum), e64v)
        dex = _expand(dt * jnp.exp(last - acum), e64v, False)
        acx = _expand(acum, e128_ref[...])
        st_ref[0] = state[...]
        left = lax.broadcasted_iota(jnp.int32, (CHUNK, 2 * HEAD_DIM), 1) < HEAD_DIM
        for g in range(N_GROUPS):
            gs = slice(g * gw, (g + 1) * gw)
            bg = xc_ref[:, boff + g * D_STATE:boff + (g + 1) * D_STATE]
            cg = xc_ref[:, coff + g * D_STATE:coff + (g + 1) * D_STATE]
            gm = _dot(cg, bg, 1, 1)
            xg = xc_ref[:, gs].astype(F32)
            xdb = (xg * dtx[:, gs]).astype(BF16)
            sin = state[:, gs]
            yo = _dot(cg, sin.astype(BF16), 1, 0) * eax[:, gs]
            for jp in range(hpg // 2):
                h0 = g * hpg + 2 * jp
                ps = slice(jp * 2 * HEAD_DIM, (jp + 1) * 2 * HEAD_DIM)
                ms = []
                for hh in (h0, h0 + 1):
                    seg = acx[:, hh * CHUNK:(hh + 1) * CHUNK] - acum_t[hh:hh + 1, :]
                    ms.append((gm * jnp.exp(jnp.where(causal, seg, -1e30))).astype(BF16))
                yd = _dot(jnp.concatenate(ms, axis=1), _pair_blockdiag(xdb[:, ps], left), 1, 0)
                col = slice(g * gw + jp * 2 * HEAD_DIM, g * gw + (jp + 1) * 2 * HEAD_DIM)
                y_ref[:, col] = (yd + yo[:, ps] + dsk_ref[:, col] * xg[:, ps]).astype(y_ref.dtype)
            xe = (xg * dex[:, gs]).astype(BF16)
            state[:, gs] = eax[CHUNK - 1:CHUNK, gs] * sin + _dot(bg, xe, 0, 0)

    small = pl.BlockSpec((1, DT_LANES), lambda c: (0, 0))
    whole = lambda a: pl.BlockSpec(a.shape, lambda c: (0, 0))
    return _pc(body, name=name, grid=(nc,),
               in_specs=[pl.BlockSpec((CHUNK, dx), lambda c: (c, 0)),
                         pl.BlockSpec((CHUNK, DT_LANES), lambda c: (c, 0)), small, small,
                         whole(dskx), whole(e64), whole(e128)],
               out_specs=[pl.BlockSpec((CHUNK, di), lambda c: (c, 0)),
                          pl.BlockSpec((1, D_STATE, di), lambda c: (c, 0, 0))],
               out_shape=[_sds((t, di), BF16), _sds((nc, D_STATE, di), F32)],
               scratch_shapes=[pltpu.VMEM((D_STATE, di), F32)],
               compiler_params=_params())(xc, dtr, dtb, alog, dskx, e64, e128)


def _ssd_bwd(xc, dtr, dy, states, dtb, alog, dskx, di, name):
    t = xc.shape[0]
    dx = xc.shape[1]
    nc = t // CHUNK
    nh = di // HEAD_DIM
    hpg = nh // N_GROUPS
    gw = hpg * HEAD_DIM
    boff, coff = di, di + N_GROUPS * D_STATE
    e64, e128 = _head_maps(di)

    def body(xc_ref, dtr_ref, dy_ref, st_ref, dtb_ref, alog_ref, dsk_ref, e64_ref, e128_ref,
             dxc_ref, ddtr_ref, sm_ref, dstate, darow):
        @pl.when(pl.program_id(0) == 0)
        def _():
            dstate[...] = jnp.zeros_like(dstate)
            sm_ref[...] = jnp.zeros_like(sm_ref)
        darow[...] = jnp.zeros_like(darow)
        xx, dt, a, acum, acum_t, causal = _ssd_chunk_terms(dtr_ref[...], dtb_ref[...], alog_ref[...])
        last = acum[CHUNK - 1:CHUNK, :]
        e64v = e64_ref[...]
        dtx = _expand(dt, e64v, False)
        eax = _expand(jnp.exp(acum), e64v)
        eex = _expand(jnp.exp(last - acum), e64v, False)
        acx = _expand(acum, e128_ref[...])
        left = lax.broadcasted_iota(jnp.int32, (CHUNK, 2 * HEAD_DIM), 1) < HEAD_DIM
        lane = lax.broadcasted_iota(jnp.int32, (CHUNK, DT_LANES), 1)
        sub8 = lax.broadcasted_iota(jnp.int32, (8, gw), 0)
        da_col = jnp.zeros((CHUNK, DT_LANES), F32)
        ddt_col = jnp.zeros((CHUNK, DT_LANES), F32)
        rows = jnp.zeros((8, DT_LANES), F32)
        for g in range(N_GROUPS):
            gs = slice(g * gw, (g + 1) * gw)
            bg = xc_ref[:, boff + g * D_STATE:boff + (g + 1) * D_STATE]
            cg = xc_ref[:, coff + g * D_STATE:coff + (g + 1) * D_STATE]
            gm = _dot(cg, bg, 1, 1)
            e64g = e64v[:, gs]
            xg = xc_ref[:, gs].astype(F32)
            dtg, eag, eeg = dtx[:, gs], eax[:, gs], eex[:, gs]
            xd = xg * dtg
            xdb = xd.astype(BF16)
            dyb = dy_ref[:, gs]
            dyf = dyb.astype(F32)
            sin = st_ref[0, :, gs]
            sinb = sin.astype(BF16)
            ds = dstate[:, gs]
            dsb = ds.astype(BF16)
            bds = _dot(bg, dsb, 1, 0)
            dyeb = (dyf * eag).astype(BF16)
            dcg = _dot(dyeb, sinb, 1, 1)
            dstate[:, gs] = eag[CHUNK - 1:CHUNK, :] * ds + _dot(cg, dyeb, 0, 0)
            yo = _dot(cg, sinb, 1, 0) * eag
            xe = xd * eeg
            dbg = _dot(xe.astype(BF16), dsb, 1, 1)
            wterm = bds * xe
            da_col = da_col + _segsum(dyf * yo - wterm, e64g)
            dg = jnp.zeros((CHUNK, CHUNK), F32)
            dxd_parts = []
            for jp in range(hpg // 2):
                h0 = g * hpg + 2 * jp
                ps = slice(jp * 2 * HEAD_DIM, (jp + 1) * 2 * HEAD_DIM)
                lms, mfs = [], []
                for hh in (h0, h0 + 1):
                    seg = acx[:, hh * CHUNK:(hh + 1) * CHUNK] - acum_t[hh:hh + 1, :]
                    lm = jnp.exp(jnp.where(causal, seg, -1e30))
                    lms.append(lm)
                    mfs.append(gm * lm)
                mstack = jnp.concatenate([m.astype(BF16) for m in mfs], axis=0)
                dyp = dyb[:, ps]
                dxd_parts.append(_dot(mstack, _pair_blockdiag(dyp, left), 0, 0))
                dm2 = _dot(dyp, _pair_blockdiag(xdb[:, ps], left), 1, 1)
                for k, hh in enumerate((h0, h0 + 1)):
                    dm = dm2[:, k * CHUNK:(k + 1) * CHUNK]
                    dg = dg + dm * lms[k]
                    q = dm * mfs[k]
                    da_col = da_col + jnp.where(lane == hh, jnp.sum(q, axis=1, keepdims=True), 0.0)
                    darow[hh:hh + 1, :] = -jnp.sum(q, axis=0, keepdims=True)
            dxd = jnp.concatenate(dxd_parts, axis=1) + bds * eeg
            ddt_col = ddt_col + _segsum(dxd * xg, e64g)
            rsum = (jnp.where(sub8 == 0, jnp.sum(wterm, axis=0, keepdims=True), 0.0)
                    + jnp.where(sub8 == 1, jnp.sum(ds * sin, axis=0, keepdims=True), 0.0)
                    + jnp.where(sub8 == 2, jnp.sum(dyf * xg, axis=0, keepdims=True), 0.0))
            rows = rows + _segsum(rsum, e64g)
            dxc_ref[:, gs] = (dxd * dtg + dsk_ref[:, gs] * dyf).astype(dxc_ref.dtype)
            dgb = dg.astype(BF16)
            dxc_ref[:, boff + g * D_STATE:boff + (g + 1) * D_STATE] = (
                dbg + _dot(dgb, cg, 0, 0)).astype(dxc_ref.dtype)
            dxc_ref[:, coff + g * D_STATE:coff + (g + 1) * D_STATE] = (
                dcg + _dot(dgb, bg, 1, 0)).astype(dxc_ref.dtype)
        at_last = rows[0:1, :] + jnp.exp(last) * rows[1:2, :]
        is_last = lax.broadcasted_iota(jnp.int32, (CHUNK, DT_LANES), 0) == CHUNK - 1
        da = da_col + jnp.where(is_last, at_last, 0.0) + darow[...].T
        li = lax.broadcasted_iota(jnp.int32, (CHUNK, CHUNK), 0)
        si = lax.broadcasted_iota(jnp.int32, (CHUNK, CHUNK), 1)
        dla = _dot((si >= li).astype(F32), da, 1, 0, HIGHEST)
        ddtr = (ddt_col + dla * a) * _sigmoid(xx)
        ddtr_ref[...] = ddtr
        sm_ref[0:1, :] += jnp.sum(ddtr, axis=0, keepdims=True)
        sm_ref[1:2, :] += jnp.sum(dla * dt, axis=0, keepdims=True) * a
        sm_ref[2:3, :] += rows[2:3, :]

    small = pl.BlockSpec((1, DT_LANES), lambda c: (0, 0))
    whole = lambda a: pl.BlockSpec(a.shape, lambda c: (0, 0))
    rev = lambda c: (nc - 1 - c, 0)
    return _pc(body, name=name, grid=(nc,),
               in_specs=[pl.BlockSpec((CHUNK, dx), rev), pl.BlockSpec((CHUNK, DT_LANES), rev),
                         pl.BlockSpec((CHUNK, di), rev),
                         pl.BlockSpec((1, D_STATE, di), lambda c: (nc - 1 - c, 0, 0)), small, small,
                         whole(dskx), whole(e64), whole(e128)],
               out_specs=[pl.BlockSpec((CHUNK, dx), rev), pl.BlockSpec((CHUNK, DT_LANES), rev),
                          pl.BlockSpec((8, DT_LANES), lambda c: (0, 0))],
               out_shape=[_sds((t, dx), BF16), _sds((t, DT_LANES), F32), _sds((8, DT_LANES), F32)],
               scratch_shapes=[pltpu.VMEM((D_STATE, di), F32), pltpu.VMEM((DT_LANES, CHUNK), F32)],
               compiler_params=_params())(xc, dtr, dy, states, dtb, alog, dskx, e64, e128)


def _adamw(parts, w, m, v, name):
    npart, rows, width = parts.shape
    if rows * width <= SMALL_PARAM:
        tr, tw = rows, width
    else:
        tr, tw = (_pick(rows, (64, 32, 16, 8)), width) if rows % 8 == 0 else (rows, 128)
    c1 = 1.0 - ADAM_B1 ** ADAM_STEP
    c2 = 1.0 - ADAM_B2 ** ADAM_STEP

    row_strips = _strips(tr) if tr % STRIP == 0 else [slice(0, tr)]
    col_chunks = [slice(c, c + 512) for c in range(0, tw, 512)] if tw % 512 == 0 else [slice(0, tw)]

    def body(p_ref, w_ref, m_ref, v_ref, g_ref, d_ref, nm_ref, nv_ref):
        for rows in row_strips:
            for cols in col_chunks:
                g = p_ref[0, rows, cols].astype(F32)
                for p in range(1, npart):
                    g = g + p_ref[p, rows, cols].astype(F32)
                nm = ADAM_B1 * m_ref[rows, cols] + (1.0 - ADAM_B1) * g
                nv = ADAM_B2 * v_ref[rows, cols] + (1.0 - ADAM_B2) * (g * g)
                g_ref[rows, cols] = g
                nm_ref[rows, cols] = nm
                nv_ref[rows, cols] = nv
                d_ref[rows, cols] = -ADAM_LR * ((nm / c1) / (jnp.sqrt(nv / c2) + ADAM_EPS)
                                                + ADAM_WD * w_ref[rows, cols])

    blk = pl.BlockSpec((tr, tw), lambda i, j: (i, j))
    return _pc(body, name=name, grid=(rows // tr, width // tw),
               in_specs=[pl.BlockSpec((npart, tr, tw), lambda i, j: (0, i, j)), blk, blk, blk],
               out_specs=[blk] * 4, out_shape=[_sds((rows, width), F32)] * 4,
               compiler_params=_params())(parts, w, m, v)


def _sum_parts(parts, name, tile=None):
    npart, rows, width = parts.shape
    tile = rows if tile is None else tile

    def body(p_ref, o_ref):
        for rows_ in _strips(tile, 8 if parts.dtype == F32 else STRIP):
            g = p_ref[0, rows_, :].astype(F32)
            for p in range(1, npart):
                g = g + p_ref[p, rows_, :].astype(F32)
            o_ref[rows_, :] = g

    return _pc(body, name=name, grid=(rows // tile,),
               in_specs=[pl.BlockSpec((npart, tile, width), lambda i: (0, i, 0))],
               out_specs=pl.BlockSpec((tile, width), lambda i: (i, 0)),
               out_shape=_sds((rows, width), F32), compiler_params=_params())(parts)


def _flip(k):
    x, y, c = lax.axis_index("x"), lax.axis_index("y"), lax.axis_index("c")
    px = 1 - x if k & 4 else x
    py = 1 - y if k & 2 else y
    pc = 1 - c if k & 1 else c
    return (px, py, pc), 4 * px + 2 * py + pc


DIRECT = tuple((k, 0) for k in range(1, N_DEV))
TO_CHIPS = ((1, 0), (2, 0), (4, 0), (6, 0))
TO_SIBLING = ((1, 2), (1, 4), (1, 6))


def _copies(arrays, lands, send_sems, recv_sems, scatter, moves):
    _, me = _flip(0)
    outgoing, incoming = [], []
    for i, (kd, kb) in enumerate(moves):
        peer, pidx = _flip(kd)
        _, out_slot = _flip(kb)
        _, in_slot = _flip(kd ^ kb)
        for j, land_ref in enumerate(lands):
            if kb:
                src = land_ref.at[out_slot]
            else:
                src = arrays[j].at[pidx] if scatter[j] else arrays[j]
            sem = len(lands) * i + j
            for dst, bucket in ((land_ref.at[out_slot], outgoing), (land_ref.at[in_slot], incoming)):
                bucket.append(pltpu.make_async_remote_copy(
                    src_ref=src, dst_ref=dst, send_sem=send_sems.at[sem], recv_sem=recv_sems.at[sem],
                    device_id=peer, device_id_type=MESH))
    return outgoing, incoming


HBM_SPEC = pl.BlockSpec(memory_space=pltpu.HBM)
SEM_SPEC = pl.BlockSpec(memory_space=pltpu.SEMAPHORE)
ANY_SPEC = pl.BlockSpec(memory_space=pl.ANY)
EFFECT = pltpu.SideEffectType.DATAFLOW_SIDE_EFFECTING


def _landing_zones(arrays, scatter):
    _, me = _flip(0)
    lands = []
    for a, sc in zip(arrays, scatter):
        own = lax.dynamic_index_in_dim(a, me, 0, keepdims=True) if sc else a[None]
        shape = a.shape if sc else (N_DEV,) + a.shape
        lands.append(lax.dynamic_update_slice(lax.empty(shape, a.dtype), own, (me,) + (0,) * (len(shape) - 1)))
    return lands


def _xchg_start(arrays, scatter, after, name, moves=DIRECT, lands=None):
    if lands is None:
        lands = _landing_zones(arrays, scatter)
    na, nl = len(arrays), len(lands)

    def body(*refs):
        ins, outs = refs[:na + nl], refs[na + nl + 1:]
        outgoing, _ = _copies(ins[:na], ins[na:], outs[0], outs[1], scatter, moves)
        for cp in outgoing:
            cp.start()
        outs[-1][...] = jnp.zeros_like(outs[-1])

    nsem = nl * len(moves)
    operands = [pltpu.with_memory_space_constraint(a, pltpu.HBM) for a in list(arrays) + list(lands)]
    out = _pc(body, name=name,
              out_shape=(pltpu.SemaphoreType.DMA((nsem,)), pltpu.SemaphoreType.DMA((nsem,)),
                         *[pltpu.HBM(a.shape, a.dtype) for a in operands], _sds((8, 128), F32)),
              in_specs=[HBM_SPEC] * (na + nl) + [ANY_SPEC],
              out_specs=(SEM_SPEC, SEM_SPEC, *[HBM_SPEC] * (na + nl), pl.BlockSpec(memory_space=pltpu.VMEM)),
              input_output_aliases={i: 2 + i for i in range(na + nl)},
              compiler_params=pltpu.CompilerParams(has_side_effects=EFFECT))(*operands, after)
    return dict(sems=out[:2], thru=out[2:2 + na + nl], token=out[-1], scatter=scatter, na=na, moves=moves)


def _xchg_wait(handle, after, name):
    na, scatter, moves, thru = handle["na"], handle["scatter"], handle["moves"], handle["thru"]
    n = len(thru)

    def body(*refs):
        ins = refs[:n]
        outgoing, incoming = _copies(ins[:na], ins[na:], refs[n], refs[n + 1], scatter, moves)
        for cp in outgoing:
            cp.wait_send()
        for cp in incoming:
            cp.wait_recv()

    out = _pc(body, name=name, out_shape=tuple(pltpu.HBM(a.shape, a.dtype) for a in thru),
              in_specs=[HBM_SPEC] * n + [SEM_SPEC, SEM_SPEC, ANY_SPEC], out_specs=tuple([HBM_SPEC] * n),
              input_output_aliases={i: i for i in range(n)},
              compiler_params=pltpu.CompilerParams(has_side_effects=EFFECT))(*thru, *handle["sems"], after)
    return out[na:]


def _pack(arrs, width, row_mult):
    flat = jnp.concatenate([a.reshape(-1) for a in arrs])
    n = flat.shape[0]
    rows = -(-n // (width * row_mult)) * row_mult
    return jnp.pad(flat, (0, rows * width - n)).reshape(rows, width)


def _unpack(packed, shapes, lead=None):
    out, off = [], 0
    flat = packed.reshape(-1) if lead is None else packed.reshape(lead, -1)
    for s in shapes:
        n = math.prod(s)
        if lead is None:
            out.append(flat[off:off + n].reshape(s))
        else:
            out.append(flat[:, off:off + n].reshape((lead,) + tuple(s)))
        off += n
    return out


def _blocks_to_cols(blocks):
    nb, rows, n = blocks.shape
    return blocks.transpose(1, 0, 2).reshape(rows, nb * n)


def _pad_rows(a, rows):
    return jnp.pad(a, ((0, rows - a.shape[0]), (0, 0)))


def _pad_lanes(a, lanes):
    return jnp.pad(a, ((0, 0), (0, lanes - a.shape[1])))


REST = ("w_a_out", "w_s_out", "w_o", "w_up", "w_down")
TRANSPOSED = ("w_up", "w_in")
CONVS = ("conv_a_w", "ssd_conv_w", "ffn_conv_w")
REPL = ("norm_mix_w", "ssd_conv_b", "dt_bias", "a_log", "d_skip", "ssd_norm_w", "norm_ffn_w", "ffn_conv_b",
        "final_norm_w")
ORDER = ("norm_mix_w", "w_in", "conv_a_w", "w_a_out", "ssd_conv_w", "ssd_conv_b", "dt_bias", "a_log", "d_skip",
         "ssd_norm_w", "w_s_out", "w_o", "norm_ffn_w", "w_up", "ffn_conv_w", "ffn_conv_b", "w_down", "final_norm_w")


def _as_rows(name, block):
    return block[0].T if name in TRANSPOSED else block[0]


def kernel(x, norm_mix_w, w_in, conv_a_w, w_a_out, ssd_conv_w, ssd_conv_b, dt_bias, a_log, d_skip, ssd_norm_w, w_s_out, w_o, norm_ffn_w, w_up, ffn_conv_w, ffn_conv_b, w_down, final_norm_w, loss_target, m_norm_mix_w, m_w_in, m_conv_a_w, m_w_a_out, m_ssd_conv_w, m_ssd_conv_b, m_dt_bias, m_a_log, m_d_skip, m_ssd_norm_w, m_w_s_out, m_w_o, m_norm_ffn_w, m_w_up, m_ffn_conv_w, m_ffn_conv_b, m_w_down, m_final_norm_w, v_norm_mix_w, v_w_in, v_conv_a_w, v_w_a_out, v_ssd_conv_w, v_ssd_conv_b, v_dt_bias, v_a_log, v_d_skip, v_ssd_norm_w, v_w_s_out, v_w_o, v_norm_ffn_w, v_w_up, v_ffn_conv_w, v_ffn_conv_b, v_w_down, v_final_norm_w):
    wts = dict(norm_mix_w=norm_mix_w, w_in=w_in, conv_a_w=conv_a_w, w_a_out=w_a_out, ssd_conv_w=ssd_conv_w,
               ssd_conv_b=ssd_conv_b, dt_bias=dt_bias, a_log=a_log, d_skip=d_skip, ssd_norm_w=ssd_norm_w,
               w_s_out=w_s_out, w_o=w_o, norm_ffn_w=norm_ffn_w, w_up=w_up, ffn_conv_w=ffn_conv_w,
               ffn_conv_b=ffn_conv_b, w_down=w_down, final_norm_w=final_norm_w)
    mom1 = dict(norm_mix_w=m_norm_mix_w, w_in=m_w_in, conv_a_w=m_conv_a_w, w_a_out=m_w_a_out,
                ssd_conv_w=m_ssd_conv_w, ssd_conv_b=m_ssd_conv_b, dt_bias=m_dt_bias, a_log=m_a_log, d_skip=m_d_skip,
                ssd_norm_w=m_ssd_norm_w, w_s_out=m_w_s_out, w_o=m_w_o, norm_ffn_w=m_norm_ffn_w, w_up=m_w_up,
                ffn_conv_w=m_ffn_conv_w, ffn_conv_b=m_ffn_conv_b, w_down=m_w_down, final_norm_w=m_final_norm_w)
    mom2 = dict(norm_mix_w=v_norm_mix_w, w_in=v_w_in, conv_a_w=v_conv_a_w, w_a_out=v_w_a_out,
                ssd_conv_w=v_ssd_conv_w, ssd_conv_b=v_ssd_conv_b, dt_bias=v_dt_bias, a_log=v_a_log, d_skip=v_d_skip,
                ssd_norm_w=v_ssd_norm_w, w_s_out=v_w_s_out, w_o=v_w_o, norm_ffn_w=v_norm_ffn_w, w_up=v_w_up,
                ffn_conv_w=v_ffn_conv_w, ffn_conv_b=v_ffn_conv_b, w_down=v_w_down, final_norm_w=v_final_norm_w)

    t, d = x.shape[1], x.shape[2]
    di = 2 * d
    nh = di // HEAD_DIM
    dxw = di + 2 * N_GROUPS * D_STATE
    f = w_down.shape[1] * N_DEV
    n_in = w_in.shape[2] * N_DEV
    me = 4 * lax.axis_index("x") + 2 * lax.axis_index("y") + lax.axis_index("c")

    rest_local = [_as_rows(k, wts[k]).astype(BF16) for k in REST]
    nrows = [a.shape[0] for a in rest_local]
    n_blk = w_in.shape[2]
    in_local = w_in[0].T.astype(BF16)
    conv_shapes = [wts[k].shape[1:] for k in CONVS]
    conv_local = _pack([wts[k] for k in CONVS], d, 8)
    x2, tgt = x[0], loss_target[0]
    h_in = _xchg_start([in_local, conv_local], [False, False], x2, "gather_in_start", moves=TO_CHIPS)
    u = _rms_fwd(x2, norm_mix_w, h_in["token"], "norm_mix")
    part = _xchg_wait(h_in, u, "gather_in_wait")
    h_fwd = _xchg_start([], [False, False], u, "gather_in_forward_start", moves=TO_SIBLING, lands=part)
    in_all, conv_all = _xchg_wait(h_fwd, u, "gather_in_forward_wait")
    win_t = in_all.reshape(n_in, d)
    h_rest = _xchg_start(rest_local, [False] * len(REST), in_all, "gather_rest_start")
    c_a, c_s, c_f = _unpack(conv_all, conv_shapes, N_DEV)
    caw, scw, fcw = _blocks_to_cols(c_a), _blocks_to_cols(c_s), _blocks_to_cols(c_f)

    o_z, o_x, o_dt = 5 * d, 7 * d, 7 * d + dxw
    seg_bounds = [0, d, 2 * d, 3 * d, 4 * d, o_z, o_x, o_dt]
    w_dt = _pad_rows(win_t[o_dt:], DT_LANES)
    dtb, alog = (_pad_lanes(p[...].reshape(1, nh), DT_LANES) for p in (dt_bias, a_log))
    dskx = jnp.repeat(d_skip.reshape(1, nh), HEAD_DIM, axis=1)

    tok = h_rest["token"]
    gates = _mm([(u, 0, d, win_t, 0)], "nt", BF16, "proj_gates", n=2 * d, tm=2048, after=tok)
    pa = _mm([(u, 0, d, win_t, 2 * d)], "nt", BF16, "proj_a", n=3 * d, tm=2048, after=tok)
    z = _mm([(u, 0, d, win_t, o_z)], "nt", BF16, "proj_z", n=2 * d, tm=2048, after=tok)
    xbc = _mm([(u, 0, d, win_t, o_x)], "nt", BF16, "proj_xbc", n=dxw, tm=2048, after=tok)
    dtr = _mm([(u, w_dt)], "nt", F32, "proj_dt", after=tok)
    ya_in, q_a = _conv_a_fwd(pa, caw, d, "conv_a")
    xc, pre_s = _conv_s_fwd(xbc, scw, ssd_conv_b, "conv_s")
    y, states = _ssd_fwd(xc, dtr, dtb, alog, dskx, di, "ssd")
    yn = _gnorm_fwd(y, z, ssd_norm_w, "gnorm")
    rest_all = _xchg_wait(h_rest, yn, "gather_rest_wait")
    waout, wsout, wo, wup_t, wdown = (a.reshape(N_DEV * n, d) for a, n in zip(rest_all, nrows))
    y_a = _mm([(ya_in, waout)], "nn", BF16, "a_out")
    y_s = _mm([(yn, wsout)], "nn", BF16, "s_out")
    merged = _merge_fwd(gates, y_a, y_s, d, "merge")
    mo = _mm([(merged, wo)], "nn", BF16, "o_proj")
    h1, v = _resnorm_fwd(x2, mo, norm_ffn_w, "norm_ffn")
    hv = _mm([(v, wup_t)], "nt", BF16, "up_proj", tm=512, resident_b=True)
    act, c1 = _ffn_fwd(hv, fcw, ffn_conv_b, f, "ffn_act")
    dd = _mm([(act, wdown)], "nn", BF16, "down_proj")
    loss11, dh2, dh2b, g_fnw = _final(h1, dd, tgt, final_norm_w.reshape(1, d), "final")

    dact = _mm([(dh2b, wdown)], "nt", BF16, "d_act", resident_b=True)
    gw_down = _mm_tn(act, dh2b, "gw_down")
    dh1f, dh3, g_ffn = _ffn_bwd(hv, c1, dact, fcw, f, "ffn_act_bwd")
    dv = _mm([(dh1f, 0, f, wup_t, 0), (dh3, 0, f, wup_t, f)], "nn", BF16, "d_v")
    gw_up_t = jnp.concatenate([_mm_tn(dh1f, v, "gw_up1"), _mm_tn(dh3, v, "gw_up3")], axis=0)
    dh1, dh1b, g_nfw = _rms_bwd(h1, dv, norm_ffn_w, dh2, "norm_ffn_bwd")
    dmerged = _mm([(dh1b, wo)], "nt", BF16, "d_merged")
    gw_o = _mm_tn(merged, dh1b, "gw_o")
    dya, dys, dga, dgs = _merge_bwd(dmerged, gates, y_a, y_s, d, "merge_bwd")
    dyain = _mm([(dya, waout)], "nt", BF16, "d_ya_in")
    gw_aout = _mm_tn(ya_in, dya, "gw_a_out")
    db, dc, dvv, g_caw = _conv_a_bwd(pa, q_a, dyain, caw, d, "conv_a_bwd")
    dyn = _mm([(dys, wsout)], "nt", BF16, "d_yn", resident_b=True)
    gw_sout = _mm_tn(yn, dys, "gw_s_out")
    grads_rest = dict(w_a_out=gw_aout, w_s_out=gw_sout, w_o=gw_o, w_up=gw_up_t, w_down=gw_down)
    rest_parts = [grads_rest[k].reshape(N_DEV, n, d) for k, n in zip(REST, nrows)]
    h_grest = _xchg_start(rest_parts, [True] * len(REST), rest_parts[0], "scatter_rest_start")
    dy, dz, g_snw = _gnorm_bwd(y, z, dyn, ssd_norm_w, "gnorm_bwd")
    dtb_after = dtb + h_grest["token"][0:1, 0:1]
    dxc, ddtr, g_ssd = _ssd_bwd(xc, dtr, dy, states, dtb_after, alog, dskx, di, "ssd_bwd")
    dxbc, g_scw = _conv_s_bwd(xbc, pre_s, dxc, scw, "conv_s_bwd")
    dsegs = [dga, dgs, db, dc, dvv, dz, dxbc, ddtr.astype(BF16)]
    pairs = [(s, c, d, win_t, a + c * d) for s, a in zip(dsegs[:-1], seg_bounds) for c in range(s.shape[1] // d)]
    pairs.append((dsegs[-1], w_dt))
    gw_in = [_mm_tn(s, u, "gw_in%d" % i) for i, s in enumerate(dsegs)]
    gw_in_t = jnp.concatenate(gw_in[:-1] + [gw_in[-1][:nh]], axis=0)
    in_parts = gw_in_t.reshape(N_DEV, n_blk, d)
    h_gin = _xchg_start([in_parts], [True], in_parts, "scatter_in_start")
    du = _mm(pairs, "nn", BF16, "d_u", tm=512, tn=1024, after=h_gin["token"], resident_b=True)
    dx, _, g_nmw = _rms_bwd(x2, du, norm_mix_w, dh1, "norm_mix_bwd")

    small_grads = dict(norm_mix_w=g_nmw[0], ssd_conv_b=g_scw[4], dt_bias=g_ssd[0, :nh], a_log=g_ssd[1, :nh],
                       d_skip=g_ssd[2, :nh], ssd_norm_w=g_snw[0], norm_ffn_w=g_nfw[0], ffn_conv_b=g_ffn[3],
                       final_norm_w=g_fnw[0], conv_a_w=g_caw[:3], ssd_conv_w=g_scw[:4], ffn_conv_w=g_ffn[:3])
    small_names = REPL + CONVS
    small_parts = _pack([small_grads[k] for k in small_names] + [loss11], d, 8)
    h_small = _xchg_start([small_parts], [False], small_parts, "gather_small_start")
    rest_recv = _xchg_wait(h_grest, dx, "scatter_rest_wait")
    (in_recv,) = _xchg_wait(h_gin, rest_recv[0], "scatter_in_wait")
    (small_all,) = _xchg_wait(h_small, in_recv, "gather_small_wait")
    small_sum = _sum_parts(small_all, "sum_small_grads")
    *small_list, loss = _unpack(small_sum, [small_grads[k].shape for k in small_names] + [()])
    small_g = dict(zip(small_names, small_list))

    res = {}

    def update(k, parts):
        outs = _adamw(parts, *(_as_rows(k, src[k]) for src in (wts, mom1, mom2)), "adamw_" + k)
        for kind, a in zip(("g", "d", "m", "v"), outs):
            res[kind, k] = (a.T if k in TRANSPOSED else a)[None]

    update("w_in", in_recv)
    for k, parts in zip(REST, rest_recv):
        update(k, parts)
    local_g = {}
    for k in REPL:
        local_g[k] = small_g[k].reshape(wts[k].shape)
    for k in CONVS:
        n = wts[k].shape[2]
        local_g[k] = lax.dynamic_slice_in_dim(small_g[k], me * n, n, axis=1)[None]
    for k in small_names:
        as2d = lambda a: a.reshape(-1, a.shape[-1])
        outs = _adamw(as2d(local_g[k])[None], *(as2d(src[k]) for src in (wts, mom1, mom2)), "adamw_" + k)
        for kind, a in zip(("g", "d", "m", "v"), outs):
            res[kind, k] = a.reshape(wts[k].shape)

    return (loss, dx[None], *[res["g", k] for k in ORDER], *[res["d", k] for k in ORDER],
            *[res["m", k] for k in ORDER], *[res["v", k] for k in ORDER])
```

```python
import functools
import math

import jax
import jax.numpy as jnp
from jax import lax
from jax.experimental import pallas as pl
from jax.experimental.pallas import tpu as pltpu

F32 = jnp.float32
BF16 = jnp.bfloat16
EPS = 1e-5
HEAD_DIM = 64
N_GROUPS = 4
D_STATE = 128
CHUNK = 128
DT_LANES = 128
HALO = 16
STRIP = 16
SMALL_PARAM = 16 * 1024
N_DEV = 8
V7X_VMEM_LIMIT = 56 * 1024 * 1024
ADAM_LR, ADAM_B1, ADAM_B2, ADAM_EPS, ADAM_WD, ADAM_STEP = 0.001, 0.9, 0.999, 1e-08, 0.01, 10
HIGHEST = lax.Precision.HIGHEST
MESH = pl.DeviceIdType.MESH


def _pc(body, **kw):
    return pl.pallas_call(body, **kw)


def _params():
    return pltpu.CompilerParams(vmem_limit_bytes=V7X_VMEM_LIMIT)


def _pick(n, cands):
    for c in cands:
        if n % c == 0:
            return c
    return n


def _dot(a, b, ca, cb, prec=None):
    return lax.dot_general(a, b, (((ca,), (cb,)), ((), ())), preferred_element_type=F32, precision=prec)


def _sigmoid(x):
    return 0.5 * jnp.tanh(0.5 * x) + 0.5


def _sds(shape, dtype):
    return jax.ShapeDtypeStruct(shape, dtype)


def _mm(pairs, mode, out_dtype, name, n=None, tm=1024, tn=1024, after=None, resident_b=False):
    pairs = [p if len(p) == 5 else (p[0], 0, p[0].shape[1], p[1], 0) for p in pairs]
    m = pairs[0][0].shape[0]
    if n is None:
        n = pairs[0][3].shape[1] if mode == "nn" else pairs[0][3].shape[0]
    tm = min(tm, m)
    rows_nt = [p[4] for p in pairs] if mode == "nt" else []
    tn = next(c for c in ((n,) if resident_b else ()) + (tn, 1408, 512, 256, 128)
              if n % c == 0 and all(r % c == 0 for r in rows_nt))
    npair = len(pairs)
    cb = 0 if mode == "nn" else 1

    def body(*refs):
        o_ref = refs[-1]
        acc = None
        for p in range(npair):
            part = _dot(refs[2 * p][...], refs[2 * p + 1][...], 1, cb)
            acc = part if acc is None else acc + part
        o_ref[...] = acc.astype(o_ref.dtype)

    in_specs, args = [], []
    for a, a_col, kk, b, b_row in pairs:
        in_specs.append(pl.BlockSpec((tm, kk), lambda i, j, c=a_col: (i, c)))
        if mode == "nn":
            assert b_row % kk == 0 and (not resident_b or n == tn)
            in_specs.append(pl.BlockSpec((kk, tn), lambda i, j, r=b_row // kk: (r, j),
                                         pipeline_mode=pl.Buffered(1) if resident_b else None))
        else:
            in_specs.append(pl.BlockSpec((tn, kk), lambda i, j, r=b_row // tn: (r + j, 0),
                                         pipeline_mode=pl.Buffered(1) if resident_b and tn == n else None))
        args += [a, b]
    if after is not None:
        in_specs.append(pl.BlockSpec(memory_space=pl.ANY))
        args.append(after)
    return _pc(body, name=name, grid=(m // tm, n // tn), in_specs=in_specs,
               out_specs=pl.BlockSpec((tm, tn), lambda i, j: (i, j)),
               out_shape=_sds((m, n), out_dtype), compiler_params=_params())(*args)


def _mm_tn(a, b, name, tm=2048):
    m, ka = a.shape
    nb = b.shape[1]
    tm = min(tm, m)
    nm = m // tm
    tk = _pick(ka, (1024, 1408, 512, 256, 128))
    tn = _pick(nb, (1024, 512, 256, 128))

    def body(a_ref, b_ref, o_ref, acc):
        t = pl.program_id(2)

        @pl.when(t == 0)
        def _():
            acc[...] = jnp.zeros_like(acc)
        acc[...] += _dot(a_ref[...], b_ref[...], 0, 0)

        @pl.when(t == nm - 1)
        def _():
            o_ref[...] = acc[...].astype(o_ref.dtype)

    return _pc(body, name=name, grid=(ka // tk, nb // tn, nm),
               in_specs=[pl.BlockSpec((tm, tk), lambda i, j, t: (t, i)),
                         pl.BlockSpec((tm, tn), lambda i, j, t: (t, j))],
               out_specs=pl.BlockSpec((tk, tn), lambda i, j, t: (i, j)),
               out_shape=_sds((ka, nb), BF16), scratch_shapes=[pltpu.VMEM((tk, tn), F32)],
               compiler_params=_params())(a, b)


def _strips(tm, strip=STRIP):
    return [slice(r * strip, (r + 1) * strip) for r in range(tm // strip)]


def _fold8(a):
    out = a[0:8, :]
    for r in range(8, a.shape[0], 8):
        out = out + a[r:r + 8, :]
    return out


def _colsum(a8):
    return jnp.sum(a8, axis=0, keepdims=True)


def _rms_fwd(x, w, after, name):
    t, d = x.shape
    tm = min(1024, t)

    def body(x_ref, w_ref, after_ref, o_ref):
        wv = w_ref[...]
        for rows in _strips(tm):
            xv = x_ref[rows, :]
            r = lax.rsqrt(jnp.mean(xv * xv, axis=-1, keepdims=True) + EPS)
            o_ref[rows, :] = (xv * r * wv).astype(o_ref.dtype)

    return _pc(body, name=name, grid=(t // tm,),
               in_specs=[pl.BlockSpec((tm, d), lambda i: (i, 0)), pl.BlockSpec((1, d), lambda i: (0, 0)),
                         pl.BlockSpec(memory_space=pl.ANY)],
               out_specs=pl.BlockSpec((tm, d), lambda i: (i, 0)),
               out_shape=_sds((t, d), BF16), compiler_params=_params())(x, w, after)


def _resnorm_fwd(x, mo, w, name):
    t, d = x.shape
    tm = min(1024, t)

    def body(x_ref, mo_ref, w_ref, h_ref, v_ref):
        wv = w_ref[...]
        for rows in _strips(tm):
            h = x_ref[rows, :] + mo_ref[rows, :].astype(F32)
            r = lax.rsqrt(jnp.mean(h * h, axis=-1, keepdims=True) + EPS)
            h_ref[rows, :] = h
            v_ref[rows, :] = (h * r * wv).astype(v_ref.dtype)

    row = pl.BlockSpec((tm, d), lambda i: (i, 0))
    return _pc(body, name=name, grid=(t // tm,),
               in_specs=[row, row, pl.BlockSpec((1, d), lambda i: (0, 0))],
               out_specs=[row, row], out_shape=[_sds((t, d), F32), _sds((t, d), BF16)],
               compiler_params=_params())(x, mo, w)


def _rms_bwd(h, dy, w, dres, name):
    t, d = h.shape
    tm = min(1024, t)

    def body(h_ref, dy_ref, w_ref, dres_ref, dx_ref, dxb_ref, dw_ref):
        @pl.when(pl.program_id(0) == 0)
        def _():
            dw_ref[...] = jnp.zeros_like(dw_ref)
        wv = w_ref[...]
        acc = jnp.zeros((8, d), F32)
        for rows in _strips(tm):
            hv = h_ref[rows, :]
            dyv = dy_ref[rows, :].astype(F32)
            r = lax.rsqrt(jnp.mean(hv * hv, axis=-1, keepdims=True) + EPS)
            n = hv * r
            dn = dyv * wv
            acc = acc + _fold8(dyv * n)
            dx = dres_ref[rows, :] + r * (dn - n * jnp.mean(dn * n, axis=-1, keepdims=True))
            dx_ref[rows, :] = dx
            dxb_ref[rows, :] = dx.astype(BF16)
        dw_ref[0:1, :] += _colsum(acc)

    row = pl.BlockSpec((tm, d), lambda i: (i, 0))
    return _pc(body, name=name, grid=(t // tm,),
               in_specs=[row, row, pl.BlockSpec((1, d), lambda i: (0, 0)), row],
               out_specs=[row, row, pl.BlockSpec((8, d), lambda i: (0, 0))],
               out_shape=[_sds((t, d), F32), _sds((t, d), BF16), _sds((8, d), F32)],
               compiler_params=_params())(h, dy, w, dres)


def _final(h1, dd, tgt, w, name):
    t, d = h1.shape
    tm = min(1024, t)
    nt = t // tm

    def body(h1_ref, dd_ref, tgt_ref, w_ref, loss_ref, dh_ref, dhb_ref, dw_ref, acc):
        i = pl.program_id(0)

        @pl.when(i == 0)
        def _():
            dw_ref[...] = jnp.zeros_like(dw_ref)
            acc[...] = jnp.zeros_like(acc)
        wv = w_ref[...]
        sq = jnp.zeros((8, d), F32)
        dw = jnp.zeros((8, d), F32)
        for rows in _strips(tm):
            h = h1_ref[rows, :] + dd_ref[rows, :].astype(F32)
            r = lax.rsqrt(jnp.mean(h * h, axis=-1, keepdims=True) + EPS)
            n = h * r
            e = n * wv - tgt_ref[rows, :]
            sq = sq + _fold8(e * e)
            dout = e * (1.0 / d)
            dn = dout * wv
            dw = dw + _fold8(dout * n)
            dh = r * (dn - n * jnp.mean(dn * n, axis=-1, keepdims=True))
            dh_ref[rows, :] = dh
            dhb_ref[rows, :] = dh.astype(BF16)
        acc[...] += _colsum(sq)
        dw_ref[0:1, :] += _colsum(dw)

        @pl.when(i == nt - 1)
        def _():
            loss_ref[...] = jnp.sum(acc[...], axis=-1, keepdims=True) * (0.5 / d)

    row = pl.BlockSpec((tm, d), lambda i: (i, 0))
    return _pc(body, name=name, grid=(nt,),
               in_specs=[row, row, row, pl.BlockSpec((1, d), lambda i: (0, 0))],
               out_specs=[pl.BlockSpec((1, 1), lambda i: (0, 0)), row, row, pl.BlockSpec((8, d), lambda i: (0, 0))],
               out_shape=[_sds((1, 1), F32), _sds((t, d), F32), _sds((t, d), BF16), _sds((8, d), F32)],
               scratch_shapes=[pltpu.VMEM((1, d), F32)], compiler_params=_params())(h1, dd, tgt, w)


def _tile_specs(t, tm, tc, col0):
    th = tm // HALO
    last = t // HALO - 1
    cur = pl.BlockSpec((tm, tc), lambda j, i: (i, col0 + j))
    prev = pl.BlockSpec((HALO, tc), lambda j, i: (jnp.maximum(i * th - 1, 0), col0 + j))
    nxt = pl.BlockSpec((HALO, tc), lambda j, i: (jnp.minimum((i + 1) * th, last), col0 + j))
    return cur, prev, nxt


def _conv_strip(buf, w, k, rows):
    out = None
    for j in range(k):
        term = w[j:j + 1, :] * buf[pl.ds(HALO - (k - 1) + j + rows.start, STRIP), :]
        out = term if out is None else out + term
    return out


def _conv_backward(dbuf, x_strip, emit, w, acc_ref, k, tm, with_bias):
    tc = dbuf.shape[1]
    accs = [jnp.zeros((8, tc), F32) for _ in range(k + int(with_bias))]
    for rows in _strips(tm):
        xs = x_strip(rows)
        dx = None
        for j in range(k):
            ds = dbuf[pl.ds(rows.start + k - 1 - j, STRIP), :]
            term = w[j:j + 1, :] * ds
            dx = term if dx is None else dx + term
            accs[j] = accs[j] + _fold8(ds * xs)
            if with_bias and j == k - 1:
                accs[k] = accs[k] + _fold8(ds)
        emit(rows, dx)
    for j, a in enumerate(accs):
        acc_ref[j:j + 1, :] += _colsum(a)


def _conv_a_fwd(pa, w, d, name):
    t = pa.shape[0]
    tm, tc = min(1024, t), _pick(d, (512, 256, 128))
    nd = d // tc

    def body(b_ref, c_ref, v_ref, cp_ref, vp_ref, w_ref, o_ref, q_ref, buf):
        keep = (pl.program_id(1) > 0).astype(F32)
        buf[0:HALO, :] = cp_ref[...].astype(F32) * vp_ref[...].astype(F32) * keep
        for rows in _strips(tm):
            buf[HALO + rows.start:HALO + rows.stop, :] = c_ref[rows, :].astype(F32) * v_ref[rows, :].astype(F32)
        wv = w_ref[...]
        for rows in _strips(tm):
            q = _conv_strip(buf, wv, 3, rows)
            q_ref[rows, :] = q.astype(q_ref.dtype)
            o_ref[rows, :] = (b_ref[rows, :].astype(F32) * q).astype(o_ref.dtype)

    b_cur, _, _ = _tile_specs(t, tm, tc, 0)
    c_cur, c_prev, _ = _tile_specs(t, tm, tc, nd)
    v_cur, v_prev, _ = _tile_specs(t, tm, tc, 2 * nd)
    return _pc(body, name=name, grid=(nd, t // tm),
               in_specs=[b_cur, c_cur, v_cur, c_prev, v_prev, pl.BlockSpec((3, tc), lambda j, i: (0, j))],
               out_specs=[pl.BlockSpec((tm, tc), lambda j, i: (i, j))] * 2,
               out_shape=[_sds((t, d), BF16)] * 2,
               scratch_shapes=[pltpu.VMEM((tm + HALO, tc), F32)],
               compiler_params=_params())(pa, pa, pa, pa, pa, w)


def _conv_a_bwd(pa, q, dya, w, d, name):
    t = pa.shape[0]
    tm, tc = min(1024, t), _pick(d, (512, 256, 128))
    nd, nt = d // tc, t // tm

    def body(b_ref, c_ref, v_ref, bn_ref, q_ref, g_ref, gn_ref, w_ref, db_ref, dc_ref, dv_ref, acc_ref, dbuf):
        i = pl.program_id(1)

        @pl.when(i == 0)
        def _():
            acc_ref[...] = jnp.zeros_like(acc_ref)
        for rows in _strips(tm):
            g = g_ref[rows, :].astype(F32)
            dbuf[rows, :] = g * b_ref[rows, :].astype(F32)
            db_ref[rows, :] = (g * q_ref[rows, :].astype(F32)).astype(BF16)
        dbuf[tm:tm + HALO, :] = gn_ref[...].astype(F32) * bn_ref[...].astype(F32) * (i < nt - 1).astype(F32)

        def emit(rows, dp):
            dc_ref[rows, :] = (dp * v_ref[rows, :].astype(F32)).astype(BF16)
            dv_ref[rows, :] = (dp * c_ref[rows, :].astype(F32)).astype(BF16)

        _conv_backward(dbuf, lambda rows: c_ref[rows, :].astype(F32) * v_ref[rows, :].astype(F32), emit,
                       w_ref[...], acc_ref, 3, tm, False)

    b_cur, _, b_next = _tile_specs(t, tm, tc, 0)
    c_cur, _, _ = _tile_specs(t, tm, tc, nd)
    v_cur, _, _ = _tile_specs(t, tm, tc, 2 * nd)
    g_cur, _, g_next = _tile_specs(t, tm, tc, 0)
    out = pl.BlockSpec((tm, tc), lambda j, i: (i, j))
    return _pc(body, name=name, grid=(nd, nt),
               in_specs=[b_cur, c_cur, v_cur, b_next, g_cur, g_cur, g_next,
                         pl.BlockSpec((3, tc), lambda j, i: (0, j))],
               out_specs=[out, out, out, pl.BlockSpec((8, tc), lambda j, i: (0, j))],
               out_shape=[_sds((t, d), BF16)] * 3 + [_sds((8, d), F32)],
               scratch_shapes=[pltpu.VMEM((tm + HALO, tc), F32)],
               compiler_params=_params())(pa, pa, pa, pa, q, dya, dya, w)


def _conv_s_fwd(xbc, w, b, name):
    t, dx = xbc.shape
    tm, tc = min(1024, t), _pick(dx, (512, 256, 128))

    def body(x_ref, xp_ref, w_ref, b_ref, o_ref, pre_ref, buf):
        buf[0:HALO, :] = xp_ref[...].astype(F32) * (pl.program_id(1) > 0).astype(F32)
        for rows in _strips(tm):
            buf[HALO + rows.start:HALO + rows.stop, :] = x_ref[rows, :].astype(F32)
        wv, bv = w_ref[...], b_ref[...]
        for rows in _strips(tm):
            pre = _conv_strip(buf, wv, 4, rows) + bv
            pre_ref[rows, :] = pre.astype(pre_ref.dtype)
            o_ref[rows, :] = (pre * _sigmoid(pre)).astype(o_ref.dtype)

    cur, prev, _ = _tile_specs(t, tm, tc, 0)
    return _pc(body, name=name, grid=(dx // tc, t // tm),
               in_specs=[cur, prev, pl.BlockSpec((4, tc), lambda j, i: (0, j)),
                         pl.BlockSpec((1, tc), lambda j, i: (0, j))],
               out_specs=[pl.BlockSpec((tm, tc), lambda j, i: (i, j))] * 2,
               out_shape=[_sds((t, dx), BF16)] * 2,
               scratch_shapes=[pltpu.VMEM((tm + HALO, tc), F32)],
               compiler_params=_params())(xbc, xbc, w, b)


def _dsilu(pre):
    s = _sigmoid(pre)
    return s * (1.0 + pre * (1.0 - s))


def _conv_s_bwd(xbc, pre, dxc, w, name):
    t, dx = xbc.shape
    tm, tc = min(1024, t), _pick(dx, (512, 256, 128))
    nt = t // tm

    def body(x_ref, p_ref, pn_ref, g_ref, gn_ref, w_ref, dx_ref, acc_ref, dbuf):
        i = pl.program_id(1)

        @pl.when(i == 0)
        def _():
            acc_ref[...] = jnp.zeros_like(acc_ref)
        for rows in _strips(tm):
            dbuf[rows, :] = g_ref[rows, :].astype(F32) * _dsilu(p_ref[rows, :].astype(F32))
        dbuf[tm:tm + HALO, :] = (gn_ref[...].astype(F32) * _dsilu(pn_ref[...].astype(F32))
                                 * (i < nt - 1).astype(F32))

        def emit(rows, d_in):
            dx_ref[rows, :] = d_in.astype(BF16)

        _conv_backward(dbuf, lambda rows: x_ref[rows, :].astype(F32), emit, w_ref[...], acc_ref, 4, tm, True)

    cur, _, nxt = _tile_specs(t, tm, tc, 0)
    return _pc(body, name=name, grid=(dx // tc, nt),
               in_specs=[cur, cur, nxt, cur, nxt, pl.BlockSpec((4, tc), lambda j, i: (0, j))],
               out_specs=[pl.BlockSpec((tm, tc), lambda j, i: (i, j)), pl.BlockSpec((8, tc), lambda j, i: (0, j))],
               out_shape=[_sds((t, dx), BF16), _sds((8, dx), F32)],
               scratch_shapes=[pltpu.VMEM((tm + HALO, tc), F32)],
               compiler_params=_params())(xbc, pre, pre, dxc, dxc, w)


def _ffn_fwd(hv, w, b, f, name):
    t = hv.shape[0]
    tm, tc = min(1024, t), _pick(f, (512, 256, 128))
    nf = f // tc

    def body(h1_ref, h1p_ref, h3_ref, w_ref, b_ref, o_ref, c1_ref, buf):
        buf[0:HALO, :] = h1p_ref[...].astype(F32) * (pl.program_id(1) > 0).astype(F32)
        for rows in _strips(tm):
            buf[HALO + rows.start:HALO + rows.stop, :] = h1_ref[rows, :].astype(F32)
        wv, bv = w_ref[...], b_ref[...]
        for rows in _strips(tm):
            c1 = _conv_strip(buf, wv, 3, rows) + bv
            c1_ref[rows, :] = c1.astype(c1_ref.dtype)
            o_ref[rows, :] = (c1 * _sigmoid(c1) * h3_ref[rows, :].astype(F32)).astype(o_ref.dtype)

    h1_cur, h1_prev, _ = _tile_specs(t, tm, tc, 0)
    h3_cur, _, _ = _tile_specs(t, tm, tc, nf)
    return _pc(body, name=name, grid=(nf, t // tm),
               in_specs=[h1_cur, h1_prev, h3_cur, pl.BlockSpec((3, tc), lambda j, i: (0, j)),
                         pl.BlockSpec((1, tc), lambda j, i: (0, j))],
               out_specs=[pl.BlockSpec((tm, tc), lambda j, i: (i, j))] * 2,
               out_shape=[_sds((t, f), BF16)] * 2,
               scratch_shapes=[pltpu.VMEM((tm + HALO, tc), F32)],
               compiler_params=_params())(hv, hv, hv, w, b)


def _ffn_bwd(hv, c1, dact, w, f, name):
    t = hv.shape[0]
    tm, tc = min(1024, t), _pick(f, (512, 256, 128))
    nf, nt = f // tc, t // tm

    def body(h1_ref, h3_ref, h3n_ref, c_ref, cn_ref, g_ref, gn_ref, w_ref, dh1_ref, dh3_ref, acc_ref, dbuf):
        i = pl.program_id(1)

        @pl.when(i == 0)
        def _():
            acc_ref[...] = jnp.zeros_like(acc_ref)
        for rows in _strips(tm):
            c1v, g = c_ref[rows, :].astype(F32), g_ref[rows, :].astype(F32)
            s1 = _sigmoid(c1v)
            dh3_ref[rows, :] = (g * c1v * s1).astype(BF16)
            dbuf[rows, :] = g * h3_ref[rows, :].astype(F32) * s1 * (1.0 + c1v * (1.0 - s1))
        dbuf[tm:tm + HALO, :] = (gn_ref[...].astype(F32) * h3n_ref[...].astype(F32)
                                 * _dsilu(cn_ref[...].astype(F32)) * (i < nt - 1).astype(F32))

        def emit(rows, d_in):
            dh1_ref[rows, :] = d_in.astype(BF16)

        _conv_backward(dbuf, lambda rows: h1_ref[rows, :].astype(F32), emit, w_ref[...], acc_ref, 3, tm, True)

    h1_cur, _, _ = _tile_specs(t, tm, tc, 0)
    h3_cur, _, h3_next = _tile_specs(t, tm, tc, nf)
    g_cur, _, g_next = _tile_specs(t, tm, tc, 0)
    out = pl.BlockSpec((tm, tc), lambda j, i: (i, j))
    return _pc(body, name=name, grid=(nf, nt),
               in_specs=[h1_cur, h3_cur, h3_next, g_cur, g_next, g_cur, g_next,
                         pl.BlockSpec((3, tc), lambda j, i: (0, j))],
               out_specs=[out, out, pl.BlockSpec((8, tc), lambda j, i: (0, j))],
               out_shape=[_sds((t, f), BF16), _sds((t, f), BF16), _sds((8, f), F32)],
               scratch_shapes=[pltpu.VMEM((tm + HALO, tc), F32)],
               compiler_params=_params())(hv, hv, hv, c1, c1, dact, dact, w)


def _gnorm_fwd(y, z, w, name):
    t, di = y.shape
    gw = di // N_GROUPS
    tm = min(1024, t)

    def body(y_ref, z_ref, w_ref, o_ref):
        wv = w_ref[...]
        for rows in _strips(tm):
            zv = z_ref[rows, :].astype(F32)
            yz = y_ref[rows, :].astype(F32) * zv * _sigmoid(zv)
            r = lax.rsqrt(jnp.mean(yz * yz, axis=-1, keepdims=True) + EPS)
            o_ref[rows, :] = (yz * r * wv).astype(o_ref.dtype)

    blk = pl.BlockSpec((tm, gw), lambda j, i: (i, j))
    return _pc(body, name=name, grid=(N_GROUPS, t // tm),
               in_specs=[blk, blk, pl.BlockSpec((1, gw), lambda j, i: (0, j))],
               out_specs=blk, out_shape=_sds((t, di), BF16), compiler_params=_params())(y, z, w)


def _gnorm_bwd(y, z, dyn, w, name):
    t, di = y.shape
    gw = di // N_GROUPS
    tm = min(1024, t)

    def body(y_ref, z_ref, g_ref, w_ref, dy_ref, dz_ref, dw_ref):
        @pl.when(pl.program_id(1) == 0)
        def _():
            dw_ref[...] = jnp.zeros_like(dw_ref)
        wv = w_ref[...]
        acc = jnp.zeros((8, gw), F32)
        for rows in _strips(tm):
            yv, zv, g = y_ref[rows, :].astype(F32), z_ref[rows, :].astype(F32), g_ref[rows, :].astype(F32)
            s = _sigmoid(zv)
            sz = zv * s
            yz = yv * sz
            r = lax.rsqrt(jnp.mean(yz * yz, axis=-1, keepdims=True) + EPS)
            n = yz * r
            dn = g * wv
            acc = acc + _fold8(g * n)
            dyz = r * (dn - n * jnp.mean(dn * n, axis=-1, keepdims=True))
            dy_ref[rows, :] = (dyz * sz).astype(BF16)
            dz_ref[rows, :] = (dyz * yv * s * (1.0 + zv * (1.0 - s))).astype(BF16)
        dw_ref[0:1, :] += _colsum(acc)

    blk = pl.BlockSpec((tm, gw), lambda j, i: (i, j))
    return _pc(body, name=name, grid=(N_GROUPS, t // tm),
               in_specs=[blk, blk, blk, pl.BlockSpec((1, gw), lambda j, i: (0, j))],
               out_specs=[blk, blk, pl.BlockSpec((8, gw), lambda j, i: (0, j))],
               out_shape=[_sds((t, di), BF16), _sds((t, di), BF16), _sds((8, di), F32)],
               compiler_params=_params())(y, z, dyn, w)


def _merge_fwd(gates, ya, ys, d, name):
    t = ya.shape[0]
    tm, tc = min(1024, t), _pick(d, (512, 256, 128))
    nd = d // tc

    def body(ga_ref, gs_ref, ya_ref, ys_ref, o_ref):
        for rows in _strips(tm):
            o_ref[rows, :] = (_sigmoid(ga_ref[rows, :].astype(F32)) * ya_ref[rows, :].astype(F32)
                              + _sigmoid(gs_ref[rows, :].astype(F32)) * ys_ref[rows, :].astype(F32)
                              ).astype(o_ref.dtype)

    blk = pl.BlockSpec((tm, tc), lambda j, i: (i, j))
    return _pc(body, name=name, grid=(nd, t // tm),
               in_specs=[blk, pl.BlockSpec((tm, tc), lambda j, i: (i, nd + j)), blk, blk],
               out_specs=blk, out_shape=_sds((t, d), BF16), compiler_params=_params())(gates, gates, ya, ys)


def _merge_bwd(dm, gates, ya, ys, d, name):
    t = ya.shape[0]
    tm, tc = min(1024, t), _pick(d, (512, 256, 128))
    nd = d // tc

    def body(dm_ref, ga_ref, gs_ref, ya_ref, ys_ref, dya_ref, dys_ref, dga_ref, dgs_ref):
        for rows in _strips(tm):
            g = dm_ref[rows, :].astype(F32)
            sa, ss = _sigmoid(ga_ref[rows, :].astype(F32)), _sigmoid(gs_ref[rows, :].astype(F32))
            dya_ref[rows, :] = (g * sa).astype(BF16)
            dys_ref[rows, :] = (g * ss).astype(BF16)
            dga_ref[rows, :] = (g * ya_ref[rows, :].astype(F32) * sa * (1.0 - sa)).astype(BF16)
            dgs_ref[rows, :] = (g * ys_ref[rows, :].astype(F32) * ss * (1.0 - ss)).astype(BF16)

    blk = pl.BlockSpec((tm, tc), lambda j, i: (i, j))
    return _pc(body, name=name, grid=(nd, t // tm),
               in_specs=[blk, blk, pl.BlockSpec((tm, tc), lambda j, i: (i, nd + j)), blk, blk],
               out_specs=[blk] * 4, out_shape=[_sds((t, d), BF16)] * 4,
               compiler_params=_params())(dm, gates, gates, ya, ys)


def _ssd_chunk_terms(dtr, dtb, alog):
    xx = dtr + dtb
    dt = jnp.maximum(xx, 0.0) + jnp.log(1.0 + jnp.exp(-jnp.abs(xx)))
    a = -jnp.exp(alog)
    li = lax.broadcasted_iota(jnp.int32, (CHUNK, CHUNK), 0)
    si = lax.broadcasted_iota(jnp.int32, (CHUNK, CHUNK), 1)
    causal = li >= si
    acum = _dot(causal.astype(F32), dt * a, 1, 0, HIGHEST)
    return xx, dt, a, acum, acum.T, causal


def _split2(x):
    hi = x.astype(BF16)
    return hi, (x - hi.astype(F32)).astype(BF16)


def _expand(v, e, exact=True):
    hi, lo = _split2(v)
    out = _dot(hi, e, 1, 0)
    return out + _dot(lo, e, 1, 0) if exact else out


def _segsum(s, e):
    hi, lo = _split2(s)
    return _dot(hi, e, 1, 1) + _dot(lo, e, 1, 1)


def _head_maps(di):
    nh = di // HEAD_DIM
    h = jnp.arange(DT_LANES)[:, None]
    e64 = (jnp.arange(di)[None, :] // HEAD_DIM == h).astype(BF16)
    e128 = (jnp.arange(nh * CHUNK)[None, :] // CHUNK == h).astype(BF16)
    return e64, e128


def _pair_blockdiag(p, left):
    zero = jnp.zeros_like(p)
    return jnp.concatenate([jnp.where(left, p, zero), jnp.where(left, zero, p)], axis=0)


def _ssd_fwd(xc, dtr, dtb, alog, dskx, di, name):
    t = xc.shape[0]
    dx = xc.shape[1]
    nc = t // CHUNK
    nh = di // HEAD_DIM
    hpg = nh // N_GROUPS
    gw = hpg * HEAD_DIM
    boff, coff = di, di + N_GROUPS * D_STATE
    e64, e128 = _head_maps(di)

    def body(xc_ref, dtr_ref, dtb_ref, alog_ref, dsk_ref, e64_ref, e128_ref, y_ref, st_ref, state):
        @pl.when(pl.program_id(0) == 0)
        def _():
            state[...] = jnp.zeros_like(state)
        _, dt, _, acum, acum_t, causal = _ssd_chunk_terms(dtr_ref[...], dtb_ref[...], alog_ref[...])
        last = acum[CHUNK - 1:CHUNK, :]
        e64v = e64_ref[...]
        dtx = _expand(dt, e64v, False)
        eax = _expand(jnp.exp(acum), e64v)
        dex = _expand(dt * jnp.exp(last - acum), e64v, False)
        acx = _expand(acum, e128_ref[...])
        st_ref[0] = state[...]
        left = lax.broadcasted_iota(jnp.int32, (CHUNK, 2 * HEAD_DIM), 1) < HEAD_DIM
        for g in range(N_GROUPS):
            gs = slice(g * gw, (g + 1) * gw)
            bg = xc_ref[:, boff + g * D_STATE:boff + (g + 1) * D_STATE]
            cg = xc_ref[:, coff + g * D_STATE:coff + (g + 1) * D_STATE]
            gm = _dot(cg, bg, 1, 1)
            xg = xc_ref[:, gs].astype(F32)
            xdb = (xg * dtx[:, gs]).astype(BF16)
            sin = state[:, gs]
            yo = _dot(cg, sin.astype(BF16), 1, 0) * eax[:, gs]
            for jp in range(hpg // 2):
                h0 = g * hpg + 2 * jp
                ps = slice(jp * 2 * HEAD_DIM, (jp + 1) * 2 * HEAD_DIM)
                ms = []
                for hh in (h0, h0 + 1):
                    seg = acx[:, hh * CHUNK:(hh + 1) * CHUNK] - acum_t[hh:hh + 1, :]
                    ms.append((gm * jnp.exp(jnp.where(causal, seg, -1e30))).astype(BF16))
                yd = _dot(jnp.concatenate(ms, axis=1), _pair_blockdiag(xdb[:, ps], left), 1, 0)
                col = slice(g * gw + jp * 2 * HEAD_DIM, g * gw + (jp + 1) * 2 * HEAD_DIM)
                y_ref[:, col] = (yd + yo[:, ps] + dsk_ref[:, col] * xg[:, ps]).astype(y_ref.dtype)
            xe = (xg * dex[:, gs]).astype(BF16)
            state[:, gs] = eax[CHUNK - 1:CHUNK, gs] * sin + _dot(bg, xe, 0, 0)

    small = pl.BlockSpec((1, DT_LANES), lambda c: (0, 0))
    whole = lambda a: pl.BlockSpec(a.shape, lambda c: (0, 0))
    return _pc(body, name=name, grid=(nc,),
               in_specs=[pl.BlockSpec((CHUNK, dx), lambda c: (c, 0)),
                         pl.BlockSpec((CHUNK, DT_LANES), lambda c: (c, 0)), small, small,
                         whole(dskx), whole(e64), whole(e128)],
               out_specs=[pl.BlockSpec((CHUNK, di), lambda c: (c, 0)),
                          pl.BlockSpec((1, D_STATE, di), lambda c: (c, 0, 0))],
               out_shape=[_sds((t, di), BF16), _sds((nc, D_STATE, di), F32)],
               scratch_shapes=[pltpu.VMEM((D_STATE, di), F32)],
               compiler_params=_params())(xc, dtr, dtb, alog, dskx, e64, e128)


def _ssd_bwd(xc, dtr, dy, states, dtb, alog, dskx, di, name):
    t = xc.shape[0]
    dx = xc.shape[1]
    nc = t // CHUNK
    nh = di // HEAD_DIM
    hpg = nh // N_GROUPS
    gw = hpg * HEAD_DIM
    boff, coff = di, di + N_GROUPS * D_STATE
    e64, e128 = _head_maps(di)

    def body(xc_ref, dtr_ref, dy_ref, st_ref, dtb_ref, alog_ref, dsk_ref, e64_ref, e128_ref,
             dxc_ref, ddtr_ref, sm_ref, dstate, darow):
        @pl.when(pl.program_id(0) == 0)
        def _():
            dstate[...] = jnp.zeros_like(dstate)
            sm_ref[...] = jnp.zeros_like(sm_ref)
        darow[...] = jnp.zeros_like(darow)
        xx, dt, a, acum, acum_t, causal = _ssd_chunk_terms(dtr_ref[...], dtb_ref[...], alog_ref[...])
        last = acum[CHUNK - 1:CHUNK, :]
        e64v = e64_ref[...]
        dtx = _expand(dt, e64v, False)
        eax = _expand(jnp.exp(acum), e64v)
        eex = _expand(jnp.exp(last - acum), e64v, False)
        acx = _expand(acum, e128_ref[...])
        left = lax.broadcasted_iota(jnp.int32, (CHUNK, 2 * HEAD_DIM), 1) < HEAD_DIM
        lane = lax.broadcasted_iota(jnp.int32, (CHUNK, DT_LANES), 1)
        sub8 = lax.broadcasted_iota(jnp.int32, (8, gw), 0)
        da_col = jnp.zeros((CHUNK, DT_LANES), F32)
        ddt_col = jnp.zeros((CHUNK, DT_LANES), F32)
        rows = jnp.zeros((8, DT_LANES), F32)
        for g in range(N_GROUPS):
            gs = slice(g * gw, (g + 1) * gw)
            bg = xc_ref[:, boff + g * D_STATE:boff + (g + 1) * D_STATE]
            cg = xc_ref[:, coff + g * D_STATE:coff + (g + 1) * D_STATE]
            gm = _dot(cg, bg, 1, 1)
            e64g = e64v[:, gs]
            xg = xc_ref[:, gs].astype(F32)
            dtg, eag, eeg = dtx[:, gs], eax[:, gs], eex[:, gs]
            xd = xg * dtg
            xdb = xd.astype(BF16)
            dyb = dy_ref[:, gs]
            dyf = dyb.astype(F32)
            sin = st_ref[0, :, gs]
            sinb = sin.astype(BF16)
            ds = dstate[:, gs]
            dsb = ds.astype(BF16)
            bds = _dot(bg, dsb, 1, 0)
            dyeb = (dyf * eag).astype(BF16)
            dcg = _dot(dyeb, sinb, 1, 1)
            dstate[:, gs] = eag[CHUNK - 1:CHUNK, :] * ds + _dot(cg, dyeb, 0, 0)
            yo = _dot(cg, sinb, 1, 0) * eag
            xe = xd * eeg
            dbg = _dot(xe.astype(BF16), dsb, 1, 1)
            wterm = bds * xe
            da_col = da_col + _segsum(dyf * yo - wterm, e64g)
            dg = jnp.zeros((CHUNK, CHUNK), F32)
            dxd_parts = []
            for jp in range(hpg // 2):
                h0 = g * hpg + 2 * jp
                ps = slice(jp * 2 * HEAD_DIM, (jp + 1) * 2 * HEAD_DIM)
                lms, mfs = [], []
                for hh in (h0, h0 + 1):
                    seg = acx[:, hh * CHUNK:(hh + 1) * CHUNK] - acum_t[hh:hh + 1, :]
                    lm = jnp.exp(jnp.where(causal, seg, -1e30))
                    lms.append(lm)
                    mfs.append(gm * lm)
                mstack = jnp.concatenate([m.astype(BF16) for m in mfs], axis=0)
                dyp = dyb[:, ps]
                dxd_parts.append(_dot(mstack, _pair_blockdiag(dyp, left), 0, 0))
                dm2 = _dot(dyp, _pair_blockdiag(xdb[:, ps], left), 1, 1)
                for k, hh in enumerate((h0, h0 + 1)):
                    dm = dm2[:, k * CHUNK:(k + 1) * CHUNK]
                    dg = dg + dm * lms[k]
                    q = dm * mfs[k]
                    da_col = da_col + jnp.where(lane == hh, jnp.sum(q, axis=1, keepdims=True), 0.0)
                    darow[hh:hh + 1, :] = -jnp.sum(q, axis=0, keepdims=True)
            dxd = jnp.concatenate(dxd_parts, axis=1) + bds * eeg
            ddt_col = ddt_col + _segsum(dxd * xg, e64g)
            rsum = (jnp.where(sub8 == 0, jnp.sum(wterm, axis=0, keepdims=True), 0.0)
                    + jnp.where(sub8 == 1, jnp.sum(ds * sin, axis=0, keepdims=True), 0.0)
                    + jnp.where(sub8 == 2, jnp.sum(dyf * xg, axis=0, keepdims=True), 0.0))
            rows = rows + _segsum(rsum, e64g)
            dxc_ref[:, gs] = (dxd * dtg + dsk_ref[:, gs] * dyf).astype(dxc_ref.dtype)
            dgb = dg.astype(BF16)
            dxc_ref[:, boff + g * D_STATE:boff + (g + 1) * D_STATE] = (
                dbg + _dot(dgb, cg, 0, 0)).astype(dxc_ref.dtype)
            dxc_ref[:, coff + g * D_STATE:coff + (g + 1) * D_STATE] = (
                dcg + _dot(dgb, bg, 1, 0)).astype(dxc_ref.dtype)
        at_last = rows[0:1, :] + jnp.exp(last) * rows[1:2, :]
        is_last = lax.broadcasted_iota(jnp.int32, (CHUNK, DT_LANES), 0) == CHUNK - 1
        da = da_col + jnp.where(is_last, at_last, 0.0) + darow[...].T
        li = lax.broadcasted_iota(jnp.int32, (CHUNK, CHUNK), 0)
        si = lax.broadcasted_iota(jnp.int32, (CHUNK, CHUNK), 1)
        dla = _dot((si >= li).astype(F32), da, 1, 0, HIGHEST)
        ddtr = (ddt_col + dla * a) * _sigmoid(xx)
        ddtr_ref[...] = ddtr
        sm_ref[0:1, :] += jnp.sum(ddtr, axis=0, keepdims=True)
        sm_ref[1:2, :] += jnp.sum(dla * dt, axis=0, keepdims=True) * a
        sm_ref[2:3, :] += rows[2:3, :]

    small = pl.BlockSpec((1, DT_LANES), lambda c: (0, 0))
    whole = lambda a: pl.BlockSpec(a.shape, lambda c: (0, 0))
    rev = lambda c: (nc - 1 - c, 0)
    return _pc(body, name=name, grid=(nc,),
               in_specs=[pl.BlockSpec((CHUNK, dx), rev), pl.BlockSpec((CHUNK, DT_LANES), rev),
                         pl.BlockSpec((CHUNK, di), rev),
                         pl.BlockSpec((1, D_STATE, di), lambda c: (nc - 1 - c, 0, 0)), small, small,
                         whole(dskx), whole(e64), whole(e128)],
               out_specs=[pl.BlockSpec((CHUNK, dx), rev), pl.BlockSpec((CHUNK, DT_LANES), rev),
                          pl.BlockSpec((8, DT_LANES), lambda c: (0, 0))],
               out_shape=[_sds((t, dx), BF16), _sds((t, DT_LANES), F32), _sds((8, DT_LANES), F32)],
               scratch_shapes=[pltpu.VMEM((D_STATE, di), F32), pltpu.VMEM((DT_LANES, CHUNK), F32)],
               compiler_params=_params())(xc, dtr, dy, states, dtb, alog, dskx, e64, e128)


def _adamw(parts, w, m, v, name):
    npart, rows, width = parts.shape
    if rows * width <= SMALL_PARAM:
        tr, tw = rows, width
    else:
        tr, tw = (_pick(rows, (64, 32, 16, 8)), width) if rows % 8 == 0 else (rows, 128)
    c1 = 1.0 - ADAM_B1 ** ADAM_STEP
    c2 = 1.0 - ADAM_B2 ** ADAM_STEP

    row_strips = _strips(tr) if tr % STRIP == 0 else [slice(0, tr)]
    col_chunks = [slice(c, c + 512) for c in range(0, tw, 512)] if tw % 512 == 0 else [slice(0, tw)]

    def body(p_ref, w_ref, m_ref, v_ref, g_ref, d_ref, nm_ref, nv_ref):
        for rows in row_strips:
            for cols in col_chunks:
                g = p_ref[0, rows, cols].astype(F32)
                for p in range(1, npart):
                    g = g + p_ref[p, rows, cols].astype(F32)
                nm = ADAM_B1 * m_ref[rows, cols] + (1.0 - ADAM_B1) * g
                nv = ADAM_B2 * v_ref[rows, cols] + (1.0 - ADAM_B2) * (g * g)
                g_ref[rows, cols] = g
                nm_ref[rows, cols] = nm
                nv_ref[rows, cols] = nv
                d_ref[rows, cols] = -ADAM_LR * ((nm / c1) / (jnp.sqrt(nv / c2) + ADAM_EPS)
                                                + ADAM_WD * w_ref[rows, cols])

    blk = pl.BlockSpec((tr, tw), lambda i, j: (i, j))
    return _pc(body, name=name, grid=(rows // tr, width // tw),
               in_specs=[pl.BlockSpec((npart, tr, tw), lambda i, j: (0, i, j)), blk, blk, blk],
               out_specs=[blk] * 4, out_shape=[_sds((rows, width), F32)] * 4,
               compiler_params=_params())(parts, w, m, v)


def _sum_parts(parts, name, tile=None):
    npart, rows, width = parts.shape
    tile = rows if tile is None else tile

    def body(p_ref, o_ref):
        for rows_ in _strips(tile, 8 if parts.dtype == F32 else STRIP):
            g = p_ref[0, rows_, :].astype(F32)
            for p in range(1, npart):
                g = g + p_ref[p, rows_, :].astype(F32)
            o_ref[rows_, :] = g

    return _pc(body, name=name, grid=(rows // tile,),
               in_specs=[pl.BlockSpec((npart, tile, width), lambda i: (0, i, 0))],
               out_specs=pl.BlockSpec((tile, width), lambda i: (i, 0)),
               out_shape=_sds((rows, width), F32), compiler_params=_params())(parts)


def _flip(k):
    x, y, c = lax.axis_index("x"), lax.axis_index("y"), lax.axis_index("c")
    px = 1 - x if k & 4 else x
    py = 1 - y if k & 2 else y
    pc = 1 - c if k & 1 else c
    return (px, py, pc), 4 * px + 2 * py + pc


DIRECT = tuple((k, 0) for k in range(1, N_DEV))
TO_CHIPS = ((1, 0), (2, 0), (4, 0), (6, 0))
TO_SIBLING = ((1, 2), (1, 4), (1, 6))


def _copies(arrays, lands, send_sems, recv_sems, scatter, moves):
    _, me = _flip(0)
    outgoing, incoming = [], []
    for i, (kd, kb) in enumerate(moves):
        peer, pidx = _flip(kd)
        _, out_slot = _flip(kb)
        _, in_slot = _flip(kd ^ kb)
        for j, land_ref in enumerate(lands):
            if kb:
                src = land_ref.at[out_slot]
            else:
                src = arrays[j].at[pidx] if scatter[j] else arrays[j]
            sem = len(lands) * i + j
            for dst, bucket in ((land_ref.at[out_slot], outgoing), (land_ref.at[in_slot], incoming)):
                bucket.append(pltpu.make_async_remote_copy(
                    src_ref=src, dst_ref=dst, send_sem=send_sems.at[sem], recv_sem=recv_sems.at[sem],
                    device_id=peer, device_id_type=MESH))
    return outgoing, incoming


HBM_SPEC = pl.BlockSpec(memory_space=pltpu.HBM)
SEM_SPEC = pl.BlockSpec(memory_space=pltpu.SEMAPHORE)
ANY_SPEC = pl.BlockSpec(memory_space=pl.ANY)
EFFECT = pltpu.SideEffectType.DATAFLOW_SIDE_EFFECTING


def _landing_zones(arrays, scatter):
    _, me = _flip(0)
    lands = []
    for a, sc in zip(arrays, scatter):
        own = lax.dynamic_index_in_dim(a, me, 0, keepdims=True) if sc else a[None]
        shape = a.shape if sc else (N_DEV,) + a.shape
        lands.append(lax.dynamic_update_slice(lax.empty(shape, a.dtype), own, (me,) + (0,) * (len(shape) - 1)))
    return lands


def _xchg_start(arrays, scatter, after, name, moves=DIRECT, lands=None):
    if lands is None:
        lands = _landing_zones(arrays, scatter)
    na, nl = len(arrays), len(lands)

    def body(*refs):
        ins, outs = refs[:na + nl], refs[na + nl + 1:]
        outgoing, _ = _copies(ins[:na], ins[na:], outs[0], outs[1], scatter, moves)
        for cp in outgoing:
            cp.start()
        outs[-1][...] = jnp.zeros_like(outs[-1])

    nsem = nl * len(moves)
    operands = [pltpu.with_memory_space_constraint(a, pltpu.HBM) for a in list(arrays) + list(lands)]
    out = _pc(body, name=name,
              out_shape=(pltpu.SemaphoreType.DMA((nsem,)), pltpu.SemaphoreType.DMA((nsem,)),
                         *[pltpu.HBM(a.shape, a.dtype) for a in operands], _sds((8, 128), F32)),
              in_specs=[HBM_SPEC] * (na + nl) + [ANY_SPEC],
              out_specs=(SEM_SPEC, SEM_SPEC, *[HBM_SPEC] * (na + nl), pl.BlockSpec(memory_space=pltpu.VMEM)),
              input_output_aliases={i: 2 + i for i in range(na + nl)},
              compiler_params=pltpu.CompilerParams(has_side_effects=EFFECT))(*operands, after)
    return dict(sems=out[:2], thru=out[2:2 + na + nl], token=out[-1], scatter=scatter, na=na, moves=moves)


def _xchg_wait(handle, after, name):
    na, scatter, moves, thru = handle["na"], handle["scatter"], handle["moves"], handle["thru"]
    n = len(thru)

    def body(*refs):
        ins = refs[:n]
        outgoing, incoming = _copies(ins[:na], ins[na:], refs[n], refs[n + 1], scatter, moves)
        for cp in outgoing:
            cp.wait_send()
        for cp in incoming:
            cp.wait_recv()

    out = _pc(body, name=name, out_shape=tuple(pltpu.HBM(a.shape, a.dtype) for a in thru),
              in_specs=[HBM_SPEC] * n + [SEM_SPEC, SEM_SPEC, ANY_SPEC], out_specs=tuple([HBM_SPEC] * n),
              input_output_aliases={i: i for i in range(n)},
              compiler_params=pltpu.CompilerParams(has_side_effects=EFFECT))(*thru, *handle["sems"], after)
    return out[na:]


def _pack(arrs, width, row_mult):
    flat = jnp.concatenate([a.reshape(-1) for a in arrs])
    n = flat.shape[0]
    rows = -(-n // (width * row_mult)) * row_mult
    return jnp.pad(flat, (0, rows * width - n)).reshape(rows, width)


def _unpack(packed, shapes, lead=None):
    out, off = [], 0
    flat = packed.reshape(-1) if lead is None else packed.reshape(lead, -1)
    for s in shapes:
        n = math.prod(s)
        if lead is None:
            out.append(flat[off:off + n].reshape(s))
        else:
            out.append(flat[:, off:off + n].reshape((lead,) + tuple(s)))
        off += n
    return out


def _blocks_to_cols(blocks):
    nb, rows, n = blocks.shape
    return blocks.transpose(1, 0, 2).reshape(rows, nb * n)


def _pad_rows(a, rows):
    return jnp.pad(a, ((0, rows - a.shape[0]), (0, 0)))


def _pad_lanes(a, lanes):
    return jnp.pad(a, ((0, 0), (0, lanes - a.shape[1])))


REST = ("w_a_out", "w_s_out", "w_o", "w_up", "w_down")
TRANSPOSED = ("w_up", "w_in")
CONVS = ("conv_a_w", "ssd_conv_w", "ffn_conv_w")
REPL = ("norm_mix_w", "ssd_conv_b", "dt_bias", "a_log", "d_skip", "ssd_norm_w", "norm_ffn_w", "ffn_conv_b",
        "final_norm_w")
ORDER = ("norm_mix_w", "w_in", "conv_a_w", "w_a_out", "ssd_conv_w", "ssd_conv_b", "dt_bias", "a_log", "d_skip",
         "ssd_norm_w", "w_s_out", "w_o", "norm_ffn_w", "w_up", "ffn_conv_w", "ffn_conv_b", "w_down", "final_norm_w")


def _as_rows(name, block):
    return block[0].T if name in TRANSPOSED else block[0]


def kernel(x, norm_mix_w, w_in, conv_a_w, w_a_out, ssd_conv_w, ssd_conv_b, dt_bias, a_log, d_skip, ssd_norm_w, w_s_out, w_o, norm_ffn_w, w_up, ffn_conv_w, ffn_conv_b, w_down, final_norm_w, loss_target, m_norm_mix_w, m_w_in, m_conv_a_w, m_w_a_out, m_ssd_conv_w, m_ssd_conv_b, m_dt_bias, m_a_log, m_d_skip, m_ssd_norm_w, m_w_s_out, m_w_o, m_norm_ffn_w, m_w_up, m_ffn_conv_w, m_ffn_conv_b, m_w_down, m_final_norm_w, v_norm_mix_w, v_w_in, v_conv_a_w, v_w_a_out, v_ssd_conv_w, v_ssd_conv_b, v_dt_bias, v_a_log, v_d_skip, v_ssd_norm_w, v_w_s_out, v_w_o, v_norm_ffn_w, v_w_up, v_ffn_conv_w, v_ffn_conv_b, v_w_down, v_final_norm_w):
    wts = dict(norm_mix_w=norm_mix_w, w_in=w_in, conv_a_w=conv_a_w, w_a_out=w_a_out, ssd_conv_w=ssd_conv_w,
               ssd_conv_b=ssd_conv_b, dt_bias=dt_bias, a_log=a_log, d_skip=d_skip, ssd_norm_w=ssd_norm_w,
               w_s_out=w_s_out, w_o=w_o, norm_ffn_w=norm_ffn_w, w_up=w_up, ffn_conv_w=ffn_conv_w,
               ffn_conv_b=ffn_conv_b, w_down=w_down, final_norm_w=final_norm_w)
    mom1 = dict(norm_mix_w=m_norm_mix_w, w_in=m_w_in, conv_a_w=m_conv_a_w, w_a_out=m_w_a_out,
                ssd_conv_w=m_ssd_conv_w, ssd_conv_b=m_ssd_conv_b, dt_bias=m_dt_bias, a_log=m_a_log, d_skip=m_d_skip,
                ssd_norm_w=m_ssd_norm_w, w_s_out=m_w_s_out, w_o=m_w_o, norm_ffn_w=m_norm_ffn_w, w_up=m_w_up,
                ffn_conv_w=m_ffn_conv_w, ffn_conv_b=m_ffn_conv_b, w_down=m_w_down, final_norm_w=m_final_norm_w)
    mom2 = dict(norm_mix_w=v_norm_mix_w, w_in=v_w_in, conv_a_w=v_conv_a_w, w_a_out=v_w_a_out,
                ssd_conv_w=v_ssd_conv_w, ssd_conv_b=v_ssd_conv_b, dt_bias=v_dt_bias, a_log=v_a_log, d_skip=v_d_skip,
                ssd_norm_w=v_ssd_norm_w, w_s_out=v_w_s_out, w_o=v_w_o, norm_ffn_w=v_norm_ffn_w, w_up=v_w_up,
                ffn_conv_w=v_ffn_conv_w, ffn_conv_b=v_ffn_conv_b, w_down=v_w_down, final_norm_w=v_final_norm_w)

    t, d = x.shape[1], x.shape[2]
    di = 2 * d
    nh = di // HEAD_DIM
    dxw = di + 2 * N_GROUPS * D_STATE
    f = w_down.shape[1] * N_DEV
    n_in = w_in.shape[2] * N_DEV
    me = 4 * lax.axis_index("x") + 2 * lax.axis_index("y") + lax.axis_index("c")

    rest_local = [_as_rows(k, wts[k]).astype(BF16) for k in REST]
    nrows = [a.shape[0] for a in rest_local]
    n_blk = w_in.shape[2]
    in_local = w_in[0].T.astype(BF16)
    conv_shapes = [wts[k].shape[1:] for k in CONVS]
    conv_local = _pack([wts[k] for k in CONVS], d, 8)
    x2, tgt = x[0], loss_target[0]
    h_in = _xchg_start([in_local, conv_local], [False, False], x2, "gather_in_start", moves=TO_CHIPS)
    u = _rms_fwd(x2, norm_mix_w, h_in["token"], "norm_mix")
    part = _xchg_wait(h_in, u, "gather_in_wait")
    h_fwd = _xchg_start([], [False, False], u, "gather_in_forward_start", moves=TO_SIBLING, lands=part)
    in_all, conv_all = _xchg_wait(h_fwd, u, "gather_in_forward_wait")
    win_t = in_all.reshape(n_in, d)
    h_rest = _xchg_start(rest_local, [False] * len(REST), in_all, "gather_rest_start")
    c_a, c_s, c_f = _unpack(conv_all, conv_shapes, N_DEV)
    caw, scw, fcw = _blocks_to_cols(c_a), _blocks_to_cols(c_s), _blocks_to_cols(c_f)

    o_z, o_x, o_dt = 5 * d, 7 * d, 7 * d + dxw
    seg_bounds = [0, d, 2 * d, 3 * d, 4 * d, o_z, o_x, o_dt]
    w_dt = _pad_rows(win_t[o_dt:], DT_LANES)
    dtb, alog = (_pad_lanes(p[...].reshape(1, nh), DT_LANES) for p in (dt_bias, a_log))
    dskx = jnp.repeat(d_skip.reshape(1, nh), HEAD_DIM, axis=1)

    tok = h_rest["token"]
    gates = _mm([(u, 0, d, win_t, 0)], "nt", BF16, "proj_gates", n=2 * d, tm=2048, after=tok)
    pa = _mm([(u, 0, d, win_t, 2 * d)], "nt", BF16, "proj_a", n=3 * d, tm=2048, after=tok)
    z = _mm([(u, 0, d, win_t, o_z)], "nt", BF16, "proj_z", n=2 * d, tm=2048, after=tok)
    xbc = _mm([(u, 0, d, win_t, o_x)], "nt", BF16, "proj_xbc", n=dxw, tm=2048, after=tok)
    dtr = _mm([(u, w_dt)], "nt", F32, "proj_dt", after=tok)
    ya_in, q_a = _conv_a_fwd(pa, caw, d, "conv_a")
    xc, pre_s = _conv_s_fwd(xbc, scw, ssd_conv_b, "conv_s")
    y, states = _ssd_fwd(xc, dtr, dtb, alog, dskx, di, "ssd")
    yn = _gnorm_fwd(y, z, ssd_norm_w, "gnorm")
    rest_all = _xchg_wait(h_rest, yn, "gather_rest_wait")
    waout, wsout, wo, wup_t, wdown = (a.reshape(N_DEV * n, d) for a, n in zip(rest_all, nrows))
    y_a = _mm([(ya_in, waout)], "nn", BF16, "a_out", tm=2048)
    y_s = _mm([(yn, wsout)], "nn", BF16, "s_out", tm=2048)
    merged = _merge_fwd(gates, y_a, y_s, d, "merge")
    mo = _mm([(merged, wo)], "nn", BF16, "o_proj", tm=2048)
    h1, v = _resnorm_fwd(x2, mo, norm_ffn_w, "norm_ffn")
    hv = _mm([(v, wup_t)], "nt", BF16, "up_proj", tm=512, resident_b=True)
    act, c1 = _ffn_fwd(hv, fcw, ffn_conv_b, f, "ffn_act")
    dd = _mm([(act, wdown)], "nn", BF16, "down_proj")
    loss11, dh2, dh2b, g_fnw = _final(h1, dd, tgt, final_norm_w.reshape(1, d), "final")

    dact = _mm([(dh2b, wdown)], "nt", BF16, "d_act", resident_b=True)
    gw_down = _mm_tn(act, dh2b, "gw_down")
    dh1f, dh3, g_ffn = _ffn_bwd(hv, c1, dact, fcw, f, "ffn_act_bwd")
    dv = _mm([(dh1f, 0, f, wup_t, 0), (dh3, 0, f, wup_t, f)], "nn", BF16, "d_v")
    gw_up_t = jnp.concatenate([_mm_tn(dh1f, v, "gw_up1"), _mm_tn(dh3, v, "gw_up3")], axis=0)
    dh1, dh1b, g_nfw = _rms_bwd(h1, dv, norm_ffn_w, dh2, "norm_ffn_bwd")
    dmerged = _mm([(dh1b, wo)], "nt", BF16, "d_merged", tm=2048)
    gw_o = _mm_tn(merged, dh1b, "gw_o")
    dya, dys, dga, dgs = _merge_bwd(dmerged, gates, y_a, y_s, d, "merge_bwd")
    dyain = _mm([(dya, waout)], "nt", BF16, "d_ya_in", tm=2048)
    gw_aout = _mm_tn(ya_in, dya, "gw_a_out")
    db, dc, dvv, g_caw = _conv_a_bwd(pa, q_a, dyain, caw, d, "conv_a_bwd")
    dyn = _mm([(dys, wsout)], "nt", BF16, "d_yn", resident_b=True)
    gw_sout = _mm_tn(yn, dys, "gw_s_out")
    grads_rest = dict(w_a_out=gw_aout, w_s_out=gw_sout, w_o=gw_o, w_up=gw_up_t, w_down=gw_down)
    rest_parts = [grads_rest[k].reshape(N_DEV, n, d) for k, n in zip(REST, nrows)]
    h_grest = _xchg_start(rest_parts, [True] * len(REST), rest_parts[0], "scatter_rest_start")
    dy, dz, g_snw = _gnorm_bwd(y, z, dyn, ssd_norm_w, "gnorm_bwd")
    dtb_after = dtb + h_grest["token"][0:1, 0:1]
    dxc, ddtr, g_ssd = _ssd_bwd(xc, dtr, dy, states, dtb_after, alog, dskx, di, "ssd_bwd")
    dxbc, g_scw = _conv_s_bwd(xbc, pre_s, dxc, scw, "conv_s_bwd")
    dsegs = [dga, dgs, db, dc, dvv, dz, dxbc, ddtr.astype(BF16)]
    pairs = [(s, c, d, win_t, a + c * d) for s, a in zip(dsegs[:-1], seg_bounds) for c in range(s.shape[1] // d)]
    pairs.append((dsegs[-1], w_dt))
    gw_in = [_mm_tn(s, u, "gw_in%d" % i) for i, s in enumerate(dsegs)]
    gw_in_t = jnp.concatenate(gw_in[:-1] + [gw_in[-1][:nh]], axis=0)
    in_parts = gw_in_t.reshape(N_DEV, n_blk, d)
    h_gin = _xchg_start([in_parts], [True], in_parts, "scatter_in_start")
    du = _mm(pairs, "nn", BF16, "d_u", tm=512, tn=1024, after=h_gin["token"], resident_b=True)
    dx, _, g_nmw = _rms_bwd(x2, du, norm_mix_w, dh1, "norm_mix_bwd")

    small_grads = dict(norm_mix_w=g_nmw[0], ssd_conv_b=g_scw[4], dt_bias=g_ssd[0, :nh], a_log=g_ssd[1, :nh],
                       d_skip=g_ssd[2, :nh], ssd_norm_w=g_snw[0], norm_ffn_w=g_nfw[0], ffn_conv_b=g_ffn[3],
                       final_norm_w=g_fnw[0], conv_a_w=g_caw[:3], ssd_conv_w=g_scw[:4], ffn_conv_w=g_ffn[:3])
    small_names = REPL + CONVS
    small_parts = _pack([small_grads[k] for k in small_names] + [loss11], d, 8)
    h_small = _xchg_start([small_parts], [False], small_parts, "gather_small_start")
    rest_recv = _xchg_wait(h_grest, dx, "scatter_rest_wait")
    (in_recv,) = _xchg_wait(h_gin, rest_recv[0], "scatter_in_wait")
    (small_all,) = _xchg_wait(h_small, in_recv, "gather_small_wait")
    small_sum = _sum_parts(small_all, "sum_small_grads")
    *small_list, loss = _unpack(small_sum, [small_grads[k].shape for k in small_names] + [()])
    small_g = dict(zip(small_names, small_list))

    res = {}

    def update(k, parts):
        outs = _adamw(parts, *(_as_rows(k, src[k]) for src in (wts, mom1, mom2)), "adamw_" + k)
        for kind, a in zip(("g", "d", "m", "v"), outs):
            res[kind, k] = (a.T if k in TRANSPOSED else a)[None]

    update("w_in", in_recv)
    for k, parts in zip(REST, rest_recv):
        update(k, parts)
    local_g = {}
    for k in REPL:
        local_g[k] = small_g[k].reshape(wts[k].shape)
    for k in CONVS:
        n = wts[k].shape[2]
        local_g[k] = lax.dynamic_slice_in_dim(small_g[k], me * n, n, axis=1)[None]
    for k in small_names:
        as2d = lambda a: a.reshape(-1, a.shape[-1])
        outs = _adamw(as2d(local_g[k])[None], *(as2d(src[k]) for src in (wts, mom1, mom2)), "adamw_" + k)
        for kind, a in zip(("g", "d", "m", "v"), outs):
            res[kind, k] = a.reshape(wts[k].shape)

    return (loss, dx[None], *[res["g", k] for k in ORDER], *[res["d", k] for k in ORDER],
            *[res["m", k] for k in ORDER], *[res["v", k] for k in ORDER])
```

```python
import functools
import math

import jax
import jax.numpy as jnp
from jax import lax
from jax.experimental import pallas as pl
from jax.experimental.pallas import tpu as pltpu

F32 = jnp.float32
BF16 = jnp.bfloat16
EPS = 1e-5
HEAD_DIM = 64
N_GROUPS = 4
D_STATE = 128
CHUNK = 128
DT_LANES = 128
HALO = 16
STRIP = 16
SMALL_PARAM = 16 * 1024
N_DEV = 8
V7X_VMEM_LIMIT = 56 * 1024 * 1024
ADAM_LR, ADAM_B1, ADAM_B2, ADAM_EPS, ADAM_WD, ADAM_STEP = 0.001, 0.9, 0.999, 1e-08, 0.01, 10
HIGHEST = lax.Precision.HIGHEST
MESH = pl.DeviceIdType.MESH


def _pc(body, **kw):
    return pl.pallas_call(body, **kw)


def _params():
    return pltpu.CompilerParams(vmem_limit_bytes=V7X_VMEM_LIMIT)


def _pick(n, cands):
    for c in cands:
        if n % c == 0:
            return c
    return n


def _dot(a, b, ca, cb, prec=None):
    return lax.dot_general(a, b, (((ca,), (cb,)), ((), ())), preferred_element_type=F32, precision=prec)


def _sigmoid(x):
    return 0.5 * jnp.tanh(0.5 * x) + 0.5


def _sds(shape, dtype):
    return jax.ShapeDtypeStruct(shape, dtype)


def _mm(pairs, mode, out_dtype, name, n=None, tm=1024, tn=1024, after=None, resident_b=False):
    pairs = [p if len(p) == 5 else (p[0], 0, p[0].shape[1], p[1], 0) for p in pairs]
    m = pairs[0][0].shape[0]
    if n is None:
        n = pairs[0][3].shape[1] if mode == "nn" else pairs[0][3].shape[0]
    tm = min(tm, m)
    rows_nt = [p[4] for p in pairs] if mode == "nt" else []
    tn = next(c for c in ((n,) if resident_b else ()) + (tn, 1408, 512, 256, 128)
              if n % c == 0 and all(r % c == 0 for r in rows_nt))
    npair = len(pairs)
    cb = 0 if mode == "nn" else 1

    def body(*refs):
        o_ref = refs[-1]
        acc = None
        for p in range(npair):
            part = _dot(refs[2 * p][...], refs[2 * p + 1][...], 1, cb)
            acc = part if acc is None else acc + part
        o_ref[...] = acc.astype(o_ref.dtype)

    in_specs, args = [], []
    for a, a_col, kk, b, b_row in pairs:
        in_specs.append(pl.BlockSpec((tm, kk), lambda i, j, c=a_col: (i, c)))
        if mode == "nn":
            assert b_row % kk == 0 and (not resident_b or n == tn)
            in_specs.append(pl.BlockSpec((kk, tn), lambda i, j, r=b_row // kk: (r, j),
                                         pipeline_mode=pl.Buffered(1) if resident_b else None))
        else:
            in_specs.append(pl.BlockSpec((tn, kk), lambda i, j, r=b_row // tn: (r + j, 0),
                                         pipeline_mode=pl.Buffered(1) if resident_b and tn == n else None))
        args += [a, b]
    if after is not None:
        in_specs.append(pl.BlockSpec(memory_space=pl.ANY))
        args.append(after)
    return _pc(body, name=name, grid=(m // tm, n // tn), in_specs=in_specs,
               out_specs=pl.BlockSpec((tm, tn), lambda i, j: (i, j)),
               out_shape=_sds((m, n), out_dtype), compiler_params=_params())(*args)


def _mm_tn(a, b, name, tm=2048):
    m, ka = a.shape
    nb = b.shape[1]
    tm = min(tm, m)
    nm = m // tm
    tk = _pick(ka, (1024, 1408, 512, 256, 128))
    tn = _pick(nb, (1024, 512, 256, 128))

    def body(a_ref, b_ref, o_ref, acc):
        t = pl.program_id(2)

        @pl.when(t == 0)
        def _():
            acc[...] = jnp.zeros_like(acc)
        acc[...] += _dot(a_ref[...], b_ref[...], 0, 0)

        @pl.when(t == nm - 1)
        def _():
            o_ref[...] = acc[...].astype(o_ref.dtype)

    return _pc(body, name=name, grid=(ka // tk, nb // tn, nm),
               in_specs=[pl.BlockSpec((tm, tk), lambda i, j, t: (t, i)),
                         pl.BlockSpec((tm, tn), lambda i, j, t: (t, j))],
               out_specs=pl.BlockSpec((tk, tn), lambda i, j, t: (i, j)),
               out_shape=_sds((ka, nb), BF16), scratch_shapes=[pltpu.VMEM((tk, tn), F32)],
               compiler_params=_params())(a, b)


def _strips(tm, strip=STRIP):
    return [slice(r * strip, (r + 1) * strip) for r in range(tm // strip)]


def _fold8(a):
    out = a[0:8, :]
    for r in range(8, a.shape[0], 8):
        out = out + a[r:r + 8, :]
    return out


def _colsum(a8):
    return jnp.sum(a8, axis=0, keepdims=True)


def _rms_fwd(x, w, after, name):
    t, d = x.shape
    tm = min(1024, t)

    def body(x_ref, w_ref, after_ref, o_ref):
        wv = w_ref[...]
        for rows in _strips(tm):
            xv = x_ref[rows, :]
            r = lax.rsqrt(jnp.mean(xv * xv, axis=-1, keepdims=True) + EPS)
            o_ref[rows, :] = (xv * r * wv).astype(o_ref.dtype)

    return _pc(body, name=name, grid=(t // tm,),
               in_specs=[pl.BlockSpec((tm, d), lambda i: (i, 0)), pl.BlockSpec((1, d), lambda i: (0, 0)),
                         pl.BlockSpec(memory_space=pl.ANY)],
               out_specs=pl.BlockSpec((tm, d), lambda i: (i, 0)),
               out_shape=_sds((t, d), BF16), compiler_params=_params())(x, w, after)


def _resnorm_fwd(x, mo, w, name):
    t, d = x.shape
    tm = min(1024, t)

    def body(x_ref, mo_ref, w_ref, h_ref, v_ref):
        wv = w_ref[...]
        for rows in _strips(tm):
            h = x_ref[rows, :] + mo_ref[rows, :].astype(F32)
            r = lax.rsqrt(jnp.mean(h * h, axis=-1, keepdims=True) + EPS)
            h_ref[rows, :] = h
            v_ref[rows, :] = (h * r * wv).astype(v_ref.dtype)

    row = pl.BlockSpec((tm, d), lambda i: (i, 0))
    return _pc(body, name=name, grid=(t // tm,),
               in_specs=[row, row, pl.BlockSpec((1, d), lambda i: (0, 0))],
               out_specs=[row, row], out_shape=[_sds((t, d), F32), _sds((t, d), BF16)],
               compiler_params=_params())(x, mo, w)


def _rms_bwd(h, dy, w, dres, name):
    t, d = h.shape
    tm = min(1024, t)

    def body(h_ref, dy_ref, w_ref, dres_ref, dx_ref, dxb_ref, dw_ref):
        @pl.when(pl.program_id(0) == 0)
        def _():
            dw_ref[...] = jnp.zeros_like(dw_ref)
        wv = w_ref[...]
        acc = jnp.zeros((8, d), F32)
        for rows in _strips(tm):
            hv = h_ref[rows, :]
            dyv = dy_ref[rows, :].astype(F32)
            r = lax.rsqrt(jnp.mean(hv * hv, axis=-1, keepdims=True) + EPS)
            n = hv * r
            dn = dyv * wv
            acc = acc + _fold8(dyv * n)
            dx = dres_ref[rows, :] + r * (dn - n * jnp.mean(dn * n, axis=-1, keepdims=True))
            dx_ref[rows, :] = dx
            dxb_ref[rows, :] = dx.astype(BF16)
        dw_ref[0:1, :] += _colsum(acc)

    row = pl.BlockSpec((tm, d), lambda i: (i, 0))
    return _pc(body, name=name, grid=(t // tm,),
               in_specs=[row, row, pl.BlockSpec((1, d), lambda i: (0, 0)), row],
               out_specs=[row, row, pl.BlockSpec((8, d), lambda i: (0, 0))],
               out_shape=[_sds((t, d), F32), _sds((t, d), BF16), _sds((8, d), F32)],
               compiler_params=_params())(h, dy, w, dres)


def _final(h1, dd, tgt, w, name):
    t, d = h1.shape
    tm = min(1024, t)
    nt = t // tm

    def body(h1_ref, dd_ref, tgt_ref, w_ref, loss_ref, dh_ref, dhb_ref, dw_ref, acc):
        i = pl.program_id(0)

        @pl.when(i == 0)
        def _():
            dw_ref[...] = jnp.zeros_like(dw_ref)
            acc[...] = jnp.zeros_like(acc)
        wv = w_ref[...]
        sq = jnp.zeros((8, d), F32)
        dw = jnp.zeros((8, d), F32)
        for rows in _strips(tm):
            h = h1_ref[rows, :] + dd_ref[rows, :].astype(F32)
            r = lax.rsqrt(jnp.mean(h * h, axis=-1, keepdims=True) + EPS)
            n = h * r
            e = n * wv - tgt_ref[rows, :]
            sq = sq + _fold8(e * e)
            dout = e * (1.0 / d)
            dn = dout * wv
            dw = dw + _fold8(dout * n)
            dh = r * (dn - n * jnp.mean(dn * n, axis=-1, keepdims=True))
            dh_ref[rows, :] = dh
            dhb_ref[rows, :] = dh.astype(BF16)
        acc[...] += _colsum(sq)
        dw_ref[0:1, :] += _colsum(dw)

        @pl.when(i == nt - 1)
        def _():
            loss_ref[...] = jnp.sum(acc[...], axis=-1, keepdims=True) * (0.5 / d)

    row = pl.BlockSpec((tm, d), lambda i: (i, 0))
    return _pc(body, name=name, grid=(nt,),
               in_specs=[row, row, row, pl.BlockSpec((1, d), lambda i: (0, 0))],
               out_specs=[pl.BlockSpec((1, 1), lambda i: (0, 0)), row, row, pl.BlockSpec((8, d), lambda i: (0, 0))],
               out_shape=[_sds((1, 1), F32), _sds((t, d), F32), _sds((t, d), BF16), _sds((8, d), F32)],
               scratch_shapes=[pltpu.VMEM((1, d), F32)], compiler_params=_params())(h1, dd, tgt, w)


def _tile_specs(t, tm, tc, col0):
    th = tm // HALO
    last = t // HALO - 1
    cur = pl.BlockSpec((tm, tc), lambda j, i: (i, col0 + j))
    prev = pl.BlockSpec((HALO, tc), lambda j, i: (jnp.maximum(i * th - 1, 0), col0 + j))
    nxt = pl.BlockSpec((HALO, tc), lambda j, i: (jnp.minimum((i + 1) * th, last), col0 + j))
    return cur, prev, nxt


def _conv_strip(buf, w, k, rows):
    out = None
    for j in range(k):
        term = w[j:j + 1, :] * buf[pl.ds(HALO - (k - 1) + j + rows.start, STRIP), :]
        out = term if out is None else out + term
    return out


def _conv_backward(dbuf, x_strip, emit, w, acc_ref, k, tm, with_bias):
    tc = dbuf.shape[1]
    accs = [jnp.zeros((8, tc), F32) for _ in range(k + int(with_bias))]
    for rows in _strips(tm):
        xs = x_strip(rows)
        dx = None
        for j in range(k):
            ds = dbuf[pl.ds(rows.start + k - 1 - j, STRIP), :]
            term = w[j:j + 1, :] * ds
            dx = term if dx is None else dx + term
            accs[j] = accs[j] + _fold8(ds * xs)
            if with_bias and j == k - 1:
                accs[k] = accs[k] + _fold8(ds)
        emit(rows, dx)
    for j, a in enumerate(accs):
        acc_ref[j:j + 1, :] += _colsum(a)


def _conv_a_fwd(pa, w, d, name):
    t = pa.shape[0]
    tm, tc = min(2048, t), _pick(d, (512, 256, 128))
    nd = d // tc

    def body(b_ref, c_ref, v_ref, cp_ref, vp_ref, w_ref, o_ref, q_ref, buf):
        keep = (pl.program_id(1) > 0).astype(F32)
        buf[0:HALO, :] = cp_ref[...].astype(F32) * vp_ref[...].astype(F32) * keep
        for rows in _strips(tm):
            buf[HALO + rows.start:HALO + rows.stop, :] = c_ref[rows, :].astype(F32) * v_ref[rows, :].astype(F32)
        wv = w_ref[...]
        for rows in _strips(tm):
            q = _conv_strip(buf, wv, 3, rows)
            q_ref[rows, :] = q.astype(q_ref.dtype)
            o_ref[rows, :] = (b_ref[rows, :].astype(F32) * q).astype(o_ref.dtype)

    b_cur, _, _ = _tile_specs(t, tm, tc, 0)
    c_cur, c_prev, _ = _tile_specs(t, tm, tc, nd)
    v_cur, v_prev, _ = _tile_specs(t, tm, tc, 2 * nd)
    return _pc(body, name=name, grid=(nd, t // tm),
               in_specs=[b_cur, c_cur, v_cur, c_prev, v_prev, pl.BlockSpec((3, tc), lambda j, i: (0, j))],
               out_specs=[pl.BlockSpec((tm, tc), lambda j, i: (i, j))] * 2,
               out_shape=[_sds((t, d), BF16)] * 2,
               scratch_shapes=[pltpu.VMEM((tm + HALO, tc), F32)],
               compiler_params=_params())(pa, pa, pa, pa, pa, w)


def _conv_a_bwd(pa, q, dya, w, d, name):
    t = pa.shape[0]
    tm, tc = min(2048, t), _pick(d, (512, 256, 128))
    nd, nt = d // tc, t // tm

    def body(b_ref, c_ref, v_ref, bn_ref, q_ref, g_ref, gn_ref, w_ref, db_ref, dc_ref, dv_ref, acc_ref, dbuf):
        i = pl.program_id(1)

        @pl.when(i == 0)
        def _():
            acc_ref[...] = jnp.zeros_like(acc_ref)
        for rows in _strips(tm):
            g = g_ref[rows, :].astype(F32)
            dbuf[rows, :] = g * b_ref[rows, :].astype(F32)
            db_ref[rows, :] = (g * q_ref[rows, :].astype(F32)).astype(BF16)
        dbuf[tm:tm + HALO, :] = gn_ref[...].astype(F32) * bn_ref[...].astype(F32) * (i < nt - 1).astype(F32)

        def emit(rows, dp):
            dc_ref[rows, :] = (dp * v_ref[rows, :].astype(F32)).astype(BF16)
            dv_ref[rows, :] = (dp * c_ref[rows, :].astype(F32)).astype(BF16)

        _conv_backward(dbuf, lambda rows: c_ref[rows, :].astype(F32) * v_ref[rows, :].astype(F32), emit,
                       w_ref[...], acc_ref, 3, tm, False)

    b_cur, _, b_next = _tile_specs(t, tm, tc, 0)
    c_cur, _, _ = _tile_specs(t, tm, tc, nd)
    v_cur, _, _ = _tile_specs(t, tm, tc, 2 * nd)
    g_cur, _, g_next = _tile_specs(t, tm, tc, 0)
    out = pl.BlockSpec((tm, tc), lambda j, i: (i, j))
    return _pc(body, name=name, grid=(nd, nt),
               in_specs=[b_cur, c_cur, v_cur, b_next, g_cur, g_cur, g_next,
                         pl.BlockSpec((3, tc), lambda j, i: (0, j))],
               out_specs=[out, out, out, pl.BlockSpec((8, tc), lambda j, i: (0, j))],
               out_shape=[_sds((t, d), BF16)] * 3 + [_sds((8, d), F32)],
               scratch_shapes=[pltpu.VMEM((tm + HALO, tc), F32)],
               compiler_params=_params())(pa, pa, pa, pa, q, dya, dya, w)


def _conv_s_fwd(xbc, w, b, name):
    t, dx = xbc.shape
    tm, tc = min(2048, t), _pick(dx, (512, 256, 128))

    def body(x_ref, xp_ref, w_ref, b_ref, o_ref, pre_ref, buf):
        buf[0:HALO, :] = xp_ref[...].astype(F32) * (pl.program_id(1) > 0).astype(F32)
        for rows in _strips(tm):
            buf[HALO + rows.start:HALO + rows.stop, :] = x_ref[rows, :].astype(F32)
        wv, bv = w_ref[...], b_ref[...]
        for rows in _strips(tm):
            pre = _conv_strip(buf, wv, 4, rows) + bv
            pre_ref[rows, :] = pre.astype(pre_ref.dtype)
            o_ref[rows, :] = (pre * _sigmoid(pre)).astype(o_ref.dtype)

    cur, prev, _ = _tile_specs(t, tm, tc, 0)
    return _pc(body, name=name, grid=(dx // tc, t // tm),
               in_specs=[cur, prev, pl.BlockSpec((4, tc), lambda j, i: (0, j)),
                         pl.BlockSpec((1, tc), lambda j, i: (0, j))],
               out_specs=[pl.BlockSpec((tm, tc), lambda j, i: (i, j))] * 2,
               out_shape=[_sds((t, dx), BF16)] * 2,
               scratch_shapes=[pltpu.VMEM((tm + HALO, tc), F32)],
               compiler_params=_params())(xbc, xbc, w, b)


def _dsilu(pre):
    s = _sigmoid(pre)
    return s * (1.0 + pre * (1.0 - s))


def _conv_s_bwd(xbc, pre, dxc, w, name):
    t, dx = xbc.shape
    tm, tc = min(2048, t), _pick(dx, (512, 256, 128))
    nt = t // tm

    def body(x_ref, p_ref, pn_ref, g_ref, gn_ref, w_ref, dx_ref, acc_ref, dbuf):
        i = pl.program_id(1)

        @pl.when(i == 0)
        def _():
            acc_ref[...] = jnp.zeros_like(acc_ref)
        for rows in _strips(tm):
            dbuf[rows, :] = g_ref[rows, :].astype(F32) * _dsilu(p_ref[rows, :].astype(F32))
        dbuf[tm:tm + HALO, :] = (gn_ref[...].astype(F32) * _dsilu(pn_ref[...].astype(F32))
                                 * (i < nt - 1).astype(F32))

        def emit(rows, d_in):
            dx_ref[rows, :] = d_in.astype(BF16)

        _conv_backward(dbuf, lambda rows: x_ref[rows, :].astype(F32), emit, w_ref[...], acc_ref, 4, tm, True)

    cur, _, nxt = _tile_specs(t, tm, tc, 0)
    return _pc(body, name=name, grid=(dx // tc, nt),
               in_specs=[cur, cur, nxt, cur, nxt, pl.BlockSpec((4, tc), lambda j, i: (0, j))],
               out_specs=[pl.BlockSpec((tm, tc), lambda j, i: (i, j)), pl.BlockSpec((8, tc), lambda j, i: (0, j))],
               out_shape=[_sds((t, dx), BF16), _sds((8, dx), F32)],
               scratch_shapes=[pltpu.VMEM((tm + HALO, tc), F32)],
               compiler_params=_params())(xbc, pre, pre, dxc, dxc, w)


def _ffn_fwd(hv, w, b, f, name):
    t = hv.shape[0]
    tm, tc = min(2048, t), _pick(f, (512, 256, 128))
    nf = f // tc

    def body(h1_ref, h1p_ref, h3_ref, w_ref, b_ref, o_ref, c1_ref, buf):
        buf[0:HALO, :] = h1p_ref[...].astype(F32) * (pl.program_id(1) > 0).astype(F32)
        for rows in _strips(tm):
            buf[HALO + rows.start:HALO + rows.stop, :] = h1_ref[rows, :].astype(F32)
        wv, bv = w_ref[...], b_ref[...]
        for rows in _strips(tm):
            c1 = _conv_strip(buf, wv, 3, rows) + bv
            c1_ref[rows, :] = c1.astype(c1_ref.dtype)
            o_ref[rows, :] = (c1 * _sigmoid(c1) * h3_ref[rows, :].astype(F32)).astype(o_ref.dtype)

    h1_cur, h1_prev, _ = _tile_specs(t, tm, tc, 0)
    h3_cur, _, _ = _tile_specs(t, tm, tc, nf)
    return _pc(body, name=name, grid=(nf, t // tm),
               in_specs=[h1_cur, h1_prev, h3_cur, pl.BlockSpec((3, tc), lambda j, i: (0, j)),
                         pl.BlockSpec((1, tc), lambda j, i: (0, j))],
               out_specs=[pl.BlockSpec((tm, tc), lambda j, i: (i, j))] * 2,
               out_shape=[_sds((t, f), BF16)] * 2,
               scratch_shapes=[pltpu.VMEM((tm + HALO, tc), F32)],
               compiler_params=_params())(hv, hv, hv, w, b)


def _ffn_bwd(hv, c1, dact, w, f, name):
    t = hv.shape[0]
    tm, tc = min(2048, t), _pick(f, (512, 256, 128))
    nf, nt = f // tc, t // tm

    def body(h1_ref, h3_ref, h3n_ref, c_ref, cn_ref, g_ref, gn_ref, w_ref, dh1_ref, dh3_ref, acc_ref, dbuf):
        i = pl.program_id(1)

        @pl.when(i == 0)
        def _():
            acc_ref[...] = jnp.zeros_like(acc_ref)
        for rows in _strips(tm):
            c1v, g = c_ref[rows, :].astype(F32), g_ref[rows, :].astype(F32)
            s1 = _sigmoid(c1v)
            dh3_ref[rows, :] = (g * c1v * s1).astype(BF16)
            dbuf[rows, :] = g * h3_ref[rows, :].astype(F32) * s1 * (1.0 + c1v * (1.0 - s1))
        dbuf[tm:tm + HALO, :] = (gn_ref[...].astype(F32) * h3n_ref[...].astype(F32)
                                 * _dsilu(cn_ref[...].astype(F32)) * (i < nt - 1).astype(F32))

        def emit(rows, d_in):
            dh1_ref[rows, :] = d_in.astype(BF16)

        _conv_backward(dbuf, lambda rows: h1_ref[rows, :].astype(F32), emit, w_ref[...], acc_ref, 3, tm, True)

    h1_cur, _, _ = _tile_specs(t, tm, tc, 0)
    h3_cur, _, h3_next = _tile_specs(t, tm, tc, nf)
    g_cur, _, g_next = _tile_specs(t, tm, tc, 0)
    out = pl.BlockSpec((tm, tc), lambda j, i: (i, j))
    return _pc(body, name=name, grid=(nf, nt),
               in_specs=[h1_cur, h3_cur, h3_next, g_cur, g_next, g_cur, g_next,
                         pl.BlockSpec((3, tc), lambda j, i: (0, j))],
               out_specs=[out, out, pl.BlockSpec((8, tc), lambda j, i: (0, j))],
               out_shape=[_sds((t, f), BF16), _sds((t, f), BF16), _sds((8, f), F32)],
               scratch_shapes=[pltpu.VMEM((tm + HALO, tc), F32)],
               compiler_params=_params())(hv, hv, hv, c1, c1, dact, dact, w)


def _gnorm_fwd(y, z, w, name):
    t, di = y.shape
    gw = di // N_GROUPS
    tm = min(2048, t)

    def body(y_ref, z_ref, w_ref, o_ref):
        wv = w_ref[...]
        for rows in _strips(tm):
            zv = z_ref[rows, :].astype(F32)
            yz = y_ref[rows, :].astype(F32) * zv * _sigmoid(zv)
            r = lax.rsqrt(jnp.mean(yz * yz, axis=-1, keepdims=True) + EPS)
            o_ref[rows, :] = (yz * r * wv).astype(o_ref.dtype)

    blk = pl.BlockSpec((tm, gw), lambda j, i: (i, j))
    return _pc(body, name=name, grid=(N_GROUPS, t // tm),
               in_specs=[blk, blk, pl.BlockSpec((1, gw), lambda j, i: (0, j))],
               out_specs=blk, out_shape=_sds((t, di), BF16), compiler_params=_params())(y, z, w)


def _gnorm_bwd(y, z, dyn, w, name):
    t, di = y.shape
    gw = di // N_GROUPS
    tm = min(2048, t)

    def body(y_ref, z_ref, g_ref, w_ref, dy_ref, dz_ref, dw_ref):
        @pl.when(pl.program_id(1) == 0)
        def _():
            dw_ref[...] = jnp.zeros_like(dw_ref)
        wv = w_ref[...]
        acc = jnp.zeros((8, gw), F32)
        for rows in _strips(tm):
            yv, zv, g = y_ref[rows, :].astype(F32), z_ref[rows, :].astype(F32), g_ref[rows, :].astype(F32)
            s = _sigmoid(zv)
            sz = zv * s
            yz = yv * sz
            r = lax.rsqrt(jnp.mean(yz * yz, axis=-1, keepdims=True) + EPS)
            n = yz * r
            dn = g * wv
            acc = acc + _fold8(g * n)
            dyz = r * (dn - n * jnp.mean(dn * n, axis=-1, keepdims=True))
            dy_ref[rows, :] = (dyz * sz).astype(BF16)
            dz_ref[rows, :] = (dyz * yv * s * (1.0 + zv * (1.0 - s))).astype(BF16)
        dw_ref[0:1, :] += _colsum(acc)

    blk = pl.BlockSpec((tm, gw), lambda j, i: (i, j))
    return _pc(body, name=name, grid=(N_GROUPS, t // tm),
               in_specs=[blk, blk, blk, pl.BlockSpec((1, gw), lambda j, i: (0, j))],
               out_specs=[blk, blk, pl.BlockSpec((8, gw), lambda j, i: (0, j))],
               out_shape=[_sds((t, di), BF16), _sds((t, di), BF16), _sds((8, di), F32)],
               compiler_params=_params())(y, z, dyn, w)


def _merge_fwd(gates, ya, ys, d, name):
    t = ya.shape[0]
    tm, tc = min(2048, t), _pick(d, (512, 256, 128))
    nd = d // tc

    def body(ga_ref, gs_ref, ya_ref, ys_ref, o_ref):
        for rows in _strips(tm):
            o_ref[rows, :] = (_sigmoid(ga_ref[rows, :].astype(F32)) * ya_ref[rows, :].astype(F32)
                              + _sigmoid(gs_ref[rows, :].astype(F32)) * ys_ref[rows, :].astype(F32)
                              ).astype(o_ref.dtype)

    blk = pl.BlockSpec((tm, tc), lambda j, i: (i, j))
    return _pc(body, name=name, grid=(nd, t // tm),
               in_specs=[blk, pl.BlockSpec((tm, tc), lambda j, i: (i, nd + j)), blk, blk],
               out_specs=blk, out_shape=_sds((t, d), BF16), compiler_params=_params())(gates, gates, ya, ys)


def _merge_bwd(dm, gates, ya, ys, d, name):
    t = ya.shape[0]
    tm, tc = min(2048, t), _pick(d, (512, 256, 128))
    nd = d // tc

    def body(dm_ref, ga_ref, gs_ref, ya_ref, ys_ref, dya_ref, dys_ref, dga_ref, dgs_ref):
        for rows in _strips(tm):
            g = dm_ref[rows, :].astype(F32)
            sa, ss = _sigmoid(ga_ref[rows, :].astype(F32)), _sigmoid(gs_ref[rows, :].astype(F32))
            dya_ref[rows, :] = (g * sa).astype(BF16)
            dys_ref[rows, :] = (g * ss).astype(BF16)
            dga_ref[rows, :] = (g * ya_ref[rows, :].astype(F32) * sa * (1.0 - sa)).astype(BF16)
            dgs_ref[rows, :] = (g * ys_ref[rows, :].astype(F32) * ss * (1.0 - ss)).astype(BF16)

    blk = pl.BlockSpec((tm, tc), lambda j, i: (i, j))
    return _pc(body, name=name, grid=(nd, t // tm),
               in_specs=[blk, blk, pl.BlockSpec((tm, tc), lambda j, i: (i, nd + j)), blk, blk],
               out_specs=[blk] * 4, out_shape=[_sds((t, d), BF16)] * 4,
               compiler_params=_params())(dm, gates, gates, ya, ys)


def _ssd_chunk_terms(dtr, dtb, alog):
    xx = dtr + dtb
    dt = jnp.maximum(xx, 0.0) + jnp.log(1.0 + jnp.exp(-jnp.abs(xx)))
    a = -jnp.exp(alog)
    li = lax.broadcasted_iota(jnp.int32, (CHUNK, CHUNK), 0)
    si = lax.broadcasted_iota(jnp.int32, (CHUNK, CHUNK), 1)
    causal = li >= si
    acum = _dot(causal.astype(F32), dt * a, 1, 0, HIGHEST)
    return xx, dt, a, acum, acum.T, causal


def _split2(x):
    hi = x.astype(BF16)
    return hi, (x - hi.astype(F32)).astype(BF16)


def _expand(v, e, exact=True):
    hi, lo = _split2(v)
    out = _dot(hi, e, 1, 0)
    return out + _dot(lo, e, 1, 0) if exact else out


def _segsum(s, e):
    hi, lo = _split2(s)
    return _dot(hi, e, 1, 1) + _dot(lo, e, 1, 1)


def _head_maps(di):
    nh = di // HEAD_DIM
    h = jnp.arange(DT_LANES)[:, None]
    e64 = (jnp.arange(di)[None, :] // HEAD_DIM == h).astype(BF16)
    e128 = (jnp.arange(nh * CHUNK)[None, :] // CHUNK == h).astype(BF16)
    return e64, e128


def _pair_blockdiag(p, left):
    zero = jnp.zeros_like(p)
    return jnp.concatenate([jnp.where(left, p, zero), jnp.where(left, zero, p)], axis=0)


def _ssd_fwd(xc, dtr, dtb, alog, dskx, di, name):
    t = xc.shape[0]
    dx = xc.shape[1]
    nc = t // CHUNK
    nh = di // HEAD_DIM
    hpg = nh // N_GROUPS
    gw = hpg * HEAD_DIM
    boff, coff = di, di + N_GROUPS * D_STATE
    e64, e128 = _head_maps(di)

    def body(xc_ref, dtr_ref, dtb_ref, alog_ref, dsk_ref, e64_ref, e128_ref, y_ref, st_ref, state):
        @pl.when(pl.program_id(0) == 0)
        def _():
            state[...] = jnp.zeros_like(state)
        _, dt, _, acum, acum_t, causal = _ssd_chunk_terms(dtr_ref[...], dtb_ref[...], alog_ref[...])
        last = acum[CHUNK - 1:CHUNK, :]
        e64v = e64_ref[...]
        dtx = _expand(dt, e64v, False)
        eax = _expand(jnp.exp(acum), e64v)
        dex = _expand(dt * jnp.exp(last - acum), e64v, False)
        acx = _expand(acum, e128_ref[...])
        st_ref[0] = state[...]
        left = lax.broadcasted_iota(jnp.int32, (CHUNK, 2 * HEAD_DIM), 1) < HEAD_DIM
        for g in range(N_GROUPS):
            gs = slice(g * gw, (g + 1) * gw)
            bg = xc_ref[:, boff + g * D_STATE:boff + (g + 1) * D_STATE]
            cg = xc_ref[:, coff + g * D_STATE:coff + (g + 1) * D_STATE]
            gm = _dot(cg, bg, 1, 1)
            xg = xc_ref[:, gs].astype(F32)
            xdb = (xg * dtx[:, gs]).astype(BF16)
            sin = state[:, gs]
            yo = _dot(cg, sin.astype(BF16), 1, 0) * eax[:, gs]
            for jp in range(hpg // 2):
                h0 = g * hpg + 2 * jp
                ps = slice(jp * 2 * HEAD_DIM, (jp + 1) * 2 * HEAD_DIM)
                ms = []
                for hh in (h0, h0 + 1):
                    seg = acx[:, hh * CHUNK:(hh + 1) * CHUNK] - acum_t[hh:hh + 1, :]
                    ms.append((gm * jnp.exp(jnp.where(causal, seg, -1e30))).astype(BF16))
                yd = _dot(jnp.concatenate(ms, axis=1), _pair_blockdiag(xdb[:, ps], left), 1, 0)
                col = slice(g * gw + jp * 2 * HEAD_DIM, g * gw + (jp + 1) * 2 * HEAD_DIM)
                y_ref[:, col] = (yd + yo[:, ps] + dsk_ref[:, col] * xg[:, ps]).astype(y_ref.dtype)
            xe = (xg * dex[:, gs]).astype(BF16)
            state[:, gs] = eax[CHUNK - 1:CHUNK, gs] * sin + _dot(bg, xe, 0, 0)

    small = pl.BlockSpec((1, DT_LANES), lambda c: (0, 0))
    whole = lambda a: pl.BlockSpec(a.shape, lambda c: (0, 0))
    return _pc(body, name=name, grid=(nc,),
               in_specs=[pl.BlockSpec((CHUNK, dx), lambda c: (c, 0)),
                         pl.BlockSpec((CHUNK, DT_LANES), lambda c: (c, 0)), small, small,
                         whole(dskx), whole(e64), whole(e128)],
               out_specs=[pl.BlockSpec((CHUNK, di), lambda c: (c, 0)),
                          pl.BlockSpec((1, D_STATE, di), lambda c: (c, 0, 0))],
               out_shape=[_sds((t, di), BF16), _sds((nc, D_STATE, di), F32)],
               scratch_shapes=[pltpu.VMEM((D_STATE, di), F32)],
               compiler_params=_params())(xc, dtr, dtb, alog, dskx, e64, e128)


def _ssd_bwd(xc, dtr, dy, states, dtb, alog, dskx, di, name):
    t = xc.shape[0]
    dx = xc.shape[1]
    nc = t // CHUNK
    nh = di // HEAD_DIM
    hpg = nh // N_GROUPS
    gw = hpg * HEAD_DIM
    boff, coff = di, di + N_GROUPS * D_STATE
    e64, e128 = _head_maps(di)

    def body(xc_ref, dtr_ref, dy_ref, st_ref, dtb_ref, alog_ref, dsk_ref, e64_ref, e128_ref,
             dxc_ref, ddtr_ref, sm_ref, dstate, darow):
        @pl.when(pl.program_id(0) == 0)
        def _():
            dstate[...] = jnp.zeros_like(dstate)
            sm_ref[...] = jnp.zeros_like(sm_ref)
        darow[...] = jnp.zeros_like(darow)
        xx, dt, a, acum, acum_t, causal = _ssd_chunk_terms(dtr_ref[...], dtb_ref[...], alog_ref[...])
        last = acum[CHUNK - 1:CHUNK, :]
        e64v = e64_ref[...]
        dtx = _expand(dt, e64v, False)
        eax = _expand(jnp.exp(acum), e64v)
        eex = _expand(jnp.exp(last - acum), e64v, False)
        acx = _expand(acum, e128_ref[...])
        left = lax.broadcasted_iota(jnp.int32, (CHUNK, 2 * HEAD_DIM), 1) < HEAD_DIM
        lane = lax.broadcasted_iota(jnp.int32, (CHUNK, DT_LANES), 1)
        sub8 = lax.broadcasted_iota(jnp.int32, (8, gw), 0)
        da_col = jnp.zeros((CHUNK, DT_LANES), F32)
        ddt_col = jnp.zeros((CHUNK, DT_LANES), F32)
        rows = jnp.zeros((8, DT_LANES), F32)
        for g in range(N_GROUPS):
            gs = slice(g * gw, (g + 1) * gw)
            bg = xc_ref[:, boff + g * D_STATE:boff + (g + 1) * D_STATE]
            cg = xc_ref[:, coff + g * D_STATE:coff + (g + 1) * D_STATE]
            gm = _dot(cg, bg, 1, 1)
            e64g = e64v[:, gs]
            xg = xc_ref[:, gs].astype(F32)
            dtg, eag, eeg = dtx[:, gs], eax[:, gs], eex[:, gs]
            xd = xg * dtg
            xdb = xd.astype(BF16)
            dyb = dy_ref[:, gs]
            dyf = dyb.astype(F32)
            sin = st_ref[0, :, gs]
            sinb = sin.astype(BF16)
            ds = dstate[:, gs]
            dsb = ds.astype(BF16)
            bds = _dot(bg, dsb, 1, 0)
            dyeb = (dyf * eag).astype(BF16)
            dcg = _dot(dyeb, sinb, 1, 1)
            dstate[:, gs] = eag[CHUNK - 1:CHUNK, :] * ds + _dot(cg, dyeb, 0, 0)
            yo = _dot(cg, sinb, 1, 0) * eag
            xe = xd * eeg
            dbg = _dot(xe.astype(BF16), dsb, 1, 1)
            wterm = bds * xe
            da_col = da_col + _segsum(dyf * yo - wterm, e64g)
            dg = jnp.zeros((CHUNK, CHUNK), F32)
            dxd_parts = []
            for jp in range(hpg // 2):
                h0 = g * hpg + 2 * jp
                ps = slice(jp * 2 * HEAD_DIM, (jp + 1) * 2 * HEAD_DIM)
                lms, mfs = [], []
                for hh in (h0, h0 + 1):
                    seg = acx[:, hh * CHUNK:(hh + 1) * CHUNK] - acum_t[hh:hh + 1, :]
                    lm = jnp.exp(jnp.where(causal, seg, -1e30))
                    lms.append(lm)
                    mfs.append(gm * lm)
                mstack = jnp.concatenate([m.astype(BF16) for m in mfs], axis=0)
                dyp = dyb[:, ps]
                dxd_parts.append(_dot(mstack, _pair_blockdiag(dyp, left), 0, 0))
                dm2 = _dot(dyp, _pair_blockdiag(xdb[:, ps], left), 1, 1)
                for k, hh in enumerate((h0, h0 + 1)):
                    dm = dm2[:, k * CHUNK:(k + 1) * CHUNK]
                    dg = dg + dm * lms[k]
                    q = dm * mfs[k]
                    da_col = da_col + jnp.where(lane == hh, jnp.sum(q, axis=1, keepdims=True), 0.0)
                    darow[hh:hh + 1, :] = -jnp.sum(q, axis=0, keepdims=True)
            dxd = jnp.concatenate(dxd_parts, axis=1) + bds * eeg
            ddt_col = ddt_col + _segsum(dxd * xg, e64g)
            rsum = (jnp.where(sub8 == 0, jnp.sum(wterm, axis=0, keepdims=True), 0.0)
                    + jnp.where(sub8 == 1, jnp.sum(ds * sin, axis=0, keepdims=True), 0.0)
                    + jnp.where(sub8 == 2, jnp.sum(dyf * xg, axis=0, keepdims=True), 0.0))
            rows = rows + _segsum(rsum, e64g)
            dxc_ref[:, gs] = (dxd * dtg + dsk_ref[:, gs] * dyf).astype(dxc_ref.dtype)
            dgb = dg.astype(BF16)
            dxc_ref[:, boff + g * D_STATE:boff + (g + 1) * D_STATE] = (
                dbg + _dot(dgb, cg, 0, 0)).astype(dxc_ref.dtype)
            dxc_ref[:, coff + g * D_STATE:coff + (g + 1) * D_STATE] = (
                dcg + _dot(dgb, bg, 1, 0)).astype(dxc_ref.dtype)
        at_last = rows[0:1, :] + jnp.exp(last) * rows[1:2, :]
        is_last = lax.broadcasted_iota(jnp.int32, (CHUNK, DT_LANES), 0) == CHUNK - 1
        da = da_col + jnp.where(is_last, at_last, 0.0) + darow[...].T
        li = lax.broadcasted_iota(jnp.int32, (CHUNK, CHUNK), 0)
        si = lax.broadcasted_iota(jnp.int32, (CHUNK, CHUNK), 1)
        dla = _dot((si >= li).astype(F32), da, 1, 0, HIGHEST)
        ddtr = (ddt_col + dla * a) * _sigmoid(xx)
        ddtr_ref[...] = ddtr
        sm_ref[0:1, :] += jnp.sum(ddtr, axis=0, keepdims=True)
        sm_ref[1:2, :] += jnp.sum(dla * dt, axis=0, keepdims=True) * a
        sm_ref[2:3, :] += rows[2:3, :]

    small = pl.BlockSpec((1, DT_LANES), lambda c: (0, 0))
    whole = lambda a: pl.BlockSpec(a.shape, lambda c: (0, 0))
    rev = lambda c: (nc - 1 - c, 0)
    return _pc(body, name=name, grid=(nc,),
               in_specs=[pl.BlockSpec((CHUNK, dx), rev), pl.BlockSpec((CHUNK, DT_LANES), rev),
                         pl.BlockSpec((CHUNK, di), rev),
                         pl.BlockSpec((1, D_STATE, di), lambda c: (nc - 1 - c, 0, 0)), small, small,
                         whole(dskx), whole(e64), whole(e128)],
               out_specs=[pl.BlockSpec((CHUNK, dx), rev), pl.BlockSpec((CHUNK, DT_LANES), rev),
                          pl.BlockSpec((8, DT_LANES), lambda c: (0, 0))],
               out_shape=[_sds((t, dx), BF16), _sds((t, DT_LANES), F32), _sds((8, DT_LANES), F32)],
               scratch_shapes=[pltpu.VMEM((D_STATE, di), F32), pltpu.VMEM((DT_LANES, CHUNK), F32)],
               compiler_params=_params())(xc, dtr, dy, states, dtb, alog, dskx, e64, e128)


def _adamw(parts, w, m, v, name):
    npart, rows, width = parts.shape
    if rows * width <= SMALL_PARAM:
        tr, tw = rows, width
    else:
        tr, tw = (_pick(rows, (64, 32, 16, 8)), width) if rows % 8 == 0 else (rows, 128)
    c1 = 1.0 - ADAM_B1 ** ADAM_STEP
    c2 = 1.0 - ADAM_B2 ** ADAM_STEP

    row_strips = _strips(tr) if tr % STRIP == 0 else [slice(0, tr)]
    col_chunks = [slice(c, c + 512) for c in range(0, tw, 512)] if tw % 512 == 0 else [slice(0, tw)]

    def body(p_ref, w_ref, m_ref, v_ref, g_ref, d_ref, nm_ref, nv_ref):
        for rows in row_strips:
            for cols in col_chunks:
                g = p_ref[0, rows, cols].astype(F32)
                for p in range(1, npart):
                    g = g + p_ref[p, rows, cols].astype(F32)
                nm = ADAM_B1 * m_ref[rows, cols] + (1.0 - ADAM_B1) * g
                nv = ADAM_B2 * v_ref[rows, cols] + (1.0 - ADAM_B2) * (g * g)
                g_ref[rows, cols] = g
                nm_ref[rows, cols] = nm
                nv_ref[rows, cols] = nv
                d_ref[rows, cols] = -ADAM_LR * ((nm / c1) / (jnp.sqrt(nv / c2) + ADAM_EPS)
                                                + ADAM_WD * w_ref[rows, cols])

    blk = pl.BlockSpec((tr, tw), lambda i, j: (i, j))
    return _pc(body, name=name, grid=(rows // tr, width // tw),
               in_specs=[pl.BlockSpec((npart, tr, tw), lambda i, j: (0, i, j)), blk, blk, blk],
               out_specs=[blk] * 4, out_shape=[_sds((rows, width), F32)] * 4,
               compiler_params=_params())(parts, w, m, v)


def _sum_parts(parts, name, tile=None):
    npart, rows, width = parts.shape
    tile = rows if tile is None else tile

    def body(p_ref, o_ref):
        for rows_ in _strips(tile, 8 if parts.dtype == F32 else STRIP):
            g = p_ref[0, rows_, :].astype(F32)
            for p in range(1, npart):
                g = g + p_ref[p, rows_, :].astype(F32)
            o_ref[rows_, :] = g

    return _pc(body, name=name, grid=(rows // tile,),
               in_specs=[pl.BlockSpec((npart, tile, width), lambda i: (0, i, 0))],
               out_specs=pl.BlockSpec((tile, width), lambda i: (i, 0)),
               out_shape=_sds((rows, width), F32), compiler_params=_params())(parts)


def _flip(k):
    x, y, c = lax.axis_index("x"), lax.axis_index("y"), lax.axis_index("c")
    px = 1 - x if k & 4 else x
    py = 1 - y if k & 2 else y
    pc = 1 - c if k & 1 else c
    return (px, py, pc), 4 * px + 2 * py + pc


DIRECT = tuple((k, 0) for k in range(1, N_DEV))
TO_CHIPS = ((1, 0), (2, 0), (4, 0), (6, 0))
TO_SIBLING = ((1, 2), (1, 4), (1, 6))


def _copies(arrays, lands, send_sems, recv_sems, scatter, moves):
    _, me = _flip(0)
    outgoing, incoming = [], []
    for i, (kd, kb) in enumerate(moves):
        peer, pidx = _flip(kd)
        _, out_slot = _flip(kb)
        _, in_slot = _flip(kd ^ kb)
        for j, land_ref in enumerate(lands):
            if kb:
                src = land_ref.at[out_slot]
            else:
                src = arrays[j].at[pidx] if scatter[j] else arrays[j]
            sem = len(lands) * i + j
            for dst, bucket in ((land_ref.at[out_slot], outgoing), (land_ref.at[in_slot], incoming)):
                bucket.append(pltpu.make_async_remote_copy(
                    src_ref=src, dst_ref=dst, send_sem=send_sems.at[sem], recv_sem=recv_sems.at[sem],
                    device_id=peer, device_id_type=MESH))
    return outgoing, incoming


HBM_SPEC = pl.BlockSpec(memory_space=pltpu.HBM)
SEM_SPEC = pl.BlockSpec(memory_space=pltpu.SEMAPHORE)
ANY_SPEC = pl.BlockSpec(memory_space=pl.ANY)
EFFECT = pltpu.SideEffectType.DATAFLOW_SIDE_EFFECTING


def _landing_zones(arrays, scatter):
    _, me = _flip(0)
    lands = []
    for a, sc in zip(arrays, scatter):
        own = lax.dynamic_index_in_dim(a, me, 0, keepdims=True) if sc else a[None]
        shape = a.shape if sc else (N_DEV,) + a.shape
        lands.append(lax.dynamic_update_slice(lax.empty(shape, a.dtype), own, (me,) + (0,) * (len(shape) - 1)))
    return lands


def _xchg_start(arrays, scatter, after, name, moves=DIRECT, lands=None):
    if lands is None:
        lands = _landing_zones(arrays, scatter)
    na, nl = len(arrays), len(lands)

    def body(*refs):
        ins, outs = refs[:na + nl], refs[na + nl + 1:]
        outgoing, _ = _copies(ins[:na], ins[na:], outs[0], outs[1], scatter, moves)
        for cp in outgoing:
            cp.start()
        outs[-1][...] = jnp.zeros_like(outs[-1])

    nsem = nl * len(moves)
    operands = [pltpu.with_memory_space_constraint(a, pltpu.HBM) for a in list(arrays) + list(lands)]
    out = _pc(body, name=name,
              out_shape=(pltpu.SemaphoreType.DMA((nsem,)), pltpu.SemaphoreType.DMA((nsem,)),
                         *[pltpu.HBM(a.shape, a.dtype) for a in operands], _sds((8, 128), F32)),
              in_specs=[HBM_SPEC] * (na + nl) + [ANY_SPEC],
              out_specs=(SEM_SPEC, SEM_SPEC, *[HBM_SPEC] * (na + nl), pl.BlockSpec(memory_space=pltpu.VMEM)),
              input_output_aliases={i: 2 + i for i in range(na + nl)},
              compiler_params=pltpu.CompilerParams(has_side_effects=EFFECT))(*operands, after)
    return dict(sems=out[:2], thru=out[2:2 + na + nl], token=out[-1], scatter=scatter, na=na, moves=moves)


def _xchg_wait(handle, after, name):
    na, scatter, moves, thru = handle["na"], handle["scatter"], handle["moves"], handle["thru"]
    n = len(thru)

    def body(*refs):
        ins = refs[:n]
        outgoing, incoming = _copies(ins[:na], ins[na:], refs[n], refs[n + 1], scatter, moves)
        for cp in outgoing:
            cp.wait_send()
        for cp in incoming:
            cp.wait_recv()

    out = _pc(body, name=name, out_shape=tuple(pltpu.HBM(a.shape, a.dtype) for a in thru),
              in_specs=[HBM_SPEC] * n + [SEM_SPEC, SEM_SPEC, ANY_SPEC], out_specs=tuple([HBM_SPEC] * n),
              input_output_aliases={i: i for i in range(n)},
              compiler_params=pltpu.CompilerParams(has_side_effects=EFFECT))(*thru, *handle["sems"], after)
    return out[na:]


def _pack(arrs, width, row_mult):
    flat = jnp.concatenate([a.reshape(-1) for a in arrs])
    n = flat.shape[0]
    rows = -(-n // (width * row_mult)) * row_mult
    return jnp.pad(flat, (0, rows * width - n)).reshape(rows, width)


def _unpack(packed, shapes, lead=None):
    out, off = [], 0
    flat = packed.reshape(-1) if lead is None else packed.reshape(lead, -1)
    for s in shapes:
        n = math.prod(s)
        if lead is None:
            out.append(flat[off:off + n].reshape(s))
        else:
            out.append(flat[:, off:off + n].reshape((lead,) + tuple(s)))
        off += n
    return out


def _blocks_to_cols(blocks):
    nb, rows, n = blocks.shape
    return blocks.transpose(1, 0, 2).reshape(rows, nb * n)


def _pad_rows(a, rows):
    return jnp.pad(a, ((0, rows - a.shape[0]), (0, 0)))


def _pad_lanes(a, lanes):
    return jnp.pad(a, ((0, 0), (0, lanes - a.shape[1])))


REST = ("w_a_out", "w_s_out", "w_o", "w_up", "w_down")
TRANSPOSED = ("w_up", "w_in")
CONVS = ("conv_a_w", "ssd_conv_w", "ffn_conv_w")
REPL = ("norm_mix_w", "ssd_conv_b", "dt_bias", "a_log", "d_skip", "ssd_norm_w", "norm_ffn_w", "ffn_conv_b",
        "final_norm_w")
ORDER = ("norm_mix_w", "w_in", "conv_a_w", "w_a_out", "ssd_conv_w", "ssd_conv_b", "dt_bias", "a_log", "d_skip",
         "ssd_norm_w", "w_s_out", "w_o", "norm_ffn_w", "w_up", "ffn_conv_w", "ffn_conv_b", "w_down", "final_norm_w")


def _as_rows(name, block):
    return block[0].T if name in TRANSPOSED else block[0]


def kernel(x, norm_mix_w, w_in, conv_a_w, w_a_out, ssd_conv_w, ssd_conv_b, dt_bias, a_log, d_skip, ssd_norm_w, w_s_out, w_o, norm_ffn_w, w_up, ffn_conv_w, ffn_conv_b, w_down, final_norm_w, loss_target, m_norm_mix_w, m_w_in, m_conv_a_w, m_w_a_out, m_ssd_conv_w, m_ssd_conv_b, m_dt_bias, m_a_log, m_d_skip, m_ssd_norm_w, m_w_s_out, m_w_o, m_norm_ffn_w, m_w_up, m_ffn_conv_w, m_ffn_conv_b, m_w_down, m_final_norm_w, v_norm_mix_w, v_w_in, v_conv_a_w, v_w_a_out, v_ssd_conv_w, v_ssd_conv_b, v_dt_bias, v_a_log, v_d_skip, v_ssd_norm_w, v_w_s_out, v_w_o, v_norm_ffn_w, v_w_up, v_ffn_conv_w, v_ffn_conv_b, v_w_down, v_final_norm_w):
    wts = dict(norm_mix_w=norm_mix_w, w_in=w_in, conv_a_w=conv_a_w, w_a_out=w_a_out, ssd_conv_w=ssd_conv_w,
               ssd_conv_b=ssd_conv_b, dt_bias=dt_bias, a_log=a_log, d_skip=d_skip, ssd_norm_w=ssd_norm_w,
               w_s_out=w_s_out, w_o=w_o, norm_ffn_w=norm_ffn_w, w_up=w_up, ffn_conv_w=ffn_conv_w,
               ffn_conv_b=ffn_conv_b, w_down=w_down, final_norm_w=final_norm_w)
    mom1 = dict(norm_mix_w=m_norm_mix_w, w_in=m_w_in, conv_a_w=m_conv_a_w, w_a_out=m_w_a_out,
                ssd_conv_w=m_ssd_conv_w, ssd_conv_b=m_ssd_conv_b, dt_bias=m_dt_bias, a_log=m_a_log, d_skip=m_d_skip,
                ssd_norm_w=m_ssd_norm_w, w_s_out=m_w_s_out, w_o=m_w_o, norm_ffn_w=m_norm_ffn_w, w_up=m_w_up,
                ffn_conv_w=m_ffn_conv_w, ffn_conv_b=m_ffn_conv_b, w_down=m_w_down, final_norm_w=m_final_norm_w)
    mom2 = dict(norm_mix_w=v_norm_mix_w, w_in=v_w_in, conv_a_w=v_conv_a_w, w_a_out=v_w_a_out,
                ssd_conv_w=v_ssd_conv_w, ssd_conv_b=v_ssd_conv_b, dt_bias=v_dt_bias, a_log=v_a_log, d_skip=v_d_skip,
                ssd_norm_w=v_ssd_norm_w, w_s_out=v_w_s_out, w_o=v_w_o, norm_ffn_w=v_norm_ffn_w, w_up=v_w_up,
                ffn_conv_w=v_ffn_conv_w, ffn_conv_b=v_ffn_conv_b, w_down=v_w_down, final_norm_w=v_final_norm_w)

    t, d = x.shape[1], x.shape[2]
    di = 2 * d
    nh = di // HEAD_DIM
    dxw = di + 2 * N_GROUPS * D_STATE
    f = w_down.shape[1] * N_DEV
    n_in = w_in.shape[2] * N_DEV
    me = 4 * lax.axis_index("x") + 2 * lax.axis_index("y") + lax.axis_index("c")

    rest_local = [_as_rows(k, wts[k]).astype(BF16) for k in REST]
    nrows = [a.shape[0] for a in rest_local]
    n_blk = w_in.shape[2]
    in_local = w_in[0].T.astype(BF16)
    conv_shapes = [wts[k].shape[1:] for k in CONVS]
    conv_local = _pack([wts[k] for k in CONVS], d, 8)
    x2, tgt = x[0], loss_target[0]
    h_in = _xchg_start([in_local, conv_local], [False, False], x2, "gather_in_start", moves=TO_CHIPS)
    u = _rms_fwd(x2, norm_mix_w, h_in["token"], "norm_mix")
    part = _xchg_wait(h_in, u, "gather_in_wait")
    h_fwd = _xchg_start([], [False, False], u, "gather_in_forward_start", moves=TO_SIBLING, lands=part)
    in_all, conv_all = _xchg_wait(h_fwd, u, "gather_in_forward_wait")
    win_t = in_all.reshape(n_in, d)
    h_rest = _xchg_start(rest_local, [False] * len(REST), in_all, "gather_rest_start")
    c_a, c_s, c_f = _unpack(conv_all, conv_shapes, N_DEV)
    caw, scw, fcw = _blocks_to_cols(c_a), _blocks_to_cols(c_s), _blocks_to_cols(c_f)

    o_z, o_x, o_dt = 5 * d, 7 * d, 7 * d + dxw
    seg_bounds = [0, d, 2 * d, 3 * d, 4 * d, o_z, o_x, o_dt]
    w_dt = _pad_rows(win_t[o_dt:], DT_LANES)
    dtb, alog = (_pad_lanes(p[...].reshape(1, nh), DT_LANES) for p in (dt_bias, a_log))
    dskx = jnp.repeat(d_skip.reshape(1, nh), HEAD_DIM, axis=1)

    tok = h_rest["token"]
    gates = _mm([(u, 0, d, win_t, 0)], "nt", BF16, "proj_gates", n=2 * d, tm=2048, after=tok)
    pa = _mm([(u, 0, d, win_t, 2 * d)], "nt", BF16, "proj_a", n=3 * d, tm=2048, after=tok)
    z = _mm([(u, 0, d, win_t, o_z)], "nt", BF16, "proj_z", n=2 * d, tm=2048, after=tok)
    xbc = _mm([(u, 0, d, win_t, o_x)], "nt", BF16, "proj_xbc", n=dxw, tm=2048, after=tok)
    dtr = _mm([(u, w_dt)], "nt", F32, "proj_dt", after=tok)
    ya_in, q_a = _conv_a_fwd(pa, caw, d, "conv_a")
    xc, pre_s = _conv_s_fwd(xbc, scw, ssd_conv_b, "conv_s")
    y, states = _ssd_fwd(xc, dtr, dtb, alog, dskx, di, "ssd")
    yn = _gnorm_fwd(y, z, ssd_norm_w, "gnorm")
    rest_all = _xchg_wait(h_rest, yn, "gather_rest_wait")
    waout, wsout, wo, wup_t, wdown = (a.reshape(N_DEV * n, d) for a, n in zip(rest_all, nrows))
    y_a = _mm([(ya_in, waout)], "nn", BF16, "a_out", tm=2048)
    y_s = _mm([(yn, wsout)], "nn", BF16, "s_out", tm=2048)
    merged = _merge_fwd(gates, y_a, y_s, d, "merge")
    mo = _mm([(merged, wo)], "nn", BF16, "o_proj", tm=2048)
    h1, v = _resnorm_fwd(x2, mo, norm_ffn_w, "norm_ffn")
    hv = _mm([(v, wup_t)], "nt", BF16, "up_proj", tm=512, resident_b=True)
    act, c1 = _ffn_fwd(hv, fcw, ffn_conv_b, f, "ffn_act")
    dd = _mm([(act, wdown)], "nn", BF16, "down_proj")
    loss11, dh2, dh2b, g_fnw = _final(h1, dd, tgt, final_norm_w.reshape(1, d), "final")

    dact = _mm([(dh2b, wdown)], "nt", BF16, "d_act", resident_b=True)
    gw_down = _mm_tn(act, dh2b, "gw_down")
    dh1f, dh3, g_ffn = _ffn_bwd(hv, c1, dact, fcw, f, "ffn_act_bwd")
    dv = _mm([(dh1f, 0, f, wup_t, 0), (dh3, 0, f, wup_t, f)], "nn", BF16, "d_v")
    gw_up_t = jnp.concatenate([_mm_tn(dh1f, v, "gw_up1"), _mm_tn(dh3, v, "gw_up3")], axis=0)
    dh1, dh1b, g_nfw = _rms_bwd(h1, dv, norm_ffn_w, dh2, "norm_ffn_bwd")
    dmerged = _mm([(dh1b, wo)], "nt", BF16, "d_merged", tm=2048)
    gw_o = _mm_tn(merged, dh1b, "gw_o")
    dya, dys, dga, dgs = _merge_bwd(dmerged, gates, y_a, y_s, d, "merge_bwd")
    dyain = _mm([(dya, waout)], "nt", BF16, "d_ya_in", tm=2048)
    gw_aout = _mm_tn(ya_in, dya, "gw_a_out")
    db, dc, dvv, g_caw = _conv_a_bwd(pa, q_a, dyain, caw, d, "conv_a_bwd")
    dyn = _mm([(dys, wsout)], "nt", BF16, "d_yn", resident_b=True)
    gw_sout = _mm_tn(yn, dys, "gw_s_out")
    grads_rest = dict(w_a_out=gw_aout, w_s_out=gw_sout, w_o=gw_o, w_up=gw_up_t, w_down=gw_down)
    rest_parts = [grads_rest[k].reshape(N_DEV, n, d) for k, n in zip(REST, nrows)]
    h_grest = _xchg_start(rest_parts, [True] * len(REST), rest_parts[0], "scatter_rest_start")
    dy, dz, g_snw = _gnorm_bwd(y, z, dyn, ssd_norm_w, "gnorm_bwd")
    dtb_after = dtb + h_grest["token"][0:1, 0:1]
    dxc, ddtr, g_ssd = _ssd_bwd(xc, dtr, dy, states, dtb_after, alog, dskx, di, "ssd_bwd")
    dxbc, g_scw = _conv_s_bwd(xbc, pre_s, dxc, scw, "conv_s_bwd")
    dsegs = [dga, dgs, db, dc, dvv, dz, dxbc, ddtr.astype(BF16)]
    pairs = [(s, c, d, win_t, a + c * d) for s, a in zip(dsegs[:-1], seg_bounds) for c in range(s.shape[1] // d)]
    pairs.append((dsegs[-1], w_dt))
    gw_in = [_mm_tn(s, u, "gw_in%d" % i) for i, s in enumerate(dsegs)]
    gw_in_t = jnp.concatenate(gw_in[:-1] + [gw_in[-1][:nh]], axis=0)
    in_parts = gw_in_t.reshape(N_DEV, n_blk, d)
    h_gin = _xchg_start([in_parts], [True], in_parts, "scatter_in_start")
    du = _mm(pairs, "nn", BF16, "d_u", tm=512, tn=1024, after=h_gin["token"], resident_b=True)
    dx, _, g_nmw = _rms_bwd(x2, du, norm_mix_w, dh1, "norm_mix_bwd")

    small_grads = dict(norm_mix_w=g_nmw[0], ssd_conv_b=g_scw[4], dt_bias=g_ssd[0, :nh], a_log=g_ssd[1, :nh],
                       d_skip=g_ssd[2, :nh], ssd_norm_w=g_snw[0], norm_ffn_w=g_nfw[0], ffn_conv_b=g_ffn[3],
                       final_norm_w=g_fnw[0], conv_a_w=g_caw[:3], ssd_conv_w=g_scw[:4], ffn_conv_w=g_ffn[:3])
    small_names = REPL + CONVS
    small_parts = _pack([small_grads[k] for k in small_names] + [loss11], d, 8)
    h_small = _xchg_start([small_parts], [False], small_parts, "gather_small_start")
    rest_recv = _xchg_wait(h_grest, dx, "scatter_rest_wait")
    (in_recv,) = _xchg_wait(h_gin, rest_recv[0], "scatter_in_wait")
    (small_all,) = _xchg_wait(h_small, in_recv, "gather_small_wait")
    small_sum = _sum_parts(small_all, "sum_small_grads")
    *small_list, loss = _unpack(small_sum, [small_grads[k].shape for k in small_names] + [()])
    small_g = dict(zip(small_names, small_list))

    res = {}

    def update(k, parts):
        outs = _adamw(parts, *(_as_rows(k, src[k]) for src in (wts, mom1, mom2)), "adamw_" + k)
        for kind, a in zip(("g", "d", "m", "v"), outs):
            res[kind, k] = (a.T if k in TRANSPOSED else a)[None]

    update("w_in", in_recv)
    for k, parts in zip(REST, rest_recv):
        update(k, parts)
    local_g = {}
    for k in REPL:
        local_g[k] = small_g[k].reshape(wts[k].shape)
    for k in CONVS:
        n = wts[k].shape[2]
        local_g[k] = lax.dynamic_slice_in_dim(small_g[k], me * n, n, axis=1)[None]
    for k in small_names:
        as2d = lambda a: a.reshape(-1, a.shape[-1])
        outs = _adamw(as2d(local_g[k])[None], *(as2d(src[k]) for src in (wts, mom1, mom2)), "adamw_" + k)
        for kind, a in zip(("g", "d", "m", "v"), outs):
            res[kind, k] = a.reshape(wts[k].shape)

    return (loss, dx[None], *[res["g", k] for k in ORDER], *[res["d", k] for k in ORDER],
            *[res["m", k] for k in ORDER], *[res["v", k] for k in ORDER])
```

```python
import functools
import math

import jax
import jax.numpy as jnp
from jax import lax
from jax.experimental import pallas as pl
from jax.experimental.pallas import tpu as pltpu

F32 = jnp.float32
BF16 = jnp.bfloat16
EPS = 1e-5
HEAD_DIM = 64
N_GROUPS = 4
D_STATE = 128
CHUNK = 128
DT_LANES = 128
HALO = 16
STRIP = 16
SMALL_PARAM = 16 * 1024
N_DEV = 8
V7X_VMEM_LIMIT = 56 * 1024 * 1024
ADAM_LR, ADAM_B1, ADAM_B2, ADAM_EPS, ADAM_WD, ADAM_STEP = 0.001, 0.9, 0.999, 1e-08, 0.01, 10
HIGHEST = lax.Precision.HIGHEST
MESH = pl.DeviceIdType.MESH


def _pc(body, **kw):
    return pl.pallas_call(body, **kw)


def _params():
    return pltpu.CompilerParams(vmem_limit_bytes=V7X_VMEM_LIMIT)


def _pick(n, cands):
    for c in cands:
        if n % c == 0:
            return c
    return n


def _dot(a, b, ca, cb, prec=None):
    return lax.dot_general(a, b, (((ca,), (cb,)), ((), ())), preferred_element_type=F32, precision=prec)


def _sigmoid(x):
    return 0.5 * jnp.tanh(0.5 * x) + 0.5


def _sds(shape, dtype):
    return jax.ShapeDtypeStruct(shape, dtype)


def _mm(pairs, mode, out_dtype, name, n=None, tm=1024, tn=1024, after=None, resident_b=False):
    pairs = [p if len(p) == 5 else (p[0], 0, p[0].shape[1], p[1], 0) for p in pairs]
    m = pairs[0][0].shape[0]
    if n is None:
        n = pairs[0][3].shape[1] if mode == "nn" else pairs[0][3].shape[0]
    tm = min(tm, m)
    rows_nt = [p[4] for p in pairs] if mode == "nt" else []
    tn = next(c for c in ((n,) if resident_b else ()) + (tn, 1408, 512, 256, 128)
              if n % c == 0 and all(r % c == 0 for r in rows_nt))
    npair = len(pairs)
    cb = 0 if mode == "nn" else 1

    def body(*refs):
        o_ref = refs[-1]
        acc = None
        for p in range(npair):
            part = _dot(refs[2 * p][...], refs[2 * p + 1][...], 1, cb)
            acc = part if acc is None else acc + part
        o_ref[...] = acc.astype(o_ref.dtype)

    in_specs, args = [], []
    for a, a_col, kk, b, b_row in pairs:
        in_specs.append(pl.BlockSpec((tm, kk), lambda i, j, c=a_col: (i, c)))
        if mode == "nn":
            assert b_row % kk == 0 and (not resident_b or n == tn)
            in_specs.append(pl.BlockSpec((kk, tn), lambda i, j, r=b_row // kk: (r, j),
                                         pipeline_mode=pl.Buffered(1) if resident_b else None))
        else:
            in_specs.append(pl.BlockSpec((tn, kk), lambda i, j, r=b_row // tn: (r + j, 0),
                                         pipeline_mode=pl.Buffered(1) if resident_b and tn == n else None))
        args += [a, b]
    if after is not None:
        in_specs.append(pl.BlockSpec(memory_space=pl.ANY))
        args.append(after)
    return _pc(body, name=name, grid=(m // tm, n // tn), in_specs=in_specs,
               out_specs=pl.BlockSpec((tm, tn), lambda i, j: (i, j)),
               out_shape=_sds((m, n), out_dtype), compiler_params=_params())(*args)


def _mm_tn(a, b, name, tm=2048):
    m, ka = a.shape
    nb = b.shape[1]
    tm = min(tm, m)
    nm = m // tm
    tk = _pick(ka, (1024, 1408, 512, 256, 128))
    tn = _pick(nb, (1024, 512, 256, 128))

    def body(a_ref, b_ref, o_ref, acc):
        t = pl.program_id(2)

        @pl.when(t == 0)
        def _():
            acc[...] = jnp.zeros_like(acc)
        acc[...] += _dot(a_ref[...], b_ref[...], 0, 0)

        @pl.when(t == nm - 1)
        def _():
            o_ref[...] = acc[...].astype(o_ref.dtype)

    return _pc(body, name=name, grid=(ka // tk, nb // tn, nm),
               in_specs=[pl.BlockSpec((tm, tk), lambda i, j, t: (t, i)),
                         pl.BlockSpec((tm, tn), lambda i, j, t: (t, j))],
               out_specs=pl.BlockSpec((tk, tn), lambda i, j, t: (i, j)),
               out_shape=_sds((ka, nb), BF16), scratch_shapes=[pltpu.VMEM((tk, tn), F32)],
               compiler_params=_params())(a, b)


def _strips(tm, strip=STRIP):
    return [slice(r * strip, (r + 1) * strip) for r in range(tm // strip)]


def _fold8(a):
    out = a[0:8, :]
    for r in range(8, a.shape[0], 8):
        out = out + a[r:r + 8, :]
    return out


def _colsum(a8):
    return jnp.sum(a8, axis=0, keepdims=True)


def _rms_fwd(x, w, after, name):
    t, d = x.shape
    tm = min(1024, t)

    def body(x_ref, w_ref, after_ref, o_ref):
        wv = w_ref[...]
        for rows in _strips(tm):
            xv = x_ref[rows, :]
            r = lax.rsqrt(jnp.mean(xv * xv, axis=-1, keepdims=True) + EPS)
            o_ref[rows, :] = (xv * r * wv).astype(o_ref.dtype)

    return _pc(body, name=name, grid=(t // tm,),
               in_specs=[pl.BlockSpec((tm, d), lambda i: (i, 0)), pl.BlockSpec((1, d), lambda i: (0, 0)),
                         pl.BlockSpec(memory_space=pl.ANY)],
               out_specs=pl.BlockSpec((tm, d), lambda i: (i, 0)),
               out_shape=_sds((t, d), BF16), compiler_params=_params())(x, w, after)


def _resnorm_fwd(x, mo, w, name):
    t, d = x.shape
    tm = min(1024, t)

    def body(x_ref, mo_ref, w_ref, h_ref, v_ref):
        wv = w_ref[...]
        for rows in _strips(tm):
            h = x_ref[rows, :] + mo_ref[rows, :].astype(F32)
            r = lax.rsqrt(jnp.mean(h * h, axis=-1, keepdims=True) + EPS)
            h_ref[rows, :] = h
            v_ref[rows, :] = (h * r * wv).astype(v_ref.dtype)

    row = pl.BlockSpec((tm, d), lambda i: (i, 0))
    return _pc(body, name=name, grid=(t // tm,),
               in_specs=[row, row, pl.BlockSpec((1, d), lambda i: (0, 0))],
               out_specs=[row, row], out_shape=[_sds((t, d), F32), _sds((t, d), BF16)],
               compiler_params=_params())(x, mo, w)


def _rms_bwd(h, dy, w, dres, name):
    t, d = h.shape
    tm = min(1024, t)

    def body(h_ref, dy_ref, w_ref, dres_ref, dx_ref, dxb_ref, dw_ref):
        @pl.when(pl.program_id(0) == 0)
        def _():
            dw_ref[...] = jnp.zeros_like(dw_ref)
        wv = w_ref[...]
        acc = jnp.zeros((8, d), F32)
        for rows in _strips(tm):
            hv = h_ref[rows, :]
            dyv = dy_ref[rows, :].astype(F32)
            r = lax.rsqrt(jnp.mean(hv * hv, axis=-1, keepdims=True) + EPS)
            n = hv * r
            dn = dyv * wv
            acc = acc + _fold8(dyv * n)
            dx = dres_ref[rows, :] + r * (dn - n * jnp.mean(dn * n, axis=-1, keepdims=True))
            dx_ref[rows, :] = dx
            dxb_ref[rows, :] = dx.astype(BF16)
        dw_ref[0:1, :] += _colsum(acc)

    row = pl.BlockSpec((tm, d), lambda i: (i, 0))
    return _pc(body, name=name, grid=(t // tm,),
               in_specs=[row, row, pl.BlockSpec((1, d), lambda i: (0, 0)), row],
               out_specs=[row, row, pl.BlockSpec((8, d), lambda i: (0, 0))],
               out_shape=[_sds((t, d), F32), _sds((t, d), BF16), _sds((8, d), F32)],
               compiler_params=_params())(h, dy, w, dres)


def _final(h1, dd, tgt, w, name):
    t, d = h1.shape
    tm = min(1024, t)
    nt = t // tm

    def body(h1_ref, dd_ref, tgt_ref, w_ref, loss_ref, dh_ref, dhb_ref, dw_ref, acc):
        i = pl.program_id(0)

        @pl.when(i == 0)
        def _():
            dw_ref[...] = jnp.zeros_like(dw_ref)
            acc[...] = jnp.zeros_like(acc)
        wv = w_ref[...]
        sq = jnp.zeros((8, d), F32)
        dw = jnp.zeros((8, d), F32)
        for rows in _strips(tm):
            h = h1_ref[rows, :] + dd_ref[rows, :].astype(F32)
            r = lax.rsqrt(jnp.mean(h * h, axis=-1, keepdims=True) + EPS)
            n = h * r
            e = n * wv - tgt_ref[rows, :]
            sq = sq + _fold8(e * e)
            dout = e * (1.0 / d)
            dn = dout * wv
            dw = dw + _fold8(dout * n)
            dh = r * (dn - n * jnp.mean(dn * n, axis=-1, keepdims=True))
            dh_ref[rows, :] = dh
            dhb_ref[rows, :] = dh.astype(BF16)
        acc[...] += _colsum(sq)
        dw_ref[0:1, :] += _colsum(dw)

        @pl.when(i == nt - 1)
        def _():
            loss_ref[...] = jnp.sum(acc[...], axis=-1, keepdims=True) * (0.5 / d)

    row = pl.BlockSpec((tm, d), lambda i: (i, 0))
    return _pc(body, name=name, grid=(nt,),
               in_specs=[row, row, row, pl.BlockSpec((1, d), lambda i: (0, 0))],
               out_specs=[pl.BlockSpec((1, 1), lambda i: (0, 0)), row, row, pl.BlockSpec((8, d), lambda i: (0, 0))],
               out_shape=[_sds((1, 1), F32), _sds((t, d), F32), _sds((t, d), BF16), _sds((8, d), F32)],
               scratch_shapes=[pltpu.VMEM((1, d), F32)], compiler_params=_params())(h1, dd, tgt, w)


def _tile_specs(t, tm, tc, col0):
    th = tm // HALO
    last = t // HALO - 1
    cur = pl.BlockSpec((tm, tc), lambda j, i: (i, col0 + j))
    prev = pl.BlockSpec((HALO, tc), lambda j, i: (jnp.maximum(i * th - 1, 0), col0 + j))
    nxt = pl.BlockSpec((HALO, tc), lambda j, i: (jnp.minimum((i + 1) * th, last), col0 + j))
    return cur, prev, nxt


def _conv_strip(buf, w, k, rows):
    out = None
    for j in range(k):
        term = w[j:j + 1, :] * buf[pl.ds(HALO - (k - 1) + j + rows.start, STRIP), :]
        out = term if out is None else out + term
    return out


def _conv_backward(dbuf, x_strip, emit, w, acc_ref, k, tm, with_bias):
    tc = dbuf.shape[1]
    accs = [jnp.zeros((8, tc), F32) for _ in range(k + int(with_bias))]
    for rows in _strips(tm):
        xs = x_strip(rows)
        dx = None
        for j in range(k):
            ds = dbuf[pl.ds(rows.start + k - 1 - j, STRIP), :]
            term = w[j:j + 1, :] * ds
            dx = term if dx is None else dx + term
            accs[j] = accs[j] + _fold8(ds * xs)
            if with_bias and j == k - 1:
                accs[k] = accs[k] + _fold8(ds)
        emit(rows, dx)
    for j, a in enumerate(accs):
        acc_ref[j:j + 1, :] += _colsum(a)


def _conv_a_fwd(pa, w, d, name):
    t = pa.shape[0]
    tm, tc = min(2048, t), _pick(d, (512, 256, 128))
    nd = d // tc

    def body(b_ref, c_ref, v_ref, cp_ref, vp_ref, w_ref, o_ref, q_ref, buf):
        keep = (pl.program_id(1) > 0).astype(F32)
        buf[0:HALO, :] = cp_ref[...].astype(F32) * vp_ref[...].astype(F32) * keep
        for rows in _strips(tm):
            buf[HALO + rows.start:HALO + rows.stop, :] = c_ref[rows, :].astype(F32) * v_ref[rows, :].astype(F32)
        wv = w_ref[...]
        for rows in _strips(tm):
            q = _conv_strip(buf, wv, 3, rows)
            q_ref[rows, :] = q.astype(q_ref.dtype)
            o_ref[rows, :] = (b_ref[rows, :].astype(F32) * q).astype(o_ref.dtype)

    b_cur, _, _ = _tile_specs(t, tm, tc, 0)
    c_cur, c_prev, _ = _tile_specs(t, tm, tc, nd)
    v_cur, v_prev, _ = _tile_specs(t, tm, tc, 2 * nd)
    return _pc(body, name=name, grid=(nd, t // tm),
               in_specs=[b_cur, c_cur, v_cur, c_prev, v_prev, pl.BlockSpec((3, tc), lambda j, i: (0, j))],
               out_specs=[pl.BlockSpec((tm, tc), lambda j, i: (i, j))] * 2,
               out_shape=[_sds((t, d), BF16)] * 2,
               scratch_shapes=[pltpu.VMEM((tm + HALO, tc), F32)],
               compiler_params=_params())(pa, pa, pa, pa, pa, w)


def _conv_a_bwd(pa, q, dya, w, d, name):
    t = pa.shape[0]
    tm, tc = min(2048, t), _pick(d, (512, 256, 128))
    nd, nt = d // tc, t // tm

    def body(b_ref, c_ref, v_ref, bn_ref, q_ref, g_ref, gn_ref, w_ref, db_ref, dc_ref, dv_ref, acc_ref, dbuf):
        i = pl.program_id(1)

        @pl.when(i == 0)
        def _():
            acc_ref[...] = jnp.zeros_like(acc_ref)
        for rows in _strips(tm):
            g = g_ref[rows, :].astype(F32)
            dbuf[rows, :] = g * b_ref[rows, :].astype(F32)
            db_ref[rows, :] = (g * q_ref[rows, :].astype(F32)).astype(BF16)
        dbuf[tm:tm + HALO, :] = gn_ref[...].astype(F32) * bn_ref[...].astype(F32) * (i < nt - 1).astype(F32)

        def emit(rows, dp):
            dc_ref[rows, :] = (dp * v_ref[rows, :].astype(F32)).astype(BF16)
            dv_ref[rows, :] = (dp * c_ref[rows, :].astype(F32)).astype(BF16)

        _conv_backward(dbuf, lambda rows: c_ref[rows, :].astype(F32) * v_ref[rows, :].astype(F32), emit,
                       w_ref[...], acc_ref, 3, tm, False)

    b_cur, _, b_next = _tile_specs(t, tm, tc, 0)
    c_cur, _, _ = _tile_specs(t, tm, tc, nd)
    v_cur, _, _ = _tile_specs(t, tm, tc, 2 * nd)
    g_cur, _, g_next = _tile_specs(t, tm, tc, 0)
    out = pl.BlockSpec((tm, tc), lambda j, i: (i, j))
    return _pc(body, name=name, grid=(nd, nt),
               in_specs=[b_cur, c_cur, v_cur, b_next, g_cur, g_cur, g_next,
                         pl.BlockSpec((3, tc), lambda j, i: (0, j))],
               out_specs=[out, out, out, pl.BlockSpec((8, tc), lambda j, i: (0, j))],
               out_shape=[_sds((t, d), BF16)] * 3 + [_sds((8, d), F32)],
               scratch_shapes=[pltpu.VMEM((tm + HALO, tc), F32)],
               compiler_params=_params())(pa, pa, pa, pa, q, dya, dya, w)


def _conv_s_fwd(xbc, w, b, name):
    t, dx = xbc.shape
    tm, tc = min(2048, t), _pick(dx, (512, 256, 128))

    def body(x_ref, xp_ref, w_ref, b_ref, o_ref, pre_ref, buf):
        buf[0:HALO, :] = xp_ref[...].astype(F32) * (pl.program_id(1) > 0).astype(F32)
        for rows in _strips(tm):
            buf[HALO + rows.start:HALO + rows.stop, :] = x_ref[rows, :].astype(F32)
        wv, bv = w_ref[...], b_ref[...]
        for rows in _strips(tm):
            pre = _conv_strip(buf, wv, 4, rows) + bv
            pre_ref[rows, :] = pre.astype(pre_ref.dtype)
            o_ref[rows, :] = (pre * _sigmoid(pre)).astype(o_ref.dtype)

    cur, prev, _ = _tile_specs(t, tm, tc, 0)
    return _pc(body, name=name, grid=(dx // tc, t // tm),
               in_specs=[cur, prev, pl.BlockSpec((4, tc), lambda j, i: (0, j)),
                         pl.BlockSpec((1, tc), lambda j, i: (0, j))],
               out_specs=[pl.BlockSpec((tm, tc), lambda j, i: (i, j))] * 2,
               out_shape=[_sds((t, dx), BF16)] * 2,
               scratch_shapes=[pltpu.VMEM((tm + HALO, tc), F32)],
               compiler_params=_params())(xbc, xbc, w, b)


def _dsilu(pre):
    s = _sigmoid(pre)
    return s * (1.0 + pre * (1.0 - s))


def _conv_s_bwd(xbc, pre, dxc, w, name):
    t, dx = xbc.shape
    tm, tc = min(2048, t), _pick(dx, (512, 256, 128))
    nt = t // tm

    def body(x_ref, p_ref, pn_ref, g_ref, gn_ref, w_ref, dx_ref, acc_ref, dbuf):
        i = pl.program_id(1)

        @pl.when(i == 0)
        def _():
            acc_ref[...] = jnp.zeros_like(acc_ref)
        for rows in _strips(tm):
            dbuf[rows, :] = g_ref[rows, :].astype(F32) * _dsilu(p_ref[rows, :].astype(F32))
        dbuf[tm:tm + HALO, :] = (gn_ref[...].astype(F32) * _dsilu(pn_ref[...].astype(F32))
                                 * (i < nt - 1).astype(F32))

        def emit(rows, d_in):
            dx_ref[rows, :] = d_in.astype(BF16)

        _conv_backward(dbuf, lambda rows: x_ref[rows, :].astype(F32), emit, w_ref[...], acc_ref, 4, tm, True)

    cur, _, nxt = _tile_specs(t, tm, tc, 0)
    return _pc(body, name=name, grid=(dx // tc, nt),
               in_specs=[cur, cur, nxt, cur, nxt, pl.BlockSpec((4, tc), lambda j, i: (0, j))],
               out_specs=[pl.BlockSpec((tm, tc), lambda j, i: (i, j)), pl.BlockSpec((8, tc), lambda j, i: (0, j))],
               out_shape=[_sds((t, dx), BF16), _sds((8, dx), F32)],
               scratch_shapes=[pltpu.VMEM((tm + HALO, tc), F32)],
               compiler_params=_params())(xbc, pre, pre, dxc, dxc, w)


def _ffn_fwd(hv, w, b, f, name):
    t = hv.shape[0]
    tm, tc = min(2048, t), _pick(f, (512, 256, 128))
    nf = f // tc

    def body(h1_ref, h1p_ref, h3_ref, w_ref, b_ref, o_ref, c1_ref, buf):
        buf[0:HALO, :] = h1p_ref[...].astype(F32) * (pl.program_id(1) > 0).astype(F32)
        for rows in _strips(tm):
            buf[HALO + rows.start:HALO + rows.stop, :] = h1_ref[rows, :].astype(F32)
        wv, bv = w_ref[...], b_ref[...]
        for rows in _strips(tm):
            c1 = _conv_strip(buf, wv, 3, rows) + bv
            c1_ref[rows, :] = c1.astype(c1_ref.dtype)
            o_ref[rows, :] = (c1 * _sigmoid(c1) * h3_ref[rows, :].astype(F32)).astype(o_ref.dtype)

    h1_cur, h1_prev, _ = _tile_specs(t, tm, tc, 0)
    h3_cur, _, _ = _tile_specs(t, tm, tc, nf)
    return _pc(body, name=name, grid=(nf, t // tm),
               in_specs=[h1_cur, h1_prev, h3_cur, pl.BlockSpec((3, tc), lambda j, i: (0, j)),
                         pl.BlockSpec((1, tc), lambda j, i: (0, j))],
               out_specs=[pl.BlockSpec((tm, tc), lambda j, i: (i, j))] * 2,
               out_shape=[_sds((t, f), BF16)] * 2,
               scratch_shapes=[pltpu.VMEM((tm + HALO, tc), F32)],
               compiler_params=_params())(hv, hv, hv, w, b)


def _ffn_bwd(hv, c1, dact, w, f, name):
    t = hv.shape[0]
    tm, tc = min(2048, t), _pick(f, (512, 256, 128))
    nf, nt = f // tc, t // tm

    def body(h1_ref, h3_ref, h3n_ref, c_ref, cn_ref, g_ref, gn_ref, w_ref, dh1_ref, dh3_ref, acc_ref, dbuf):
        i = pl.program_id(1)

        @pl.when(i == 0)
        def _():
            acc_ref[...] = jnp.zeros_like(acc_ref)
        for rows in _strips(tm):
            c1v, g = c_ref[rows, :].astype(F32), g_ref[rows, :].astype(F32)
            s1 = _sigmoid(c1v)
            dh3_ref[rows, :] = (g * c1v * s1).astype(BF16)
            dbuf[rows, :] = g * h3_ref[rows, :].astype(F32) * s1 * (1.0 + c1v * (1.0 - s1))
        dbuf[tm:tm + HALO, :] = (gn_ref[...].astype(F32) * h3n_ref[...].astype(F32)
                                 * _dsilu(cn_ref[...].astype(F32)) * (i < nt - 1).astype(F32))

        def emit(rows, d_in):
            dh1_ref[rows, :] = d_in.astype(BF16)

        _conv_backward(dbuf, lambda rows: h1_ref[rows, :].astype(F32), emit, w_ref[...], acc_ref, 3, tm, True)

    h1_cur, _, _ = _tile_specs(t, tm, tc, 0)
    h3_cur, _, h3_next = _tile_specs(t, tm, tc, nf)
    g_cur, _, g_next = _tile_specs(t, tm, tc, 0)
    out = pl.BlockSpec((tm, tc), lambda j, i: (i, j))
    return _pc(body, name=name, grid=(nf, nt),
               in_specs=[h1_cur, h3_cur, h3_next, g_cur, g_next, g_cur, g_next,
                         pl.BlockSpec((3, tc), lambda j, i: (0, j))],
               out_specs=[out, out, pl.BlockSpec((8, tc), lambda j, i: (0, j))],
               out_shape=[_sds((t, f), BF16), _sds((t, f), BF16), _sds((8, f), F32)],
               scratch_shapes=[pltpu.VMEM((tm + HALO, tc), F32)],
               compiler_params=_params())(hv, hv, hv, c1, c1, dact, dact, w)


def _gnorm_fwd(y, z, w, name):
    t, di = y.shape
    gw = di // N_GROUPS
    tm = min(2048, t)

    def body(y_ref, z_ref, w_ref, o_ref):
        wv = w_ref[...]
        for rows in _strips(tm):
            zv = z_ref[rows, :].astype(F32)
            yz = y_ref[rows, :].astype(F32) * zv * _sigmoid(zv)
            r = lax.rsqrt(jnp.mean(yz * yz, axis=-1, keepdims=True) + EPS)
            o_ref[rows, :] = (yz * r * wv).astype(o_ref.dtype)

    blk = pl.BlockSpec((tm, gw), lambda j, i: (i, j))
    return _pc(body, name=name, grid=(N_GROUPS, t // tm),
               in_specs=[blk, blk, pl.BlockSpec((1, gw), lambda j, i: (0, j))],
               out_specs=blk, out_shape=_sds((t, di), BF16), compiler_params=_params())(y, z, w)


def _gnorm_bwd(y, z, dyn, w, name):
    t, di = y.shape
    gw = di // N_GROUPS
    tm = min(2048, t)

    def body(y_ref, z_ref, g_ref, w_ref, dy_ref, dz_ref, dw_ref):
        @pl.when(pl.program_id(1) == 0)
        def _():
            dw_ref[...] = jnp.zeros_like(dw_ref)
        wv = w_ref[...]
        acc = jnp.zeros((8, gw), F32)
        for rows in _strips(tm):
            yv, zv, g = y_ref[rows, :].astype(F32), z_ref[rows, :].astype(F32), g_ref[rows, :].astype(F32)
            s = _sigmoid(zv)
            sz = zv * s
            yz = yv * sz
            r = lax.rsqrt(jnp.mean(yz * yz, axis=-1, keepdims=True) + EPS)
            n = yz * r
            dn = g * wv
            acc = acc + _fold8(g * n)
            dyz = r * (dn - n * jnp.mean(dn * n, axis=-1, keepdims=True))
            dy_ref[rows, :] = (dyz * sz).astype(BF16)
            dz_ref[rows, :] = (dyz * yv * s * (1.0 + zv * (1.0 - s))).astype(BF16)
        dw_ref[0:1, :] += _colsum(acc)

    blk = pl.BlockSpec((tm, gw), lambda j, i: (i, j))
    return _pc(body, name=name, grid=(N_GROUPS, t // tm),
               in_specs=[blk, blk, blk, pl.BlockSpec((1, gw), lambda j, i: (0, j))],
               out_specs=[blk, blk, pl.BlockSpec((8, gw), lambda j, i: (0, j))],
               out_shape=[_sds((t, di), BF16), _sds((t, di), BF16), _sds((8, di), F32)],
               compiler_params=_params())(y, z, dyn, w)


def _merge_fwd(gates, ya, ys, d, name):
    t = ya.shape[0]
    tm, tc = min(2048, t), _pick(d, (512, 256, 128))
    nd = d // tc

    def body(ga_ref, gs_ref, ya_ref, ys_ref, o_ref):
        for rows in _strips(tm):
            o_ref[rows, :] = (_sigmoid(ga_ref[rows, :].astype(F32)) * ya_ref[rows, :].astype(F32)
                              + _sigmoid(gs_ref[rows, :].astype(F32)) * ys_ref[rows, :].astype(F32)
                              ).astype(o_ref.dtype)

    blk = pl.BlockSpec((tm, tc), lambda j, i: (i, j))
    return _pc(body, name=name, grid=(nd, t // tm),
               in_specs=[blk, pl.BlockSpec((tm, tc), lambda j, i: (i, nd + j)), blk, blk],
               out_specs=blk, out_shape=_sds((t, d), BF16), compiler_params=_params())(gates, gates, ya, ys)


def _merge_bwd(dm, gates, ya, ys, d, name):
    t = ya.shape[0]
    tm, tc = min(2048, t), _pick(d, (512, 256, 128))
    nd = d // tc

    def body(dm_ref, ga_ref, gs_ref, ya_ref, ys_ref, dya_ref, dys_ref, dga_ref, dgs_ref):
        for rows in _strips(tm):
            g = dm_ref[rows, :].astype(F32)
            sa, ss = _sigmoid(ga_ref[rows, :].astype(F32)), _sigmoid(gs_ref[rows, :].astype(F32))
            dya_ref[rows, :] = (g * sa).astype(BF16)
            dys_ref[rows, :] = (g * ss).astype(BF16)
            dga_ref[rows, :] = (g * ya_ref[rows, :].astype(F32) * sa * (1.0 - sa)).astype(BF16)
            dgs_ref[rows, :] = (g * ys_ref[rows, :].astype(F32) * ss * (1.0 - ss)).astype(BF16)

    blk = pl.BlockSpec((tm, tc), lambda j, i: (i, j))
    return _pc(body, name=name, grid=(nd, t // tm),
               in_specs=[blk, blk, pl.BlockSpec((tm, tc), lambda j, i: (i, nd + j)), blk, blk],
               out_specs=[blk] * 4, out_shape=[_sds((t, d), BF16)] * 4,
               compiler_params=_params())(dm, gates, gates, ya, ys)


def _ssd_chunk_terms(dtr, dtb, alog):
    xx = dtr + dtb
    dt = jnp.maximum(xx, 0.0) + jnp.log(1.0 + jnp.exp(-jnp.abs(xx)))
    a = -jnp.exp(alog)
    li = lax.broadcasted_iota(jnp.int32, (CHUNK, CHUNK), 0)
    si = lax.broadcasted_iota(jnp.int32, (CHUNK, CHUNK), 1)
    causal = li >= si
    acum = _dot(causal.astype(F32), dt * a, 1, 0, HIGHEST)
    return xx, dt, a, acum, acum.T, causal


def _split2(x):
    hi = x.astype(BF16)
    return hi, (x - hi.astype(F32)).astype(BF16)


def _expand(v, e, exact=True):
    hi, lo = _split2(v)
    out = _dot(hi, e, 1, 0)
    return out + _dot(lo, e, 1, 0) if exact else out


def _segsum(s, e):
    hi, lo = _split2(s)
    return _dot(hi, e, 1, 1) + _dot(lo, e, 1, 1)


def _head_maps(di):
    nh = di // HEAD_DIM
    h = jnp.arange(DT_LANES)[:, None]
    e64 = (jnp.arange(di)[None, :] // HEAD_DIM == h).astype(BF16)
    e128 = (jnp.arange(nh * CHUNK)[None, :] // CHUNK == h).astype(BF16)
    return e64, e128


def _pair_blockdiag(p, left):
    zero = jnp.zeros_like(p)
    return jnp.concatenate([jnp.where(left, p, zero), jnp.where(left, zero, p)], axis=0)


def _ssd_fwd(xc, dtr, dtb, alog, dskx, di, name):
    t = xc.shape[0]
    dx = xc.shape[1]
    nc = t // CHUNK
    nh = di // HEAD_DIM
    hpg = nh // N_GROUPS
    gw = hpg * HEAD_DIM
    boff, coff = di, di + N_GROUPS * D_STATE
    e64, e128 = _head_maps(di)

    def body(xc_ref, dtr_ref, dtb_ref, alog_ref, dsk_ref, e64_ref, e128_ref, y_ref, st_ref, state):
        @pl.when(pl.program_id(0) == 0)
        def _():
            state[...] = jnp.zeros_like(state)
        _, dt, _, acum, acum_t, causal = _ssd_chunk_terms(dtr_ref[...], dtb_ref[...], alog_ref[...])
        last = acum[CHUNK - 1:CHUNK, :]
        e64v = e64_ref[...]
        dt_t = dt.T
        eax = _expand(jnp.exp(acum), e64v)
        dex = _expand(dt * jnp.exp(last - acum), e64v, False)
        acx = _expand(acum, e128_ref[...])
        st_ref[0] = state[...]
        left = lax.broadcasted_iota(jnp.int32, (CHUNK, 2 * HEAD_DIM), 1) < HEAD_DIM
        for g in range(N_GROUPS):
            gs = slice(g * gw, (g + 1) * gw)
            bg = xc_ref[:, boff + g * D_STATE:boff + (g + 1) * D_STATE]
            cg = xc_ref[:, coff + g * D_STATE:coff + (g + 1) * D_STATE]
            gm = _dot(cg, bg, 1, 1)
            xb = xc_ref[:, gs]
            xg = xb.astype(F32)
            sin = state[:, gs]
            yo = _dot(cg, sin.astype(BF16), 1, 0) * eax[:, gs]
            for jp in range(hpg // 2):
                h0 = g * hpg + 2 * jp
                ps = slice(jp * 2 * HEAD_DIM, (jp + 1) * 2 * HEAD_DIM)
                ms = []
                for hh in (h0, h0 + 1):
                    seg = acx[:, hh * CHUNK:(hh + 1) * CHUNK] - acum_t[hh:hh + 1, :]
                    ms.append((gm * jnp.exp(jnp.where(causal, seg, -1e30)) * dt_t[hh:hh + 1, :]).astype(BF16))
                yd = _dot(jnp.concatenate(ms, axis=1), _pair_blockdiag(xb[:, ps], left), 1, 0)
                col = slice(g * gw + jp * 2 * HEAD_DIM, g * gw + (jp + 1) * 2 * HEAD_DIM)
                y_ref[:, col] = (yd + yo[:, ps] + dsk_ref[:, col] * xg[:, ps]).astype(y_ref.dtype)
            xe = (xg * dex[:, gs]).astype(BF16)
            state[:, gs] = eax[CHUNK - 1:CHUNK, gs] * sin + _dot(bg, xe, 0, 0)

    small = pl.BlockSpec((1, DT_LANES), lambda c: (0, 0))
    whole = lambda a: pl.BlockSpec(a.shape, lambda c: (0, 0))
    return _pc(body, name=name, grid=(nc,),
               in_specs=[pl.BlockSpec((CHUNK, dx), lambda c: (c, 0)),
                         pl.BlockSpec((CHUNK, DT_LANES), lambda c: (c, 0)), small, small,
                         whole(dskx), whole(e64), whole(e128)],
               out_specs=[pl.BlockSpec((CHUNK, di), lambda c: (c, 0)),
                          pl.BlockSpec((1, D_STATE, di), lambda c: (c, 0, 0))],
               out_shape=[_sds((t, di), BF16), _sds((nc, D_STATE, di), F32)],
               scratch_shapes=[pltpu.VMEM((D_STATE, di), F32)],
               compiler_params=_params())(xc, dtr, dtb, alog, dskx, e64, e128)


def _ssd_bwd(xc, dtr, dy, states, dtb, alog, dskx, di, name):
    t = xc.shape[0]
    dx = xc.shape[1]
    nc = t // CHUNK
    nh = di // HEAD_DIM
    hpg = nh // N_GROUPS
    gw = hpg * HEAD_DIM
    boff, coff = di, di + N_GROUPS * D_STATE
    e64, e128 = _head_maps(di)

    def body(xc_ref, dtr_ref, dy_ref, st_ref, dtb_ref, alog_ref, dsk_ref, e64_ref, e128_ref,
             dxc_ref, ddtr_ref, sm_ref, dstate, darow):
        @pl.when(pl.program_id(0) == 0)
        def _():
            dstate[...] = jnp.zeros_like(dstate)
            sm_ref[...] = jnp.zeros_like(sm_ref)
        darow[...] = jnp.zeros_like(darow)
        xx, dt, a, acum, acum_t, causal = _ssd_chunk_terms(dtr_ref[...], dtb_ref[...], alog_ref[...])
        last = acum[CHUNK - 1:CHUNK, :]
        e64v = e64_ref[...]
        dtx = _expand(dt, e64v, False)
        eax = _expand(jnp.exp(acum), e64v)
        eex = _expand(jnp.exp(last - acum), e64v, False)
        acx = _expand(acum, e128_ref[...])
        left = lax.broadcasted_iota(jnp.int32, (CHUNK, 2 * HEAD_DIM), 1) < HEAD_DIM
        lane = lax.broadcasted_iota(jnp.int32, (CHUNK, DT_LANES), 1)
        sub8 = lax.broadcasted_iota(jnp.int32, (8, gw), 0)
        da_col = jnp.zeros((CHUNK, DT_LANES), F32)
        ddt_col = jnp.zeros((CHUNK, DT_LANES), F32)
        rows = jnp.zeros((8, DT_LANES), F32)
        for g in range(N_GROUPS):
            gs = slice(g * gw, (g + 1) * gw)
            bg = xc_ref[:, boff + g * D_STATE:boff + (g + 1) * D_STATE]
            cg = xc_ref[:, coff + g * D_STATE:coff + (g + 1) * D_STATE]
            gm = _dot(cg, bg, 1, 1)
            e64g = e64v[:, gs]
            xg = xc_ref[:, gs].astype(F32)
            dtg, eag, eeg = dtx[:, gs], eax[:, gs], eex[:, gs]
            xd = xg * dtg
            xdb = xd.astype(BF16)
            dyb = dy_ref[:, gs]
            dyf = dyb.astype(F32)
            sin = st_ref[0, :, gs]
            sinb = sin.astype(BF16)
            ds = dstate[:, gs]
            dsb = ds.astype(BF16)
            bds = _dot(bg, dsb, 1, 0)
            dyeb = (dyf * eag).astype(BF16)
            dcg = _dot(dyeb, sinb, 1, 1)
            dstate[:, gs] = eag[CHUNK - 1:CHUNK, :] * ds + _dot(cg, dyeb, 0, 0)
            yo = _dot(cg, sinb, 1, 0) * eag
            xe = xd * eeg
            dbg = _dot(xe.astype(BF16), dsb, 1, 1)
            wterm = bds * xe
            da_col = da_col + _segsum(dyf * yo - wterm, e64g)
            dg = jnp.zeros((CHUNK, CHUNK), F32)
            dxd_parts = []
            for jp in range(hpg // 2):
                h0 = g * hpg + 2 * jp
                ps = slice(jp * 2 * HEAD_DIM, (jp + 1) * 2 * HEAD_DIM)
                lms, mfs = [], []
                for hh in (h0, h0 + 1):
                    seg = acx[:, hh * CHUNK:(hh + 1) * CHUNK] - acum_t[hh:hh + 1, :]
                    lm = jnp.exp(jnp.where(causal, seg, -1e30))
                    lms.append(lm)
                    mfs.append(gm * lm)
                mstack = jnp.concatenate([m.astype(BF16) for m in mfs], axis=0)
                dyp = dyb[:, ps]
                dxd_parts.append(_dot(mstack, _pair_blockdiag(dyp, left), 0, 0))
                dm2 = _dot(dyp, _pair_blockdiag(xdb[:, ps], left), 1, 1)
                for k, hh in enumerate((h0, h0 + 1)):
                    dm = dm2[:, k * CHUNK:(k + 1) * CHUNK]
                    dg = dg + dm * lms[k]
                    q = dm * mfs[k]
                    da_col = da_col + jnp.where(lane == hh, jnp.sum(q, axis=1, keepdims=True), 0.0)
                    darow[hh:hh + 1, :] = -jnp.sum(q, axis=0, keepdims=True)
            dxd = jnp.concatenate(dxd_parts, axis=1) + bds * eeg
            ddt_col = ddt_col + _segsum(dxd * xg, e64g)
            rsum = (jnp.where(sub8 == 0, jnp.sum(wterm, axis=0, keepdims=True), 0.0)
                    + jnp.where(sub8 == 1, jnp.sum(ds * sin, axis=0, keepdims=True), 0.0)
                    + jnp.where(sub8 == 2, jnp.sum(dyf * xg, axis=0, keepdims=True), 0.0))
            rows = rows + _segsum(rsum, e64g)
            dxc_ref[:, gs] = (dxd * dtg + dsk_ref[:, gs] * dyf).astype(dxc_ref.dtype)
            dgb = dg.astype(BF16)
            dxc_ref[:, boff + g * D_STATE:boff + (g + 1) * D_STATE] = (
                dbg + _dot(dgb, cg, 0, 0)).astype(dxc_ref.dtype)
            dxc_ref[:, coff + g * D_STATE:coff + (g + 1) * D_STATE] = (
                dcg + _dot(dgb, bg, 1, 0)).astype(dxc_ref.dtype)
        at_last = rows[0:1, :] + jnp.exp(last) * rows[1:2, :]
        is_last = lax.broadcasted_iota(jnp.int32, (CHUNK, DT_LANES), 0) == CHUNK - 1
        da = da_col + jnp.where(is_last, at_last, 0.0) + darow[...].T
        li = lax.broadcasted_iota(jnp.int32, (CHUNK, CHUNK), 0)
        si = lax.broadcasted_iota(jnp.int32, (CHUNK, CHUNK), 1)
        dla = _dot((si >= li).astype(F32), da, 1, 0, HIGHEST)
        ddtr = (ddt_col + dla * a) * _sigmoid(xx)
        ddtr_ref[...] = ddtr
        sm_ref[0:1, :] += jnp.sum(ddtr, axis=0, keepdims=True)
        sm_ref[1:2, :] += jnp.sum(dla * dt, axis=0, keepdims=True) * a
        sm_ref[2:3, :] += rows[2:3, :]

    small = pl.BlockSpec((1, DT_LANES), lambda c: (0, 0))
    whole = lambda a: pl.BlockSpec(a.shape, lambda c: (0, 0))
    rev = lambda c: (nc - 1 - c, 0)
    return _pc(body, name=name, grid=(nc,),
               in_specs=[pl.BlockSpec((CHUNK, dx), rev), pl.BlockSpec((CHUNK, DT_LANES), rev),
                         pl.BlockSpec((CHUNK, di), rev),
                         pl.BlockSpec((1, D_STATE, di), lambda c: (nc - 1 - c, 0, 0)), small, small,
                         whole(dskx), whole(e64), whole(e128)],
               out_specs=[pl.BlockSpec((CHUNK, dx), rev), pl.BlockSpec((CHUNK, DT_LANES), rev),
                          pl.BlockSpec((8, DT_LANES), lambda c: (0, 0))],
               out_shape=[_sds((t, dx), BF16), _sds((t, DT_LANES), F32), _sds((8, DT_LANES), F32)],
               scratch_shapes=[pltpu.VMEM((D_STATE, di), F32), pltpu.VMEM((DT_LANES, CHUNK), F32)],
               compiler_params=_params())(xc, dtr, dy, states, dtb, alog, dskx, e64, e128)


def _adamw(parts, w, m, v, name):
    npart, rows, width = parts.shape
    if rows * width <= SMALL_PARAM:
        tr, tw = rows, width
    else:
        tr, tw = (_pick(rows, (64, 32, 16, 8)), width) if rows % 8 == 0 else (rows, 128)
    c1 = 1.0 - ADAM_B1 ** ADAM_STEP
    c2 = 1.0 - ADAM_B2 ** ADAM_STEP

    row_strips = _strips(tr) if tr % STRIP == 0 else [slice(0, tr)]
    col_chunks = [slice(c, c + 512) for c in range(0, tw, 512)] if tw % 512 == 0 else [slice(0, tw)]

    def body(p_ref, w_ref, m_ref, v_ref, g_ref, d_ref, nm_ref, nv_ref):
        for rows in row_strips:
            for cols in col_chunks:
                g = p_ref[0, rows, cols].astype(F32)
                for p in range(1, npart):
                    g = g + p_ref[p, rows, cols].astype(F32)
                nm = ADAM_B1 * m_ref[rows, cols] + (1.0 - ADAM_B1) * g
                nv = ADAM_B2 * v_ref[rows, cols] + (1.0 - ADAM_B2) * (g * g)
                g_ref[rows, cols] = g
                nm_ref[rows, cols] = nm
                nv_ref[rows, cols] = nv
                d_ref[rows, cols] = -ADAM_LR * ((nm / c1) / (jnp.sqrt(nv / c2) + ADAM_EPS)
                                                + ADAM_WD * w_ref[rows, cols])

    blk = pl.BlockSpec((tr, tw), lambda i, j: (i, j))
    return _pc(body, name=name, grid=(rows // tr, width // tw),
               in_specs=[pl.BlockSpec((npart, tr, tw), lambda i, j: (0, i, j)), blk, blk, blk],
               out_specs=[blk] * 4, out_shape=[_sds((rows, width), F32)] * 4,
               compiler_params=_params())(parts, w, m, v)


def _sum_parts(parts, name, tile=None):
    npart, rows, width = parts.shape
    tile = rows if tile is None else tile

    def body(p_ref, o_ref):
        for rows_ in _strips(tile, 8 if parts.dtype == F32 else STRIP):
            g = p_ref[0, rows_, :].astype(F32)
            for p in range(1, npart):
                g = g + p_ref[p, rows_, :].astype(F32)
            o_ref[rows_, :] = g

    return _pc(body, name=name, grid=(rows // tile,),
               in_specs=[pl.BlockSpec((npart, tile, width), lambda i: (0, i, 0))],
               out_specs=pl.BlockSpec((tile, width), lambda i: (i, 0)),
               out_shape=_sds((rows, width), F32), compiler_params=_params())(parts)


def _flip(k):
    x, y, c = lax.axis_index("x"), lax.axis_index("y"), lax.axis_index("c")
    px = 1 - x if k & 4 else x
    py = 1 - y if k & 2 else y
    pc = 1 - c if k & 1 else c
    return (px, py, pc), 4 * px + 2 * py + pc


DIRECT = tuple((k, 0) for k in range(1, N_DEV))
TO_CHIPS = ((1, 0), (2, 0), (4, 0), (6, 0))
TO_SIBLING = ((1, 2), (1, 4), (1, 6))


def _copies(arrays, lands, send_sems, recv_sems, scatter, moves):
    _, me = _flip(0)
    outgoing, incoming = [], []
    for i, (kd, kb) in enumerate(moves):
        peer, pidx = _flip(kd)
        _, out_slot = _flip(kb)
        _, in_slot = _flip(kd ^ kb)
        for j, land_ref in enumerate(lands):
            if kb:
                src = land_ref.at[out_slot]
            else:
                src = arrays[j].at[pidx] if scatter[j] else arrays[j]
            sem = len(lands) * i + j
            for dst, bucket in ((land_ref.at[out_slot], outgoing), (land_ref.at[in_slot], incoming)):
                bucket.append(pltpu.make_async_remote_copy(
                    src_ref=src, dst_ref=dst, send_sem=send_sems.at[sem], recv_sem=recv_sems.at[sem],
                    device_id=peer, device_id_type=MESH))
    return outgoing, incoming


HBM_SPEC = pl.BlockSpec(memory_space=pltpu.HBM)
SEM_SPEC = pl.BlockSpec(memory_space=pltpu.SEMAPHORE)
ANY_SPEC = pl.BlockSpec(memory_space=pl.ANY)
EFFECT = pltpu.SideEffectType.DATAFLOW_SIDE_EFFECTING


def _landing_zones(arrays, scatter):
    _, me = _flip(0)
    lands = []
    for a, sc in zip(arrays, scatter):
        own = lax.dynamic_index_in_dim(a, me, 0, keepdims=True) if sc else a[None]
        shape = a.shape if sc else (N_DEV,) + a.shape
        lands.append(lax.dynamic_update_slice(lax.empty(shape, a.dtype), own, (me,) + (0,) * (len(shape) - 1)))
    return lands


def _xchg_start(arrays, scatter, after, name, moves=DIRECT, lands=None):
    if lands is None:
        lands = _landing_zones(arrays, scatter)
    na, nl = len(arrays), len(lands)

    def body(*refs):
        ins, outs = refs[:na + nl], refs[na + nl + 1:]
        outgoing, _ = _copies(ins[:na], ins[na:], outs[0], outs[1], scatter, moves)
        for cp in outgoing:
            cp.start()
        outs[-1][...] = jnp.zeros_like(outs[-1])

    nsem = nl * len(moves)
    operands = [pltpu.with_memory_space_constraint(a, pltpu.HBM) for a in list(arrays) + list(lands)]
    out = _pc(body, name=name,
              out_shape=(pltpu.SemaphoreType.DMA((nsem,)), pltpu.SemaphoreType.DMA((nsem,)),
                         *[pltpu.HBM(a.shape, a.dtype) for a in operands], _sds((8, 128), F32)),
              in_specs=[HBM_SPEC] * (na + nl) + [ANY_SPEC],
              out_specs=(SEM_SPEC, SEM_SPEC, *[HBM_SPEC] * (na + nl), pl.BlockSpec(memory_space=pltpu.VMEM)),
              input_output_aliases={i: 2 + i for i in range(na + nl)},
              compiler_params=pltpu.CompilerParams(has_side_effects=EFFECT))(*operands, after)
    return dict(sems=out[:2], thru=out[2:2 + na + nl], token=out[-1], scatter=scatter, na=na, moves=moves)


def _xchg_wait(handle, after, name):
    na, scatter, moves, thru = handle["na"], handle["scatter"], handle["moves"], handle["thru"]
    n = len(thru)

    def body(*refs):
        ins = refs[:n]
        outgoing, incoming = _copies(ins[:na], ins[na:], refs[n], refs[n + 1], scatter, moves)
        for cp in outgoing:
            cp.wait_send()
        for cp in incoming:
            cp.wait_recv()

    out = _pc(body, name=name, out_shape=tuple(pltpu.HBM(a.shape, a.dtype) for a in thru),
              in_specs=[HBM_SPEC] * n + [SEM_SPEC, SEM_SPEC, ANY_SPEC], out_specs=tuple([HBM_SPEC] * n),
              input_output_aliases={i: i for i in range(n)},
              compiler_params=pltpu.CompilerParams(has_side_effects=EFFECT))(*thru, *handle["sems"], after)
    return out[na:]


def _pack(arrs, width, row_mult):
    flat = jnp.concatenate([a.reshape(-1) for a in arrs])
    n = flat.shape[0]
    rows = -(-n // (width * row_mult)) * row_mult
    return jnp.pad(flat, (0, rows * width - n)).reshape(rows, width)


def _unpack(packed, shapes, lead=None):
    out, off = [], 0
    flat = packed.reshape(-1) if lead is None else packed.reshape(lead, -1)
    for s in shapes:
        n = math.prod(s)
        if lead is None:
            out.append(flat[off:off + n].reshape(s))
        else:
            out.append(flat[:, off:off + n].reshape((lead,) + tuple(s)))
        off += n
    return out


def _blocks_to_cols(blocks):
    nb, rows, n = blocks.shape
    return blocks.transpose(1, 0, 2).reshape(rows, nb * n)


def _pad_rows(a, rows):
    return jnp.pad(a, ((0, rows - a.shape[0]), (0, 0)))


def _pad_lanes(a, lanes):
    return jnp.pad(a, ((0, 0), (0, lanes - a.shape[1])))


REST = ("w_a_out", "w_s_out", "w_o", "w_up", "w_down")
TRANSPOSED = ("w_up", "w_in")
CONVS = ("conv_a_w", "ssd_conv_w", "ffn_conv_w")
REPL = ("norm_mix_w", "ssd_conv_b", "dt_bias", "a_log", "d_skip", "ssd_norm_w", "norm_ffn_w", "ffn_conv_b",
        "final_norm_w")
ORDER = ("norm_mix_w", "w_in", "conv_a_w", "w_a_out", "ssd_conv_w", "ssd_conv_b", "dt_bias", "a_log", "d_skip",
         "ssd_norm_w", "w_s_out", "w_o", "norm_ffn_w", "w_up", "ffn_conv_w", "ffn_conv_b", "w_down", "final_norm_w")


def _as_rows(name, block):
    return block[0].T if name in TRANSPOSED else block[0]


def kernel(x, norm_mix_w, w_in, conv_a_w, w_a_out, ssd_conv_w, ssd_conv_b, dt_bias, a_log, d_skip, ssd_norm_w, w_s_out, w_o, norm_ffn_w, w_up, ffn_conv_w, ffn_conv_b, w_down, final_norm_w, loss_target, m_norm_mix_w, m_w_in, m_conv_a_w, m_w_a_out, m_ssd_conv_w, m_ssd_conv_b, m_dt_bias, m_a_log, m_d_skip, m_ssd_norm_w, m_w_s_out, m_w_o, m_norm_ffn_w, m_w_up, m_ffn_conv_w, m_ffn_conv_b, m_w_down, m_final_norm_w, v_norm_mix_w, v_w_in, v_conv_a_w, v_w_a_out, v_ssd_conv_w, v_ssd_conv_b, v_dt_bias, v_a_log, v_d_skip, v_ssd_norm_w, v_w_s_out, v_w_o, v_norm_ffn_w, v_w_up, v_ffn_conv_w, v_ffn_conv_b, v_w_down, v_final_norm_w):
    wts = dict(norm_mix_w=norm_mix_w, w_in=w_in, conv_a_w=conv_a_w, w_a_out=w_a_out, ssd_conv_w=ssd_conv_w,
               ssd_conv_b=ssd_conv_b, dt_bias=dt_bias, a_log=a_log, d_skip=d_skip, ssd_norm_w=ssd_norm_w,
               w_s_out=w_s_out, w_o=w_o, norm_ffn_w=norm_ffn_w, w_up=w_up, ffn_conv_w=ffn_conv_w,
               ffn_conv_b=ffn_conv_b, w_down=w_down, final_norm_w=final_norm_w)
    mom1 = dict(norm_mix_w=m_norm_mix_w, w_in=m_w_in, conv_a_w=m_conv_a_w, w_a_out=m_w_a_out,
                ssd_conv_w=m_ssd_conv_w, ssd_conv_b=m_ssd_conv_b, dt_bias=m_dt_bias, a_log=m_a_log, d_skip=m_d_skip,
                ssd_norm_w=m_ssd_norm_w, w_s_out=m_w_s_out, w_o=m_w_o, norm_ffn_w=m_norm_ffn_w, w_up=m_w_up,
                ffn_conv_w=m_ffn_conv_w, ffn_conv_b=m_ffn_conv_b, w_down=m_w_down, final_norm_w=m_final_norm_w)
    mom2 = dict(norm_mix_w=v_norm_mix_w, w_in=v_w_in, conv_a_w=v_conv_a_w, w_a_out=v_w_a_out,
                ssd_conv_w=v_ssd_conv_w, ssd_conv_b=v_ssd_conv_b, dt_bias=v_dt_bias, a_log=v_a_log, d_skip=v_d_skip,
                ssd_norm_w=v_ssd_norm_w, w_s_out=v_w_s_out, w_o=v_w_o, norm_ffn_w=v_norm_ffn_w, w_up=v_w_up,
                ffn_conv_w=v_ffn_conv_w, ffn_conv_b=v_ffn_conv_b, w_down=v_w_down, final_norm_w=v_final_norm_w)

    t, d = x.shape[1], x.shape[2]
    di = 2 * d
    nh = di // HEAD_DIM
    dxw = di + 2 * N_GROUPS * D_STATE
    f = w_down.shape[1] * N_DEV
    n_in = w_in.shape[2] * N_DEV
    me = 4 * lax.axis_index("x") + 2 * lax.axis_index("y") + lax.axis_index("c")

    rest_local = [_as_rows(k, wts[k]).astype(BF16) for k in REST]
    nrows = [a.shape[0] for a in rest_local]
    n_blk = w_in.shape[2]
    in_local = w_in[0].T.astype(BF16)
    conv_shapes = [wts[k].shape[1:] for k in CONVS]
    conv_local = _pack([wts[k] for k in CONVS], d, 8)
    x2, tgt = x[0], loss_target[0]
    h_in = _xchg_start([in_local, conv_local], [False, False], x2, "gather_in_start", moves=TO_CHIPS)
    u = _rms_fwd(x2, norm_mix_w, h_in["token"], "norm_mix")
    part = _xchg_wait(h_in, u, "gather_in_wait")
    h_fwd = _xchg_start([], [False, False], u, "gather_in_forward_start", moves=TO_SIBLING, lands=part)
    in_all, conv_all = _xchg_wait(h_fwd, u, "gather_in_forward_wait")
    win_t = in_all.reshape(n_in, d)
    h_rest = _xchg_start(rest_local, [False] * len(REST), in_all, "gather_rest_start")
    c_a, c_s, c_f = _unpack(conv_all, conv_shapes, N_DEV)
    caw, scw, fcw = _blocks_to_cols(c_a), _blocks_to_cols(c_s), _blocks_to_cols(c_f)

    o_z, o_x, o_dt = 5 * d, 7 * d, 7 * d + dxw
    seg_bounds = [0, d, 2 * d, 3 * d, 4 * d, o_z, o_x, o_dt]
    w_dt = _pad_rows(win_t[o_dt:], DT_LANES)
    dtb, alog = (_pad_lanes(p[...].reshape(1, nh), DT_LANES) for p in (dt_bias, a_log))
    dskx = jnp.repeat(d_skip.reshape(1, nh), HEAD_DIM, axis=1)

    tok = h_rest["token"]
    gates = _mm([(u, 0, d, win_t, 0)], "nt", BF16, "proj_gates", n=2 * d, tm=2048, after=tok)
    pa = _mm([(u, 0, d, win_t, 2 * d)], "nt", BF16, "proj_a", n=3 * d, tm=2048, after=tok)
    z = _mm([(u, 0, d, win_t, o_z)], "nt", BF16, "proj_z", n=2 * d, tm=2048, after=tok)
    xbc = _mm([(u, 0, d, win_t, o_x)], "nt", BF16, "proj_xbc", n=dxw, tm=2048, after=tok)
    dtr = _mm([(u, w_dt)], "nt", F32, "proj_dt", after=tok)
    ya_in, q_a = _conv_a_fwd(pa, caw, d, "conv_a")
    xc, pre_s = _conv_s_fwd(xbc, scw, ssd_conv_b, "conv_s")
    y, states = _ssd_fwd(xc, dtr, dtb, alog, dskx, di, "ssd")
    yn = _gnorm_fwd(y, z, ssd_norm_w, "gnorm")
    rest_all = _xchg_wait(h_rest, yn, "gather_rest_wait")
    waout, wsout, wo, wup_t, wdown = (a.reshape(N_DEV * n, d) for a, n in zip(rest_all, nrows))
    y_a = _mm([(ya_in, waout)], "nn", BF16, "a_out", tm=2048)
    y_s = _mm([(yn, wsout)], "nn", BF16, "s_out", tm=2048)
    merged = _merge_fwd(gates, y_a, y_s, d, "merge")
    mo = _mm([(merged, wo)], "nn", BF16, "o_proj", tm=2048)
    h1, v = _resnorm_fwd(x2, mo, norm_ffn_w, "norm_ffn")
    hv = _mm([(v, wup_t)], "nt", BF16, "up_proj", tm=512, resident_b=True)
    act, c1 = _ffn_fwd(hv, fcw, ffn_conv_b, f, "ffn_act")
    dd = _mm([(act, wdown)], "nn", BF16, "down_proj")
    loss11, dh2, dh2b, g_fnw = _final(h1, dd, tgt, final_norm_w.reshape(1, d), "final")

    dact = _mm([(dh2b, wdown)], "nt", BF16, "d_act", resident_b=True)
    gw_down = _mm_tn(act, dh2b, "gw_down")
    dh1f, dh3, g_ffn = _ffn_bwd(hv, c1, dact, fcw, f, "ffn_act_bwd")
    dv = _mm([(dh1f, 0, f, wup_t, 0), (dh3, 0, f, wup_t, f)], "nn", BF16, "d_v")
    gw_up_t = jnp.concatenate([_mm_tn(dh1f, v, "gw_up1"), _mm_tn(dh3, v, "gw_up3")], axis=0)
    dh1, dh1b, g_nfw = _rms_bwd(h1, dv, norm_ffn_w, dh2, "norm_ffn_bwd")
    dmerged = _mm([(dh1b, wo)], "nt", BF16, "d_merged", tm=2048)
    gw_o = _mm_tn(merged, dh1b, "gw_o")
    dya, dys, dga, dgs = _merge_bwd(dmerged, gates, y_a, y_s, d, "merge_bwd")
    dyain = _mm([(dya, waout)], "nt", BF16, "d_ya_in", tm=2048)
    gw_aout = _mm_tn(ya_in, dya, "gw_a_out")
    db, dc, dvv, g_caw = _conv_a_bwd(pa, q_a, dyain, caw, d, "conv_a_bwd")
    dyn = _mm([(dys, wsout)], "nt", BF16, "d_yn", resident_b=True)
    gw_sout = _mm_tn(yn, dys, "gw_s_out")
    grads_rest = dict(w_a_out=gw_aout, w_s_out=gw_sout, w_o=gw_o, w_up=gw_up_t, w_down=gw_down)
    rest_parts = [grads_rest[k].reshape(N_DEV, n, d) for k, n in zip(REST, nrows)]
    h_grest = _xchg_start(rest_parts, [True] * len(REST), rest_parts[0], "scatter_rest_start")
    dy, dz, g_snw = _gnorm_bwd(y, z, dyn, ssd_norm_w, "gnorm_bwd")
    dtb_after = dtb + h_grest["token"][0:1, 0:1]
    dxc, ddtr, g_ssd = _ssd_bwd(xc, dtr, dy, states, dtb_after, alog, dskx, di, "ssd_bwd")
    dxbc, g_scw = _conv_s_bwd(xbc, pre_s, dxc, scw, "conv_s_bwd")
    dsegs = [dga, dgs, db, dc, dvv, dz, dxbc, ddtr.astype(BF16)]
    pairs = [(s, c, d, win_t, a + c * d) for s, a in zip(dsegs[:-1], seg_bounds) for c in range(s.shape[1] // d)]
    pairs.append((dsegs[-1], w_dt))
    gw_in = [_mm_tn(s, u, "gw_in%d" % i) for i, s in enumerate(dsegs)]
    gw_in_t = jnp.concatenate(gw_in[:-1] + [gw_in[-1][:nh]], axis=0)
    in_parts = gw_in_t.reshape(N_DEV, n_blk, d)
    h_gin = _xchg_start([in_parts], [True], in_parts, "scatter_in_start")
    du = _mm(pairs, "nn", BF16, "d_u", tm=512, tn=1024, after=h_gin["token"], resident_b=True)
    dx, _, g_nmw = _rms_bwd(x2, du, norm_mix_w, dh1, "norm_mix_bwd")

    small_grads = dict(norm_mix_w=g_nmw[0], ssd_conv_b=g_scw[4], dt_bias=g_ssd[0, :nh], a_log=g_ssd[1, :nh],
                       d_skip=g_ssd[2, :nh], ssd_norm_w=g_snw[0], norm_ffn_w=g_nfw[0], ffn_conv_b=g_ffn[3],
                       final_norm_w=g_fnw[0], conv_a_w=g_caw[:3], ssd_conv_w=g_scw[:4], ffn_conv_w=g_ffn[:3])
    small_names = REPL + CONVS
    small_parts = _pack([small_grads[k] for k in small_names] + [loss11], d, 8)
    h_small = _xchg_start([small_parts], [False], small_parts, "gather_small_start")
    rest_recv = _xchg_wait(h_grest, dx, "scatter_rest_wait")
    (in_recv,) = _xchg_wait(h_gin, rest_recv[0], "scatter_in_wait")
    (small_all,) = _xchg_wait(h_small, in_recv, "gather_small_wait")
    small_sum = _sum_parts(small_all, "sum_small_grads")
    *small_list, loss = _unpack(small_sum, [small_grads[k].shape for k in small_names] + [()])
    small_g = dict(zip(small_names, small_list))

    res = {}

    def update(k, parts):
        outs = _adamw(parts, *(_as_rows(k, src[k]) for src in (wts, mom1, mom2)), "adamw_" + k)
        for kind, a in zip(("g", "d", "m", "v"), outs):
            res[kind, k] = (a.T if k in TRANSPOSED else a)[None]

    update("w_in", in_recv)
    for k, parts in zip(REST, rest_recv):
        update(k, parts)
    local_g = {}
    for k in REPL:
        local_g[k] = small_g[k].reshape(wts[k].shape)
    for k in CONVS:
        n = wts[k].shape[2]
        local_g[k] = lax.dynamic_slice_in_dim(small_g[k], me * n, n, axis=1)[None]
    for k in small_names:
        as2d = lambda a: a.reshape(-1, a.shape[-1])
        outs = _adamw(as2d(local_g[k])[None], *(as2d(src[k]) for src in (wts, mom1, mom2)), "adamw_" + k)
        for kind, a in zip(("g", "d", "m", "v"), outs):
            res[kind, k] = a.reshape(wts[k].shape)

    return (loss, dx[None], *[res["g", k] for k in ORDER], *[res["d", k] for k in ORDER],
            *[res["m", k] for k in ORDER], *[res["v", k] for k in ORDER])
```
